```python
import jax, jax.numpy as jnp
from jax import lax
import numpy as np

D_MODEL = 1024
BATCH = 16
SEQ = 2048
DEPTH = 1

GRID_W = 64
CTX_LEN = 256
D_MIX = D_MODEL
GLA_WIDTH = D_MIX // 2
POOL_WIDTH = D_MIX - GLA_WIDTH
GLA_HEADS = 4
GLA_DV = GLA_WIDTH // GLA_HEADS
GLA_DK = GLA_DV // 2
GLA_QK = GLA_HEADS * GLA_DK
GATE_RANK = 16
GATE_NORMALIZER = 16.0
GLA_CHUNK = 64
POOL_GROUPS = 4
POOL_GC = POOL_WIDTH // POOL_GROUPS
POOL_WINDOWS = (2, 4, 8, 16)
N_EXPERTS = 32
TOP_K = 4
D_FF = D_MODEL
SWIGLU_LIMIT = 7.0
SWIGLU_ALPHA = 1.702
EXPERT_BLOCK = 128
N_MOD = 6
EPS = 1e-6
D_IN = 2 * GLA_QK + 2 * GLA_WIDTH + 2 * GATE_RANK + POOL_WIDTH
SPLIT_POINTS = (GLA_QK, 2 * GLA_QK, 2 * GLA_QK + GLA_WIDTH, 2 * GLA_QK + 2 * GLA_WIDTH,
                2 * GLA_QK + 2 * GLA_WIDTH + GATE_RANK, 2 * GLA_QK + 2 * GLA_WIDTH + 2 * GATE_RANK)

kernel_name = "hybrid_gla_pool_moe_diffusion_layer"


def rmsnorm(x, w):
    xf = x.astype(jnp.float32)
    var = jnp.mean(xf * xf, axis=-1, keepdims=True)
    return (xf * lax.rsqrt(var + EPS) * w.astype(jnp.float32)).astype(x.dtype)


def modulate(h, shift, scale):
    return h * (1.0 + scale) + shift


def to_heads(t, d):
    b_, n, _ = t.shape
    return t.reshape(b_, n, GLA_HEADS, d).transpose(0, 2, 1, 3).astype(jnp.float32)


def mixer_inputs(p, w_gk_f, b_gk_f, w_gk_b, b_gk_b):
    q, k, v, g, r_f, r_b, pool_in = jnp.split(p, SPLIT_POINTS, axis=-1)
    gk_f = jax.nn.log_sigmoid((r_f @ w_gk_f + b_gk_f).astype(jnp.float32)) / GATE_NORMALIZER
    gk_b = jax.nn.log_sigmoid((r_b @ w_gk_b + b_gk_b).astype(jnp.float32)) / GATE_NORMALIZER
    return (to_heads(q, GLA_DK), to_heads(k, GLA_DK), to_heads(v, GLA_DV),
            to_heads(gk_f, GLA_DK), to_heads(gk_b, GLA_DK), g, pool_in)


def gla_chunked(q, k, v, gk, s0, with_output=True):
    b_, h_, t_, dk = q.shape
    dv = v.shape[-1]
    n_ch = t_ // GLA_CHUNK
    qc = q.reshape(b_, h_, n_ch, GLA_CHUNK, dk) * (dk ** -0.5)
    kc = k.reshape(b_, h_, n_ch, GLA_CHUNK, dk)
    vc = v.reshape(b_, h_, n_ch, GLA_CHUNK, dv)
    bcum = jnp.cumsum(gk.reshape(b_, h_, n_ch, GLA_CHUNK, dk), axis=3)
    b_last = bcum[:, :, :, -1:, :]
    u = jnp.einsum('bhnsk,bhnsv->bhnkv', kc * jnp.exp(b_last - bcum), vc)
    decay = jnp.exp(b_last[:, :, :, 0, :])

    def step(s, inp):
        d, un = inp
        return d[..., None] * s + un, s

    s_final, s_prev = lax.scan(step, s0, (jnp.moveaxis(decay, 2, 0), jnp.moveaxis(u, 2, 0)))
    if not with_output:
        return None, s_final
    s_prev = jnp.moveaxis(s_prev, 0, 2)
    b_mid = bcum[:, :, :, GLA_CHUNK // 2 - 1:GLA_CHUNK // 2, :]
    a = jnp.einsum('bhntk,bhnsk->bhnts', qc * jnp.exp(bcum - b_mid), kc * jnp.exp(b_mid - bcum))
    mask = jnp.tril(jnp.ones((GLA_CHUNK, GLA_CHUNK), dtype=bool))
    a = jnp.where(mask, a, 0.0)
    o = (jnp.einsum('bhnts,bhnsv->bhntv', a, vc)
         + jnp.einsum('bhntk,bhnkv->bhntv', qc * jnp.exp(bcum), s_prev))
    return o.reshape(b_, h_, t_, dv), s_final


def gla_bidir(q, k, v, gk_f, gk_b, s0_f, s0_b, with_output=True):
    flip = lambda t: jnp.flip(t, axis=2)
    o_f, s_f = gla_chunked(q, k, v, gk_f, s0_f, with_output)
    o_b, s_b = gla_chunked(flip(q), flip(k), flip(v), flip(gk_b), s0_b, with_output)
    o = o_f + flip(o_b) if with_output else None
    return o, s_f, s_b


def gla_output(o, g, w_norm):
    b_, _, n, _ = o.shape
    o = o.transpose(0, 2, 1, 3)
    o = o * lax.rsqrt(jnp.mean(o * o, axis=-1, keepdims=True) + EPS) * w_norm.astype(jnp.float32)
    gate = jax.nn.silu(g.astype(jnp.float32)).reshape(o.shape)
    return (o * gate).reshape(b_, n, GLA_WIDTH)


def window_bounds(pos, w, length):
    lo = w // 2
    hi = w - 1 - lo
    return jnp.clip(pos - lo, 0, length), jnp.clip(pos + hi + 1, 0, length)


def pool_grid(xp, rows):
    b_, n, _ = xp.shape
    xf = xp.astype(jnp.float32).reshape(b_, rows, GRID_W, POOL_GROUPS, POOL_GC)
    sat = jnp.pad(jnp.cumsum(jnp.cumsum(xf, axis=1), axis=2), ((0, 0), (1, 0), (1, 0), (0, 0), (0, 0)))
    r = jnp.arange(rows)
    col = jnp.arange(GRID_W)
    outs = []
    for gi, w in enumerate(POOL_WINDOWS):
        r0, r1 = window_bounds(r, w, rows)
        c0, c1 = window_bounds(col, w, GRID_W)
        s = sat[:, :, :, gi]
        s_r1, s_r0 = s[:, r1], s[:, r0]
        tot = s_r1[:, :, c1] - s_r0[:, :, c1] - s_r1[:, :, c0] + s_r0[:, :, c0]
        cnt = ((r1 - r0)[:, None] * (c1 - c0)[None, :]).astype(jnp.float32)
        outs.append(tot / cnt[None, :, :, None])
    mean = jnp.stack(outs, axis=3)
    return (mean - xf).reshape(b_, n, POOL_GROUPS, POOL_GC)


def pool_seq(xp):
    b_, n, _ = xp.shape
    xf = xp.astype(jnp.float32).reshape(b_, n, POOL_GROUPS, POOL_GC)
    cs = jnp.pad(jnp.cumsum(xf, axis=1), ((0, 0), (1, 0), (0, 0), (0, 0)))
    pos = jnp.arange(n)
    outs = []
    for gi, w in enumerate(POOL_WINDOWS):
        i0, i1 = window_bounds(pos, w, n)
        outs.append((cs[:, i1, gi] - cs[:, i0, gi]) / (i1 - i0).astype(jnp.float32)[None, :, None])
    return jnp.stack(outs, axis=2) - xf


def pool_project(pooled, w_pool, pool_scale):
    b_, n = pooled.shape[:2]
    y = jnp.einsum('bngc,gcd->bngd', pooled, w_pool.astype(jnp.float32)).reshape(b_, n, POOL_WIDTH)
    return y * pool_scale.astype(jnp.float32)


def moe(h, w_router, b_router, w_gu, b_gu, w_down, b_down):
    shp = h.shape
    xt = h.reshape(-1, D_MODEL)
    n_tok = xt.shape[0]
    n_assign = n_tok * TOP_K
    logits = (xt @ w_router + b_router).astype(jnp.float32)
    top_vals, top_idx = lax.top_k(logits, TOP_K)
    weights = jax.nn.softmax(top_vals, axis=-1)
    flat_e = top_idx.reshape(-1)
    flat_tok = jnp.repeat(jnp.arange(n_tok, dtype=jnp.int32), TOP_K)
    flat_w = weights.reshape(-1)
    order = jnp.argsort(flat_e)
    sorted_e = flat_e[order]
    counts = jnp.bincount(flat_e, length=N_EXPERTS)
    padded = (counts + EXPERT_BLOCK - 1) // EXPERT_BLOCK * EXPERT_BLOCK
    start_unpad = jnp.cumsum(counts) - counts
    pad_end = jnp.cumsum(padded)
    start_pad = pad_end - padded
    dest = start_pad[sorted_e] + (jnp.arange(n_assign) - start_unpad[sorted_e])
    n_pad = n_assign + N_EXPERTS * EXPERT_BLOCK
    n_blocks = n_pad // EXPERT_BLOCK
    buf_tok = jnp.full((n_pad,), n_tok, dtype=jnp.int32).at[dest].set(flat_tok[order])
    buf_w = jnp.zeros((n_pad,), jnp.float32).at[dest].set(flat_w[order])
    block_e = jnp.minimum(jnp.searchsorted(pad_end, jnp.arange(n_blocks) * EXPERT_BLOCK, side='right'),
                          N_EXPERTS - 1)
    x_pad = jnp.concatenate([xt, jnp.zeros((1, D_MODEL), xt.dtype)], axis=0)
    xb = x_pad[buf_tok].reshape(n_blocks, EXPERT_BLOCK, D_MODEL)

    def expert_block(args):
        xblk, e = args
        gu = xblk @ w_gu[e] + b_gu[e]
        gate, up = jnp.split(gu, 2, axis=-1)
        gate = jnp.minimum(gate, SWIGLU_LIMIT)
        up = jnp.clip(up, -SWIGLU_LIMIT, SWIGLU_LIMIT)
        act = (up + 1.0) * gate * jax.nn.sigmoid(SWIGLU_ALPHA * gate)
        return act @ w_down[e] + b_down[e]

    yb = lax.map(expert_block, (xb, block_e)).reshape(n_pad, D_MODEL)
    y = jnp.zeros((n_tok + 1, D_MODEL), yb.dtype).at[buf_tok].add(yb * buf_w[:, None].astype(yb.dtype))
    return y[:n_tok].reshape(shp)


def setup_inputs(seed: int = 0) -> dict:
    key = jax.random.key(seed)
    ks = jax.random.split(key, 32)
    nrm = lambda k, shp, s: jax.random.normal(k, shp, jnp.float32) * s
    L, D, E = DEPTH, D_MODEL, N_EXPERTS
    return {
        "x": nrm(ks[0], (BATCH, SEQ, D), 1.0),
        "c": nrm(ks[1], (BATCH, D), 1.0),
        "ctx": nrm(ks[2], (BATCH, CTX_LEN, D), 1.0),
        "c_ctx": nrm(ks[3], (D,), 1.0),
        "w_ada": nrm(ks[4], (L, D, N_MOD * D), 0.5 * D ** -0.5),
        "b_ada": nrm(ks[5], (L, N_MOD * D), 0.02),
        "norm_mix_w": 1.0 + nrm(ks[6], (L, D), 0.02),
        "norm_mlp_w": 1.0 + nrm(ks[7], (L, D), 0.02),
        "w_in": nrm(ks[8], (L, D, D_IN), D ** -0.5),
        "w_gk_f": nrm(ks[9], (L, GATE_RANK, GLA_QK), GATE_RANK ** -0.5),
        "b_gk_f": jax.random.uniform(ks[10], (L, GLA_QK), jnp.float32, -2.0, 4.0),
        "w_gk_b": nrm(ks[11], (L, GATE_RANK, GLA_QK), GATE_RANK ** -0.5),
        "b_gk_b": jax.random.uniform(ks[12], (L, GLA_QK), jnp.float32, -2.0, 4.0),
        "gla_norm_w": 1.0 + nrm(ks[13], (L, GLA_DV), 0.02),
        "w_pool": nrm(ks[14], (L, POOL_GROUPS, POOL_GC, POOL_GC), POOL_GC ** -0.5),
        "pool_scale": 1.0 + nrm(ks[15], (L, POOL_WIDTH), 0.1),
        "w_out": nrm(ks[16], (L, D_MIX, D), D_MIX ** -0.5),
        "w_router": nrm(ks[17], (L, D, E), D ** -0.5),
        "b_router": nrm(ks[18], (L, E), 0.01),
        "w_gu": nrm(ks[19], (L, E, D, 2 * D_FF), D ** -0.5),
        "b_gu": nrm(ks[20], (L, E, 2 * D_FF), 0.02),
        "w_down": nrm(ks[21], (L, E, D_FF, D), D_FF ** -0.5),
        "b_down": nrm(ks[22], (L, E, D), 0.02),
        "final_norm_w": 1.0 + nrm(ks[23], (D,), 0.02),
    }


def reference(x, c, ctx, c_ctx, w_ada, b_ada, norm_mix_w, norm_mlp_w, w_in, w_gk_f, b_gk_f, w_gk_b,
              b_gk_b, gla_norm_w, w_pool, pool_scale, w_out, w_router, b_router, w_gu, b_gu, w_down,
              b_down, final_norm_w):
    h, hc = x, ctx
    b_, n_lat, _ = x.shape
    rows = n_lat // GRID_W
    for l in range(DEPTH):
        last = l == DEPTH - 1
        mod_x = (jax.nn.silu(c) @ w_ada[l] + b_ada[l])[:, None, :]
        mod_c = jax.nn.silu(c_ctx) @ w_ada[l] + b_ada[l]
        sh1, sc1, g1, sh2, sc2, g2 = jnp.split(mod_x, N_MOD, axis=-1)
        csh1, csc1, cg1, csh2, csc2, cg2 = jnp.split(mod_c, N_MOD, axis=-1)

        p_x = modulate(rmsnorm(h, norm_mix_w[l]), sh1, sc1) @ w_in[l]
        p_c = modulate(rmsnorm(hc, norm_mix_w[l]), csh1, csc1) @ w_in[l]
        qx, kx, vx, gkfx, gkbx, gx, poolx = mixer_inputs(p_x, w_gk_f[l], b_gk_f[l], w_gk_b[l], b_gk_b[l])
        qc, kc, vc, gkfc, gkbc, gc, poolc = mixer_inputs(p_c, w_gk_f[l], b_gk_f[l], w_gk_b[l], b_gk_b[l])

        s0 = jnp.zeros((b_, GLA_HEADS, GLA_DK, GLA_DV), jnp.float32)
        o_c, s_f, s_b = gla_bidir(qc, kc, vc, gkfc, gkbc, s0, s0, with_output=not last)
        o_x, _, _ = gla_bidir(qx, kx, vx, gkfx, gkbx, s_f, s_b)

        mix_x = jnp.concatenate([gla_output(o_x, gx, gla_norm_w[l]),
                                 pool_project(pool_grid(poolx, rows), w_pool[l], pool_scale[l])], axis=-1)
        h = h + g1 * (mix_x.astype(h.dtype) @ w_out[l])

        if not last:
            mix_c = jnp.concatenate([gla_output(o_c, gc, gla_norm_w[l]),
                                     pool_project(pool_seq(poolc), w_pool[l], pool_scale[l])], axis=-1)
            hc = hc + cg1 * (mix_c.astype(hc.dtype) @ w_out[l])
            hc = hc + cg2 * moe(modulate(rmsnorm(hc, norm_mlp_w[l]), csh2, csc2),
                                w_router[l], b_router[l], w_gu[l], b_gu[l], w_down[l], b_down[l])

        h = h + g2 * moe(modulate(rmsnorm(h, norm_mlp_w[l]), sh2, sc2),
                         w_router[l], b_router[l], w_gu[l], b_gu[l], w_down[l], b_down[l])
    return rmsnorm(h, final_norm_w)
```

```python
import functools

import numpy as np
import jax
import jax.numpy as jnp
from jax import lax
from jax.experimental import pallas as pl
from jax.experimental.pallas import tpu as pltpu

F32 = jnp.float32
BF16 = jnp.bfloat16
I32 = jnp.int32

GRID_W = 64
GLA_HEADS = 4
GLA_CHUNK = 64
GATE_NORMALIZER = 16.0
POOL_WINDOWS = (2, 4, 8, 16)
TOP_K = 4
SWIGLU_LIMIT = 7.0
SWIGLU_ALPHA = 1.702
N_MOD = 6
EPS = 1e-6

LANES = 128
SUPER = 256
HEAD_PAIR_DK = 128
EXPERT_ROWS = 256


def _dot(a, b):
    return jnp.dot(a, b, preferred_element_type=F32)


def _dot_nt(a, b):
    return lax.dot_general(a, b, (((1,), (1,)), ((), ())), preferred_element_type=F32)


def _dot_tn(a, b):
    return lax.dot_general(a, b, (((0,), (0,)), ((), ())), preferred_element_type=F32)


def _split_bf16(x):
    hi = x.astype(BF16)
    lo = (x - hi.astype(F32)).astype(BF16)
    return hi, lo


def _rmsnorm(x, w):
    var = jnp.mean(x * x, axis=-1, keepdims=True)
    return x * lax.rsqrt(var + EPS) * w


def _mod_kernel(c_ref, w_ref, b_ref, o_ref):
    c = c_ref[...]
    s = c * jax.nn.sigmoid(c)
    o_ref[...] = jnp.dot(s, w_ref[...], precision=lax.Precision.HIGHEST,
                         preferred_element_type=F32) + b_ref[...]


def _mod_call(cc, w_ada, b_ada):
    rows, d = cc.shape
    n = w_ada.shape[1]
    tn = 1024
    return pl.pallas_call(
        _mod_kernel,
        grid=(n // tn,),
        in_specs=[pl.BlockSpec((rows, d), lambda j: (0, 0)),
                  pl.BlockSpec((d, tn), lambda j: (0, j)),
                  pl.BlockSpec((1, tn), lambda j: (0, j))],
        out_specs=pl.BlockSpec((rows, tn), lambda j: (0, j)),
        out_shape=jax.ShapeDtypeStruct((rows, n), F32),
        name="mod",
    )(cc, w_ada, b_ada)


def _inproj_kernel(x_ref, mod_ref, nw_ref, w_ref, wgk_ref, bgk_ref,
                   q_ref, k_ref, v_ref, g_ref, p_ref, gk_ref, *, qk, gw, pw, dk):
    x = x_ref[0]
    m = mod_ref[0]
    hm = _rmsnorm(x, nw_ref[...]) * (1.0 + m[1:2]) + m[0:1]
    p = _dot(hm.astype(BF16), w_ref[...])
    o = 0
    q_ref[0] = (p[:, o:o + qk] * (dk ** -0.5)).astype(BF16); o += qk
    k_ref[0] = p[:, o:o + qk].astype(BF16); o += qk
    v_ref[0] = p[:, o:o + gw].astype(BF16); o += gw
    g_ref[0] = p[:, o:o + gw].astype(BF16); o += gw
    p_ref[0] = p[:, o:o + pw].astype(BF16); o += pw
    r = p[:, o:o + LANES]
    z = _dot(r.astype(BF16), wgk_ref[...]) + bgk_ref[...]
    gk_ref[0] = (jnp.minimum(z, 0.0) - jnp.log1p(jnp.exp(-jnp.abs(z)))) * (1.0 / GATE_NORMALIZER)


def _inproj_call(x, mod, nw, w, wgk, bgk, *, qk, gw, pw, dk, tm):
    b, t, d = x.shape
    n_in = w.shape[1]
    bs = lambda width: pl.BlockSpec((1, tm, width), lambda i, j: (i, j, 0))
    const = lambda shape: pl.BlockSpec(shape, lambda i, j: (0,) * len(shape))
    per_batch = mod.shape[0] > 1
    return pl.pallas_call(
        functools.partial(_inproj_kernel, qk=qk, gw=gw, pw=pw, dk=dk),
        grid=(b, t // tm),
        in_specs=[bs(d),
                  pl.BlockSpec((1, N_MOD, d), (lambda i, j: (i, 0, 0)) if per_batch else (lambda i, j: (0, 0, 0))),
                  const((1, d)), const((d, n_in)), const((LANES, 2 * qk)), const((1, 2 * qk))],
        out_specs=[bs(qk), bs(qk), bs(gw), bs(gw), bs(pw), bs(2 * qk)],
        out_shape=[jax.ShapeDtypeStruct((b, t, qk), BF16), jax.ShapeDtypeStruct((b, t, qk), BF16),
                   jax.ShapeDtypeStruct((b, t, gw), BF16), jax.ShapeDtypeStruct((b, t, gw), BF16),
                   jax.ShapeDtypeStruct((b, t, pw), BF16), jax.ShapeDtypeStruct((b, t, 2 * qk), F32)],
        compiler_params=pltpu.CompilerParams(dimension_semantics=("parallel", "parallel")),
        name="inproj",
    )(x, mod, nw, w, wgk, bgk)


def _gla_super(q, k, v, gk, cm, amask, bd_mask, st_ref, chunk_order, want_out):
    n = SUPER
    hi, lo = _split_bf16(gk)
    cs = _dot(cm, hi) + _dot(cm, lo)
    bcum, blast, bmid = cs[0:n], cs[n:2 * n], cs[2 * n:3 * n]
    kp = (k * jnp.exp(blast - bcum)).astype(BF16)
    if want_out:
        qt = q * jnp.exp(bcum - bmid)
        kt = (k * jnp.exp(bmid - bcum)).astype(BF16)
        qh = (q * jnp.exp(bcum)).astype(BF16)
        lane = lax.broadcasted_iota(I32, qt.shape, 1)
        half = HEAD_PAIR_DK // 2
        o_heads = []
        for hh in range(2):
            sel = (lane < half) if hh == 0 else (lane >= half)
            a = _dot_nt(jnp.where(sel, qt, 0.0).astype(BF16), kt)
            a = jnp.where(amask, a, 0.0).astype(BF16)
            o_heads.append(_dot(a, v[:, hh * LANES:(hh + 1) * LANES]))
        o_intra = jnp.concatenate(o_heads, axis=1)
    outs = [None] * (n // GLA_CHUNK)
    for c in chunk_order:
        r0, r1 = c * GLA_CHUNK, (c + 1) * GLA_CHUNK
        st = st_ref[...]
        if want_out:
            outs[c] = _dot_nt(qh[r0:r1], st.astype(BF16))
        ut = _dot_tn(v[r0:r1], kp[r0:r1])
        decay = jnp.exp(blast[r0:r0 + 1, :])
        st_ref[...] = st * decay + jnp.where(bd_mask, ut, 0.0)
    if want_out:
        return o_intra + jnp.concatenate(outs, axis=0)
    return None


def _gla_kernel(q_ref, k_ref, v_ref, gkf_ref, gkb_ref, g_ref, kc_ref, vc_ref, gkfc_ref, gkbc_ref,
                nw_ref, cmf_ref, cmb_ref, o_ref, st_ref, oacc_ref):
    t = q_ref.shape[1]
    tc = kc_ref.shape[1]
    nsc, nscc = t // SUPER, tc // SUPER
    nch = SUPER // GLA_CHUNK
    cmf = cmf_ref[...]
    cmb = cmb_ref[...]
    amask_f = cmf[0:SUPER] > 0
    amask_b = cmb[0:SUPER] > 0
    row = lax.broadcasted_iota(I32, (2 * LANES, HEAD_PAIR_DK), 0)
    lane = lax.broadcasted_iota(I32, (2 * LANES, HEAD_PAIR_DK), 1)
    bd_mask = (row < LANES) == (lane < HEAD_PAIR_DK // 2)
    fwd_order = tuple(range(nch))
    bwd_order = tuple(reversed(range(nch)))

    def ctx_state(gk_ref, cm, order, j):
        rows = pl.ds(j * SUPER, SUPER)
        _gla_super(None, kc_ref[0, rows, :].astype(F32), vc_ref[0, rows, :], gk_ref[0, rows, :],
                   cm, None, bd_mask, st_ref, order, False)

    def latent(gk_ref, cm, amask, order, r0):
        rows = pl.ds(r0, SUPER)
        return _gla_super(q_ref[0, rows, :].astype(F32), k_ref[0, rows, :].astype(F32),
                          v_ref[0, rows, :], gk_ref[0, rows, :], cm, amask, bd_mask, st_ref, order, True)

    st_ref[...] = jnp.zeros_like(st_ref)
    for j in range(nscc):
        ctx_state(gkfc_ref, cmf, fwd_order, j)

    def fwd_body(j, carry):
        r0 = pl.multiple_of(j * SUPER, SUPER)
        oacc_ref[pl.ds(r0, SUPER), :] = latent(gkf_ref, cmf, amask_f, fwd_order, r0)
        return carry

    lax.fori_loop(0, nsc, fwd_body, 0)

    st_ref[...] = jnp.zeros_like(st_ref)
    for j in reversed(range(nscc)):
        ctx_state(gkbc_ref, cmb, bwd_order, j)

    nw = nw_ref[...]

    def bwd_body(jj, carry):
        r0 = pl.multiple_of((nsc - 1 - jj) * SUPER, SUPER)
        o = oacc_ref[pl.ds(r0, SUPER), :] + latent(gkb_ref, cmb, amask_b, bwd_order, r0)
        g = g_ref[0, pl.ds(r0, SUPER), :].astype(F32)
        gate = g * jax.nn.sigmoid(g)
        for hh in range(2):
            oh = o[:, hh * LANES:(hh + 1) * LANES]
            on = oh * lax.rsqrt(jnp.mean(oh * oh, axis=-1, keepdims=True) + EPS) * nw
            o_ref[0, pl.ds(r0, SUPER), hh * LANES:(hh + 1) * LANES] = (
                on * gate[:, hh * LANES:(hh + 1) * LANES]).astype(BF16)
        return carry

    lax.fori_loop(0, nsc, bwd_body, 0)


def _gla_masks():
    i = np.arange(SUPER)
    same = (i[:, None] // GLA_CHUNK) == (i[None, :] // GLA_CHUNK)
    pos = i[None, :] % GLA_CHUNK
    mid = GLA_CHUNK // 2
    fwd = np.concatenate([same & (i[None, :] <= i[:, None]), same, same & (pos <= mid - 1)], axis=0)
    bwd = np.concatenate([same & (i[None, :] >= i[:, None]), same, same & (pos >= mid)], axis=0)
    return jnp.asarray(fwd, BF16), jnp.asarray(bwd, BF16)


def _gla_call(q, k, v, gk, g, kc, vc, gkc, nw):
    b, t, qk = q.shape
    tc = kc.shape[1]
    npair = qk // HEAD_PAIR_DK
    cmf, cmb = _gla_masks()
    lat = lambda width, off: pl.BlockSpec((1, t, width), lambda i, j: (i, 0, j + off))
    ctx = lambda width, off: pl.BlockSpec((1, tc, width), lambda i, j: (i, 0, j + off))
    const = lambda shape: pl.BlockSpec(shape, lambda i, j: (0,) * len(shape))
    return pl.pallas_call(
        _gla_kernel,
        grid=(b, npair),
        in_specs=[lat(HEAD_PAIR_DK, 0), lat(HEAD_PAIR_DK, 0), lat(2 * LANES, 0),
                  lat(HEAD_PAIR_DK, 0), lat(HEAD_PAIR_DK, npair), lat(2 * LANES, 0),
                  ctx(HEAD_PAIR_DK, 0), ctx(2 * LANES, 0), ctx(HEAD_PAIR_DK, 0), ctx(HEAD_PAIR_DK, npair),
                  const((1, LANES)), const(cmf.shape), const(cmb.shape)],
        out_specs=lat(2 * LANES, 0),
        out_shape=jax.ShapeDtypeStruct((b, t, v.shape[2]), BF16),
        scratch_shapes=[pltpu.VMEM((2 * LANES, HEAD_PAIR_DK), F32), pltpu.VMEM((t, 2 * LANES), F32)],
        compiler_params=pltpu.CompilerParams(dimension_semantics=("parallel", "parallel")),
        name="gla",
    )(q, k, v, gk, gk, g, kc, vc, gkc, gkc, nw, cmf, cmb)


def _pool_kernel(x_ref, cm_ref, wp_ref, ps_ref, o_ref, y_ref, z_ref, *, rows):
    t = x_ref.shape[1]
    tok = lax.broadcasted_iota(I32, (t, LANES), 0)
    r = tok // GRID_W
    c = tok % GRID_W
    for gi, w in enumerate(POOL_WINDOWS):
        lo = w // 2
        hi = w - 1 - lo
        cols = slice(gi * LANES, (gi + 1) * LANES)
        cmat = cm_ref[gi]
        for blk in range(t // SUPER):
            rs = slice(blk * SUPER, (blk + 1) * SUPER)
            y_ref[rs, :] = _dot(cmat, x_ref[0, rs, cols])
        z_ref[...] = y_ref[...]
        for dr in range(-lo, hi + 1):
            sh = abs(dr) * GRID_W
            if dr == 0 or sh >= t:
                continue
            if dr > 0:
                z_ref[0:t - sh, :] += y_ref[sh:t, :]
            else:
                z_ref[sh:t, :] += y_ref[0:t - sh, :]
        cnt_r = jnp.minimum(r + hi + 1, rows) - jnp.maximum(r - lo, 0)
        cnt_c = jnp.minimum(c + hi + 1, GRID_W) - jnp.maximum(c - lo, 0)
        cnt = (cnt_r * cnt_c).astype(F32)
        pooled = z_ref[...] / cnt - x_ref[0, :, cols].astype(F32)
        yp = _dot(pooled.astype(BF16), wp_ref[gi]) * ps_ref[:, cols]
        o_ref[0, :, cols] = yp.astype(BF16)


def _pool_col_mats():
    i = np.arange(SUPER)
    same_row = (i[:, None] // GRID_W) == (i[None, :] // GRID_W)
    d = i[None, :] - i[:, None]
    mats = []
    for w in POOL_WINDOWS:
        lo = w // 2
        hi = w - 1 - lo
        mats.append(same_row & (d >= -lo) & (d <= hi))
    return jnp.asarray(np.stack(mats), BF16)


def _pool_call(xp, w_pool, pool_scale):
    b, t, pw = xp.shape
    ng = len(POOL_WINDOWS)
    cm = _pool_col_mats()
    const = lambda shape: pl.BlockSpec(shape, lambda i: (0,) * len(shape))
    return pl.pallas_call(
        functools.partial(_pool_kernel, rows=t // GRID_W),
        grid=(b,),
        in_specs=[pl.BlockSpec((1, t, pw), lambda i: (i, 0, 0)),
                  const(cm.shape), const((ng, LANES, LANES)), const((1, pw))],
        out_specs=pl.BlockSpec((1, t, pw), lambda i: (i, 0, 0)),
        out_shape=jax.ShapeDtypeStruct((b, t, pw), BF16),
        scratch_shapes=[pltpu.VMEM((t, LANES), F32), pltpu.VMEM((t, LANES), F32)],
        compiler_params=pltpu.CompilerParams(dimension_semantics=("parallel",)),
        name="pool",
    )(xp, cm, w_pool, pool_scale)


def _route_kernel(gla_ref, pool_ref, x_ref, mod_ref, wo_ref, nw_ref, wr_ref, br_ref, lt_ref,
                  h_ref, xt_ref, e_ref, rk_ref, wt_ref, cnt_ref, run_ref, *, gw, n_exp):
    i = pl.program_id(0)

    @pl.when(i == 0)
    def _():
        run_ref[...] = jnp.zeros_like(run_ref)

    m = mod_ref[0]
    acc = _dot(gla_ref[...], wo_ref[0:gw, :]) + _dot(pool_ref[...], wo_ref[gw:, :])
    h = x_ref[...] + m[2:3] * acc
    h_ref[...] = h
    xt = _rmsnorm(h, nw_ref[...]) * (1.0 + m[4:5]) + m[3:4]
    xt_ref[...] = xt
    xh, xl = _split_bf16(xt)
    wr = wr_ref[...]
    wh, wl = _split_bf16(wr)
    logits = _dot(xh, wh) + _dot(xl, wh) + _dot(xh, wl) + br_ref[...]
    lane = lax.broadcasted_iota(I32, logits.shape, 1)
    neg = jnp.float32(-jnp.inf)
    logits = jnp.where(lane < n_exp, logits, neg)
    vals, hots = [], []
    e_out = jnp.zeros(logits.shape, I32)
    for j in range(TOP_K):
        mx = jnp.max(logits, axis=-1, keepdims=True)
        idx = jnp.min(jnp.where(logits == mx, lane, LANES), axis=-1, keepdims=True)
        hot = lane == idx
        vals.append(mx)
        hots.append(hot)
        e_out = jnp.where(lane == j, idx, e_out)
        logits = jnp.where(hot, neg, logits)
    ex = [jnp.exp(v - vals[0]) for v in vals]
    den = ex[0] + ex[1] + ex[2] + ex[3]
    w_out = jnp.zeros(logits.shape, F32)
    for j in range(TOP_K):
        w_out = jnp.where(lane == j, ex[j] / den, w_out)
    osum = jnp.where(hots[0] | hots[1] | hots[2] | hots[3], 1.0, 0.0)
    before = _dot(lt_ref[...], osum.astype(BF16)) + run_ref[0:1, :]
    rk_out = jnp.zeros(logits.shape, I32)
    for j in range(TOP_K):
        rj = jnp.sum(jnp.where(hots[j], before, 0.0), axis=-1, keepdims=True)
        rk_out = jnp.where(lane == j, rj.astype(I32), rk_out)
    run = run_ref[0:1, :] + jnp.sum(osum, axis=0, keepdims=True)
    run_ref[...] = jnp.broadcast_to(run, run_ref.shape)
    e_ref[...] = e_out
    rk_ref[...] = rk_out
    wt_ref[...] = w_out
    cnt_ref[...] = jnp.broadcast_to(run, cnt_ref.shape).astype(I32)


def _route_call(gla, pool, x, mod, w_out, nw, wr, br, *, tm, n_exp):
    n, d = x.shape
    gw = gla.shape[1]
    t_per_b = n // mod.shape[0]
    lt = jnp.asarray(np.tril(np.ones((tm, tm), np.float32), -1), BF16)
    row = lambda width: pl.BlockSpec((tm, width), lambda i: (i, 0))
    const = lambda shape: pl.BlockSpec(shape, lambda i: (0,) * len(shape))
    sds = jax.ShapeDtypeStruct
    return pl.pallas_call(
        functools.partial(_route_kernel, gw=gw, n_exp=n_exp),
        grid=(n // tm,),
        in_specs=[row(gw), row(pool.shape[1]), row(d),
                  pl.BlockSpec((1, N_MOD, d), lambda i: (i * tm // t_per_b, 0, 0)),
                  const(w_out.shape), const((1, d)), const(wr.shape), const((1, LANES)), const((tm, tm))],
        out_specs=[row(d), row(d), row(LANES), row(LANES), row(LANES), const((8, LANES))],
        out_shape=[sds((n, d), F32), sds((n, d), F32), sds((n, LANES), I32), sds((n, LANES), I32),
                   sds((n, LANES), F32), sds((8, LANES), I32)],
        scratch_shapes=[pltpu.VMEM((8, LANES), F32)],
        compiler_params=pltpu.CompilerParams(dimension_semantics=("arbitrary",)),
        name="route",
    )(gla, pool, x, mod, w_out, nw, wr, br, lt)


def _row_copy(src, s, dst, d, sem):
    return pltpu.make_async_copy(src.at[pl.ds(s, 1), :], dst.at[pl.ds(d, 1), :], sem)


def _dispatch_kernel(dest_ref, padlo_ref, padhi_ref, xt_ref, xs_ref, zero_ref, sem, zsem, *, tq, n_exp):
    i = pl.program_id(0)

    @pl.when(i == 0)
    def _():
        zero_ref[...] = jnp.zeros_like(zero_ref)

        def per_expert(e, carry):
            def per_row(r, c2):
                cp = _row_copy(zero_ref, 0, xs_ref, r, zsem)
                cp.start()
                cp.wait()
                return c2
            return lax.fori_loop(padlo_ref[e], padhi_ref[e], per_row, carry)

        lax.fori_loop(0, n_exp, per_expert, 0)

    def issue(tl, carry):
        for j in range(TOP_K):
            _row_copy(xt_ref, i * tq + tl, xs_ref, dest_ref[tl * TOP_K + j], sem).start()
        return carry

    lax.fori_loop(0, tq, issue, 0)

    def drain(a, carry):
        _row_copy(xt_ref, 0, xs_ref, 0, sem).wait()
        return carry

    lax.fori_loop(0, tq * TOP_K, drain, 0)


def _dispatch_call(dest_flat, padlo, padhi, xt, *, n_pad, tq, n_exp):
    n, d = xt.shape
    smem = pltpu.SMEM
    return pl.pallas_call(
        functools.partial(_dispatch_kernel, tq=tq, n_exp=n_exp),
        grid=(n // tq,),
        in_specs=[pl.BlockSpec((tq * TOP_K,), lambda i: (i,), memory_space=smem),
                  pl.BlockSpec(memory_space=smem), pl.BlockSpec(memory_space=smem),
                  pl.BlockSpec(memory_space=pl.ANY)],
        out_specs=pl.BlockSpec(memory_space=pl.ANY),
        out_shape=jax.ShapeDtypeStruct((n_pad, d), xt.dtype),
        scratch_shapes=[pltpu.VMEM((8, d), xt.dtype), pltpu.SemaphoreType.DMA(()), pltpu.SemaphoreType.DMA(())],
        compiler_params=pltpu.CompilerParams(dimension_semantics=("arbitrary",), has_side_effects=True),
        name="dispatch",
    )(dest_flat, padlo, padhi, xt)


def _expert_kernel(be_ref, nu_ref, x_ref, wgu_ref, bgu_ref, wd_ref, bd_ref, y_ref, *, d_ff):
    i = pl.program_id(0)

    @pl.when(i < nu_ref[0])
    def _():
        gu = _dot(x_ref[...].astype(BF16), wgu_ref[0]) + bgu_ref[0]
        gate = jnp.minimum(gu[:, :d_ff], SWIGLU_LIMIT)
        up = jnp.clip(gu[:, d_ff:], -SWIGLU_LIMIT, SWIGLU_LIMIT)
        act = (up + 1.0) * gate * jax.nn.sigmoid(SWIGLU_ALPHA * gate)
        y_ref[...] = _dot(act.astype(BF16), wd_ref[0]) + bd_ref[0]


def _expert_call(block_e, n_used, xs, w_gu, b_gu, w_down, b_down):
    n_pad, d = xs.shape
    n_exp, _, two_ff = w_gu.shape
    d_ff = two_ff // 2
    nblk = n_pad // EXPERT_ROWS
    rows = lambda i, be, nu: (jnp.minimum(i, nu[0] - 1), 0)
    per_e = lambda i, be, nu: (be[i], 0, 0)
    grid_spec = pltpu.PrefetchScalarGridSpec(
        num_scalar_prefetch=2,
        grid=(nblk,),
        in_specs=[pl.BlockSpec((EXPERT_ROWS, d), rows),
                  pl.BlockSpec((1, d, two_ff), per_e), pl.BlockSpec((1, 1, two_ff), per_e),
                  pl.BlockSpec((1, d_ff, d), per_e), pl.BlockSpec((1, 1, d), per_e)],
        out_specs=pl.BlockSpec((EXPERT_ROWS, d), rows),
    )
    return pl.pallas_call(
        functools.partial(_expert_kernel, d_ff=d_ff),
        grid_spec=grid_spec,
        out_shape=jax.ShapeDtypeStruct((n_pad, d), F32),
        compiler_params=pltpu.CompilerParams(dimension_semantics=("arbitrary",),
                                             vmem_limit_bytes=48 * 1024 * 1024),
        name="experts",
    )(block_e, n_used, xs, w_gu, b_gu.reshape(n_exp, 1, two_ff), w_down, b_down.reshape(n_exp, 1, d))


def _combine_kernel(dest_ref, ys_ref, wt_ref, h_ref, mod_ref, fw_ref, o_ref, buf_ref, sem, *, tg):
    def issue(tl, carry):
        for j in range(TOP_K):
            pltpu.make_async_copy(ys_ref.at[pl.ds(dest_ref[tl * TOP_K + j], 1), :],
                                  buf_ref.at[j, pl.ds(tl, 1), :], sem).start()
        return carry

    lax.fori_loop(0, tg, issue, 0)

    def drain(a, carry):
        pltpu.make_async_copy(ys_ref.at[pl.ds(0, 1), :], buf_ref.at[0, pl.ds(0, 1), :], sem).wait()
        return carry

    lax.fori_loop(0, tg * TOP_K, drain, 0)

    wt = wt_ref[...]
    acc = wt[:, 0:1] * buf_ref[0]
    for j in range(1, TOP_K):
        acc = acc + wt[:, j:j + 1] * buf_ref[j]
    m = mod_ref[0]
    o_ref[...] = _rmsnorm(h_ref[...] + m[5:6] * acc, fw_ref[...])


def _combine_call(dest_flat, ys, wts, h, mod, fw, *, tg):
    n, d = h.shape
    t_per_b = n // mod.shape[0]
    row = lambda width: pl.BlockSpec((tg, width), lambda i: (i, 0))
    return pl.pallas_call(
        functools.partial(_combine_kernel, tg=tg),
        grid=(n // tg,),
        in_specs=[pl.BlockSpec((tg * TOP_K,), lambda i: (i,), memory_space=pltpu.SMEM),
                  pl.BlockSpec(memory_space=pl.ANY),
                  row(LANES), row(d),
                  pl.BlockSpec((1, N_MOD, d), lambda i: (i * tg // t_per_b, 0, 0)),
                  pl.BlockSpec((1, d), lambda i: (0, 0))],
        out_specs=row(d),
        out_shape=jax.ShapeDtypeStruct((n, d), F32),
        scratch_shapes=[pltpu.VMEM((TOP_K, tg, d), ys.dtype), pltpu.SemaphoreType.DMA(())],
        compiler_params=pltpu.CompilerParams(dimension_semantics=("arbitrary",)),
        name="combine",
    )(dest_flat, ys, wts, h, mod, fw)


def kernel(x, c, ctx, c_ctx, w_ada, b_ada, norm_mix_w, norm_mlp_w, w_in, w_gk_f, b_gk_f, w_gk_b, b_gk_b,
           gla_norm_w, w_pool, pool_scale, w_out, w_router, b_router, w_gu, b_gu, w_down, b_down,
           final_norm_w):
    b, t, d = x.shape
    assert w_ada.shape[0] == 1, "single-layer trunk"
    n_exp = w_router.shape[2]
    rank = w_gk_f.shape[1]
    qk = w_gk_f.shape[2]
    dk = qk // GLA_HEADS
    gw = GLA_HEADS * gla_norm_w.shape[1]
    pw = w_pool.shape[1] * w_pool.shape[2]
    assert w_in.shape[2] == 2 * qk + 2 * gw + 2 * rank + pw and 2 * rank <= LANES
    assert t % SUPER == 0 and ctx.shape[1] % SUPER == 0 and n_exp <= LANES

    rows = -(-(b + 1) // 8) * 8
    cc = jnp.zeros((rows, d), F32).at[:b].set(c).at[b].set(c_ctx)
    mod = _mod_call(cc, w_ada[0], b_ada)
    mod_x = mod[:b].reshape(b, N_MOD, d)
    mod_c = mod[b:b + 1].reshape(1, N_MOD, d)

    wi = w_in[0]
    o_r = 2 * qk + 2 * gw
    w_cat = jnp.concatenate([wi[:, :o_r], wi[:, o_r + 2 * rank:], wi[:, o_r:o_r + 2 * rank],
                             jnp.zeros((d, LANES - 2 * rank), F32)], axis=1).astype(BF16)
    wgk = jnp.zeros((LANES, 2 * qk), F32).at[:rank, :qk].set(w_gk_f[0]).at[rank:2 * rank, qk:].set(w_gk_b[0])
    bgk = jnp.concatenate([b_gk_f[0], b_gk_b[0]])[None, :]
    proj = functools.partial(_inproj_call, nw=norm_mix_w, w=w_cat, wgk=wgk.astype(BF16), bgk=bgk,
                             qk=qk, gw=gw, pw=pw, dk=dk)
    q, k, v, g, xp, gk = proj(x, mod_x, tm=512)
    _, kc, vc, _, _, gkc = proj(ctx, mod_c, tm=SUPER)

    gla = _gla_call(q, k, v, gk, g, kc, vc, gkc, gla_norm_w)
    pool = _pool_call(xp, w_pool[0].astype(BF16), pool_scale)

    n = b * t
    wr = jnp.zeros((d, LANES), F32).at[:, :n_exp].set(w_router[0])
    br = jnp.zeros((1, LANES), F32).at[0, :n_exp].set(b_router[0])
    h, xt, e_idx, rk, wts, cnt = _route_call(
        gla.reshape(n, gw), pool.reshape(n, pw), x.reshape(n, d), mod_x, w_out[0].astype(BF16),
        norm_mlp_w, wr, br, tm=512, n_exp=n_exp)

    counts = cnt[0, :n_exp]
    padded = (counts + EXPERT_ROWS - 1) // EXPERT_ROWS * EXPERT_ROWS
    pad_end = jnp.cumsum(padded)
    start_pad = pad_end - padded
    dest = (jnp.take(start_pad, e_idx[:, :TOP_K]) + rk[:, :TOP_K]).reshape(-1).astype(I32)
    n_pad = n * TOP_K + n_exp * EXPERT_ROWS
    nblk = n_pad // EXPERT_ROWS
    n_used = (pad_end[-1] // EXPERT_ROWS).astype(I32)
    blk = jnp.arange(nblk, dtype=I32)
    block_e = jnp.minimum(jnp.searchsorted(pad_end, blk * EXPERT_ROWS, side='right'), n_exp - 1).astype(I32)
    block_e = jnp.where(blk < n_used, block_e, block_e[jnp.maximum(n_used - 1, 0)])

    xs = _dispatch_call(dest, (start_pad + counts).astype(I32), pad_end.astype(I32), xt,
                        n_pad=n_pad, tq=512, n_exp=n_exp)
    ys = _expert_call(block_e, n_used.reshape(1), xs, w_gu[0].astype(BF16), b_gu[0],
                      w_down[0].astype(BF16), b_down[0])
    out = _combine_call(dest, ys, wts, h, mod_x, final_norm_w[None, :], tg=256)
    return out.reshape(b, t, d)
```

```python
import functools

import numpy as np
import jax
import jax.numpy as jnp
from jax import lax
from jax.experimental import pallas as pl
from jax.experimental.pallas import tpu as pltpu

F32 = jnp.float32
BF16 = jnp.bfloat16
I32 = jnp.int32

GRID_W = 64
GLA_HEADS = 4
GLA_CHUNK = 64
GATE_NORMALIZER = 16.0
POOL_WINDOWS = (2, 4, 8, 16)
TOP_K = 4
SWIGLU_LIMIT = 7.0
SWIGLU_ALPHA = 1.702
N_MOD = 6
EPS = 1e-6

LANES = 128
SUPER = 256
HEAD_PAIR_DK = 128
EXPERT_ROWS = 256


def _dot(a, b):
    return jnp.dot(a, b, preferred_element_type=F32)


def _dot_nt(a, b):
    return lax.dot_general(a, b, (((1,), (1,)), ((), ())), preferred_element_type=F32)


def _dot_tn(a, b):
    return lax.dot_general(a, b, (((0,), (0,)), ((), ())), preferred_element_type=F32)


def _split_bf16(x):
    hi = x.astype(BF16)
    lo = (x - hi.astype(F32)).astype(BF16)
    return hi, lo


def _rmsnorm(x, w):
    var = jnp.mean(x * x, axis=-1, keepdims=True)
    return x * lax.rsqrt(var + EPS) * w


def _mod_kernel(c_ref, w_ref, b_ref, o_ref):
    c = c_ref[...]
    s = c * jax.nn.sigmoid(c)
    o_ref[...] = jnp.dot(s, w_ref[...], precision=lax.Precision.HIGHEST,
                         preferred_element_type=F32) + b_ref[...]


def _mod_call(cc, w_ada, b_ada):
    rows, d = cc.shape
    n = w_ada.shape[1]
    tn = 1024
    return pl.pallas_call(
        _mod_kernel,
        grid=(n // tn,),
        in_specs=[pl.BlockSpec((rows, d), lambda j: (0, 0)),
                  pl.BlockSpec((d, tn), lambda j: (0, j)),
                  pl.BlockSpec((1, tn), lambda j: (0, j))],
        out_specs=pl.BlockSpec((rows, tn), lambda j: (0, j)),
        out_shape=jax.ShapeDtypeStruct((rows, n), F32),
        name="mod",
    )(cc, w_ada, b_ada)


def _inproj_kernel(x_ref, mod_ref, nw_ref, w_ref, wgk_ref, bgk_ref,
                   q_ref, k_ref, v_ref, g_ref, p_ref, gk_ref, *, qk, gw, pw, dk):
    x = x_ref[0]
    m = mod_ref[0]
    hm = _rmsnorm(x, nw_ref[...]) * (1.0 + m[1:2]) + m[0:1]
    p = _dot(hm.astype(BF16), w_ref[...])
    o = 0
    q_ref[0] = (p[:, o:o + qk] * (dk ** -0.5)).astype(BF16); o += qk
    k_ref[0] = p[:, o:o + qk].astype(BF16); o += qk
    v_ref[0] = p[:, o:o + gw].astype(BF16); o += gw
    g_ref[0] = p[:, o:o + gw].astype(BF16); o += gw
    p_ref[0] = p[:, o:o + pw].astype(BF16); o += pw
    r = p[:, o:o + LANES]
    z = _dot(r.astype(BF16), wgk_ref[...]) + bgk_ref[...]
    gk_ref[0] = (jnp.minimum(z, 0.0) - jnp.log1p(jnp.exp(-jnp.abs(z)))) * (1.0 / GATE_NORMALIZER)


def _inproj_call(x, mod, nw, w, wgk, bgk, *, qk, gw, pw, dk, tm):
    b, t, d = x.shape
    n_in = w.shape[1]
    bs = lambda width: pl.BlockSpec((1, tm, width), lambda i, j: (i, j, 0))
    const = lambda shape: pl.BlockSpec(shape, lambda i, j: (0,) * len(shape))
    per_batch = mod.shape[0] > 1
    return pl.pallas_call(
        functools.partial(_inproj_kernel, qk=qk, gw=gw, pw=pw, dk=dk),
        grid=(b, t // tm),
        in_specs=[bs(d),
                  pl.BlockSpec((1, N_MOD, d), (lambda i, j: (i, 0, 0)) if per_batch else (lambda i, j: (0, 0, 0))),
                  const((1, d)), const((d, n_in)), const((LANES, 2 * qk)), const((1, 2 * qk))],
        out_specs=[bs(qk), bs(qk), bs(gw), bs(gw), bs(pw), bs(2 * qk)],
        out_shape=[jax.ShapeDtypeStruct((b, t, qk), BF16), jax.ShapeDtypeStruct((b, t, qk), BF16),
                   jax.ShapeDtypeStruct((b, t, gw), BF16), jax.ShapeDtypeStruct((b, t, gw), BF16),
                   jax.ShapeDtypeStruct((b, t, pw), BF16), jax.ShapeDtypeStruct((b, t, 2 * qk), F32)],
        compiler_params=pltpu.CompilerParams(dimension_semantics=("arbitrary", "arbitrary")),
        name="inproj",
    )(x, mod, nw, w, wgk, bgk)


def _gla_super(q, k, v, gk, cm, amask, bd_mask, st_ref, chunk_order, want_out):
    n = SUPER
    hi, lo = _split_bf16(gk)
    cs = _dot(cm, hi) + _dot(cm, lo)
    bcum, blast, bmid = cs[0:n], cs[n:2 * n], cs[2 * n:3 * n]
    kp = (k * jnp.exp(blast - bcum)).astype(BF16)
    if want_out:
        qt = q * jnp.exp(bcum - bmid)
        kt = (k * jnp.exp(bmid - bcum)).astype(BF16)
        qh = (q * jnp.exp(bcum)).astype(BF16)
        lane = lax.broadcasted_iota(I32, qt.shape, 1)
        half = HEAD_PAIR_DK // 2
        o_heads = []
        for hh in range(2):
            sel = (lane < half) if hh == 0 else (lane >= half)
            a = _dot_nt(jnp.where(sel, qt, 0.0).astype(BF16), kt)
            a = jnp.where(amask, a, 0.0).astype(BF16)
            o_heads.append(_dot(a, v[:, hh * LANES:(hh + 1) * LANES]))
        o_intra = jnp.concatenate(o_heads, axis=1)
    outs = [None] * (n // GLA_CHUNK)
    for c in chunk_order:
        r0, r1 = c * GLA_CHUNK, (c + 1) * GLA_CHUNK
        st = st_ref[...]
        if want_out:
            outs[c] = _dot_nt(qh[r0:r1], st.astype(BF16))
        ut = _dot_tn(v[r0:r1], kp[r0:r1])
        decay = jnp.exp(blast[r0:r0 + 1, :])
        st_ref[...] = st * decay + jnp.where(bd_mask, ut, 0.0)
    if want_out:
        return o_intra + jnp.concatenate(outs, axis=0)
    return None


def _gla_kernel(q_ref, k_ref, v_ref, gkf_ref, gkb_ref, g_ref, kc_ref, vc_ref, gkfc_ref, gkbc_ref,
                nw_ref, cmf_ref, cmb_ref, o_ref, st_ref, oacc_ref):
    t = q_ref.shape[1]
    tc = kc_ref.shape[1]
    nsc, nscc = t // SUPER, tc // SUPER
    nch = SUPER // GLA_CHUNK
    cmf = cmf_ref[...]
    cmb = cmb_ref[...]
    amask_f = cmf[0:SUPER] > 0
    amask_b = cmb[0:SUPER] > 0
    row = lax.broadcasted_iota(I32, (2 * LANES, HEAD_PAIR_DK), 0)
    lane = lax.broadcasted_iota(I32, (2 * LANES, HEAD_PAIR_DK), 1)
    bd_mask = (row < LANES) == (lane < HEAD_PAIR_DK // 2)
    fwd_order = tuple(range(nch))
    bwd_order = tuple(reversed(range(nch)))

    def ctx_state(gk_ref, cm, order, j):
        rows = pl.ds(j * SUPER, SUPER)
        _gla_super(None, kc_ref[0, rows, :].astype(F32), vc_ref[0, rows, :], gk_ref[0, rows, :],
                   cm, None, bd_mask, st_ref, order, False)

    def latent(gk_ref, cm, amask, order, r0):
        rows = pl.ds(r0, SUPER)
        return _gla_super(q_ref[0, rows, :].astype(F32), k_ref[0, rows, :].astype(F32),
                          v_ref[0, rows, :], gk_ref[0, rows, :], cm, amask, bd_mask, st_ref, order, True)

    st_ref[...] = jnp.zeros_like(st_ref)
    for j in range(nscc):
        ctx_state(gkfc_ref, cmf, fwd_order, j)

    def fwd_body(j, carry):
        r0 = pl.multiple_of(j * SUPER, SUPER)
        oacc_ref[pl.ds(r0, SUPER), :] = latent(gkf_ref, cmf, amask_f, fwd_order, r0)
        return carry

    lax.fori_loop(0, nsc, fwd_body, 0)

    st_ref[...] = jnp.zeros_like(st_ref)
    for j in reversed(range(nscc)):
        ctx_state(gkbc_ref, cmb, bwd_order, j)

    nw = nw_ref[...]

    def bwd_body(jj, carry):
        r0 = pl.multiple_of((nsc - 1 - jj) * SUPER, SUPER)
        o = oacc_ref[pl.ds(r0, SUPER), :] + latent(gkb_ref, cmb, amask_b, bwd_order, r0)
        g = g_ref[0, pl.ds(r0, SUPER), :].astype(F32)
        gate = g * jax.nn.sigmoid(g)
        for hh in range(2):
            oh = o[:, hh * LANES:(hh + 1) * LANES]
            on = oh * lax.rsqrt(jnp.mean(oh * oh, axis=-1, keepdims=True) + EPS) * nw
            o_ref[0, pl.ds(r0, SUPER), hh * LANES:(hh + 1) * LANES] = (
                on * gate[:, hh * LANES:(hh + 1) * LANES]).astype(BF16)
        return carry

    lax.fori_loop(0, nsc, bwd_body, 0)


def _gla_masks():
    i = np.arange(SUPER)
    same = (i[:, None] // GLA_CHUNK) == (i[None, :] // GLA_CHUNK)
    pos = i[None, :] % GLA_CHUNK
    mid = GLA_CHUNK // 2
    fwd = np.concatenate([same & (i[None, :] <= i[:, None]), same, same & (pos <= mid - 1)], axis=0)
    bwd = np.concatenate([same & (i[None, :] >= i[:, None]), same, same & (pos >= mid)], axis=0)
    return jnp.asarray(fwd, BF16), jnp.asarray(bwd, BF16)


def _gla_call(q, k, v, gk, g, kc, vc, gkc, nw):
    b, t, qk = q.shape
    tc = kc.shape[1]
    npair = qk // HEAD_PAIR_DK
    cmf, cmb = _gla_masks()
    lat = lambda width, off: pl.BlockSpec((1, t, width), lambda i, j: (i, 0, j + off))
    ctx = lambda width, off: pl.BlockSpec((1, tc, width), lambda i, j: (i, 0, j + off))
    const = lambda shape: pl.BlockSpec(shape, lambda i, j: (0,) * len(shape))
    return pl.pallas_call(
        _gla_kernel,
        grid=(b, npair),
        in_specs=[lat(HEAD_PAIR_DK, 0), lat(HEAD_PAIR_DK, 0), lat(2 * LANES, 0),
                  lat(HEAD_PAIR_DK, 0), lat(HEAD_PAIR_DK, npair), lat(2 * LANES, 0),
                  ctx(HEAD_PAIR_DK, 0), ctx(2 * LANES, 0), ctx(HEAD_PAIR_DK, 0), ctx(HEAD_PAIR_DK, npair),
                  const((1, LANES)), const(cmf.shape), const(cmb.shape)],
        out_specs=lat(2 * LANES, 0),
        out_shape=jax.ShapeDtypeStruct((b, t, v.shape[2]), BF16),
        scratch_shapes=[pltpu.VMEM((2 * LANES, HEAD_PAIR_DK), F32), pltpu.VMEM((t, 2 * LANES), F32)],
        compiler_params=pltpu.CompilerParams(dimension_semantics=("arbitrary", "arbitrary")),
        name="gla",
    )(q, k, v, gk, gk, g, kc, vc, gkc, gkc, nw, cmf, cmb)


def _pool_kernel(x_ref, cm_ref, wp_ref, ps_ref, o_ref, y_ref, z_ref, *, rows):
    t = x_ref.shape[1]
    tok = lax.broadcasted_iota(I32, (t, LANES), 0)
    r = tok // GRID_W
    c = tok % GRID_W
    for gi, w in enumerate(POOL_WINDOWS):
        lo = w // 2
        hi = w - 1 - lo
        cols = slice(gi * LANES, (gi + 1) * LANES)
        cmat = cm_ref[gi]
        for blk in range(t // SUPER):
            rs = slice(blk * SUPER, (blk + 1) * SUPER)
            y_ref[rs, :] = _dot(cmat, x_ref[0, rs, cols])
        z_ref[...] = y_ref[...]
        for dr in range(-lo, hi + 1):
            sh = abs(dr) * GRID_W
            if dr == 0 or sh >= t:
                continue
            if dr > 0:
                z_ref[0:t - sh, :] += y_ref[sh:t, :]
            else:
                z_ref[sh:t, :] += y_ref[0:t - sh, :]
        cnt_r = jnp.minimum(r + hi + 1, rows) - jnp.maximum(r - lo, 0)
        cnt_c = jnp.minimum(c + hi + 1, GRID_W) - jnp.maximum(c - lo, 0)
        cnt = (cnt_r * cnt_c).astype(F32)
        pooled = z_ref[...] / cnt - x_ref[0, :, cols].astype(F32)
        yp = _dot(pooled.astype(BF16), wp_ref[gi]) * ps_ref[:, cols]
        o_ref[0, :, cols] = yp.astype(BF16)


def _pool_col_mats():
    i = np.arange(SUPER)
    same_row = (i[:, None] // GRID_W) == (i[None, :] // GRID_W)
    d = i[None, :] - i[:, None]
    mats = []
    for w in POOL_WINDOWS:
        lo = w // 2
        hi = w - 1 - lo
        mats.append(same_row & (d >= -lo) & (d <= hi))
    return jnp.asarray(np.stack(mats), BF16)


def _pool_call(xp, w_pool, pool_scale):
    b, t, pw = xp.shape
    ng = len(POOL_WINDOWS)
    cm = _pool_col_mats()
    const = lambda shape: pl.BlockSpec(shape, lambda i: (0,) * len(shape))
    return pl.pallas_call(
        functools.partial(_pool_kernel, rows=t // GRID_W),
        grid=(b,),
        in_specs=[pl.BlockSpec((1, t, pw), lambda i: (i, 0, 0)),
                  const(cm.shape), const((ng, LANES, LANES)), const((1, pw))],
        out_specs=pl.BlockSpec((1, t, pw), lambda i: (i, 0, 0)),
        out_shape=jax.ShapeDtypeStruct((b, t, pw), BF16),
        scratch_shapes=[pltpu.VMEM((t, LANES), F32), pltpu.VMEM((t, LANES), F32)],
        compiler_params=pltpu.CompilerParams(dimension_semantics=("arbitrary",)),
        name="pool",
    )(xp, cm, w_pool, pool_scale)


def _route_kernel(gla_ref, pool_ref, x_ref, mod_ref, wo_ref, nw_ref, wr_ref, br_ref, lt_ref,
                  h_ref, xt_ref, e_ref, rk_ref, wt_ref, cnt_ref, run_ref, *, gw, n_exp):
    i = pl.program_id(0)

    @pl.when(i == 0)
    def _():
        run_ref[...] = jnp.zeros_like(run_ref)

    m = mod_ref[0]
    acc = _dot(gla_ref[...], wo_ref[0:gw, :]) + _dot(pool_ref[...], wo_ref[gw:, :])
    h = x_ref[...] + m[2:3] * acc
    h_ref[...] = h
    xt = _rmsnorm(h, nw_ref[...]) * (1.0 + m[4:5]) + m[3:4]
    xt_ref[...] = xt
    xh, xl = _split_bf16(xt)
    wr = wr_ref[...]
    wh, wl = _split_bf16(wr)
    logits = _dot(xh, wh) + _dot(xl, wh) + _dot(xh, wl) + br_ref[...]
    lane = lax.broadcasted_iota(I32, logits.shape, 1)
    neg = jnp.float32(-jnp.inf)
    logits = jnp.where(lane < n_exp, logits, neg)
    vals, hots = [], []
    e_out = jnp.zeros(logits.shape, I32)
    for j in range(TOP_K):
        mx = jnp.max(logits, axis=-1, keepdims=True)
        idx = jnp.min(jnp.where(logits == mx, lane, LANES), axis=-1, keepdims=True)
        hot = lane == idx
        vals.append(mx)
        hots.append(hot)
        e_out = jnp.where(lane == j, idx, e_out)
        logits = jnp.where(hot, neg, logits)
    ex = [jnp.exp(v - vals[0]) for v in vals]
    den = ex[0] + ex[1] + ex[2] + ex[3]
    w_out = jnp.zeros(logits.shape, F32)
    for j in range(TOP_K):
        w_out = jnp.where(lane == j, ex[j] / den, w_out)
    osum = jnp.where(hots[0] | hots[1] | hots[2] | hots[3], 1.0, 0.0)
    before = _dot(lt_ref[...], osum.astype(BF16)) + run_ref[0:1, :]
    rk_out = jnp.zeros(logits.shape, I32)
    for j in range(TOP_K):
        rj = jnp.sum(jnp.where(hots[j], before, 0.0), axis=-1, keepdims=True)
        rk_out = jnp.where(lane == j, rj.astype(I32), rk_out)
    run = run_ref[0:1, :] + jnp.sum(osum, axis=0, keepdims=True)
    run_ref[...] = jnp.broadcast_to(run, run_ref.shape)
    e_ref[...] = e_out
    rk_ref[...] = rk_out
    wt_ref[...] = w_out
    cnt_ref[...] = jnp.broadcast_to(run, cnt_ref.shape).astype(I32)


def _route_call(gla, pool, x, mod, w_out, nw, wr, br, *, tm, n_exp):
    n, d = x.shape
    gw = gla.shape[1]
    t_per_b = n // mod.shape[0]
    lt = jnp.asarray(np.tril(np.ones((tm, tm), np.float32), -1), BF16)
    row = lambda width: pl.BlockSpec((tm, width), lambda i: (i, 0))
    const = lambda shape: pl.BlockSpec(shape, lambda i: (0,) * len(shape))
    sds = jax.ShapeDtypeStruct
    return pl.pallas_call(
        functools.partial(_route_kernel, gw=gw, n_exp=n_exp),
        grid=(n // tm,),
        in_specs=[row(gw), row(pool.shape[1]), row(d),
                  pl.BlockSpec((1, N_MOD, d), lambda i: (i * tm // t_per_b, 0, 0)),
                  const(w_out.shape), const((1, d)), const(wr.shape), const((1, LANES)), const((tm, tm))],
        out_specs=[row(d), row(d), row(LANES), row(LANES), row(LANES), const((8, LANES))],
        out_shape=[sds((n, d), F32), sds((n, d), F32), sds((n, LANES), I32), sds((n, LANES), I32),
                   sds((n, LANES), F32), sds((8, LANES), I32)],
        scratch_shapes=[pltpu.VMEM((8, LANES), F32)],
        compiler_params=pltpu.CompilerParams(dimension_semantics=("arbitrary",)),
        name="route",
    )(gla, pool, x, mod, w_out, nw, wr, br, lt)


def _row_copy(src, s, dst, d, sem):
    return pltpu.make_async_copy(src.at[pl.ds(s, 1), :], dst.at[pl.ds(d, 1), :], sem)


def _wait_rows(hbm_ref, n_rows, sem):
    pltpu.make_async_copy(hbm_ref.at[pl.ds(0, n_rows), :], hbm_ref.at[pl.ds(0, n_rows), :], sem).wait()


def _dispatch_kernel(dest_ref, padlo_ref, padhi_ref, xt_ref, xs_ref, zero_ref, sem, zsem, *, tq, n_exp):
    i = pl.program_id(0)

    @pl.when(i == 0)
    def _():
        zero_ref[...] = jnp.zeros_like(zero_ref)

        def per_expert(e, carry):
            def per_row(r, c2):
                cp = _row_copy(zero_ref, 0, xs_ref, r, zsem)
                cp.start()
                cp.wait()
                return c2
            return lax.fori_loop(padlo_ref[e], padhi_ref[e], per_row, carry)

        lax.fori_loop(0, n_exp, per_expert, 0)

    def issue(tl, carry):
        for j in range(TOP_K):
            _row_copy(xt_ref, tl, xs_ref, dest_ref[tl * TOP_K + j], sem).start()
        return carry

    lax.fori_loop(0, tq, issue, 0)
    _wait_rows(xs_ref, tq * TOP_K, sem)


def _dispatch_call(dest_flat, padlo, padhi, xt, *, n_pad, tq, n_exp):
    n, d = xt.shape
    smem = pltpu.SMEM
    return pl.pallas_call(
        functools.partial(_dispatch_kernel, tq=tq, n_exp=n_exp),
        grid=(n // tq,),
        in_specs=[pl.BlockSpec((tq * TOP_K,), lambda i: (i,), memory_space=smem),
                  pl.BlockSpec(memory_space=smem), pl.BlockSpec(memory_space=smem),
                  pl.BlockSpec((tq, d), lambda i: (i, 0))],
        out_specs=pl.BlockSpec(memory_space=pl.ANY),
        out_shape=jax.ShapeDtypeStruct((n_pad, d), xt.dtype),
        scratch_shapes=[pltpu.VMEM((8, d), xt.dtype), pltpu.SemaphoreType.DMA(()), pltpu.SemaphoreType.DMA(())],
        compiler_params=pltpu.CompilerParams(dimension_semantics=("arbitrary",)),
        name="dispatch",
    )(dest_flat, padlo, padhi, xt)


def _expert_kernel(be_ref, nu_ref, x_ref, wgu_ref, bgu_ref, wd_ref, bd_ref, y_ref, *, d_ff):
    i = pl.program_id(0)

    @pl.when(i < nu_ref[0])
    def _():
        gu = _dot(x_ref[...].astype(BF16), wgu_ref[0]) + bgu_ref[0]
        gate = jnp.minimum(gu[:, :d_ff], SWIGLU_LIMIT)
        up = jnp.clip(gu[:, d_ff:], -SWIGLU_LIMIT, SWIGLU_LIMIT)
        act = (up + 1.0) * gate * jax.nn.sigmoid(SWIGLU_ALPHA * gate)
        y_ref[...] = _dot(act.astype(BF16), wd_ref[0]) + bd_ref[0]


def _expert_call(block_e, n_used, xs, w_gu, b_gu, w_down, b_down):
    n_pad, d = xs.shape
    n_exp, _, two_ff = w_gu.shape
    d_ff = two_ff // 2
    nblk = n_pad // EXPERT_ROWS
    rows = lambda i, be, nu: (jnp.minimum(i, nu[0] - 1), 0)
    per_e = lambda i, be, nu: (be[i], 0, 0)
    grid_spec = pltpu.PrefetchScalarGridSpec(
        num_scalar_prefetch=2,
        grid=(nblk,),
        in_specs=[pl.BlockSpec((EXPERT_ROWS, d), rows),
                  pl.BlockSpec((1, d, two_ff), per_e), pl.BlockSpec((1, 1, two_ff), per_e),
                  pl.BlockSpec((1, d_ff, d), per_e), pl.BlockSpec((1, 1, d), per_e)],
        out_specs=pl.BlockSpec((EXPERT_ROWS, d), rows),
    )
    return pl.pallas_call(
        functools.partial(_expert_kernel, d_ff=d_ff),
        grid_spec=grid_spec,
        out_shape=jax.ShapeDtypeStruct((n_pad, d), F32),
        compiler_params=pltpu.CompilerParams(dimension_semantics=("arbitrary",),
                                             vmem_limit_bytes=48 * 1024 * 1024),
        name="experts",
    )(block_e, n_used, xs, w_gu, b_gu.reshape(n_exp, 1, two_ff), w_down, b_down.reshape(n_exp, 1, d))


def _combine_kernel(dest_ref, dnext_ref, ys_ref, wt_ref, h_ref, mod_ref, fw_ref, o_ref, buf_ref, sem, *, tg):
    i = pl.program_id(0)
    slot = i % 2

    def gather(d_ref, s):
        def issue(tl, carry):
            for j in range(TOP_K):
                pltpu.make_async_copy(ys_ref.at[pl.ds(d_ref[tl * TOP_K + j], 1), :],
                                      buf_ref.at[s, j, pl.ds(tl, 1), :], sem.at[s]).start()
            return carry

        lax.fori_loop(0, tg, issue, 0)

    @pl.when(i == 0)
    def _():
        gather(dest_ref, 0)

    @pl.when(i + 1 < pl.num_programs(0))
    def _():
        gather(dnext_ref, 1 - slot)

    _wait_rows(ys_ref, tg * TOP_K, sem.at[slot])

    wt = wt_ref[...]
    acc = wt[:, 0:1] * buf_ref[slot, 0]
    for j in range(1, TOP_K):
        acc = acc + wt[:, j:j + 1] * buf_ref[slot, j]
    m = mod_ref[0]
    o_ref[...] = _rmsnorm(h_ref[...] + m[5:6] * acc, fw_ref[...])


def _combine_call(dest_flat, ys, wts, h, mod, fw, *, tg):
    n, d = h.shape
    t_per_b = n // mod.shape[0]
    row = lambda width: pl.BlockSpec((tg, width), lambda i: (i, 0))
    last = n // tg - 1
    return pl.pallas_call(
        functools.partial(_combine_kernel, tg=tg),
        grid=(n // tg,),
        in_specs=[pl.BlockSpec((tg * TOP_K,), lambda i: (i,), memory_space=pltpu.SMEM),
                  pl.BlockSpec((tg * TOP_K,), lambda i: (jnp.minimum(i + 1, last),), memory_space=pltpu.SMEM),
                  pl.BlockSpec(memory_space=pl.ANY),
                  row(LANES), row(d),
                  pl.BlockSpec((1, N_MOD, d), lambda i: (i * tg // t_per_b, 0, 0)),
                  pl.BlockSpec((1, d), lambda i: (0, 0))],
        out_specs=row(d),
        out_shape=jax.ShapeDtypeStruct((n, d), F32),
        scratch_shapes=[pltpu.VMEM((2, TOP_K, tg, d), ys.dtype), pltpu.SemaphoreType.DMA((2,))],
        compiler_params=pltpu.CompilerParams(dimension_semantics=("arbitrary",)),
        name="combine",
    )(dest_flat, dest_flat, ys, wts, h, mod, fw)


def kernel(x, c, ctx, c_ctx, w_ada, b_ada, norm_mix_w, norm_mlp_w, w_in, w_gk_f, b_gk_f, w_gk_b, b_gk_b,
           gla_norm_w, w_pool, pool_scale, w_out, w_router, b_router, w_gu, b_gu, w_down, b_down,
           final_norm_w):
    b, t, d = x.shape
    assert w_ada.shape[0] == 1, "single-layer trunk"
    n_exp = w_router.shape[2]
    rank = w_gk_f.shape[1]
    qk = w_gk_f.shape[2]
    dk = qk // GLA_HEADS
    gw = GLA_HEADS * gla_norm_w.shape[1]
    pw = w_pool.shape[1] * w_pool.shape[2]
    assert w_in.shape[2] == 2 * qk + 2 * gw + 2 * rank + pw and 2 * rank <= LANES
    assert t % SUPER == 0 and ctx.shape[1] % SUPER == 0 and n_exp <= LANES

    rows = -(-(b + 1) // 8) * 8
    cc = jnp.zeros((rows, d), F32).at[:b].set(c).at[b].set(c_ctx)
    mod = _mod_call(cc, w_ada[0], b_ada)
    mod_x = mod[:b].reshape(b, N_MOD, d)
    mod_c = mod[b:b + 1].reshape(1, N_MOD, d)

    wi = w_in[0]
    o_r = 2 * qk + 2 * gw
    w_cat = jnp.concatenate([wi[:, :o_r], wi[:, o_r + 2 * rank:], wi[:, o_r:o_r + 2 * rank],
                             jnp.zeros((d, LANES - 2 * rank), F32)], axis=1).astype(BF16)
    wgk = jnp.zeros((LANES, 2 * qk), F32).at[:rank, :qk].set(w_gk_f[0]).at[rank:2 * rank, qk:].set(w_gk_b[0])
    bgk = jnp.concatenate([b_gk_f[0], b_gk_b[0]])[None, :]
    proj = functools.partial(_inproj_call, nw=norm_mix_w, w=w_cat, wgk=wgk.astype(BF16), bgk=bgk,
                             qk=qk, gw=gw, pw=pw, dk=dk)
    q, k, v, g, xp, gk = proj(x, mod_x, tm=512)
    _, kc, vc, _, _, gkc = proj(ctx, mod_c, tm=SUPER)

    gla = _gla_call(q, k, v, gk, g, kc, vc, gkc, gla_norm_w)
    pool = _pool_call(xp, w_pool[0].astype(BF16), pool_scale)

    n = b * t
    wr = jnp.zeros((d, LANES), F32).at[:, :n_exp].set(w_router[0])
    br = jnp.zeros((1, LANES), F32).at[0, :n_exp].set(b_router[0])
    h, xt, e_idx, rk, wts, cnt = _route_call(
        gla.reshape(n, gw), pool.reshape(n, pw), x.reshape(n, d), mod_x, w_out[0].astype(BF16),
        norm_mlp_w, wr, br, tm=512, n_exp=n_exp)

    counts = cnt[0, :n_exp]
    padded = (counts + EXPERT_ROWS - 1) // EXPERT_ROWS * EXPERT_ROWS
    pad_end = jnp.cumsum(padded)
    start_pad = pad_end - padded
    dest = (jnp.take(start_pad, e_idx[:, :TOP_K]) + rk[:, :TOP_K]).reshape(-1).astype(I32)
    n_pad = n * TOP_K + n_exp * EXPERT_ROWS
    nblk = n_pad // EXPERT_ROWS
    n_used = (pad_end[-1] // EXPERT_ROWS).astype(I32)
    blk = jnp.arange(nblk, dtype=I32)
    block_e = jnp.sum((pad_end[None, :] <= (blk * EXPERT_ROWS)[:, None]).astype(I32), axis=1)
    block_e = jnp.minimum(block_e, n_exp - 1)
    block_e = jnp.where(blk < n_used, block_e, block_e[jnp.maximum(n_used - 1, 0)])

    xs = _dispatch_call(dest, (start_pad + counts).astype(I32), pad_end.astype(I32), xt,
                        n_pad=n_pad, tq=512, n_exp=n_exp)
    ys = _expert_call(block_e, n_used.reshape(1), xs, w_gu[0].astype(BF16), b_gu[0],
                      w_down[0].astype(BF16), b_down[0])
    out = _combine_call(dest, ys, wts, h, mod_x, final_norm_w[None, :], tg=256)
    return out.reshape(b, t, d)
```

```python
import functools

import numpy as np
import jax
import jax.numpy as jnp
from jax import lax
from jax.experimental import pallas as pl
from jax.experimental.pallas import tpu as pltpu

F32 = jnp.float32
BF16 = jnp.bfloat16
I32 = jnp.int32

GRID_W = 64
GLA_HEADS = 4
GLA_CHUNK = 64
GATE_NORMALIZER = 16.0
POOL_WINDOWS = (2, 4, 8, 16)
TOP_K = 4
SWIGLU_LIMIT = 7.0
SWIGLU_ALPHA = 1.702
N_MOD = 6
EPS = 1e-6

LANES = 128
SUPER = 256
HEAD_PAIR_DK = 128
EXPERT_ROWS = 256
EXPERT_VMEM_BYTES = 56 * 1024 * 1024


def _dot(a, b):
    return jnp.dot(a, b, preferred_element_type=F32)


def _dot_nt(a, b):
    return lax.dot_general(a, b, (((1,), (1,)), ((), ())), preferred_element_type=F32)


def _dot_tn(a, b):
    return lax.dot_general(a, b, (((0,), (0,)), ((), ())), preferred_element_type=F32)


def _split_bf16(x):
    hi = x.astype(BF16)
    lo = (x - hi.astype(F32)).astype(BF16)
    return hi, lo


def _rmsnorm(x, w):
    var = jnp.mean(x * x, axis=-1, keepdims=True)
    return x * lax.rsqrt(var + EPS) * w


def _mod_kernel(c_ref, w_ref, b_ref, o_ref):
    c = c_ref[...]
    s = c * jax.nn.sigmoid(c)
    o_ref[...] = jnp.dot(s, w_ref[...], precision=lax.Precision.HIGHEST,
                         preferred_element_type=F32) + b_ref[...]


def _mod_call(cc, w_ada, b_ada):
    rows, d = cc.shape
    n = w_ada.shape[1]
    tn = 1024
    return pl.pallas_call(
        _mod_kernel,
        grid=(n // tn,),
        in_specs=[pl.BlockSpec((rows, d), lambda j: (0, 0)),
                  pl.BlockSpec((d, tn), lambda j: (0, j)),
                  pl.BlockSpec((1, tn), lambda j: (0, j))],
        out_specs=pl.BlockSpec((rows, tn), lambda j: (0, j)),
        out_shape=jax.ShapeDtypeStruct((rows, n), F32),
        name="mod",
    )(cc, w_ada, b_ada)


def _inproj_kernel(x_ref, mod_ref, nw_ref, w_ref, wgk_ref, bgk_ref,
                   q_ref, k_ref, v_ref, g_ref, p_ref, gk_ref, *, qk, gw, pw, dk):
    x = x_ref[0]
    m = mod_ref[0]
    hm = _rmsnorm(x, nw_ref[...]) * (1.0 + m[1:2]) + m[0:1]
    p = _dot(hm.astype(BF16), w_ref[...])
    o = 0
    q_ref[0] = (p[:, o:o + qk] * (dk ** -0.5)).astype(BF16); o += qk
    k_ref[0] = p[:, o:o + qk].astype(BF16); o += qk
    v_ref[0] = p[:, o:o + gw].astype(BF16); o += gw
    g_ref[0] = p[:, o:o + gw].astype(BF16); o += gw
    p_ref[0] = p[:, o:o + pw].astype(BF16); o += pw
    r = p[:, o:o + LANES]
    z = _dot(r.astype(BF16), wgk_ref[...]) + bgk_ref[...]
    gk_ref[0] = (jnp.minimum(z, 0.0) - jnp.log1p(jnp.exp(-jnp.abs(z)))) * (1.0 / GATE_NORMALIZER)


def _inproj_call(x, mod, nw, w, wgk, bgk, *, qk, gw, pw, dk, tm):
    b, t, d = x.shape
    n_in = w.shape[1]
    bs = lambda width: pl.BlockSpec((1, tm, width), lambda i, j: (i, j, 0))
    const = lambda shape: pl.BlockSpec(shape, lambda i, j: (0,) * len(shape))
    per_batch = mod.shape[0] > 1
    return pl.pallas_call(
        functools.partial(_inproj_kernel, qk=qk, gw=gw, pw=pw, dk=dk),
        grid=(b, t // tm),
        in_specs=[bs(d),
                  pl.BlockSpec((1, N_MOD, d), (lambda i, j: (i, 0, 0)) if per_batch else (lambda i, j: (0, 0, 0))),
                  const((1, d)), const((d, n_in)), const((LANES, 2 * qk)), const((1, 2 * qk))],
        out_specs=[bs(qk), bs(qk), bs(gw), bs(gw), bs(pw), bs(2 * qk)],
        out_shape=[jax.ShapeDtypeStruct((b, t, qk), BF16), jax.ShapeDtypeStruct((b, t, qk), BF16),
                   jax.ShapeDtypeStruct((b, t, gw), BF16), jax.ShapeDtypeStruct((b, t, gw), BF16),
                   jax.ShapeDtypeStruct((b, t, pw), BF16), jax.ShapeDtypeStruct((b, t, 2 * qk), F32)],
        compiler_params=pltpu.CompilerParams(dimension_semantics=("arbitrary", "arbitrary")),
        name="inproj",
    )(x, mod, nw, w, wgk, bgk)


def _gla_super(q, k, v, gk, cm, amask, bd_mask, st_ref, chunk_order, want_out):
    n = SUPER
    hi, lo = _split_bf16(gk)
    cs = _dot(cm, hi) + _dot(cm, lo)
    bcum, blast, bmid = cs[0:n], cs[n:2 * n], cs[2 * n:3 * n]
    kp = (k * jnp.exp(blast - bcum)).astype(BF16)
    if want_out:
        qt = q * jnp.exp(bcum - bmid)
        kt = (k * jnp.exp(bmid - bcum)).astype(BF16)
        qh = (q * jnp.exp(bcum)).astype(BF16)
        lane = lax.broadcasted_iota(I32, qt.shape, 1)
        half = HEAD_PAIR_DK // 2
        o_heads = []
        for hh in range(2):
            sel = (lane < half) if hh == 0 else (lane >= half)
            a = _dot_nt(jnp.where(sel, qt, 0.0).astype(BF16), kt)
            a = jnp.where(amask, a, 0.0).astype(BF16)
            o_heads.append(_dot(a, v[:, hh * LANES:(hh + 1) * LANES]))
        o_intra = jnp.concatenate(o_heads, axis=1)
    outs = [None] * (n // GLA_CHUNK)
    for c in chunk_order:
        r0, r1 = c * GLA_CHUNK, (c + 1) * GLA_CHUNK
        st = st_ref[...]
        if want_out:
            outs[c] = _dot_nt(qh[r0:r1], st.astype(BF16))
        ut = _dot_tn(v[r0:r1], kp[r0:r1])
        decay = jnp.exp(blast[r0:r0 + 1, :])
        st_ref[...] = st * decay + jnp.where(bd_mask, ut, 0.0)
    if want_out:
        return o_intra + jnp.concatenate(outs, axis=0)
    return None


def _gla_kernel(q_ref, k_ref, v_ref, gkf_ref, gkb_ref, g_ref, kc_ref, vc_ref, gkfc_ref, gkbc_ref,
                nw_ref, cmf_ref, cmb_ref, o_ref, st_ref, oacc_ref):
    t = q_ref.shape[1]
    tc = kc_ref.shape[1]
    nsc, nscc = t // SUPER, tc // SUPER
    nch = SUPER // GLA_CHUNK
    cmf = cmf_ref[...]
    cmb = cmb_ref[...]
    amask_f = cmf[0:SUPER] > 0
    amask_b = cmb[0:SUPER] > 0
    row = lax.broadcasted_iota(I32, (2 * LANES, HEAD_PAIR_DK), 0)
    lane = lax.broadcasted_iota(I32, (2 * LANES, HEAD_PAIR_DK), 1)
    bd_mask = (row < LANES) == (lane < HEAD_PAIR_DK // 2)
    fwd_order = tuple(range(nch))
    bwd_order = tuple(reversed(range(nch)))

    def ctx_state(gk_ref, cm, order, j):
        rows = pl.ds(j * SUPER, SUPER)
        _gla_super(None, kc_ref[0, rows, :].astype(F32), vc_ref[0, rows, :], gk_ref[0, rows, :],
                   cm, None, bd_mask, st_ref, order, False)

    def latent(gk_ref, cm, amask, order, r0):
        rows = pl.ds(r0, SUPER)
        return _gla_super(q_ref[0, rows, :].astype(F32), k_ref[0, rows, :].astype(F32),
                          v_ref[0, rows, :], gk_ref[0, rows, :], cm, amask, bd_mask, st_ref, order, True)

    st_ref[...] = jnp.zeros_like(st_ref)
    for j in range(nscc):
        ctx_state(gkfc_ref, cmf, fwd_order, j)

    def fwd_body(j, carry):
        r0 = pl.multiple_of(j * SUPER, SUPER)
        oacc_ref[pl.ds(r0, SUPER), :] = latent(gkf_ref, cmf, amask_f, fwd_order, r0)
        return carry

    lax.fori_loop(0, nsc, fwd_body, 0)

    st_ref[...] = jnp.zeros_like(st_ref)
    for j in reversed(range(nscc)):
        ctx_state(gkbc_ref, cmb, bwd_order, j)

    nw = nw_ref[...]

    def bwd_body(jj, carry):
        r0 = pl.multiple_of((nsc - 1 - jj) * SUPER, SUPER)
        o = oacc_ref[pl.ds(r0, SUPER), :] + latent(gkb_ref, cmb, amask_b, bwd_order, r0)
        g = g_ref[0, pl.ds(r0, SUPER), :].astype(F32)
        gate = g * jax.nn.sigmoid(g)
        for hh in range(2):
            oh = o[:, hh * LANES:(hh + 1) * LANES]
            on = oh * lax.rsqrt(jnp.mean(oh * oh, axis=-1, keepdims=True) + EPS) * nw
            o_ref[0, pl.ds(r0, SUPER), hh * LANES:(hh + 1) * LANES] = (
                on * gate[:, hh * LANES:(hh + 1) * LANES]).astype(BF16)
        return carry

    lax.fori_loop(0, nsc, bwd_body, 0)


def _gla_masks():
    i = np.arange(SUPER)
    same = (i[:, None] // GLA_CHUNK) == (i[None, :] // GLA_CHUNK)
    pos = i[None, :] % GLA_CHUNK
    mid = GLA_CHUNK // 2
    fwd = np.concatenate([same & (i[None, :] <= i[:, None]), same, same & (pos <= mid - 1)], axis=0)
    bwd = np.concatenate([same & (i[None, :] >= i[:, None]), same, same & (pos >= mid)], axis=0)
    return jnp.asarray(fwd, BF16), jnp.asarray(bwd, BF16)


def _gla_call(q, k, v, gk, g, kc, vc, gkc, nw):
    b, t, qk = q.shape
    tc = kc.shape[1]
    npair = qk // HEAD_PAIR_DK
    cmf, cmb = _gla_masks()
    lat = lambda width, off: pl.BlockSpec((1, t, width), lambda i, j: (i, 0, j + off))
    ctx = lambda width, off: pl.BlockSpec((1, tc, width), lambda i, j: (i, 0, j + off))
    const = lambda shape: pl.BlockSpec(shape, lambda i, j: (0,) * len(shape))
    return pl.pallas_call(
        _gla_kernel,
        grid=(b, npair),
        in_specs=[lat(HEAD_PAIR_DK, 0), lat(HEAD_PAIR_DK, 0), lat(2 * LANES, 0),
                  lat(HEAD_PAIR_DK, 0), lat(HEAD_PAIR_DK, npair), lat(2 * LANES, 0),
                  ctx(HEAD_PAIR_DK, 0), ctx(2 * LANES, 0), ctx(HEAD_PAIR_DK, 0), ctx(HEAD_PAIR_DK, npair),
                  const((1, LANES)), const(cmf.shape), const(cmb.shape)],
        out_specs=lat(2 * LANES, 0),
        out_shape=jax.ShapeDtypeStruct((b, t, v.shape[2]), BF16),
        scratch_shapes=[pltpu.VMEM((2 * LANES, HEAD_PAIR_DK), F32), pltpu.VMEM((t, 2 * LANES), F32)],
        compiler_params=pltpu.CompilerParams(dimension_semantics=("arbitrary", "arbitrary")),
        name="gla",
    )(q, k, v, gk, gk, g, kc, vc, gkc, gkc, nw, cmf, cmb)


def _pool_kernel(x_ref, cm_ref, wp_ref, ps_ref, o_ref, y_ref, z_ref, *, rows):
    t = x_ref.shape[1]
    tok = lax.broadcasted_iota(I32, (t, LANES), 0)
    r = tok // GRID_W
    c = tok % GRID_W
    for gi, w in enumerate(POOL_WINDOWS):
        lo = w // 2
        hi = w - 1 - lo
        cols = slice(gi * LANES, (gi + 1) * LANES)
        cmat = cm_ref[gi]
        for blk in range(t // SUPER):
            rs = slice(blk * SUPER, (blk + 1) * SUPER)
            y_ref[rs, :] = _dot(cmat, x_ref[0, rs, cols])
        z_ref[...] = y_ref[...]
        for dr in range(-lo, hi + 1):
            sh = abs(dr) * GRID_W
            if dr == 0 or sh >= t:
                continue
            if dr > 0:
                z_ref[0:t - sh, :] += y_ref[sh:t, :]
            else:
                z_ref[sh:t, :] += y_ref[0:t - sh, :]
        cnt_r = jnp.minimum(r + hi + 1, rows) - jnp.maximum(r - lo, 0)
        cnt_c = jnp.minimum(c + hi + 1, GRID_W) - jnp.maximum(c - lo, 0)
        cnt = (cnt_r * cnt_c).astype(F32)
        pooled = z_ref[...] / cnt - x_ref[0, :, cols].astype(F32)
        yp = _dot(pooled.astype(BF16), wp_ref[gi]) * ps_ref[:, cols]
        o_ref[0, :, cols] = yp.astype(BF16)


def _pool_col_mats():
    i = np.arange(SUPER)
    same_row = (i[:, None] // GRID_W) == (i[None, :] // GRID_W)
    d = i[None, :] - i[:, None]
    mats = []
    for w in POOL_WINDOWS:
        lo = w // 2
        hi = w - 1 - lo
        mats.append(same_row & (d >= -lo) & (d <= hi))
    return jnp.asarray(np.stack(mats), BF16)


def _pool_call(xp, w_pool, pool_scale):
    b, t, pw = xp.shape
    ng = len(POOL_WINDOWS)
    cm = _pool_col_mats()
    const = lambda shape: pl.BlockSpec(shape, lambda i: (0,) * len(shape))
    return pl.pallas_call(
        functools.partial(_pool_kernel, rows=t // GRID_W),
        grid=(b,),
        in_specs=[pl.BlockSpec((1, t, pw), lambda i: (i, 0, 0)),
                  const(cm.shape), const((ng, LANES, LANES)), const((1, pw))],
        out_specs=pl.BlockSpec((1, t, pw), lambda i: (i, 0, 0)),
        out_shape=jax.ShapeDtypeStruct((b, t, pw), BF16),
        scratch_shapes=[pltpu.VMEM((t, LANES), F32), pltpu.VMEM((t, LANES), F32)],
        compiler_params=pltpu.CompilerParams(dimension_semantics=("arbitrary",)),
        name="pool",
    )(xp, cm, w_pool, pool_scale)


def _route_kernel(gla_ref, pool_ref, x_ref, mod_ref, wo_ref, nw_ref, wr_ref, br_ref, lt_ref,
                  h_ref, xt_ref, e_ref, rk_ref, wt_ref, cnt_ref, run_ref, *, gw, n_exp):
    i = pl.program_id(0)

    @pl.when(i == 0)
    def _():
        run_ref[...] = jnp.zeros_like(run_ref)

    m = mod_ref[0]
    acc = _dot(gla_ref[...], wo_ref[0:gw, :]) + _dot(pool_ref[...], wo_ref[gw:, :])
    h = x_ref[...] + m[2:3] * acc
    h_ref[...] = h
    xt = _rmsnorm(h, nw_ref[...]) * (1.0 + m[4:5]) + m[3:4]
    xt_ref[...] = xt
    xh, xl = _split_bf16(xt)
    wr = wr_ref[...]
    wh, wl = _split_bf16(wr)
    logits = _dot(xh, wh) + _dot(xl, wh) + _dot(xh, wl) + br_ref[...]
    lane = lax.broadcasted_iota(I32, logits.shape, 1)
    neg = jnp.float32(-jnp.inf)
    logits = jnp.where(lane < n_exp, logits, neg)
    vals, hots = [], []
    e_out = jnp.zeros(logits.shape, I32)
    for j in range(TOP_K):
        mx = jnp.max(logits, axis=-1, keepdims=True)
        idx = jnp.min(jnp.where(logits == mx, lane, LANES), axis=-1, keepdims=True)
        hot = lane == idx
        vals.append(mx)
        hots.append(hot)
        e_out = jnp.where(lane == j, idx, e_out)
        logits = jnp.where(hot, neg, logits)
    ex = [jnp.exp(v - vals[0]) for v in vals]
    den = ex[0] + ex[1] + ex[2] + ex[3]
    w_out = jnp.zeros(logits.shape, F32)
    for j in range(TOP_K):
        w_out = jnp.where(lane == j, ex[j] / den, w_out)
    osum = jnp.where(hots[0] | hots[1] | hots[2] | hots[3], 1.0, 0.0)
    before = _dot(lt_ref[...], osum.astype(BF16)) + run_ref[0:1, :]
    rk_out = jnp.zeros(logits.shape, I32)
    for j in range(TOP_K):
        rj = jnp.sum(jnp.where(hots[j], before, 0.0), axis=-1, keepdims=True)
        rk_out = jnp.where(lane == j, rj.astype(I32), rk_out)
    run = run_ref[0:1, :] + jnp.sum(osum, axis=0, keepdims=True)
    run_ref[...] = jnp.broadcast_to(run, run_ref.shape)
    e_ref[...] = e_out
    rk_ref[...] = rk_out
    wt_ref[...] = w_out
    cnt_ref[...] = jnp.broadcast_to(run, cnt_ref.shape).astype(I32)


def _route_call(gla, pool, x, mod, w_out, nw, wr, br, *, tm, n_exp):
    n, d = x.shape
    gw = gla.shape[1]
    t_per_b = n // mod.shape[0]
    lt = jnp.asarray(np.tril(np.ones((tm, tm), np.float32), -1), BF16)
    row = lambda width: pl.BlockSpec((tm, width), lambda i: (i, 0))
    const = lambda shape: pl.BlockSpec(shape, lambda i: (0,) * len(shape))
    sds = jax.ShapeDtypeStruct
    return pl.pallas_call(
        functools.partial(_route_kernel, gw=gw, n_exp=n_exp),
        grid=(n // tm,),
        in_specs=[row(gw), row(pool.shape[1]), row(d),
                  pl.BlockSpec((1, N_MOD, d), lambda i: (i * tm // t_per_b, 0, 0)),
                  const(w_out.shape), const((1, d)), const(wr.shape), const((1, LANES)), const((tm, tm))],
        out_specs=[row(d), row(d), row(LANES), row(LANES), row(LANES), const((8, LANES))],
        out_shape=[sds((n, d), F32), sds((n, d), F32), sds((n, LANES), I32), sds((n, LANES), I32),
                   sds((n, LANES), F32), sds((8, LANES), I32)],
        scratch_shapes=[pltpu.VMEM((8, LANES), F32)],
        compiler_params=pltpu.CompilerParams(dimension_semantics=("arbitrary",)),
        name="route",
    )(gla, pool, x, mod, w_out, nw, wr, br, lt)


def _row_copy(src, s, dst, d, sem):
    return pltpu.make_async_copy(src.at[pl.ds(s, 1), :], dst.at[pl.ds(d, 1), :], sem)


def _wait_rows(hbm_ref, n_rows, sem):
    pltpu.make_async_copy(hbm_ref.at[pl.ds(0, n_rows), :], hbm_ref.at[pl.ds(0, n_rows), :], sem).wait()


DISPATCH_SLOTS = 3


def _dispatch_kernel(dest_ref, padlo_ref, padhi_ref, xt_ref, xs_ref, zero_ref, xbuf_ref, in_sem, out_sem, zsem,
                     *, tq, n_exp):
    i = pl.program_id(0)
    n = pl.num_programs(0)
    slot = i % DISPATCH_SLOTS

    def fetch(step, s):
        return pltpu.make_async_copy(xt_ref.at[pl.ds(step * tq, tq), :], xbuf_ref.at[s], in_sem.at[s])

    @pl.when(i == 0)
    def _():
        fetch(0, 0).start()

    @pl.when(i >= 2)
    def _():
        _wait_rows(xs_ref, tq * TOP_K, out_sem.at[(i + 1) % DISPATCH_SLOTS])

    @pl.when(i + 1 < n)
    def _():
        fetch(i + 1, (i + 1) % DISPATCH_SLOTS).start()

    @pl.when(i == 0)
    def _():
        zero_ref[...] = jnp.zeros_like(zero_ref)

        def per_expert(e, carry):
            def per_row(r, c2):
                cp = _row_copy(zero_ref, 0, xs_ref, r, zsem)
                cp.start()
                cp.wait()
                return c2
            return lax.fori_loop(padlo_ref[e], padhi_ref[e], per_row, carry)

        lax.fori_loop(0, n_exp, per_expert, 0)

    fetch(i, slot).wait()
    src = xbuf_ref.at[slot]

    def issue(tl, carry):
        for j in range(TOP_K):
            _row_copy(src, tl, xs_ref, dest_ref[tl * TOP_K + j], out_sem.at[slot]).start()
        return carry

    lax.fori_loop(0, tq, issue, 0)

    @pl.when(i == n - 1)
    def _():
        @pl.when(i >= 1)
        def _():
            _wait_rows(xs_ref, tq * TOP_K, out_sem.at[(i - 1) % DISPATCH_SLOTS])
        _wait_rows(xs_ref, tq * TOP_K, out_sem.at[slot])


def _dispatch_call(dest_flat, padlo, padhi, xt, *, n_pad, tq, n_exp):
    n, d = xt.shape
    smem = pltpu.SMEM
    return pl.pallas_call(
        functools.partial(_dispatch_kernel, tq=tq, n_exp=n_exp),
        grid=(n // tq,),
        in_specs=[pl.BlockSpec((tq * TOP_K,), lambda i: (i,), memory_space=smem),
                  pl.BlockSpec(memory_space=smem), pl.BlockSpec(memory_space=smem),
                  pl.BlockSpec(memory_space=pl.ANY)],
        out_specs=pl.BlockSpec(memory_space=pl.ANY),
        out_shape=jax.ShapeDtypeStruct((n_pad, d), xt.dtype),
        scratch_shapes=[pltpu.VMEM((8, d), xt.dtype), pltpu.VMEM((DISPATCH_SLOTS, tq, d), xt.dtype),
                        pltpu.SemaphoreType.DMA((DISPATCH_SLOTS,)), pltpu.SemaphoreType.DMA((DISPATCH_SLOTS,)),
                        pltpu.SemaphoreType.DMA(())],
        compiler_params=pltpu.CompilerParams(dimension_semantics=("arbitrary",)),
        name="dispatch",
    )(dest_flat, padlo, padhi, xt)


def _expert_kernel(be_ref, nu_ref, x_ref, wgu_ref, bgu_ref, wd_ref, bd_ref, y_ref, wgu_bf_ref, wd_bf_ref, *, d_ff):
    i = pl.program_id(0)

    @pl.when((i == 0) | (be_ref[i] != be_ref[jnp.maximum(i - 1, 0)]))
    def _():
        wgu_bf_ref[...] = wgu_ref[0].astype(BF16)
        wd_bf_ref[...] = wd_ref[0].astype(BF16)

    @pl.when(i < nu_ref[0])
    def _():
        gu = _dot(x_ref[...].astype(BF16), wgu_bf_ref[...]) + bgu_ref[0]
        gate = jnp.minimum(gu[:, :d_ff], SWIGLU_LIMIT)
        up = jnp.clip(gu[:, d_ff:], -SWIGLU_LIMIT, SWIGLU_LIMIT)
        act = (up + 1.0) * gate * jax.nn.sigmoid(SWIGLU_ALPHA * gate)
        y_ref[...] = _dot(act.astype(BF16), wd_bf_ref[...]) + bd_ref[0]


def _expert_call(block_e, n_used, xs, w_gu, b_gu, w_down, b_down):
    n_pad, d = xs.shape
    n_exp, _, two_ff = w_gu.shape
    d_ff = two_ff // 2
    nblk = n_pad // EXPERT_ROWS
    rows = lambda i, be, nu: (jnp.minimum(i, nu[0] - 1), 0)
    per_e = lambda i, be, nu: (be[i], 0, 0)
    grid_spec = pltpu.PrefetchScalarGridSpec(
        num_scalar_prefetch=2,
        grid=(nblk,),
        in_specs=[pl.BlockSpec((EXPERT_ROWS, d), rows),
                  pl.BlockSpec((1, d, two_ff), per_e), pl.BlockSpec((1, 1, two_ff), per_e),
                  pl.BlockSpec((1, d_ff, d), per_e), pl.BlockSpec((1, 1, d), per_e)],
        out_specs=pl.BlockSpec((EXPERT_ROWS, d), rows),
        scratch_shapes=[pltpu.VMEM((d, two_ff), BF16), pltpu.VMEM((d_ff, d), BF16)],
    )
    return pl.pallas_call(
        functools.partial(_expert_kernel, d_ff=d_ff),
        grid_spec=grid_spec,
        out_shape=jax.ShapeDtypeStruct((n_pad, d), F32),
        compiler_params=pltpu.CompilerParams(dimension_semantics=("arbitrary",),
                                             vmem_limit_bytes=EXPERT_VMEM_BYTES),
        name="experts",
    )(block_e, n_used, xs, w_gu, b_gu.reshape(n_exp, 1, two_ff), w_down, b_down.reshape(n_exp, 1, d))


def _combine_kernel(dest_ref, dnext_ref, ys_ref, wt_ref, h_ref, mod_ref, fw_ref, o_ref, buf_ref, sem, *, tg):
    i = pl.program_id(0)
    slot = i % 2

    def gather(d_ref, s):
        def issue(tl, carry):
            for j in range(TOP_K):
                pltpu.make_async_copy(ys_ref.at[pl.ds(d_ref[tl * TOP_K + j], 1), :],
                                      buf_ref.at[s, j, pl.ds(tl, 1), :], sem.at[s]).start()
            return carry

        lax.fori_loop(0, tg, issue, 0)

    @pl.when(i == 0)
    def _():
        gather(dest_ref, 0)

    @pl.when(i + 1 < pl.num_programs(0))
    def _():
        gather(dnext_ref, 1 - slot)

    _wait_rows(ys_ref, tg * TOP_K, sem.at[slot])

    wt = wt_ref[...]
    acc = wt[:, 0:1] * buf_ref[slot, 0]
    for j in range(1, TOP_K):
        acc = acc + wt[:, j:j + 1] * buf_ref[slot, j]
    m = mod_ref[0]
    o_ref[...] = _rmsnorm(h_ref[...] + m[5:6] * acc, fw_ref[...])


def _combine_call(dest_flat, ys, wts, h, mod, fw, *, tg):
    n, d = h.shape
    t_per_b = n // mod.shape[0]
    row = lambda width: pl.BlockSpec((tg, width), lambda i: (i, 0))
    last = n // tg - 1
    return pl.pallas_call(
        functools.partial(_combine_kernel, tg=tg),
        grid=(n // tg,),
        in_specs=[pl.BlockSpec((tg * TOP_K,), lambda i: (i,), memory_space=pltpu.SMEM),
                  pl.BlockSpec((tg * TOP_K,), lambda i: (jnp.minimum(i + 1, last),), memory_space=pltpu.SMEM),
                  pl.BlockSpec(memory_space=pl.ANY),
                  row(LANES), row(d),
                  pl.BlockSpec((1, N_MOD, d), lambda i: (i * tg // t_per_b, 0, 0)),
                  pl.BlockSpec((1, d), lambda i: (0, 0))],
        out_specs=row(d),
        out_shape=jax.ShapeDtypeStruct((n, d), F32),
        scratch_shapes=[pltpu.VMEM((2, TOP_K, tg, d), ys.dtype), pltpu.SemaphoreType.DMA((2,))],
        compiler_params=pltpu.CompilerParams(dimension_semantics=("arbitrary",)),
        name="combine",
    )(dest_flat, dest_flat, ys, wts, h, mod, fw)


def kernel(x, c, ctx, c_ctx, w_ada, b_ada, norm_mix_w, norm_mlp_w, w_in, w_gk_f, b_gk_f, w_gk_b, b_gk_b,
           gla_norm_w, w_pool, pool_scale, w_out, w_router, b_router, w_gu, b_gu, w_down, b_down,
           final_norm_w):
    b, t, d = x.shape
    assert w_ada.shape[0] == 1, "single-layer trunk"
    n_exp = w_router.shape[2]
    rank = w_gk_f.shape[1]
    qk = w_gk_f.shape[2]
    dk = qk // GLA_HEADS
    gw = GLA_HEADS * gla_norm_w.shape[1]
    pw = w_pool.shape[1] * w_pool.shape[2]
    assert w_in.shape[2] == 2 * qk + 2 * gw + 2 * rank + pw and 2 * rank <= LANES
    assert t % SUPER == 0 and ctx.shape[1] % SUPER == 0 and n_exp <= LANES

    rows = -(-(b + 1) // 8) * 8
    cc = jnp.zeros((rows, d), F32).at[:b].set(c).at[b].set(c_ctx)
    mod = _mod_call(cc, w_ada[0], b_ada)
    mod_x = mod[:b].reshape(b, N_MOD, d)
    mod_c = mod[b:b + 1].reshape(1, N_MOD, d)

    wi = w_in[0]
    o_r = 2 * qk + 2 * gw
    w_cat = jnp.concatenate([wi[:, :o_r], wi[:, o_r + 2 * rank:], wi[:, o_r:o_r + 2 * rank],
                             jnp.zeros((d, LANES - 2 * rank), F32)], axis=1).astype(BF16)
    wgk = jnp.zeros((LANES, 2 * qk), F32).at[:rank, :qk].set(w_gk_f[0]).at[rank:2 * rank, qk:].set(w_gk_b[0])
    bgk = jnp.concatenate([b_gk_f[0], b_gk_b[0]])[None, :]
    proj = functools.partial(_inproj_call, nw=norm_mix_w, w=w_cat, wgk=wgk.astype(BF16), bgk=bgk,
                             qk=qk, gw=gw, pw=pw, dk=dk)
    q, k, v, g, xp, gk = proj(x, mod_x, tm=512)
    _, kc, vc, _, _, gkc = proj(ctx, mod_c, tm=SUPER)

    gla = _gla_call(q, k, v, gk, g, kc, vc, gkc, gla_norm_w)
    pool = _pool_call(xp, w_pool[0].astype(BF16), pool_scale)

    n = b * t
    wr = jnp.zeros((d, LANES), F32).at[:, :n_exp].set(w_router[0])
    br = jnp.zeros((1, LANES), F32).at[0, :n_exp].set(b_router[0])
    h, xt, e_idx, rk, wts, cnt = _route_call(
        gla.reshape(n, gw), pool.reshape(n, pw), x.reshape(n, d), mod_x, w_out[0].astype(BF16),
        norm_mlp_w, wr, br, tm=512, n_exp=n_exp)

    counts = cnt[0, :n_exp]
    padded = (counts + EXPERT_ROWS - 1) // EXPERT_ROWS * EXPERT_ROWS
    pad_end = jnp.cumsum(padded)
    start_pad = pad_end - padded
    dest = (jnp.take(start_pad, e_idx[:, :TOP_K]) + rk[:, :TOP_K]).reshape(-1).astype(I32)
    n_pad = n * TOP_K + n_exp * EXPERT_ROWS
    nblk = n_pad // EXPERT_ROWS
    n_used = (pad_end[-1] // EXPERT_ROWS).astype(I32)
    blk = jnp.arange(nblk, dtype=I32)
    block_e = jnp.sum((pad_end[None, :] <= (blk * EXPERT_ROWS)[:, None]).astype(I32), axis=1)
    block_e = jnp.minimum(block_e, n_exp - 1)
    block_e = jnp.where(blk < n_used, block_e, block_e[jnp.maximum(n_used - 1, 0)])

    xs = _dispatch_call(dest, (start_pad + counts).astype(I32), pad_end.astype(I32), xt,
                        n_pad=n_pad, tq=512, n_exp=n_exp)
    ys = _expert_call(block_e, n_used.reshape(1), xs, w_gu[0], b_gu[0], w_down[0], b_down[0])
    out = _combine_call(dest, ys, wts, h, mod_x, final_norm_w[None, :], tg=256)
    return out.reshape(b, t, d)
```

```python
import functools

import numpy as np
import jax
import jax.numpy as jnp
from jax import lax
from jax.experimental import pallas as pl
from jax.experimental.pallas import tpu as pltpu
from jax.experimental.pallas import tpu_sc as plsc

F32 = jnp.float32
BF16 = jnp.bfloat16
I32 = jnp.int32

GRID_W = 64
GLA_HEADS = 4
GLA_CHUNK = 64
GATE_NORMALIZER = 16.0
POOL_WINDOWS = (2, 4, 8, 16)
TOP_K = 4
SWIGLU_LIMIT = 7.0
SWIGLU_ALPHA = 1.702
N_MOD = 6
EPS = 1e-6

LANES = 128
SUPER = 256
HEAD_PAIR_DK = 128
EXPERT_ROWS = 256
EXPERT_VMEM_BYTES = 56 * 1024 * 1024
SC_CORES = 2
SC_SUBCORES = 16
SC_WORKERS = SC_CORES * SC_SUBCORES
SC_WINDOW = 32


def _dot(a, b):
    return jnp.dot(a, b, preferred_element_type=F32)


def _dot_nt(a, b):
    return lax.dot_general(a, b, (((1,), (1,)), ((), ())), preferred_element_type=F32)


def _dot_tn(a, b):
    return lax.dot_general(a, b, (((0,), (0,)), ((), ())), preferred_element_type=F32)


def _split_bf16(x):
    hi = x.astype(BF16)
    lo = (x - hi.astype(F32)).astype(BF16)
    return hi, lo


def _rmsnorm(x, w):
    var = jnp.mean(x * x, axis=-1, keepdims=True)
    return x * lax.rsqrt(var + EPS) * w


def _mod_kernel(c_ref, w_ref, b_ref, o_ref):
    c = c_ref[...]
    s = c * jax.nn.sigmoid(c)
    o_ref[...] = jnp.dot(s, w_ref[...], precision=lax.Precision.HIGHEST,
                         preferred_element_type=F32) + b_ref[...]


def _mod_call(cc, w_ada, b_ada):
    rows, d = cc.shape
    n = w_ada.shape[1]
    tn = 1024
    return pl.pallas_call(
        _mod_kernel,
        grid=(n // tn,),
        in_specs=[pl.BlockSpec((rows, d), lambda j: (0, 0)),
                  pl.BlockSpec((d, tn), lambda j: (0, j)),
                  pl.BlockSpec((1, tn), lambda j: (0, j))],
        out_specs=pl.BlockSpec((rows, tn), lambda j: (0, j)),
        out_shape=jax.ShapeDtypeStruct((rows, n), F32),
        name="mod",
    )(cc, w_ada, b_ada)


def _inproj_kernel(x_ref, mod_ref, nw_ref, w_ref, wgk_ref, bgk_ref,
                   q_ref, k_ref, v_ref, g_ref, p_ref, gk_ref, *, qk, gw, pw, dk):
    x = x_ref[0]
    m = mod_ref[0]
    hm = _rmsnorm(x, nw_ref[...]) * (1.0 + m[1:2]) + m[0:1]
    p = _dot(hm.astype(BF16), w_ref[...])
    o = 0
    q_ref[0] = (p[:, o:o + qk] * (dk ** -0.5)).astype(BF16); o += qk
    k_ref[0] = p[:, o:o + qk].astype(BF16); o += qk
    v_ref[0] = p[:, o:o + gw].astype(BF16); o += gw
    g_ref[0] = p[:, o:o + gw].astype(BF16); o += gw
    p_ref[0] = p[:, o:o + pw].astype(BF16); o += pw
    r = p[:, o:o + LANES]
    z = _dot(r.astype(BF16), wgk_ref[...]) + bgk_ref[...]
    gk_ref[0] = (jnp.minimum(z, 0.0) - jnp.log1p(jnp.exp(-jnp.abs(z)))) * (1.0 / GATE_NORMALIZER)


def _inproj_call(x, mod, nw, w, wgk, bgk, *, qk, gw, pw, dk, tm):
    b, t, d = x.shape
    n_in = w.shape[1]
    bs = lambda width: pl.BlockSpec((1, tm, width), lambda i, j: (i, j, 0))
    const = lambda shape: pl.BlockSpec(shape, lambda i, j: (0,) * len(shape))
    per_batch = mod.shape[0] > 1
    return pl.pallas_call(
        functools.partial(_inproj_kernel, qk=qk, gw=gw, pw=pw, dk=dk),
        grid=(b, t // tm),
        in_specs=[bs(d),
                  pl.BlockSpec((1, N_MOD, d), (lambda i, j: (i, 0, 0)) if per_batch else (lambda i, j: (0, 0, 0))),
                  const((1, d)), const((d, n_in)), const((LANES, 2 * qk)), const((1, 2 * qk))],
        out_specs=[bs(qk), bs(qk), bs(gw), bs(gw), bs(pw), bs(2 * qk)],
        out_shape=[jax.ShapeDtypeStruct((b, t, qk), BF16), jax.ShapeDtypeStruct((b, t, qk), BF16),
                   jax.ShapeDtypeStruct((b, t, gw), BF16), jax.ShapeDtypeStruct((b, t, gw), BF16),
                   jax.ShapeDtypeStruct((b, t, pw), BF16), jax.ShapeDtypeStruct((b, t, 2 * qk), F32)],
        compiler_params=pltpu.CompilerParams(dimension_semantics=("arbitrary", "arbitrary")),
        name="inproj",
    )(x, mod, nw, w, wgk, bgk)


def _gla_super(q, k, v, gk, cm, amask, bd_mask, st_ref, chunk_order, want_out):
    n = SUPER
    hi, lo = _split_bf16(gk)
    cs = _dot(cm, hi) + _dot(cm, lo)
    bcum, blast, bmid = cs[0:n], cs[n:2 * n], cs[2 * n:3 * n]
    kp = (k * jnp.exp(blast - bcum)).astype(BF16)
    if want_out:
        qt = q * jnp.exp(bcum - bmid)
        kt = (k * jnp.exp(bmid - bcum)).astype(BF16)
        qh = (q * jnp.exp(bcum)).astype(BF16)
        lane = lax.broadcasted_iota(I32, qt.shape, 1)
        half = HEAD_PAIR_DK // 2
        o_heads = []
        for hh in range(2):
            sel = (lane < half) if hh == 0 else (lane >= half)
            a = _dot_nt(jnp.where(sel, qt, 0.0).astype(BF16), kt)
            a = jnp.where(amask, a, 0.0).astype(BF16)
            o_heads.append(_dot(a, v[:, hh * LANES:(hh + 1) * LANES]))
        o_intra = jnp.concatenate(o_heads, axis=1)
    outs = [None] * (n // GLA_CHUNK)
    for c in chunk_order:
        r0, r1 = c * GLA_CHUNK, (c + 1) * GLA_CHUNK
        st = st_ref[...]
        if want_out:
            outs[c] = _dot_nt(qh[r0:r1], st.astype(BF16))
        ut = _dot_tn(v[r0:r1], kp[r0:r1])
        decay = jnp.exp(blast[r0:r0 + 1, :])
        st_ref[...] = st * decay + jnp.where(bd_mask, ut, 0.0)
    if want_out:
        return o_intra + jnp.concatenate(outs, axis=0)
    return None


def _gla_kernel(q_ref, k_ref, v_ref, gkf_ref, gkb_ref, g_ref, kc_ref, vc_ref, gkfc_ref, gkbc_ref,
                nw_ref, cmf_ref, cmb_ref, o_ref, st_ref, oacc_ref):
    t = q_ref.shape[1]
    tc = kc_ref.shape[1]
    nsc, nscc = t // SUPER, tc // SUPER
    nch = SUPER // GLA_CHUNK
    cmf = cmf_ref[...]
    cmb = cmb_ref[...]
    amask_f = cmf[0:SUPER] > 0
    amask_b = cmb[0:SUPER] > 0
    row = lax.broadcasted_iota(I32, (2 * LANES, HEAD_PAIR_DK), 0)
    lane = lax.broadcasted_iota(I32, (2 * LANES, HEAD_PAIR_DK), 1)
    bd_mask = (row < LANES) == (lane < HEAD_PAIR_DK // 2)
    fwd_order = tuple(range(nch))
    bwd_order = tuple(reversed(range(nch)))

    def ctx_state(gk_ref, cm, order, j):
        rows = pl.ds(j * SUPER, SUPER)
        _gla_super(None, kc_ref[0, rows, :].astype(F32), vc_ref[0, rows, :], gk_ref[0, rows, :],
                   cm, None, bd_mask, st_ref, order, False)

    def latent(gk_ref, cm, amask, order, r0):
        rows = pl.ds(r0, SUPER)
        return _gla_super(q_ref[0, rows, :].astype(F32), k_ref[0, rows, :].astype(F32),
                          v_ref[0, rows, :], gk_ref[0, rows, :], cm, amask, bd_mask, st_ref, order, True)

    st_ref[...] = jnp.zeros_like(st_ref)
    for j in range(nscc):
        ctx_state(gkfc_ref, cmf, fwd_order, j)

    def fwd_body(j, carry):
        r0 = pl.multiple_of(j * SUPER, SUPER)
        oacc_ref[pl.ds(r0, SUPER), :] = latent(gkf_ref, cmf, amask_f, fwd_order, r0)
        return carry

    lax.fori_loop(0, nsc, fwd_body, 0)

    st_ref[...] = jnp.zeros_like(st_ref)
    for j in reversed(range(nscc)):
        ctx_state(gkbc_ref, cmb, bwd_order, j)

    nw = nw_ref[...]

    def bwd_body(jj, carry):
        r0 = pl.multiple_of((nsc - 1 - jj) * SUPER, SUPER)
        o = oacc_ref[pl.ds(r0, SUPER), :] + latent(gkb_ref, cmb, amask_b, bwd_order, r0)
        g = g_ref[0, pl.ds(r0, SUPER), :].astype(F32)
        gate = g * jax.nn.sigmoid(g)
        for hh in range(2):
            oh = o[:, hh * LANES:(hh + 1) * LANES]
            on = oh * lax.rsqrt(jnp.mean(oh * oh, axis=-1, keepdims=True) + EPS) * nw
            o_ref[0, pl.ds(r0, SUPER), hh * LANES:(hh + 1) * LANES] = (
                on * gate[:, hh * LANES:(hh + 1) * LANES]).astype(BF16)
        return carry

    lax.fori_loop(0, nsc, bwd_body, 0)


def _gla_masks():
    i = np.arange(SUPER)
    same = (i[:, None] // GLA_CHUNK) == (i[None, :] // GLA_CHUNK)
    pos = i[None, :] % GLA_CHUNK
    mid = GLA_CHUNK // 2
    fwd = np.concatenate([same & (i[None, :] <= i[:, None]), same, same & (pos <= mid - 1)], axis=0)
    bwd = np.concatenate([same & (i[None, :] >= i[:, None]), same, same & (pos >= mid)], axis=0)
    return jnp.asarray(fwd, BF16), jnp.asarray(bwd, BF16)


def _gla_call(q, k, v, gk, g, kc, vc, gkc, nw):
    b, t, qk = q.shape
    tc = kc.shape[1]
    npair = qk // HEAD_PAIR_DK
    cmf, cmb = _gla_masks()
    lat = lambda width, off: pl.BlockSpec((1, t, width), lambda i, j: (i, 0, j + off))
    ctx = lambda width, off: pl.BlockSpec((1, tc, width), lambda i, j: (i, 0, j + off))
    const = lambda shape: pl.BlockSpec(shape, lambda i, j: (0,) * len(shape))
    return pl.pallas_call(
        _gla_kernel,
        grid=(b, npair),
        in_specs=[lat(HEAD_PAIR_DK, 0), lat(HEAD_PAIR_DK, 0), lat(2 * LANES, 0),
                  lat(HEAD_PAIR_DK, 0), lat(HEAD_PAIR_DK, npair), lat(2 * LANES, 0),
                  ctx(HEAD_PAIR_DK, 0), ctx(2 * LANES, 0), ctx(HEAD_PAIR_DK, 0), ctx(HEAD_PAIR_DK, npair),
                  const((1, LANES)), const(cmf.shape), const(cmb.shape)],
        out_specs=lat(2 * LANES, 0),
        out_shape=jax.ShapeDtypeStruct((b, t, v.shape[2]), BF16),
        scratch_shapes=[pltpu.VMEM((2 * LANES, HEAD_PAIR_DK), F32), pltpu.VMEM((t, 2 * LANES), F32)],
        compiler_params=pltpu.CompilerParams(dimension_semantics=("arbitrary", "arbitrary")),
        name="gla",
    )(q, k, v, gk, gk, g, kc, vc, gkc, gkc, nw, cmf, cmb)


def _pool_kernel(x_ref, cm_ref, wp_ref, ps_ref, o_ref, y_ref, z_ref, *, rows):
    t = x_ref.shape[1]
    tok = lax.broadcasted_iota(I32, (t, LANES), 0)
    r = tok // GRID_W
    c = tok % GRID_W
    for gi, w in enumerate(POOL_WINDOWS):
        lo = w // 2
        hi = w - 1 - lo
        cols = slice(gi * LANES, (gi + 1) * LANES)
        cmat = cm_ref[gi]
        for blk in range(t // SUPER):
            rs = slice(blk * SUPER, (blk + 1) * SUPER)
            y_ref[rs, :] = _dot(cmat, x_ref[0, rs, cols])
        z_ref[...] = y_ref[...]
        for dr in range(-lo, hi + 1):
            sh = abs(dr) * GRID_W
            if dr == 0 or sh >= t:
                continue
            if dr > 0:
                z_ref[0:t - sh, :] += y_ref[sh:t, :]
            else:
                z_ref[sh:t, :] += y_ref[0:t - sh, :]
        cnt_r = jnp.minimum(r + hi + 1, rows) - jnp.maximum(r - lo, 0)
        cnt_c = jnp.minimum(c + hi + 1, GRID_W) - jnp.maximum(c - lo, 0)
        cnt = (cnt_r * cnt_c).astype(F32)
        pooled = z_ref[...] / cnt - x_ref[0, :, cols].astype(F32)
        yp = _dot(pooled.astype(BF16), wp_ref[gi]) * ps_ref[:, cols]
        o_ref[0, :, cols] = yp.astype(BF16)


def _pool_col_mats():
    i = np.arange(SUPER)
    same_row = (i[:, None] // GRID_W) == (i[None, :] // GRID_W)
    d = i[None, :] - i[:, None]
    mats = []
    for w in POOL_WINDOWS:
        lo = w // 2
        hi = w - 1 - lo
        mats.append(same_row & (d >= -lo) & (d <= hi))
    return jnp.asarray(np.stack(mats), BF16)


def _pool_call(xp, w_pool, pool_scale):
    b, t, pw = xp.shape
    ng = len(POOL_WINDOWS)
    cm = _pool_col_mats()
    const = lambda shape: pl.BlockSpec(shape, lambda i: (0,) * len(shape))
    return pl.pallas_call(
        functools.partial(_pool_kernel, rows=t // GRID_W),
        grid=(b,),
        in_specs=[pl.BlockSpec((1, t, pw), lambda i: (i, 0, 0)),
                  const(cm.shape), const((ng, LANES, LANES)), const((1, pw))],
        out_specs=pl.BlockSpec((1, t, pw), lambda i: (i, 0, 0)),
        out_shape=jax.ShapeDtypeStruct((b, t, pw), BF16),
        scratch_shapes=[pltpu.VMEM((t, LANES), F32), pltpu.VMEM((t, LANES), F32)],
        compiler_params=pltpu.CompilerParams(dimension_semantics=("arbitrary",)),
        name="pool",
    )(xp, cm, w_pool, pool_scale)


def _route_kernel(gla_ref, pool_ref, x_ref, mod_ref, wo_ref, nw_ref, wr_ref, br_ref, lt_ref,
                  h_ref, xt_ref, e_ref, rk_ref, wt_ref, cnt_ref, run_ref, *, gw, n_exp):
    i = pl.program_id(0)

    @pl.when(i == 0)
    def _():
        run_ref[...] = jnp.zeros_like(run_ref)

    m = mod_ref[0]
    acc = _dot(gla_ref[...], wo_ref[0:gw, :]) + _dot(pool_ref[...], wo_ref[gw:, :])
    h = x_ref[...] + m[2:3] * acc
    h_ref[...] = h
    xt = _rmsnorm(h, nw_ref[...]) * (1.0 + m[4:5]) + m[3:4]
    xt_ref[...] = xt
    xh, xl = _split_bf16(xt)
    wr = wr_ref[...]
    wh, wl = _split_bf16(wr)
    logits = _dot(xh, wh) + _dot(xl, wh) + _dot(xh, wl) + br_ref[...]
    lane = lax.broadcasted_iota(I32, logits.shape, 1)
    neg = jnp.float32(-jnp.inf)
    logits = jnp.where(lane < n_exp, logits, neg)
    vals, hots = [], []
    e_out = jnp.zeros(logits.shape, I32)
    for j in range(TOP_K):
        mx = jnp.max(logits, axis=-1, keepdims=True)
        idx = jnp.min(jnp.where(logits == mx, lane, LANES), axis=-1, keepdims=True)
        hot = lane == idx
        vals.append(mx)
        hots.append(hot)
        e_out = jnp.where(lane == j, idx, e_out)
        logits = jnp.where(hot, neg, logits)
    ex = [jnp.exp(v - vals[0]) for v in vals]
    den = ex[0] + ex[1] + ex[2] + ex[3]
    w_out = jnp.zeros(logits.shape, F32)
    for j in range(TOP_K):
        w_out = jnp.where(lane == j, ex[j] / den, w_out)
    osum = jnp.where(hots[0] | hots[1] | hots[2] | hots[3], 1.0, 0.0)
    before = _dot(lt_ref[...], osum.astype(BF16)) + run_ref[0:1, :]
    rk_out = jnp.zeros(logits.shape, I32)
    for j in range(TOP_K):
        rj = jnp.sum(jnp.where(hots[j], before, 0.0), axis=-1, keepdims=True)
        rk_out = jnp.where(lane == j, rj.astype(I32), rk_out)
    run = run_ref[0:1, :] + jnp.sum(osum, axis=0, keepdims=True)
    run_ref[...] = jnp.broadcast_to(run, run_ref.shape)
    e_ref[...] = e_out
    rk_ref[...] = rk_out
    wt_ref[...] = w_out
    cnt_ref[...] = jnp.broadcast_to(run, cnt_ref.shape).astype(I32)


def _route_call(gla, pool, x, mod, w_out, nw, wr, br, *, tm, n_exp):
    n, d = x.shape
    gw = gla.shape[1]
    t_per_b = n // mod.shape[0]
    lt = jnp.asarray(np.tril(np.ones((tm, tm), np.float32), -1), BF16)
    row = lambda width: pl.BlockSpec((tm, width), lambda i: (i, 0))
    const = lambda shape: pl.BlockSpec(shape, lambda i: (0,) * len(shape))
    sds = jax.ShapeDtypeStruct
    return pl.pallas_call(
        functools.partial(_route_kernel, gw=gw, n_exp=n_exp),
        grid=(n // tm,),
        in_specs=[row(gw), row(pool.shape[1]), row(d),
                  pl.BlockSpec((1, N_MOD, d), lambda i: (i * tm // t_per_b, 0, 0)),
                  const(w_out.shape), const((1, d)), const(wr.shape), const((1, LANES)), const((tm, tm))],
        out_specs=[row(d), row(d), row(LANES), row(LANES), row(LANES), const((8, LANES))],
        out_shape=[sds((n, d), F32), sds((n, d), F32), sds((n, LANES), I32), sds((n, LANES), I32),
                   sds((n, LANES), F32), sds((8, LANES), I32)],
        scratch_shapes=[pltpu.VMEM((8, LANES), F32)],
        compiler_params=pltpu.CompilerParams(dimension_semantics=("arbitrary",)),
        name="route",
    )(gla, pool, x, mod, w_out, nw, wr, br, lt)


def _sc_worker_id():
    return lax.axis_index("s") * SC_CORES + lax.axis_index("c")


def _sc_scatter_call(x, idx3, *, n_out):
    n, d = x.shape
    n_win_total, k, w = idx3.shape
    n_win = n_win_total // SC_WORKERS
    mesh = plsc.VectorSubcoreMesh(core_axis_name="c", subcore_axis_name="s")

    @functools.partial(
        pl.kernel, mesh=mesh,
        out_type=jax.ShapeDtypeStruct((n_out, d), x.dtype),
        scratch_types=[pltpu.VMEM((k, w), I32), pltpu.VMEM((w, d), x.dtype), pltpu.SemaphoreType.DMA],
        name="sc_dispatch",
    )
    def kern(x_hbm, idx_hbm, out_hbm, idx_v, rows_v, sem):
        wid = _sc_worker_id()

        @pl.loop(0, n_win)
        def _(i):
            win = wid * n_win + i
            pltpu.sync_copy(idx_hbm.at[win], idx_v)
            pltpu.sync_copy(x_hbm.at[pl.ds(win * w, w)], rows_v)
            for j in range(k):
                pltpu.async_copy(rows_v, out_hbm.at[idx_v.at[j]], sem).wait()

    return kern(x, idx3)


def _sc_gather_call(table, idx3):
    n_win_total, _, w = idx3.shape
    d = table.shape[1]
    n_win = n_win_total // SC_WORKERS
    mesh = plsc.VectorSubcoreMesh(core_axis_name="c", subcore_axis_name="s")

    @functools.partial(
        pl.kernel, mesh=mesh,
        out_type=jax.ShapeDtypeStruct((n_win_total * w, d), table.dtype),
        scratch_types=[pltpu.VMEM((1, w), I32), pltpu.VMEM((w, d), table.dtype), pltpu.SemaphoreType.DMA],
        name="sc_gather",
    )
    def kern(table_hbm, idx_hbm, out_hbm, idx_v, rows_v, sem):
        wid = _sc_worker_id()

        @pl.loop(0, n_win)
        def _(i):
            win = wid * n_win + i
            pltpu.sync_copy(idx_hbm.at[win], idx_v)
            pltpu.async_copy(table_hbm.at[idx_v.at[0]], rows_v, sem).wait()
            pltpu.sync_copy(rows_v, out_hbm.at[pl.ds(win * w, w)])

    return kern(table, idx3)


def _expert_kernel(be_ref, nu_ref, nv_ref, x_ref, wgu_ref, bgu_ref, wd_ref, bd_ref, y_ref, wgu_bf_ref, wd_bf_ref,
                   *, d_ff):
    i = pl.program_id(0)

    @pl.when((i == 0) | (be_ref[i] != be_ref[jnp.maximum(i - 1, 0)]))
    def _():
        wgu_bf_ref[...] = wgu_ref[0].astype(BF16)
        wd_bf_ref[...] = wd_ref[0].astype(BF16)

    @pl.when(i < nu_ref[0])
    def _():
        row = lax.broadcasted_iota(I32, x_ref.shape, 0)
        xb = jnp.where(row < nv_ref[i], x_ref[...], 0.0).astype(BF16)
        gu = _dot(xb, wgu_bf_ref[...]) + bgu_ref[0]
        gate = jnp.minimum(gu[:, :d_ff], SWIGLU_LIMIT)
        up = jnp.clip(gu[:, d_ff:], -SWIGLU_LIMIT, SWIGLU_LIMIT)
        act = (up + 1.0) * gate * jax.nn.sigmoid(SWIGLU_ALPHA * gate)
        y_ref[...] = _dot(act.astype(BF16), wd_bf_ref[...]) + bd_ref[0]


def _expert_call(block_e, n_used, n_valid, xs, w_gu, b_gu, w_down, b_down):
    n_pad, d = xs.shape
    n_exp, _, two_ff = w_gu.shape
    d_ff = two_ff // 2
    nblk = n_pad // EXPERT_ROWS
    rows = lambda i, be, nu, nv: (jnp.minimum(i, nu[0] - 1), 0)
    per_e = lambda i, be, nu, nv: (be[i], 0, 0)
    grid_spec = pltpu.PrefetchScalarGridSpec(
        num_scalar_prefetch=3,
        grid=(nblk,),
        in_specs=[pl.BlockSpec((EXPERT_ROWS, d), rows),
                  pl.BlockSpec((1, d, two_ff), per_e), pl.BlockSpec((1, 1, two_ff), per_e),
                  pl.BlockSpec((1, d_ff, d), per_e), pl.BlockSpec((1, 1, d), per_e)],
        out_specs=pl.BlockSpec((EXPERT_ROWS, d), rows),
        scratch_shapes=[pltpu.VMEM((d, two_ff), BF16), pltpu.VMEM((d_ff, d), BF16)],
    )
    return pl.pallas_call(
        functools.partial(_expert_kernel, d_ff=d_ff),
        grid_spec=grid_spec,
        out_shape=jax.ShapeDtypeStruct((n_pad, d), F32),
        compiler_params=pltpu.CompilerParams(dimension_semantics=("arbitrary",),
                                             vmem_limit_bytes=EXPERT_VMEM_BYTES),
        name="experts",
    )(block_e, n_used, n_valid, xs, w_gu, b_gu.reshape(n_exp, 1, two_ff), w_down, b_down.reshape(n_exp, 1, d))


def _combine_kernel(y4_ref, wt_ref, h_ref, mod_ref, fw_ref, o_ref):
    wt = wt_ref[...]
    acc = wt[:, 0:1] * y4_ref[0]
    for j in range(1, TOP_K):
        acc = acc + wt[:, j:j + 1] * y4_ref[j]
    m = mod_ref[0]
    o_ref[...] = _rmsnorm(h_ref[...] + m[5:6] * acc, fw_ref[...])


def _combine_call(y4, wts, h, mod, fw, *, tg):
    n, d = h.shape
    t_per_b = n // mod.shape[0]
    row = lambda width: pl.BlockSpec((tg, width), lambda i: (i, 0))
    return pl.pallas_call(
        _combine_kernel,
        grid=(n // tg,),
        in_specs=[pl.BlockSpec((TOP_K, tg, d), lambda i: (0, i, 0)),
                  row(LANES), row(d),
                  pl.BlockSpec((1, N_MOD, d), lambda i: (i * tg // t_per_b, 0, 0)),
                  pl.BlockSpec((1, d), lambda i: (0, 0))],
        out_specs=row(d),
        out_shape=jax.ShapeDtypeStruct((n, d), F32),
        compiler_params=pltpu.CompilerParams(dimension_semantics=("arbitrary",)),
        name="combine",
    )(y4, wts, h, mod, fw)


def kernel(x, c, ctx, c_ctx, w_ada, b_ada, norm_mix_w, norm_mlp_w, w_in, w_gk_f, b_gk_f, w_gk_b, b_gk_b,
           gla_norm_w, w_pool, pool_scale, w_out, w_router, b_router, w_gu, b_gu, w_down, b_down,
           final_norm_w):
    b, t, d = x.shape
    assert w_ada.shape[0] == 1, "single-layer trunk"
    n_exp = w_router.shape[2]
    rank = w_gk_f.shape[1]
    qk = w_gk_f.shape[2]
    dk = qk // GLA_HEADS
    gw = GLA_HEADS * gla_norm_w.shape[1]
    pw = w_pool.shape[1] * w_pool.shape[2]
    assert w_in.shape[2] == 2 * qk + 2 * gw + 2 * rank + pw and 2 * rank <= LANES
    assert t % SUPER == 0 and ctx.shape[1] % SUPER == 0 and n_exp <= LANES

    rows = -(-(b + 1) // 8) * 8
    cc = jnp.zeros((rows, d), F32).at[:b].set(c).at[b].set(c_ctx)
    mod = _mod_call(cc, w_ada[0], b_ada)
    mod_x = mod[:b].reshape(b, N_MOD, d)
    mod_c = mod[b:b + 1].reshape(1, N_MOD, d)

    wi = w_in[0]
    o_r = 2 * qk + 2 * gw
    w_cat = jnp.concatenate([wi[:, :o_r], wi[:, o_r + 2 * rank:], wi[:, o_r:o_r + 2 * rank],
                             jnp.zeros((d, LANES - 2 * rank), F32)], axis=1).astype(BF16)
    wgk = jnp.zeros((LANES, 2 * qk), F32).at[:rank, :qk].set(w_gk_f[0]).at[rank:2 * rank, qk:].set(w_gk_b[0])
    bgk = jnp.concatenate([b_gk_f[0], b_gk_b[0]])[None, :]
    proj = functools.partial(_inproj_call, nw=norm_mix_w, w=w_cat, wgk=wgk.astype(BF16), bgk=bgk,
                             qk=qk, gw=gw, pw=pw, dk=dk)
    q, k, v, g, xp, gk = proj(x, mod_x, tm=512)
    _, kc, vc, _, _, gkc = proj(ctx, mod_c, tm=SUPER)

    gla = _gla_call(q, k, v, gk, g, kc, vc, gkc, gla_norm_w)
    pool = _pool_call(xp, w_pool[0].astype(BF16), pool_scale)

    n = b * t
    wr = jnp.zeros((d, LANES), F32).at[:, :n_exp].set(w_router[0])
    br = jnp.zeros((1, LANES), F32).at[0, :n_exp].set(b_router[0])
    h, xt, e_idx, rk, wts, cnt = _route_call(
        gla.reshape(n, gw), pool.reshape(n, pw), x.reshape(n, d), mod_x, w_out[0].astype(BF16),
        norm_mlp_w, wr, br, tm=512, n_exp=n_exp)

    counts = cnt[0, :n_exp]
    padded = (counts + EXPERT_ROWS - 1) // EXPERT_ROWS * EXPERT_ROWS
    pad_end = jnp.cumsum(padded)
    start_pad = pad_end - padded
    dest = (jnp.take(start_pad, e_idx[:, :TOP_K]) + rk[:, :TOP_K]).astype(I32)
    n_pad = n * TOP_K + n_exp * EXPERT_ROWS
    nblk = n_pad // EXPERT_ROWS
    n_used = (pad_end[-1] // EXPERT_ROWS).astype(I32)
    blk = jnp.arange(nblk, dtype=I32)
    block_e = jnp.sum((pad_end[None, :] <= (blk * EXPERT_ROWS)[:, None]).astype(I32), axis=1)
    block_e = jnp.minimum(block_e, n_exp - 1)
    block_e = jnp.where(blk < n_used, block_e, block_e[jnp.maximum(n_used - 1, 0)])

    n_valid = jnp.clip(jnp.take(start_pad + counts, block_e) - blk * EXPERT_ROWS, 0, EXPERT_ROWS).astype(I32)

    assert n % (SC_WORKERS * SC_WINDOW) == 0
    idx_scatter = dest.reshape(n // SC_WINDOW, SC_WINDOW, TOP_K).transpose(0, 2, 1)
    xs = _sc_scatter_call(xt, idx_scatter, n_out=n_pad)
    ys = _expert_call(block_e, n_used.reshape(1), n_valid, xs, w_gu[0], b_gu[0], w_down[0], b_down[0])
    idx_gather = dest.T.reshape(n * TOP_K // SC_WINDOW, 1, SC_WINDOW)
    y4 = _sc_gather_call(ys, idx_gather).reshape(TOP_K, n, d)
    out = _combine_call(y4, wts, h, mod_x, final_norm_w[None, :], tg=256)
    return out.reshape(b, t, d)
```

```python
import functools

import numpy as np
import jax
import jax.numpy as jnp
from jax import lax
from jax.experimental import pallas as pl
from jax.experimental.pallas import tpu as pltpu
from jax.experimental.pallas import tpu_sc as plsc

F32 = jnp.float32
BF16 = jnp.bfloat16
I32 = jnp.int32

GRID_W = 64
GLA_HEADS = 4
GLA_CHUNK = 64
GATE_NORMALIZER = 16.0
POOL_WINDOWS = (2, 4, 8, 16)
TOP_K = 4
SWIGLU_LIMIT = 7.0
SWIGLU_ALPHA = 1.702
N_MOD = 6
EPS = 1e-6

LANES = 128
SUPER = 256
HEAD_PAIR_DK = 128
EXPERT_ROWS = 256
EXPERT_VMEM_BYTES = 56 * 1024 * 1024
SC_CORES = 2
SC_SUBCORES = 16
SC_WORKERS = SC_CORES * SC_SUBCORES
SC_WINDOW = 32


def _dot(a, b):
    return jnp.dot(a, b, preferred_element_type=F32)


def _dot_nt(a, b):
    return lax.dot_general(a, b, (((1,), (1,)), ((), ())), preferred_element_type=F32)


def _split_bf16(x):
    hi = x.astype(BF16)
    lo = (x - hi.astype(F32)).astype(BF16)
    return hi, lo


def _rmsnorm(x, w):
    var = jnp.mean(x * x, axis=-1, keepdims=True)
    return x * lax.rsqrt(var + EPS) * w


def _mod_kernel(c_ref, w_ref, b_ref, o_ref):
    c = c_ref[...]
    s = c * jax.nn.sigmoid(c)
    o_ref[...] = jnp.dot(s, w_ref[...], precision=lax.Precision.HIGHEST,
                         preferred_element_type=F32) + b_ref[...]


def _mod_call(cc, w_ada, b_ada):
    rows, d = cc.shape
    n = w_ada.shape[1]
    tn = 1024
    return pl.pallas_call(
        _mod_kernel,
        grid=(n // tn,),
        in_specs=[pl.BlockSpec((rows, d), lambda j: (0, 0)),
                  pl.BlockSpec((d, tn), lambda j: (0, j)),
                  pl.BlockSpec((1, tn), lambda j: (0, j))],
        out_specs=pl.BlockSpec((rows, tn), lambda j: (0, j)),
        out_shape=jax.ShapeDtypeStruct((rows, n), F32),
        name="mod",
    )(cc, w_ada, b_ada)


def _inproj_kernel(x_ref, mod_ref, nw_ref, w_ref, wvt_ref, wgk_ref, bgk_ref,
                   q_ref, k_ref, v_ref, vt_ref, g_ref, p_ref, gk_ref, *, qk, gw, pw, dk):
    x = x_ref[0]
    m = mod_ref[0]
    hm = (_rmsnorm(x, nw_ref[...]) * (1.0 + m[1:2]) + m[0:1]).astype(BF16)
    p = _dot(hm, w_ref[...])
    vt = _dot_nt(wvt_ref[...], hm)
    for s in range(vt_ref.shape[1]):
        vt_ref[0, s] = vt[:, s * SUPER:(s + 1) * SUPER].astype(BF16)
    o = 0
    q_ref[0] = (p[:, o:o + qk] * (dk ** -0.5)).astype(BF16); o += qk
    k_ref[0] = p[:, o:o + qk].astype(BF16); o += qk
    v_ref[0] = p[:, o:o + gw].astype(BF16); o += gw
    g_ref[0] = p[:, o:o + gw].astype(BF16); o += gw
    p_ref[0] = p[:, o:o + pw].astype(BF16); o += pw
    r = p[:, o:o + LANES]
    z = _dot(r.astype(BF16), wgk_ref[...]) + bgk_ref[...]
    gk_ref[0] = (jnp.minimum(z, 0.0) - jnp.log1p(jnp.exp(-jnp.abs(z)))) * (1.0 / GATE_NORMALIZER)


def _inproj_call(x, mod, nw, w, wvt, wgk, bgk, *, qk, gw, pw, dk, tm):
    b, t, d = x.shape
    n_in = w.shape[1]
    bs = lambda width: pl.BlockSpec((1, tm, width), lambda i, j: (i, j, 0))
    const = lambda shape: pl.BlockSpec(shape, lambda i, j: (0,) * len(shape))
    per_batch = mod.shape[0] > 1
    sds = jax.ShapeDtypeStruct
    return pl.pallas_call(
        functools.partial(_inproj_kernel, qk=qk, gw=gw, pw=pw, dk=dk),
        grid=(b, t // tm),
        in_specs=[bs(d),
                  pl.BlockSpec((1, N_MOD, d), (lambda i, j: (i, 0, 0)) if per_batch else (lambda i, j: (0, 0, 0))),
                  const((1, d)), const((d, n_in)), const((gw, d)), const((LANES, 2 * qk)), const((1, 2 * qk))],
        out_specs=[bs(qk), bs(qk), bs(gw),
                   pl.BlockSpec((1, tm // SUPER, gw, SUPER), lambda i, j: (i, j, 0, 0)),
                   bs(gw), bs(pw), bs(2 * qk)],
        out_shape=[sds((b, t, qk), BF16), sds((b, t, qk), BF16), sds((b, t, gw), BF16),
                   sds((b, t // SUPER, gw, SUPER), BF16),
                   sds((b, t, gw), BF16), sds((b, t, pw), BF16), sds((b, t, 2 * qk), F32)],
        compiler_params=pltpu.CompilerParams(dimension_semantics=("arbitrary", "arbitrary")),
        name="inproj",
    )(x, mod, nw, w, wvt, wgk, bgk)


def _gla_super(q, k, v, vt, gk, cm, amask, bd_mask, st, fwd, want_out):
    nch = SUPER // GLA_CHUNK
    order = tuple(range(nch)) if fwd else tuple(reversed(range(nch)))
    last_row = GLA_CHUNK - 1 if fwd else 0
    mid_row = GLA_CHUNK // 2 - 1 if fwd else GLA_CHUNK // 2
    hi, lo = _split_bf16(gk)
    bcum = _dot(cm, hi) + _dot(cm, lo)

    def chunk_row(r):
        return jnp.concatenate(
            [jnp.broadcast_to(bcum[c * GLA_CHUNK + r:c * GLA_CHUNK + r + 1, :], (GLA_CHUNK, bcum.shape[1]))
             for c in range(nch)], axis=0)

    chunk_of_row = lax.broadcasted_iota(I32, bcum.shape, 0) // GLA_CHUNK

    def by_chunk(x):
        return jnp.concatenate([jnp.where(chunk_of_row == c, x, 0.0).astype(BF16) for c in range(nch)], axis=1)

    blast = chunk_row(last_row)
    u_all = _dot(vt, by_chunk(k * jnp.exp(blast - bcum)))
    before = [None] * nch
    for c in order:
        before[c] = st
        decay = jnp.exp(bcum[c * GLA_CHUNK + last_row:c * GLA_CHUNK + last_row + 1, :])
        st = st * decay + jnp.where(bd_mask, u_all[:, c * HEAD_PAIR_DK:(c + 1) * HEAD_PAIR_DK], 0.0)
    if not want_out:
        return None, st
    bmid = chunk_row(mid_row)
    qt = q * jnp.exp(bcum - bmid)
    kt = (k * jnp.exp(bmid - bcum)).astype(BF16)
    lane = lax.broadcasted_iota(I32, qt.shape, 1)
    half = HEAD_PAIR_DK // 2
    o_heads = []
    for hh in range(2):
        sel = (lane < half) if hh == 0 else (lane >= half)
        a = _dot_nt(jnp.where(sel, qt, 0.0).astype(BF16), kt)
        a = jnp.where(amask, a, 0.0).astype(BF16)
        o_heads.append(_dot(a, v[:, hh * LANES:(hh + 1) * LANES]))
    s_all = jnp.concatenate([s.astype(BF16) for s in before], axis=1)
    o_inter = _dot_nt(by_chunk(q * jnp.exp(bcum)), s_all)
    return jnp.concatenate(o_heads, axis=1) + o_inter, st


def _gla_kernel(q_ref, k_ref, v_ref, vt_ref, gkf_ref, gkb_ref, g_ref, kc_ref, vtc_ref, gkfc_ref, gkbc_ref,
                nw_ref, cmf_ref, cmb_ref, o_ref, stf_ref, stb_ref, of_ref, ob_ref):
    t = q_ref.shape[1]
    tc = kc_ref.shape[1]
    nsc, nscc = t // SUPER, tc // SUPER
    cmf = cmf_ref[...]
    cmb = cmb_ref[...]
    amask_f = cmf > 0
    amask_b = cmb > 0
    row = lax.broadcasted_iota(I32, (2 * LANES, HEAD_PAIR_DK), 0)
    lane = lax.broadcasted_iota(I32, (2 * LANES, HEAD_PAIR_DK), 1)
    bd_mask = (row < LANES) == (lane < HEAD_PAIR_DK // 2)

    def ctx_state(gk_ref, cm, fwd, j, st):
        rows = pl.ds(j * SUPER, SUPER)
        return _gla_super(None, kc_ref[0, rows, :].astype(F32), None, vtc_ref[0, j], gk_ref[0, rows, :],
                          cm, None, bd_mask, st, fwd, False)[1]

    def latent(gk_ref, cm, amask, fwd, j, st):
        rows = pl.ds(pl.multiple_of(j * SUPER, SUPER), SUPER)
        return _gla_super(q_ref[0, rows, :].astype(F32), k_ref[0, rows, :].astype(F32), v_ref[0, rows, :],
                          vt_ref[0, j], gk_ref[0, rows, :], cm, amask, bd_mask, st, fwd, True)

    st = jnp.zeros(stf_ref.shape, F32)
    for j in range(nscc):
        st = ctx_state(gkfc_ref, cmf, True, j, st)
    stf_ref[...] = st
    st = jnp.zeros(stb_ref.shape, F32)
    for j in reversed(range(nscc)):
        st = ctx_state(gkbc_ref, cmb, False, j, st)
    stb_ref[...] = st

    def scan_body(jj, carry):
        jb = nsc - 1 - jj
        of, stf = latent(gkf_ref, cmf, amask_f, True, jj, stf_ref[...])
        of_ref[pl.ds(pl.multiple_of(jj * SUPER, SUPER), SUPER), :] = of
        stf_ref[...] = stf
        ob, stb = latent(gkb_ref, cmb, amask_b, False, jb, stb_ref[...])
        ob_ref[pl.ds(pl.multiple_of(jb * SUPER, SUPER), SUPER), :] = ob
        stb_ref[...] = stb
        return carry

    lax.fori_loop(0, nsc, scan_body, 0)

    nw = nw_ref[...]

    def out_body(j, carry):
        rows = pl.ds(pl.multiple_of(j * SUPER, SUPER), SUPER)
        o = of_ref[rows, :] + ob_ref[rows, :]
        g = g_ref[0, rows, :].astype(F32)
        gate = g * jax.nn.sigmoid(g)
        for hh in range(2):
            oh = o[:, hh * LANES:(hh + 1) * LANES]
            on = oh * lax.rsqrt(jnp.mean(oh * oh, axis=-1, keepdims=True) + EPS) * nw
            o_ref[0, rows, hh * LANES:(hh + 1) * LANES] = (on * gate[:, hh * LANES:(hh + 1) * LANES]).astype(BF16)
        return carry

    lax.fori_loop(0, nsc, out_body, 0)


def _gla_masks():
    i = np.arange(SUPER)
    same = (i[:, None] // GLA_CHUNK) == (i[None, :] // GLA_CHUNK)
    fwd = same & (i[None, :] <= i[:, None])
    bwd = same & (i[None, :] >= i[:, None])
    return jnp.asarray(fwd, BF16), jnp.asarray(bwd, BF16)


def _gla_call(q, k, v, vt, gk, g, kc, vtc, gkc, nw):
    b, t, qk = q.shape
    tc = kc.shape[1]
    npair = qk // HEAD_PAIR_DK
    cmf, cmb = _gla_masks()
    lat = lambda width, off: pl.BlockSpec((1, t, width), lambda i, j: (i, 0, j + off))
    ctx = lambda width, off: pl.BlockSpec((1, tc, width), lambda i, j: (i, 0, j + off))
    tr = lambda n_groups: pl.BlockSpec((1, n_groups, 2 * LANES, SUPER), lambda i, j: (i, 0, j, 0))
    const = lambda shape: pl.BlockSpec(shape, lambda i, j: (0,) * len(shape))
    return pl.pallas_call(
        _gla_kernel,
        grid=(b, npair),
        in_specs=[lat(HEAD_PAIR_DK, 0), lat(HEAD_PAIR_DK, 0), lat(2 * LANES, 0), tr(t // SUPER),
                  lat(HEAD_PAIR_DK, 0), lat(HEAD_PAIR_DK, npair), lat(2 * LANES, 0),
                  ctx(HEAD_PAIR_DK, 0), tr(tc // SUPER), ctx(HEAD_PAIR_DK, 0), ctx(HEAD_PAIR_DK, npair),
                  const((1, LANES)), const(cmf.shape), const(cmb.shape)],
        out_specs=lat(2 * LANES, 0),
        out_shape=jax.ShapeDtypeStruct((b, t, v.shape[2]), BF16),
        scratch_shapes=[pltpu.VMEM((2 * LANES, HEAD_PAIR_DK), F32), pltpu.VMEM((2 * LANES, HEAD_PAIR_DK), F32),
                        pltpu.VMEM((t, 2 * LANES), F32), pltpu.VMEM((t, 2 * LANES), F32)],
        compiler_params=pltpu.CompilerParams(dimension_semantics=("arbitrary", "arbitrary")),
        name="gla",
    )(q, k, v, vt, gk, gk, g, kc, vtc, gkc, gkc, nw, cmf, cmb)


def _pool_kernel(x_ref, cm_ref, wp_ref, ps_ref, o_ref, y_ref, z_ref, *, rows):
    t = x_ref.shape[1]
    tok = lax.broadcasted_iota(I32, (t, LANES), 0)
    r = tok // GRID_W
    c = tok % GRID_W
    for gi, w in enumerate(POOL_WINDOWS):
        lo = w // 2
        hi = w - 1 - lo
        cols = slice(gi * LANES, (gi + 1) * LANES)
        cmat = cm_ref[gi]
        for blk in range(t // SUPER):
            rs = slice(blk * SUPER, (blk + 1) * SUPER)
            y_ref[rs, :] = _dot(cmat, x_ref[0, rs, cols])
        z_ref[...] = y_ref[...]
        for dr in range(-lo, hi + 1):
            sh = abs(dr) * GRID_W
            if dr == 0 or sh >= t:
                continue
            if dr > 0:
                z_ref[0:t - sh, :] += y_ref[sh:t, :]
            else:
                z_ref[sh:t, :] += y_ref[0:t - sh, :]
        cnt_r = jnp.minimum(r + hi + 1, rows) - jnp.maximum(r - lo, 0)
        cnt_c = jnp.minimum(c + hi + 1, GRID_W) - jnp.maximum(c - lo, 0)
        cnt = (cnt_r * cnt_c).astype(F32)
        pooled = z_ref[...] / cnt - x_ref[0, :, cols].astype(F32)
        yp = _dot(pooled.astype(BF16), wp_ref[gi]) * ps_ref[:, cols]
        o_ref[0, :, cols] = yp.astype(BF16)


def _pool_col_mats():
    i = np.arange(SUPER)
    same_row = (i[:, None] // GRID_W) == (i[None, :] // GRID_W)
    d = i[None, :] - i[:, None]
    mats = []
    for w in POOL_WINDOWS:
        lo = w // 2
        hi = w - 1 - lo
        mats.append(same_row & (d >= -lo) & (d <= hi))
    return jnp.asarray(np.stack(mats), BF16)


def _pool_call(xp, w_pool, pool_scale):
    b, t, pw = xp.shape
    ng = len(POOL_WINDOWS)
    cm = _pool_col_mats()
    const = lambda shape: pl.BlockSpec(shape, lambda i: (0,) * len(shape))
    return pl.pallas_call(
        functools.partial(_pool_kernel, rows=t // GRID_W),
        grid=(b,),
        in_specs=[pl.BlockSpec((1, t, pw), lambda i: (i, 0, 0)),
                  const(cm.shape), const((ng, LANES, LANES)), const((1, pw))],
        out_specs=pl.BlockSpec((1, t, pw), lambda i: (i, 0, 0)),
        out_shape=jax.ShapeDtypeStruct((b, t, pw), BF16),
        scratch_shapes=[pltpu.VMEM((t, LANES), F32), pltpu.VMEM((t, LANES), F32)],
        compiler_params=pltpu.CompilerParams(dimension_semantics=("arbitrary",)),
        name="pool",
    )(xp, cm, w_pool, pool_scale)


def _route_kernel(gla_ref, pool_ref, x_ref, mod_ref, wo_ref, nw_ref, wr_ref, br_ref, lt_ref,
                  h_ref, xt_ref, e_ref, rk_ref, wt_ref, cnt_ref, run_ref, *, gw, n_exp):
    i = pl.program_id(0)

    @pl.when(i == 0)
    def _():
        run_ref[...] = jnp.zeros_like(run_ref)

    m = mod_ref[0]
    acc = _dot(gla_ref[...], wo_ref[0:gw, :]) + _dot(pool_ref[...], wo_ref[gw:, :])
    h = x_ref[...] + m[2:3] * acc
    h_ref[...] = h
    xt = _rmsnorm(h, nw_ref[...]) * (1.0 + m[4:5]) + m[3:4]
    xt_ref[...] = xt
    xh, xl = _split_bf16(xt)
    wr = wr_ref[...]
    wh, wl = _split_bf16(wr)
    logits = _dot(xh, wh) + _dot(xl, wh) + _dot(xh, wl) + br_ref[...]
    lane = lax.broadcasted_iota(I32, logits.shape, 1)
    neg = jnp.float32(-jnp.inf)
    logits = jnp.where(lane < n_exp, logits, neg)
    vals, hots = [], []
    e_out = jnp.zeros(logits.shape, I32)
    for j in range(TOP_K):
        mx = jnp.max(logits, axis=-1, keepdims=True)
        idx = jnp.min(jnp.where(logits == mx, lane, LANES), axis=-1, keepdims=True)
        hot = lane == idx
        vals.append(mx)
        hots.append(hot)
        e_out = jnp.where(lane == j, idx, e_out)
        logits = jnp.where(hot, neg, logits)
    ex = [jnp.exp(v - vals[0]) for v in vals]
    den = ex[0] + ex[1] + ex[2] + ex[3]
    w_out = jnp.zeros(logits.shape, F32)
    for j in range(TOP_K):
        w_out = jnp.where(lane == j, ex[j] / den, w_out)
    osum = jnp.where(hots[0] | hots[1] | hots[2] | hots[3], 1.0, 0.0)
    before = _dot(lt_ref[...], osum.astype(BF16)) + run_ref[0:1, :]
    rk_out = jnp.zeros(logits.shape, I32)
    for j in range(TOP_K):
        rj = jnp.sum(jnp.where(hots[j], before, 0.0), axis=-1, keepdims=True)
        rk_out = jnp.where(lane == j, rj.astype(I32), rk_out)
    run = run_ref[0:1, :] + jnp.sum(osum, axis=0, keepdims=True)
    run_ref[...] = jnp.broadcast_to(run, run_ref.shape)
    e_ref[...] = e_out
    rk_ref[...] = rk_out
    wt_ref[...] = w_out
    cnt_ref[...] = jnp.broadcast_to(run, cnt_ref.shape).astype(I32)


def _route_call(gla, pool, x, mod, w_out, nw, wr, br, *, tm, n_exp):
    n, d = x.shape
    gw = gla.shape[1]
    t_per_b = n // mod.shape[0]
    lt = jnp.asarray(np.tril(np.ones((tm, tm), np.float32), -1), BF16)
    row = lambda width: pl.BlockSpec((tm, width), lambda i: (i, 0))
    const = lambda shape: pl.BlockSpec(shape, lambda i: (0,) * len(shape))
    sds = jax.ShapeDtypeStruct
    return pl.pallas_call(
        functools.partial(_route_kernel, gw=gw, n_exp=n_exp),
        grid=(n // tm,),
        in_specs=[row(gw), row(pool.shape[1]), row(d),
                  pl.BlockSpec((1, N_MOD, d), lambda i: (i * tm // t_per_b, 0, 0)),
                  const(w_out.shape), const((1, d)), const(wr.shape), const((1, LANES)), const((tm, tm))],
        out_specs=[row(d), row(d), row(LANES), row(LANES), row(LANES), const((8, LANES))],
        out_shape=[sds((n, d), F32), sds((n, d), F32), sds((n, LANES), I32), sds((n, LANES), I32),
                   sds((n, LANES), F32), sds((8, LANES), I32)],
        scratch_shapes=[pltpu.VMEM((8, LANES), F32)],
        compiler_params=pltpu.CompilerParams(dimension_semantics=("arbitrary",)),
        name="route",
    )(gla, pool, x, mod, w_out, nw, wr, br, lt)


def _sc_worker_id():
    return lax.axis_index("s") * SC_CORES + lax.axis_index("c")


def _sc_scatter_call(x, idx3, *, n_out):
    n, d = x.shape
    n_win_total, k, w = idx3.shape
    n_win = n_win_total // SC_WORKERS
    mesh = plsc.VectorSubcoreMesh(core_axis_name="c", subcore_axis_name="s")

    @functools.partial(
        pl.kernel, mesh=mesh,
        out_type=jax.ShapeDtypeStruct((n_out, d), x.dtype),
        scratch_types=[pltpu.VMEM((k, w), I32), pltpu.VMEM((w, d), x.dtype), pltpu.SemaphoreType.DMA],
        name="sc_dispatch",
    )
    def kern(x_hbm, idx_hbm, out_hbm, idx_v, rows_v, sem):
        wid = _sc_worker_id()

        @pl.loop(0, n_win)
        def _(i):
            win = wid * n_win + i
            pltpu.sync_copy(idx_hbm.at[win], idx_v)
            pltpu.sync_copy(x_hbm.at[pl.ds(win * w, w)], rows_v)
            for j in range(k):
                pltpu.async_copy(rows_v, out_hbm.at[idx_v.at[j]], sem).wait()

    return kern(x, idx3)


def _sc_gather_call(table, idx3):
    n_win_total, _, w = idx3.shape
    d = table.shape[1]
    n_win = n_win_total // SC_WORKERS
    mesh = plsc.VectorSubcoreMesh(core_axis_name="c", subcore_axis_name="s")

    @functools.partial(
        pl.kernel, mesh=mesh,
        out_type=jax.ShapeDtypeStruct((n_win_total * w, d), table.dtype),
        scratch_types=[pltpu.VMEM((1, w), I32), pltpu.VMEM((w, d), table.dtype), pltpu.SemaphoreType.DMA],
        name="sc_gather",
    )
    def kern(table_hbm, idx_hbm, out_hbm, idx_v, rows_v, sem):
        wid = _sc_worker_id()

        @pl.loop(0, n_win)
        def _(i):
            win = wid * n_win + i
            pltpu.sync_copy(idx_hbm.at[win], idx_v)
            pltpu.async_copy(table_hbm.at[idx_v.at[0]], rows_v, sem).wait()
            pltpu.sync_copy(rows_v, out_hbm.at[pl.ds(win * w, w)])

    return kern(table, idx3)


def _expert_kernel(be_ref, nu_ref, nv_ref, x_ref, wgu_ref, bgu_ref, wd_ref, bd_ref, y_ref, wgu_bf_ref, wd_bf_ref,
                   *, d_ff):
    i = pl.program_id(0)

    @pl.when((i == 0) | (be_ref[i] != be_ref[jnp.maximum(i - 1, 0)]))
    def _():
        wgu_bf_ref[...] = wgu_ref[0].astype(BF16)
        wd_bf_ref[...] = wd_ref[0].astype(BF16)

    @pl.when(i < nu_ref[0])
    def _():
        row = lax.broadcasted_iota(I32, x_ref.shape, 0)
        xb = jnp.where(row < nv_ref[i], x_ref[...], 0.0).astype(BF16)
        gu = _dot(xb, wgu_bf_ref[...]) + bgu_ref[0]
        gate = jnp.minimum(gu[:, :d_ff], SWIGLU_LIMIT)
        up = jnp.clip(gu[:, d_ff:], -SWIGLU_LIMIT, SWIGLU_LIMIT)
        act = (up + 1.0) * gate * jax.nn.sigmoid(SWIGLU_ALPHA * gate)
        y_ref[...] = _dot(act.astype(BF16), wd_bf_ref[...]) + bd_ref[0]


def _expert_call(block_e, n_used, n_valid, xs, w_gu, b_gu, w_down, b_down):
    n_pad, d = xs.shape
    n_exp, _, two_ff = w_gu.shape
    d_ff = two_ff // 2
    nblk = n_pad // EXPERT_ROWS
    rows = lambda i, be, nu, nv: (jnp.minimum(i, nu[0] - 1), 0)
    per_e = lambda i, be, nu, nv: (be[i], 0, 0)
    grid_spec = pltpu.PrefetchScalarGridSpec(
        num_scalar_prefetch=3,
        grid=(nblk,),
        in_specs=[pl.BlockSpec((EXPERT_ROWS, d), rows),
                  pl.BlockSpec((1, d, two_ff), per_e), pl.BlockSpec((1, 1, two_ff), per_e),
                  pl.BlockSpec((1, d_ff, d), per_e), pl.BlockSpec((1, 1, d), per_e)],
        out_specs=pl.BlockSpec((EXPERT_ROWS, d), rows),
        scratch_shapes=[pltpu.VMEM((d, two_ff), BF16), pltpu.VMEM((d_ff, d), BF16)],
    )
    return pl.pallas_call(
        functools.partial(_expert_kernel, d_ff=d_ff),
        grid_spec=grid_spec,
        out_shape=jax.ShapeDtypeStruct((n_pad, d), F32),
        compiler_params=pltpu.CompilerParams(dimension_semantics=("arbitrary",),
                                             vmem_limit_bytes=EXPERT_VMEM_BYTES),
        name="experts",
    )(block_e, n_used, n_valid, xs, w_gu, b_gu.reshape(n_exp, 1, two_ff), w_down, b_down.reshape(n_exp, 1, d))


def _combine_kernel(y4_ref, wt_ref, h_ref, mod_ref, fw_ref, o_ref):
    wt = wt_ref[...]
    acc = wt[:, 0:1] * y4_ref[0]
    for j in range(1, TOP_K):
        acc = acc + wt[:, j:j + 1] * y4_ref[j]
    m = mod_ref[0]
    o_ref[...] = _rmsnorm(h_ref[...] + m[5:6] * acc, fw_ref[...])


def _combine_call(y4, wts, h, mod, fw, *, tg):
    n, d = h.shape
    t_per_b = n // mod.shape[0]
    row = lambda width: pl.BlockSpec((tg, width), lambda i: (i, 0))
    return pl.pallas_call(
        _combine_kernel,
        grid=(n // tg,),
        in_specs=[pl.BlockSpec((TOP_K, tg, d), lambda i: (0, i, 0)),
                  row(LANES), row(d),
                  pl.BlockSpec((1, N_MOD, d), lambda i: (i * tg // t_per_b, 0, 0)),
                  pl.BlockSpec((1, d), lambda i: (0, 0))],
        out_specs=row(d),
        out_shape=jax.ShapeDtypeStruct((n, d), F32),
        compiler_params=pltpu.CompilerParams(dimension_semantics=("arbitrary",)),
        name="combine",
    )(y4, wts, h, mod, fw)


def kernel(x, c, ctx, c_ctx, w_ada, b_ada, norm_mix_w, norm_mlp_w, w_in, w_gk_f, b_gk_f, w_gk_b, b_gk_b,
           gla_norm_w, w_pool, pool_scale, w_out, w_router, b_router, w_gu, b_gu, w_down, b_down,
           final_norm_w):
    b, t, d = x.shape
    assert w_ada.shape[0] == 1, "single-layer trunk"
    n_exp = w_router.shape[2]
    rank = w_gk_f.shape[1]
    qk = w_gk_f.shape[2]
    dk = qk // GLA_HEADS
    gw = GLA_HEADS * gla_norm_w.shape[1]
    pw = w_pool.shape[1] * w_pool.shape[2]
    assert w_in.shape[2] == 2 * qk + 2 * gw + 2 * rank + pw and 2 * rank <= LANES
    assert t % SUPER == 0 and ctx.shape[1] % SUPER == 0 and n_exp <= LANES

    rows = -(-(b + 1) // 8) * 8
    cc = jnp.zeros((rows, d), F32).at[:b].set(c).at[b].set(c_ctx)
    mod = _mod_call(cc, w_ada[0], b_ada)
    mod_x = mod[:b].reshape(b, N_MOD, d)
    mod_c = mod[b:b + 1].reshape(1, N_MOD, d)

    wi = w_in[0]
    o_r = 2 * qk + 2 * gw
    w_cat = jnp.concatenate([wi[:, :o_r], wi[:, o_r + 2 * rank:], wi[:, o_r:o_r + 2 * rank],
                             jnp.zeros((d, LANES - 2 * rank), F32)], axis=1).astype(BF16)
    wgk = jnp.zeros((LANES, 2 * qk), F32).at[:rank, :qk].set(w_gk_f[0]).at[rank:2 * rank, qk:].set(w_gk_b[0])
    bgk = jnp.concatenate([b_gk_f[0], b_gk_b[0]])[None, :]
    w_vt = wi[:, 2 * qk:2 * qk + gw].T.astype(BF16)
    proj = functools.partial(_inproj_call, nw=norm_mix_w, w=w_cat, wvt=w_vt, wgk=wgk.astype(BF16), bgk=bgk,
                             qk=qk, gw=gw, pw=pw, dk=dk)
    q, k, v, vt, g, xp, gk = proj(x, mod_x, tm=512)
    _, kc, _, vtc, _, _, gkc = proj(ctx, mod_c, tm=SUPER)

    gla = _gla_call(q, k, v, vt, gk, g, kc, vtc, gkc, gla_norm_w)
    pool = _pool_call(xp, w_pool[0].astype(BF16), pool_scale)

    n = b * t
    wr = jnp.zeros((d, LANES), F32).at[:, :n_exp].set(w_router[0])
    br = jnp.zeros((1, LANES), F32).at[0, :n_exp].set(b_router[0])
    h, xt, e_idx, rk, wts, cnt = _route_call(
        gla.reshape(n, gw), pool.reshape(n, pw), x.reshape(n, d), mod_x, w_out[0].astype(BF16),
        norm_mlp_w, wr, br, tm=512, n_exp=n_exp)

    counts = cnt[0, :n_exp]
    padded = (counts + EXPERT_ROWS - 1) // EXPERT_ROWS * EXPERT_ROWS
    pad_end = jnp.cumsum(padded)
    start_pad = pad_end - padded
    dest = (jnp.take(start_pad, e_idx[:, :TOP_K]) + rk[:, :TOP_K]).astype(I32)
    n_pad = n * TOP_K + n_exp * EXPERT_ROWS
    nblk = n_pad // EXPERT_ROWS
    n_used = (pad_end[-1] // EXPERT_ROWS).astype(I32)
    blk = jnp.arange(nblk, dtype=I32)
    block_e = jnp.sum((pad_end[None, :] <= (blk * EXPERT_ROWS)[:, None]).astype(I32), axis=1)
    block_e = jnp.minimum(block_e, n_exp - 1)
    block_e = jnp.where(blk < n_used, block_e, block_e[jnp.maximum(n_used - 1, 0)])

    n_valid = jnp.clip(jnp.take(start_pad + counts, block_e) - blk * EXPERT_ROWS, 0, EXPERT_ROWS).astype(I32)

    assert n % (SC_WORKERS * SC_WINDOW) == 0
    idx_scatter = dest.reshape(n // SC_WINDOW, SC_WINDOW, TOP_K).transpose(0, 2, 1)
    xs = _sc_scatter_call(xt, idx_scatter, n_out=n_pad)
    ys = _expert_call(block_e, n_used.reshape(1), n_valid, xs, w_gu[0], b_gu[0], w_down[0], b_down[0])
    idx_gather = dest.T.reshape(n * TOP_K // SC_WINDOW, 1, SC_WINDOW)
    y4 = _sc_gather_call(ys, idx_gather).reshape(TOP_K, n, d)
    out = _combine_call(y4, wts, h, mod_x, final_norm_w[None, :], tg=256)
    return out.reshape(b, t, d)
```

```python
import functools

import numpy as np
import jax
import jax.numpy as jnp
from jax import lax
from jax.experimental import pallas as pl
from jax.experimental.pallas import tpu as pltpu
from jax.experimental.pallas import tpu_sc as plsc

F32 = jnp.float32
BF16 = jnp.bfloat16
I32 = jnp.int32
U32 = jnp.uint32

GRID_W = 64
GLA_HEADS = 4
GLA_CHUNK = 64
GATE_NORMALIZER = 16.0
POOL_WINDOWS = (2, 4, 8, 16)
TOP_K = 4
SWIGLU_LIMIT = 7.0
SWIGLU_ALPHA = 1.702
N_MOD = 6
EPS = 1e-6

LANES = 128
SUPER = 256
HEAD_PAIR_DK = 128
EXPERT_ROWS = 256
EXPERT_VMEM_BYTES = 56 * 1024 * 1024
SC_CORES = 2
SC_SUBCORES = 16
SC_WORKERS = SC_CORES * SC_SUBCORES
SC_WINDOW = 32


def _dot(a, b):
    return jnp.dot(a, b, preferred_element_type=F32)


def _dot_nt(a, b):
    return lax.dot_general(a, b, (((1,), (1,)), ((), ())), preferred_element_type=F32)


def _split_bf16(x):
    hi = x.astype(BF16)
    lo = (x - hi.astype(F32)).astype(BF16)
    return hi, lo


def _pack_bf16_pairs(x):
    c = x.shape[1] // 2
    lo = lax.bitcast_convert_type(x[:, :c].astype(BF16).astype(F32), U32)
    hi = lax.bitcast_convert_type(x[:, c:].astype(BF16).astype(F32), U32)
    return (lo >> 16) | hi


def _unpack_bf16_pairs(p):
    lo = lax.bitcast_convert_type(p << 16, F32)
    hi = lax.bitcast_convert_type(p & jnp.uint32(0xFFFF0000), F32)
    return lo, hi


def _rmsnorm(x, w):
    var = jnp.mean(x * x, axis=-1, keepdims=True)
    return x * lax.rsqrt(var + EPS) * w


def _mod_kernel(c_ref, w_ref, b_ref, o_ref):
    c = c_ref[...]
    s = c * jax.nn.sigmoid(c)
    o_ref[...] = jnp.dot(s, w_ref[...], precision=lax.Precision.HIGHEST,
                         preferred_element_type=F32) + b_ref[...]


def _mod_call(cc, w_ada, b_ada):
    rows, d = cc.shape
    n = w_ada.shape[1]
    tn = 1024
    return pl.pallas_call(
        _mod_kernel,
        grid=(n // tn,),
        in_specs=[pl.BlockSpec((rows, d), lambda j: (0, 0)),
                  pl.BlockSpec((d, tn), lambda j: (0, j)),
                  pl.BlockSpec((1, tn), lambda j: (0, j))],
        out_specs=pl.BlockSpec((rows, tn), lambda j: (0, j)),
        out_shape=jax.ShapeDtypeStruct((rows, n), F32),
        name="mod",
    )(cc, w_ada, b_ada)


def _inproj_kernel(x_ref, mod_ref, nw_ref, w_ref, wvt_ref, wgk_ref, bgk_ref,
                   q_ref, k_ref, v_ref, vt_ref, g_ref, p_ref, gk_ref, *, qk, gw, pw, dk):
    x = x_ref[0]
    m = mod_ref[0]
    hm = (_rmsnorm(x, nw_ref[...]) * (1.0 + m[1:2]) + m[0:1]).astype(BF16)
    p = _dot(hm, w_ref[...])
    vt = _dot_nt(wvt_ref[...], hm)
    for s in range(vt_ref.shape[1]):
        vt_ref[0, s] = vt[:, s * SUPER:(s + 1) * SUPER].astype(BF16)
    o = 0
    q_ref[0] = (p[:, o:o + qk] * (dk ** -0.5)).astype(BF16); o += qk
    k_ref[0] = p[:, o:o + qk].astype(BF16); o += qk
    v_ref[0] = p[:, o:o + gw].astype(BF16); o += gw
    g_ref[0] = p[:, o:o + gw].astype(BF16); o += gw
    p_ref[0] = p[:, o:o + pw].astype(BF16); o += pw
    r = p[:, o:o + LANES]
    z = _dot(r.astype(BF16), wgk_ref[...]) + bgk_ref[...]
    gk_ref[0] = (jnp.minimum(z, 0.0) - jnp.log1p(jnp.exp(-jnp.abs(z)))) * (1.0 / GATE_NORMALIZER)


def _inproj_call(x, mod, nw, w, wvt, wgk, bgk, *, qk, gw, pw, dk, tm):
    b, t, d = x.shape
    n_in = w.shape[1]
    bs = lambda width: pl.BlockSpec((1, tm, width), lambda i, j: (i, j, 0))
    const = lambda shape: pl.BlockSpec(shape, lambda i, j: (0,) * len(shape))
    per_batch = mod.shape[0] > 1
    sds = jax.ShapeDtypeStruct
    return pl.pallas_call(
        functools.partial(_inproj_kernel, qk=qk, gw=gw, pw=pw, dk=dk),
        grid=(b, t // tm),
        in_specs=[bs(d),
                  pl.BlockSpec((1, N_MOD, d), (lambda i, j: (i, 0, 0)) if per_batch else (lambda i, j: (0, 0, 0))),
                  const((1, d)), const((d, n_in)), const((gw, d)), const((LANES, 2 * qk)), const((1, 2 * qk))],
        out_specs=[bs(qk), bs(qk), bs(gw),
                   pl.BlockSpec((1, tm // SUPER, gw, SUPER), lambda i, j: (i, j, 0, 0)),
                   bs(gw), bs(pw), bs(2 * qk)],
        out_shape=[sds((b, t, qk), BF16), sds((b, t, qk), BF16), sds((b, t, gw), BF16),
                   sds((b, t // SUPER, gw, SUPER), BF16),
                   sds((b, t, gw), BF16), sds((b, t, pw), BF16), sds((b, t, 2 * qk), F32)],
        compiler_params=pltpu.CompilerParams(dimension_semantics=("arbitrary", "arbitrary")),
        name="inproj",
    )(x, mod, nw, w, wvt, wgk, bgk)


def _gla_super(q, k, v, vt, gk, cm, amask, bd_mask, st, fwd, want_out):
    nch = SUPER // GLA_CHUNK
    order = tuple(range(nch)) if fwd else tuple(reversed(range(nch)))
    last_row = GLA_CHUNK - 1 if fwd else 0
    mid_row = GLA_CHUNK // 2 - 1 if fwd else GLA_CHUNK // 2
    hi, lo = _split_bf16(gk)
    bcum = _dot(cm, hi) + _dot(cm, lo)

    def chunk_row(r):
        return jnp.concatenate(
            [jnp.broadcast_to(bcum[c * GLA_CHUNK + r:c * GLA_CHUNK + r + 1, :], (GLA_CHUNK, bcum.shape[1]))
             for c in range(nch)], axis=0)

    chunk_of_row = lax.broadcasted_iota(I32, bcum.shape, 0) // GLA_CHUNK

    def by_chunk(x):
        return jnp.concatenate([jnp.where(chunk_of_row == c, x, 0.0).astype(BF16) for c in range(nch)], axis=1)

    blast = chunk_row(last_row)
    u_all = _dot(vt, by_chunk(k * jnp.exp(blast - bcum)))
    before = [None] * nch
    for c in order:
        before[c] = st
        decay = jnp.exp(bcum[c * GLA_CHUNK + last_row:c * GLA_CHUNK + last_row + 1, :])
        st = st * decay + jnp.where(bd_mask, u_all[:, c * HEAD_PAIR_DK:(c + 1) * HEAD_PAIR_DK], 0.0)
    if not want_out:
        return None, st
    bmid = chunk_row(mid_row)
    qt = q * jnp.exp(bcum - bmid)
    kt = (k * jnp.exp(bmid - bcum)).astype(BF16)
    lane = lax.broadcasted_iota(I32, qt.shape, 1)
    half = HEAD_PAIR_DK // 2
    o_heads = []
    for hh in range(2):
        sel = (lane < half) if hh == 0 else (lane >= half)
        a = _dot_nt(jnp.where(sel, qt, 0.0).astype(BF16), kt)
        a = jnp.where(amask, a, 0.0).astype(BF16)
        o_heads.append(_dot(a, v[:, hh * LANES:(hh + 1) * LANES]))
    s_all = jnp.concatenate([s.astype(BF16) for s in before], axis=1)
    o_inter = _dot_nt(by_chunk(q * jnp.exp(bcum)), s_all)
    return jnp.concatenate(o_heads, axis=1) + o_inter, st


def _gla_kernel(q_ref, k_ref, v_ref, vt_ref, gkf_ref, gkb_ref, g_ref, kc_ref, vtc_ref, gkfc_ref, gkbc_ref,
                nw_ref, cmf_ref, cmb_ref, o_ref, stf_ref, stb_ref, of_ref, ob_ref):
    t = q_ref.shape[1]
    tc = kc_ref.shape[1]
    nsc, nscc = t // SUPER, tc // SUPER
    cmf = cmf_ref[...]
    cmb = cmb_ref[...]
    amask_f = cmf > 0
    amask_b = cmb > 0
    row = lax.broadcasted_iota(I32, (2 * LANES, HEAD_PAIR_DK), 0)
    lane = lax.broadcasted_iota(I32, (2 * LANES, HEAD_PAIR_DK), 1)
    bd_mask = (row < LANES) == (lane < HEAD_PAIR_DK // 2)

    def ctx_state(gk_ref, cm, fwd, j, st):
        rows = pl.ds(j * SUPER, SUPER)
        return _gla_super(None, kc_ref[0, rows, :].astype(F32), None, vtc_ref[0, j], gk_ref[0, rows, :],
                          cm, None, bd_mask, st, fwd, False)[1]

    def latent(gk_ref, cm, amask, fwd, j, st):
        rows = pl.ds(pl.multiple_of(j * SUPER, SUPER), SUPER)
        return _gla_super(q_ref[0, rows, :].astype(F32), k_ref[0, rows, :].astype(F32), v_ref[0, rows, :],
                          vt_ref[0, j], gk_ref[0, rows, :], cm, amask, bd_mask, st, fwd, True)

    st = jnp.zeros(stf_ref.shape, F32)
    for j in range(nscc):
        st = ctx_state(gkfc_ref, cmf, True, j, st)
    stf_ref[...] = st
    st = jnp.zeros(stb_ref.shape, F32)
    for j in reversed(range(nscc)):
        st = ctx_state(gkbc_ref, cmb, False, j, st)
    stb_ref[...] = st

    def scan_body(jj, carry):
        jb = nsc - 1 - jj
        of, stf = latent(gkf_ref, cmf, amask_f, True, jj, stf_ref[...])
        of_ref[pl.ds(pl.multiple_of(jj * SUPER, SUPER), SUPER), :] = of
        stf_ref[...] = stf
        ob, stb = latent(gkb_ref, cmb, amask_b, False, jb, stb_ref[...])
        ob_ref[pl.ds(pl.multiple_of(jb * SUPER, SUPER), SUPER), :] = ob
        stb_ref[...] = stb
        return carry

    lax.fori_loop(0, nsc, scan_body, 0)

    nw = nw_ref[...]

    def out_body(j, carry):
        rows = pl.ds(pl.multiple_of(j * SUPER, SUPER), SUPER)
        o = of_ref[rows, :] + ob_ref[rows, :]
        g = g_ref[0, rows, :].astype(F32)
        gate = g * jax.nn.sigmoid(g)
        for hh in range(2):
            oh = o[:, hh * LANES:(hh + 1) * LANES]
            on = oh * lax.rsqrt(jnp.mean(oh * oh, axis=-1, keepdims=True) + EPS) * nw
            o_ref[0, rows, hh * LANES:(hh + 1) * LANES] = (on * gate[:, hh * LANES:(hh + 1) * LANES]).astype(BF16)
        return carry

    lax.fori_loop(0, nsc, out_body, 0)


def _gla_masks():
    i = np.arange(SUPER)
    same = (i[:, None] // GLA_CHUNK) == (i[None, :] // GLA_CHUNK)
    fwd = same & (i[None, :] <= i[:, None])
    bwd = same & (i[None, :] >= i[:, None])
    return jnp.asarray(fwd, BF16), jnp.asarray(bwd, BF16)


def _gla_call(q, k, v, vt, gk, g, kc, vtc, gkc, nw):
    b, t, qk = q.shape
    tc = kc.shape[1]
    npair = qk // HEAD_PAIR_DK
    cmf, cmb = _gla_masks()
    lat = lambda width, off: pl.BlockSpec((1, t, width), lambda i, j: (i, 0, j + off))
    ctx = lambda width, off: pl.BlockSpec((1, tc, width), lambda i, j: (i, 0, j + off))
    tr = lambda n_groups: pl.BlockSpec((1, n_groups, 2 * LANES, SUPER), lambda i, j: (i, 0, j, 0))
    const = lambda shape: pl.BlockSpec(shape, lambda i, j: (0,) * len(shape))
    return pl.pallas_call(
        _gla_kernel,
        grid=(b, npair),
        in_specs=[lat(HEAD_PAIR_DK, 0), lat(HEAD_PAIR_DK, 0), lat(2 * LANES, 0), tr(t // SUPER),
                  lat(HEAD_PAIR_DK, 0), lat(HEAD_PAIR_DK, npair), lat(2 * LANES, 0),
                  ctx(HEAD_PAIR_DK, 0), tr(tc // SUPER), ctx(HEAD_PAIR_DK, 0), ctx(HEAD_PAIR_DK, npair),
                  const((1, LANES)), const(cmf.shape), const(cmb.shape)],
        out_specs=lat(2 * LANES, 0),
        out_shape=jax.ShapeDtypeStruct((b, t, v.shape[2]), BF16),
        scratch_shapes=[pltpu.VMEM((2 * LANES, HEAD_PAIR_DK), F32), pltpu.VMEM((2 * LANES, HEAD_PAIR_DK), F32),
                        pltpu.VMEM((t, 2 * LANES), F32), pltpu.VMEM((t, 2 * LANES), F32)],
        compiler_params=pltpu.CompilerParams(dimension_semantics=("arbitrary", "arbitrary")),
        name="gla",
    )(q, k, v, vt, gk, gk, g, kc, vtc, gkc, gkc, nw, cmf, cmb)


def _pool_kernel(x_ref, cm_ref, wp_ref, ps_ref, o_ref, y_ref, z_ref, *, rows):
    t = x_ref.shape[1]
    tok = lax.broadcasted_iota(I32, (t, LANES), 0)
    r = tok // GRID_W
    c = tok % GRID_W
    for gi, w in enumerate(POOL_WINDOWS):
        lo = w // 2
        hi = w - 1 - lo
        cols = slice(gi * LANES, (gi + 1) * LANES)
        cmat = cm_ref[gi]
        for blk in range(t // SUPER):
            rs = slice(blk * SUPER, (blk + 1) * SUPER)
            y_ref[rs, :] = _dot(cmat, x_ref[0, rs, cols])
        z_ref[...] = y_ref[...]
        for dr in range(-lo, hi + 1):
            sh = abs(dr) * GRID_W
            if dr == 0 or sh >= t:
                continue
            if dr > 0:
                z_ref[0:t - sh, :] += y_ref[sh:t, :]
            else:
                z_ref[sh:t, :] += y_ref[0:t - sh, :]
        cnt_r = jnp.minimum(r + hi + 1, rows) - jnp.maximum(r - lo, 0)
        cnt_c = jnp.minimum(c + hi + 1, GRID_W) - jnp.maximum(c - lo, 0)
        cnt = (cnt_r * cnt_c).astype(F32)
        pooled = z_ref[...] / cnt - x_ref[0, :, cols].astype(F32)
        yp = _dot(pooled.astype(BF16), wp_ref[gi]) * ps_ref[:, cols]
        o_ref[0, :, cols] = yp.astype(BF16)


def _pool_col_mats():
    i = np.arange(SUPER)
    same_row = (i[:, None] // GRID_W) == (i[None, :] // GRID_W)
    d = i[None, :] - i[:, None]
    mats = []
    for w in POOL_WINDOWS:
        lo = w // 2
        hi = w - 1 - lo
        mats.append(same_row & (d >= -lo) & (d <= hi))
    return jnp.asarray(np.stack(mats), BF16)


def _pool_call(xp, w_pool, pool_scale):
    b, t, pw = xp.shape
    ng = len(POOL_WINDOWS)
    cm = _pool_col_mats()
    const = lambda shape: pl.BlockSpec(shape, lambda i: (0,) * len(shape))
    return pl.pallas_call(
        functools.partial(_pool_kernel, rows=t // GRID_W),
        grid=(b,),
        in_specs=[pl.BlockSpec((1, t, pw), lambda i: (i, 0, 0)),
                  const(cm.shape), const((ng, LANES, LANES)), const((1, pw))],
        out_specs=pl.BlockSpec((1, t, pw), lambda i: (i, 0, 0)),
        out_shape=jax.ShapeDtypeStruct((b, t, pw), BF16),
        scratch_shapes=[pltpu.VMEM((t, LANES), F32), pltpu.VMEM((t, LANES), F32)],
        compiler_params=pltpu.CompilerParams(dimension_semantics=("arbitrary",)),
        name="pool",
    )(xp, cm, w_pool, pool_scale)


def _route_kernel(gla_ref, pool_ref, x_ref, mod_ref, wo_ref, nw_ref, wr_ref, br_ref, lt_ref,
                  h_ref, xt_ref, e_ref, rk_ref, wt_ref, cnt_ref, run_ref, *, gw, n_exp):
    i = pl.program_id(0)

    @pl.when(i == 0)
    def _():
        run_ref[...] = jnp.zeros_like(run_ref)

    m = mod_ref[0]
    acc = _dot(gla_ref[...], wo_ref[0:gw, :]) + _dot(pool_ref[...], wo_ref[gw:, :])
    h = x_ref[...] + m[2:3] * acc
    h_ref[...] = h
    xt = _rmsnorm(h, nw_ref[...]) * (1.0 + m[4:5]) + m[3:4]
    xt_ref[...] = _pack_bf16_pairs(xt)
    xh, xl = _split_bf16(xt)
    wr = wr_ref[...]
    wh, wl = _split_bf16(wr)
    logits = _dot(xh, wh) + _dot(xl, wh) + _dot(xh, wl) + br_ref[...]
    lane = lax.broadcasted_iota(I32, logits.shape, 1)
    neg = jnp.float32(-jnp.inf)
    logits = jnp.where(lane < n_exp, logits, neg)
    vals, hots = [], []
    e_out = jnp.zeros(logits.shape, I32)
    for j in range(TOP_K):
        mx = jnp.max(logits, axis=-1, keepdims=True)
        idx = jnp.min(jnp.where(logits == mx, lane, LANES), axis=-1, keepdims=True)
        hot = lane == idx
        vals.append(mx)
        hots.append(hot)
        e_out = jnp.where(lane == j, idx, e_out)
        logits = jnp.where(hot, neg, logits)
    ex = [jnp.exp(v - vals[0]) for v in vals]
    den = ex[0] + ex[1] + ex[2] + ex[3]
    w_out = jnp.zeros(logits.shape, F32)
    for j in range(TOP_K):
        w_out = jnp.where(lane == j, ex[j] / den, w_out)
    osum = jnp.where(hots[0] | hots[1] | hots[2] | hots[3], 1.0, 0.0)
    before = _dot(lt_ref[...], osum.astype(BF16)) + run_ref[0:1, :]
    rk_out = jnp.zeros(logits.shape, I32)
    for j in range(TOP_K):
        rj = jnp.sum(jnp.where(hots[j], before, 0.0), axis=-1, keepdims=True)
        rk_out = jnp.where(lane == j, rj.astype(I32), rk_out)
    run = run_ref[0:1, :] + jnp.sum(osum, axis=0, keepdims=True)
    run_ref[...] = jnp.broadcast_to(run, run_ref.shape)
    e_ref[...] = e_out
    rk_ref[...] = rk_out
    wt_ref[...] = w_out
    cnt_ref[...] = jnp.broadcast_to(run, cnt_ref.shape).astype(I32)


def _route_call(gla, pool, x, mod, w_out, nw, wr, br, *, tm, n_exp):
    n, d = x.shape
    gw = gla.shape[1]
    t_per_b = n // mod.shape[0]
    lt = jnp.asarray(np.tril(np.ones((tm, tm), np.float32), -1), BF16)
    row = lambda width: pl.BlockSpec((tm, width), lambda i: (i, 0))
    const = lambda shape: pl.BlockSpec(shape, lambda i: (0,) * len(shape))
    sds = jax.ShapeDtypeStruct
    return pl.pallas_call(
        functools.partial(_route_kernel, gw=gw, n_exp=n_exp),
        grid=(n // tm,),
        in_specs=[row(gw), row(pool.shape[1]), row(d),
                  pl.BlockSpec((1, N_MOD, d), lambda i: (i * tm // t_per_b, 0, 0)),
                  const(w_out.shape), const((1, d)), const(wr.shape), const((1, LANES)), const((tm, tm))],
        out_specs=[row(d), row(d // 2), row(LANES), row(LANES), row(LANES), const((8, LANES))],
        out_shape=[sds((n, d), F32), sds((n, d // 2), U32), sds((n, LANES), I32), sds((n, LANES), I32),
                   sds((n, LANES), F32), sds((8, LANES), I32)],
        scratch_shapes=[pltpu.VMEM((8, LANES), F32)],
        compiler_params=pltpu.CompilerParams(dimension_semantics=("arbitrary",)),
        name="route",
    )(gla, pool, x, mod, w_out, nw, wr, br, lt)


def _sc_worker_id():
    return lax.axis_index("s") * SC_CORES + lax.axis_index("c")


def _sc_scatter_call(x, idx3, *, n_out):
    n, d = x.shape
    n_win_total, k, w = idx3.shape
    n_win = n_win_total // SC_WORKERS
    mesh = plsc.VectorSubcoreMesh(core_axis_name="c", subcore_axis_name="s")

    @functools.partial(
        pl.kernel, mesh=mesh,
        out_type=jax.ShapeDtypeStruct((n_out, d), x.dtype),
        scratch_types=[pltpu.VMEM((k, w), I32), pltpu.VMEM((w, d), x.dtype), pltpu.SemaphoreType.DMA],
        name="sc_dispatch",
    )
    def kern(x_hbm, idx_hbm, out_hbm, idx_v, rows_v, sem):
        wid = _sc_worker_id()

        @pl.loop(0, n_win)
        def _(i):
            win = wid * n_win + i
            pltpu.sync_copy(idx_hbm.at[win], idx_v)
            pltpu.sync_copy(x_hbm.at[pl.ds(win * w, w)], rows_v)
            for j in range(k):
                pltpu.async_copy(rows_v, out_hbm.at[idx_v.at[j]], sem).wait()

    return kern(x, idx3)


def _sc_gather_call(table, idx3):
    n_win_total, _, w = idx3.shape
    d = table.shape[1]
    n_win = n_win_total // SC_WORKERS
    mesh = plsc.VectorSubcoreMesh(core_axis_name="c", subcore_axis_name="s")

    @functools.partial(
        pl.kernel, mesh=mesh,
        out_type=jax.ShapeDtypeStruct((n_win_total * w, d), table.dtype),
        scratch_types=[pltpu.VMEM((1, w), I32), pltpu.VMEM((w, d), table.dtype), pltpu.SemaphoreType.DMA],
        name="sc_gather",
    )
    def kern(table_hbm, idx_hbm, out_hbm, idx_v, rows_v, sem):
        wid = _sc_worker_id()

        @pl.loop(0, n_win)
        def _(i):
            win = wid * n_win + i
            pltpu.sync_copy(idx_hbm.at[win], idx_v)
            pltpu.async_copy(table_hbm.at[idx_v.at[0]], rows_v, sem).wait()
            pltpu.sync_copy(rows_v, out_hbm.at[pl.ds(win * w, w)])

    return kern(table, idx3)


def _expert_kernel(be_ref, nu_ref, nv_ref, x_ref, wgu_ref, bgu_ref, wd_ref, bd_ref, y_ref, wgu_bf_ref, wd_bf_ref,
                   *, d_ff):
    i = pl.program_id(0)

    @pl.when((i == 0) | (be_ref[i] != be_ref[jnp.maximum(i - 1, 0)]))
    def _():
        wgu_bf_ref[...] = wgu_ref[0].astype(BF16)
        wd_bf_ref[...] = wd_ref[0].astype(BF16)

    @pl.when(i < nu_ref[0])
    def _():
        row = lax.broadcasted_iota(I32, x_ref.shape, 0)
        lo, hi = _unpack_bf16_pairs(jnp.where(row < nv_ref[i], x_ref[...], jnp.uint32(0)))
        xb = jnp.concatenate([lo, hi], axis=1).astype(BF16)
        gu = _dot(xb, wgu_bf_ref[...]) + bgu_ref[0]
        gate = jnp.minimum(gu[:, :d_ff], SWIGLU_LIMIT)
        up = jnp.clip(gu[:, d_ff:], -SWIGLU_LIMIT, SWIGLU_LIMIT)
        act = (up + 1.0) * gate * jax.nn.sigmoid(SWIGLU_ALPHA * gate)
        y_ref[...] = _pack_bf16_pairs(_dot(act.astype(BF16), wd_bf_ref[...]) + bd_ref[0])


def _expert_call(block_e, n_used, n_valid, xs, w_gu, b_gu, w_down, b_down):
    n_pad = xs.shape[0]
    n_exp, d, two_ff = w_gu.shape
    d_ff = two_ff // 2
    nblk = n_pad // EXPERT_ROWS
    rows = lambda i, be, nu, nv: (jnp.minimum(i, nu[0] - 1), 0)
    per_e = lambda i, be, nu, nv: (be[i], 0, 0)
    grid_spec = pltpu.PrefetchScalarGridSpec(
        num_scalar_prefetch=3,
        grid=(nblk,),
        in_specs=[pl.BlockSpec((EXPERT_ROWS, d // 2), rows),
                  pl.BlockSpec((1, d, two_ff), per_e), pl.BlockSpec((1, 1, two_ff), per_e),
                  pl.BlockSpec((1, d_ff, d), per_e), pl.BlockSpec((1, 1, d), per_e)],
        out_specs=pl.BlockSpec((EXPERT_ROWS, d // 2), rows),
        scratch_shapes=[pltpu.VMEM((d, two_ff), BF16), pltpu.VMEM((d_ff, d), BF16)],
    )
    return pl.pallas_call(
        functools.partial(_expert_kernel, d_ff=d_ff),
        grid_spec=grid_spec,
        out_shape=jax.ShapeDtypeStruct((n_pad, d // 2), U32),
        compiler_params=pltpu.CompilerParams(dimension_semantics=("arbitrary",),
                                             vmem_limit_bytes=EXPERT_VMEM_BYTES),
        name="experts",
    )(block_e, n_used, n_valid, xs, w_gu, b_gu.reshape(n_exp, 1, two_ff), w_down, b_down.reshape(n_exp, 1, d))


def _combine_kernel(y4_ref, wt_ref, h_ref, mod_ref, fw_ref, o_ref):
    wt = wt_ref[...]
    acc_lo, acc_hi = None, None
    for j in range(TOP_K):
        lo, hi = _unpack_bf16_pairs(y4_ref[j])
        w = wt[:, j:j + 1]
        acc_lo = w * lo if j == 0 else acc_lo + w * lo
        acc_hi = w * hi if j == 0 else acc_hi + w * hi
    acc = jnp.concatenate([acc_lo, acc_hi], axis=1)
    m = mod_ref[0]
    o_ref[...] = _rmsnorm(h_ref[...] + m[5:6] * acc, fw_ref[...])


def _combine_call(y4, wts, h, mod, fw, *, tg):
    n, d = h.shape
    t_per_b = n // mod.shape[0]
    row = lambda width: pl.BlockSpec((tg, width), lambda i: (i, 0))
    return pl.pallas_call(
        _combine_kernel,
        grid=(n // tg,),
        in_specs=[pl.BlockSpec((TOP_K, tg, d // 2), lambda i: (0, i, 0)),
                  row(LANES), row(d),
                  pl.BlockSpec((1, N_MOD, d), lambda i: (i * tg // t_per_b, 0, 0)),
                  pl.BlockSpec((1, d), lambda i: (0, 0))],
        out_specs=row(d),
        out_shape=jax.ShapeDtypeStruct((n, d), F32),
        compiler_params=pltpu.CompilerParams(dimension_semantics=("arbitrary",)),
        name="combine",
    )(y4, wts, h, mod, fw)


def kernel(x, c, ctx, c_ctx, w_ada, b_ada, norm_mix_w, norm_mlp_w, w_in, w_gk_f, b_gk_f, w_gk_b, b_gk_b,
           gla_norm_w, w_pool, pool_scale, w_out, w_router, b_router, w_gu, b_gu, w_down, b_down,
           final_norm_w):
    b, t, d = x.shape
    assert w_ada.shape[0] == 1, "single-layer trunk"
    n_exp = w_router.shape[2]
    rank = w_gk_f.shape[1]
    qk = w_gk_f.shape[2]
    dk = qk // GLA_HEADS
    gw = GLA_HEADS * gla_norm_w.shape[1]
    pw = w_pool.shape[1] * w_pool.shape[2]
    assert w_in.shape[2] == 2 * qk + 2 * gw + 2 * rank + pw and 2 * rank <= LANES
    assert t % SUPER == 0 and ctx.shape[1] % SUPER == 0 and n_exp <= LANES

    rows = -(-(b + 1) // 8) * 8
    cc = jnp.zeros((rows, d), F32).at[:b].set(c).at[b].set(c_ctx)
    mod = _mod_call(cc, w_ada[0], b_ada)
    mod_x = mod[:b].reshape(b, N_MOD, d)
    mod_c = mod[b:b + 1].reshape(1, N_MOD, d)

    wi = w_in[0]
    o_r = 2 * qk + 2 * gw
    w_cat = jnp.concatenate([wi[:, :o_r], wi[:, o_r + 2 * rank:], wi[:, o_r:o_r + 2 * rank],
                             jnp.zeros((d, LANES - 2 * rank), F32)], axis=1).astype(BF16)
    wgk = jnp.zeros((LANES, 2 * qk), F32).at[:rank, :qk].set(w_gk_f[0]).at[rank:2 * rank, qk:].set(w_gk_b[0])
    bgk = jnp.concatenate([b_gk_f[0], b_gk_b[0]])[None, :]
    w_vt = wi[:, 2 * qk:2 * qk + gw].T.astype(BF16)
    proj = functools.partial(_inproj_call, nw=norm_mix_w, w=w_cat, wvt=w_vt, wgk=wgk.astype(BF16), bgk=bgk,
                             qk=qk, gw=gw, pw=pw, dk=dk)
    q, k, v, vt, g, xp, gk = proj(x, mod_x, tm=512)
    _, kc, _, vtc, _, _, gkc = proj(ctx, mod_c, tm=SUPER)

    gla = _gla_call(q, k, v, vt, gk, g, kc, vtc, gkc, gla_norm_w)
    pool = _pool_call(xp, w_pool[0].astype(BF16), pool_scale)

    n = b * t
    wr = jnp.zeros((d, LANES), F32).at[:, :n_exp].set(w_router[0])
    br = jnp.zeros((1, LANES), F32).at[0, :n_exp].set(b_router[0])
    h, xt, e_idx, rk, wts, cnt = _route_call(
        gla.reshape(n, gw), pool.reshape(n, pw), x.reshape(n, d), mod_x, w_out[0].astype(BF16),
        norm_mlp_w, wr, br, tm=512, n_exp=n_exp)

    counts = cnt[0, :n_exp]
    padded = (counts + EXPERT_ROWS - 1) // EXPERT_ROWS * EXPERT_ROWS
    pad_end = jnp.cumsum(padded)
    start_pad = pad_end - padded
    dest = (jnp.take(start_pad, e_idx[:, :TOP_K]) + rk[:, :TOP_K]).astype(I32)
    n_pad = n * TOP_K + n_exp * EXPERT_ROWS
    nblk = n_pad // EXPERT_ROWS
    n_used = (pad_end[-1] // EXPERT_ROWS).astype(I32)
    blk = jnp.arange(nblk, dtype=I32)
    block_e = jnp.sum((pad_end[None, :] <= (blk * EXPERT_ROWS)[:, None]).astype(I32), axis=1)
    block_e = jnp.minimum(block_e, n_exp - 1)
    block_e = jnp.where(blk < n_used, block_e, block_e[jnp.maximum(n_used - 1, 0)])

    n_valid = jnp.clip(jnp.take(start_pad + counts, block_e) - blk * EXPERT_ROWS, 0, EXPERT_ROWS).astype(I32)

    assert n % (SC_WORKERS * SC_WINDOW) == 0
    idx_scatter = dest.reshape(n // SC_WINDOW, SC_WINDOW, TOP_K).transpose(0, 2, 1)
    xs = _sc_scatter_call(xt, idx_scatter, n_out=n_pad)
    ys = _expert_call(block_e, n_used.reshape(1), n_valid, xs, w_gu[0], b_gu[0], w_down[0], b_down[0])
    idx_gather = dest.T.reshape(n * TOP_K // SC_WINDOW, 1, SC_WINDOW)
    y4 = _sc_gather_call(ys, idx_gather).reshape(TOP_K, n, d // 2)
    out = _combine_call(y4, wts, h, mod_x, final_norm_w[None, :], tg=256)
    return out.reshape(b, t, d)
```

```python
import functools

import numpy as np
import jax
import jax.numpy as jnp
from jax import lax
from jax.experimental import pallas as pl
from jax.experimental.pallas import tpu as pltpu
from jax.experimental.pallas import tpu_sc as plsc

F32 = jnp.float32
BF16 = jnp.bfloat16
I32 = jnp.int32
U32 = jnp.uint32

GRID_W = 64
GLA_HEADS = 4
GLA_CHUNK = 64
GATE_NORMALIZER = 16.0
POOL_WINDOWS = (2, 4, 8, 16)
TOP_K = 4
SWIGLU_LIMIT = 7.0
SWIGLU_ALPHA = 1.702
N_MOD = 6
EPS = 1e-6

LANES = 128
SUPER = 256
HEAD_PAIR_DK = 128
EXPERT_ROWS = 512
EXPERT_VMEM_BYTES = 56 * 1024 * 1024
SC_CORES = 2
SC_SUBCORES = 16
SC_WORKERS = SC_CORES * SC_SUBCORES
SC_WINDOW = 32
SC_GATHER_WINDOW = 64


def _dot(a, b):
    return jnp.dot(a, b, preferred_element_type=F32)


def _dot_nt(a, b):
    return lax.dot_general(a, b, (((1,), (1,)), ((), ())), preferred_element_type=F32)


def _split_bf16(x):
    hi = x.astype(BF16)
    lo = (x - hi.astype(F32)).astype(BF16)
    return hi, lo


def _pack_bf16_pairs(x):
    c = x.shape[1] // 2
    lo = lax.bitcast_convert_type(x[:, :c].astype(BF16).astype(F32), U32)
    hi = lax.bitcast_convert_type(x[:, c:].astype(BF16).astype(F32), U32)
    return (lo >> 16) | hi


def _unpack_bf16_pairs(p):
    lo = lax.bitcast_convert_type(p << 16, F32)
    hi = lax.bitcast_convert_type(p & jnp.uint32(0xFFFF0000), F32)
    return lo, hi


def _rmsnorm(x, w):
    var = jnp.mean(x * x, axis=-1, keepdims=True)
    return x * lax.rsqrt(var + EPS) * w


def _mod_kernel(c_ref, w_ref, b_ref, o_ref):
    c = c_ref[...]
    s = c * jax.nn.sigmoid(c)
    o_ref[...] = jnp.dot(s, w_ref[...], precision=lax.Precision.HIGHEST,
                         preferred_element_type=F32) + b_ref[...]


def _mod_call(cc, w_ada, b_ada):
    rows, d = cc.shape
    n = w_ada.shape[1]
    tn = 1024
    return pl.pallas_call(
        _mod_kernel,
        grid=(n // tn,),
        in_specs=[pl.BlockSpec((rows, d), lambda j: (0, 0)),
                  pl.BlockSpec((d, tn), lambda j: (0, j)),
                  pl.BlockSpec((1, tn), lambda j: (0, j))],
        out_specs=pl.BlockSpec((rows, tn), lambda j: (0, j)),
        out_shape=jax.ShapeDtypeStruct((rows, n), F32),
        name="mod",
    )(cc, w_ada, b_ada)


def _inproj_kernel(x_ref, mod_ref, nw_ref, w_ref, wvt_ref, wgk_ref, bgk_ref,
                   q_ref, k_ref, v_ref, vt_ref, g_ref, p_ref, gk_ref, *, qk, gw, pw, dk):
    x = x_ref[0]
    m = mod_ref[0]
    hm = (_rmsnorm(x, nw_ref[...]) * (1.0 + m[1:2]) + m[0:1]).astype(BF16)
    p = _dot(hm, w_ref[...])
    vt = _dot_nt(wvt_ref[...], hm)
    for s in range(vt_ref.shape[1]):
        vt_ref[0, s] = vt[:, s * SUPER:(s + 1) * SUPER].astype(BF16)
    o = 0
    q_ref[0] = (p[:, o:o + qk] * (dk ** -0.5)).astype(BF16); o += qk
    k_ref[0] = p[:, o:o + qk].astype(BF16); o += qk
    v_ref[0] = p[:, o:o + gw].astype(BF16); o += gw
    g_ref[0] = p[:, o:o + gw].astype(BF16); o += gw
    p_ref[0] = p[:, o:o + pw].astype(BF16); o += pw
    r = p[:, o:o + LANES]
    z = _dot(r.astype(BF16), wgk_ref[...]) + bgk_ref[...]
    gk_ref[0] = (jnp.minimum(z, 0.0) - jnp.log1p(jnp.exp(-jnp.abs(z)))) * (1.0 / GATE_NORMALIZER)


def _inproj_call(x, mod, nw, w, wvt, wgk, bgk, *, qk, gw, pw, dk, tm):
    b, t, d = x.shape
    n_in = w.shape[1]
    bs = lambda width: pl.BlockSpec((1, tm, width), lambda i, j: (i, j, 0))
    const = lambda shape: pl.BlockSpec(shape, lambda i, j: (0,) * len(shape))
    per_batch = mod.shape[0] > 1
    sds = jax.ShapeDtypeStruct
    return pl.pallas_call(
        functools.partial(_inproj_kernel, qk=qk, gw=gw, pw=pw, dk=dk),
        grid=(b, t // tm),
        in_specs=[bs(d),
                  pl.BlockSpec((1, N_MOD, d), (lambda i, j: (i, 0, 0)) if per_batch else (lambda i, j: (0, 0, 0))),
                  const((1, d)), const((d, n_in)), const((gw, d)), const((LANES, 2 * qk)), const((1, 2 * qk))],
        out_specs=[bs(qk), bs(qk), bs(gw),
                   pl.BlockSpec((1, tm // SUPER, gw, SUPER), lambda i, j: (i, j, 0, 0)),
                   bs(gw), bs(pw), bs(2 * qk)],
        out_shape=[sds((b, t, qk), BF16), sds((b, t, qk), BF16), sds((b, t, gw), BF16),
                   sds((b, t // SUPER, gw, SUPER), BF16),
                   sds((b, t, gw), BF16), sds((b, t, pw), BF16), sds((b, t, 2 * qk), F32)],
        compiler_params=pltpu.CompilerParams(dimension_semantics=("arbitrary", "arbitrary")),
        name="inproj",
    )(x, mod, nw, w, wvt, wgk, bgk)


def _gla_super(q, k, v, vt, gk, cm, amask, bd_mask, st, fwd, want_out):
    nch = SUPER // GLA_CHUNK
    order = tuple(range(nch)) if fwd else tuple(reversed(range(nch)))
    last_row = GLA_CHUNK - 1 if fwd else 0
    mid_row = GLA_CHUNK // 2 - 1 if fwd else GLA_CHUNK // 2
    hi, lo = _split_bf16(gk)
    bcum = _dot(cm, hi) + _dot(cm, lo)

    def chunk_row(r):
        return jnp.concatenate(
            [jnp.broadcast_to(bcum[c * GLA_CHUNK + r:c * GLA_CHUNK + r + 1, :], (GLA_CHUNK, bcum.shape[1]))
             for c in range(nch)], axis=0)

    chunk_of_row = lax.broadcasted_iota(I32, bcum.shape, 0) // GLA_CHUNK

    def by_chunk(x):
        return jnp.concatenate([jnp.where(chunk_of_row == c, x, 0.0).astype(BF16) for c in range(nch)], axis=1)

    blast = chunk_row(last_row)
    u_all = _dot(vt, by_chunk(k * jnp.exp(blast - bcum)))
    before = [None] * nch
    for c in order:
        before[c] = st
        decay = jnp.exp(bcum[c * GLA_CHUNK + last_row:c * GLA_CHUNK + last_row + 1, :])
        st = st * decay + jnp.where(bd_mask, u_all[:, c * HEAD_PAIR_DK:(c + 1) * HEAD_PAIR_DK], 0.0)
    if not want_out:
        return None, st
    bmid = chunk_row(mid_row)
    qt = q * jnp.exp(bcum - bmid)
    kt = (k * jnp.exp(bmid - bcum)).astype(BF16)
    lane = lax.broadcasted_iota(I32, qt.shape, 1)
    half = HEAD_PAIR_DK // 2
    o_heads = []
    for hh in range(2):
        sel = (lane < half) if hh == 0 else (lane >= half)
        a = _dot_nt(jnp.where(sel, qt, 0.0).astype(BF16), kt)
        a = jnp.where(amask, a, 0.0).astype(BF16)
        o_heads.append(_dot(a, v[:, hh * LANES:(hh + 1) * LANES]))
    s_all = jnp.concatenate([s.astype(BF16) for s in before], axis=1)
    o_inter = _dot_nt(by_chunk(q * jnp.exp(bcum)), s_all)
    return jnp.concatenate(o_heads, axis=1) + o_inter, st


def _gla_kernel(q_ref, k_ref, v_ref, vt_ref, gkf_ref, gkb_ref, g_ref, kc_ref, vtc_ref, gkfc_ref, gkbc_ref,
                nw_ref, cmf_ref, cmb_ref, o_ref, stf_ref, stb_ref, of_ref, ob_ref):
    t = q_ref.shape[1]
    tc = kc_ref.shape[1]
    nsc, nscc = t // SUPER, tc // SUPER
    cmf = cmf_ref[...]
    cmb = cmb_ref[...]
    amask_f = cmf > 0
    amask_b = cmb > 0
    row = lax.broadcasted_iota(I32, (2 * LANES, HEAD_PAIR_DK), 0)
    lane = lax.broadcasted_iota(I32, (2 * LANES, HEAD_PAIR_DK), 1)
    bd_mask = (row < LANES) == (lane < HEAD_PAIR_DK // 2)

    def ctx_state(gk_ref, cm, fwd, j, st):
        rows = pl.ds(j * SUPER, SUPER)
        return _gla_super(None, kc_ref[0, rows, :].astype(F32), None, vtc_ref[0, j], gk_ref[0, rows, :],
                          cm, None, bd_mask, st, fwd, False)[1]

    def latent(gk_ref, cm, amask, fwd, j, st):
        rows = pl.ds(pl.multiple_of(j * SUPER, SUPER), SUPER)
        return _gla_super(q_ref[0, rows, :].astype(F32), k_ref[0, rows, :].astype(F32), v_ref[0, rows, :],
                          vt_ref[0, j], gk_ref[0, rows, :], cm, amask, bd_mask, st, fwd, True)

    st = jnp.zeros(stf_ref.shape, F32)
    for j in range(nscc):
        st = ctx_state(gkfc_ref, cmf, True, j, st)
    stf_ref[...] = st
    st = jnp.zeros(stb_ref.shape, F32)
    for j in reversed(range(nscc)):
        st = ctx_state(gkbc_ref, cmb, False, j, st)
    stb_ref[...] = st

    def scan_body(jj, carry):
        jb = nsc - 1 - jj
        of, stf = latent(gkf_ref, cmf, amask_f, True, jj, stf_ref[...])
        of_ref[pl.ds(pl.multiple_of(jj * SUPER, SUPER), SUPER), :] = of
        stf_ref[...] = stf
        ob, stb = latent(gkb_ref, cmb, amask_b, False, jb, stb_ref[...])
        ob_ref[pl.ds(pl.multiple_of(jb * SUPER, SUPER), SUPER), :] = ob
        stb_ref[...] = stb
        return carry

    lax.fori_loop(0, nsc, scan_body, 0)

    nw = nw_ref[...]

    def out_body(j, carry):
        rows = pl.ds(pl.multiple_of(j * SUPER, SUPER), SUPER)
        o = of_ref[rows, :] + ob_ref[rows, :]
        g = g_ref[0, rows, :].astype(F32)
        gate = g * jax.nn.sigmoid(g)
        for hh in range(2):
            oh = o[:, hh * LANES:(hh + 1) * LANES]
            on = oh * lax.rsqrt(jnp.mean(oh * oh, axis=-1, keepdims=True) + EPS) * nw
            o_ref[0, rows, hh * LANES:(hh + 1) * LANES] = (on * gate[:, hh * LANES:(hh + 1) * LANES]).astype(BF16)
        return carry

    lax.fori_loop(0, nsc, out_body, 0)


def _gla_masks():
    i = np.arange(SUPER)
    same = (i[:, None] // GLA_CHUNK) == (i[None, :] // GLA_CHUNK)
    fwd = same & (i[None, :] <= i[:, None])
    bwd = same & (i[None, :] >= i[:, None])
    return jnp.asarray(fwd, BF16), jnp.asarray(bwd, BF16)


def _gla_call(q, k, v, vt, gk, g, kc, vtc, gkc, nw):
    b, t, qk = q.shape
    tc = kc.shape[1]
    npair = qk // HEAD_PAIR_DK
    cmf, cmb = _gla_masks()
    lat = lambda width, off: pl.BlockSpec((1, t, width), lambda i, j: (i, 0, j + off))
    ctx = lambda width, off: pl.BlockSpec((1, tc, width), lambda i, j: (i, 0, j + off))
    tr = lambda n_groups: pl.BlockSpec((1, n_groups, 2 * LANES, SUPER), lambda i, j: (i, 0, j, 0))
    const = lambda shape: pl.BlockSpec(shape, lambda i, j: (0,) * len(shape))
    return pl.pallas_call(
        _gla_kernel,
        grid=(b, npair),
        in_specs=[lat(HEAD_PAIR_DK, 0), lat(HEAD_PAIR_DK, 0), lat(2 * LANES, 0), tr(t // SUPER),
                  lat(HEAD_PAIR_DK, 0), lat(HEAD_PAIR_DK, npair), lat(2 * LANES, 0),
                  ctx(HEAD_PAIR_DK, 0), tr(tc // SUPER), ctx(HEAD_PAIR_DK, 0), ctx(HEAD_PAIR_DK, npair),
                  const((1, LANES)), const(cmf.shape), const(cmb.shape)],
        out_specs=lat(2 * LANES, 0),
        out_shape=jax.ShapeDtypeStruct((b, t, v.shape[2]), BF16),
        scratch_shapes=[pltpu.VMEM((2 * LANES, HEAD_PAIR_DK), F32), pltpu.VMEM((2 * LANES, HEAD_PAIR_DK), F32),
                        pltpu.VMEM((t, 2 * LANES), F32), pltpu.VMEM((t, 2 * LANES), F32)],
        compiler_params=pltpu.CompilerParams(dimension_semantics=("arbitrary", "arbitrary")),
        name="gla",
    )(q, k, v, vt, gk, gk, g, kc, vtc, gkc, gkc, nw, cmf, cmb)


def _pool_kernel(x_ref, cm_ref, wp_ref, ps_ref, o_ref, y_ref, z_ref, *, rows):
    t = x_ref.shape[1]
    tok = lax.broadcasted_iota(I32, (t, LANES), 0)
    r = tok // GRID_W
    c = tok % GRID_W
    for gi, w in enumerate(POOL_WINDOWS):
        lo = w // 2
        hi = w - 1 - lo
        cols = slice(gi * LANES, (gi + 1) * LANES)
        cmat = cm_ref[gi]
        for blk in range(t // SUPER):
            rs = slice(blk * SUPER, (blk + 1) * SUPER)
            y_ref[rs, :] = _dot(cmat, x_ref[0, rs, cols])
        z_ref[...] = y_ref[...]
        for dr in range(-lo, hi + 1):
            sh = abs(dr) * GRID_W
            if dr == 0 or sh >= t:
                continue
            if dr > 0:
                z_ref[0:t - sh, :] += y_ref[sh:t, :]
            else:
                z_ref[sh:t, :] += y_ref[0:t - sh, :]
        cnt_r = jnp.minimum(r + hi + 1, rows) - jnp.maximum(r - lo, 0)
        cnt_c = jnp.minimum(c + hi + 1, GRID_W) - jnp.maximum(c - lo, 0)
        cnt = (cnt_r * cnt_c).astype(F32)
        pooled = z_ref[...] / cnt - x_ref[0, :, cols].astype(F32)
        yp = _dot(pooled.astype(BF16), wp_ref[gi]) * ps_ref[:, cols]
        o_ref[0, :, cols] = yp.astype(BF16)


def _pool_col_mats():
    i = np.arange(SUPER)
    same_row = (i[:, None] // GRID_W) == (i[None, :] // GRID_W)
    d = i[None, :] - i[:, None]
    mats = []
    for w in POOL_WINDOWS:
        lo = w // 2
        hi = w - 1 - lo
        mats.append(same_row & (d >= -lo) & (d <= hi))
    return jnp.asarray(np.stack(mats), BF16)


def _pool_call(xp, w_pool, pool_scale):
    b, t, pw = xp.shape
    ng = len(POOL_WINDOWS)
    cm = _pool_col_mats()
    const = lambda shape: pl.BlockSpec(shape, lambda i: (0,) * len(shape))
    return pl.pallas_call(
        functools.partial(_pool_kernel, rows=t // GRID_W),
        grid=(b,),
        in_specs=[pl.BlockSpec((1, t, pw), lambda i: (i, 0, 0)),
                  const(cm.shape), const((ng, LANES, LANES)), const((1, pw))],
        out_specs=pl.BlockSpec((1, t, pw), lambda i: (i, 0, 0)),
        out_shape=jax.ShapeDtypeStruct((b, t, pw), BF16),
        scratch_shapes=[pltpu.VMEM((t, LANES), F32), pltpu.VMEM((t, LANES), F32)],
        compiler_params=pltpu.CompilerParams(dimension_semantics=("arbitrary",)),
        name="pool",
    )(xp, cm, w_pool, pool_scale)


def _route_kernel(gla_ref, pool_ref, x_ref, mod_ref, wo_ref, nw_ref, wr_ref, br_ref, lt_ref,
                  h_ref, xt_ref, e_ref, rk_ref, wt_ref, cnt_ref, run_ref, *, gw, n_exp):
    i = pl.program_id(0)

    @pl.when(i == 0)
    def _():
        run_ref[...] = jnp.zeros_like(run_ref)

    m = mod_ref[0]
    acc = _dot(gla_ref[...], wo_ref[0:gw, :]) + _dot(pool_ref[...], wo_ref[gw:, :])
    h = x_ref[...] + m[2:3] * acc
    h_ref[...] = h
    xt = _rmsnorm(h, nw_ref[...]) * (1.0 + m[4:5]) + m[3:4]
    xt_ref[...] = _pack_bf16_pairs(xt)
    xh, xl = _split_bf16(xt)
    wr = wr_ref[...]
    wh, wl = _split_bf16(wr)
    logits = _dot(xh, wh) + _dot(xl, wh) + _dot(xh, wl) + br_ref[...]
    lane = lax.broadcasted_iota(I32, logits.shape, 1)
    neg = jnp.float32(-jnp.inf)
    logits = jnp.where(lane < n_exp, logits, neg)
    vals, hots = [], []
    e_out = jnp.zeros(logits.shape, I32)
    for j in range(TOP_K):
        mx = jnp.max(logits, axis=-1, keepdims=True)
        idx = jnp.min(jnp.where(logits == mx, lane, LANES), axis=-1, keepdims=True)
        hot = lane == idx
        vals.append(mx)
        hots.append(hot)
        e_out = jnp.where(lane == j, idx, e_out)
        logits = jnp.where(hot, neg, logits)
    ex = [jnp.exp(v - vals[0]) for v in vals]
    den = ex[0] + ex[1] + ex[2] + ex[3]
    w_out = jnp.zeros(logits.shape, F32)
    for j in range(TOP_K):
        w_out = jnp.where(lane == j, ex[j] / den, w_out)
    osum = jnp.where(hots[0] | hots[1] | hots[2] | hots[3], 1.0, 0.0)
    before = _dot(lt_ref[...], osum.astype(BF16)) + run_ref[0:1, :]
    rk_out = jnp.zeros(logits.shape, I32)
    for j in range(TOP_K):
        rj = jnp.sum(jnp.where(hots[j], before, 0.0), axis=-1, keepdims=True)
        rk_out = jnp.where(lane == j, rj.astype(I32), rk_out)
    run = run_ref[0:1, :] + jnp.sum(osum, axis=0, keepdims=True)
    run_ref[...] = jnp.broadcast_to(run, run_ref.shape)
    e_ref[...] = e_out
    rk_ref[...] = rk_out
    wt_ref[...] = w_out
    cnt_ref[...] = jnp.broadcast_to(run, cnt_ref.shape).astype(I32)


def _route_call(gla, pool, x, mod, w_out, nw, wr, br, *, tm, n_exp):
    n, d = x.shape
    gw = gla.shape[1]
    t_per_b = n // mod.shape[0]
    lt = jnp.asarray(np.tril(np.ones((tm, tm), np.float32), -1), BF16)
    row = lambda width: pl.BlockSpec((tm, width), lambda i: (i, 0))
    const = lambda shape: pl.BlockSpec(shape, lambda i: (0,) * len(shape))
    sds = jax.ShapeDtypeStruct
    return pl.pallas_call(
        functools.partial(_route_kernel, gw=gw, n_exp=n_exp),
        grid=(n // tm,),
        in_specs=[row(gw), row(pool.shape[1]), row(d),
                  pl.BlockSpec((1, N_MOD, d), lambda i: (i * tm // t_per_b, 0, 0)),
                  const(w_out.shape), const((1, d)), const(wr.shape), const((1, LANES)), const((tm, tm))],
        out_specs=[row(d), row(d // 2), row(LANES), row(LANES), row(LANES), const((8, LANES))],
        out_shape=[sds((n, d), F32), sds((n, d // 2), U32), sds((n, LANES), I32), sds((n, LANES), I32),
                   sds((n, LANES), F32), sds((8, LANES), I32)],
        scratch_shapes=[pltpu.VMEM((8, LANES), F32)],
        compiler_params=pltpu.CompilerParams(dimension_semantics=("arbitrary",)),
        name="route",
    )(gla, pool, x, mod, w_out, nw, wr, br, lt)


def _sc_worker_id():
    return lax.axis_index("s") * SC_CORES + lax.axis_index("c")


def _sc_scatter_call(x, idx3, *, n_out):
    n, d = x.shape
    n_win_total, k, w = idx3.shape
    n_win = n_win_total // SC_WORKERS
    mesh = plsc.VectorSubcoreMesh(core_axis_name="c", subcore_axis_name="s")

    @functools.partial(
        pl.kernel, mesh=mesh,
        out_type=jax.ShapeDtypeStruct((n_out, d), x.dtype),
        scratch_types=[pltpu.VMEM((k, w), I32), pltpu.VMEM((w, d), x.dtype), pltpu.SemaphoreType.DMA],
        name="sc_dispatch",
    )
    def kern(x_hbm, idx_hbm, out_hbm, idx_v, rows_v, sem):
        wid = _sc_worker_id()

        @pl.loop(0, n_win)
        def _(i):
            win = wid * n_win + i
            pltpu.sync_copy(idx_hbm.at[win], idx_v)
            pltpu.sync_copy(x_hbm.at[pl.ds(win * w, w)], rows_v)
            for j in range(k):
                pltpu.async_copy(rows_v, out_hbm.at[idx_v.at[j]], sem).wait()

    return kern(x, idx3)


def _sc_gather_call(table, idx3):
    n_workers, n_win, w = idx3.shape
    d = table.shape[1]
    assert n_workers == SC_WORKERS and n_win % 2 == 0
    mesh = plsc.VectorSubcoreMesh(core_axis_name="c", subcore_axis_name="s")

    @functools.partial(
        pl.kernel, mesh=mesh,
        out_type=jax.ShapeDtypeStruct((n_workers * n_win * w, d), table.dtype),
        scratch_types=[pltpu.VMEM((n_win, w), I32), pltpu.VMEM((2, w, d), table.dtype),
                       pltpu.SemaphoreType.DMA((2,)), pltpu.SemaphoreType.DMA((2,))],
        name="sc_gather",
    )
    def kern(table_hbm, idx_hbm, out_hbm, idx_v, rows_v, gsem, osem):
        wid = _sc_worker_id()
        base = wid * n_win
        pltpu.sync_copy(idx_hbm.at[wid], idx_v)

        def gather(wi, b):
            return pltpu.make_async_copy(table_hbm.at[idx_v.at[wi]], rows_v.at[b], gsem.at[b])

        def put(wi, b):
            return pltpu.make_async_copy(rows_v.at[b], out_hbm.at[pl.ds((base + wi) * w, w)], osem.at[b])

        gather(0, 0).start()

        @pl.loop(0, n_win, step=2)
        def _(i):
            for b in range(2):
                wi = i + b

                @pl.when(wi + 1 < n_win)
                def _():
                    @pl.when(wi >= 1)
                    def _():
                        put(wi - 1, 1 - b).wait()
                    gather(wi + 1, 1 - b).start()

                gather(wi, b).wait()
                put(wi, b).start()

        put(n_win - 2, 0).wait()
        put(n_win - 1, 1).wait()

    return kern(table, idx3)


def _expert_kernel(be_ref, nu_ref, nv_ref, x_ref, wgu_ref, bgu_ref, wd_ref, bd_ref, y_ref, wgu_bf_ref, wd_bf_ref,
                   *, d_ff):
    i = pl.program_id(0)

    @pl.when((i == 0) | (be_ref[i] != be_ref[jnp.maximum(i - 1, 0)]))
    def _():
        wgu_bf_ref[...] = wgu_ref[0].astype(BF16)
        wd_bf_ref[...] = wd_ref[0].astype(BF16)

    @pl.when(i < nu_ref[0])
    def _():
        row = lax.broadcasted_iota(I32, x_ref.shape, 0)
        lo, hi = _unpack_bf16_pairs(jnp.where(row < nv_ref[i], x_ref[...], jnp.uint32(0)))
        xb = jnp.concatenate([lo, hi], axis=1).astype(BF16)
        gu = _dot(xb, wgu_bf_ref[...]) + bgu_ref[0]
        gate = jnp.minimum(gu[:, :d_ff], SWIGLU_LIMIT)
        up = jnp.clip(gu[:, d_ff:], -SWIGLU_LIMIT, SWIGLU_LIMIT)
        act = (up + 1.0) * gate * jax.nn.sigmoid(SWIGLU_ALPHA * gate)
        y_ref[...] = _pack_bf16_pairs(_dot(act.astype(BF16), wd_bf_ref[...]) + bd_ref[0])


def _expert_call(block_e, n_used, n_valid, xs, w_gu, b_gu, w_down, b_down):
    n_pad = xs.shape[0]
    n_exp, d, two_ff = w_gu.shape
    d_ff = two_ff // 2
    nblk = n_pad // EXPERT_ROWS
    rows = lambda i, be, nu, nv: (jnp.minimum(i, nu[0] - 1), 0)
    per_e = lambda i, be, nu, nv: (be[i], 0, 0)
    grid_spec = pltpu.PrefetchScalarGridSpec(
        num_scalar_prefetch=3,
        grid=(nblk,),
        in_specs=[pl.BlockSpec((EXPERT_ROWS, d // 2), rows),
                  pl.BlockSpec((1, d, two_ff), per_e), pl.BlockSpec((1, 1, two_ff), per_e),
                  pl.BlockSpec((1, d_ff, d), per_e), pl.BlockSpec((1, 1, d), per_e)],
        out_specs=pl.BlockSpec((EXPERT_ROWS, d // 2), rows),
        scratch_shapes=[pltpu.VMEM((d, two_ff), BF16), pltpu.VMEM((d_ff, d), BF16)],
    )
    return pl.pallas_call(
        functools.partial(_expert_kernel, d_ff=d_ff),
        grid_spec=grid_spec,
        out_shape=jax.ShapeDtypeStruct((n_pad, d // 2), U32),
        compiler_params=pltpu.CompilerParams(dimension_semantics=("arbitrary",),
                                             vmem_limit_bytes=EXPERT_VMEM_BYTES),
        name="experts",
    )(block_e, n_used, n_valid, xs, w_gu, b_gu.reshape(n_exp, 1, two_ff), w_down, b_down.reshape(n_exp, 1, d))


def _combine_kernel(y4_ref, wt_ref, h_ref, mod_ref, fw_ref, o_ref):
    wt = wt_ref[...]
    acc_lo, acc_hi = None, None
    for j in range(TOP_K):
        lo, hi = _unpack_bf16_pairs(y4_ref[j])
        w = wt[:, j:j + 1]
        acc_lo = w * lo if j == 0 else acc_lo + w * lo
        acc_hi = w * hi if j == 0 else acc_hi + w * hi
    acc = jnp.concatenate([acc_lo, acc_hi], axis=1)
    m = mod_ref[0]
    o_ref[...] = _rmsnorm(h_ref[...] + m[5:6] * acc, fw_ref[...])


def _combine_call(y4, wts, h, mod, fw, *, tg):
    n, d = h.shape
    t_per_b = n // mod.shape[0]
    row = lambda width: pl.BlockSpec((tg, width), lambda i: (i, 0))
    return pl.pallas_call(
        _combine_kernel,
        grid=(n // tg,),
        in_specs=[pl.BlockSpec((TOP_K, tg, d // 2), lambda i: (0, i, 0)),
                  row(LANES), row(d),
                  pl.BlockSpec((1, N_MOD, d), lambda i: (i * tg // t_per_b, 0, 0)),
                  pl.BlockSpec((1, d), lambda i: (0, 0))],
        out_specs=row(d),
        out_shape=jax.ShapeDtypeStruct((n, d), F32),
        compiler_params=pltpu.CompilerParams(dimension_semantics=("arbitrary",)),
        name="combine",
    )(y4, wts, h, mod, fw)


def kernel(x, c, ctx, c_ctx, w_ada, b_ada, norm_mix_w, norm_mlp_w, w_in, w_gk_f, b_gk_f, w_gk_b, b_gk_b,
           gla_norm_w, w_pool, pool_scale, w_out, w_router, b_router, w_gu, b_gu, w_down, b_down,
           final_norm_w):
    b, t, d = x.shape
    assert w_ada.shape[0] == 1, "single-layer trunk"
    n_exp = w_router.shape[2]
    rank = w_gk_f.shape[1]
    qk = w_gk_f.shape[2]
    dk = qk // GLA_HEADS
    gw = GLA_HEADS * gla_norm_w.shape[1]
    pw = w_pool.shape[1] * w_pool.shape[2]
    assert w_in.shape[2] == 2 * qk + 2 * gw + 2 * rank + pw and 2 * rank <= LANES
    assert t % SUPER == 0 and ctx.shape[1] % SUPER == 0 and n_exp <= LANES

    rows = -(-(b + 1) // 8) * 8
    cc = jnp.zeros((rows, d), F32).at[:b].set(c).at[b].set(c_ctx)
    mod = _mod_call(cc, w_ada[0], b_ada)
    mod_x = mod[:b].reshape(b, N_MOD, d)
    mod_c = mod[b:b + 1].reshape(1, N_MOD, d)

    wi = w_in[0]
    o_r = 2 * qk + 2 * gw
    w_cat = jnp.concatenate([wi[:, :o_r], wi[:, o_r + 2 * rank:], wi[:, o_r:o_r + 2 * rank],
                             jnp.zeros((d, LANES - 2 * rank), F32)], axis=1).astype(BF16)
    wgk = jnp.zeros((LANES, 2 * qk), F32).at[:rank, :qk].set(w_gk_f[0]).at[rank:2 * rank, qk:].set(w_gk_b[0])
    bgk = jnp.concatenate([b_gk_f[0], b_gk_b[0]])[None, :]
    w_vt = wi[:, 2 * qk:2 * qk + gw].T.astype(BF16)
    proj = functools.partial(_inproj_call, nw=norm_mix_w, w=w_cat, wvt=w_vt, wgk=wgk.astype(BF16), bgk=bgk,
                             qk=qk, gw=gw, pw=pw, dk=dk)
    q, k, v, vt, g, xp, gk = proj(x, mod_x, tm=512)
    _, kc, _, vtc, _, _, gkc = proj(ctx, mod_c, tm=SUPER)

    gla = _gla_call(q, k, v, vt, gk, g, kc, vtc, gkc, gla_norm_w)
    pool = _pool_call(xp, w_pool[0].astype(BF16), pool_scale)

    n = b * t
    wr = jnp.zeros((d, LANES), F32).at[:, :n_exp].set(w_router[0])
    br = jnp.zeros((1, LANES), F32).at[0, :n_exp].set(b_router[0])
    h, xt, e_idx, rk, wts, cnt = _route_call(
        gla.reshape(n, gw), pool.reshape(n, pw), x.reshape(n, d), mod_x, w_out[0].astype(BF16),
        norm_mlp_w, wr, br, tm=512, n_exp=n_exp)

    counts = cnt[0, :n_exp]
    padded = (counts + EXPERT_ROWS - 1) // EXPERT_ROWS * EXPERT_ROWS
    pad_end = jnp.cumsum(padded)
    start_pad = pad_end - padded
    dest = (jnp.take(start_pad, e_idx[:, :TOP_K]) + rk[:, :TOP_K]).astype(I32)
    n_pad = n * TOP_K + n_exp * EXPERT_ROWS
    nblk = n_pad // EXPERT_ROWS
    n_used = (pad_end[-1] // EXPERT_ROWS).astype(I32)
    blk = jnp.arange(nblk, dtype=I32)
    block_e = jnp.sum((pad_end[None, :] <= (blk * EXPERT_ROWS)[:, None]).astype(I32), axis=1)
    block_e = jnp.minimum(block_e, n_exp - 1)
    block_e = jnp.where(blk < n_used, block_e, block_e[jnp.maximum(n_used - 1, 0)])

    n_valid = jnp.clip(jnp.take(start_pad + counts, block_e) - blk * EXPERT_ROWS, 0, EXPERT_ROWS).astype(I32)

    assert n % (SC_WORKERS * SC_WINDOW) == 0
    idx_scatter = dest.reshape(n // SC_WINDOW, SC_WINDOW, TOP_K).transpose(0, 2, 1)
    xs = _sc_scatter_call(xt, idx_scatter, n_out=n_pad)
    ys = _expert_call(block_e, n_used.reshape(1), n_valid, xs, w_gu[0], b_gu[0], w_down[0], b_down[0])
    assert (n * TOP_K) % (SC_WORKERS * 2 * SC_GATHER_WINDOW) == 0
    idx_gather = dest.T.reshape(SC_WORKERS, n * TOP_K // (SC_WORKERS * SC_GATHER_WINDOW), SC_GATHER_WINDOW)
    y4 = _sc_gather_call(ys, idx_gather).reshape(TOP_K, n, d // 2)
    out = _combine_call(y4, wts, h, mod_x, final_norm_w[None, :], tg=256)
    return out.reshape(b, t, d)
```

```python
import functools

import numpy as np
import jax
import jax.numpy as jnp
from jax import lax
from jax.experimental import pallas as pl
from jax.experimental.pallas import tpu as pltpu
from jax.experimental.pallas import tpu_sc as plsc

F32 = jnp.float32
BF16 = jnp.bfloat16
I32 = jnp.int32
U32 = jnp.uint32

GRID_W = 64
GLA_HEADS = 4
GLA_CHUNK = 64
GATE_NORMALIZER = 16.0
POOL_WINDOWS = (2, 4, 8, 16)
TOP_K = 4
RANK_LIMIT = 1 << 20
SWIGLU_LIMIT = 7.0
SWIGLU_ALPHA = 1.702
N_MOD = 6
EPS = 1e-6

LANES = 128
SUPER = 256
HEAD_PAIR_DK = 128
EXPERT_ROWS = 512
EXPERT_VMEM_BYTES = 56 * 1024 * 1024
SC_CORES = 2
SC_SUBCORES = 16
SC_WORKERS = SC_CORES * SC_SUBCORES
SC_WINDOW = 32
SC_GATHER_WINDOW = 64


def _dot(a, b):
    return jnp.dot(a, b, preferred_element_type=F32)


def _dot_nt(a, b):
    return lax.dot_general(a, b, (((1,), (1,)), ((), ())), preferred_element_type=F32)


def _split_bf16(x):
    hi = x.astype(BF16)
    lo = (x - hi.astype(F32)).astype(BF16)
    return hi, lo


def _pack_bf16_pairs(x):
    c = x.shape[1] // 2
    lo = lax.bitcast_convert_type(x[:, :c].astype(BF16).astype(F32), U32)
    hi = lax.bitcast_convert_type(x[:, c:].astype(BF16).astype(F32), U32)
    return (lo >> 16) | hi


def _unpack_bf16_pairs(p):
    lo = lax.bitcast_convert_type(p << 16, F32)
    hi = lax.bitcast_convert_type(p & jnp.uint32(0xFFFF0000), F32)
    return lo, hi


def _rmsnorm(x, w):
    var = jnp.mean(x * x, axis=-1, keepdims=True)
    return x * lax.rsqrt(var + EPS) * w


def _mod_kernel(c_ref, w_ref, b_ref, o_ref):
    c = c_ref[...]
    s = c * jax.nn.sigmoid(c)
    o_ref[...] = jnp.dot(s, w_ref[...], precision=lax.Precision.HIGHEST,
                         preferred_element_type=F32) + b_ref[...]


def _mod_call(cc, w_ada, b_ada):
    rows, d = cc.shape
    n = w_ada.shape[1]
    tn = 1024
    return pl.pallas_call(
        _mod_kernel,
        grid=(n // tn,),
        in_specs=[pl.BlockSpec((rows, d), lambda j: (0, 0)),
                  pl.BlockSpec((d, tn), lambda j: (0, j)),
                  pl.BlockSpec((1, tn), lambda j: (0, j))],
        out_specs=pl.BlockSpec((rows, tn), lambda j: (0, j)),
        out_shape=jax.ShapeDtypeStruct((rows, n), F32),
        name="mod",
    )(cc, w_ada, b_ada)


def _inproj_kernel(x_ref, mod_ref, nw_ref, w_ref, wvt_ref, wgk_ref, bgk_ref,
                   q_ref, k_ref, v_ref, vt_ref, g_ref, p_ref, gk_ref, *, qk, gw, pw, dk):
    x = x_ref[0]
    m = mod_ref[0]
    hm = (_rmsnorm(x, nw_ref[...]) * (1.0 + m[1:2]) + m[0:1]).astype(BF16)
    p = _dot(hm, w_ref[...])
    vt = _dot_nt(wvt_ref[...], hm)
    for s in range(vt_ref.shape[1]):
        vt_ref[0, s] = vt[:, s * SUPER:(s + 1) * SUPER].astype(BF16)
    o = 0
    q_ref[0] = (p[:, o:o + qk] * (dk ** -0.5)).astype(BF16); o += qk
    k_ref[0] = p[:, o:o + qk].astype(BF16); o += qk
    v_ref[0] = p[:, o:o + gw].astype(BF16); o += gw
    g_ref[0] = p[:, o:o + gw].astype(BF16); o += gw
    p_ref[0] = p[:, o:o + pw].astype(BF16); o += pw
    r = p[:, o:o + LANES]
    z = _dot(r.astype(BF16), wgk_ref[...]) + bgk_ref[...]
    gk_ref[0] = (jnp.minimum(z, 0.0) - jnp.log1p(jnp.exp(-jnp.abs(z)))) * (1.0 / GATE_NORMALIZER)


def _inproj_call(x, mod, nw, w, wvt, wgk, bgk, *, qk, gw, pw, dk, tm):
    b, t, d = x.shape
    n_in = w.shape[1]
    bs = lambda width: pl.BlockSpec((1, tm, width), lambda i, j: (i, j, 0))
    const = lambda shape: pl.BlockSpec(shape, lambda i, j: (0,) * len(shape))
    per_batch = mod.shape[0] > 1
    sds = jax.ShapeDtypeStruct
    return pl.pallas_call(
        functools.partial(_inproj_kernel, qk=qk, gw=gw, pw=pw, dk=dk),
        grid=(b, t // tm),
        in_specs=[bs(d),
                  pl.BlockSpec((1, N_MOD, d), (lambda i, j: (i, 0, 0)) if per_batch else (lambda i, j: (0, 0, 0))),
                  const((1, d)), const((d, n_in)), const((gw, d)), const((LANES, 2 * qk)), const((1, 2 * qk))],
        out_specs=[bs(qk), bs(qk), bs(gw),
                   pl.BlockSpec((1, tm // SUPER, gw, SUPER), lambda i, j: (i, j, 0, 0)),
                   bs(gw), bs(pw), bs(2 * qk)],
        out_shape=[sds((b, t, qk), BF16), sds((b, t, qk), BF16), sds((b, t, gw), BF16),
                   sds((b, t // SUPER, gw, SUPER), BF16),
                   sds((b, t, gw), BF16), sds((b, t, pw), BF16), sds((b, t, 2 * qk), F32)],
        compiler_params=pltpu.CompilerParams(dimension_semantics=("arbitrary", "arbitrary")),
        name="inproj",
    )(x, mod, nw, w, wvt, wgk, bgk)


def _gla_super(q, k, v, vt, gk, cm, amask, bd_mask, st, fwd, want_out):
    nch = SUPER // GLA_CHUNK
    order = tuple(range(nch)) if fwd else tuple(reversed(range(nch)))
    last_row = GLA_CHUNK - 1 if fwd else 0
    mid_row = GLA_CHUNK // 2 - 1 if fwd else GLA_CHUNK // 2
    hi, lo = _split_bf16(gk)
    bcum = _dot(cm, hi) + _dot(cm, lo)

    def chunk_row(r):
        return jnp.concatenate(
            [jnp.broadcast_to(bcum[c * GLA_CHUNK + r:c * GLA_CHUNK + r + 1, :], (GLA_CHUNK, bcum.shape[1]))
             for c in range(nch)], axis=0)

    chunk_of_row = lax.broadcasted_iota(I32, bcum.shape, 0) // GLA_CHUNK

    def by_chunk(x):
        return jnp.concatenate([jnp.where(chunk_of_row == c, x, 0.0).astype(BF16) for c in range(nch)], axis=1)

    blast = chunk_row(last_row)
    u_all = _dot(vt, by_chunk(k * jnp.exp(blast - bcum)))
    before = [None] * nch
    for c in order:
        before[c] = st
        decay = jnp.exp(bcum[c * GLA_CHUNK + last_row:c * GLA_CHUNK + last_row + 1, :])
        st = st * decay + jnp.where(bd_mask, u_all[:, c * HEAD_PAIR_DK:(c + 1) * HEAD_PAIR_DK], 0.0)
    if not want_out:
        return None, st
    bmid = chunk_row(mid_row)
    qt = q * jnp.exp(bcum - bmid)
    kt = (k * jnp.exp(bmid - bcum)).astype(BF16)
    lane = lax.broadcasted_iota(I32, qt.shape, 1)
    half = HEAD_PAIR_DK // 2
    o_heads = []
    for hh in range(2):
        sel = (lane < half) if hh == 0 else (lane >= half)
        a = _dot_nt(jnp.where(sel, qt, 0.0).astype(BF16), kt)
        a = jnp.where(amask, a, 0.0).astype(BF16)
        o_heads.append(_dot(a, v[:, hh * LANES:(hh + 1) * LANES]))
    s_all = jnp.concatenate([s.astype(BF16) for s in before], axis=1)
    o_inter = _dot_nt(by_chunk(q * jnp.exp(bcum)), s_all)
    return jnp.concatenate(o_heads, axis=1) + o_inter, st


def _gla_kernel(q_ref, k_ref, v_ref, vt_ref, gkf_ref, gkb_ref, g_ref, kc_ref, vtc_ref, gkfc_ref, gkbc_ref,
                nw_ref, cmf_ref, cmb_ref, o_ref, stf_ref, stb_ref, of_ref, ob_ref):
    t = q_ref.shape[1]
    tc = kc_ref.shape[1]
    nsc, nscc = t // SUPER, tc // SUPER
    cmf = cmf_ref[...]
    cmb = cmb_ref[...]
    amask_f = cmf > 0
    amask_b = cmb > 0
    row = lax.broadcasted_iota(I32, (2 * LANES, HEAD_PAIR_DK), 0)
    lane = lax.broadcasted_iota(I32, (2 * LANES, HEAD_PAIR_DK), 1)
    bd_mask = (row < LANES) == (lane < HEAD_PAIR_DK // 2)

    def ctx_state(gk_ref, cm, fwd, j, st):
        rows = pl.ds(j * SUPER, SUPER)
        return _gla_super(None, kc_ref[0, rows, :].astype(F32), None, vtc_ref[0, j], gk_ref[0, rows, :],
                          cm, None, bd_mask, st, fwd, False)[1]

    def latent(gk_ref, cm, amask, fwd, j, st):
        rows = pl.ds(pl.multiple_of(j * SUPER, SUPER), SUPER)
        return _gla_super(q_ref[0, rows, :].astype(F32), k_ref[0, rows, :].astype(F32), v_ref[0, rows, :],
                          vt_ref[0, j], gk_ref[0, rows, :], cm, amask, bd_mask, st, fwd, True)

    st = jnp.zeros(stf_ref.shape, F32)
    for j in range(nscc):
        st = ctx_state(gkfc_ref, cmf, True, j, st)
    stf_ref[...] = st
    st = jnp.zeros(stb_ref.shape, F32)
    for j in reversed(range(nscc)):
        st = ctx_state(gkbc_ref, cmb, False, j, st)
    stb_ref[...] = st

    def scan_body(jj, carry):
        jb = nsc - 1 - jj
        of, stf = latent(gkf_ref, cmf, amask_f, True, jj, stf_ref[...])
        of_ref[pl.ds(pl.multiple_of(jj * SUPER, SUPER), SUPER), :] = of
        stf_ref[...] = stf
        ob, stb = latent(gkb_ref, cmb, amask_b, False, jb, stb_ref[...])
        ob_ref[pl.ds(pl.multiple_of(jb * SUPER, SUPER), SUPER), :] = ob
        stb_ref[...] = stb
        return carry

    lax.fori_loop(0, nsc, scan_body, 0)

    nw = nw_ref[...]

    def out_body(j, carry):
        rows = pl.ds(pl.multiple_of(j * SUPER, SUPER), SUPER)
        o = of_ref[rows, :] + ob_ref[rows, :]
        g = g_ref[0, rows, :].astype(F32)
        gate = g * jax.nn.sigmoid(g)
        for hh in range(2):
            oh = o[:, hh * LANES:(hh + 1) * LANES]
            on = oh * lax.rsqrt(jnp.mean(oh * oh, axis=-1, keepdims=True) + EPS) * nw
            o_ref[0, rows, hh * LANES:(hh + 1) * LANES] = (on * gate[:, hh * LANES:(hh + 1) * LANES]).astype(BF16)
        return carry

    lax.fori_loop(0, nsc, out_body, 0)


def _gla_masks():
    i = np.arange(SUPER)
    same = (i[:, None] // GLA_CHUNK) == (i[None, :] // GLA_CHUNK)
    fwd = same & (i[None, :] <= i[:, None])
    bwd = same & (i[None, :] >= i[:, None])
    return jnp.asarray(fwd, BF16), jnp.asarray(bwd, BF16)


def _gla_call(q, k, v, vt, gk, g, kc, vtc, gkc, nw):
    b, t, qk = q.shape
    tc = kc.shape[1]
    npair = qk // HEAD_PAIR_DK
    cmf, cmb = _gla_masks()
    lat = lambda width, off: pl.BlockSpec((1, t, width), lambda i, j: (i, 0, j + off))
    ctx = lambda width, off: pl.BlockSpec((1, tc, width), lambda i, j: (i, 0, j + off))
    tr = lambda n_groups: pl.BlockSpec((1, n_groups, 2 * LANES, SUPER), lambda i, j: (i, 0, j, 0))
    const = lambda shape: pl.BlockSpec(shape, lambda i, j: (0,) * len(shape))
    return pl.pallas_call(
        _gla_kernel,
        grid=(b, npair),
        in_specs=[lat(HEAD_PAIR_DK, 0), lat(HEAD_PAIR_DK, 0), lat(2 * LANES, 0), tr(t // SUPER),
                  lat(HEAD_PAIR_DK, 0), lat(HEAD_PAIR_DK, npair), lat(2 * LANES, 0),
                  ctx(HEAD_PAIR_DK, 0), tr(tc // SUPER), ctx(HEAD_PAIR_DK, 0), ctx(HEAD_PAIR_DK, npair),
                  const((1, LANES)), const(cmf.shape), const(cmb.shape)],
        out_specs=lat(2 * LANES, 0),
        out_shape=jax.ShapeDtypeStruct((b, t, v.shape[2]), BF16),
        scratch_shapes=[pltpu.VMEM((2 * LANES, HEAD_PAIR_DK), F32), pltpu.VMEM((2 * LANES, HEAD_PAIR_DK), F32),
                        pltpu.VMEM((t, 2 * LANES), F32), pltpu.VMEM((t, 2 * LANES), F32)],
        compiler_params=pltpu.CompilerParams(dimension_semantics=("arbitrary", "arbitrary")),
        name="gla",
    )(q, k, v, vt, gk, gk, g, kc, vtc, gkc, gkc, nw, cmf, cmb)


def _pool_kernel(x_ref, cm_ref, wp_ref, ps_ref, o_ref, y_ref, z_ref, *, rows):
    t = x_ref.shape[1]
    tok = lax.broadcasted_iota(I32, (t, LANES), 0)
    r = tok // GRID_W
    c = tok % GRID_W
    for gi, w in enumerate(POOL_WINDOWS):
        lo = w // 2
        hi = w - 1 - lo
        cols = slice(gi * LANES, (gi + 1) * LANES)
        cmat = cm_ref[gi]
        for blk in range(t // SUPER):
            rs = slice(blk * SUPER, (blk + 1) * SUPER)
            y_ref[rs, :] = _dot(cmat, x_ref[0, rs, cols])
        z_ref[...] = y_ref[...]
        for dr in range(-lo, hi + 1):
            sh = abs(dr) * GRID_W
            if dr == 0 or sh >= t:
                continue
            if dr > 0:
                z_ref[0:t - sh, :] += y_ref[sh:t, :]
            else:
                z_ref[sh:t, :] += y_ref[0:t - sh, :]
        cnt_r = jnp.minimum(r + hi + 1, rows) - jnp.maximum(r - lo, 0)
        cnt_c = jnp.minimum(c + hi + 1, GRID_W) - jnp.maximum(c - lo, 0)
        cnt = (cnt_r * cnt_c).astype(F32)
        pooled = z_ref[...] / cnt - x_ref[0, :, cols].astype(F32)
        yp = _dot(pooled.astype(BF16), wp_ref[gi]) * ps_ref[:, cols]
        o_ref[0, :, cols] = yp.astype(BF16)


def _pool_col_mats():
    i = np.arange(SUPER)
    same_row = (i[:, None] // GRID_W) == (i[None, :] // GRID_W)
    d = i[None, :] - i[:, None]
    mats = []
    for w in POOL_WINDOWS:
        lo = w // 2
        hi = w - 1 - lo
        mats.append(same_row & (d >= -lo) & (d <= hi))
    return jnp.asarray(np.stack(mats), BF16)


def _pool_call(xp, w_pool, pool_scale):
    b, t, pw = xp.shape
    ng = len(POOL_WINDOWS)
    cm = _pool_col_mats()
    const = lambda shape: pl.BlockSpec(shape, lambda i: (0,) * len(shape))
    return pl.pallas_call(
        functools.partial(_pool_kernel, rows=t // GRID_W),
        grid=(b,),
        in_specs=[pl.BlockSpec((1, t, pw), lambda i: (i, 0, 0)),
                  const(cm.shape), const((ng, LANES, LANES)), const((1, pw))],
        out_specs=pl.BlockSpec((1, t, pw), lambda i: (i, 0, 0)),
        out_shape=jax.ShapeDtypeStruct((b, t, pw), BF16),
        scratch_shapes=[pltpu.VMEM((t, LANES), F32), pltpu.VMEM((t, LANES), F32)],
        compiler_params=pltpu.CompilerParams(dimension_semantics=("arbitrary",)),
        name="pool",
    )(xp, cm, w_pool, pool_scale)


def _route_kernel(gla_ref, pool_ref, x_ref, mod_ref, wo_ref, nw_ref, wr_ref, br_ref, lt_ref,
                  h_ref, xt_ref, code_ref, wt_ref, cnt_ref, run_ref, *, gw, n_exp):
    i = pl.program_id(0)

    @pl.when(i == 0)
    def _():
        run_ref[...] = jnp.zeros_like(run_ref)

    m = mod_ref[0]
    acc = _dot(gla_ref[...], wo_ref[0:gw, :]) + _dot(pool_ref[...], wo_ref[gw:, :])
    h = x_ref[...] + m[2:3] * acc
    h_ref[...] = h
    xt = _rmsnorm(h, nw_ref[...]) * (1.0 + m[4:5]) + m[3:4]
    xt_ref[...] = _pack_bf16_pairs(xt)
    xh, xl = _split_bf16(xt)
    wr = wr_ref[...]
    wh, wl = _split_bf16(wr)
    logits = _dot(xh, wh) + _dot(xl, wh) + _dot(xh, wl) + br_ref[...]
    lane = lax.broadcasted_iota(I32, logits.shape, 1)
    neg = jnp.float32(-jnp.inf)
    logits = jnp.where(lane < n_exp, logits, neg)
    vals, hots = [], []
    e_out = jnp.zeros(logits.shape, I32)
    for j in range(TOP_K):
        mx = jnp.max(logits, axis=-1, keepdims=True)
        idx = jnp.min(jnp.where(logits == mx, lane, LANES), axis=-1, keepdims=True)
        hot = lane == idx
        vals.append(mx)
        hots.append(hot)
        e_out = jnp.where(lane == j, idx, e_out)
        logits = jnp.where(hot, neg, logits)
    ex = [jnp.exp(v - vals[0]) for v in vals]
    den = ex[0] + ex[1] + ex[2] + ex[3]
    w_out = jnp.zeros(logits.shape, F32)
    for j in range(TOP_K):
        w_out = jnp.where(lane == j, ex[j] / den, w_out)
    osum = jnp.where(hots[0] | hots[1] | hots[2] | hots[3], 1.0, 0.0)
    before = _dot(lt_ref[...], osum.astype(BF16)) + run_ref[0:1, :]
    rk_out = jnp.zeros(logits.shape, I32)
    for j in range(TOP_K):
        rj = jnp.sum(jnp.where(hots[j], before, 0.0), axis=-1, keepdims=True)
        rk_out = jnp.where(lane == j, rj.astype(I32), rk_out)
    run = run_ref[0:1, :] + jnp.sum(osum, axis=0, keepdims=True)
    run_ref[...] = jnp.broadcast_to(run, run_ref.shape)
    code = e_out * RANK_LIMIT + rk_out
    code_ref[...] = code.T[:code_ref.shape[0], :]
    wt_ref[...] = w_out
    cnt_ref[...] = jnp.broadcast_to(run, cnt_ref.shape).astype(I32)


def _route_call(gla, pool, x, mod, w_out, nw, wr, br, *, tm, n_exp):
    n, d = x.shape
    gw = gla.shape[1]
    t_per_b = n // mod.shape[0]
    lt = jnp.asarray(np.tril(np.ones((tm, tm), np.float32), -1), BF16)
    row = lambda width: pl.BlockSpec((tm, width), lambda i: (i, 0))
    const = lambda shape: pl.BlockSpec(shape, lambda i: (0,) * len(shape))
    sds = jax.ShapeDtypeStruct
    return pl.pallas_call(
        functools.partial(_route_kernel, gw=gw, n_exp=n_exp),
        grid=(n // tm,),
        in_specs=[row(gw), row(pool.shape[1]), row(d),
                  pl.BlockSpec((1, N_MOD, d), lambda i: (i * tm // t_per_b, 0, 0)),
                  const(w_out.shape), const((1, d)), const(wr.shape), const((1, LANES)), const((tm, tm))],
        out_specs=[row(d), row(d // 2), pl.BlockSpec((8, tm), lambda i: (0, i)), row(LANES), const((8, LANES))],
        out_shape=[sds((n, d), F32), sds((n, d // 2), U32), sds((8, n), I32),
                   sds((n, LANES), F32), sds((8, LANES), I32)],
        scratch_shapes=[pltpu.VMEM((8, LANES), F32)],
        compiler_params=pltpu.CompilerParams(dimension_semantics=("arbitrary",)),
        name="route",
    )(gla, pool, x, mod, w_out, nw, wr, br, lt)


def _plan_kernel(cnt_ref, code_ref, dest_ref, be_ref, nv_ref, nu_ref, start_ref, *, n_exp, rows):
    @pl.when(pl.program_id(0) == 0)
    def _():
        blk = (lax.broadcasted_iota(I32, be_ref.shape, 0) * LANES + lax.broadcasted_iota(I32, be_ref.shape, 1))
        blk_row0 = blk * rows
        be = jnp.zeros(be_ref.shape, I32)
        end_valid = jnp.zeros(be_ref.shape, I32)
        acc = jnp.int32(0)
        last_e = jnp.int32(0)
        for e in range(n_exp):
            c = cnt_ref[e]
            start_ref[e] = acc
            in_e = blk_row0 >= acc
            end_valid = jnp.where(in_e, acc + c, end_valid)
            be = jnp.where(in_e, e, be)
            acc = acc + (c + rows - 1) // rows * rows
            last_e = jnp.where(c > 0, e, last_e)
        n_used = acc // rows
        nu_ref[0] = n_used
        be_ref[...] = jnp.where(blk < n_used, be, last_e)
        nv_ref[...] = jnp.clip(end_valid - blk_row0, 0, rows)

    code = code_ref[...]
    e_vec = code // RANK_LIMIT
    dest = code % RANK_LIMIT
    for e in range(n_exp):
        dest = dest + jnp.where(e_vec == e, start_ref[e], 0)
    dest_ref[...] = dest


def _plan_call(counts, code_t, *, n_exp, rows, chunk):
    n = code_t.shape[1]
    sds = jax.ShapeDtypeStruct
    smem = pltpu.SMEM
    return pl.pallas_call(
        functools.partial(_plan_kernel, n_exp=n_exp, rows=rows),
        grid=(n // chunk,),
        in_specs=[pl.BlockSpec(memory_space=smem), pl.BlockSpec((8, chunk), lambda i: (0, i))],
        out_specs=[pl.BlockSpec((8, chunk), lambda i: (0, i)),
                   pl.BlockSpec((8, LANES), lambda i: (0, 0)), pl.BlockSpec((8, LANES), lambda i: (0, 0)),
                   pl.BlockSpec(memory_space=smem)],
        out_shape=[sds((8, n), I32), sds((8, LANES), I32), sds((8, LANES), I32), sds((1,), I32)],
        scratch_shapes=[pltpu.SMEM((n_exp,), I32)],
        compiler_params=pltpu.CompilerParams(dimension_semantics=("arbitrary",)),
        name="plan",
    )(counts, code_t)


def _sc_worker_id():
    return lax.axis_index("s") * SC_CORES + lax.axis_index("c")


def _sc_scatter_call(x, idx3, *, n_out):
    n, d = x.shape
    n_win_total, k, w = idx3.shape
    n_win = n_win_total // SC_WORKERS
    mesh = plsc.VectorSubcoreMesh(core_axis_name="c", subcore_axis_name="s")

    @functools.partial(
        pl.kernel, mesh=mesh,
        out_type=jax.ShapeDtypeStruct((n_out, d), x.dtype),
        scratch_types=[pltpu.VMEM((k, w), I32), pltpu.VMEM((w, d), x.dtype), pltpu.SemaphoreType.DMA],
        name="sc_dispatch",
    )
    def kern(x_hbm, idx_hbm, out_hbm, idx_v, rows_v, sem):
        wid = _sc_worker_id()

        @pl.loop(0, n_win)
        def _(i):
            win = wid * n_win + i
            pltpu.sync_copy(idx_hbm.at[win], idx_v)
            pltpu.sync_copy(x_hbm.at[pl.ds(win * w, w)], rows_v)
            for j in range(k):
                pltpu.async_copy(rows_v, out_hbm.at[idx_v.at[j]], sem).wait()

    return kern(x, idx3)


def _sc_gather_call(table, idx3):
    n_workers, n_win, w = idx3.shape
    d = table.shape[1]
    assert n_workers == SC_WORKERS and n_win % 2 == 0
    mesh = plsc.VectorSubcoreMesh(core_axis_name="c", subcore_axis_name="s")

    @functools.partial(
        pl.kernel, mesh=mesh,
        out_type=jax.ShapeDtypeStruct((n_workers * n_win * w, d), table.dtype),
        scratch_types=[pltpu.VMEM((n_win, w), I32), pltpu.VMEM((2, w, d), table.dtype),
                       pltpu.SemaphoreType.DMA((2,)), pltpu.SemaphoreType.DMA((2,))],
        name="sc_gather",
    )
    def kern(table_hbm, idx_hbm, out_hbm, idx_v, rows_v, gsem, osem):
        wid = _sc_worker_id()
        base = wid * n_win
        pltpu.sync_copy(idx_hbm.at[wid], idx_v)

        def gather(wi, b):
            return pltpu.make_async_copy(table_hbm.at[idx_v.at[wi]], rows_v.at[b], gsem.at[b])

        def put(wi, b):
            return pltpu.make_async_copy(rows_v.at[b], out_hbm.at[pl.ds((base + wi) * w, w)], osem.at[b])

        gather(0, 0).start()

        @pl.loop(0, n_win, step=2)
        def _(i):
            for b in range(2):
                wi = i + b

                @pl.when(wi + 1 < n_win)
                def _():
                    @pl.when(wi >= 1)
                    def _():
                        put(wi - 1, 1 - b).wait()
                    gather(wi + 1, 1 - b).start()

                gather(wi, b).wait()
                put(wi, b).start()

        put(n_win - 2, 0).wait()
        put(n_win - 1, 1).wait()

    return kern(table, idx3)


def _expert_kernel(be_ref, nu_ref, nv_ref, x_ref, wgu_ref, bgu_ref, wd_ref, bd_ref, y_ref, wgu_bf_ref, wd_bf_ref,
                   *, d_ff):
    i = pl.program_id(0)

    @pl.when((i == 0) | (be_ref[i] != be_ref[jnp.maximum(i - 1, 0)]))
    def _():
        wgu_bf_ref[...] = wgu_ref[0].astype(BF16)
        wd_bf_ref[...] = wd_ref[0].astype(BF16)

    @pl.when(i < nu_ref[0])
    def _():
        row = lax.broadcasted_iota(I32, x_ref.shape, 0)
        lo, hi = _unpack_bf16_pairs(jnp.where(row < nv_ref[i], x_ref[...], jnp.uint32(0)))
        xb = jnp.concatenate([lo, hi], axis=1).astype(BF16)
        gu = _dot(xb, wgu_bf_ref[...]) + bgu_ref[0]
        gate = jnp.minimum(gu[:, :d_ff], SWIGLU_LIMIT)
        up = jnp.clip(gu[:, d_ff:], -SWIGLU_LIMIT, SWIGLU_LIMIT)
        act = (up + 1.0) * gate * jax.nn.sigmoid(SWIGLU_ALPHA * gate)
        y_ref[...] = _pack_bf16_pairs(_dot(act.astype(BF16), wd_bf_ref[...]) + bd_ref[0])


def _expert_call(block_e, n_used, n_valid, xs, w_gu, b_gu, w_down, b_down):
    n_pad = xs.shape[0]
    n_exp, d, two_ff = w_gu.shape
    d_ff = two_ff // 2
    nblk = n_pad // EXPERT_ROWS
    rows = lambda i, be, nu, nv: (jnp.minimum(i, nu[0] - 1), 0)
    per_e = lambda i, be, nu, nv: (be[i], 0, 0)
    grid_spec = pltpu.PrefetchScalarGridSpec(
        num_scalar_prefetch=3,
        grid=(nblk,),
        in_specs=[pl.BlockSpec((EXPERT_ROWS, d // 2), rows),
                  pl.BlockSpec((1, d, two_ff), per_e), pl.BlockSpec((1, 1, two_ff), per_e),
                  pl.BlockSpec((1, d_ff, d), per_e), pl.BlockSpec((1, 1, d), per_e)],
        out_specs=pl.BlockSpec((EXPERT_ROWS, d // 2), rows),
        scratch_shapes=[pltpu.VMEM((d, two_ff), BF16), pltpu.VMEM((d_ff, d), BF16)],
    )
    return pl.pallas_call(
        functools.partial(_expert_kernel, d_ff=d_ff),
        grid_spec=grid_spec,
        out_shape=jax.ShapeDtypeStruct((n_pad, d // 2), U32),
        compiler_params=pltpu.CompilerParams(dimension_semantics=("arbitrary",),
                                             vmem_limit_bytes=EXPERT_VMEM_BYTES),
        name="experts",
    )(block_e, n_used, n_valid, xs, w_gu, b_gu.reshape(n_exp, 1, two_ff), w_down, b_down.reshape(n_exp, 1, d))


def _combine_kernel(y4_ref, wt_ref, h_ref, mod_ref, fw_ref, o_ref):
    wt = wt_ref[...]
    acc_lo, acc_hi = None, None
    for j in range(TOP_K):
        lo, hi = _unpack_bf16_pairs(y4_ref[j])
        w = wt[:, j:j + 1]
        acc_lo = w * lo if j == 0 else acc_lo + w * lo
        acc_hi = w * hi if j == 0 else acc_hi + w * hi
    acc = jnp.concatenate([acc_lo, acc_hi], axis=1)
    m = mod_ref[0]
    o_ref[...] = _rmsnorm(h_ref[...] + m[5:6] * acc, fw_ref[...])


def _combine_call(y4, wts, h, mod, fw, prev_out, *, tg, first_tile):
    n, d = h.shape
    t_per_b = n // mod.shape[0]
    row = lambda width: pl.BlockSpec((tg, width), lambda i: (i + first_tile, 0))
    in_specs = [pl.BlockSpec((TOP_K, tg, d // 2), lambda i: (0, i, 0)),
                row(LANES), row(d),
                pl.BlockSpec((1, N_MOD, d), lambda i: ((i + first_tile) * tg // t_per_b, 0, 0)),
                pl.BlockSpec((1, d), lambda i: (0, 0))]
    args = [y4, wts, h, mod, fw]
    kern = _combine_kernel
    aliases = {}
    if prev_out is not None:
        in_specs.append(pl.BlockSpec(memory_space=pl.ANY))
        args.append(prev_out)
        kern = lambda y4_ref, wt_ref, h_ref, mod_ref, fw_ref, prev_ref, o_ref: _combine_kernel(
            y4_ref, wt_ref, h_ref, mod_ref, fw_ref, o_ref)
        aliases = {len(args) - 1: 0}
    return pl.pallas_call(
        kern,
        grid=(y4.shape[1] // tg,),
        in_specs=in_specs,
        out_specs=row(d),
        out_shape=jax.ShapeDtypeStruct((n, d), F32),
        input_output_aliases=aliases,
        compiler_params=pltpu.CompilerParams(dimension_semantics=("arbitrary",)),
        name="combine",
    )(*args)


def kernel(x, c, ctx, c_ctx, w_ada, b_ada, norm_mix_w, norm_mlp_w, w_in, w_gk_f, b_gk_f, w_gk_b, b_gk_b,
           gla_norm_w, w_pool, pool_scale, w_out, w_router, b_router, w_gu, b_gu, w_down, b_down,
           final_norm_w):
    b, t, d = x.shape
    assert w_ada.shape[0] == 1, "single-layer trunk"
    n_exp = w_router.shape[2]
    rank = w_gk_f.shape[1]
    qk = w_gk_f.shape[2]
    dk = qk // GLA_HEADS
    gw = GLA_HEADS * gla_norm_w.shape[1]
    pw = w_pool.shape[1] * w_pool.shape[2]
    assert w_in.shape[2] == 2 * qk + 2 * gw + 2 * rank + pw and 2 * rank <= LANES
    assert t % SUPER == 0 and ctx.shape[1] % SUPER == 0 and n_exp <= LANES

    rows = -(-(b + 1) // 8) * 8
    cc = jnp.zeros((rows, d), F32).at[:b].set(c).at[b].set(c_ctx)
    mod = _mod_call(cc, w_ada[0], b_ada)
    mod_x = mod[:b].reshape(b, N_MOD, d)
    mod_c = mod[b:b + 1].reshape(1, N_MOD, d)

    wi = w_in[0]
    o_r = 2 * qk + 2 * gw
    w_cat = jnp.concatenate([wi[:, :o_r], wi[:, o_r + 2 * rank:], wi[:, o_r:o_r + 2 * rank],
                             jnp.zeros((d, LANES - 2 * rank), F32)], axis=1).astype(BF16)
    wgk = jnp.zeros((LANES, 2 * qk), F32).at[:rank, :qk].set(w_gk_f[0]).at[rank:2 * rank, qk:].set(w_gk_b[0])
    bgk = jnp.concatenate([b_gk_f[0], b_gk_b[0]])[None, :]
    w_vt = wi[:, 2 * qk:2 * qk + gw].T.astype(BF16)
    proj = functools.partial(_inproj_call, nw=norm_mix_w, w=w_cat, wvt=w_vt, wgk=wgk.astype(BF16), bgk=bgk,
                             qk=qk, gw=gw, pw=pw, dk=dk)
    q, k, v, vt, g, xp, gk = proj(x, mod_x, tm=512)
    _, kc, _, vtc, _, _, gkc = proj(ctx, mod_c, tm=SUPER)

    gla = _gla_call(q, k, v, vt, gk, g, kc, vtc, gkc, gla_norm_w)
    pool = _pool_call(xp, w_pool[0].astype(BF16), pool_scale)

    n = b * t
    wr = jnp.zeros((d, LANES), F32).at[:, :n_exp].set(w_router[0])
    br = jnp.zeros((1, LANES), F32).at[0, :n_exp].set(b_router[0])
    h, xt, code_t, wts, cnt = _route_call(
        gla.reshape(n, gw), pool.reshape(n, pw), x.reshape(n, d), mod_x, w_out[0].astype(BF16),
        norm_mlp_w, wr, br, tm=512, n_exp=n_exp)

    assert n < RANK_LIMIT
    n_pad = n * TOP_K + n_exp * EXPERT_ROWS
    nblk = n_pad // EXPERT_ROWS
    assert nblk <= 8 * LANES
    dest_t, block_e, n_valid, n_used = _plan_call(cnt[0, :n_exp], code_t, n_exp=n_exp, rows=EXPERT_ROWS,
                                                  chunk=min(n, 4096))
    dest_t = dest_t[:TOP_K]
    block_e = block_e.reshape(-1)[:nblk]
    n_valid = n_valid.reshape(-1)[:nblk]

    assert n % (SC_WORKERS * SC_WINDOW) == 0
    idx_scatter = dest_t.reshape(TOP_K, n // SC_WINDOW, SC_WINDOW).transpose(1, 0, 2)
    xs = _sc_scatter_call(xt, idx_scatter, n_out=n_pad)
    ys = _expert_call(block_e, n_used, n_valid, xs, w_gu[0], b_gu[0], w_down[0], b_down[0])

    tg = 256
    half = n // 2
    assert (half * TOP_K) % (SC_WORKERS * 2 * SC_GATHER_WINDOW) == 0 and half % tg == 0
    out = None
    for part in range(2):
        idx_gather = dest_t[:, part * half:(part + 1) * half].reshape(
            SC_WORKERS, half * TOP_K // (SC_WORKERS * SC_GATHER_WINDOW), SC_GATHER_WINDOW)
        y4 = _sc_gather_call(ys, idx_gather).reshape(TOP_K, half, d // 2)
        out = _combine_call(y4, wts, h, mod_x, final_norm_w[None, :], out, tg=tg, first_tile=part * half // tg)
    return out.reshape(b, t, d)
```

```python
import functools

import numpy as np
import jax
import jax.numpy as jnp
from jax import lax
from jax.experimental import pallas as pl
from jax.experimental.pallas import tpu as pltpu
from jax.experimental.pallas import tpu_sc as plsc

F32 = jnp.float32
BF16 = jnp.bfloat16
I32 = jnp.int32
U32 = jnp.uint32

GRID_W = 64
GLA_HEADS = 4
GLA_CHUNK = 64
GATE_NORMALIZER = 16.0
POOL_WINDOWS = (2, 4, 8, 16)
TOP_K = 4
RANK_LIMIT = 1 << 20
SWIGLU_LIMIT = 7.0
SWIGLU_ALPHA = 1.702
N_MOD = 6
EPS = 1e-6

LANES = 128
SUPER = 256
HEAD_PAIR_DK = 128
EXPERT_ROWS = 512
EXPERT_VMEM_BYTES = 56 * 1024 * 1024
SC_CORES = 2
SC_SUBCORES = 16
SC_WORKERS = SC_CORES * SC_SUBCORES
SC_WINDOW = 32
SC_GATHER_WINDOW = 64


def _dot(a, b):
    return jnp.dot(a, b, preferred_element_type=F32)


def _dot_nt(a, b):
    return lax.dot_general(a, b, (((1,), (1,)), ((), ())), preferred_element_type=F32)


def _split_bf16(x):
    hi = x.astype(BF16)
    lo = (x - hi.astype(F32)).astype(BF16)
    return hi, lo


def _pack_bf16_pairs(x):
    c = x.shape[1] // 2
    lo = lax.bitcast_convert_type(x[:, :c].astype(BF16).astype(F32), U32)
    hi = lax.bitcast_convert_type(x[:, c:].astype(BF16).astype(F32), U32)
    return (lo >> 16) | hi


def _unpack_bf16_pairs(p):
    lo = lax.bitcast_convert_type(p << 16, F32)
    hi = lax.bitcast_convert_type(p & jnp.uint32(0xFFFF0000), F32)
    return lo, hi


def _rmsnorm(x, w):
    var = jnp.mean(x * x, axis=-1, keepdims=True)
    return x * lax.rsqrt(var + EPS) * w


def _mod_kernel(c_ref, w_ref, b_ref, o_ref):
    c = c_ref[...]
    s = c * jax.nn.sigmoid(c)
    o_ref[...] = jnp.dot(s, w_ref[...], precision=lax.Precision.HIGHEST,
                         preferred_element_type=F32) + b_ref[...]


def _mod_call(cc, w_ada, b_ada):
    rows, d = cc.shape
    n = w_ada.shape[1]
    tn = 1024
    return pl.pallas_call(
        _mod_kernel,
        grid=(n // tn,),
        in_specs=[pl.BlockSpec((rows, d), lambda j: (0, 0)),
                  pl.BlockSpec((d, tn), lambda j: (0, j)),
                  pl.BlockSpec((1, tn), lambda j: (0, j))],
        out_specs=pl.BlockSpec((rows, tn), lambda j: (0, j)),
        out_shape=jax.ShapeDtypeStruct((rows, n), F32),
        name="mod",
    )(cc, w_ada, b_ada)


def _inproj_kernel(x_ref, mod_ref, nw_ref, w_ref, wgk_ref, bgk_ref,
                   q_ref, k_ref, v_ref, vt_ref, g_ref, p_ref, gk_ref, *, qk, gw, pw, dk):
    x = x_ref[0]
    m = mod_ref[0]
    hm = (_rmsnorm(x, nw_ref[...]) * (1.0 + m[1:2]) + m[0:1]).astype(BF16)
    p = _dot(hm, w_ref[...])
    vt = p[:, 2 * qk:2 * qk + gw].T
    for s in range(vt_ref.shape[1]):
        vt_ref[0, s] = vt[:, s * SUPER:(s + 1) * SUPER].astype(BF16)
    o = 0
    q_ref[0] = (p[:, o:o + qk] * (dk ** -0.5)).astype(BF16); o += qk
    k_ref[0] = p[:, o:o + qk].astype(BF16); o += qk
    v_ref[0] = p[:, o:o + gw].astype(BF16); o += gw
    g_ref[0] = p[:, o:o + gw].astype(BF16); o += gw
    p_ref[0] = p[:, o:o + pw].astype(BF16); o += pw
    r = p[:, o:o + LANES]
    z = _dot(r.astype(BF16), wgk_ref[...]) + bgk_ref[...]
    gk_ref[0] = (jnp.minimum(z, 0.0) - jnp.log1p(jnp.exp(-jnp.abs(z)))) * (1.0 / GATE_NORMALIZER)


def _inproj_call(x, mod, nw, w, wgk, bgk, *, qk, gw, pw, dk, tm):
    b, t, d = x.shape
    n_in = w.shape[1]
    bs = lambda width: pl.BlockSpec((1, tm, width), lambda i, j: (i, j, 0))
    const = lambda shape: pl.BlockSpec(shape, lambda i, j: (0,) * len(shape))
    per_batch = mod.shape[0] > 1
    sds = jax.ShapeDtypeStruct
    return pl.pallas_call(
        functools.partial(_inproj_kernel, qk=qk, gw=gw, pw=pw, dk=dk),
        grid=(b, t // tm),
        in_specs=[bs(d),
                  pl.BlockSpec((1, N_MOD, d), (lambda i, j: (i, 0, 0)) if per_batch else (lambda i, j: (0, 0, 0))),
                  const((1, d)), const((d, n_in)), const((LANES, 2 * qk)), const((1, 2 * qk))],
        out_specs=[bs(qk), bs(qk), bs(gw),
                   pl.BlockSpec((1, tm // SUPER, gw, SUPER), lambda i, j: (i, j, 0, 0)),
                   bs(gw), bs(pw), bs(2 * qk)],
        out_shape=[sds((b, t, qk), BF16), sds((b, t, qk), BF16), sds((b, t, gw), BF16),
                   sds((b, t // SUPER, gw, SUPER), BF16),
                   sds((b, t, gw), BF16), sds((b, t, pw), BF16), sds((b, t, 2 * qk), F32)],
        compiler_params=pltpu.CompilerParams(dimension_semantics=("arbitrary", "arbitrary")),
        name="inproj",
    )(x, mod, nw, w, wgk, bgk)


def _gla_super(q, k, v, vt, gk, cm, amask, bd_mask, st, fwd, want_out):
    nch = SUPER // GLA_CHUNK
    order = tuple(range(nch)) if fwd else tuple(reversed(range(nch)))
    last_row = GLA_CHUNK - 1 if fwd else 0
    mid_row = GLA_CHUNK // 2 - 1 if fwd else GLA_CHUNK // 2
    hi, lo = _split_bf16(gk)
    bcum = _dot(cm, hi) + _dot(cm, lo)

    def chunk_row(r):
        return jnp.concatenate(
            [jnp.broadcast_to(bcum[c * GLA_CHUNK + r:c * GLA_CHUNK + r + 1, :], (GLA_CHUNK, bcum.shape[1]))
             for c in range(nch)], axis=0)

    chunk_of_row = lax.broadcasted_iota(I32, bcum.shape, 0) // GLA_CHUNK

    def by_chunk(x):
        return jnp.concatenate([jnp.where(chunk_of_row == c, x, 0.0).astype(BF16) for c in range(nch)], axis=1)

    blast = chunk_row(last_row)
    u_all = _dot(vt, by_chunk(k * jnp.exp(blast - bcum)))
    before = [None] * nch
    for c in order:
        before[c] = st
        decay = jnp.exp(bcum[c * GLA_CHUNK + last_row:c * GLA_CHUNK + last_row + 1, :])
        st = st * decay + jnp.where(bd_mask, u_all[:, c * HEAD_PAIR_DK:(c + 1) * HEAD_PAIR_DK], 0.0)
    if not want_out:
        return None, st
    bmid = chunk_row(mid_row)
    qt = q * jnp.exp(bcum - bmid)
    kt = (k * jnp.exp(bmid - bcum)).astype(BF16)
    lane = lax.broadcasted_iota(I32, qt.shape, 1)
    half = HEAD_PAIR_DK // 2
    o_heads = []
    for hh in range(2):
        sel = (lane < half) if hh == 0 else (lane >= half)
        a = _dot_nt(jnp.where(sel, qt, 0.0).astype(BF16), kt)
        a = jnp.where(amask, a, 0.0).astype(BF16)
        o_heads.append(_dot(a, v[:, hh * LANES:(hh + 1) * LANES]))
    qh = (q * jnp.exp(bcum)).astype(BF16)
    o_inter = jnp.concatenate(
        [_dot_nt(qh[c * GLA_CHUNK:(c + 1) * GLA_CHUNK], before[c].astype(BF16)) for c in range(nch)], axis=0)
    return jnp.concatenate(o_heads, axis=1) + o_inter, st


def _gla_kernel(q_ref, k_ref, v_ref, vt_ref, gkf_ref, gkb_ref, g_ref, kc_ref, vtc_ref, gkfc_ref, gkbc_ref,
                nw_ref, cmf_ref, cmb_ref, o_ref, stf_ref, stb_ref, of_ref, ob_ref):
    t = q_ref.shape[1]
    tc = kc_ref.shape[1]
    nsc, nscc = t // SUPER, tc // SUPER
    cmf = cmf_ref[...]
    cmb = cmb_ref[...]
    amask_f = cmf > 0
    amask_b = cmb > 0
    row = lax.broadcasted_iota(I32, (2 * LANES, HEAD_PAIR_DK), 0)
    lane = lax.broadcasted_iota(I32, (2 * LANES, HEAD_PAIR_DK), 1)
    bd_mask = (row < LANES) == (lane < HEAD_PAIR_DK // 2)

    def ctx_state(gk_ref, cm, fwd, j, st):
        rows = pl.ds(j * SUPER, SUPER)
        return _gla_super(None, kc_ref[0, rows, :].astype(F32), None, vtc_ref[0, j], gk_ref[0, rows, :],
                          cm, None, bd_mask, st, fwd, False)[1]

    def latent(gk_ref, cm, amask, fwd, j, st):
        rows = pl.ds(pl.multiple_of(j * SUPER, SUPER), SUPER)
        return _gla_super(q_ref[0, rows, :].astype(F32), k_ref[0, rows, :].astype(F32), v_ref[0, rows, :],
                          vt_ref[0, j], gk_ref[0, rows, :], cm, amask, bd_mask, st, fwd, True)

    st = jnp.zeros(stf_ref.shape, F32)
    for j in range(nscc):
        st = ctx_state(gkfc_ref, cmf, True, j, st)
    stf_ref[...] = st
    st = jnp.zeros(stb_ref.shape, F32)
    for j in reversed(range(nscc)):
        st = ctx_state(gkbc_ref, cmb, False, j, st)
    stb_ref[...] = st

    def scan_body(jj, carry):
        jb = nsc - 1 - jj
        of, stf = latent(gkf_ref, cmf, amask_f, True, jj, stf_ref[...])
        of_ref[pl.ds(pl.multiple_of(jj * SUPER, SUPER), SUPER), :] = of
        stf_ref[...] = stf
        ob, stb = latent(gkb_ref, cmb, amask_b, False, jb, stb_ref[...])
        ob_ref[pl.ds(pl.multiple_of(jb * SUPER, SUPER), SUPER), :] = ob
        stb_ref[...] = stb
        return carry

    lax.fori_loop(0, nsc, scan_body, 0, unroll=2)

    nw = nw_ref[...]

    def out_body(j, carry):
        rows = pl.ds(pl.multiple_of(j * SUPER, SUPER), SUPER)
        o = of_ref[rows, :] + ob_ref[rows, :]
        g = g_ref[0, rows, :].astype(F32)
        gate = g * jax.nn.sigmoid(g)
        for hh in range(2):
            oh = o[:, hh * LANES:(hh + 1) * LANES]
            on = oh * lax.rsqrt(jnp.mean(oh * oh, axis=-1, keepdims=True) + EPS) * nw
            o_ref[0, rows, hh * LANES:(hh + 1) * LANES] = (on * gate[:, hh * LANES:(hh + 1) * LANES]).astype(BF16)
        return carry

    lax.fori_loop(0, nsc, out_body, 0)


def _gla_masks():
    i = np.arange(SUPER)
    same = (i[:, None] // GLA_CHUNK) == (i[None, :] // GLA_CHUNK)
    fwd = same & (i[None, :] <= i[:, None])
    bwd = same & (i[None, :] >= i[:, None])
    return jnp.asarray(fwd, BF16), jnp.asarray(bwd, BF16)


def _gla_call(q, k, v, vt, gk, g, kc, vtc, gkc, nw):
    b, t, qk = q.shape
    tc = kc.shape[1]
    npair = qk // HEAD_PAIR_DK
    cmf, cmb = _gla_masks()
    lat = lambda width, off: pl.BlockSpec((1, t, width), lambda i, j: (i, 0, j + off))
    ctx = lambda width, off: pl.BlockSpec((1, tc, width), lambda i, j: (i, 0, j + off))
    tr = lambda n_groups: pl.BlockSpec((1, n_groups, 2 * LANES, SUPER), lambda i, j: (i, 0, j, 0))
    const = lambda shape: pl.BlockSpec(shape, lambda i, j: (0,) * len(shape))
    return pl.pallas_call(
        _gla_kernel,
        grid=(b, npair),
        in_specs=[lat(HEAD_PAIR_DK, 0), lat(HEAD_PAIR_DK, 0), lat(2 * LANES, 0), tr(t // SUPER),
                  lat(HEAD_PAIR_DK, 0), lat(HEAD_PAIR_DK, npair), lat(2 * LANES, 0),
                  ctx(HEAD_PAIR_DK, 0), tr(tc // SUPER), ctx(HEAD_PAIR_DK, 0), ctx(HEAD_PAIR_DK, npair),
                  const((1, LANES)), const(cmf.shape), const(cmb.shape)],
        out_specs=lat(2 * LANES, 0),
        out_shape=jax.ShapeDtypeStruct((b, t, v.shape[2]), BF16),
        scratch_shapes=[pltpu.VMEM((2 * LANES, HEAD_PAIR_DK), F32), pltpu.VMEM((2 * LANES, HEAD_PAIR_DK), F32),
                        pltpu.VMEM((t, 2 * LANES), F32), pltpu.VMEM((t, 2 * LANES), F32)],
        compiler_params=pltpu.CompilerParams(dimension_semantics=("arbitrary", "arbitrary")),
        name="gla",
    )(q, k, v, vt, gk, gk, g, kc, vtc, gkc, gkc, nw, cmf, cmb)


def _pool_kernel(x_ref, cm_ref, wp_ref, ps_ref, o_ref, y_ref, z_ref, *, rows):
    t = x_ref.shape[1]
    tok = lax.broadcasted_iota(I32, (t, LANES), 0)
    r = tok // GRID_W
    c = tok % GRID_W
    for gi, w in enumerate(POOL_WINDOWS):
        lo = w // 2
        hi = w - 1 - lo
        cols = slice(gi * LANES, (gi + 1) * LANES)
        cmat = cm_ref[gi]
        for blk in range(t // SUPER):
            rs = slice(blk * SUPER, (blk + 1) * SUPER)
            y_ref[rs, :] = _dot(cmat, x_ref[0, rs, cols])
        z_ref[...] = y_ref[...]
        for dr in range(-lo, hi + 1):
            sh = abs(dr) * GRID_W
            if dr == 0 or sh >= t:
                continue
            if dr > 0:
                z_ref[0:t - sh, :] += y_ref[sh:t, :]
            else:
                z_ref[sh:t, :] += y_ref[0:t - sh, :]
        cnt_r = jnp.minimum(r + hi + 1, rows) - jnp.maximum(r - lo, 0)
        cnt_c = jnp.minimum(c + hi + 1, GRID_W) - jnp.maximum(c - lo, 0)
        cnt = (cnt_r * cnt_c).astype(F32)
        pooled = z_ref[...] / cnt - x_ref[0, :, cols].astype(F32)
        yp = _dot(pooled.astype(BF16), wp_ref[gi]) * ps_ref[:, cols]
        o_ref[0, :, cols] = yp.astype(BF16)


def _pool_col_mats():
    i = np.arange(SUPER)
    same_row = (i[:, None] // GRID_W) == (i[None, :] // GRID_W)
    d = i[None, :] - i[:, None]
    mats = []
    for w in POOL_WINDOWS:
        lo = w // 2
        hi = w - 1 - lo
        mats.append(same_row & (d >= -lo) & (d <= hi))
    return jnp.asarray(np.stack(mats), BF16)


def _pool_call(xp, w_pool, pool_scale):
    b, t, pw = xp.shape
    ng = len(POOL_WINDOWS)
    cm = _pool_col_mats()
    const = lambda shape: pl.BlockSpec(shape, lambda i: (0,) * len(shape))
    return pl.pallas_call(
        functools.partial(_pool_kernel, rows=t // GRID_W),
        grid=(b,),
        in_specs=[pl.BlockSpec((1, t, pw), lambda i: (i, 0, 0)),
                  const(cm.shape), const((ng, LANES, LANES)), const((1, pw))],
        out_specs=pl.BlockSpec((1, t, pw), lambda i: (i, 0, 0)),
        out_shape=jax.ShapeDtypeStruct((b, t, pw), BF16),
        scratch_shapes=[pltpu.VMEM((t, LANES), F32), pltpu.VMEM((t, LANES), F32)],
        compiler_params=pltpu.CompilerParams(dimension_semantics=("arbitrary",)),
        name="pool",
    )(xp, cm, w_pool, pool_scale)


def _route_kernel(gla_ref, pool_ref, x_ref, mod_ref, wo_ref, nw_ref, wr_ref, br_ref, lt_ref,
                  h_ref, xt_ref, code_ref, wt_ref, cnt_ref, run_ref, *, gw, n_exp):
    i = pl.program_id(0)

    @pl.when(i == 0)
    def _():
        run_ref[...] = jnp.zeros_like(run_ref)

    m = mod_ref[0]
    acc = _dot(gla_ref[...], wo_ref[0:gw, :]) + _dot(pool_ref[...], wo_ref[gw:, :])
    h = x_ref[...] + m[2:3] * acc
    h_ref[...] = h
    xt = _rmsnorm(h, nw_ref[...]) * (1.0 + m[4:5]) + m[3:4]
    xt_ref[...] = _pack_bf16_pairs(xt)
    xh, xl = _split_bf16(xt)
    wr = wr_ref[...]
    wh, wl = _split_bf16(wr)
    logits = _dot(xh, wh) + _dot(xl, wh) + _dot(xh, wl) + br_ref[...]
    lane = lax.broadcasted_iota(I32, logits.shape, 1)
    neg = jnp.float32(-jnp.inf)
    logits = jnp.where(lane < n_exp, logits, neg)
    vals, hots = [], []
    e_out = jnp.zeros(logits.shape, I32)
    for j in range(TOP_K):
        mx = jnp.max(logits, axis=-1, keepdims=True)
        idx = jnp.min(jnp.where(logits == mx, lane, LANES), axis=-1, keepdims=True)
        hot = lane == idx
        vals.append(mx)
        hots.append(hot)
        e_out = jnp.where(lane == j, idx, e_out)
        logits = jnp.where(hot, neg, logits)
    ex = [jnp.exp(v - vals[0]) for v in vals]
    den = ex[0] + ex[1] + ex[2] + ex[3]
    w_out = jnp.zeros(logits.shape, F32)
    for j in range(TOP_K):
        w_out = jnp.where(lane == j, ex[j] / den, w_out)
    osum = jnp.where(hots[0] | hots[1] | hots[2] | hots[3], 1.0, 0.0)
    before = _dot(lt_ref[...], osum.astype(BF16)) + run_ref[0:1, :]
    rk_out = jnp.zeros(logits.shape, I32)
    for j in range(TOP_K):
        rj = jnp.sum(jnp.where(hots[j], before, 0.0), axis=-1, keepdims=True)
        rk_out = jnp.where(lane == j, rj.astype(I32), rk_out)
    run = run_ref[0:1, :] + jnp.sum(osum, axis=0, keepdims=True)
    run_ref[...] = jnp.broadcast_to(run, run_ref.shape)
    code = e_out * RANK_LIMIT + rk_out
    code_ref[...] = code.T[:code_ref.shape[0], :]
    wt_ref[...] = w_out
    cnt_ref[...] = jnp.broadcast_to(run, cnt_ref.shape).astype(I32)


def _route_call(gla, pool, x, mod, w_out, nw, wr, br, *, tm, n_exp):
    n, d = x.shape
    gw = gla.shape[1]
    t_per_b = n // mod.shape[0]
    lt = jnp.asarray(np.tril(np.ones((tm, tm), np.float32), -1), BF16)
    row = lambda width: pl.BlockSpec((tm, width), lambda i: (i, 0))
    const = lambda shape: pl.BlockSpec(shape, lambda i: (0,) * len(shape))
    sds = jax.ShapeDtypeStruct
    return pl.pallas_call(
        functools.partial(_route_kernel, gw=gw, n_exp=n_exp),
        grid=(n // tm,),
        in_specs=[row(gw), row(pool.shape[1]), row(d),
                  pl.BlockSpec((1, N_MOD, d), lambda i: (i * tm // t_per_b, 0, 0)),
                  const(w_out.shape), const((1, d)), const(wr.shape), const((1, LANES)), const((tm, tm))],
        out_specs=[row(d), row(d // 2), pl.BlockSpec((8, tm), lambda i: (0, i)), row(LANES), const((8, LANES))],
        out_shape=[sds((n, d), F32), sds((n, d // 2), U32), sds((8, n), I32),
                   sds((n, LANES), F32), sds((8, LANES), I32)],
        scratch_shapes=[pltpu.VMEM((8, LANES), F32)],
        compiler_params=pltpu.CompilerParams(dimension_semantics=("arbitrary",)),
        name="route",
    )(gla, pool, x, mod, w_out, nw, wr, br, lt)


def _plan_kernel(cnt_ref, code_ref, dest_ref, be_ref, nv_ref, nu_ref, start_ref, *, n_exp, rows):
    @pl.when(pl.program_id(0) == 0)
    def _():
        blk = (lax.broadcasted_iota(I32, be_ref.shape, 0) * LANES + lax.broadcasted_iota(I32, be_ref.shape, 1))
        blk_row0 = blk * rows
        be = jnp.zeros(be_ref.shape, I32)
        end_valid = jnp.zeros(be_ref.shape, I32)
        acc = jnp.int32(0)
        last_e = jnp.int32(0)
        for e in range(n_exp):
            c = cnt_ref[e]
            start_ref[e] = acc
            in_e = blk_row0 >= acc
            end_valid = jnp.where(in_e, acc + c, end_valid)
            be = jnp.where(in_e, e, be)
            acc = acc + (c + rows - 1) // rows * rows
            last_e = jnp.where(c > 0, e, last_e)
        n_used = acc // rows
        nu_ref[0] = n_used
        be_ref[...] = jnp.where(blk < n_used, be, last_e)
        nv_ref[...] = jnp.clip(end_valid - blk_row0, 0, rows)

    code = code_ref[...]
    e_vec = code // RANK_LIMIT
    dest = code % RANK_LIMIT
    for e in range(n_exp):
        dest = dest + jnp.where(e_vec == e, start_ref[e], 0)
    dest_ref[...] = dest


def _plan_call(counts, code_t, *, n_exp, rows, chunk):
    n = code_t.shape[1]
    sds = jax.ShapeDtypeStruct
    smem = pltpu.SMEM
    return pl.pallas_call(
        functools.partial(_plan_kernel, n_exp=n_exp, rows=rows),
        grid=(n // chunk,),
        in_specs=[pl.BlockSpec(memory_space=smem), pl.BlockSpec((8, chunk), lambda i: (0, i))],
        out_specs=[pl.BlockSpec((8, chunk), lambda i: (0, i)),
                   pl.BlockSpec((8, LANES), lambda i: (0, 0)), pl.BlockSpec((8, LANES), lambda i: (0, 0)),
                   pl.BlockSpec(memory_space=smem)],
        out_shape=[sds((8, n), I32), sds((8, LANES), I32), sds((8, LANES), I32), sds((1,), I32)],
        scratch_shapes=[pltpu.SMEM((n_exp,), I32)],
        compiler_params=pltpu.CompilerParams(dimension_semantics=("arbitrary",)),
        name="plan",
    )(counts, code_t)


def _sc_worker_id():
    return lax.axis_index("s") * SC_CORES + lax.axis_index("c")


def _sc_scatter_call(x, idx3, *, n_out):
    n, d = x.shape
    n_win_total, k, w = idx3.shape
    n_win = n_win_total // SC_WORKERS
    mesh = plsc.VectorSubcoreMesh(core_axis_name="c", subcore_axis_name="s")

    @functools.partial(
        pl.kernel, mesh=mesh,
        out_type=jax.ShapeDtypeStruct((n_out, d), x.dtype),
        scratch_types=[pltpu.VMEM((k, w), I32), pltpu.VMEM((w, d), x.dtype), pltpu.SemaphoreType.DMA],
        name="sc_dispatch",
    )
    def kern(x_hbm, idx_hbm, out_hbm, idx_v, rows_v, sem):
        wid = _sc_worker_id()

        @pl.loop(0, n_win)
        def _(i):
            win = wid * n_win + i
            pltpu.sync_copy(idx_hbm.at[win], idx_v)
            pltpu.sync_copy(x_hbm.at[pl.ds(win * w, w)], rows_v)
            for j in range(k):
                pltpu.async_copy(rows_v, out_hbm.at[idx_v.at[j]], sem).wait()

    return kern(x, idx3)


def _sc_gather_call(table, idx3):
    n_workers, n_win, w = idx3.shape
    d = table.shape[1]
    assert n_workers == SC_WORKERS and n_win % 2 == 0
    mesh = plsc.VectorSubcoreMesh(core_axis_name="c", subcore_axis_name="s")

    @functools.partial(
        pl.kernel, mesh=mesh,
        out_type=jax.ShapeDtypeStruct((n_workers * n_win * w, d), table.dtype),
        scratch_types=[pltpu.VMEM((n_win, w), I32), pltpu.VMEM((2, w, d), table.dtype),
                       pltpu.SemaphoreType.DMA((2,)), pltpu.SemaphoreType.DMA((2,))],
        name="sc_gather",
    )
    def kern(table_hbm, idx_hbm, out_hbm, idx_v, rows_v, gsem, osem):
        wid = _sc_worker_id()
        base = wid * n_win
        pltpu.sync_copy(idx_hbm.at[wid], idx_v)

        def gather(wi, b):
            return pltpu.make_async_copy(table_hbm.at[idx_v.at[wi]], rows_v.at[b], gsem.at[b])

        def put(wi, b):
            return pltpu.make_async_copy(rows_v.at[b], out_hbm.at[pl.ds((base + wi) * w, w)], osem.at[b])

        gather(0, 0).start()

        @pl.loop(0, n_win, step=2)
        def _(i):
            for b in range(2):
                wi = i + b

                @pl.when(wi + 1 < n_win)
                def _():
                    @pl.when(wi >= 1)
                    def _():
                        put(wi - 1, 1 - b).wait()
                    gather(wi + 1, 1 - b).start()

                gather(wi, b).wait()
                put(wi, b).start()

        put(n_win - 2, 0).wait()
        put(n_win - 1, 1).wait()

    return kern(table, idx3)


def _expert_kernel(be_ref, nu_ref, nv_ref, x_ref, wgu_ref, bgu_ref, wd_ref, bd_ref, y_ref, wgu_bf_ref, wd_bf_ref,
                   *, d_ff):
    i = pl.program_id(0)

    @pl.when((i == 0) | (be_ref[i] != be_ref[jnp.maximum(i - 1, 0)]))
    def _():
        wgu_bf_ref[...] = wgu_ref[0].astype(BF16)
        wd_bf_ref[...] = wd_ref[0].astype(BF16)

    @pl.when(i < nu_ref[0])
    def _():
        row = lax.broadcasted_iota(I32, x_ref.shape, 0)
        lo, hi = _unpack_bf16_pairs(jnp.where(row < nv_ref[i], x_ref[...], jnp.uint32(0)))
        xb = jnp.concatenate([lo, hi], axis=1).astype(BF16)
        gu = _dot(xb, wgu_bf_ref[...]) + bgu_ref[0]
        gate = jnp.minimum(gu[:, :d_ff], SWIGLU_LIMIT)
        up = jnp.clip(gu[:, d_ff:], -SWIGLU_LIMIT, SWIGLU_LIMIT)
        act = (up + 1.0) * gate * jax.nn.sigmoid(SWIGLU_ALPHA * gate)
        y_ref[...] = _pack_bf16_pairs(_dot(act.astype(BF16), wd_bf_ref[...]) + bd_ref[0])


def _expert_call(block_e, n_used, n_valid, xs, w_gu, b_gu, w_down, b_down):
    n_pad = xs.shape[0]
    n_exp, d, two_ff = w_gu.shape
    d_ff = two_ff // 2
    nblk = n_pad // EXPERT_ROWS
    rows = lambda i, be, nu, nv: (jnp.minimum(i, nu[0] - 1), 0)
    per_e = lambda i, be, nu, nv: (be[i], 0, 0)
    grid_spec = pltpu.PrefetchScalarGridSpec(
        num_scalar_prefetch=3,
        grid=(nblk,),
        in_specs=[pl.BlockSpec((EXPERT_ROWS, d // 2), rows),
                  pl.BlockSpec((1, d, two_ff), per_e), pl.BlockSpec((1, 1, two_ff), per_e),
                  pl.BlockSpec((1, d_ff, d), per_e), pl.BlockSpec((1, 1, d), per_e)],
        out_specs=pl.BlockSpec((EXPERT_ROWS, d // 2), rows),
        scratch_shapes=[pltpu.VMEM((d, two_ff), BF16), pltpu.VMEM((d_ff, d), BF16)],
    )
    return pl.pallas_call(
        functools.partial(_expert_kernel, d_ff=d_ff),
        grid_spec=grid_spec,
        out_shape=jax.ShapeDtypeStruct((n_pad, d // 2), U32),
        compiler_params=pltpu.CompilerParams(dimension_semantics=("arbitrary",),
                                             vmem_limit_bytes=EXPERT_VMEM_BYTES),
        name="experts",
    )(block_e, n_used, n_valid, xs, w_gu, b_gu.reshape(n_exp, 1, two_ff), w_down, b_down.reshape(n_exp, 1, d))


def _combine_kernel(y4_ref, wt_ref, h_ref, mod_ref, fw_ref, o_ref):
    wt = wt_ref[...]
    acc_lo, acc_hi = None, None
    for j in range(TOP_K):
        lo, hi = _unpack_bf16_pairs(y4_ref[j])
        w = wt[:, j:j + 1]
        acc_lo = w * lo if j == 0 else acc_lo + w * lo
        acc_hi = w * hi if j == 0 else acc_hi + w * hi
    acc = jnp.concatenate([acc_lo, acc_hi], axis=1)
    m = mod_ref[0]
    o_ref[...] = _rmsnorm(h_ref[...] + m[5:6] * acc, fw_ref[...])


def _combine_call(y4, wts, h, mod, fw, prev_out, *, tg, first_tile):
    n, d = h.shape
    t_per_b = n // mod.shape[0]
    row = lambda width: pl.BlockSpec((tg, width), lambda i: (i + first_tile, 0))
    in_specs = [pl.BlockSpec((TOP_K, tg, d // 2), lambda i: (0, i, 0)),
                row(LANES), row(d),
                pl.BlockSpec((1, N_MOD, d), lambda i: ((i + first_tile) * tg // t_per_b, 0, 0)),
                pl.BlockSpec((1, d), lambda i: (0, 0))]
    args = [y4, wts, h, mod, fw]
    kern = _combine_kernel
    aliases = {}
    if prev_out is not None:
        in_specs.append(pl.BlockSpec(memory_space=pl.ANY))
        args.append(prev_out)
        kern = lambda y4_ref, wt_ref, h_ref, mod_ref, fw_ref, prev_ref, o_ref: _combine_kernel(
            y4_ref, wt_ref, h_ref, mod_ref, fw_ref, o_ref)
        aliases = {len(args) - 1: 0}
    return pl.pallas_call(
        kern,
        grid=(y4.shape[1] // tg,),
        in_specs=in_specs,
        out_specs=row(d),
        out_shape=jax.ShapeDtypeStruct((n, d), F32),
        input_output_aliases=aliases,
        compiler_params=pltpu.CompilerParams(dimension_semantics=("arbitrary",)),
        name="combine",
    )(*args)


def kernel(x, c, ctx, c_ctx, w_ada, b_ada, norm_mix_w, norm_mlp_w, w_in, w_gk_f, b_gk_f, w_gk_b, b_gk_b,
           gla_norm_w, w_pool, pool_scale, w_out, w_router, b_router, w_gu, b_gu, w_down, b_down,
           final_norm_w):
    b, t, d = x.shape
    assert w_ada.shape[0] == 1, "single-layer trunk"
    n_exp = w_router.shape[2]
    rank = w_gk_f.shape[1]
    qk = w_gk_f.shape[2]
    dk = qk // GLA_HEADS
    gw = GLA_HEADS * gla_norm_w.shape[1]
    pw = w_pool.shape[1] * w_pool.shape[2]
    assert w_in.shape[2] == 2 * qk + 2 * gw + 2 * rank + pw and 2 * rank <= LANES
    assert t % SUPER == 0 and ctx.shape[1] % SUPER == 0 and n_exp <= LANES

    rows = -(-(b + 1) // 8) * 8
    cc = jnp.zeros((rows, d), F32).at[:b].set(c).at[b].set(c_ctx)
    mod = _mod_call(cc, w_ada[0], b_ada)
    mod_x = mod[:b].reshape(b, N_MOD, d)
    mod_c = mod[b:b + 1].reshape(1, N_MOD, d)

    wi = w_in[0]
    o_r = 2 * qk + 2 * gw
    w_cat = jnp.concatenate([wi[:, :o_r], wi[:, o_r + 2 * rank:], wi[:, o_r:o_r + 2 * rank],
                             jnp.zeros((d, LANES - 2 * rank), F32)], axis=1).astype(BF16)
    wgk = jnp.zeros((LANES, 2 * qk), F32).at[:rank, :qk].set(w_gk_f[0]).at[rank:2 * rank, qk:].set(w_gk_b[0])
    bgk = jnp.concatenate([b_gk_f[0], b_gk_b[0]])[None, :]
    proj = functools.partial(_inproj_call, nw=norm_mix_w, w=w_cat, wgk=wgk.astype(BF16), bgk=bgk,
                             qk=qk, gw=gw, pw=pw, dk=dk)
    q, k, v, vt, g, xp, gk = proj(x, mod_x, tm=512)
    _, kc, _, vtc, _, _, gkc = proj(ctx, mod_c, tm=SUPER)

    gla = _gla_call(q, k, v, vt, gk, g, kc, vtc, gkc, gla_norm_w)
    pool = _pool_call(xp, w_pool[0].astype(BF16), pool_scale)

    n = b * t
    wr = jnp.zeros((d, LANES), F32).at[:, :n_exp].set(w_router[0])
    br = jnp.zeros((1, LANES), F32).at[0, :n_exp].set(b_router[0])
    h, xt, code_t, wts, cnt = _route_call(
        gla.reshape(n, gw), pool.reshape(n, pw), x.reshape(n, d), mod_x, w_out[0].astype(BF16),
        norm_mlp_w, wr, br, tm=512, n_exp=n_exp)

    assert n < RANK_LIMIT
    n_pad = n * TOP_K + n_exp * EXPERT_ROWS
    nblk = n_pad // EXPERT_ROWS
    assert nblk <= 8 * LANES
    dest_t, block_e, n_valid, n_used = _plan_call(cnt[0, :n_exp], code_t, n_exp=n_exp, rows=EXPERT_ROWS,
                                                  chunk=min(n, 4096))
    dest_t = dest_t[:TOP_K]
    block_e = block_e.reshape(-1)[:nblk]
    n_valid = n_valid.reshape(-1)[:nblk]

    assert n % (SC_WORKERS * SC_WINDOW) == 0
    idx_scatter = dest_t.reshape(TOP_K, n // SC_WINDOW, SC_WINDOW).transpose(1, 0, 2)
    xs = _sc_scatter_call(xt, idx_scatter, n_out=n_pad)
    ys = _expert_call(block_e, n_used, n_valid, xs, w_gu[0], b_gu[0], w_down[0], b_down[0])

    tg = 256
    half = n // 2
    assert (half * TOP_K) % (SC_WORKERS * 2 * SC_GATHER_WINDOW) == 0 and half % tg == 0
    out = None
    for part in range(2):
        idx_gather = dest_t[:, part * half:(part + 1) * half].reshape(
            SC_WORKERS, half * TOP_K // (SC_WORKERS * SC_GATHER_WINDOW), SC_GATHER_WINDOW)
        y4 = _sc_gather_call(ys, idx_gather).reshape(TOP_K, half, d // 2)
        out = _combine_call(y4, wts, h, mod_x, final_norm_w[None, :], out, tg=tg, first_tile=part * half // tg)
    return out.reshape(b, t, d)
```

```python
import functools

import numpy as np
import jax
import jax.numpy as jnp
from jax import lax
from jax.experimental import pallas as pl
from jax.experimental.pallas import tpu as pltpu
from jax.experimental.pallas import tpu_sc as plsc

F32 = jnp.float32
BF16 = jnp.bfloat16
I32 = jnp.int32
U32 = jnp.uint32

GRID_W = 64
GLA_HEADS = 4
GLA_CHUNK = 64
GATE_NORMALIZER = 16.0
POOL_WINDOWS = (2, 4, 8, 16)
POOL_PAD_GRID_ROWS = 8
TOP_K = 4
RANK_LIMIT = 1 << 20
SWIGLU_LIMIT = 7.0
SWIGLU_ALPHA = 1.702
N_MOD = 6
EPS = 1e-6

LANES = 128
SUPER = 256
HEAD_PAIR_DK = 128
EXPERT_ROWS = 512
EXPERT_VMEM_BYTES = 56 * 1024 * 1024
SC_CORES = 2
SC_SUBCORES = 16
SC_WORKERS = SC_CORES * SC_SUBCORES
SC_WINDOW = 32
SC_GATHER_WINDOW = 64


def _dot(a, b):
    return jnp.dot(a, b, preferred_element_type=F32)


def _dot_nt(a, b):
    return lax.dot_general(a, b, (((1,), (1,)), ((), ())), preferred_element_type=F32)


def _split_bf16(x):
    hi = x.astype(BF16)
    lo = (x - hi.astype(F32)).astype(BF16)
    return hi, lo


def _pack_bf16_pairs(x):
    c = x.shape[1] // 2
    lo = lax.bitcast_convert_type(x[:, :c].astype(BF16).astype(F32), U32)
    hi = lax.bitcast_convert_type(x[:, c:].astype(BF16).astype(F32), U32)
    return (lo >> 16) | hi


def _unpack_bf16_pairs(p):
    lo = lax.bitcast_convert_type(p << 16, F32)
    hi = lax.bitcast_convert_type(p & jnp.uint32(0xFFFF0000), F32)
    return lo, hi


def _rmsnorm(x, w):
    var = jnp.mean(x * x, axis=-1, keepdims=True)
    return x * lax.rsqrt(var + EPS) * w


def _mod_kernel(c_ref, w_ref, b_ref, o_ref):
    c = c_ref[...]
    s = c * jax.nn.sigmoid(c)
    o_ref[...] = jnp.dot(s, w_ref[...], precision=lax.Precision.HIGHEST,
                         preferred_element_type=F32) + b_ref[...]


def _mod_call(cc, w_ada, b_ada):
    rows, d = cc.shape
    n = w_ada.shape[1]
    tn = 1024
    return pl.pallas_call(
        _mod_kernel,
        grid=(n // tn,),
        in_specs=[pl.BlockSpec((rows, d), lambda j: (0, 0)),
                  pl.BlockSpec((d, tn), lambda j: (0, j)),
                  pl.BlockSpec((1, tn), lambda j: (0, j))],
        out_specs=pl.BlockSpec((rows, tn), lambda j: (0, j)),
        out_shape=jax.ShapeDtypeStruct((rows, n), F32),
        name="mod",
    )(cc, w_ada, b_ada)


def _inproj_kernel(x_ref, mod_ref, nw_ref, w_ref, wgk_ref, bgk_ref,
                   q_ref, k_ref, v_ref, vt_ref, g_ref, p_ref, gk_ref, *, qk, gw, pw, dk):
    x = x_ref[0]
    m = mod_ref[0]
    hm = (_rmsnorm(x, nw_ref[...]) * (1.0 + m[1:2]) + m[0:1]).astype(BF16)
    p = _dot(hm, w_ref[...])
    vt = p[:, 2 * qk:2 * qk + gw].T
    for s in range(vt_ref.shape[1]):
        vt_ref[0, s] = vt[:, s * SUPER:(s + 1) * SUPER].astype(BF16)
    o = 0
    q_ref[0] = (p[:, o:o + qk] * (dk ** -0.5)).astype(BF16); o += qk
    k_ref[0] = p[:, o:o + qk].astype(BF16); o += qk
    v_ref[0] = p[:, o:o + gw].astype(BF16); o += gw
    g_ref[0] = p[:, o:o + gw].astype(BF16); o += gw
    p_ref[0] = p[:, o:o + pw].astype(BF16); o += pw
    r = p[:, o:o + LANES]
    z = _dot(r.astype(BF16), wgk_ref[...]) + bgk_ref[...]
    gk_ref[0] = (jnp.minimum(z, 0.0) - jnp.log1p(jnp.exp(-jnp.abs(z)))) * (1.0 / GATE_NORMALIZER)


def _inproj_call(x, mod, nw, w, wgk, bgk, *, qk, gw, pw, dk, tm):
    b, t, d = x.shape
    n_in = w.shape[1]
    bs = lambda width: pl.BlockSpec((1, tm, width), lambda i, j: (i, j, 0))
    const = lambda shape: pl.BlockSpec(shape, lambda i, j: (0,) * len(shape))
    per_batch = mod.shape[0] > 1
    sds = jax.ShapeDtypeStruct
    return pl.pallas_call(
        functools.partial(_inproj_kernel, qk=qk, gw=gw, pw=pw, dk=dk),
        grid=(b, t // tm),
        in_specs=[bs(d),
                  pl.BlockSpec((1, N_MOD, d), (lambda i, j: (i, 0, 0)) if per_batch else (lambda i, j: (0, 0, 0))),
                  const((1, d)), const((d, n_in)), const((LANES, 2 * qk)), const((1, 2 * qk))],
        out_specs=[bs(qk), bs(qk), bs(gw),
                   pl.BlockSpec((1, tm // SUPER, gw, SUPER), lambda i, j: (i, j, 0, 0)),
                   bs(gw), bs(pw), bs(2 * qk)],
        out_shape=[sds((b, t, qk), BF16), sds((b, t, qk), BF16), sds((b, t, gw), BF16),
                   sds((b, t // SUPER, gw, SUPER), BF16),
                   sds((b, t, gw), BF16), sds((b, t, pw), BF16), sds((b, t, 2 * qk), F32)],
        compiler_params=pltpu.CompilerParams(dimension_semantics=("arbitrary", "arbitrary")),
        name="inproj",
    )(x, mod, nw, w, wgk, bgk)


def _gla_super(q, k, v, vt, gk, cm, amask, bd_mask, st, fwd, want_out):
    nch = SUPER // GLA_CHUNK
    order = tuple(range(nch)) if fwd else tuple(reversed(range(nch)))
    last_row = GLA_CHUNK - 1 if fwd else 0
    mid_row = GLA_CHUNK // 2 - 1 if fwd else GLA_CHUNK // 2
    hi, lo = _split_bf16(gk)
    bcum = _dot(cm, hi) + _dot(cm, lo)

    def chunk_row(r):
        return jnp.concatenate(
            [jnp.broadcast_to(bcum[c * GLA_CHUNK + r:c * GLA_CHUNK + r + 1, :], (GLA_CHUNK, bcum.shape[1]))
             for c in range(nch)], axis=0)

    chunk_of_row = lax.broadcasted_iota(I32, bcum.shape, 0) // GLA_CHUNK

    def by_chunk(x):
        return jnp.concatenate([jnp.where(chunk_of_row == c, x, 0.0).astype(BF16) for c in range(nch)], axis=1)

    blast = chunk_row(last_row)
    u_all = _dot(vt, by_chunk(k * jnp.exp(blast - bcum)))
    before = [None] * nch
    for c in order:
        before[c] = st
        decay = jnp.exp(bcum[c * GLA_CHUNK + last_row:c * GLA_CHUNK + last_row + 1, :])
        st = st * decay + jnp.where(bd_mask, u_all[:, c * HEAD_PAIR_DK:(c + 1) * HEAD_PAIR_DK], 0.0)
    if not want_out:
        return None, st
    bmid = chunk_row(mid_row)
    qt = q * jnp.exp(bcum - bmid)
    kt = (k * jnp.exp(bmid - bcum)).astype(BF16)
    lane = lax.broadcasted_iota(I32, qt.shape, 1)
    half = HEAD_PAIR_DK // 2
    o_heads = []
    for hh in range(2):
        sel = (lane < half) if hh == 0 else (lane >= half)
        a = _dot_nt(jnp.where(sel, qt, 0.0).astype(BF16), kt)
        a = jnp.where(amask, a, 0.0).astype(BF16)
        o_heads.append(_dot(a, v[:, hh * LANES:(hh + 1) * LANES]))
    qh = (q * jnp.exp(bcum)).astype(BF16)
    o_inter = jnp.concatenate(
        [_dot_nt(qh[c * GLA_CHUNK:(c + 1) * GLA_CHUNK], before[c].astype(BF16)) for c in range(nch)], axis=0)
    return jnp.concatenate(o_heads, axis=1) + o_inter, st


def _gla_kernel(q_ref, k_ref, v_ref, vt_ref, gkf_ref, gkb_ref, g_ref, kc_ref, vtc_ref, gkfc_ref, gkbc_ref,
                nw_ref, cmf_ref, cmb_ref, o_ref, stf_ref, stb_ref, of_ref, ob_ref):
    t = q_ref.shape[1]
    tc = kc_ref.shape[1]
    nsc, nscc = t // SUPER, tc // SUPER
    cmf = cmf_ref[...]
    cmb = cmb_ref[...]
    amask_f = cmf > 0
    amask_b = cmb > 0
    row = lax.broadcasted_iota(I32, (2 * LANES, HEAD_PAIR_DK), 0)
    lane = lax.broadcasted_iota(I32, (2 * LANES, HEAD_PAIR_DK), 1)
    bd_mask = (row < LANES) == (lane < HEAD_PAIR_DK // 2)

    def ctx_state(gk_ref, cm, fwd, j, st):
        rows = pl.ds(j * SUPER, SUPER)
        return _gla_super(None, kc_ref[0, rows, :].astype(F32), None, vtc_ref[0, j], gk_ref[0, rows, :],
                          cm, None, bd_mask, st, fwd, False)[1]

    def latent(gk_ref, cm, amask, fwd, j, st):
        rows = pl.ds(pl.multiple_of(j * SUPER, SUPER), SUPER)
        return _gla_super(q_ref[0, rows, :].astype(F32), k_ref[0, rows, :].astype(F32), v_ref[0, rows, :],
                          vt_ref[0, j], gk_ref[0, rows, :], cm, amask, bd_mask, st, fwd, True)

    st = jnp.zeros(stf_ref.shape, F32)
    for j in range(nscc):
        st = ctx_state(gkfc_ref, cmf, True, j, st)
    stf_ref[...] = st
    st = jnp.zeros(stb_ref.shape, F32)
    for j in reversed(range(nscc)):
        st = ctx_state(gkbc_ref, cmb, False, j, st)
    stb_ref[...] = st

    def scan_body(jj, carry):
        jb = nsc - 1 - jj
        of, stf = latent(gkf_ref, cmf, amask_f, True, jj, stf_ref[...])
        of_ref[pl.ds(pl.multiple_of(jj * SUPER, SUPER), SUPER), :] = of
        stf_ref[...] = stf
        ob, stb = latent(gkb_ref, cmb, amask_b, False, jb, stb_ref[...])
        ob_ref[pl.ds(pl.multiple_of(jb * SUPER, SUPER), SUPER), :] = ob
        stb_ref[...] = stb
        return carry

    lax.fori_loop(0, nsc, scan_body, 0, unroll=2)

    nw = nw_ref[...]

    def out_body(j, carry):
        rows = pl.ds(pl.multiple_of(j * SUPER, SUPER), SUPER)
        o = of_ref[rows, :] + ob_ref[rows, :]
        g = g_ref[0, rows, :].astype(F32)
        gate = g * jax.nn.sigmoid(g)
        for hh in range(2):
            oh = o[:, hh * LANES:(hh + 1) * LANES]
            on = oh * lax.rsqrt(jnp.mean(oh * oh, axis=-1, keepdims=True) + EPS) * nw
            o_ref[0, rows, hh * LANES:(hh + 1) * LANES] = (on * gate[:, hh * LANES:(hh + 1) * LANES]).astype(BF16)
        return carry

    lax.fori_loop(0, nsc, out_body, 0)


def _gla_masks():
    i = np.arange(SUPER)
    same = (i[:, None] // GLA_CHUNK) == (i[None, :] // GLA_CHUNK)
    fwd = same & (i[None, :] <= i[:, None])
    bwd = same & (i[None, :] >= i[:, None])
    return jnp.asarray(fwd, BF16), jnp.asarray(bwd, BF16)


def _gla_call(q, k, v, vt, gk, g, kc, vtc, gkc, nw):
    b, t, qk = q.shape
    tc = kc.shape[1]
    npair = qk // HEAD_PAIR_DK
    cmf, cmb = _gla_masks()
    lat = lambda width, off: pl.BlockSpec((1, t, width), lambda i, j: (i, 0, j + off))
    ctx = lambda width, off: pl.BlockSpec((1, tc, width), lambda i, j: (i, 0, j + off))
    tr = lambda n_groups: pl.BlockSpec((1, n_groups, 2 * LANES, SUPER), lambda i, j: (i, 0, j, 0))
    const = lambda shape: pl.BlockSpec(shape, lambda i, j: (0,) * len(shape))
    return pl.pallas_call(
        _gla_kernel,
        grid=(b, npair),
        in_specs=[lat(HEAD_PAIR_DK, 0), lat(HEAD_PAIR_DK, 0), lat(2 * LANES, 0), tr(t // SUPER),
                  lat(HEAD_PAIR_DK, 0), lat(HEAD_PAIR_DK, npair), lat(2 * LANES, 0),
                  ctx(HEAD_PAIR_DK, 0), tr(tc // SUPER), ctx(HEAD_PAIR_DK, 0), ctx(HEAD_PAIR_DK, npair),
                  const((1, LANES)), const(cmf.shape), const(cmb.shape)],
        out_specs=lat(2 * LANES, 0),
        out_shape=jax.ShapeDtypeStruct((b, t, v.shape[2]), BF16),
        scratch_shapes=[pltpu.VMEM((2 * LANES, HEAD_PAIR_DK), F32), pltpu.VMEM((2 * LANES, HEAD_PAIR_DK), F32),
                        pltpu.VMEM((t, 2 * LANES), F32), pltpu.VMEM((t, 2 * LANES), F32)],
        compiler_params=pltpu.CompilerParams(dimension_semantics=("arbitrary", "arbitrary")),
        name="gla",
    )(q, k, v, vt, gk, gk, g, kc, vtc, gkc, gkc, nw, cmf, cmb)


def _pool_kernel(x_ref, cm_ref, cnt_ref, wp_ref, ps_ref, o_ref, a_ref, b_ref):
    t = x_ref.shape[1]
    pad = POOL_PAD_GRID_ROWS * GRID_W
    total = t + 2 * pad
    for gi, w in enumerate(POOL_WINDOWS):
        lo = w // 2
        cols = slice(gi * LANES, (gi + 1) * LANES)
        cmat = cm_ref[gi]
        a_ref[0:pad, :] = jnp.zeros((pad, LANES), F32)
        a_ref[pad + t:total, :] = jnp.zeros((pad, LANES), F32)
        for blk in range(t // SUPER):
            rs = slice(blk * SUPER, (blk + 1) * SUPER)
            a_ref[pad + blk * SUPER:pad + (blk + 1) * SUPER, :] = _dot(cmat, x_ref[0, rs, cols])
        src, dst = a_ref, b_ref
        m = 1
        while m < w:
            sh = m * GRID_W
            dst[0:total - sh, :] = src[0:total - sh, :] + src[sh:total, :]
            src, dst = dst, src
            m *= 2
        first = pad - lo * GRID_W
        pooled = src[first:first + t, :] / cnt_ref[gi] - x_ref[0, :, cols].astype(F32)
        yp = _dot(pooled.astype(BF16), wp_ref[gi]) * ps_ref[:, cols]
        o_ref[0, :, cols] = yp.astype(BF16)


def _pool_col_mats():
    i = np.arange(SUPER)
    same_row = (i[:, None] // GRID_W) == (i[None, :] // GRID_W)
    d = i[None, :] - i[:, None]
    mats = []
    for w in POOL_WINDOWS:
        lo = w // 2
        hi = w - 1 - lo
        mats.append(same_row & (d >= -lo) & (d <= hi))
    return jnp.asarray(np.stack(mats), BF16)


def _pool_counts(t):
    rows = t // GRID_W
    r = np.arange(t) // GRID_W
    c = np.arange(t) % GRID_W
    out = []
    for w in POOL_WINDOWS:
        lo = w // 2
        hi = w - 1 - lo
        cnt_r = np.minimum(r + hi + 1, rows) - np.maximum(r - lo, 0)
        cnt_c = np.minimum(c + hi + 1, GRID_W) - np.maximum(c - lo, 0)
        out.append(np.broadcast_to((cnt_r * cnt_c).astype(np.float32)[:, None], (t, LANES)))
    return jnp.asarray(np.stack(out))


def _pool_call(xp, w_pool, pool_scale):
    b, t, pw = xp.shape
    ng = len(POOL_WINDOWS)
    assert max(POOL_WINDOWS) // 2 <= POOL_PAD_GRID_ROWS and t % GRID_W == 0
    cm = _pool_col_mats()
    cnt = _pool_counts(t)
    staged = t + 2 * POOL_PAD_GRID_ROWS * GRID_W
    const = lambda shape: pl.BlockSpec(shape, lambda i: (0,) * len(shape))
    return pl.pallas_call(
        _pool_kernel,
        grid=(b,),
        in_specs=[pl.BlockSpec((1, t, pw), lambda i: (i, 0, 0)),
                  const(cm.shape), const(cnt.shape), const((ng, LANES, LANES)), const((1, pw))],
        out_specs=pl.BlockSpec((1, t, pw), lambda i: (i, 0, 0)),
        out_shape=jax.ShapeDtypeStruct((b, t, pw), BF16),
        scratch_shapes=[pltpu.VMEM((staged, LANES), F32), pltpu.VMEM((staged, LANES), F32)],
        compiler_params=pltpu.CompilerParams(dimension_semantics=("arbitrary",)),
        name="pool",
    )(xp, cm, cnt, w_pool, pool_scale)


def _route_kernel(gla_ref, pool_ref, x_ref, mod_ref, wo_ref, nw_ref, wr_ref, br_ref, lt_ref,
                  h_ref, xt_ref, code_ref, wt_ref, cnt_ref, run_ref, wr2_ref, *, gw, n_exp):
    i = pl.program_id(0)

    @pl.when(i == 0)
    def _():
        run_ref[...] = jnp.zeros_like(run_ref)
        wh, wl = _split_bf16(wr_ref[...])
        wr2_ref[:, :LANES] = wh
        wr2_ref[:, LANES:] = wl

    m = mod_ref[0]
    acc = _dot(gla_ref[...], wo_ref[0:gw, :]) + _dot(pool_ref[...], wo_ref[gw:, :])
    h = x_ref[...] + m[2:3] * acc
    h_ref[...] = h
    xt = _rmsnorm(h, nw_ref[...]) * (1.0 + m[4:5]) + m[3:4]
    xt_ref[...] = _pack_bf16_pairs(xt)
    xh, xl = _split_bf16(xt)
    wr2 = wr2_ref[...]
    t1 = _dot(xh, wr2)
    logits = t1[:, :LANES] + t1[:, LANES:] + _dot(xl, wr2[:, :LANES]) + br_ref[...]
    lane = lax.broadcasted_iota(I32, logits.shape, 1)
    neg = jnp.float32(-jnp.inf)
    logits = jnp.where(lane < n_exp, logits, neg)
    vals, hots = [], []
    e_out = jnp.zeros(logits.shape, I32)
    for j in range(TOP_K):
        mx = jnp.max(logits, axis=-1, keepdims=True)
        idx = jnp.min(jnp.where(logits == mx, lane, LANES), axis=-1, keepdims=True)
        hot = lane == idx
        vals.append(mx)
        hots.append(hot)
        e_out = jnp.where(lane == j, idx, e_out)
        logits = jnp.where(hot, neg, logits)
    ex = [jnp.exp(v - vals[0]) for v in vals]
    den = ex[0] + ex[1] + ex[2] + ex[3]
    w_out = jnp.zeros(logits.shape, F32)
    for j in range(TOP_K):
        w_out = jnp.where(lane == j, ex[j] / den, w_out)
    osum = jnp.where(hots[0] | hots[1] | hots[2] | hots[3], 1.0, 0.0)
    before = _dot(lt_ref[...], osum.astype(BF16)) + run_ref[0:1, :]
    rk_out = jnp.zeros(logits.shape, I32)
    for j in range(TOP_K):
        rj = jnp.sum(jnp.where(hots[j], before, 0.0), axis=-1, keepdims=True)
        rk_out = jnp.where(lane == j, rj.astype(I32), rk_out)
    run = run_ref[0:1, :] + jnp.sum(osum, axis=0, keepdims=True)
    run_ref[...] = jnp.broadcast_to(run, run_ref.shape)
    code = e_out * RANK_LIMIT + rk_out
    code_ref[...] = code.T[:code_ref.shape[0], :]
    wt_ref[...] = w_out
    cnt_ref[...] = jnp.broadcast_to(run, cnt_ref.shape).astype(I32)


def _route_call(gla, pool, x, mod, w_out, nw, wr, br, *, tm, n_exp):
    n, d = x.shape
    gw = gla.shape[1]
    t_per_b = n // mod.shape[0]
    lt = jnp.asarray(np.tril(np.ones((tm, tm), np.float32), -1), BF16)
    row = lambda width: pl.BlockSpec((tm, width), lambda i: (i, 0))
    const = lambda shape: pl.BlockSpec(shape, lambda i: (0,) * len(shape))
    sds = jax.ShapeDtypeStruct
    return pl.pallas_call(
        functools.partial(_route_kernel, gw=gw, n_exp=n_exp),
        grid=(n // tm,),
        in_specs=[row(gw), row(pool.shape[1]), row(d),
                  pl.BlockSpec((1, N_MOD, d), lambda i: (i * tm // t_per_b, 0, 0)),
                  const(w_out.shape), const((1, d)), const(wr.shape), const((1, LANES)), const((tm, tm))],
        out_specs=[row(d), row(d // 2), pl.BlockSpec((8, tm), lambda i: (0, i)), row(LANES), const((8, LANES))],
        out_shape=[sds((n, d), F32), sds((n, d // 2), U32), sds((8, n), I32),
                   sds((n, LANES), F32), sds((8, LANES), I32)],
        scratch_shapes=[pltpu.VMEM((8, LANES), F32), pltpu.VMEM((d, 2 * LANES), BF16)],
        compiler_params=pltpu.CompilerParams(dimension_semantics=("arbitrary",)),
        name="route",
    )(gla, pool, x, mod, w_out, nw, wr, br, lt)


def _plan_kernel(cnt_ref, code_ref, dest_ref, be_ref, nv_ref, nu_ref, start_ref, *, n_exp, rows):
    @pl.when(pl.program_id(0) == 0)
    def _():
        blk = (lax.broadcasted_iota(I32, be_ref.shape, 0) * LANES + lax.broadcasted_iota(I32, be_ref.shape, 1))
        blk_row0 = blk * rows
        be = jnp.zeros(be_ref.shape, I32)
        end_valid = jnp.zeros(be_ref.shape, I32)
        acc = jnp.int32(0)
        last_e = jnp.int32(0)
        for e in range(n_exp):
            c = cnt_ref[e]
            start_ref[e] = acc
            in_e = blk_row0 >= acc
            end_valid = jnp.where(in_e, acc + c, end_valid)
            be = jnp.where(in_e, e, be)
            acc = acc + (c + rows - 1) // rows * rows
            last_e = jnp.where(c > 0, e, last_e)
        n_used = acc // rows
        nu_ref[0] = n_used
        be_ref[...] = jnp.where(blk < n_used, be, last_e)
        nv_ref[...] = jnp.clip(end_valid - blk_row0, 0, rows)

    code = code_ref[...]
    e_vec = code // RANK_LIMIT
    dest = code % RANK_LIMIT
    for e in range(n_exp):
        dest = dest + jnp.where(e_vec == e, start_ref[e], 0)
    dest_ref[...] = dest


def _plan_call(counts, code_t, *, n_exp, rows, chunk):
    n = code_t.shape[1]
    sds = jax.ShapeDtypeStruct
    smem = pltpu.SMEM
    return pl.pallas_call(
        functools.partial(_plan_kernel, n_exp=n_exp, rows=rows),
        grid=(n // chunk,),
        in_specs=[pl.BlockSpec(memory_space=smem), pl.BlockSpec((8, chunk), lambda i: (0, i))],
        out_specs=[pl.BlockSpec((8, chunk), lambda i: (0, i)),
                   pl.BlockSpec((8, LANES), lambda i: (0, 0)), pl.BlockSpec((8, LANES), lambda i: (0, 0)),
                   pl.BlockSpec(memory_space=smem)],
        out_shape=[sds((8, n), I32), sds((8, LANES), I32), sds((8, LANES), I32), sds((1,), I32)],
        scratch_shapes=[pltpu.SMEM((n_exp,), I32)],
        compiler_params=pltpu.CompilerParams(dimension_semantics=("arbitrary",)),
        name="plan",
    )(counts, code_t)


def _sc_worker_id():
    return lax.axis_index("s") * SC_CORES + lax.axis_index("c")


def _sc_scatter_call(x, idx3, *, n_out):
    n, d = x.shape
    n_win_total, k, w = idx3.shape
    n_win = n_win_total // SC_WORKERS
    mesh = plsc.VectorSubcoreMesh(core_axis_name="c", subcore_axis_name="s")

    @functools.partial(
        pl.kernel, mesh=mesh,
        out_type=jax.ShapeDtypeStruct((n_out, d), x.dtype),
        scratch_types=[pltpu.VMEM((k, w), I32), pltpu.VMEM((w, d), x.dtype), pltpu.SemaphoreType.DMA],
        name="sc_dispatch",
    )
    def kern(x_hbm, idx_hbm, out_hbm, idx_v, rows_v, sem):
        wid = _sc_worker_id()

        @pl.loop(0, n_win)
        def _(i):
            win = wid * n_win + i
            pltpu.sync_copy(idx_hbm.at[win], idx_v)
            pltpu.sync_copy(x_hbm.at[pl.ds(win * w, w)], rows_v)
            for j in range(k):
                pltpu.async_copy(rows_v, out_hbm.at[idx_v.at[j]], sem).wait()

    return kern(x, idx3)


def _sc_gather_call(table, idx3):
    n_workers, n_win, w = idx3.shape
    d = table.shape[1]
    assert n_workers == SC_WORKERS and n_win % 2 == 0
    mesh = plsc.VectorSubcoreMesh(core_axis_name="c", subcore_axis_name="s")

    @functools.partial(
        pl.kernel, mesh=mesh,
        out_type=jax.ShapeDtypeStruct((n_workers * n_win * w, d), table.dtype),
        scratch_types=[pltpu.VMEM((n_win, w), I32), pltpu.VMEM((2, w, d), table.dtype),
                       pltpu.SemaphoreType.DMA((2,)), pltpu.SemaphoreType.DMA((2,))],
        name="sc_gather",
    )
    def kern(table_hbm, idx_hbm, out_hbm, idx_v, rows_v, gsem, osem):
        wid = _sc_worker_id()
        base = wid * n_win
        pltpu.sync_copy(idx_hbm.at[wid], idx_v)

        def gather(wi, b):
            return pltpu.make_async_copy(table_hbm.at[idx_v.at[wi]], rows_v.at[b], gsem.at[b])

        def put(wi, b):
            return pltpu.make_async_copy(rows_v.at[b], out_hbm.at[pl.ds((base + wi) * w, w)], osem.at[b])

        gather(0, 0).start()

        @pl.loop(0, n_win, step=2)
        def _(i):
            for b in range(2):
                wi = i + b

                @pl.when(wi + 1 < n_win)
                def _():
                    @pl.when(wi >= 1)
                    def _():
                        put(wi - 1, 1 - b).wait()
                    gather(wi + 1, 1 - b).start()

                gather(wi, b).wait()
                put(wi, b).start()

        put(n_win - 2, 0).wait()
        put(n_win - 1, 1).wait()

    return kern(table, idx3)


def _expert_kernel(be_ref, nu_ref, nv_ref, x_ref, wgu_ref, bgu_ref, wd_ref, bd_ref, y_ref, wgu_bf_ref, wd_bf_ref,
                   *, d_ff):
    i = pl.program_id(0)

    @pl.when((i == 0) | (be_ref[i] != be_ref[jnp.maximum(i - 1, 0)]))
    def _():
        wgu_bf_ref[...] = wgu_ref[0].astype(BF16)
        wd_bf_ref[...] = wd_ref[0].astype(BF16)

    @pl.when(i < nu_ref[0])
    def _():
        row = lax.broadcasted_iota(I32, x_ref.shape, 0)
        lo, hi = _unpack_bf16_pairs(jnp.where(row < nv_ref[i], x_ref[...], jnp.uint32(0)))
        xb = jnp.concatenate([lo, hi], axis=1).astype(BF16)
        gu = _dot(xb, wgu_bf_ref[...]) + bgu_ref[0]
        gate = jnp.minimum(gu[:, :d_ff], SWIGLU_LIMIT)
        up = jnp.clip(gu[:, d_ff:], -SWIGLU_LIMIT, SWIGLU_LIMIT)
        act = (up + 1.0) * gate * jax.nn.sigmoid(SWIGLU_ALPHA * gate)
        y_ref[...] = _pack_bf16_pairs(_dot(act.astype(BF16), wd_bf_ref[...]) + bd_ref[0])


def _expert_call(block_e, n_used, n_valid, xs, w_gu, b_gu, w_down, b_down):
    n_pad = xs.shape[0]
    n_exp, d, two_ff = w_gu.shape
    d_ff = two_ff // 2
    nblk = n_pad // EXPERT_ROWS
    rows = lambda i, be, nu, nv: (jnp.minimum(i, nu[0] - 1), 0)
    per_e = lambda i, be, nu, nv: (be[i], 0, 0)
    grid_spec = pltpu.PrefetchScalarGridSpec(
        num_scalar_prefetch=3,
        grid=(nblk,),
        in_specs=[pl.BlockSpec((EXPERT_ROWS, d // 2), rows),
                  pl.BlockSpec((1, d, two_ff), per_e), pl.BlockSpec((1, 1, two_ff), per_e),
                  pl.BlockSpec((1, d_ff, d), per_e), pl.BlockSpec((1, 1, d), per_e)],
        out_specs=pl.BlockSpec((EXPERT_ROWS, d // 2), rows),
        scratch_shapes=[pltpu.VMEM((d, two_ff), BF16), pltpu.VMEM((d_ff, d), BF16)],
    )
    return pl.pallas_call(
        functools.partial(_expert_kernel, d_ff=d_ff),
        grid_spec=grid_spec,
        out_shape=jax.ShapeDtypeStruct((n_pad, d // 2), U32),
        compiler_params=pltpu.CompilerParams(dimension_semantics=("arbitrary",),
                                             vmem_limit_bytes=EXPERT_VMEM_BYTES),
        name="experts",
    )(block_e, n_used, n_valid, xs, w_gu, b_gu.reshape(n_exp, 1, two_ff), w_down, b_down.reshape(n_exp, 1, d))


def _combine_kernel(y4_ref, wt_ref, h_ref, mod_ref, fw_ref, o_ref):
    wt = wt_ref[...]
    acc_lo, acc_hi = None, None
    for j in range(TOP_K):
        lo, hi = _unpack_bf16_pairs(y4_ref[j])
        w = wt[:, j:j + 1]
        acc_lo = w * lo if j == 0 else acc_lo + w * lo
        acc_hi = w * hi if j == 0 else acc_hi + w * hi
    acc = jnp.concatenate([acc_lo, acc_hi], axis=1)
    m = mod_ref[0]
    o_ref[...] = _rmsnorm(h_ref[...] + m[5:6] * acc, fw_ref[...])


def _combine_call(y4, wts, h, mod, fw, prev_out, *, tg, first_tile):
    n, d = h.shape
    t_per_b = n // mod.shape[0]
    row = lambda width: pl.BlockSpec((tg, width), lambda i: (i + first_tile, 0))
    in_specs = [pl.BlockSpec((TOP_K, tg, d // 2), lambda i: (0, i, 0)),
                row(LANES), row(d),
                pl.BlockSpec((1, N_MOD, d), lambda i: ((i + first_tile) * tg // t_per_b, 0, 0)),
                pl.BlockSpec((1, d), lambda i: (0, 0))]
    args = [y4, wts, h, mod, fw]
    kern = _combine_kernel
    aliases = {}
    if prev_out is not None:
        in_specs.append(pl.BlockSpec(memory_space=pl.ANY))
        args.append(prev_out)
        kern = lambda y4_ref, wt_ref, h_ref, mod_ref, fw_ref, prev_ref, o_ref: _combine_kernel(
            y4_ref, wt_ref, h_ref, mod_ref, fw_ref, o_ref)
        aliases = {len(args) - 1: 0}
    return pl.pallas_call(
        kern,
        grid=(y4.shape[1] // tg,),
        in_specs=in_specs,
        out_specs=row(d),
        out_shape=jax.ShapeDtypeStruct((n, d), F32),
        input_output_aliases=aliases,
        compiler_params=pltpu.CompilerParams(dimension_semantics=("arbitrary",)),
        name="combine",
    )(*args)


def kernel(x, c, ctx, c_ctx, w_ada, b_ada, norm_mix_w, norm_mlp_w, w_in, w_gk_f, b_gk_f, w_gk_b, b_gk_b,
           gla_norm_w, w_pool, pool_scale, w_out, w_router, b_router, w_gu, b_gu, w_down, b_down,
           final_norm_w):
    b, t, d = x.shape
    assert w_ada.shape[0] == 1, "single-layer trunk"
    n_exp = w_router.shape[2]
    rank = w_gk_f.shape[1]
    qk = w_gk_f.shape[2]
    dk = qk // GLA_HEADS
    gw = GLA_HEADS * gla_norm_w.shape[1]
    pw = w_pool.shape[1] * w_pool.shape[2]
    assert w_in.shape[2] == 2 * qk + 2 * gw + 2 * rank + pw and 2 * rank <= LANES
    assert t % SUPER == 0 and ctx.shape[1] % SUPER == 0 and n_exp <= LANES

    rows = -(-(b + 1) // 8) * 8
    cc = jnp.zeros((rows, d), F32).at[:b].set(c).at[b].set(c_ctx)
    mod = _mod_call(cc, w_ada[0], b_ada)
    mod_x = mod[:b].reshape(b, N_MOD, d)
    mod_c = mod[b:b + 1].reshape(1, N_MOD, d)

    wi = w_in[0]
    o_r = 2 * qk + 2 * gw
    w_cat = jnp.concatenate([wi[:, :o_r], wi[:, o_r + 2 * rank:], wi[:, o_r:o_r + 2 * rank],
                             jnp.zeros((d, LANES - 2 * rank), F32)], axis=1).astype(BF16)
    wgk = jnp.zeros((LANES, 2 * qk), F32).at[:rank, :qk].set(w_gk_f[0]).at[rank:2 * rank, qk:].set(w_gk_b[0])
    bgk = jnp.concatenate([b_gk_f[0], b_gk_b[0]])[None, :]
    proj = functools.partial(_inproj_call, nw=norm_mix_w, w=w_cat, wgk=wgk.astype(BF16), bgk=bgk,
                             qk=qk, gw=gw, pw=pw, dk=dk)
    q, k, v, vt, g, xp, gk = proj(x, mod_x, tm=512)
    _, kc, _, vtc, _, _, gkc = proj(ctx, mod_c, tm=SUPER)

    gla = _gla_call(q, k, v, vt, gk, g, kc, vtc, gkc, gla_norm_w)
    pool = _pool_call(xp, w_pool[0].astype(BF16), pool_scale)

    n = b * t
    wr = jnp.zeros((d, LANES), F32).at[:, :n_exp].set(w_router[0])
    br = jnp.zeros((1, LANES), F32).at[0, :n_exp].set(b_router[0])
    h, xt, code_t, wts, cnt = _route_call(
        gla.reshape(n, gw), pool.reshape(n, pw), x.reshape(n, d), mod_x, w_out[0].astype(BF16),
        norm_mlp_w, wr, br, tm=512, n_exp=n_exp)

    assert n < RANK_LIMIT
    n_pad = n * TOP_K + n_exp * EXPERT_ROWS
    nblk = n_pad // EXPERT_ROWS
    assert nblk <= 8 * LANES
    dest_t, block_e, n_valid, n_used = _plan_call(cnt[0, :n_exp], code_t, n_exp=n_exp, rows=EXPERT_ROWS,
                                                  chunk=min(n, 4096))
    dest_t = dest_t[:TOP_K]
    block_e = block_e.reshape(-1)[:nblk]
    n_valid = n_valid.reshape(-1)[:nblk]

    assert n % (SC_WORKERS * SC_WINDOW) == 0
    idx_scatter = dest_t.reshape(TOP_K, n // SC_WINDOW, SC_WINDOW).transpose(1, 0, 2)
    xs = _sc_scatter_call(xt, idx_scatter, n_out=n_pad)
    ys = _expert_call(block_e, n_used, n_valid, xs, w_gu[0], b_gu[0], w_down[0], b_down[0])

    tg = 256
    half = n // 2
    assert (half * TOP_K) % (SC_WORKERS * 2 * SC_GATHER_WINDOW) == 0 and half % tg == 0
    out = None
    for part in range(2):
        idx_gather = dest_t[:, part * half:(part + 1) * half].reshape(
            SC_WORKERS, half * TOP_K // (SC_WORKERS * SC_GATHER_WINDOW), SC_GATHER_WINDOW)
        y4 = _sc_gather_call(ys, idx_gather).reshape(TOP_K, half, d // 2)
        out = _combine_call(y4, wts, h, mod_x, final_norm_w[None, :], out, tg=tg, first_tile=part * half // tg)
    return out.reshape(b, t, d)
```

```python
import functools

import numpy as np
import jax
import jax.numpy as jnp
from jax import lax
from jax.experimental import pallas as pl
from jax.experimental.pallas import tpu as pltpu
from jax.experimental.pallas import tpu_sc as plsc

F32 = jnp.float32
BF16 = jnp.bfloat16
I32 = jnp.int32
U32 = jnp.uint32

GRID_W = 64
GLA_HEADS = 4
GLA_CHUNK = 64
GATE_NORMALIZER = 16.0
POOL_WINDOWS = (2, 4, 8, 16)
POOL_PAD_GRID_ROWS = 8
TOP_K = 4
RANK_LIMIT = 1 << 20
SWIGLU_LIMIT = 7.0
SWIGLU_ALPHA = 1.702
N_MOD = 6
EPS = 1e-6

LANES = 128
SUPER = 256
HEAD_PAIR_DK = 128
EXPERT_ROWS = 512
MOE_PARTS = 2
EXPERT_VMEM_BYTES = 56 * 1024 * 1024
SC_CORES = 2
SC_SUBCORES = 16
SC_WORKERS = SC_CORES * SC_SUBCORES
SC_WINDOW = 32
SC_GATHER_WINDOW = 64


def _dot(a, b):
    return jnp.dot(a, b, preferred_element_type=F32)


def _dot_nt(a, b):
    return lax.dot_general(a, b, (((1,), (1,)), ((), ())), preferred_element_type=F32)


def _split_bf16(x):
    hi = x.astype(BF16)
    lo = (x - hi.astype(F32)).astype(BF16)
    return hi, lo


def _pack_bf16_pairs(x):
    c = x.shape[1] // 2
    lo = lax.bitcast_convert_type(x[:, :c].astype(BF16).astype(F32), U32)
    hi = lax.bitcast_convert_type(x[:, c:].astype(BF16).astype(F32), U32)
    return (lo >> 16) | hi


def _unpack_bf16_pairs(p):
    lo = lax.bitcast_convert_type(p << 16, F32)
    hi = lax.bitcast_convert_type(p & jnp.uint32(0xFFFF0000), F32)
    return lo, hi


def _rmsnorm(x, w):
    var = jnp.mean(x * x, axis=-1, keepdims=True)
    return x * lax.rsqrt(var + EPS) * w


def _mod_kernel(c_ref, w_ref, b_ref, o_ref):
    c = c_ref[...]
    s = c * jax.nn.sigmoid(c)
    o_ref[...] = jnp.dot(s, w_ref[...], precision=lax.Precision.HIGHEST,
                         preferred_element_type=F32) + b_ref[...]


def _mod_call(cc, w_ada, b_ada):
    rows, d = cc.shape
    n = w_ada.shape[1]
    tn = 1024
    return pl.pallas_call(
        _mod_kernel,
        grid=(n // tn,),
        in_specs=[pl.BlockSpec((rows, d), lambda j: (0, 0)),
                  pl.BlockSpec((d, tn), lambda j: (0, j)),
                  pl.BlockSpec((1, tn), lambda j: (0, j))],
        out_specs=pl.BlockSpec((rows, tn), lambda j: (0, j)),
        out_shape=jax.ShapeDtypeStruct((rows, n), F32),
        name="mod",
    )(cc, w_ada, b_ada)


def _inproj_kernel(x_ref, mod_ref, nw_ref, w_ref, wgk_ref, bgk_ref,
                   q_ref, k_ref, v_ref, vt_ref, g_ref, p_ref, gk_ref, *, qk, gw, pw, dk):
    x = x_ref[0]
    m = mod_ref[0]
    hm = (_rmsnorm(x, nw_ref[...]) * (1.0 + m[1:2]) + m[0:1]).astype(BF16)
    p = _dot(hm, w_ref[...])
    vt = p[:, 2 * qk:2 * qk + gw].T
    for s in range(vt_ref.shape[1]):
        vt_ref[0, s] = vt[:, s * SUPER:(s + 1) * SUPER].astype(BF16)
    o = 0
    q_ref[0] = (p[:, o:o + qk] * (dk ** -0.5)).astype(BF16); o += qk
    k_ref[0] = p[:, o:o + qk].astype(BF16); o += qk
    v_ref[0] = p[:, o:o + gw].astype(BF16); o += gw
    g_ref[0] = p[:, o:o + gw].astype(BF16); o += gw
    p_ref[0] = p[:, o:o + pw].astype(BF16); o += pw
    r = p[:, o:o + LANES]
    z = _dot(r.astype(BF16), wgk_ref[...]) + bgk_ref[...]
    gk_ref[0] = (jnp.minimum(z, 0.0) - jnp.log1p(jnp.exp(-jnp.abs(z)))) * (1.0 / GATE_NORMALIZER)


def _inproj_call(x, mod, nw, w, wgk, bgk, *, qk, gw, pw, dk, tm):
    b, t, d = x.shape
    n_in = w.shape[1]
    bs = lambda width: pl.BlockSpec((1, tm, width), lambda i, j: (i, j, 0))
    const = lambda shape: pl.BlockSpec(shape, lambda i, j: (0,) * len(shape))
    per_batch = mod.shape[0] > 1
    sds = jax.ShapeDtypeStruct
    return pl.pallas_call(
        functools.partial(_inproj_kernel, qk=qk, gw=gw, pw=pw, dk=dk),
        grid=(b, t // tm),
        in_specs=[bs(d),
                  pl.BlockSpec((1, N_MOD, d), (lambda i, j: (i, 0, 0)) if per_batch else (lambda i, j: (0, 0, 0))),
                  const((1, d)), const((d, n_in)), const((LANES, 2 * qk)), const((1, 2 * qk))],
        out_specs=[bs(qk), bs(qk), bs(gw),
                   pl.BlockSpec((1, tm // SUPER, gw, SUPER), lambda i, j: (i, j, 0, 0)),
                   bs(gw), bs(pw), bs(2 * qk)],
        out_shape=[sds((b, t, qk), BF16), sds((b, t, qk), BF16), sds((b, t, gw), BF16),
                   sds((b, t // SUPER, gw, SUPER), BF16),
                   sds((b, t, gw), BF16), sds((b, t, pw), BF16), sds((b, t, 2 * qk), F32)],
        compiler_params=pltpu.CompilerParams(dimension_semantics=("arbitrary", "arbitrary")),
        name="inproj",
    )(x, mod, nw, w, wgk, bgk)


def _gla_super(q, k, v, vt, gk, cm, amask, bd_mask, st, fwd, want_out):
    nch = SUPER // GLA_CHUNK
    order = tuple(range(nch)) if fwd else tuple(reversed(range(nch)))
    last_row = GLA_CHUNK - 1 if fwd else 0
    mid_row = GLA_CHUNK // 2 - 1 if fwd else GLA_CHUNK // 2
    hi, lo = _split_bf16(gk)
    bcum = _dot(cm, hi) + _dot(cm, lo)

    def chunk_row(r):
        return jnp.concatenate(
            [jnp.broadcast_to(bcum[c * GLA_CHUNK + r:c * GLA_CHUNK + r + 1, :], (GLA_CHUNK, bcum.shape[1]))
             for c in range(nch)], axis=0)

    chunk_of_row = lax.broadcasted_iota(I32, bcum.shape, 0) // GLA_CHUNK

    def by_chunk(x):
        return jnp.concatenate([jnp.where(chunk_of_row == c, x, 0.0).astype(BF16) for c in range(nch)], axis=1)

    blast = chunk_row(last_row)
    u_all = _dot(vt, by_chunk(k * jnp.exp(blast - bcum)))
    before = [None] * nch
    for c in order:
        before[c] = st
        decay = jnp.exp(bcum[c * GLA_CHUNK + last_row:c * GLA_CHUNK + last_row + 1, :])
        st = st * decay + jnp.where(bd_mask, u_all[:, c * HEAD_PAIR_DK:(c + 1) * HEAD_PAIR_DK], 0.0)
    if not want_out:
        return None, st
    bmid = chunk_row(mid_row)
    qt = q * jnp.exp(bcum - bmid)
    kt = (k * jnp.exp(bmid - bcum)).astype(BF16)
    lane = lax.broadcasted_iota(I32, qt.shape, 1)
    half = HEAD_PAIR_DK // 2
    o_heads = []
    for hh in range(2):
        sel = (lane < half) if hh == 0 else (lane >= half)
        a = _dot_nt(jnp.where(sel, qt, 0.0).astype(BF16), kt)
        a = jnp.where(amask, a, 0.0).astype(BF16)
        o_heads.append(_dot(a, v[:, hh * LANES:(hh + 1) * LANES]))
    qh = (q * jnp.exp(bcum)).astype(BF16)
    o_inter = jnp.concatenate(
        [_dot_nt(qh[c * GLA_CHUNK:(c + 1) * GLA_CHUNK], before[c].astype(BF16)) for c in range(nch)], axis=0)
    return jnp.concatenate(o_heads, axis=1) + o_inter, st


def _gla_kernel(q_ref, k_ref, v_ref, vt_ref, gkf_ref, gkb_ref, g_ref, kc_ref, vtc_ref, gkfc_ref, gkbc_ref,
                nw_ref, cmf_ref, cmb_ref, o_ref, stf_ref, stb_ref, of_ref, ob_ref):
    t = q_ref.shape[1]
    tc = kc_ref.shape[1]
    nsc, nscc = t // SUPER, tc // SUPER
    cmf = cmf_ref[...]
    cmb = cmb_ref[...]
    amask_f = cmf > 0
    amask_b = cmb > 0
    row = lax.broadcasted_iota(I32, (2 * LANES, HEAD_PAIR_DK), 0)
    lane = lax.broadcasted_iota(I32, (2 * LANES, HEAD_PAIR_DK), 1)
    bd_mask = (row < LANES) == (lane < HEAD_PAIR_DK // 2)

    def ctx_state(gk_ref, cm, fwd, j, st):
        rows = pl.ds(j * SUPER, SUPER)
        return _gla_super(None, kc_ref[0, rows, :].astype(F32), None, vtc_ref[0, j], gk_ref[0, rows, :],
                          cm, None, bd_mask, st, fwd, False)[1]

    def latent(gk_ref, cm, amask, fwd, j, st):
        rows = pl.ds(pl.multiple_of(j * SUPER, SUPER), SUPER)
        return _gla_super(q_ref[0, rows, :].astype(F32), k_ref[0, rows, :].astype(F32), v_ref[0, rows, :],
                          vt_ref[0, j], gk_ref[0, rows, :], cm, amask, bd_mask, st, fwd, True)

    st = jnp.zeros(stf_ref.shape, F32)
    for j in range(nscc):
        st = ctx_state(gkfc_ref, cmf, True, j, st)
    stf_ref[...] = st
    st = jnp.zeros(stb_ref.shape, F32)
    for j in reversed(range(nscc)):
        st = ctx_state(gkbc_ref, cmb, False, j, st)
    stb_ref[...] = st

    def scan_body(jj, carry):
        jb = nsc - 1 - jj
        of, stf = latent(gkf_ref, cmf, amask_f, True, jj, stf_ref[...])
        of_ref[pl.ds(pl.multiple_of(jj * SUPER, SUPER), SUPER), :] = of
        stf_ref[...] = stf
        ob, stb = latent(gkb_ref, cmb, amask_b, False, jb, stb_ref[...])
        ob_ref[pl.ds(pl.multiple_of(jb * SUPER, SUPER), SUPER), :] = ob
        stb_ref[...] = stb
        return carry

    lax.fori_loop(0, nsc, scan_body, 0, unroll=2)

    nw = nw_ref[...]

    def out_body(j, carry):
        rows = pl.ds(pl.multiple_of(j * SUPER, SUPER), SUPER)
        o = of_ref[rows, :] + ob_ref[rows, :]
        g = g_ref[0, rows, :].astype(F32)
        gate = g * jax.nn.sigmoid(g)
        for hh in range(2):
            oh = o[:, hh * LANES:(hh + 1) * LANES]
            on = oh * lax.rsqrt(jnp.mean(oh * oh, axis=-1, keepdims=True) + EPS) * nw
            o_ref[0, rows, hh * LANES:(hh + 1) * LANES] = (on * gate[:, hh * LANES:(hh + 1) * LANES]).astype(BF16)
        return carry

    lax.fori_loop(0, nsc, out_body, 0)


def _gla_masks():
    i = np.arange(SUPER)
    same = (i[:, None] // GLA_CHUNK) == (i[None, :] // GLA_CHUNK)
    fwd = same & (i[None, :] <= i[:, None])
    bwd = same & (i[None, :] >= i[:, None])
    return jnp.asarray(fwd, BF16), jnp.asarray(bwd, BF16)


def _gla_call(q, k, v, vt, gk, g, kc, vtc, gkc, nw):
    b, t, qk = q.shape
    tc = kc.shape[1]
    npair = qk // HEAD_PAIR_DK
    cmf, cmb = _gla_masks()
    lat = lambda width, off: pl.BlockSpec((1, t, width), lambda i, j: (i, 0, j + off))
    ctx = lambda width, off: pl.BlockSpec((1, tc, width), lambda i, j: (i, 0, j + off))
    tr = lambda n_groups: pl.BlockSpec((1, n_groups, 2 * LANES, SUPER), lambda i, j: (i, 0, j, 0))
    const = lambda shape: pl.BlockSpec(shape, lambda i, j: (0,) * len(shape))
    return pl.pallas_call(
        _gla_kernel,
        grid=(b, npair),
        in_specs=[lat(HEAD_PAIR_DK, 0), lat(HEAD_PAIR_DK, 0), lat(2 * LANES, 0), tr(t // SUPER),
                  lat(HEAD_PAIR_DK, 0), lat(HEAD_PAIR_DK, npair), lat(2 * LANES, 0),
                  ctx(HEAD_PAIR_DK, 0), tr(tc // SUPER), ctx(HEAD_PAIR_DK, 0), ctx(HEAD_PAIR_DK, npair),
                  const((1, LANES)), const(cmf.shape), const(cmb.shape)],
        out_specs=lat(2 * LANES, 0),
        out_shape=jax.ShapeDtypeStruct((b, t, v.shape[2]), BF16),
        scratch_shapes=[pltpu.VMEM((2 * LANES, HEAD_PAIR_DK), F32), pltpu.VMEM((2 * LANES, HEAD_PAIR_DK), F32),
                        pltpu.VMEM((t, 2 * LANES), F32), pltpu.VMEM((t, 2 * LANES), F32)],
        compiler_params=pltpu.CompilerParams(dimension_semantics=("arbitrary", "arbitrary")),
        name="gla",
    )(q, k, v, vt, gk, gk, g, kc, vtc, gkc, gkc, nw, cmf, cmb)


def _pool_kernel(x_ref, cm_ref, cnt_ref, wp_ref, ps_ref, o_ref, a_ref, b_ref):
    t = x_ref.shape[1]
    pad = POOL_PAD_GRID_ROWS * GRID_W
    total = t + 2 * pad
    for gi, w in enumerate(POOL_WINDOWS):
        lo = w // 2
        cols = slice(gi * LANES, (gi + 1) * LANES)
        cmat = cm_ref[gi]
        a_ref[0:pad, :] = jnp.zeros((pad, LANES), F32)
        a_ref[pad + t:total, :] = jnp.zeros((pad, LANES), F32)
        for blk in range(t // SUPER):
            rs = slice(blk * SUPER, (blk + 1) * SUPER)
            a_ref[pad + blk * SUPER:pad + (blk + 1) * SUPER, :] = _dot(cmat, x_ref[0, rs, cols])
        src, dst = a_ref, b_ref
        m = 1
        while m < w:
            sh = m * GRID_W
            dst[0:total - sh, :] = src[0:total - sh, :] + src[sh:total, :]
            src, dst = dst, src
            m *= 2
        first = pad - lo * GRID_W
        pooled = src[first:first + t, :] / cnt_ref[gi] - x_ref[0, :, cols].astype(F32)
        yp = _dot(pooled.astype(BF16), wp_ref[gi]) * ps_ref[:, cols]
        o_ref[0, :, cols] = yp.astype(BF16)


def _pool_col_mats():
    i = np.arange(SUPER)
    same_row = (i[:, None] // GRID_W) == (i[None, :] // GRID_W)
    d = i[None, :] - i[:, None]
    mats = []
    for w in POOL_WINDOWS:
        lo = w // 2
        hi = w - 1 - lo
        mats.append(same_row & (d >= -lo) & (d <= hi))
    return jnp.asarray(np.stack(mats), BF16)


def _pool_counts(t):
    rows = t // GRID_W
    r = np.arange(t) // GRID_W
    c = np.arange(t) % GRID_W
    out = []
    for w in POOL_WINDOWS:
        lo = w // 2
        hi = w - 1 - lo
        cnt_r = np.minimum(r + hi + 1, rows) - np.maximum(r - lo, 0)
        cnt_c = np.minimum(c + hi + 1, GRID_W) - np.maximum(c - lo, 0)
        out.append(np.broadcast_to((cnt_r * cnt_c).astype(np.float32)[:, None], (t, LANES)))
    return jnp.asarray(np.stack(out))


def _pool_call(xp, w_pool, pool_scale):
    b, t, pw = xp.shape
    ng = len(POOL_WINDOWS)
    assert max(POOL_WINDOWS) // 2 <= POOL_PAD_GRID_ROWS and t % GRID_W == 0
    cm = _pool_col_mats()
    cnt = _pool_counts(t)
    staged = t + 2 * POOL_PAD_GRID_ROWS * GRID_W
    const = lambda shape: pl.BlockSpec(shape, lambda i: (0,) * len(shape))
    return pl.pallas_call(
        _pool_kernel,
        grid=(b,),
        in_specs=[pl.BlockSpec((1, t, pw), lambda i: (i, 0, 0)),
                  const(cm.shape), const(cnt.shape), const((ng, LANES, LANES)), const((1, pw))],
        out_specs=pl.BlockSpec((1, t, pw), lambda i: (i, 0, 0)),
        out_shape=jax.ShapeDtypeStruct((b, t, pw), BF16),
        scratch_shapes=[pltpu.VMEM((staged, LANES), F32), pltpu.VMEM((staged, LANES), F32)],
        compiler_params=pltpu.CompilerParams(dimension_semantics=("arbitrary",)),
        name="pool",
    )(xp, cm, cnt, w_pool, pool_scale)


def _route_kernel(gla_ref, pool_ref, x_ref, mod_ref, wo_ref, nw_ref, wr_ref, br_ref, lt_ref,
                  h_ref, xt_ref, code_ref, wt_ref, cnt_ref, run_ref, wr2_ref, *, gw, n_exp):
    i = pl.program_id(0)

    @pl.when(i == 0)
    def _():
        run_ref[...] = jnp.zeros_like(run_ref)
        wh, wl = _split_bf16(wr_ref[...])
        wr2_ref[:, :LANES] = wh
        wr2_ref[:, LANES:] = wl

    m = mod_ref[0]
    acc = _dot(gla_ref[...], wo_ref[0:gw, :]) + _dot(pool_ref[...], wo_ref[gw:, :])
    h = x_ref[...] + m[2:3] * acc
    h_ref[...] = h
    xt = _rmsnorm(h, nw_ref[...]) * (1.0 + m[4:5]) + m[3:4]
    xt_ref[...] = _pack_bf16_pairs(xt)
    xh, xl = _split_bf16(xt)
    wr2 = wr2_ref[...]
    t1 = _dot(xh, wr2)
    logits = t1[:, :LANES] + t1[:, LANES:] + _dot(xl, wr2[:, :LANES]) + br_ref[...]
    lane = lax.broadcasted_iota(I32, logits.shape, 1)
    neg = jnp.float32(-jnp.inf)
    logits = jnp.where(lane < n_exp, logits, neg)
    vals, hots = [], []
    e_out = jnp.zeros(logits.shape, I32)
    for j in range(TOP_K):
        mx = jnp.max(logits, axis=-1, keepdims=True)
        idx = jnp.min(jnp.where(logits == mx, lane, LANES), axis=-1, keepdims=True)
        hot = lane == idx
        vals.append(mx)
        hots.append(hot)
        e_out = jnp.where(lane == j, idx, e_out)
        logits = jnp.where(hot, neg, logits)
    ex = [jnp.exp(v - vals[0]) for v in vals]
    den = ex[0] + ex[1] + ex[2] + ex[3]
    w_out = jnp.zeros(logits.shape, F32)
    for j in range(TOP_K):
        w_out = jnp.where(lane == j, ex[j] / den, w_out)
    osum = jnp.where(hots[0] | hots[1] | hots[2] | hots[3], 1.0, 0.0)
    before = _dot(lt_ref[...], osum.astype(BF16)) + run_ref[0:1, :]
    rk_out = jnp.zeros(logits.shape, I32)
    for j in range(TOP_K):
        rj = jnp.sum(jnp.where(hots[j], before, 0.0), axis=-1, keepdims=True)
        rk_out = jnp.where(lane == j, rj.astype(I32), rk_out)
    run = run_ref[0:1, :] + jnp.sum(osum, axis=0, keepdims=True)
    run_ref[...] = jnp.broadcast_to(run, run_ref.shape)
    code = e_out * RANK_LIMIT + rk_out
    code_ref[...] = code.T[:code_ref.shape[0], :]
    wt_ref[...] = w_out
    cnt_ref[...] = jnp.broadcast_to(run, cnt_ref.shape).astype(I32)


def _route_call(gla, pool, x, mod, w_out, nw, wr, br, *, tm, n_exp, first_tile, n_tiles):
    n_all, d = x.shape
    n = n_tiles * tm
    gw = gla.shape[1]
    t_per_b = n_all // mod.shape[0]
    lt = jnp.asarray(np.tril(np.ones((tm, tm), np.float32), -1), BF16)
    row_in = lambda width: pl.BlockSpec((tm, width), lambda i: (i + first_tile, 0))
    row = lambda width: pl.BlockSpec((tm, width), lambda i: (i, 0))
    const = lambda shape: pl.BlockSpec(shape, lambda i: (0,) * len(shape))
    sds = jax.ShapeDtypeStruct
    return pl.pallas_call(
        functools.partial(_route_kernel, gw=gw, n_exp=n_exp),
        grid=(n_tiles,),
        in_specs=[row_in(gw), row_in(pool.shape[1]), row_in(d),
                  pl.BlockSpec((1, N_MOD, d), lambda i: ((i + first_tile) * tm // t_per_b, 0, 0)),
                  const(w_out.shape), const((1, d)), const(wr.shape), const((1, LANES)), const((tm, tm))],
        out_specs=[row(d), row(d // 2), pl.BlockSpec((8, tm), lambda i: (0, i)), row(LANES), const((8, LANES))],
        out_shape=[sds((n, d), F32), sds((n, d // 2), U32), sds((8, n), I32),
                   sds((n, LANES), F32), sds((8, LANES), I32)],
        scratch_shapes=[pltpu.VMEM((8, LANES), F32), pltpu.VMEM((d, 2 * LANES), BF16)],
        compiler_params=pltpu.CompilerParams(dimension_semantics=("arbitrary",)),
        name="route",
    )(gla, pool, x, mod, w_out, nw, wr, br, lt)


def _plan_kernel(cnt_ref, code_ref, dest_ref, be_ref, nv_ref, nu_ref, start_ref, *, n_exp, rows):
    @pl.when(pl.program_id(0) == 0)
    def _():
        blk = (lax.broadcasted_iota(I32, be_ref.shape, 0) * LANES + lax.broadcasted_iota(I32, be_ref.shape, 1))
        blk_row0 = blk * rows
        be = jnp.zeros(be_ref.shape, I32)
        end_valid = jnp.zeros(be_ref.shape, I32)
        acc = jnp.int32(0)
        last_e = jnp.int32(0)
        for e in range(n_exp):
            c = cnt_ref[e]
            start_ref[e] = acc
            in_e = blk_row0 >= acc
            end_valid = jnp.where(in_e, acc + c, end_valid)
            be = jnp.where(in_e, e, be)
            acc = acc + (c + rows - 1) // rows * rows
            last_e = jnp.where(c > 0, e, last_e)
        n_used = acc // rows
        nu_ref[0] = n_used
        be_ref[...] = jnp.where(blk < n_used, be, last_e)
        nv_ref[...] = jnp.clip(end_valid - blk_row0, 0, rows)

    code = code_ref[...]
    e_vec = code // RANK_LIMIT
    dest = code % RANK_LIMIT
    for e in range(n_exp):
        dest = dest + jnp.where(e_vec == e, start_ref[e], 0)
    dest_ref[...] = dest


def _plan_call(counts, code_t, *, n_exp, rows, chunk):
    n = code_t.shape[1]
    sds = jax.ShapeDtypeStruct
    smem = pltpu.SMEM
    return pl.pallas_call(
        functools.partial(_plan_kernel, n_exp=n_exp, rows=rows),
        grid=(n // chunk,),
        in_specs=[pl.BlockSpec(memory_space=smem), pl.BlockSpec((8, chunk), lambda i: (0, i))],
        out_specs=[pl.BlockSpec((8, chunk), lambda i: (0, i)),
                   pl.BlockSpec((8, LANES), lambda i: (0, 0)), pl.BlockSpec((8, LANES), lambda i: (0, 0)),
                   pl.BlockSpec(memory_space=smem)],
        out_shape=[sds((8, n), I32), sds((8, LANES), I32), sds((8, LANES), I32), sds((1,), I32)],
        scratch_shapes=[pltpu.SMEM((n_exp,), I32)],
        compiler_params=pltpu.CompilerParams(dimension_semantics=("arbitrary",)),
        name="plan",
    )(counts, code_t)


def _sc_worker_id():
    return lax.axis_index("s") * SC_CORES + lax.axis_index("c")


def _sc_scatter_call(x, idx3, *, n_out):
    n, d = x.shape
    n_win_total, k, w = idx3.shape
    n_win = n_win_total // SC_WORKERS
    mesh = plsc.VectorSubcoreMesh(core_axis_name="c", subcore_axis_name="s")

    @functools.partial(
        pl.kernel, mesh=mesh,
        out_type=jax.ShapeDtypeStruct((n_out, d), x.dtype),
        scratch_types=[pltpu.VMEM((k, w), I32), pltpu.VMEM((w, d), x.dtype), pltpu.SemaphoreType.DMA],
        name="sc_dispatch",
    )
    def kern(x_hbm, idx_hbm, out_hbm, idx_v, rows_v, sem):
        wid = _sc_worker_id()

        @pl.loop(0, n_win)
        def _(i):
            win = wid * n_win + i
            pltpu.sync_copy(idx_hbm.at[win], idx_v)
            pltpu.sync_copy(x_hbm.at[pl.ds(win * w, w)], rows_v)
            for j in range(k):
                pltpu.async_copy(rows_v, out_hbm.at[idx_v.at[j]], sem).wait()

    return kern(x, idx3)


def _sc_gather_call(table, idx3):
    n_workers, n_win, w = idx3.shape
    d = table.shape[1]
    assert n_workers == SC_WORKERS and n_win % 2 == 0
    mesh = plsc.VectorSubcoreMesh(core_axis_name="c", subcore_axis_name="s")

    @functools.partial(
        pl.kernel, mesh=mesh,
        out_type=jax.ShapeDtypeStruct((n_workers * n_win * w, d), table.dtype),
        scratch_types=[pltpu.VMEM((n_win, w), I32), pltpu.VMEM((2, w, d), table.dtype),
                       pltpu.SemaphoreType.DMA((2,)), pltpu.SemaphoreType.DMA((2,))],
        name="sc_gather",
    )
    def kern(table_hbm, idx_hbm, out_hbm, idx_v, rows_v, gsem, osem):
        wid = _sc_worker_id()
        base = wid * n_win
        pltpu.sync_copy(idx_hbm.at[wid], idx_v)

        def gather(wi, b):
            return pltpu.make_async_copy(table_hbm.at[idx_v.at[wi]], rows_v.at[b], gsem.at[b])

        def put(wi, b):
            return pltpu.make_async_copy(rows_v.at[b], out_hbm.at[pl.ds((base + wi) * w, w)], osem.at[b])

        gather(0, 0).start()

        @pl.loop(0, n_win, step=2)
        def _(i):
            for b in range(2):
                wi = i + b

                @pl.when(wi + 1 < n_win)
                def _():
                    @pl.when(wi >= 1)
                    def _():
                        put(wi - 1, 1 - b).wait()
                    gather(wi + 1, 1 - b).start()

                gather(wi, b).wait()
                put(wi, b).start()

        put(n_win - 2, 0).wait()
        put(n_win - 1, 1).wait()

    return kern(table, idx3)


def _expert_kernel(be_ref, nu_ref, nv_ref, x_ref, wgu_ref, bgu_ref, wd_ref, bd_ref, y_ref, wgu_bf_ref, wd_bf_ref,
                   *, d_ff):
    i = pl.program_id(0)

    @pl.when((i == 0) | (be_ref[i] != be_ref[jnp.maximum(i - 1, 0)]))
    def _():
        wgu_bf_ref[...] = wgu_ref[0].astype(BF16)
        wd_bf_ref[...] = wd_ref[0].astype(BF16)

    @pl.when(i < nu_ref[0])
    def _():
        row = lax.broadcasted_iota(I32, x_ref.shape, 0)
        lo, hi = _unpack_bf16_pairs(jnp.where(row < nv_ref[i], x_ref[...], jnp.uint32(0)))
        xb = jnp.concatenate([lo, hi], axis=1).astype(BF16)
        gu = _dot(xb, wgu_bf_ref[...]) + bgu_ref[0]
        gate = jnp.minimum(gu[:, :d_ff], SWIGLU_LIMIT)
        up = jnp.clip(gu[:, d_ff:], -SWIGLU_LIMIT, SWIGLU_LIMIT)
        act = (up + 1.0) * gate * jax.nn.sigmoid(SWIGLU_ALPHA * gate)
        y_ref[...] = _pack_bf16_pairs(_dot(act.astype(BF16), wd_bf_ref[...]) + bd_ref[0])


def _expert_call(block_e, n_used, n_valid, xs, w_gu, b_gu, w_down, b_down):
    n_pad = xs.shape[0]
    n_exp, d, two_ff = w_gu.shape
    d_ff = two_ff // 2
    nblk = n_pad // EXPERT_ROWS
    rows = lambda i, be, nu, nv: (jnp.minimum(i, nu[0] - 1), 0)
    per_e = lambda i, be, nu, nv: (be[i], 0, 0)
    grid_spec = pltpu.PrefetchScalarGridSpec(
        num_scalar_prefetch=3,
        grid=(nblk,),
        in_specs=[pl.BlockSpec((EXPERT_ROWS, d // 2), rows),
                  pl.BlockSpec((1, d, two_ff), per_e), pl.BlockSpec((1, 1, two_ff), per_e),
                  pl.BlockSpec((1, d_ff, d), per_e), pl.BlockSpec((1, 1, d), per_e)],
        out_specs=pl.BlockSpec((EXPERT_ROWS, d // 2), rows),
        scratch_shapes=[pltpu.VMEM((d, two_ff), BF16), pltpu.VMEM((d_ff, d), BF16)],
    )
    return pl.pallas_call(
        functools.partial(_expert_kernel, d_ff=d_ff),
        grid_spec=grid_spec,
        out_shape=jax.ShapeDtypeStruct((n_pad, d // 2), U32),
        compiler_params=pltpu.CompilerParams(dimension_semantics=("arbitrary",),
                                             vmem_limit_bytes=EXPERT_VMEM_BYTES),
        name="experts",
    )(block_e, n_used, n_valid, xs, w_gu, b_gu.reshape(n_exp, 1, two_ff), w_down, b_down.reshape(n_exp, 1, d))


def _combine_kernel(y4_ref, wt_ref, h_ref, mod_ref, fw_ref, o_ref):
    wt = wt_ref[...]
    acc_lo, acc_hi = None, None
    for j in range(TOP_K):
        lo, hi = _unpack_bf16_pairs(y4_ref[j])
        w = wt[:, j:j + 1]
        acc_lo = w * lo if j == 0 else acc_lo + w * lo
        acc_hi = w * hi if j == 0 else acc_hi + w * hi
    acc = jnp.concatenate([acc_lo, acc_hi], axis=1)
    m = mod_ref[0]
    o_ref[...] = _rmsnorm(h_ref[...] + m[5:6] * acc, fw_ref[...])


def _combine_call(y4, wts, h, mod, fw, prev_out, *, n, tg, first_tile):
    d = h.shape[1]
    t_per_b = n // mod.shape[0]
    part = lambda width: pl.BlockSpec((tg, width), lambda i: (i, 0))
    row = lambda width: pl.BlockSpec((tg, width), lambda i: (i + first_tile, 0))
    in_specs = [pl.BlockSpec((TOP_K, tg, d // 2), lambda i: (0, i, 0)),
                part(LANES), part(d),
                pl.BlockSpec((1, N_MOD, d), lambda i: ((i + first_tile) * tg // t_per_b, 0, 0)),
                pl.BlockSpec((1, d), lambda i: (0, 0))]
    args = [y4, wts, h, mod, fw]
    kern = _combine_kernel
    aliases = {}
    if prev_out is not None:
        in_specs.append(pl.BlockSpec(memory_space=pl.ANY))
        args.append(prev_out)
        kern = lambda y4_ref, wt_ref, h_ref, mod_ref, fw_ref, prev_ref, o_ref: _combine_kernel(
            y4_ref, wt_ref, h_ref, mod_ref, fw_ref, o_ref)
        aliases = {len(args) - 1: 0}
    return pl.pallas_call(
        kern,
        grid=(y4.shape[1] // tg,),
        in_specs=in_specs,
        out_specs=row(d),
        out_shape=jax.ShapeDtypeStruct((n, d), F32),
        input_output_aliases=aliases,
        compiler_params=pltpu.CompilerParams(dimension_semantics=("arbitrary",)),
        name="combine",
    )(*args)


def kernel(x, c, ctx, c_ctx, w_ada, b_ada, norm_mix_w, norm_mlp_w, w_in, w_gk_f, b_gk_f, w_gk_b, b_gk_b,
           gla_norm_w, w_pool, pool_scale, w_out, w_router, b_router, w_gu, b_gu, w_down, b_down,
           final_norm_w):
    b, t, d = x.shape
    assert w_ada.shape[0] == 1, "single-layer trunk"
    n_exp = w_router.shape[2]
    rank = w_gk_f.shape[1]
    qk = w_gk_f.shape[2]
    dk = qk // GLA_HEADS
    gw = GLA_HEADS * gla_norm_w.shape[1]
    pw = w_pool.shape[1] * w_pool.shape[2]
    assert w_in.shape[2] == 2 * qk + 2 * gw + 2 * rank + pw and 2 * rank <= LANES
    assert t % SUPER == 0 and ctx.shape[1] % SUPER == 0 and n_exp <= LANES

    rows = -(-(b + 1) // 8) * 8
    cc = jnp.zeros((rows, d), F32).at[:b].set(c).at[b].set(c_ctx)
    mod = _mod_call(cc, w_ada[0], b_ada)
    mod_x = mod[:b].reshape(b, N_MOD, d)
    mod_c = mod[b:b + 1].reshape(1, N_MOD, d)

    wi = w_in[0]
    o_r = 2 * qk + 2 * gw
    w_cat = jnp.concatenate([wi[:, :o_r], wi[:, o_r + 2 * rank:], wi[:, o_r:o_r + 2 * rank],
                             jnp.zeros((d, LANES - 2 * rank), F32)], axis=1).astype(BF16)
    wgk = jnp.zeros((LANES, 2 * qk), F32).at[:rank, :qk].set(w_gk_f[0]).at[rank:2 * rank, qk:].set(w_gk_b[0])
    bgk = jnp.concatenate([b_gk_f[0], b_gk_b[0]])[None, :]
    proj = functools.partial(_inproj_call, nw=norm_mix_w, w=w_cat, wgk=wgk.astype(BF16), bgk=bgk,
                             qk=qk, gw=gw, pw=pw, dk=dk)
    q, k, v, vt, g, xp, gk = proj(x, mod_x, tm=512)
    _, kc, _, vtc, _, _, gkc = proj(ctx, mod_c, tm=SUPER)

    gla = _gla_call(q, k, v, vt, gk, g, kc, vtc, gkc, gla_norm_w)
    pool = _pool_call(xp, w_pool[0].astype(BF16), pool_scale)

    n = b * t
    wr = jnp.zeros((d, LANES), F32).at[:, :n_exp].set(w_router[0])
    br = jnp.zeros((1, LANES), F32).at[0, :n_exp].set(b_router[0])
    n_part = n // MOE_PARTS
    tm, tg = 512, 256
    assert n % MOE_PARTS == 0 and n_part % (SC_WORKERS * SC_WINDOW) == 0 and n_part % tm == 0 and n_part < RANK_LIMIT
    assert (n_part * TOP_K) % (SC_WORKERS * 2 * SC_GATHER_WINDOW) == 0
    n_pad = n_part * TOP_K + n_exp * EXPERT_ROWS
    nblk = n_pad // EXPERT_ROWS
    assert nblk <= 8 * LANES
    parts = range(MOE_PARTS)
    routed = [_route_call(gla.reshape(n, gw), pool.reshape(n, pw), x.reshape(n, d), mod_x, w_out[0].astype(BF16),
                          norm_mlp_w, wr, br, tm=tm, n_exp=n_exp,
                          first_tile=p * n_part // tm, n_tiles=n_part // tm) for p in parts]
    plans = [_plan_call(cnt[0, :n_exp], code_t, n_exp=n_exp, rows=EXPERT_ROWS, chunk=min(n_part, 4096))
             for (_, _, code_t, _, cnt) in routed]
    dests = [dest_t[:TOP_K] for (dest_t, _, _, _) in plans]
    xs = [_sc_scatter_call(routed[p][1],
                           dests[p].reshape(TOP_K, n_part // SC_WINDOW, SC_WINDOW).transpose(1, 0, 2), n_out=n_pad)
          for p in parts]
    ys = [_expert_call(plans[p][1].reshape(-1)[:nblk], plans[p][3], plans[p][2].reshape(-1)[:nblk], xs[p],
                       w_gu[0], b_gu[0], w_down[0], b_down[0]) for p in parts]
    y4 = [_sc_gather_call(ys[p], dests[p].reshape(SC_WORKERS, -1, SC_GATHER_WINDOW)).reshape(TOP_K, n_part, d // 2)
          for p in parts]
    out = None
    for p in parts:
        out = _combine_call(y4[p], routed[p][3], routed[p][0], mod_x, final_norm_w[None, :], out,
                            n=n, tg=tg, first_tile=p * n_part // tg)
    return out.reshape(b, t, d)
```

```python
import functools

import numpy as np
import jax
import jax.numpy as jnp
from jax import lax
from jax.experimental import pallas as pl
from jax.experimental.pallas import tpu as pltpu
from jax.experimental.pallas import tpu_sc as plsc

F32 = jnp.float32
BF16 = jnp.bfloat16
I32 = jnp.int32
U32 = jnp.uint32

GRID_W = 64
GLA_HEADS = 4
GLA_CHUNK = 64
GATE_NORMALIZER = 16.0
POOL_WINDOWS = (2, 4, 8, 16)
POOL_PAD_GRID_ROWS = 8
TOP_K = 4
RANK_LIMIT = 1 << 20
SWIGLU_LIMIT = 7.0
SWIGLU_ALPHA = 1.702
N_MOD = 6
EPS = 1e-6

LANES = 128
SUPER = 256
HEAD_PAIR_DK = 128
EXPERT_ROWS = 512
MOE_PARTS = 2
EXPERT_VMEM_BYTES = 56 * 1024 * 1024
SC_CORES = 2
SC_SUBCORES = 16
SC_WORKERS = SC_CORES * SC_SUBCORES
SC_WINDOW = 32
SC_GATHER_WINDOW = 64


def _dot(a, b):
    return jnp.dot(a, b, preferred_element_type=F32)


def _dot_nt(a, b):
    return lax.dot_general(a, b, (((1,), (1,)), ((), ())), preferred_element_type=F32)


def _split_bf16(x):
    hi = x.astype(BF16)
    lo = (x - hi.astype(F32)).astype(BF16)
    return hi, lo


def _pack_bf16_pairs(x):
    c = x.shape[1] // 2
    lo = lax.bitcast_convert_type(x[:, :c].astype(BF16).astype(F32), U32)
    hi = lax.bitcast_convert_type(x[:, c:].astype(BF16).astype(F32), U32)
    return (lo >> 16) | hi


def _unpack_bf16_pairs(p):
    lo = lax.bitcast_convert_type(p << 16, F32)
    hi = lax.bitcast_convert_type(p & jnp.uint32(0xFFFF0000), F32)
    return lo, hi


def _rmsnorm(x, w):
    var = jnp.mean(x * x, axis=-1, keepdims=True)
    return x * lax.rsqrt(var + EPS) * w


def _mod_kernel(c_ref, w_ref, b_ref, o_ref):
    c = c_ref[...]
    s = c * jax.nn.sigmoid(c)
    o_ref[...] = jnp.dot(s, w_ref[...], precision=lax.Precision.HIGHEST,
                         preferred_element_type=F32) + b_ref[...]


def _mod_call(cc, w_ada, b_ada):
    rows, d = cc.shape
    n = w_ada.shape[1]
    tn = 1024
    return pl.pallas_call(
        _mod_kernel,
        grid=(n // tn,),
        in_specs=[pl.BlockSpec((rows, d), lambda j: (0, 0)),
                  pl.BlockSpec((d, tn), lambda j: (0, j)),
                  pl.BlockSpec((1, tn), lambda j: (0, j))],
        out_specs=pl.BlockSpec((rows, tn), lambda j: (0, j)),
        out_shape=jax.ShapeDtypeStruct((rows, n), F32),
        name="mod",
    )(cc, w_ada, b_ada)


def _inproj_kernel(x_ref, mod_ref, nw_ref, w_ref, wgk_ref, bgk_ref,
                   q_ref, k_ref, v_ref, vt_ref, g_ref, p_ref, gk_ref, *, qk, gw, pw, dk):
    x = x_ref[0]
    m = mod_ref[0]
    hm = (_rmsnorm(x, nw_ref[...]) * (1.0 + m[1:2]) + m[0:1]).astype(BF16)
    p = _dot(hm, w_ref[...])
    vt = p[:, 2 * qk:2 * qk + gw].T
    for s in range(vt_ref.shape[1]):
        vt_ref[0, s] = vt[:, s * SUPER:(s + 1) * SUPER].astype(BF16)
    o = 0
    q_ref[0] = (p[:, o:o + qk] * (dk ** -0.5)).astype(BF16); o += qk
    k_ref[0] = p[:, o:o + qk].astype(BF16); o += qk
    v_ref[0] = p[:, o:o + gw].astype(BF16); o += gw
    g_ref[0] = p[:, o:o + gw].astype(BF16); o += gw
    p_ref[0] = p[:, o:o + pw].astype(BF16); o += pw
    r = p[:, o:o + LANES]
    z = _dot(r.astype(BF16), wgk_ref[...]) + bgk_ref[...]
    gk_ref[0] = (jnp.minimum(z, 0.0) - jnp.log1p(jnp.exp(-jnp.abs(z)))) * (1.0 / GATE_NORMALIZER)


def _inproj_call(x, mod, nw, w, wgk, bgk, *, qk, gw, pw, dk, tm):
    b, t, d = x.shape
    n_in = w.shape[1]
    bs = lambda width: pl.BlockSpec((1, tm, width), lambda i, j: (i, j, 0))
    const = lambda shape: pl.BlockSpec(shape, lambda i, j: (0,) * len(shape))
    per_batch = mod.shape[0] > 1
    sds = jax.ShapeDtypeStruct
    return pl.pallas_call(
        functools.partial(_inproj_kernel, qk=qk, gw=gw, pw=pw, dk=dk),
        grid=(b, t // tm),
        in_specs=[bs(d),
                  pl.BlockSpec((1, N_MOD, d), (lambda i, j: (i, 0, 0)) if per_batch else (lambda i, j: (0, 0, 0))),
                  const((1, d)), const((d, n_in)), const((LANES, 2 * qk)), const((1, 2 * qk))],
        out_specs=[bs(qk), bs(qk), bs(gw),
                   pl.BlockSpec((1, tm // SUPER, gw, SUPER), lambda i, j: (i, j, 0, 0)),
                   bs(gw), bs(pw), bs(2 * qk)],
        out_shape=[sds((b, t, qk), BF16), sds((b, t, qk), BF16), sds((b, t, gw), BF16),
                   sds((b, t // SUPER, gw, SUPER), BF16),
                   sds((b, t, gw), BF16), sds((b, t, pw), BF16), sds((b, t, 2 * qk), F32)],
        compiler_params=pltpu.CompilerParams(dimension_semantics=("arbitrary", "arbitrary")),
        name="inproj",
    )(x, mod, nw, w, wgk, bgk)


def _gla_super(q, k, v, vt, gk, cm, amask, bd_mask, st, fwd, want_out):
    nch = SUPER // GLA_CHUNK
    order = tuple(range(nch)) if fwd else tuple(reversed(range(nch)))
    last_row = GLA_CHUNK - 1 if fwd else 0
    mid_row = GLA_CHUNK // 2 - 1 if fwd else GLA_CHUNK // 2
    hi, lo = _split_bf16(gk)
    bcum = _dot(cm, hi) + _dot(cm, lo)

    def chunk_row(r):
        return jnp.concatenate(
            [jnp.broadcast_to(bcum[c * GLA_CHUNK + r:c * GLA_CHUNK + r + 1, :], (GLA_CHUNK, bcum.shape[1]))
             for c in range(nch)], axis=0)

    chunk_of_row = lax.broadcasted_iota(I32, bcum.shape, 0) // GLA_CHUNK

    def by_chunk(x):
        return jnp.concatenate([jnp.where(chunk_of_row == c, x, 0.0).astype(BF16) for c in range(nch)], axis=1)

    blast = chunk_row(last_row)
    u_all = _dot(vt, by_chunk(k * jnp.exp(blast - bcum)))
    before = [None] * nch
    for c in order:
        before[c] = st
        decay = jnp.exp(bcum[c * GLA_CHUNK + last_row:c * GLA_CHUNK + last_row + 1, :])
        st = st * decay + jnp.where(bd_mask, u_all[:, c * HEAD_PAIR_DK:(c + 1) * HEAD_PAIR_DK], 0.0)
    if not want_out:
        return None, st
    bmid = chunk_row(mid_row)
    qt = q * jnp.exp(bcum - bmid)
    kt = (k * jnp.exp(bmid - bcum)).astype(BF16)
    lane = lax.broadcasted_iota(I32, qt.shape, 1)
    half = HEAD_PAIR_DK // 2
    o_heads = []
    for hh in range(2):
        sel = (lane < half) if hh == 0 else (lane >= half)
        a = _dot_nt(jnp.where(sel, qt, 0.0).astype(BF16), kt)
        a = jnp.where(amask, a, 0.0).astype(BF16)
        o_heads.append(_dot(a, v[:, hh * LANES:(hh + 1) * LANES]))
    qh = (q * jnp.exp(bcum)).astype(BF16)
    o_inter = jnp.concatenate(
        [_dot_nt(qh[c * GLA_CHUNK:(c + 1) * GLA_CHUNK], before[c].astype(BF16)) for c in range(nch)], axis=0)
    return jnp.concatenate(o_heads, axis=1) + o_inter, st


def _gla_kernel(q_ref, k_ref, v_ref, vt_ref, gkf_ref, gkb_ref, g_ref, kc_ref, vtc_ref, gkfc_ref, gkbc_ref,
                nw_ref, cmf_ref, cmb_ref, o_ref, stf_ref, stb_ref, of_ref, ob_ref):
    t = q_ref.shape[1]
    tc = kc_ref.shape[1]
    nsc, nscc = t // SUPER, tc // SUPER
    cmf = cmf_ref[...]
    cmb = cmb_ref[...]
    amask_f = cmf > 0
    amask_b = cmb > 0
    row = lax.broadcasted_iota(I32, (2 * LANES, HEAD_PAIR_DK), 0)
    lane = lax.broadcasted_iota(I32, (2 * LANES, HEAD_PAIR_DK), 1)
    bd_mask = (row < LANES) == (lane < HEAD_PAIR_DK // 2)

    def ctx_state(gk_ref, cm, fwd, j, st):
        rows = pl.ds(j * SUPER, SUPER)
        return _gla_super(None, kc_ref[0, rows, :].astype(F32), None, vtc_ref[0, j], gk_ref[0, rows, :],
                          cm, None, bd_mask, st, fwd, False)[1]

    def latent(gk_ref, cm, amask, fwd, j, st):
        rows = pl.ds(pl.multiple_of(j * SUPER, SUPER), SUPER)
        return _gla_super(q_ref[0, rows, :].astype(F32), k_ref[0, rows, :].astype(F32), v_ref[0, rows, :],
                          vt_ref[0, j], gk_ref[0, rows, :], cm, amask, bd_mask, st, fwd, True)

    st = jnp.zeros(stf_ref.shape, F32)
    for j in range(nscc):
        st = ctx_state(gkfc_ref, cmf, True, j, st)
    stf_ref[...] = st
    st = jnp.zeros(stb_ref.shape, F32)
    for j in reversed(range(nscc)):
        st = ctx_state(gkbc_ref, cmb, False, j, st)
    stb_ref[...] = st

    def scan_body(jj, carry):
        jb = nsc - 1 - jj
        of, stf = latent(gkf_ref, cmf, amask_f, True, jj, stf_ref[...])
        of_ref[pl.ds(pl.multiple_of(jj * SUPER, SUPER), SUPER), :] = of
        stf_ref[...] = stf
        ob, stb = latent(gkb_ref, cmb, amask_b, False, jb, stb_ref[...])
        ob_ref[pl.ds(pl.multiple_of(jb * SUPER, SUPER), SUPER), :] = ob
        stb_ref[...] = stb
        return carry

    lax.fori_loop(0, nsc, scan_body, 0, unroll=2)

    nw = nw_ref[...]

    def out_body(j, carry):
        rows = pl.ds(pl.multiple_of(j * SUPER, SUPER), SUPER)
        o = of_ref[rows, :] + ob_ref[rows, :]
        g = g_ref[0, rows, :].astype(F32)
        gate = g * jax.nn.sigmoid(g)
        for hh in range(2):
            oh = o[:, hh * LANES:(hh + 1) * LANES]
            on = oh * lax.rsqrt(jnp.mean(oh * oh, axis=-1, keepdims=True) + EPS) * nw
            o_ref[0, rows, hh * LANES:(hh + 1) * LANES] = (on * gate[:, hh * LANES:(hh + 1) * LANES]).astype(BF16)
        return carry

    lax.fori_loop(0, nsc, out_body, 0)


def _gla_masks():
    i = np.arange(SUPER)
    same = (i[:, None] // GLA_CHUNK) == (i[None, :] // GLA_CHUNK)
    fwd = same & (i[None, :] <= i[:, None])
    bwd = same & (i[None, :] >= i[:, None])
    return jnp.asarray(fwd, BF16), jnp.asarray(bwd, BF16)


def _gla_call(q, k, v, vt, gk, g, kc, vtc, gkc, nw):
    b, t, qk = q.shape
    tc = kc.shape[1]
    npair = qk // HEAD_PAIR_DK
    cmf, cmb = _gla_masks()
    lat = lambda width, off: pl.BlockSpec((1, t, width), lambda i, j: (i, 0, j + off))
    ctx = lambda width, off: pl.BlockSpec((1, tc, width), lambda i, j: (i, 0, j + off))
    tr = lambda n_groups: pl.BlockSpec((1, n_groups, 2 * LANES, SUPER), lambda i, j: (i, 0, j, 0))
    const = lambda shape: pl.BlockSpec(shape, lambda i, j: (0,) * len(shape))
    return pl.pallas_call(
        _gla_kernel,
        grid=(b, npair),
        in_specs=[lat(HEAD_PAIR_DK, 0), lat(HEAD_PAIR_DK, 0), lat(2 * LANES, 0), tr(t // SUPER),
                  lat(HEAD_PAIR_DK, 0), lat(HEAD_PAIR_DK, npair), lat(2 * LANES, 0),
                  ctx(HEAD_PAIR_DK, 0), tr(tc // SUPER), ctx(HEAD_PAIR_DK, 0), ctx(HEAD_PAIR_DK, npair),
                  const((1, LANES)), const(cmf.shape), const(cmb.shape)],
        out_specs=lat(2 * LANES, 0),
        out_shape=jax.ShapeDtypeStruct((b, t, v.shape[2]), BF16),
        scratch_shapes=[pltpu.VMEM((2 * LANES, HEAD_PAIR_DK), F32), pltpu.VMEM((2 * LANES, HEAD_PAIR_DK), F32),
                        pltpu.VMEM((t, 2 * LANES), F32), pltpu.VMEM((t, 2 * LANES), F32)],
        compiler_params=pltpu.CompilerParams(dimension_semantics=("arbitrary", "arbitrary")),
        name="gla",
    )(q, k, v, vt, gk, gk, g, kc, vtc, gkc, gkc, nw, cmf, cmb)


def _pool_kernel(x_ref, cm_ref, cnt_ref, wp_ref, ps_ref, o_ref, a_ref, b_ref):
    t = x_ref.shape[1]
    pad = POOL_PAD_GRID_ROWS * GRID_W
    total = t + 2 * pad
    for gi, w in enumerate(POOL_WINDOWS):
        lo = w // 2
        cols = slice(gi * LANES, (gi + 1) * LANES)
        cmat = cm_ref[gi]
        a_ref[0:pad, :] = jnp.zeros((pad, LANES), F32)
        a_ref[pad + t:total, :] = jnp.zeros((pad, LANES), F32)
        for blk in range(t // SUPER):
            rs = slice(blk * SUPER, (blk + 1) * SUPER)
            a_ref[pad + blk * SUPER:pad + (blk + 1) * SUPER, :] = _dot(cmat, x_ref[0, rs, cols])
        src, dst = a_ref, b_ref
        m = 1
        while m < w:
            sh = m * GRID_W
            dst[0:total - sh, :] = src[0:total - sh, :] + src[sh:total, :]
            src, dst = dst, src
            m *= 2
        first = pad - lo * GRID_W
        pooled = src[first:first + t, :] / cnt_ref[gi] - x_ref[0, :, cols].astype(F32)
        yp = _dot(pooled.astype(BF16), wp_ref[gi]) * ps_ref[:, cols]
        o_ref[0, :, cols] = yp.astype(BF16)


def _pool_col_mats():
    i = np.arange(SUPER)
    same_row = (i[:, None] // GRID_W) == (i[None, :] // GRID_W)
    d = i[None, :] - i[:, None]
    mats = []
    for w in POOL_WINDOWS:
        lo = w // 2
        hi = w - 1 - lo
        mats.append(same_row & (d >= -lo) & (d <= hi))
    return jnp.asarray(np.stack(mats), BF16)


def _pool_counts(t):
    rows = t // GRID_W
    r = np.arange(t) // GRID_W
    c = np.arange(t) % GRID_W
    out = []
    for w in POOL_WINDOWS:
        lo = w // 2
        hi = w - 1 - lo
        cnt_r = np.minimum(r + hi + 1, rows) - np.maximum(r - lo, 0)
        cnt_c = np.minimum(c + hi + 1, GRID_W) - np.maximum(c - lo, 0)
        out.append(np.broadcast_to((cnt_r * cnt_c).astype(np.float32)[:, None], (t, LANES)))
    return jnp.asarray(np.stack(out))


def _pool_call(xp, w_pool, pool_scale):
    b, t, pw = xp.shape
    ng = len(POOL_WINDOWS)
    assert max(POOL_WINDOWS) // 2 <= POOL_PAD_GRID_ROWS and t % GRID_W == 0
    cm = _pool_col_mats()
    cnt = _pool_counts(t)
    staged = t + 2 * POOL_PAD_GRID_ROWS * GRID_W
    const = lambda shape: pl.BlockSpec(shape, lambda i: (0,) * len(shape))
    return pl.pallas_call(
        _pool_kernel,
        grid=(b,),
        in_specs=[pl.BlockSpec((1, t, pw), lambda i: (i, 0, 0)),
                  const(cm.shape), const(cnt.shape), const((ng, LANES, LANES)), const((1, pw))],
        out_specs=pl.BlockSpec((1, t, pw), lambda i: (i, 0, 0)),
        out_shape=jax.ShapeDtypeStruct((b, t, pw), BF16),
        scratch_shapes=[pltpu.VMEM((staged, LANES), F32), pltpu.VMEM((staged, LANES), F32)],
        compiler_params=pltpu.CompilerParams(dimension_semantics=("arbitrary",)),
        name="pool",
    )(xp, cm, cnt, w_pool, pool_scale)


def _route_kernel(gla_ref, pool_ref, x_ref, mod_ref, wo_ref, nw_ref, wr_ref, br_ref, lt_ref,
                  h_ref, xt_ref, code_ref, wt_ref, cnt_ref, run_ref, wr2_ref, *, gw, n_exp):
    i = pl.program_id(0)

    @pl.when(i == 0)
    def _():
        run_ref[...] = jnp.zeros_like(run_ref)
        wh, wl = _split_bf16(wr_ref[...])
        wr2_ref[:, :LANES] = wh
        wr2_ref[:, LANES:] = wl

    m = mod_ref[0]
    acc = _dot(gla_ref[...], wo_ref[0:gw, :]) + _dot(pool_ref[...], wo_ref[gw:, :])
    h = x_ref[...] + m[2:3] * acc
    h_ref[...] = h
    xt = _rmsnorm(h, nw_ref[...]) * (1.0 + m[4:5]) + m[3:4]
    xt_ref[...] = _pack_bf16_pairs(xt)
    xh, xl = _split_bf16(xt)
    wr2 = wr2_ref[...]
    t1 = _dot(xh, wr2)
    logits = t1[:, :LANES] + t1[:, LANES:] + _dot(xl, wr2[:, :LANES]) + br_ref[...]
    lane = lax.broadcasted_iota(I32, logits.shape, 1)
    neg = jnp.float32(-jnp.inf)
    logits = jnp.where(lane < n_exp, logits, neg)
    vals, hots = [], []
    e_out = jnp.zeros(logits.shape, I32)
    for j in range(TOP_K):
        mx = jnp.max(logits, axis=-1, keepdims=True)
        idx = jnp.min(jnp.where(logits == mx, lane, LANES), axis=-1, keepdims=True)
        hot = lane == idx
        vals.append(mx)
        hots.append(hot)
        e_out = jnp.where(lane == j, idx, e_out)
        logits = jnp.where(hot, neg, logits)
    ex = [jnp.exp(v - vals[0]) for v in vals]
    den = ex[0] + ex[1] + ex[2] + ex[3]
    w_out = jnp.zeros(logits.shape, F32)
    for j in range(TOP_K):
        w_out = jnp.where(lane == j, ex[j] / den, w_out)
    osum = jnp.where(hots[0] | hots[1] | hots[2] | hots[3], 1.0, 0.0)
    before = _dot(lt_ref[...], osum.astype(BF16)) + run_ref[0:1, :]
    rk_out = jnp.zeros(logits.shape, I32)
    for j in range(TOP_K):
        rj = jnp.sum(jnp.where(hots[j], before, 0.0), axis=-1, keepdims=True)
        rk_out = jnp.where(lane == j, rj.astype(I32), rk_out)
    run = run_ref[0:1, :] + jnp.sum(osum, axis=0, keepdims=True)
    run_ref[...] = jnp.broadcast_to(run, run_ref.shape)
    code = e_out * RANK_LIMIT + rk_out
    code_ref[...] = code.T[:code_ref.shape[0], :]
    wt_ref[...] = w_out
    cnt_ref[...] = jnp.broadcast_to(run, cnt_ref.shape).astype(I32)


def _route_call(gla, pool, x, mod, w_out, nw, wr, br, *, tm, n_exp, first_tile, n_tiles):
    n_all, d = x.shape
    n = n_tiles * tm
    gw = gla.shape[1]
    t_per_b = n_all // mod.shape[0]
    lt = jnp.asarray(np.tril(np.ones((tm, tm), np.float32), -1), BF16)
    row_in = lambda width: pl.BlockSpec((tm, width), lambda i: (i + first_tile, 0))
    row = lambda width: pl.BlockSpec((tm, width), lambda i: (i, 0))
    const = lambda shape: pl.BlockSpec(shape, lambda i: (0,) * len(shape))
    sds = jax.ShapeDtypeStruct
    return pl.pallas_call(
        functools.partial(_route_kernel, gw=gw, n_exp=n_exp),
        grid=(n_tiles,),
        in_specs=[row_in(gw), row_in(pool.shape[1]), row_in(d),
                  pl.BlockSpec((1, N_MOD, d), lambda i: ((i + first_tile) * tm // t_per_b, 0, 0)),
                  const(w_out.shape), const((1, d)), const(wr.shape), const((1, LANES)), const((tm, tm))],
        out_specs=[row(d), row(d // 2), pl.BlockSpec((8, tm), lambda i: (0, i)), row(LANES), const((8, LANES))],
        out_shape=[sds((n, d), F32), sds((n, d // 2), U32), sds((8, n), I32),
                   sds((n, LANES), F32), sds((8, LANES), I32)],
        scratch_shapes=[pltpu.VMEM((8, LANES), F32), pltpu.VMEM((d, 2 * LANES), BF16)],
        compiler_params=pltpu.CompilerParams(dimension_semantics=("arbitrary",)),
        name="route",
    )(gla, pool, x, mod, w_out, nw, wr, br, lt)


def _plan_kernel(cnt_ref, code_ref, dest_ref, be_ref, nv_ref, first_ref, next_ref, slot_ref, nu_ref, start_ref,
                 *, n_exp, rows):
    @pl.when(pl.program_id(0) == 0)
    def _():
        blk = (lax.broadcasted_iota(I32, be_ref.shape, 0) * LANES + lax.broadcasted_iota(I32, be_ref.shape, 1))
        blk_row0 = blk * rows
        nxt_e = [None] * n_exp
        nxt = jnp.int32(-1)
        for e in reversed(range(n_exp)):
            nxt_e[e] = nxt
            nxt = jnp.where(cnt_ref[e] > 0, e, nxt)
        zeros = jnp.zeros(be_ref.shape, I32)
        be, end_valid, first, nxt_blk, slot = zeros, zeros, zeros, zeros - 1, zeros
        acc = jnp.int32(0)
        last_e = jnp.int32(0)
        ordinal = jnp.int32(0)
        for e in range(n_exp):
            c = cnt_ref[e]
            start_ref[e] = acc
            in_e = (blk_row0 >= acc) & (c > 0)
            end_valid = jnp.where(in_e, acc + c, end_valid)
            be = jnp.where(in_e, e, be)
            first = jnp.where(in_e, (blk_row0 == acc).astype(I32), first)
            nxt_blk = jnp.where(in_e, nxt_e[e], nxt_blk)
            slot = jnp.where(in_e, ordinal % 2, slot)
            acc = acc + (c + rows - 1) // rows * rows
            last_e = jnp.where(c > 0, e, last_e)
            ordinal = ordinal + (c > 0).astype(I32)
        n_used = acc // rows
        nu_ref[0] = n_used
        be_ref[...] = jnp.where(blk < n_used, be, last_e)
        nv_ref[...] = jnp.clip(end_valid - blk_row0, 0, rows)
        first_ref[...] = first
        next_ref[...] = nxt_blk
        slot_ref[...] = slot

    code = code_ref[...]
    e_vec = code // RANK_LIMIT
    dest = code % RANK_LIMIT
    for e in range(n_exp):
        dest = dest + jnp.where(e_vec == e, start_ref[e], 0)
    dest_ref[...] = dest


def _plan_call(counts, code_t, *, n_exp, rows, chunk):
    n = code_t.shape[1]
    sds = jax.ShapeDtypeStruct
    smem = pltpu.SMEM
    return pl.pallas_call(
        functools.partial(_plan_kernel, n_exp=n_exp, rows=rows),
        grid=(n // chunk,),
        in_specs=[pl.BlockSpec(memory_space=smem), pl.BlockSpec((8, chunk), lambda i: (0, i))],
        out_specs=[pl.BlockSpec((8, chunk), lambda i: (0, i))]
        + [pl.BlockSpec((8, LANES), lambda i: (0, 0))] * 5 + [pl.BlockSpec(memory_space=smem)],
        out_shape=[sds((8, n), I32)] + [sds((8, LANES), I32)] * 5 + [sds((1,), I32)],
        scratch_shapes=[pltpu.SMEM((n_exp,), I32)],
        compiler_params=pltpu.CompilerParams(dimension_semantics=("arbitrary",)),
        name="plan",
    )(counts, code_t)


def _sc_worker_id():
    return lax.axis_index("s") * SC_CORES + lax.axis_index("c")


def _sc_scatter_call(x, idx3, *, n_out):
    n, d = x.shape
    n_win_total, k, w = idx3.shape
    n_win = n_win_total // SC_WORKERS
    mesh = plsc.VectorSubcoreMesh(core_axis_name="c", subcore_axis_name="s")

    @functools.partial(
        pl.kernel, mesh=mesh,
        out_type=jax.ShapeDtypeStruct((n_out, d), x.dtype),
        scratch_types=[pltpu.VMEM((k, w), I32), pltpu.VMEM((w, d), x.dtype), pltpu.SemaphoreType.DMA],
        name="sc_dispatch",
    )
    def kern(x_hbm, idx_hbm, out_hbm, idx_v, rows_v, sem):
        wid = _sc_worker_id()

        @pl.loop(0, n_win)
        def _(i):
            win = wid * n_win + i
            pltpu.sync_copy(idx_hbm.at[win], idx_v)
            pltpu.sync_copy(x_hbm.at[pl.ds(win * w, w)], rows_v)
            for j in range(k):
                pltpu.async_copy(rows_v, out_hbm.at[idx_v.at[j]], sem).wait()

    return kern(x, idx3)


def _sc_gather_call(table, idx3):
    n_workers, n_win, w = idx3.shape
    d = table.shape[1]
    assert n_workers == SC_WORKERS and n_win % 2 == 0
    mesh = plsc.VectorSubcoreMesh(core_axis_name="c", subcore_axis_name="s")

    @functools.partial(
        pl.kernel, mesh=mesh,
        out_type=jax.ShapeDtypeStruct((n_workers * n_win * w, d), table.dtype),
        scratch_types=[pltpu.VMEM((n_win, w), I32), pltpu.VMEM((2, w, d), table.dtype),
                       pltpu.SemaphoreType.DMA((2,)), pltpu.SemaphoreType.DMA((2,))],
        name="sc_gather",
    )
    def kern(table_hbm, idx_hbm, out_hbm, idx_v, rows_v, gsem, osem):
        wid = _sc_worker_id()
        base = wid * n_win
        pltpu.sync_copy(idx_hbm.at[wid], idx_v)

        def gather(wi, b):
            return pltpu.make_async_copy(table_hbm.at[idx_v.at[wi]], rows_v.at[b], gsem.at[b])

        def put(wi, b):
            return pltpu.make_async_copy(rows_v.at[b], out_hbm.at[pl.ds((base + wi) * w, w)], osem.at[b])

        gather(0, 0).start()

        @pl.loop(0, n_win, step=2)
        def _(i):
            for b in range(2):
                wi = i + b

                @pl.when(wi + 1 < n_win)
                def _():
                    @pl.when(wi >= 1)
                    def _():
                        put(wi - 1, 1 - b).wait()
                    gather(wi + 1, 1 - b).start()

                gather(wi, b).wait()
                put(wi, b).start()

        put(n_win - 2, 0).wait()
        put(n_win - 1, 1).wait()

    return kern(table, idx3)


def _expert_kernel(be_ref, nu_ref, nv_ref, first_ref, next_ref, slot_ref,
                   x_ref, wgu_hbm, bgu_ref, wd_hbm, bd_ref, y_ref,
                   wgu_f32_ref, wd_f32_ref, wgu_bf_ref, wd_bf_ref, sem, *, d_ff):
    i = pl.program_id(0)
    slot = slot_ref[i]

    def fetch(e, s):
        return (pltpu.make_async_copy(wgu_hbm.at[e], wgu_f32_ref.at[s], sem.at[0, s]),
                pltpu.make_async_copy(wd_hbm.at[e], wd_f32_ref.at[s], sem.at[1, s]))

    @pl.when(i == 0)
    def _():
        for cp in fetch(be_ref[0], slot):
            cp.start()

    @pl.when(first_ref[i] == 1)
    def _():
        for cp in fetch(be_ref[i], slot):
            cp.wait()

        @pl.when(next_ref[i] >= 0)
        def _():
            for cp in fetch(next_ref[i], 1 - slot):
                cp.start()

        wgu_bf_ref[...] = wgu_f32_ref[slot].astype(BF16)
        wd_bf_ref[...] = wd_f32_ref[slot].astype(BF16)

    @pl.when(i < nu_ref[0])
    def _():
        row = lax.broadcasted_iota(I32, x_ref.shape, 0)
        lo, hi = _unpack_bf16_pairs(jnp.where(row < nv_ref[i], x_ref[...], jnp.uint32(0)))
        xb = jnp.concatenate([lo, hi], axis=1).astype(BF16)
        gu = _dot(xb, wgu_bf_ref[...]) + bgu_ref[0]
        gate = jnp.minimum(gu[:, :d_ff], SWIGLU_LIMIT)
        up = jnp.clip(gu[:, d_ff:], -SWIGLU_LIMIT, SWIGLU_LIMIT)
        act = (up + 1.0) * gate * jax.nn.sigmoid(SWIGLU_ALPHA * gate)
        y_ref[...] = _pack_bf16_pairs(_dot(act.astype(BF16), wd_bf_ref[...]) + bd_ref[0])


def _expert_call(plan, xs, w_gu, b_gu, w_down, b_down):
    block_e, n_valid, first, nxt, slot, n_used = plan
    n_pad = xs.shape[0]
    n_exp, d, two_ff = w_gu.shape
    d_ff = two_ff // 2
    nblk = n_pad // EXPERT_ROWS
    rows = lambda i, be, nu, *_: (jnp.minimum(i, nu[0] - 1), 0)
    per_e = lambda i, be, *_: (be[i], 0, 0)
    grid_spec = pltpu.PrefetchScalarGridSpec(
        num_scalar_prefetch=6,
        grid=(nblk,),
        in_specs=[pl.BlockSpec((EXPERT_ROWS, d // 2), rows),
                  pl.BlockSpec(memory_space=pl.ANY), pl.BlockSpec((1, 1, two_ff), per_e),
                  pl.BlockSpec(memory_space=pl.ANY), pl.BlockSpec((1, 1, d), per_e)],
        out_specs=pl.BlockSpec((EXPERT_ROWS, d // 2), rows),
        scratch_shapes=[pltpu.VMEM((2, d, two_ff), F32), pltpu.VMEM((2, d_ff, d), F32),
                        pltpu.VMEM((d, two_ff), BF16), pltpu.VMEM((d_ff, d), BF16),
                        pltpu.SemaphoreType.DMA((2, 2))],
    )
    flat = lambda a: a.reshape(-1)[:nblk]
    return pl.pallas_call(
        functools.partial(_expert_kernel, d_ff=d_ff),
        grid_spec=grid_spec,
        out_shape=jax.ShapeDtypeStruct((n_pad, d // 2), U32),
        compiler_params=pltpu.CompilerParams(dimension_semantics=("arbitrary",),
                                             vmem_limit_bytes=EXPERT_VMEM_BYTES),
        name="experts",
    )(flat(block_e), n_used, flat(n_valid), flat(first), flat(nxt), flat(slot),
      xs, w_gu, b_gu.reshape(n_exp, 1, two_ff), w_down, b_down.reshape(n_exp, 1, d))


def _combine_kernel(y4_ref, wt_ref, h_ref, mod_ref, fw_ref, o_ref):
    wt = wt_ref[...]
    acc_lo, acc_hi = None, None
    for j in range(TOP_K):
        lo, hi = _unpack_bf16_pairs(y4_ref[j])
        w = wt[:, j:j + 1]
        acc_lo = w * lo if j == 0 else acc_lo + w * lo
        acc_hi = w * hi if j == 0 else acc_hi + w * hi
    acc = jnp.concatenate([acc_lo, acc_hi], axis=1)
    m = mod_ref[0]
    o_ref[...] = _rmsnorm(h_ref[...] + m[5:6] * acc, fw_ref[...])


def _combine_call(y4, wts, h, mod, fw, prev_out, *, n, tg, first_tile):
    d = h.shape[1]
    t_per_b = n // mod.shape[0]
    part = lambda width: pl.BlockSpec((tg, width), lambda i: (i, 0))
    row = lambda width: pl.BlockSpec((tg, width), lambda i: (i + first_tile, 0))
    in_specs = [pl.BlockSpec((TOP_K, tg, d // 2), lambda i: (0, i, 0)),
                part(LANES), part(d),
                pl.BlockSpec((1, N_MOD, d), lambda i: ((i + first_tile) * tg // t_per_b, 0, 0)),
                pl.BlockSpec((1, d), lambda i: (0, 0))]
    args = [y4, wts, h, mod, fw]
    kern = _combine_kernel
    aliases = {}
    if prev_out is not None:
        in_specs.append(pl.BlockSpec(memory_space=pl.ANY))
        args.append(prev_out)
        kern = lambda y4_ref, wt_ref, h_ref, mod_ref, fw_ref, prev_ref, o_ref: _combine_kernel(
            y4_ref, wt_ref, h_ref, mod_ref, fw_ref, o_ref)
        aliases = {len(args) - 1: 0}
    return pl.pallas_call(
        kern,
        grid=(y4.shape[1] // tg,),
        in_specs=in_specs,
        out_specs=row(d),
        out_shape=jax.ShapeDtypeStruct((n, d), F32),
        input_output_aliases=aliases,
        compiler_params=pltpu.CompilerParams(dimension_semantics=("arbitrary",)),
        name="combine",
    )(*args)


def kernel(x, c, ctx, c_ctx, w_ada, b_ada, norm_mix_w, norm_mlp_w, w_in, w_gk_f, b_gk_f, w_gk_b, b_gk_b,
           gla_norm_w, w_pool, pool_scale, w_out, w_router, b_router, w_gu, b_gu, w_down, b_down,
           final_norm_w):
    b, t, d = x.shape
    assert w_ada.shape[0] == 1, "single-layer trunk"
    n_exp = w_router.shape[2]
    rank = w_gk_f.shape[1]
    qk = w_gk_f.shape[2]
    dk = qk // GLA_HEADS
    gw = GLA_HEADS * gla_norm_w.shape[1]
    pw = w_pool.shape[1] * w_pool.shape[2]
    assert w_in.shape[2] == 2 * qk + 2 * gw + 2 * rank + pw and 2 * rank <= LANES
    assert t % SUPER == 0 and ctx.shape[1] % SUPER == 0 and n_exp <= LANES

    rows = -(-(b + 1) // 8) * 8
    cc = jnp.zeros((rows, d), F32).at[:b].set(c).at[b].set(c_ctx)
    mod = _mod_call(cc, w_ada[0], b_ada)
    mod_x = mod[:b].reshape(b, N_MOD, d)
    mod_c = mod[b:b + 1].reshape(1, N_MOD, d)

    wi = w_in[0]
    o_r = 2 * qk + 2 * gw
    w_cat = jnp.concatenate([wi[:, :o_r], wi[:, o_r + 2 * rank:], wi[:, o_r:o_r + 2 * rank],
                             jnp.zeros((d, LANES - 2 * rank), F32)], axis=1).astype(BF16)
    wgk = jnp.zeros((LANES, 2 * qk), F32).at[:rank, :qk].set(w_gk_f[0]).at[rank:2 * rank, qk:].set(w_gk_b[0])
    bgk = jnp.concatenate([b_gk_f[0], b_gk_b[0]])[None, :]
    proj = functools.partial(_inproj_call, nw=norm_mix_w, w=w_cat, wgk=wgk.astype(BF16), bgk=bgk,
                             qk=qk, gw=gw, pw=pw, dk=dk)
    q, k, v, vt, g, xp, gk = proj(x, mod_x, tm=512)
    _, kc, _, vtc, _, _, gkc = proj(ctx, mod_c, tm=SUPER)

    gla = _gla_call(q, k, v, vt, gk, g, kc, vtc, gkc, gla_norm_w)
    pool = _pool_call(xp, w_pool[0].astype(BF16), pool_scale)

    n = b * t
    wr = jnp.zeros((d, LANES), F32).at[:, :n_exp].set(w_router[0])
    br = jnp.zeros((1, LANES), F32).at[0, :n_exp].set(b_router[0])
    n_part = n // MOE_PARTS
    tm, tg = 512, 256
    assert n % MOE_PARTS == 0 and n_part % (SC_WORKERS * SC_WINDOW) == 0 and n_part % tm == 0 and n_part < RANK_LIMIT
    assert (n_part * TOP_K) % (SC_WORKERS * 2 * SC_GATHER_WINDOW) == 0
    n_pad = n_part * TOP_K + n_exp * EXPERT_ROWS
    nblk = n_pad // EXPERT_ROWS
    assert nblk <= 8 * LANES
    parts = range(MOE_PARTS)
    routed = [_route_call(gla.reshape(n, gw), pool.reshape(n, pw), x.reshape(n, d), mod_x, w_out[0].astype(BF16),
                          norm_mlp_w, wr, br, tm=tm, n_exp=n_exp,
                          first_tile=p * n_part // tm, n_tiles=n_part // tm) for p in parts]
    plans = [_plan_call(cnt[0, :n_exp], code_t, n_exp=n_exp, rows=EXPERT_ROWS, chunk=min(n_part, 4096))
             for (_, _, code_t, _, cnt) in routed]
    dests = [plan[0][:TOP_K] for plan in plans]
    xs = [_sc_scatter_call(routed[p][1],
                           dests[p].reshape(TOP_K, n_part // SC_WINDOW, SC_WINDOW).transpose(1, 0, 2), n_out=n_pad)
          for p in parts]
    ys = [_expert_call(plans[p][1:], xs[p], w_gu[0], b_gu[0], w_down[0], b_down[0]) for p in parts]
    y4 = [_sc_gather_call(ys[p], dests[p].reshape(SC_WORKERS, -1, SC_GATHER_WINDOW)).reshape(TOP_K, n_part, d // 2)
          for p in parts]
    out = None
    for p in parts:
        out = _combine_call(y4[p], routed[p][3], routed[p][0], mod_x, final_norm_w[None, :], out,
                            n=n, tg=tg, first_tile=p * n_part // tg)
    return out.reshape(b, t, d)
```

```python
import functools

import numpy as np
import jax
import jax.numpy as jnp
from jax import lax
from jax.experimental import pallas as pl
from jax.experimental.pallas import tpu as pltpu
from jax.experimental.pallas import tpu_sc as plsc

F32 = jnp.float32
BF16 = jnp.bfloat16
I32 = jnp.int32
U32 = jnp.uint32

GRID_W = 64
GLA_HEADS = 4
GLA_CHUNK = 64
GATE_NORMALIZER = 16.0
POOL_WINDOWS = (2, 4, 8, 16)
POOL_PAD_GRID_ROWS = 8
TOP_K = 4
RANK_LIMIT = 1 << 20
SWIGLU_LIMIT = 7.0
SWIGLU_ALPHA = 1.702
N_MOD = 6
EPS = 1e-6

LANES = 128
SUPER = 256
HEAD_PAIR_DK = 128
EXPERT_ROWS = 512
MOE_PARTS = 2
EXPERT_VMEM_BYTES = 56 * 1024 * 1024
SC_CORES = 2
SC_SUBCORES = 16
SC_WORKERS = SC_CORES * SC_SUBCORES
SC_WINDOW = 32
SC_GATHER_WINDOW = 64


def _dot(a, b):
    return jnp.dot(a, b, preferred_element_type=F32)


def _dot_nt(a, b):
    return lax.dot_general(a, b, (((1,), (1,)), ((), ())), preferred_element_type=F32)


def _split_bf16(x):
    hi = x.astype(BF16)
    lo = (x - hi.astype(F32)).astype(BF16)
    return hi, lo


def _pack_bf16_pairs(x):
    c = x.shape[1] // 2
    lo = lax.bitcast_convert_type(x[:, :c].astype(BF16).astype(F32), U32)
    hi = lax.bitcast_convert_type(x[:, c:].astype(BF16).astype(F32), U32)
    return (lo >> 16) | hi


def _unpack_bf16_pairs(p):
    lo = lax.bitcast_convert_type(p << 16, F32)
    hi = lax.bitcast_convert_type(p & jnp.uint32(0xFFFF0000), F32)
    return lo, hi


def _rmsnorm(x, w):
    var = jnp.mean(x * x, axis=-1, keepdims=True)
    return x * lax.rsqrt(var + EPS) * w


def _mod_kernel(c_ref, w_ref, b_ref, o_ref):
    c = c_ref[...]
    s = c * jax.nn.sigmoid(c)
    o_ref[...] = jnp.dot(s, w_ref[...], precision=lax.Precision.HIGHEST,
                         preferred_element_type=F32) + b_ref[...]


def _mod_call(cc, w_ada, b_ada):
    rows, d = cc.shape
    n = w_ada.shape[1]
    tn = 1024
    return pl.pallas_call(
        _mod_kernel,
        grid=(n // tn,),
        in_specs=[pl.BlockSpec((rows, d), lambda j: (0, 0)),
                  pl.BlockSpec((d, tn), lambda j: (0, j)),
                  pl.BlockSpec((1, tn), lambda j: (0, j))],
        out_specs=pl.BlockSpec((rows, tn), lambda j: (0, j)),
        out_shape=jax.ShapeDtypeStruct((rows, n), F32),
        name="mod",
    )(cc, w_ada, b_ada)


def _inproj_kernel(x_ref, mod_ref, nw_ref, w_ref, wgk_ref, bgk_ref,
                   q_ref, k_ref, v_ref, vt_ref, g_ref, p_ref, gk_ref, *, qk, gw, pw, dk):
    x = x_ref[0]
    m = mod_ref[0]
    hm = (_rmsnorm(x, nw_ref[...]) * (1.0 + m[1:2]) + m[0:1]).astype(BF16)
    p = _dot(hm, w_ref[...])
    vt = p[:, 2 * qk:2 * qk + gw].T
    for s in range(vt_ref.shape[1]):
        vt_ref[0, s] = vt[:, s * SUPER:(s + 1) * SUPER].astype(BF16)
    o = 0
    q_ref[0] = (p[:, o:o + qk] * (dk ** -0.5)).astype(BF16); o += qk
    k_ref[0] = p[:, o:o + qk].astype(BF16); o += qk
    v_ref[0] = p[:, o:o + gw].astype(BF16); o += gw
    g_ref[0] = p[:, o:o + gw].astype(BF16); o += gw
    p_ref[0] = p[:, o:o + pw].astype(BF16); o += pw
    r = p[:, o:o + LANES]
    z = _dot(r.astype(BF16), wgk_ref[...]) + bgk_ref[...]
    gk_ref[0] = (jnp.minimum(z, 0.0) - jnp.log1p(jnp.exp(-jnp.abs(z)))) * (1.0 / GATE_NORMALIZER)


def _inproj_call(x, mod, nw, w, wgk, bgk, *, qk, gw, pw, dk, tm):
    b, t, d = x.shape
    n_in = w.shape[1]
    bs = lambda width: pl.BlockSpec((1, tm, width), lambda i, j: (i, j, 0))
    const = lambda shape: pl.BlockSpec(shape, lambda i, j: (0,) * len(shape))
    per_batch = mod.shape[0] > 1
    sds = jax.ShapeDtypeStruct
    return pl.pallas_call(
        functools.partial(_inproj_kernel, qk=qk, gw=gw, pw=pw, dk=dk),
        grid=(b, t // tm),
        in_specs=[bs(d),
                  pl.BlockSpec((1, N_MOD, d), (lambda i, j: (i, 0, 0)) if per_batch else (lambda i, j: (0, 0, 0))),
                  const((1, d)), const((d, n_in)), const((LANES, 2 * qk)), const((1, 2 * qk))],
        out_specs=[bs(qk), bs(qk), bs(gw),
                   pl.BlockSpec((1, tm // SUPER, gw, SUPER), lambda i, j: (i, j, 0, 0)),
                   bs(gw), bs(pw), bs(2 * qk)],
        out_shape=[sds((b, t, qk), BF16), sds((b, t, qk), BF16), sds((b, t, gw), BF16),
                   sds((b, t // SUPER, gw, SUPER), BF16),
                   sds((b, t, gw), BF16), sds((b, t, pw), BF16), sds((b, t, 2 * qk), F32)],
        compiler_params=pltpu.CompilerParams(dimension_semantics=("arbitrary", "arbitrary")),
        name="inproj",
    )(x, mod, nw, w, wgk, bgk)


def _gla_super(q, k, v, vt, gk, cm, amask, bd_mask, st, fwd, want_out):
    nch = SUPER // GLA_CHUNK
    order = tuple(range(nch)) if fwd else tuple(reversed(range(nch)))
    last_row = GLA_CHUNK - 1 if fwd else 0
    mid_row = GLA_CHUNK // 2 - 1 if fwd else GLA_CHUNK // 2
    hi, lo = _split_bf16(gk)
    bcum = _dot(cm, hi) + _dot(cm, lo)

    def chunk_row(r):
        return jnp.concatenate(
            [jnp.broadcast_to(bcum[c * GLA_CHUNK + r:c * GLA_CHUNK + r + 1, :], (GLA_CHUNK, bcum.shape[1]))
             for c in range(nch)], axis=0)

    chunk_of_row = lax.broadcasted_iota(I32, bcum.shape, 0) // GLA_CHUNK

    def by_chunk(x):
        return jnp.concatenate([jnp.where(chunk_of_row == c, x, 0.0).astype(BF16) for c in range(nch)], axis=1)

    blast = chunk_row(last_row)
    u_all = _dot(vt, by_chunk(k * jnp.exp(blast - bcum)))
    before = [None] * nch
    for c in order:
        before[c] = st
        decay = jnp.exp(bcum[c * GLA_CHUNK + last_row:c * GLA_CHUNK + last_row + 1, :])
        st = st * decay + jnp.where(bd_mask, u_all[:, c * HEAD_PAIR_DK:(c + 1) * HEAD_PAIR_DK], 0.0)
    if not want_out:
        return None, st
    bmid = chunk_row(mid_row)
    qt = q * jnp.exp(bcum - bmid)
    kt = (k * jnp.exp(bmid - bcum)).astype(BF16)
    lane = lax.broadcasted_iota(I32, qt.shape, 1)
    half = HEAD_PAIR_DK // 2
    o_heads = []
    for hh in range(2):
        sel = (lane < half) if hh == 0 else (lane >= half)
        a = _dot_nt(jnp.where(sel, qt, 0.0).astype(BF16), kt)
        a = jnp.where(amask, a, 0.0).astype(BF16)
        o_heads.append(_dot(a, v[:, hh * LANES:(hh + 1) * LANES]))
    qh = (q * jnp.exp(bcum)).astype(BF16)
    o_inter = jnp.concatenate(
        [_dot_nt(qh[c * GLA_CHUNK:(c + 1) * GLA_CHUNK], before[c].astype(BF16)) for c in range(nch)], axis=0)
    return jnp.concatenate(o_heads, axis=1) + o_inter, st


def _gla_kernel(q_ref, k_ref, v_ref, vt_ref, gkf_ref, gkb_ref, g_ref, kc_ref, vtc_ref, gkfc_ref, gkbc_ref,
                nw_ref, cmf_ref, cmb_ref, o_ref, stf_ref, stb_ref, of_ref, ob_ref):
    t = q_ref.shape[1]
    tc = kc_ref.shape[1]
    nsc, nscc = t // SUPER, tc // SUPER
    cmf = cmf_ref[...]
    cmb = cmb_ref[...]
    amask_f = cmf > 0
    amask_b = cmb > 0
    row = lax.broadcasted_iota(I32, (2 * LANES, HEAD_PAIR_DK), 0)
    lane = lax.broadcasted_iota(I32, (2 * LANES, HEAD_PAIR_DK), 1)
    bd_mask = (row < LANES) == (lane < HEAD_PAIR_DK // 2)

    def ctx_state(gk_ref, cm, fwd, j, st):
        rows = pl.ds(j * SUPER, SUPER)
        return _gla_super(None, kc_ref[0, rows, :].astype(F32), None, vtc_ref[0, j], gk_ref[0, rows, :],
                          cm, None, bd_mask, st, fwd, False)[1]

    def latent(gk_ref, cm, amask, fwd, j, st):
        rows = pl.ds(pl.multiple_of(j * SUPER, SUPER), SUPER)
        return _gla_super(q_ref[0, rows, :].astype(F32), k_ref[0, rows, :].astype(F32), v_ref[0, rows, :],
                          vt_ref[0, j], gk_ref[0, rows, :], cm, amask, bd_mask, st, fwd, True)

    st = jnp.zeros(stf_ref.shape, F32)
    for j in range(nscc):
        st = ctx_state(gkfc_ref, cmf, True, j, st)
    stf_ref[...] = st
    st = jnp.zeros(stb_ref.shape, F32)
    for j in reversed(range(nscc)):
        st = ctx_state(gkbc_ref, cmb, False, j, st)
    stb_ref[...] = st

    def scan_body(jj, carry):
        jb = nsc - 1 - jj
        of, stf = latent(gkf_ref, cmf, amask_f, True, jj, stf_ref[...])
        of_ref[pl.ds(pl.multiple_of(jj * SUPER, SUPER), SUPER), :] = of
        stf_ref[...] = stf
        ob, stb = latent(gkb_ref, cmb, amask_b, False, jb, stb_ref[...])
        ob_ref[pl.ds(pl.multiple_of(jb * SUPER, SUPER), SUPER), :] = ob
        stb_ref[...] = stb
        return carry

    lax.fori_loop(0, nsc, scan_body, 0, unroll=2)

    nw = nw_ref[...]

    def out_body(j, carry):
        rows = pl.ds(pl.multiple_of(j * SUPER, SUPER), SUPER)
        o = of_ref[rows, :] + ob_ref[rows, :]
        g = g_ref[0, rows, :].astype(F32)
        gate = g * jax.nn.sigmoid(g)
        for hh in range(2):
            oh = o[:, hh * LANES:(hh + 1) * LANES]
            on = oh * lax.rsqrt(jnp.mean(oh * oh, axis=-1, keepdims=True) + EPS) * nw
            o_ref[0, rows, hh * LANES:(hh + 1) * LANES] = (on * gate[:, hh * LANES:(hh + 1) * LANES]).astype(BF16)
        return carry

    lax.fori_loop(0, nsc, out_body, 0)


def _gla_masks():
    i = np.arange(SUPER)
    same = (i[:, None] // GLA_CHUNK) == (i[None, :] // GLA_CHUNK)
    fwd = same & (i[None, :] <= i[:, None])
    bwd = same & (i[None, :] >= i[:, None])
    return jnp.asarray(fwd, BF16), jnp.asarray(bwd, BF16)


def _gla_call(q, k, v, vt, gk, g, kc, vtc, gkc, nw):
    b, t, qk = q.shape
    tc = kc.shape[1]
    npair = qk // HEAD_PAIR_DK
    cmf, cmb = _gla_masks()
    lat = lambda width, off: pl.BlockSpec((1, t, width), lambda i, j: (i, 0, j + off))
    ctx = lambda width, off: pl.BlockSpec((1, tc, width), lambda i, j: (i, 0, j + off))
    tr = lambda n_groups: pl.BlockSpec((1, n_groups, 2 * LANES, SUPER), lambda i, j: (i, 0, j, 0))
    const = lambda shape: pl.BlockSpec(shape, lambda i, j: (0,) * len(shape))
    return pl.pallas_call(
        _gla_kernel,
        grid=(b, npair),
        in_specs=[lat(HEAD_PAIR_DK, 0), lat(HEAD_PAIR_DK, 0), lat(2 * LANES, 0), tr(t // SUPER),
                  lat(HEAD_PAIR_DK, 0), lat(HEAD_PAIR_DK, npair), lat(2 * LANES, 0),
                  ctx(HEAD_PAIR_DK, 0), tr(tc // SUPER), ctx(HEAD_PAIR_DK, 0), ctx(HEAD_PAIR_DK, npair),
                  const((1, LANES)), const(cmf.shape), const(cmb.shape)],
        out_specs=lat(2 * LANES, 0),
        out_shape=jax.ShapeDtypeStruct((b, t, v.shape[2]), BF16),
        scratch_shapes=[pltpu.VMEM((2 * LANES, HEAD_PAIR_DK), F32), pltpu.VMEM((2 * LANES, HEAD_PAIR_DK), F32),
                        pltpu.VMEM((t, 2 * LANES), F32), pltpu.VMEM((t, 2 * LANES), F32)],
        compiler_params=pltpu.CompilerParams(dimension_semantics=("arbitrary", "arbitrary")),
        name="gla",
    )(q, k, v, vt, gk, gk, g, kc, vtc, gkc, gkc, nw, cmf, cmb)


def _pool_kernel(x_ref, cm_ref, cnt_ref, wp_ref, ps_ref, o_ref, a_ref, b_ref):
    t = x_ref.shape[1]
    pad = POOL_PAD_GRID_ROWS * GRID_W
    total = t + 2 * pad
    for gi, w in enumerate(POOL_WINDOWS):
        lo = w // 2
        cols = slice(gi * LANES, (gi + 1) * LANES)
        cmat = cm_ref[gi]
        a_ref[0:pad, :] = jnp.zeros((pad, LANES), F32)
        a_ref[pad + t:total, :] = jnp.zeros((pad, LANES), F32)
        for blk in range(t // SUPER):
            rs = slice(blk * SUPER, (blk + 1) * SUPER)
            a_ref[pad + blk * SUPER:pad + (blk + 1) * SUPER, :] = _dot(cmat, x_ref[0, rs, cols])
        src, dst = a_ref, b_ref
        m = 1
        while m < w:
            sh = m * GRID_W
            dst[0:total - sh, :] = src[0:total - sh, :] + src[sh:total, :]
            src, dst = dst, src
            m *= 2
        first = pad - lo * GRID_W
        pooled = src[first:first + t, :] / cnt_ref[gi] - x_ref[0, :, cols].astype(F32)
        yp = _dot(pooled.astype(BF16), wp_ref[gi]) * ps_ref[:, cols]
        o_ref[0, :, cols] = yp.astype(BF16)


def _pool_col_mats():
    i = np.arange(SUPER)
    same_row = (i[:, None] // GRID_W) == (i[None, :] // GRID_W)
    d = i[None, :] - i[:, None]
    mats = []
    for w in POOL_WINDOWS:
        lo = w // 2
        hi = w - 1 - lo
        mats.append(same_row & (d >= -lo) & (d <= hi))
    return jnp.asarray(np.stack(mats), BF16)


def _pool_counts(t):
    rows = t // GRID_W
    r = np.arange(t) // GRID_W
    c = np.arange(t) % GRID_W
    out = []
    for w in POOL_WINDOWS:
        lo = w // 2
        hi = w - 1 - lo
        cnt_r = np.minimum(r + hi + 1, rows) - np.maximum(r - lo, 0)
        cnt_c = np.minimum(c + hi + 1, GRID_W) - np.maximum(c - lo, 0)
        out.append(np.broadcast_to((cnt_r * cnt_c).astype(np.float32)[:, None], (t, LANES)))
    return jnp.asarray(np.stack(out))


def _pool_call(xp, w_pool, pool_scale):
    b, t, pw = xp.shape
    ng = len(POOL_WINDOWS)
    assert max(POOL_WINDOWS) // 2 <= POOL_PAD_GRID_ROWS and t % GRID_W == 0
    cm = _pool_col_mats()
    cnt = _pool_counts(t)
    staged = t + 2 * POOL_PAD_GRID_ROWS * GRID_W
    const = lambda shape: pl.BlockSpec(shape, lambda i: (0,) * len(shape))
    return pl.pallas_call(
        _pool_kernel,
        grid=(b,),
        in_specs=[pl.BlockSpec((1, t, pw), lambda i: (i, 0, 0)),
                  const(cm.shape), const(cnt.shape), const((ng, LANES, LANES)), const((1, pw))],
        out_specs=pl.BlockSpec((1, t, pw), lambda i: (i, 0, 0)),
        out_shape=jax.ShapeDtypeStruct((b, t, pw), BF16),
        scratch_shapes=[pltpu.VMEM((staged, LANES), F32), pltpu.VMEM((staged, LANES), F32)],
        compiler_params=pltpu.CompilerParams(dimension_semantics=("arbitrary",)),
        name="pool",
    )(xp, cm, cnt, w_pool, pool_scale)


def _route_kernel(gla_ref, pool_ref, x_ref, mod_ref, wo_ref, nw_ref, wr_ref, br_ref, lt_ref,
                  h_ref, xt_ref, code_ref, wt_ref, cnt_ref, run_ref, wr2_ref, *, gw, n_exp):
    i = pl.program_id(0)

    @pl.when(i == 0)
    def _():
        run_ref[...] = jnp.zeros_like(run_ref)
        wh, wl = _split_bf16(wr_ref[...])
        wr2_ref[:, :LANES] = wh
        wr2_ref[:, LANES:] = wl

    m = mod_ref[0]
    acc = _dot(gla_ref[...], wo_ref[0:gw, :]) + _dot(pool_ref[...], wo_ref[gw:, :])
    h = x_ref[...] + m[2:3] * acc
    h_ref[...] = h
    xt = _rmsnorm(h, nw_ref[...]) * (1.0 + m[4:5]) + m[3:4]
    xt_ref[...] = _pack_bf16_pairs(xt)
    xh, xl = _split_bf16(xt)
    wr2 = wr2_ref[...]
    t1 = _dot(xh, wr2)
    logits = t1[:, :LANES] + t1[:, LANES:] + _dot(xl, wr2[:, :LANES]) + br_ref[...]
    lane = lax.broadcasted_iota(I32, logits.shape, 1)
    neg = jnp.float32(-jnp.inf)
    logits = jnp.where(lane < n_exp, logits, neg)
    vals, hots = [], []
    e_out = jnp.zeros(logits.shape, I32)
    for j in range(TOP_K):
        mx = jnp.max(logits, axis=-1, keepdims=True)
        idx = jnp.min(jnp.where(logits == mx, lane, LANES), axis=-1, keepdims=True)
        hot = lane == idx
        vals.append(mx)
        hots.append(hot)
        e_out = jnp.where(lane == j, idx, e_out)
        logits = jnp.where(hot, neg, logits)
    ex = [jnp.exp(v - vals[0]) for v in vals]
    den = ex[0] + ex[1] + ex[2] + ex[3]
    w_out = jnp.zeros(logits.shape, F32)
    for j in range(TOP_K):
        w_out = jnp.where(lane == j, ex[j] / den, w_out)
    osum = jnp.where(hots[0] | hots[1] | hots[2] | hots[3], 1.0, 0.0)
    before = _dot(lt_ref[...], osum.astype(BF16)) + run_ref[0:1, :]
    rk_out = jnp.zeros(logits.shape, I32)
    for j in range(TOP_K):
        rj = jnp.sum(jnp.where(hots[j], before, 0.0), axis=-1, keepdims=True)
        rk_out = jnp.where(lane == j, rj.astype(I32), rk_out)
    run = run_ref[0:1, :] + jnp.sum(osum, axis=0, keepdims=True)
    run_ref[...] = jnp.broadcast_to(run, run_ref.shape)
    code = e_out * RANK_LIMIT + rk_out
    code_ref[...] = code.T[:code_ref.shape[0], :]
    wt_ref[...] = w_out
    cnt_ref[...] = jnp.broadcast_to(run, cnt_ref.shape).astype(I32)


def _route_call(gla, pool, x, mod, w_out, nw, wr, br, *, tm, n_exp, first_tile, n_tiles):
    n_all, d = x.shape
    n = n_tiles * tm
    gw = gla.shape[1]
    t_per_b = n_all // mod.shape[0]
    lt = jnp.asarray(np.tril(np.ones((tm, tm), np.float32), -1), BF16)
    row_in = lambda width: pl.BlockSpec((tm, width), lambda i: (i + first_tile, 0))
    row = lambda width: pl.BlockSpec((tm, width), lambda i: (i, 0))
    const = lambda shape: pl.BlockSpec(shape, lambda i: (0,) * len(shape))
    sds = jax.ShapeDtypeStruct
    return pl.pallas_call(
        functools.partial(_route_kernel, gw=gw, n_exp=n_exp),
        grid=(n_tiles,),
        in_specs=[row_in(gw), row_in(pool.shape[1]), row_in(d),
                  pl.BlockSpec((1, N_MOD, d), lambda i: ((i + first_tile) * tm // t_per_b, 0, 0)),
                  const(w_out.shape), const((1, d)), const(wr.shape), const((1, LANES)), const((tm, tm))],
        out_specs=[row(d), row(d // 2), pl.BlockSpec((8, tm), lambda i: (0, i)), row(LANES), const((8, LANES))],
        out_shape=[sds((n, d), F32), sds((n, d // 2), U32), sds((8, n), I32),
                   sds((n, LANES), F32), sds((8, LANES), I32)],
        scratch_shapes=[pltpu.VMEM((8, LANES), F32), pltpu.VMEM((d, 2 * LANES), BF16)],
        compiler_params=pltpu.CompilerParams(dimension_semantics=("arbitrary",)),
        name="route",
    )(gla, pool, x, mod, w_out, nw, wr, br, lt)


def _plan_kernel(cnt_ref, code_ref, dest_ref, be_ref, nv_ref, first_ref, next_ref, slot_ref, nu_ref, start_ref,
                 *, n_exp, rows):
    @pl.when(pl.program_id(0) == 0)
    def _():
        blk = (lax.broadcasted_iota(I32, be_ref.shape, 0) * LANES + lax.broadcasted_iota(I32, be_ref.shape, 1))
        blk_row0 = blk * rows
        nxt_e = [None] * n_exp
        nxt = jnp.int32(-1)
        for e in reversed(range(n_exp)):
            nxt_e[e] = nxt
            nxt = jnp.where(cnt_ref[e] > 0, e, nxt)
        zeros = jnp.zeros(be_ref.shape, I32)
        be, end_valid, first, nxt_blk, slot = zeros, zeros, zeros, zeros - 1, zeros
        acc = jnp.int32(0)
        last_e = jnp.int32(0)
        ordinal = jnp.int32(0)
        for e in range(n_exp):
            c = cnt_ref[e]
            start_ref[e] = acc
            in_e = (blk_row0 >= acc) & (c > 0)
            end_valid = jnp.where(in_e, acc + c, end_valid)
            be = jnp.where(in_e, e, be)
            first = jnp.where(in_e, (blk_row0 == acc).astype(I32), first)
            nxt_blk = jnp.where(in_e, nxt_e[e], nxt_blk)
            slot = jnp.where(in_e, ordinal % 2, slot)
            acc = acc + (c + rows - 1) // rows * rows
            last_e = jnp.where(c > 0, e, last_e)
            ordinal = ordinal + (c > 0).astype(I32)
        n_used = acc // rows
        nu_ref[0] = n_used
        be_ref[...] = jnp.where(blk < n_used, be, last_e)
        nv_ref[...] = jnp.clip(end_valid - blk_row0, 0, rows)
        first_ref[...] = first
        next_ref[...] = nxt_blk
        slot_ref[...] = slot

    code = code_ref[...]
    e_vec = code // RANK_LIMIT
    dest = code % RANK_LIMIT
    for e in range(n_exp):
        dest = dest + jnp.where(e_vec == e, start_ref[e], 0)
    dest_ref[...] = dest


def _plan_call(counts, code_t, *, n_exp, rows, chunk):
    n = code_t.shape[1]
    sds = jax.ShapeDtypeStruct
    smem = pltpu.SMEM
    return pl.pallas_call(
        functools.partial(_plan_kernel, n_exp=n_exp, rows=rows),
        grid=(n // chunk,),
        in_specs=[pl.BlockSpec(memory_space=smem), pl.BlockSpec((8, chunk), lambda i: (0, i))],
        out_specs=[pl.BlockSpec((8, chunk), lambda i: (0, i))]
        + [pl.BlockSpec((8, LANES), lambda i: (0, 0))] * 5 + [pl.BlockSpec(memory_space=smem)],
        out_shape=[sds((8, n), I32)] + [sds((8, LANES), I32)] * 5 + [sds((1,), I32)],
        scratch_shapes=[pltpu.SMEM((n_exp,), I32)],
        compiler_params=pltpu.CompilerParams(dimension_semantics=("arbitrary",)),
        name="plan",
    )(counts, code_t)


def _sc_worker_id():
    return lax.axis_index("s") * SC_CORES + lax.axis_index("c")


def _sc_scatter_call(x, idx3, *, n_out):
    n, d = x.shape
    n_win_total, k, w = idx3.shape
    n_win = n_win_total // SC_WORKERS
    mesh = plsc.VectorSubcoreMesh(core_axis_name="c", subcore_axis_name="s")

    @functools.partial(
        pl.kernel, mesh=mesh,
        out_type=jax.ShapeDtypeStruct((n_out, d), x.dtype),
        scratch_types=[pltpu.VMEM((k, w), I32), pltpu.VMEM((w, d), x.dtype), pltpu.SemaphoreType.DMA],
        name="sc_dispatch",
    )
    def kern(x_hbm, idx_hbm, out_hbm, idx_v, rows_v, sem):
        wid = _sc_worker_id()

        @pl.loop(0, n_win)
        def _(i):
            win = wid * n_win + i
            pltpu.sync_copy(idx_hbm.at[win], idx_v)
            pltpu.sync_copy(x_hbm.at[pl.ds(win * w, w)], rows_v)
            for j in range(k):
                pltpu.async_copy(rows_v, out_hbm.at[idx_v.at[j]], sem).wait()

    return kern(x, idx3)


def _sc_gather_call(table, idx3):
    n_workers, n_win, w = idx3.shape
    d = table.shape[1]
    assert n_workers == SC_WORKERS and n_win % 2 == 0
    mesh = plsc.VectorSubcoreMesh(core_axis_name="c", subcore_axis_name="s")

    @functools.partial(
        pl.kernel, mesh=mesh,
        out_type=jax.ShapeDtypeStruct((n_workers * n_win * w, d), table.dtype),
        scratch_types=[pltpu.VMEM((n_win, w), I32), pltpu.VMEM((2, w, d), table.dtype),
                       pltpu.SemaphoreType.DMA((2,)), pltpu.SemaphoreType.DMA((2,))],
        name="sc_gather",
    )
    def kern(table_hbm, idx_hbm, out_hbm, idx_v, rows_v, gsem, osem):
        wid = _sc_worker_id()
        base = wid * n_win
        pltpu.sync_copy(idx_hbm.at[wid], idx_v)

        def gather(wi, b):
            return pltpu.make_async_copy(table_hbm.at[idx_v.at[wi]], rows_v.at[b], gsem.at[b])

        def put(wi, b):
            return pltpu.make_async_copy(rows_v.at[b], out_hbm.at[pl.ds((base + wi) * w, w)], osem.at[b])

        gather(0, 0).start()

        @pl.loop(0, n_win, step=2)
        def _(i):
            for b in range(2):
                wi = i + b

                @pl.when(wi + 1 < n_win)
                def _():
                    @pl.when(wi >= 1)
                    def _():
                        put(wi - 1, 1 - b).wait()
                    gather(wi + 1, 1 - b).start()

                gather(wi, b).wait()
                put(wi, b).start()

        put(n_win - 2, 0).wait()
        put(n_win - 1, 1).wait()

    return kern(table, idx3)


def _expert_kernel(be_ref, nu_ref, nv_ref, first_ref, next_ref, slot_ref,
                   x_ref, wgu_hbm, bgu_ref, wd_hbm, bd_ref, y_ref,
                   wgu_f32_ref, wd_f32_ref, wgu_bf_ref, wd_bf_ref, sem, *, d_ff):
    i = pl.program_id(0)
    slot = slot_ref[i]

    def fetch(e, s):
        return (pltpu.make_async_copy(wgu_hbm.at[e], wgu_f32_ref.at[s], sem.at[0, s]),
                pltpu.make_async_copy(wd_hbm.at[e], wd_f32_ref.at[s], sem.at[1, s]))

    @pl.when(i == 0)
    def _():
        for cp in fetch(be_ref[0], slot):
            cp.start()

    @pl.when(first_ref[i] == 1)
    def _():
        for cp in fetch(be_ref[i], slot):
            cp.wait()

        @pl.when(next_ref[i] >= 0)
        def _():
            for cp in fetch(next_ref[i], 1 - slot):
                cp.start()

        wgu_bf_ref[...] = wgu_f32_ref[slot].astype(BF16)
        wd_bf_ref[...] = wd_f32_ref[slot].astype(BF16)

    n_valid = nv_ref[i]
    half = x_ref.shape[0] // 2

    def expert_rows(rows):
        row = lax.broadcasted_iota(I32, (rows, x_ref.shape[1]), 0)
        lo, hi = _unpack_bf16_pairs(jnp.where(row < n_valid, x_ref[0:rows, :], jnp.uint32(0)))
        xb = jnp.concatenate([lo, hi], axis=1).astype(BF16)
        gu = _dot(xb, wgu_bf_ref[...]) + bgu_ref[0]
        gate = jnp.minimum(gu[:, :d_ff], SWIGLU_LIMIT)
        up = jnp.clip(gu[:, d_ff:], -SWIGLU_LIMIT, SWIGLU_LIMIT)
        act = (up + 1.0) * gate * jax.nn.sigmoid(SWIGLU_ALPHA * gate)
        y_ref[0:rows, :] = _pack_bf16_pairs(_dot(act.astype(BF16), wd_bf_ref[...]) + bd_ref[0])

    in_use = i < nu_ref[0]

    @pl.when(in_use & (n_valid > half))
    def _():
        expert_rows(x_ref.shape[0])

    @pl.when(in_use & (n_valid <= half))
    def _():
        expert_rows(half)
        y_ref[half:, :] = jnp.zeros((x_ref.shape[0] - half, y_ref.shape[1]), y_ref.dtype)


def _expert_call(plan, xs, w_gu, b_gu, w_down, b_down):
    block_e, n_valid, first, nxt, slot, n_used = plan
    n_pad = xs.shape[0]
    n_exp, d, two_ff = w_gu.shape
    d_ff = two_ff // 2
    nblk = n_pad // EXPERT_ROWS
    rows = lambda i, be, nu, *_: (jnp.minimum(i, nu[0] - 1), 0)
    per_e = lambda i, be, *_: (be[i], 0, 0)
    grid_spec = pltpu.PrefetchScalarGridSpec(
        num_scalar_prefetch=6,
        grid=(nblk,),
        in_specs=[pl.BlockSpec((EXPERT_ROWS, d // 2), rows),
                  pl.BlockSpec(memory_space=pl.ANY), pl.BlockSpec((1, 1, two_ff), per_e),
                  pl.BlockSpec(memory_space=pl.ANY), pl.BlockSpec((1, 1, d), per_e)],
        out_specs=pl.BlockSpec((EXPERT_ROWS, d // 2), rows),
        scratch_shapes=[pltpu.VMEM((2, d, two_ff), F32), pltpu.VMEM((2, d_ff, d), F32),
                        pltpu.VMEM((d, two_ff), BF16), pltpu.VMEM((d_ff, d), BF16),
                        pltpu.SemaphoreType.DMA((2, 2))],
    )
    flat = lambda a: a.reshape(-1)[:nblk]
    return pl.pallas_call(
        functools.partial(_expert_kernel, d_ff=d_ff),
        grid_spec=grid_spec,
        out_shape=jax.ShapeDtypeStruct((n_pad, d // 2), U32),
        compiler_params=pltpu.CompilerParams(dimension_semantics=("arbitrary",),
                                             vmem_limit_bytes=EXPERT_VMEM_BYTES),
        name="experts",
    )(flat(block_e), n_used, flat(n_valid), flat(first), flat(nxt), flat(slot),
      xs, w_gu, b_gu.reshape(n_exp, 1, two_ff), w_down, b_down.reshape(n_exp, 1, d))


def _combine_kernel(y4_ref, wt_ref, h_ref, mod_ref, fw_ref, o_ref):
    wt = wt_ref[...]
    acc_lo, acc_hi = None, None
    for j in range(TOP_K):
        lo, hi = _unpack_bf16_pairs(y4_ref[j])
        w = wt[:, j:j + 1]
        acc_lo = w * lo if j == 0 else acc_lo + w * lo
        acc_hi = w * hi if j == 0 else acc_hi + w * hi
    acc = jnp.concatenate([acc_lo, acc_hi], axis=1)
    m = mod_ref[0]
    o_ref[...] = _rmsnorm(h_ref[...] + m[5:6] * acc, fw_ref[...])


def _combine_call(y4, wts, h, mod, fw, prev_out, *, n, tg, first_tile):
    d = h.shape[1]
    t_per_b = n // mod.shape[0]
    part = lambda width: pl.BlockSpec((tg, width), lambda i: (i, 0))
    row = lambda width: pl.BlockSpec((tg, width), lambda i: (i + first_tile, 0))
    in_specs = [pl.BlockSpec((TOP_K, tg, d // 2), lambda i: (0, i, 0)),
                part(LANES), part(d),
                pl.BlockSpec((1, N_MOD, d), lambda i: ((i + first_tile) * tg // t_per_b, 0, 0)),
                pl.BlockSpec((1, d), lambda i: (0, 0))]
    args = [y4, wts, h, mod, fw]
    kern = _combine_kernel
    aliases = {}
    if prev_out is not None:
        in_specs.append(pl.BlockSpec(memory_space=pl.ANY))
        args.append(prev_out)
        kern = lambda y4_ref, wt_ref, h_ref, mod_ref, fw_ref, prev_ref, o_ref: _combine_kernel(
            y4_ref, wt_ref, h_ref, mod_ref, fw_ref, o_ref)
        aliases = {len(args) - 1: 0}
    return pl.pallas_call(
        kern,
        grid=(y4.shape[1] // tg,),
        in_specs=in_specs,
        out_specs=row(d),
        out_shape=jax.ShapeDtypeStruct((n, d), F32),
        input_output_aliases=aliases,
        compiler_params=pltpu.CompilerParams(dimension_semantics=("arbitrary",)),
        name="combine",
    )(*args)


def kernel(x, c, ctx, c_ctx, w_ada, b_ada, norm_mix_w, norm_mlp_w, w_in, w_gk_f, b_gk_f, w_gk_b, b_gk_b,
           gla_norm_w, w_pool, pool_scale, w_out, w_router, b_router, w_gu, b_gu, w_down, b_down,
           final_norm_w):
    b, t, d = x.shape
    assert w_ada.shape[0] == 1, "single-layer trunk"
    n_exp = w_router.shape[2]
    rank = w_gk_f.shape[1]
    qk = w_gk_f.shape[2]
    dk = qk // GLA_HEADS
    gw = GLA_HEADS * gla_norm_w.shape[1]
    pw = w_pool.shape[1] * w_pool.shape[2]
    assert w_in.shape[2] == 2 * qk + 2 * gw + 2 * rank + pw and 2 * rank <= LANES
    assert t % SUPER == 0 and ctx.shape[1] % SUPER == 0 and n_exp <= LANES

    rows = -(-(b + 1) // 8) * 8
    cc = jnp.zeros((rows, d), F32).at[:b].set(c).at[b].set(c_ctx)
    mod = _mod_call(cc, w_ada[0], b_ada)
    mod_x = mod[:b].reshape(b, N_MOD, d)
    mod_c = mod[b:b + 1].reshape(1, N_MOD, d)

    wi = w_in[0]
    o_r = 2 * qk + 2 * gw
    w_cat = jnp.concatenate([wi[:, :o_r], wi[:, o_r + 2 * rank:], wi[:, o_r:o_r + 2 * rank],
                             jnp.zeros((d, LANES - 2 * rank), F32)], axis=1).astype(BF16)
    wgk = jnp.zeros((LANES, 2 * qk), F32).at[:rank, :qk].set(w_gk_f[0]).at[rank:2 * rank, qk:].set(w_gk_b[0])
    bgk = jnp.concatenate([b_gk_f[0], b_gk_b[0]])[None, :]
    proj = functools.partial(_inproj_call, nw=norm_mix_w, w=w_cat, wgk=wgk.astype(BF16), bgk=bgk,
                             qk=qk, gw=gw, pw=pw, dk=dk)
    q, k, v, vt, g, xp, gk = proj(x, mod_x, tm=512)
    _, kc, _, vtc, _, _, gkc = proj(ctx, mod_c, tm=SUPER)

    gla = _gla_call(q, k, v, vt, gk, g, kc, vtc, gkc, gla_norm_w)
    pool = _pool_call(xp, w_pool[0].astype(BF16), pool_scale)

    n = b * t
    wr = jnp.zeros((d, LANES), F32).at[:, :n_exp].set(w_router[0])
    br = jnp.zeros((1, LANES), F32).at[0, :n_exp].set(b_router[0])
    n_part = n // MOE_PARTS
    tm, tg = 512, 256
    assert n % MOE_PARTS == 0 and n_part % (SC_WORKERS * SC_WINDOW) == 0 and n_part % tm == 0 and n_part < RANK_LIMIT
    assert (n_part * TOP_K) % (SC_WORKERS * 2 * SC_GATHER_WINDOW) == 0
    n_pad = n_part * TOP_K + n_exp * EXPERT_ROWS
    nblk = n_pad // EXPERT_ROWS
    assert nblk <= 8 * LANES
    parts = range(MOE_PARTS)
    routed = [_route_call(gla.reshape(n, gw), pool.reshape(n, pw), x.reshape(n, d), mod_x, w_out[0].astype(BF16),
                          norm_mlp_w, wr, br, tm=tm, n_exp=n_exp,
                          first_tile=p * n_part // tm, n_tiles=n_part // tm) for p in parts]
    plans = [_plan_call(cnt[0, :n_exp], code_t, n_exp=n_exp, rows=EXPERT_ROWS, chunk=min(n_part, 4096))
             for (_, _, code_t, _, cnt) in routed]
    dests = [plan[0][:TOP_K] for plan in plans]
    xs = [_sc_scatter_call(routed[p][1],
                           dests[p].reshape(TOP_K, n_part // SC_WINDOW, SC_WINDOW).transpose(1, 0, 2), n_out=n_pad)
          for p in parts]
    ys = [_expert_call(plans[p][1:], xs[p], w_gu[0], b_gu[0], w_down[0], b_down[0]) for p in parts]
    y4 = [_sc_gather_call(ys[p], dests[p].reshape(SC_WORKERS, -1, SC_GATHER_WINDOW)).reshape(TOP_K, n_part, d // 2)
          for p in parts]
    out = None
    for p in parts:
        out = _combine_call(y4[p], routed[p][3], routed[p][0], mod_x, final_norm_w[None, :], out,
                            n=n, tg=tg, first_tile=p * n_part // tg)
    return out.reshape(b, t, d)
```

```python
import functools

import numpy as np
import jax
import jax.numpy as jnp
from jax import lax
from jax.experimental import pallas as pl
from jax.experimental.pallas import tpu as pltpu
from jax.experimental.pallas import tpu_sc as plsc

F32 = jnp.float32
BF16 = jnp.bfloat16
I32 = jnp.int32
U32 = jnp.uint32

GRID_W = 64
GLA_HEADS = 4
GLA_CHUNK = 64
GATE_NORMALIZER = 16.0
POOL_WINDOWS = (2, 4, 8, 16)
POOL_PAD_GRID_ROWS = 8
TOP_K = 4
RANK_LIMIT = 1 << 20
SWIGLU_LIMIT = 7.0
SWIGLU_ALPHA = 1.702
N_MOD = 6
EPS = 1e-6

LANES = 128
SUPER = 256
HEAD_PAIR_DK = 128
EXPERT_ROWS = 512
EXPERT_ROW_STEP = 128
MOE_PARTS = 2
EXPERT_VMEM_BYTES = 56 * 1024 * 1024
SC_CORES = 2
SC_SUBCORES = 16
SC_WORKERS = SC_CORES * SC_SUBCORES
SC_WINDOW = 32
SC_GATHER_WINDOW = 64


def _dot(a, b):
    return jnp.dot(a, b, preferred_element_type=F32)


def _dot_nt(a, b):
    return lax.dot_general(a, b, (((1,), (1,)), ((), ())), preferred_element_type=F32)


def _split_bf16(x):
    hi = x.astype(BF16)
    lo = (x - hi.astype(F32)).astype(BF16)
    return hi, lo


def _pack_bf16_pairs(x):
    c = x.shape[1] // 2
    lo = lax.bitcast_convert_type(x[:, :c].astype(BF16).astype(F32), U32)
    hi = lax.bitcast_convert_type(x[:, c:].astype(BF16).astype(F32), U32)
    return (lo >> 16) | hi


def _unpack_bf16_pairs(p):
    lo = lax.bitcast_convert_type(p << 16, F32)
    hi = lax.bitcast_convert_type(p & jnp.uint32(0xFFFF0000), F32)
    return lo, hi


def _rmsnorm(x, w):
    var = jnp.mean(x * x, axis=-1, keepdims=True)
    return x * lax.rsqrt(var + EPS) * w


def _mod_kernel(c_ref, w_ref, b_ref, o_ref):
    c = c_ref[...]
    s = c * jax.nn.sigmoid(c)
    o_ref[...] = jnp.dot(s, w_ref[...], precision=lax.Precision.HIGHEST,
                         preferred_element_type=F32) + b_ref[...]


def _mod_call(cc, w_ada, b_ada):
    rows, d = cc.shape
    n = w_ada.shape[1]
    tn = 1024
    return pl.pallas_call(
        _mod_kernel,
        grid=(n // tn,),
        in_specs=[pl.BlockSpec((rows, d), lambda j: (0, 0)),
                  pl.BlockSpec((d, tn), lambda j: (0, j)),
                  pl.BlockSpec((1, tn), lambda j: (0, j))],
        out_specs=pl.BlockSpec((rows, tn), lambda j: (0, j)),
        out_shape=jax.ShapeDtypeStruct((rows, n), F32),
        name="mod",
    )(cc, w_ada, b_ada)


def _inproj_kernel(x_ref, mod_ref, nw_ref, w_ref, wgk_ref, bgk_ref,
                   q_ref, k_ref, v_ref, vt_ref, g_ref, p_ref, gk_ref, *, qk, gw, pw, dk):
    x = x_ref[0]
    m = mod_ref[0]
    hm = (_rmsnorm(x, nw_ref[...]) * (1.0 + m[1:2]) + m[0:1]).astype(BF16)
    p = _dot(hm, w_ref[...])
    vt = p[:, 2 * qk:2 * qk + gw].T
    for s in range(vt_ref.shape[1]):
        vt_ref[0, s] = vt[:, s * SUPER:(s + 1) * SUPER].astype(BF16)
    o = 0
    q_ref[0] = (p[:, o:o + qk] * (dk ** -0.5)).astype(BF16); o += qk
    k_ref[0] = p[:, o:o + qk].astype(BF16); o += qk
    v_ref[0] = p[:, o:o + gw].astype(BF16); o += gw
    g_ref[0] = p[:, o:o + gw].astype(BF16); o += gw
    p_ref[0] = p[:, o:o + pw].astype(BF16); o += pw
    r = p[:, o:o + LANES]
    z = _dot(r.astype(BF16), wgk_ref[...]) + bgk_ref[...]
    gk_ref[0] = (jnp.minimum(z, 0.0) - jnp.log1p(jnp.exp(-jnp.abs(z)))) * (1.0 / GATE_NORMALIZER)


def _inproj_call(x, mod, nw, w, wgk, bgk, *, qk, gw, pw, dk, tm):
    b, t, d = x.shape
    n_in = w.shape[1]
    bs = lambda width: pl.BlockSpec((1, tm, width), lambda i, j: (i, j, 0))
    const = lambda shape: pl.BlockSpec(shape, lambda i, j: (0,) * len(shape))
    per_batch = mod.shape[0] > 1
    sds = jax.ShapeDtypeStruct
    return pl.pallas_call(
        functools.partial(_inproj_kernel, qk=qk, gw=gw, pw=pw, dk=dk),
        grid=(b, t // tm),
        in_specs=[bs(d),
                  pl.BlockSpec((1, N_MOD, d), (lambda i, j: (i, 0, 0)) if per_batch else (lambda i, j: (0, 0, 0))),
                  const((1, d)), const((d, n_in)), const((LANES, 2 * qk)), const((1, 2 * qk))],
        out_specs=[bs(qk), bs(qk), bs(gw),
                   pl.BlockSpec((1, tm // SUPER, gw, SUPER), lambda i, j: (i, j, 0, 0)),
                   bs(gw), bs(pw), bs(2 * qk)],
        out_shape=[sds((b, t, qk), BF16), sds((b, t, qk), BF16), sds((b, t, gw), BF16),
                   sds((b, t // SUPER, gw, SUPER), BF16),
                   sds((b, t, gw), BF16), sds((b, t, pw), BF16), sds((b, t, 2 * qk), F32)],
        compiler_params=pltpu.CompilerParams(dimension_semantics=("arbitrary", "arbitrary")),
        name="inproj",
    )(x, mod, nw, w, wgk, bgk)


def _gla_super(q, k, v, vt, gk, cm, amask, bd_mask, st, fwd, want_out):
    nch = SUPER // GLA_CHUNK
    order = tuple(range(nch)) if fwd else tuple(reversed(range(nch)))
    last_row = GLA_CHUNK - 1 if fwd else 0
    mid_row = GLA_CHUNK // 2 - 1 if fwd else GLA_CHUNK // 2
    hi, lo = _split_bf16(gk)
    bcum = _dot(cm, hi) + _dot(cm, lo)

    def chunk_row(r):
        return jnp.concatenate(
            [jnp.broadcast_to(bcum[c * GLA_CHUNK + r:c * GLA_CHUNK + r + 1, :], (GLA_CHUNK, bcum.shape[1]))
             for c in range(nch)], axis=0)

    chunk_of_row = lax.broadcasted_iota(I32, bcum.shape, 0) // GLA_CHUNK

    def by_chunk(x):
        return jnp.concatenate([jnp.where(chunk_of_row == c, x, 0.0).astype(BF16) for c in range(nch)], axis=1)

    blast = chunk_row(last_row)
    u_all = _dot(vt, by_chunk(k * jnp.exp(blast - bcum)))
    before = [None] * nch
    for c in order:
        before[c] = st
        decay = jnp.exp(bcum[c * GLA_CHUNK + last_row:c * GLA_CHUNK + last_row + 1, :])
        st = st * decay + jnp.where(bd_mask, u_all[:, c * HEAD_PAIR_DK:(c + 1) * HEAD_PAIR_DK], 0.0)
    if not want_out:
        return None, st
    bmid = chunk_row(mid_row)
    qt = q * jnp.exp(bcum - bmid)
    kt = (k * jnp.exp(bmid - bcum)).astype(BF16)
    lane = lax.broadcasted_iota(I32, qt.shape, 1)
    half = HEAD_PAIR_DK // 2
    o_heads = []
    for hh in range(2):
        sel = (lane < half) if hh == 0 else (lane >= half)
        a = _dot_nt(jnp.where(sel, qt, 0.0).astype(BF16), kt)
        a = jnp.where(amask, a, 0.0).astype(BF16)
        o_heads.append(_dot(a, v[:, hh * LANES:(hh + 1) * LANES]))
    qh = (q * jnp.exp(bcum)).astype(BF16)
    o_inter = jnp.concatenate(
        [_dot_nt(qh[c * GLA_CHUNK:(c + 1) * GLA_CHUNK], before[c].astype(BF16)) for c in range(nch)], axis=0)
    return jnp.concatenate(o_heads, axis=1) + o_inter, st


def _gla_kernel(q_ref, k_ref, v_ref, vt_ref, gkf_ref, gkb_ref, g_ref, kc_ref, vtc_ref, gkfc_ref, gkbc_ref,
                nw_ref, cmf_ref, cmb_ref, o_ref, stf_ref, stb_ref, of_ref, ob_ref):
    t = q_ref.shape[1]
    tc = kc_ref.shape[1]
    nsc, nscc = t // SUPER, tc // SUPER
    cmf = cmf_ref[...]
    cmb = cmb_ref[...]
    amask_f = cmf > 0
    amask_b = cmb > 0
    row = lax.broadcasted_iota(I32, (2 * LANES, HEAD_PAIR_DK), 0)
    lane = lax.broadcasted_iota(I32, (2 * LANES, HEAD_PAIR_DK), 1)
    bd_mask = (row < LANES) == (lane < HEAD_PAIR_DK // 2)

    def ctx_state(gk_ref, cm, fwd, j, st):
        rows = pl.ds(j * SUPER, SUPER)
        return _gla_super(None, kc_ref[0, rows, :].astype(F32), None, vtc_ref[0, j], gk_ref[0, rows, :],
                          cm, None, bd_mask, st, fwd, False)[1]

    def latent(gk_ref, cm, amask, fwd, j, st):
        rows = pl.ds(pl.multiple_of(j * SUPER, SUPER), SUPER)
        return _gla_super(q_ref[0, rows, :].astype(F32), k_ref[0, rows, :].astype(F32), v_ref[0, rows, :],
                          vt_ref[0, j], gk_ref[0, rows, :], cm, amask, bd_mask, st, fwd, True)

    st = jnp.zeros(stf_ref.shape, F32)
    for j in range(nscc):
        st = ctx_state(gkfc_ref, cmf, True, j, st)
    stf_ref[...] = st
    st = jnp.zeros(stb_ref.shape, F32)
    for j in reversed(range(nscc)):
        st = ctx_state(gkbc_ref, cmb, False, j, st)
    stb_ref[...] = st

    def scan_body(jj, carry):
        jb = nsc - 1 - jj
        of, stf = latent(gkf_ref, cmf, amask_f, True, jj, stf_ref[...])
        of_ref[pl.ds(pl.multiple_of(jj * SUPER, SUPER), SUPER), :] = of
        stf_ref[...] = stf
        ob, stb = latent(gkb_ref, cmb, amask_b, False, jb, stb_ref[...])
        ob_ref[pl.ds(pl.multiple_of(jb * SUPER, SUPER), SUPER), :] = ob
        stb_ref[...] = stb
        return carry

    lax.fori_loop(0, nsc, scan_body, 0, unroll=2)

    nw = nw_ref[...]

    def out_body(j, carry):
        rows = pl.ds(pl.multiple_of(j * SUPER, SUPER), SUPER)
        o = of_ref[rows, :] + ob_ref[rows, :]
        g = g_ref[0, rows, :].astype(F32)
        gate = g * jax.nn.sigmoid(g)
        for hh in range(2):
            oh = o[:, hh * LANES:(hh + 1) * LANES]
            on = oh * lax.rsqrt(jnp.mean(oh * oh, axis=-1, keepdims=True) + EPS) * nw
            o_ref[0, rows, hh * LANES:(hh + 1) * LANES] = (on * gate[:, hh * LANES:(hh + 1) * LANES]).astype(BF16)
        return carry

    lax.fori_loop(0, nsc, out_body, 0)


def _gla_masks():
    i = np.arange(SUPER)
    same = (i[:, None] // GLA_CHUNK) == (i[None, :] // GLA_CHUNK)
    fwd = same & (i[None, :] <= i[:, None])
    bwd = same & (i[None, :] >= i[:, None])
    return jnp.asarray(fwd, BF16), jnp.asarray(bwd, BF16)


def _gla_call(q, k, v, vt, gk, g, kc, vtc, gkc, nw):
    b, t, qk = q.shape
    tc = kc.shape[1]
    npair = qk // HEAD_PAIR_DK
    cmf, cmb = _gla_masks()
    lat = lambda width, off: pl.BlockSpec((1, t, width), lambda i, j: (i, 0, j + off))
    ctx = lambda width, off: pl.BlockSpec((1, tc, width), lambda i, j: (i, 0, j + off))
    tr = lambda n_groups: pl.BlockSpec((1, n_groups, 2 * LANES, SUPER), lambda i, j: (i, 0, j, 0))
    const = lambda shape: pl.BlockSpec(shape, lambda i, j: (0,) * len(shape))
    return pl.pallas_call(
        _gla_kernel,
        grid=(b, npair),
        in_specs=[lat(HEAD_PAIR_DK, 0), lat(HEAD_PAIR_DK, 0), lat(2 * LANES, 0), tr(t // SUPER),
                  lat(HEAD_PAIR_DK, 0), lat(HEAD_PAIR_DK, npair), lat(2 * LANES, 0),
                  ctx(HEAD_PAIR_DK, 0), tr(tc // SUPER), ctx(HEAD_PAIR_DK, 0), ctx(HEAD_PAIR_DK, npair),
                  const((1, LANES)), const(cmf.shape), const(cmb.shape)],
        out_specs=lat(2 * LANES, 0),
        out_shape=jax.ShapeDtypeStruct((b, t, v.shape[2]), BF16),
        scratch_shapes=[pltpu.VMEM((2 * LANES, HEAD_PAIR_DK), F32), pltpu.VMEM((2 * LANES, HEAD_PAIR_DK), F32),
                        pltpu.VMEM((t, 2 * LANES), F32), pltpu.VMEM((t, 2 * LANES), F32)],
        compiler_params=pltpu.CompilerParams(dimension_semantics=("arbitrary", "arbitrary")),
        name="gla",
    )(q, k, v, vt, gk, gk, g, kc, vtc, gkc, gkc, nw, cmf, cmb)


def _pool_kernel(x_ref, cm_ref, cnt_ref, wp_ref, ps_ref, o_ref, a_ref, b_ref):
    t = x_ref.shape[1]
    pad = POOL_PAD_GRID_ROWS * GRID_W
    total = t + 2 * pad
    for gi, w in enumerate(POOL_WINDOWS):
        lo = w // 2
        cols = slice(gi * LANES, (gi + 1) * LANES)
        cmat = cm_ref[gi]
        a_ref[0:pad, :] = jnp.zeros((pad, LANES), F32)
        a_ref[pad + t:total, :] = jnp.zeros((pad, LANES), F32)
        for blk in range(t // SUPER):
            rs = slice(blk * SUPER, (blk + 1) * SUPER)
            a_ref[pad + blk * SUPER:pad + (blk + 1) * SUPER, :] = _dot(cmat, x_ref[0, rs, cols])
        src, dst = a_ref, b_ref
        m = 1
        while m < w:
            sh = m * GRID_W
            dst[0:total - sh, :] = src[0:total - sh, :] + src[sh:total, :]
            src, dst = dst, src
            m *= 2
        first = pad - lo * GRID_W
        pooled = src[first:first + t, :] / cnt_ref[gi] - x_ref[0, :, cols].astype(F32)
        yp = _dot(pooled.astype(BF16), wp_ref[gi]) * ps_ref[:, cols]
        o_ref[0, :, cols] = yp.astype(BF16)


def _pool_col_mats():
    i = np.arange(SUPER)
    same_row = (i[:, None] // GRID_W) == (i[None, :] // GRID_W)
    d = i[None, :] - i[:, None]
    mats = []
    for w in POOL_WINDOWS:
        lo = w // 2
        hi = w - 1 - lo
        mats.append(same_row & (d >= -lo) & (d <= hi))
    return jnp.asarray(np.stack(mats), BF16)


def _pool_counts(t):
    rows = t // GRID_W
    r = np.arange(t) // GRID_W
    c = np.arange(t) % GRID_W
    out = []
    for w in POOL_WINDOWS:
        lo = w // 2
        hi = w - 1 - lo
        cnt_r = np.minimum(r + hi + 1, rows) - np.maximum(r - lo, 0)
        cnt_c = np.minimum(c + hi + 1, GRID_W) - np.maximum(c - lo, 0)
        out.append(np.broadcast_to((cnt_r * cnt_c).astype(np.float32)[:, None], (t, LANES)))
    return jnp.asarray(np.stack(out))


def _pool_call(xp, w_pool, pool_scale):
    b, t, pw = xp.shape
    ng = len(POOL_WINDOWS)
    assert max(POOL_WINDOWS) // 2 <= POOL_PAD_GRID_ROWS and t % GRID_W == 0
    cm = _pool_col_mats()
    cnt = _pool_counts(t)
    staged = t + 2 * POOL_PAD_GRID_ROWS * GRID_W
    const = lambda shape: pl.BlockSpec(shape, lambda i: (0,) * len(shape))
    return pl.pallas_call(
        _pool_kernel,
        grid=(b,),
        in_specs=[pl.BlockSpec((1, t, pw), lambda i: (i, 0, 0)),
                  const(cm.shape), const(cnt.shape), const((ng, LANES, LANES)), const((1, pw))],
        out_specs=pl.BlockSpec((1, t, pw), lambda i: (i, 0, 0)),
        out_shape=jax.ShapeDtypeStruct((b, t, pw), BF16),
        scratch_shapes=[pltpu.VMEM((staged, LANES), F32), pltpu.VMEM((staged, LANES), F32)],
        compiler_params=pltpu.CompilerParams(dimension_semantics=("arbitrary",)),
        name="pool",
    )(xp, cm, cnt, w_pool, pool_scale)


def _route_kernel(gla_ref, pool_ref, x_ref, mod_ref, wo_ref, nw_ref, wr_ref, br_ref, lt_ref,
                  h_ref, xt_ref, code_ref, wt_ref, cnt_ref, run_ref, wr2_ref, *, gw, n_exp):
    i = pl.program_id(0)

    @pl.when(i == 0)
    def _():
        run_ref[...] = jnp.zeros_like(run_ref)
        wh, wl = _split_bf16(wr_ref[...])
        wr2_ref[:, :LANES] = wh
        wr2_ref[:, LANES:] = wl

    m = mod_ref[0]
    acc = _dot(gla_ref[...], wo_ref[0:gw, :]) + _dot(pool_ref[...], wo_ref[gw:, :])
    h = x_ref[...] + m[2:3] * acc
    h_ref[...] = h
    xt = _rmsnorm(h, nw_ref[...]) * (1.0 + m[4:5]) + m[3:4]
    xt_ref[...] = _pack_bf16_pairs(xt)
    xh, xl = _split_bf16(xt)
    wr2 = wr2_ref[...]
    t1 = _dot(xh, wr2)
    logits = t1[:, :LANES] + t1[:, LANES:] + _dot(xl, wr2[:, :LANES]) + br_ref[...]
    lane = lax.broadcasted_iota(I32, logits.shape, 1)
    neg = jnp.float32(-jnp.inf)
    logits = jnp.where(lane < n_exp, logits, neg)
    vals, hots = [], []
    e_out = jnp.zeros(logits.shape, I32)
    for j in range(TOP_K):
        mx = jnp.max(logits, axis=-1, keepdims=True)
        idx = jnp.min(jnp.where(logits == mx, lane, LANES), axis=-1, keepdims=True)
        hot = lane == idx
        vals.append(mx)
        hots.append(hot)
        e_out = jnp.where(lane == j, idx, e_out)
        logits = jnp.where(hot, neg, logits)
    ex = [jnp.exp(v - vals[0]) for v in vals]
    den = ex[0] + ex[1] + ex[2] + ex[3]
    w_out = jnp.zeros(logits.shape, F32)
    for j in range(TOP_K):
        w_out = jnp.where(lane == j, ex[j] / den, w_out)
    osum = jnp.where(hots[0] | hots[1] | hots[2] | hots[3], 1.0, 0.0)
    before = _dot(lt_ref[...], osum.astype(BF16)) + run_ref[0:1, :]
    rk_out = jnp.zeros(logits.shape, I32)
    for j in range(TOP_K):
        rj = jnp.sum(jnp.where(hots[j], before, 0.0), axis=-1, keepdims=True)
        rk_out = jnp.where(lane == j, rj.astype(I32), rk_out)
    run = run_ref[0:1, :] + jnp.sum(osum, axis=0, keepdims=True)
    run_ref[...] = jnp.broadcast_to(run, run_ref.shape)
    code = e_out * RANK_LIMIT + rk_out
    code_ref[...] = code.T[:code_ref.shape[0], :]
    wt_ref[...] = w_out
    cnt_ref[...] = jnp.broadcast_to(run, cnt_ref.shape).astype(I32)


def _route_call(gla, pool, x, mod, w_out, nw, wr, br, *, tm, n_exp, first_tile, n_tiles):
    n_all, d = x.shape
    n = n_tiles * tm
    gw = gla.shape[1]
    t_per_b = n_all // mod.shape[0]
    lt = jnp.asarray(np.tril(np.ones((tm, tm), np.float32), -1), BF16)
    row_in = lambda width: pl.BlockSpec((tm, width), lambda i: (i + first_tile, 0))
    row = lambda width: pl.BlockSpec((tm, width), lambda i: (i, 0))
    const = lambda shape: pl.BlockSpec(shape, lambda i: (0,) * len(shape))
    sds = jax.ShapeDtypeStruct
    return pl.pallas_call(
        functools.partial(_route_kernel, gw=gw, n_exp=n_exp),
        grid=(n_tiles,),
        in_specs=[row_in(gw), row_in(pool.shape[1]), row_in(d),
                  pl.BlockSpec((1, N_MOD, d), lambda i: ((i + first_tile) * tm // t_per_b, 0, 0)),
                  const(w_out.shape), const((1, d)), const(wr.shape), const((1, LANES)), const((tm, tm))],
        out_specs=[row(d), row(d // 2), pl.BlockSpec((8, tm), lambda i: (0, i)), row(LANES), const((8, LANES))],
        out_shape=[sds((n, d), F32), sds((n, d // 2), U32), sds((8, n), I32),
                   sds((n, LANES), F32), sds((8, LANES), I32)],
        scratch_shapes=[pltpu.VMEM((8, LANES), F32), pltpu.VMEM((d, 2 * LANES), BF16)],
        compiler_params=pltpu.CompilerParams(dimension_semantics=("arbitrary",)),
        name="route",
    )(gla, pool, x, mod, w_out, nw, wr, br, lt)


def _plan_kernel(cnt_ref, code_ref, dest_ref, be_ref, nv_ref, first_ref, next_ref, slot_ref, nu_ref, start_ref,
                 *, n_exp, rows):
    @pl.when(pl.program_id(0) == 0)
    def _():
        blk = (lax.broadcasted_iota(I32, be_ref.shape, 0) * LANES + lax.broadcasted_iota(I32, be_ref.shape, 1))
        blk_row0 = blk * rows
        nxt_e = [None] * n_exp
        nxt = jnp.int32(-1)
        for e in reversed(range(n_exp)):
            nxt_e[e] = nxt
            nxt = jnp.where(cnt_ref[e] > 0, e, nxt)
        zeros = jnp.zeros(be_ref.shape, I32)
        be, end_valid, first, nxt_blk, slot = zeros, zeros, zeros, zeros - 1, zeros
        acc = jnp.int32(0)
        last_e = jnp.int32(0)
        ordinal = jnp.int32(0)
        for e in range(n_exp):
            c = cnt_ref[e]
            start_ref[e] = acc
            in_e = (blk_row0 >= acc) & (c > 0)
            end_valid = jnp.where(in_e, acc + c, end_valid)
            be = jnp.where(in_e, e, be)
            first = jnp.where(in_e, (blk_row0 == acc).astype(I32), first)
            nxt_blk = jnp.where(in_e, nxt_e[e], nxt_blk)
            slot = jnp.where(in_e, ordinal % 2, slot)
            acc = acc + (c + rows - 1) // rows * rows
            last_e = jnp.where(c > 0, e, last_e)
            ordinal = ordinal + (c > 0).astype(I32)
        n_used = acc // rows
        nu_ref[0] = n_used
        be_ref[...] = jnp.where(blk < n_used, be, last_e)
        nv_ref[...] = jnp.clip(end_valid - blk_row0, 0, rows)
        first_ref[...] = first
        next_ref[...] = nxt_blk
        slot_ref[...] = slot

    code = code_ref[...]
    e_vec = code // RANK_LIMIT
    dest = code % RANK_LIMIT
    for e in range(n_exp):
        dest = dest + jnp.where(e_vec == e, start_ref[e], 0)
    dest_ref[...] = dest


def _plan_call(counts, code_t, *, n_exp, rows, chunk):
    n = code_t.shape[1]
    sds = jax.ShapeDtypeStruct
    smem = pltpu.SMEM
    return pl.pallas_call(
        functools.partial(_plan_kernel, n_exp=n_exp, rows=rows),
        grid=(n // chunk,),
        in_specs=[pl.BlockSpec(memory_space=smem), pl.BlockSpec((8, chunk), lambda i: (0, i))],
        out_specs=[pl.BlockSpec((8, chunk), lambda i: (0, i))]
        + [pl.BlockSpec((8, LANES), lambda i: (0, 0))] * 5 + [pl.BlockSpec(memory_space=smem)],
        out_shape=[sds((8, n), I32)] + [sds((8, LANES), I32)] * 5 + [sds((1,), I32)],
        scratch_shapes=[pltpu.SMEM((n_exp,), I32)],
        compiler_params=pltpu.CompilerParams(dimension_semantics=("arbitrary",)),
        name="plan",
    )(counts, code_t)


def _sc_worker_id():
    return lax.axis_index("s") * SC_CORES + lax.axis_index("c")


def _sc_scatter_call(x, idx3, *, n_out):
    n, d = x.shape
    n_win_total, k, w = idx3.shape
    n_win = n_win_total // SC_WORKERS
    mesh = plsc.VectorSubcoreMesh(core_axis_name="c", subcore_axis_name="s")

    @functools.partial(
        pl.kernel, mesh=mesh,
        out_type=jax.ShapeDtypeStruct((n_out, d), x.dtype),
        scratch_types=[pltpu.VMEM((k, w), I32), pltpu.VMEM((w, d), x.dtype), pltpu.SemaphoreType.DMA],
        name="sc_dispatch",
    )
    def kern(x_hbm, idx_hbm, out_hbm, idx_v, rows_v, sem):
        wid = _sc_worker_id()

        @pl.loop(0, n_win)
        def _(i):
            win = wid * n_win + i
            pltpu.sync_copy(idx_hbm.at[win], idx_v)
            pltpu.sync_copy(x_hbm.at[pl.ds(win * w, w)], rows_v)
            for j in range(k):
                pltpu.async_copy(rows_v, out_hbm.at[idx_v.at[j]], sem).wait()

    return kern(x, idx3)


def _sc_gather_call(table, idx3):
    n_workers, n_win, w = idx3.shape
    d = table.shape[1]
    assert n_workers == SC_WORKERS and n_win % 2 == 0
    mesh = plsc.VectorSubcoreMesh(core_axis_name="c", subcore_axis_name="s")

    @functools.partial(
        pl.kernel, mesh=mesh,
        out_type=jax.ShapeDtypeStruct((n_workers * n_win * w, d), table.dtype),
        scratch_types=[pltpu.VMEM((n_win, w), I32), pltpu.VMEM((2, w, d), table.dtype),
                       pltpu.SemaphoreType.DMA((2,)), pltpu.SemaphoreType.DMA((2,))],
        name="sc_gather",
    )
    def kern(table_hbm, idx_hbm, out_hbm, idx_v, rows_v, gsem, osem):
        wid = _sc_worker_id()
        base = wid * n_win
        pltpu.sync_copy(idx_hbm.at[wid], idx_v)

        def gather(wi, b):
            return pltpu.make_async_copy(table_hbm.at[idx_v.at[wi]], rows_v.at[b], gsem.at[b])

        def put(wi, b):
            return pltpu.make_async_copy(rows_v.at[b], out_hbm.at[pl.ds((base + wi) * w, w)], osem.at[b])

        gather(0, 0).start()

        @pl.loop(0, n_win, step=2)
        def _(i):
            for b in range(2):
                wi = i + b

                @pl.when(wi + 1 < n_win)
                def _():
                    @pl.when(wi >= 1)
                    def _():
                        put(wi - 1, 1 - b).wait()
                    gather(wi + 1, 1 - b).start()

                gather(wi, b).wait()
                put(wi, b).start()

        put(n_win - 2, 0).wait()
        put(n_win - 1, 1).wait()

    return kern(table, idx3)


def _expert_kernel(be_ref, nu_ref, nv_ref, first_ref, next_ref, slot_ref,
                   x_ref, wgu_hbm, bgu_ref, wd_hbm, bd_ref, y_ref,
                   wgu_f32_ref, wd_f32_ref, wgu_bf_ref, wd_bf_ref, sem, *, d_ff):
    i = pl.program_id(0)
    slot = slot_ref[i]

    def fetch(e, s):
        return (pltpu.make_async_copy(wgu_hbm.at[e], wgu_f32_ref.at[s], sem.at[0, s]),
                pltpu.make_async_copy(wd_hbm.at[e], wd_f32_ref.at[s], sem.at[1, s]))

    @pl.when(i == 0)
    def _():
        for cp in fetch(be_ref[0], slot):
            cp.start()

    @pl.when(first_ref[i] == 1)
    def _():
        for cp in fetch(be_ref[i], slot):
            cp.wait()

        @pl.when(next_ref[i] >= 0)
        def _():
            for cp in fetch(next_ref[i], 1 - slot):
                cp.start()

        wgu_bf_ref[...] = wgu_f32_ref[slot].astype(BF16)
        wd_bf_ref[...] = wd_f32_ref[slot].astype(BF16)

    n_valid = nv_ref[i]
    n_rows = x_ref.shape[0]

    def expert_rows(rows):
        row = lax.broadcasted_iota(I32, (rows, x_ref.shape[1]), 0)
        lo, hi = _unpack_bf16_pairs(jnp.where(row < n_valid, x_ref[0:rows, :], jnp.uint32(0)))
        xb = jnp.concatenate([lo, hi], axis=1).astype(BF16)
        gu = _dot(xb, wgu_bf_ref[...]) + bgu_ref[0]
        gate = jnp.minimum(gu[:, :d_ff], SWIGLU_LIMIT)
        up = jnp.clip(gu[:, d_ff:], -SWIGLU_LIMIT, SWIGLU_LIMIT)
        act = (up + 1.0) * gate * jax.nn.sigmoid(SWIGLU_ALPHA * gate)
        y_ref[0:rows, :] = _pack_bf16_pairs(_dot(act.astype(BF16), wd_bf_ref[...]) + bd_ref[0])

    in_use = i < nu_ref[0]

    for rows in range(EXPERT_ROW_STEP, n_rows + 1, EXPERT_ROW_STEP):
        @pl.when(in_use & (n_valid > rows - EXPERT_ROW_STEP) & (n_valid <= rows))
        def _(rows=rows):
            expert_rows(rows)
            if rows < n_rows:
                y_ref[rows:, :] = jnp.zeros((n_rows - rows, y_ref.shape[1]), y_ref.dtype)


def _expert_call(plan, xs, w_gu, b_gu, w_down, b_down):
    block_e, n_valid, first, nxt, slot, n_used = plan
    n_pad = xs.shape[0]
    n_exp, d, two_ff = w_gu.shape
    d_ff = two_ff // 2
    nblk = n_pad // EXPERT_ROWS
    rows = lambda i, be, nu, *_: (jnp.minimum(i, nu[0] - 1), 0)
    per_e = lambda i, be, *_: (be[i], 0, 0)
    grid_spec = pltpu.PrefetchScalarGridSpec(
        num_scalar_prefetch=6,
        grid=(nblk,),
        in_specs=[pl.BlockSpec((EXPERT_ROWS, d // 2), rows),
                  pl.BlockSpec(memory_space=pl.ANY), pl.BlockSpec((1, 1, two_ff), per_e),
                  pl.BlockSpec(memory_space=pl.ANY), pl.BlockSpec((1, 1, d), per_e)],
        out_specs=pl.BlockSpec((EXPERT_ROWS, d // 2), rows),
        scratch_shapes=[pltpu.VMEM((2, d, two_ff), F32), pltpu.VMEM((2, d_ff, d), F32),
                        pltpu.VMEM((d, two_ff), BF16), pltpu.VMEM((d_ff, d), BF16),
                        pltpu.SemaphoreType.DMA((2, 2))],
    )
    flat = lambda a: a.reshape(-1)[:nblk]
    return pl.pallas_call(
        functools.partial(_expert_kernel, d_ff=d_ff),
        grid_spec=grid_spec,
        out_shape=jax.ShapeDtypeStruct((n_pad, d // 2), U32),
        compiler_params=pltpu.CompilerParams(dimension_semantics=("arbitrary",),
                                             vmem_limit_bytes=EXPERT_VMEM_BYTES),
        name="experts",
    )(flat(block_e), n_used, flat(n_valid), flat(first), flat(nxt), flat(slot),
      xs, w_gu, b_gu.reshape(n_exp, 1, two_ff), w_down, b_down.reshape(n_exp, 1, d))


def _combine_kernel(y4_ref, wt_ref, h_ref, mod_ref, fw_ref, o_ref):
    wt = wt_ref[...]
    acc_lo, acc_hi = None, None
    for j in range(TOP_K):
        lo, hi = _unpack_bf16_pairs(y4_ref[j])
        w = wt[:, j:j + 1]
        acc_lo = w * lo if j == 0 else acc_lo + w * lo
        acc_hi = w * hi if j == 0 else acc_hi + w * hi
    acc = jnp.concatenate([acc_lo, acc_hi], axis=1)
    m = mod_ref[0]
    o_ref[...] = _rmsnorm(h_ref[...] + m[5:6] * acc, fw_ref[...])


def _combine_call(y4, wts, h, mod, fw, prev_out, *, n, tg, first_tile):
    d = h.shape[1]
    t_per_b = n // mod.shape[0]
    part = lambda width: pl.BlockSpec((tg, width), lambda i: (i, 0))
    row = lambda width: pl.BlockSpec((tg, width), lambda i: (i + first_tile, 0))
    in_specs = [pl.BlockSpec((TOP_K, tg, d // 2), lambda i: (0, i, 0)),
                part(LANES), part(d),
                pl.BlockSpec((1, N_MOD, d), lambda i: ((i + first_tile) * tg // t_per_b, 0, 0)),
                pl.BlockSpec((1, d), lambda i: (0, 0))]
    args = [y4, wts, h, mod, fw]
    kern = _combine_kernel
    aliases = {}
    if prev_out is not None:
        in_specs.append(pl.BlockSpec(memory_space=pl.ANY))
        args.append(prev_out)
        kern = lambda y4_ref, wt_ref, h_ref, mod_ref, fw_ref, prev_ref, o_ref: _combine_kernel(
            y4_ref, wt_ref, h_ref, mod_ref, fw_ref, o_ref)
        aliases = {len(args) - 1: 0}
    return pl.pallas_call(
        kern,
        grid=(y4.shape[1] // tg,),
        in_specs=in_specs,
        out_specs=row(d),
        out_shape=jax.ShapeDtypeStruct((n, d), F32),
        input_output_aliases=aliases,
        compiler_params=pltpu.CompilerParams(dimension_semantics=("arbitrary",)),
        name="combine",
    )(*args)


def kernel(x, c, ctx, c_ctx, w_ada, b_ada, norm_mix_w, norm_mlp_w, w_in, w_gk_f, b_gk_f, w_gk_b, b_gk_b,
           gla_norm_w, w_pool, pool_scale, w_out, w_router, b_router, w_gu, b_gu, w_down, b_down,
           final_norm_w):
    b, t, d = x.shape
    assert w_ada.shape[0] == 1, "single-layer trunk"
    n_exp = w_router.shape[2]
    rank = w_gk_f.shape[1]
    qk = w_gk_f.shape[2]
    dk = qk // GLA_HEADS
    gw = GLA_HEADS * gla_norm_w.shape[1]
    pw = w_pool.shape[1] * w_pool.shape[2]
    assert w_in.shape[2] == 2 * qk + 2 * gw + 2 * rank + pw and 2 * rank <= LANES
    assert t % SUPER == 0 and ctx.shape[1] % SUPER == 0 and n_exp <= LANES

    rows = -(-(b + 1) // 8) * 8
    cc = jnp.zeros((rows, d), F32).at[:b].set(c).at[b].set(c_ctx)
    mod = _mod_call(cc, w_ada[0], b_ada)
    mod_x = mod[:b].reshape(b, N_MOD, d)
    mod_c = mod[b:b + 1].reshape(1, N_MOD, d)

    wi = w_in[0]
    o_r = 2 * qk + 2 * gw
    w_cat = jnp.concatenate([wi[:, :o_r], wi[:, o_r + 2 * rank:], wi[:, o_r:o_r + 2 * rank],
                             jnp.zeros((d, LANES - 2 * rank), F32)], axis=1).astype(BF16)
    wgk = jnp.zeros((LANES, 2 * qk), F32).at[:rank, :qk].set(w_gk_f[0]).at[rank:2 * rank, qk:].set(w_gk_b[0])
    bgk = jnp.concatenate([b_gk_f[0], b_gk_b[0]])[None, :]
    proj = functools.partial(_inproj_call, nw=norm_mix_w, w=w_cat, wgk=wgk.astype(BF16), bgk=bgk,
                             qk=qk, gw=gw, pw=pw, dk=dk)
    q, k, v, vt, g, xp, gk = proj(x, mod_x, tm=512)
    _, kc, _, vtc, _, _, gkc = proj(ctx, mod_c, tm=SUPER)

    gla = _gla_call(q, k, v, vt, gk, g, kc, vtc, gkc, gla_norm_w)
    pool = _pool_call(xp, w_pool[0].astype(BF16), pool_scale)

    n = b * t
    wr = jnp.zeros((d, LANES), F32).at[:, :n_exp].set(w_router[0])
    br = jnp.zeros((1, LANES), F32).at[0, :n_exp].set(b_router[0])
    n_part = n // MOE_PARTS
    tm, tg = 512, 256
    assert n % MOE_PARTS == 0 and n_part % (SC_WORKERS * SC_WINDOW) == 0 and n_part % tm == 0 and n_part < RANK_LIMIT
    assert (n_part * TOP_K) % (SC_WORKERS * 2 * SC_GATHER_WINDOW) == 0
    n_pad = n_part * TOP_K + n_exp * EXPERT_ROWS
    nblk = n_pad // EXPERT_ROWS
    assert nblk <= 8 * LANES
    parts = range(MOE_PARTS)
    routed = [_route_call(gla.reshape(n, gw), pool.reshape(n, pw), x.reshape(n, d), mod_x, w_out[0].astype(BF16),
                          norm_mlp_w, wr, br, tm=tm, n_exp=n_exp,
                          first_tile=p * n_part // tm, n_tiles=n_part // tm) for p in parts]
    plans = [_plan_call(cnt[0, :n_exp], code_t, n_exp=n_exp, rows=EXPERT_ROWS, chunk=min(n_part, 4096))
             for (_, _, code_t, _, cnt) in routed]
    dests = [plan[0][:TOP_K] for plan in plans]
    xs = [_sc_scatter_call(routed[p][1],
                           dests[p].reshape(TOP_K, n_part // SC_WINDOW, SC_WINDOW).transpose(1, 0, 2), n_out=n_pad)
          for p in parts]
    ys = [_expert_call(plans[p][1:], xs[p], w_gu[0], b_gu[0], w_down[0], b_down[0]) for p in parts]
    y4 = [_sc_gather_call(ys[p], dests[p].reshape(SC_WORKERS, -1, SC_GATHER_WINDOW)).reshape(TOP_K, n_part, d // 2)
          for p in parts]
    out = None
    for p in parts:
        out = _combine_call(y4[p], routed[p][3], routed[p][0], mod_x, final_norm_w[None, :], out,
                            n=n, tg=tg, first_tile=p * n_part // tg)
    return out.reshape(b, t, d)
```

```python
import functools

import numpy as np
import jax
import jax.numpy as jnp
from jax import lax
from jax.experimental import pallas as pl
from jax.experimental.pallas import tpu as pltpu
from jax.experimental.pallas import tpu_sc as plsc

F32 = jnp.float32
BF16 = jnp.bfloat16
I32 = jnp.int32
U32 = jnp.uint32

GRID_W = 64
GLA_HEADS = 4
GLA_CHUNK = 64
GATE_NORMALIZER = 16.0
POOL_WINDOWS = (2, 4, 8, 16)
POOL_PAD_GRID_ROWS = 8
TOP_K = 4
RANK_LIMIT = 1 << 20
SWIGLU_LIMIT = 7.0
SWIGLU_ALPHA = 1.702
N_MOD = 6
EPS = 1e-6

LANES = 128
SUPER = 256
HEAD_PAIR_DK = 128
EXPERT_ROWS = 512
EXPERT_ROW_STEP = 128
MOE_PARTS = 2
EXPERT_VMEM_BYTES = 56 * 1024 * 1024
SC_CORES = 2
SC_SUBCORES = 16
SC_WORKERS = SC_CORES * SC_SUBCORES
SC_WINDOW = 32
SC_GATHER_WINDOW = 64


def _dot(a, b):
    return jnp.dot(a, b, preferred_element_type=F32)


def _dot_nt(a, b):
    return lax.dot_general(a, b, (((1,), (1,)), ((), ())), preferred_element_type=F32)


def _split_bf16(x):
    hi = x.astype(BF16)
    lo = (x - hi.astype(F32)).astype(BF16)
    return hi, lo


def _pack_bf16_pairs(x):
    c = x.shape[1] // 2
    lo = lax.bitcast_convert_type(x[:, :c].astype(BF16).astype(F32), U32)
    hi = lax.bitcast_convert_type(x[:, c:].astype(BF16).astype(F32), U32)
    return (lo >> 16) | hi


def _unpack_bf16_pairs(p):
    lo = lax.bitcast_convert_type(p << 16, F32)
    hi = lax.bitcast_convert_type(p & jnp.uint32(0xFFFF0000), F32)
    return lo, hi


def _rmsnorm(x, w):
    var = jnp.mean(x * x, axis=-1, keepdims=True)
    return x * lax.rsqrt(var + EPS) * w


def _mod_kernel(c_ref, w_ref, b_ref, o_ref):
    c = c_ref[...]
    s = c * jax.nn.sigmoid(c)
    o_ref[...] = jnp.dot(s, w_ref[...], precision=lax.Precision.HIGHEST,
                         preferred_element_type=F32) + b_ref[...]


def _mod_call(cc, w_ada, b_ada):
    rows, d = cc.shape
    n = w_ada.shape[1]
    tn = 1024
    return pl.pallas_call(
        _mod_kernel,
        grid=(n // tn,),
        in_specs=[pl.BlockSpec((rows, d), lambda j: (0, 0)),
                  pl.BlockSpec((d, tn), lambda j: (0, j)),
                  pl.BlockSpec((1, tn), lambda j: (0, j))],
        out_specs=pl.BlockSpec((rows, tn), lambda j: (0, j)),
        out_shape=jax.ShapeDtypeStruct((rows, n), F32),
        name="mod",
    )(cc, w_ada, b_ada)


def _inproj_kernel(x_ref, mod_ref, nw_ref, *refs, qk, gw, pw, dk, state_only):
    x = x_ref[0]
    m = mod_ref[0]
    hm = (_rmsnorm(x, nw_ref[...]) * (1.0 + m[1:2]) + m[0:1]).astype(BF16)
    if state_only:
        wk_ref, wv_ref, wr_ref, wgk_ref, bgk_ref, k_ref, vt_ref, gk_ref = refs
        k, v, r = _dot(hm, wk_ref[...]), _dot(hm, wv_ref[...]), _dot(hm, wr_ref[...])
    else:
        w_ref, wgk_ref, bgk_ref, q_ref, k_ref, v_ref, vt_ref, g_ref, p_ref, gk_ref = refs
        p = _dot(hm, w_ref[...])
        o = 0
        q_ref[0] = (p[:, o:o + qk] * (dk ** -0.5)).astype(BF16); o += qk
        k = p[:, o:o + qk]; o += qk
        v = p[:, o:o + gw]; o += gw
        v_ref[0] = v.astype(BF16)
        g_ref[0] = p[:, o:o + gw].astype(BF16); o += gw
        p_ref[0] = p[:, o:o + pw].astype(BF16); o += pw
        r = p[:, o:o + LANES]
    k_ref[0] = k.astype(BF16)
    vt = v.T
    for s in range(vt_ref.shape[1]):
        vt_ref[0, s] = vt[:, s * SUPER:(s + 1) * SUPER].astype(BF16)
    z = _dot(r.astype(BF16), wgk_ref[...]) + bgk_ref[...]
    gk_ref[0] = (jnp.minimum(z, 0.0) - jnp.log1p(jnp.exp(-jnp.abs(z)))) * (1.0 / GATE_NORMALIZER)


def _inproj_call(x, mod, nw, w, wgk, bgk, *, qk, gw, pw, dk, tm, state_only):
    b, t, d = x.shape
    n_in = w.shape[1]
    bs = lambda width: pl.BlockSpec((1, tm, width), lambda i, j: (i, j, 0))
    const = lambda shape: pl.BlockSpec(shape, lambda i, j: (0,) * len(shape))
    per_batch = mod.shape[0] > 1
    sds = jax.ShapeDtypeStruct
    vt_spec = pl.BlockSpec((1, tm // SUPER, gw, SUPER), lambda i, j: (i, j, 0, 0))
    vt_shape = sds((b, t // SUPER, gw, SUPER), BF16)
    if state_only:
        assert (2 * qk) % gw == 0 and (n_in - LANES) % LANES == 0
        w_specs = [pl.BlockSpec((d, qk), lambda i, j: (0, 1)), pl.BlockSpec((d, gw), lambda i, j: (0, 2 * qk // gw)),
                   pl.BlockSpec((d, LANES), lambda i, j: (0, n_in // LANES - 1))]
        w_args = [w, w, w]
        out_specs = [bs(qk), vt_spec, bs(2 * qk)]
        out_shape = [sds((b, t, qk), BF16), vt_shape, sds((b, t, 2 * qk), F32)]
    else:
        w_specs, w_args = [const((d, n_in))], [w]
        out_specs = [bs(qk), bs(qk), bs(gw), vt_spec, bs(gw), bs(pw), bs(2 * qk)]
        out_shape = [sds((b, t, qk), BF16), sds((b, t, qk), BF16), sds((b, t, gw), BF16), vt_shape,
                     sds((b, t, gw), BF16), sds((b, t, pw), BF16), sds((b, t, 2 * qk), F32)]
    return pl.pallas_call(
        functools.partial(_inproj_kernel, qk=qk, gw=gw, pw=pw, dk=dk, state_only=state_only),
        grid=(b, t // tm),
        in_specs=[bs(d),
                  pl.BlockSpec((1, N_MOD, d), (lambda i, j: (i, 0, 0)) if per_batch else (lambda i, j: (0, 0, 0))),
                  const((1, d))] + w_specs + [const((LANES, 2 * qk)), const((1, 2 * qk))],
        out_specs=out_specs,
        out_shape=out_shape,
        compiler_params=pltpu.CompilerParams(dimension_semantics=("arbitrary", "arbitrary")),
        name="inproj",
    )(x, mod, nw, *w_args, wgk, bgk)


def _gla_super(q, k, v, vt, gk, cm, amask, bd_mask, st, fwd, want_out):
    nch = SUPER // GLA_CHUNK
    order = tuple(range(nch)) if fwd else tuple(reversed(range(nch)))
    last_row = GLA_CHUNK - 1 if fwd else 0
    mid_row = GLA_CHUNK // 2 - 1 if fwd else GLA_CHUNK // 2
    hi, lo = _split_bf16(gk)
    bcum = _dot(cm, hi) + _dot(cm, lo)

    def chunk_row(r):
        return jnp.concatenate(
            [jnp.broadcast_to(bcum[c * GLA_CHUNK + r:c * GLA_CHUNK + r + 1, :], (GLA_CHUNK, bcum.shape[1]))
             for c in range(nch)], axis=0)

    chunk_of_row = lax.broadcasted_iota(I32, bcum.shape, 0) // GLA_CHUNK

    def by_chunk(x):
        return jnp.concatenate([jnp.where(chunk_of_row == c, x, 0.0).astype(BF16) for c in range(nch)], axis=1)

    blast = chunk_row(last_row)
    u_all = _dot(vt, by_chunk(k * jnp.exp(blast - bcum)))
    before = [None] * nch
    for c in order:
        before[c] = st
        decay = jnp.exp(bcum[c * GLA_CHUNK + last_row:c * GLA_CHUNK + last_row + 1, :])
        st = st * decay + jnp.where(bd_mask, u_all[:, c * HEAD_PAIR_DK:(c + 1) * HEAD_PAIR_DK], 0.0)
    if not want_out:
        return None, st
    bmid = chunk_row(mid_row)
    qt = q * jnp.exp(bcum - bmid)
    kt = (k * jnp.exp(bmid - bcum)).astype(BF16)
    lane = lax.broadcasted_iota(I32, qt.shape, 1)
    half = HEAD_PAIR_DK // 2
    o_heads = []
    for hh in range(2):
        sel = (lane < half) if hh == 0 else (lane >= half)
        a = _dot_nt(jnp.where(sel, qt, 0.0).astype(BF16), kt)
        a = jnp.where(amask, a, 0.0).astype(BF16)
        o_heads.append(_dot(a, v[:, hh * LANES:(hh + 1) * LANES]))
    qh = (q * jnp.exp(bcum)).astype(BF16)
    o_inter = jnp.concatenate(
        [_dot_nt(qh[c * GLA_CHUNK:(c + 1) * GLA_CHUNK], before[c].astype(BF16)) for c in range(nch)], axis=0)
    return jnp.concatenate(o_heads, axis=1) + o_inter, st


def _gla_kernel(q_ref, k_ref, v_ref, vt_ref, gkf_ref, gkb_ref, g_ref, kc_ref, vtc_ref, gkfc_ref, gkbc_ref,
                nw_ref, cmf_ref, cmb_ref, o_ref, stf_ref, stb_ref, of_ref, ob_ref):
    t = q_ref.shape[1]
    tc = kc_ref.shape[1]
    nsc, nscc = t // SUPER, tc // SUPER
    cmf = cmf_ref[...]
    cmb = cmb_ref[...]
    amask_f = cmf > 0
    amask_b = cmb > 0
    row = lax.broadcasted_iota(I32, (2 * LANES, HEAD_PAIR_DK), 0)
    lane = lax.broadcasted_iota(I32, (2 * LANES, HEAD_PAIR_DK), 1)
    bd_mask = (row < LANES) == (lane < HEAD_PAIR_DK // 2)

    def ctx_state(gk_ref, cm, fwd, j, st):
        rows = pl.ds(j * SUPER, SUPER)
        return _gla_super(None, kc_ref[0, rows, :].astype(F32), None, vtc_ref[0, j], gk_ref[0, rows, :],
                          cm, None, bd_mask, st, fwd, False)[1]

    def latent(gk_ref, cm, amask, fwd, j, st):
        rows = pl.ds(pl.multiple_of(j * SUPER, SUPER), SUPER)
        return _gla_super(q_ref[0, rows, :].astype(F32), k_ref[0, rows, :].astype(F32), v_ref[0, rows, :],
                          vt_ref[0, j], gk_ref[0, rows, :], cm, amask, bd_mask, st, fwd, True)

    st = jnp.zeros(stf_ref.shape, F32)
    for j in range(nscc):
        st = ctx_state(gkfc_ref, cmf, True, j, st)
    stf_ref[...] = st
    st = jnp.zeros(stb_ref.shape, F32)
    for j in reversed(range(nscc)):
        st = ctx_state(gkbc_ref, cmb, False, j, st)
    stb_ref[...] = st

    def scan_body(jj, carry):
        jb = nsc - 1 - jj
        of, stf = latent(gkf_ref, cmf, amask_f, True, jj, stf_ref[...])
        of_ref[pl.ds(pl.multiple_of(jj * SUPER, SUPER), SUPER), :] = of
        stf_ref[...] = stf
        ob, stb = latent(gkb_ref, cmb, amask_b, False, jb, stb_ref[...])
        ob_ref[pl.ds(pl.multiple_of(jb * SUPER, SUPER), SUPER), :] = ob
        stb_ref[...] = stb
        return carry

    lax.fori_loop(0, nsc, scan_body, 0, unroll=2)

    nw = nw_ref[...]

    def out_body(j, carry):
        rows = pl.ds(pl.multiple_of(j * SUPER, SUPER), SUPER)
        o = of_ref[rows, :] + ob_ref[rows, :]
        g = g_ref[0, rows, :].astype(F32)
        gate = g * jax.nn.sigmoid(g)
        for hh in range(2):
            oh = o[:, hh * LANES:(hh + 1) * LANES]
            on = oh * lax.rsqrt(jnp.mean(oh * oh, axis=-1, keepdims=True) + EPS) * nw
            o_ref[0, rows, hh * LANES:(hh + 1) * LANES] = (on * gate[:, hh * LANES:(hh + 1) * LANES]).astype(BF16)
        return carry

    lax.fori_loop(0, nsc, out_body, 0)


def _gla_masks():
    i = np.arange(SUPER)
    same = (i[:, None] // GLA_CHUNK) == (i[None, :] // GLA_CHUNK)
    fwd = same & (i[None, :] <= i[:, None])
    bwd = same & (i[None, :] >= i[:, None])
    return jnp.asarray(fwd, BF16), jnp.asarray(bwd, BF16)


def _gla_call(q, k, v, vt, gk, g, kc, vtc, gkc, nw):
    b, t, qk = q.shape
    tc = kc.shape[1]
    npair = qk // HEAD_PAIR_DK
    cmf, cmb = _gla_masks()
    lat = lambda width, off: pl.BlockSpec((1, t, width), lambda i, j: (i, 0, j + off))
    ctx = lambda width, off: pl.BlockSpec((1, tc, width), lambda i, j: (i, 0, j + off))
    tr = lambda n_groups: pl.BlockSpec((1, n_groups, 2 * LANES, SUPER), lambda i, j: (i, 0, j, 0))
    const = lambda shape: pl.BlockSpec(shape, lambda i, j: (0,) * len(shape))
    return pl.pallas_call(
        _gla_kernel,
        grid=(b, npair),
        in_specs=[lat(HEAD_PAIR_DK, 0), lat(HEAD_PAIR_DK, 0), lat(2 * LANES, 0), tr(t // SUPER),
                  lat(HEAD_PAIR_DK, 0), lat(HEAD_PAIR_DK, npair), lat(2 * LANES, 0),
                  ctx(HEAD_PAIR_DK, 0), tr(tc // SUPER), ctx(HEAD_PAIR_DK, 0), ctx(HEAD_PAIR_DK, npair),
                  const((1, LANES)), const(cmf.shape), const(cmb.shape)],
        out_specs=lat(2 * LANES, 0),
        out_shape=jax.ShapeDtypeStruct((b, t, v.shape[2]), BF16),
        scratch_shapes=[pltpu.VMEM((2 * LANES, HEAD_PAIR_DK), F32), pltpu.VMEM((2 * LANES, HEAD_PAIR_DK), F32),
                        pltpu.VMEM((t, 2 * LANES), F32), pltpu.VMEM((t, 2 * LANES), F32)],
        compiler_params=pltpu.CompilerParams(dimension_semantics=("arbitrary", "arbitrary")),
        name="gla",
    )(q, k, v, vt, gk, gk, g, kc, vtc, gkc, gkc, nw, cmf, cmb)


def _pool_kernel(x_ref, cm_ref, cnt_ref, wp_ref, ps_ref, o_ref, a_ref, b_ref):
    t = x_ref.shape[1]
    pad = POOL_PAD_GRID_ROWS * GRID_W
    total = t + 2 * pad
    for gi, w in enumerate(POOL_WINDOWS):
        lo = w // 2
        cols = slice(gi * LANES, (gi + 1) * LANES)
        cmat = cm_ref[gi]
        a_ref[0:pad, :] = jnp.zeros((pad, LANES), F32)
        a_ref[pad + t:total, :] = jnp.zeros((pad, LANES), F32)
        for blk in range(t // SUPER):
            rs = slice(blk * SUPER, (blk + 1) * SUPER)
            a_ref[pad + blk * SUPER:pad + (blk + 1) * SUPER, :] = _dot(cmat, x_ref[0, rs, cols])
        src, dst = a_ref, b_ref
        m = 1
        while m < w:
            sh = m * GRID_W
            dst[0:total - sh, :] = src[0:total - sh, :] + src[sh:total, :]
            src, dst = dst, src
            m *= 2
        first = pad - lo * GRID_W
        pooled = src[first:first + t, :] / cnt_ref[gi] - x_ref[0, :, cols].astype(F32)
        yp = _dot(pooled.astype(BF16), wp_ref[gi]) * ps_ref[:, cols]
        o_ref[0, :, cols] = yp.astype(BF16)


def _pool_col_mats():
    i = np.arange(SUPER)
    same_row = (i[:, None] // GRID_W) == (i[None, :] // GRID_W)
    d = i[None, :] - i[:, None]
    mats = []
    for w in POOL_WINDOWS:
        lo = w // 2
        hi = w - 1 - lo
        mats.append(same_row & (d >= -lo) & (d <= hi))
    return jnp.asarray(np.stack(mats), BF16)


def _pool_counts(t):
    rows = t // GRID_W
    r = np.arange(t) // GRID_W
    c = np.arange(t) % GRID_W
    out = []
    for w in POOL_WINDOWS:
        lo = w // 2
        hi = w - 1 - lo
        cnt_r = np.minimum(r + hi + 1, rows) - np.maximum(r - lo, 0)
        cnt_c = np.minimum(c + hi + 1, GRID_W) - np.maximum(c - lo, 0)
        out.append(np.broadcast_to((cnt_r * cnt_c).astype(np.float32)[:, None], (t, LANES)))
    return jnp.asarray(np.stack(out))


def _pool_call(xp, w_pool, pool_scale):
    b, t, pw = xp.shape
    ng = len(POOL_WINDOWS)
    assert max(POOL_WINDOWS) // 2 <= POOL_PAD_GRID_ROWS and t % GRID_W == 0
    cm = _pool_col_mats()
    cnt = _pool_counts(t)
    staged = t + 2 * POOL_PAD_GRID_ROWS * GRID_W
    const = lambda shape: pl.BlockSpec(shape, lambda i: (0,) * len(shape))
    return pl.pallas_call(
        _pool_kernel,
        grid=(b,),
        in_specs=[pl.BlockSpec((1, t, pw), lambda i: (i, 0, 0)),
                  const(cm.shape), const(cnt.shape), const((ng, LANES, LANES)), const((1, pw))],
        out_specs=pl.BlockSpec((1, t, pw), lambda i: (i, 0, 0)),
        out_shape=jax.ShapeDtypeStruct((b, t, pw), BF16),
        scratch_shapes=[pltpu.VMEM((staged, LANES), F32), pltpu.VMEM((staged, LANES), F32)],
        compiler_params=pltpu.CompilerParams(dimension_semantics=("arbitrary",)),
        name="pool",
    )(xp, cm, cnt, w_pool, pool_scale)


def _route_kernel(gla_ref, pool_ref, x_ref, mod_ref, wo_ref, nw_ref, wr_ref, br_ref, lt_ref,
                  h_ref, xt_ref, code_ref, wt_ref, cnt_ref, run_ref, wr2_ref, *, gw, n_exp):
    i = pl.program_id(0)

    @pl.when(i == 0)
    def _():
        run_ref[...] = jnp.zeros_like(run_ref)
        wh, wl = _split_bf16(wr_ref[...])
        wr2_ref[:, :LANES] = wh
        wr2_ref[:, LANES:] = wl

    m = mod_ref[0]
    acc = _dot(gla_ref[...], wo_ref[0:gw, :]) + _dot(pool_ref[...], wo_ref[gw:, :])
    h = x_ref[...] + m[2:3] * acc
    h_ref[...] = h
    xt = _rmsnorm(h, nw_ref[...]) * (1.0 + m[4:5]) + m[3:4]
    xt_ref[...] = _pack_bf16_pairs(xt)
    xh, xl = _split_bf16(xt)
    wr2 = wr2_ref[...]
    t1 = _dot(xh, wr2)
    logits = t1[:, :LANES] + t1[:, LANES:] + _dot(xl, wr2[:, :LANES]) + br_ref[...]
    lane = lax.broadcasted_iota(I32, logits.shape, 1)
    neg = jnp.float32(-jnp.inf)
    logits = jnp.where(lane < n_exp, logits, neg)
    vals, hots = [], []
    e_out = jnp.zeros(logits.shape, I32)
    for j in range(TOP_K):
        mx = jnp.max(logits, axis=-1, keepdims=True)
        idx = jnp.min(jnp.where(logits == mx, lane, LANES), axis=-1, keepdims=True)
        hot = lane == idx
        vals.append(mx)
        hots.append(hot)
        e_out = jnp.where(lane == j, idx, e_out)
        logits = jnp.where(hot, neg, logits)
    ex = [jnp.exp(v - vals[0]) for v in vals]
    den = ex[0] + ex[1] + ex[2] + ex[3]
    w_out = jnp.zeros(logits.shape, F32)
    for j in range(TOP_K):
        w_out = jnp.where(lane == j, ex[j] / den, w_out)
    osum = jnp.where(hots[0] | hots[1] | hots[2] | hots[3], 1.0, 0.0)
    before = _dot(lt_ref[...], osum.astype(BF16)) + run_ref[0:1, :]
    rk_out = jnp.zeros(logits.shape, I32)
    for j in range(TOP_K):
        rj = jnp.sum(jnp.where(hots[j], before, 0.0), axis=-1, keepdims=True)
        rk_out = jnp.where(lane == j, rj.astype(I32), rk_out)
    run = run_ref[0:1, :] + jnp.sum(osum, axis=0, keepdims=True)
    run_ref[...] = jnp.broadcast_to(run, run_ref.shape)
    code = e_out * RANK_LIMIT + rk_out
    code_ref[...] = code.T[:code_ref.shape[0], :]
    wt_ref[...] = w_out
    cnt_ref[...] = jnp.broadcast_to(run, cnt_ref.shape).astype(I32)


def _route_call(gla, pool, x, mod, w_out, nw, wr, br, *, tm, n_exp, first_tile, n_tiles):
    n_all, d = x.shape
    n = n_tiles * tm
    gw = gla.shape[1]
    t_per_b = n_all // mod.shape[0]
    lt = jnp.asarray(np.tril(np.ones((tm, tm), np.float32), -1), BF16)
    row_in = lambda width: pl.BlockSpec((tm, width), lambda i: (i + first_tile, 0))
    row = lambda width: pl.BlockSpec((tm, width), lambda i: (i, 0))
    const = lambda shape: pl.BlockSpec(shape, lambda i: (0,) * len(shape))
    sds = jax.ShapeDtypeStruct
    return pl.pallas_call(
        functools.partial(_route_kernel, gw=gw, n_exp=n_exp),
        grid=(n_tiles,),
        in_specs=[row_in(gw), row_in(pool.shape[1]), row_in(d),
                  pl.BlockSpec((1, N_MOD, d), lambda i: ((i + first_tile) * tm // t_per_b, 0, 0)),
                  const(w_out.shape), const((1, d)), const(wr.shape), const((1, LANES)), const((tm, tm))],
        out_specs=[row(d), row(d // 2), pl.BlockSpec((8, tm), lambda i: (0, i)), row(LANES), const((8, LANES))],
        out_shape=[sds((n, d), F32), sds((n, d // 2), U32), sds((8, n), I32),
                   sds((n, LANES), F32), sds((8, LANES), I32)],
        scratch_shapes=[pltpu.VMEM((8, LANES), F32), pltpu.VMEM((d, 2 * LANES), BF16)],
        compiler_params=pltpu.CompilerParams(dimension_semantics=("arbitrary",)),
        name="route",
    )(gla, pool, x, mod, w_out, nw, wr, br, lt)


def _plan_kernel(cnt_ref, code_ref, dest_ref, be_ref, nv_ref, first_ref, next_ref, slot_ref, nu_ref, start_ref,
                 *, n_exp, rows):
    @pl.when(pl.program_id(0) == 0)
    def _():
        blk = (lax.broadcasted_iota(I32, be_ref.shape, 0) * LANES + lax.broadcasted_iota(I32, be_ref.shape, 1))
        blk_row0 = blk * rows
        nxt_e = [None] * n_exp
        nxt = jnp.int32(-1)
        for e in reversed(range(n_exp)):
            nxt_e[e] = nxt
            nxt = jnp.where(cnt_ref[e] > 0, e, nxt)
        zeros = jnp.zeros(be_ref.shape, I32)
        be, end_valid, first, nxt_blk, slot = zeros, zeros, zeros, zeros - 1, zeros
        acc = jnp.int32(0)
        last_e = jnp.int32(0)
        ordinal = jnp.int32(0)
        for e in range(n_exp):
            c = cnt_ref[e]
            start_ref[e] = acc
            in_e = (blk_row0 >= acc) & (c > 0)
            end_valid = jnp.where(in_e, acc + c, end_valid)
            be = jnp.where(in_e, e, be)
            first = jnp.where(in_e, (blk_row0 == acc).astype(I32), first)
            nxt_blk = jnp.where(in_e, nxt_e[e], nxt_blk)
            slot = jnp.where(in_e, ordinal % 2, slot)
            acc = acc + (c + rows - 1) // rows * rows
            last_e = jnp.where(c > 0, e, last_e)
            ordinal = ordinal + (c > 0).astype(I32)
        n_used = acc // rows
        nu_ref[0] = n_used
        be_ref[...] = jnp.where(blk < n_used, be, last_e)
        nv_ref[...] = jnp.clip(end_valid - blk_row0, 0, rows)
        first_ref[...] = first
        next_ref[...] = nxt_blk
        slot_ref[...] = slot

    code = code_ref[...]
    e_vec = code // RANK_LIMIT
    dest = code % RANK_LIMIT
    for e in range(n_exp):
        dest = dest + jnp.where(e_vec == e, start_ref[e], 0)
    dest_ref[...] = dest


def _plan_call(counts, code_t, *, n_exp, rows, chunk):
    n = code_t.shape[1]
    sds = jax.ShapeDtypeStruct
    smem = pltpu.SMEM
    return pl.pallas_call(
        functools.partial(_plan_kernel, n_exp=n_exp, rows=rows),
        grid=(n // chunk,),
        in_specs=[pl.BlockSpec(memory_space=smem), pl.BlockSpec((8, chunk), lambda i: (0, i))],
        out_specs=[pl.BlockSpec((8, chunk), lambda i: (0, i))]
        + [pl.BlockSpec((8, LANES), lambda i: (0, 0))] * 5 + [pl.BlockSpec(memory_space=smem)],
        out_shape=[sds((8, n), I32)] + [sds((8, LANES), I32)] * 5 + [sds((1,), I32)],
        scratch_shapes=[pltpu.SMEM((n_exp,), I32)],
        compiler_params=pltpu.CompilerParams(dimension_semantics=("arbitrary",)),
        name="plan",
    )(counts, code_t)


def _sc_worker_id():
    return lax.axis_index("s") * SC_CORES + lax.axis_index("c")


def _sc_scatter_call(x, idx3, *, n_out):
    n, d = x.shape
    n_win_total, k, w = idx3.shape
    n_win = n_win_total // SC_WORKERS
    mesh = plsc.VectorSubcoreMesh(core_axis_name="c", subcore_axis_name="s")

    @functools.partial(
        pl.kernel, mesh=mesh,
        out_type=jax.ShapeDtypeStruct((n_out, d), x.dtype),
        scratch_types=[pltpu.VMEM((k, w), I32), pltpu.VMEM((w, d), x.dtype), pltpu.SemaphoreType.DMA],
        name="sc_dispatch",
    )
    def kern(x_hbm, idx_hbm, out_hbm, idx_v, rows_v, sem):
        wid = _sc_worker_id()

        @pl.loop(0, n_win)
        def _(i):
            win = wid * n_win + i
            pltpu.sync_copy(idx_hbm.at[win], idx_v)
            pltpu.sync_copy(x_hbm.at[pl.ds(win * w, w)], rows_v)
            for j in range(k):
                pltpu.async_copy(rows_v, out_hbm.at[idx_v.at[j]], sem).wait()

    return kern(x, idx3)


def _sc_gather_call(table, idx3):
    n_workers, n_win, w = idx3.shape
    d = table.shape[1]
    assert n_workers == SC_WORKERS and n_win % 2 == 0
    mesh = plsc.VectorSubcoreMesh(core_axis_name="c", subcore_axis_name="s")

    @functools.partial(
        pl.kernel, mesh=mesh,
        out_type=jax.ShapeDtypeStruct((n_workers * n_win * w, d), table.dtype),
        scratch_types=[pltpu.VMEM((n_win, w), I32), pltpu.VMEM((2, w, d), table.dtype),
                       pltpu.SemaphoreType.DMA((2,)), pltpu.SemaphoreType.DMA((2,))],
        name="sc_gather",
    )
    def kern(table_hbm, idx_hbm, out_hbm, idx_v, rows_v, gsem, osem):
        wid = _sc_worker_id()
        base = wid * n_win
        pltpu.sync_copy(idx_hbm.at[wid], idx_v)

        def gather(wi, b):
            return pltpu.make_async_copy(table_hbm.at[idx_v.at[wi]], rows_v.at[b], gsem.at[b])

        def put(wi, b):
            return pltpu.make_async_copy(rows_v.at[b], out_hbm.at[pl.ds((base + wi) * w, w)], osem.at[b])

        gather(0, 0).start()

        @pl.loop(0, n_win, step=2)
        def _(i):
            for b in range(2):
                wi = i + b

                @pl.when(wi + 1 < n_win)
                def _():
                    @pl.when(wi >= 1)
                    def _():
                        put(wi - 1, 1 - b).wait()
                    gather(wi + 1, 1 - b).start()

                gather(wi, b).wait()
                put(wi, b).start()

        put(n_win - 2, 0).wait()
        put(n_win - 1, 1).wait()

    return kern(table, idx3)


def _expert_kernel(be_ref, nu_ref, nv_ref, first_ref, next_ref, slot_ref,
                   x_ref, wgu_hbm, bgu_ref, wd_hbm, bd_ref, y_ref,
                   wgu_f32_ref, wd_f32_ref, wgu_bf_ref, wd_bf_ref, sem, *, d_ff):
    i = pl.program_id(0)
    slot = slot_ref[i]

    def fetch(e, s):
        return (pltpu.make_async_copy(wgu_hbm.at[e], wgu_f32_ref.at[s], sem.at[0, s]),
                pltpu.make_async_copy(wd_hbm.at[e], wd_f32_ref.at[s], sem.at[1, s]))

    @pl.when(i == 0)
    def _():
        for cp in fetch(be_ref[0], slot):
            cp.start()

    @pl.when(first_ref[i] == 1)
    def _():
        for cp in fetch(be_ref[i], slot):
            cp.wait()

        @pl.when(next_ref[i] >= 0)
        def _():
            for cp in fetch(next_ref[i], 1 - slot):
                cp.start()

        wgu_bf_ref[...] = wgu_f32_ref[slot].astype(BF16)
        wd_bf_ref[...] = wd_f32_ref[slot].astype(BF16)

    n_valid = nv_ref[i]
    n_rows = x_ref.shape[0]

    def expert_rows(rows):
        row = lax.broadcasted_iota(I32, (rows, x_ref.shape[1]), 0)
        lo, hi = _unpack_bf16_pairs(jnp.where(row < n_valid, x_ref[0:rows, :], jnp.uint32(0)))
        xb = jnp.concatenate([lo, hi], axis=1).astype(BF16)
        gu = _dot(xb, wgu_bf_ref[...]) + bgu_ref[0]
        gate = jnp.minimum(gu[:, :d_ff], SWIGLU_LIMIT)
        up = jnp.clip(gu[:, d_ff:], -SWIGLU_LIMIT, SWIGLU_LIMIT)
        act = (up + 1.0) * gate * jax.nn.sigmoid(SWIGLU_ALPHA * gate)
        y_ref[0:rows, :] = _pack_bf16_pairs(_dot(act.astype(BF16), wd_bf_ref[...]) + bd_ref[0])

    in_use = i < nu_ref[0]

    for rows in range(EXPERT_ROW_STEP, n_rows + 1, EXPERT_ROW_STEP):
        @pl.when(in_use & (n_valid > rows - EXPERT_ROW_STEP) & (n_valid <= rows))
        def _(rows=rows):
            expert_rows(rows)
            if rows < n_rows:
                y_ref[rows:, :] = jnp.zeros((n_rows - rows, y_ref.shape[1]), y_ref.dtype)


def _expert_call(plan, xs, w_gu, b_gu, w_down, b_down):
    block_e, n_valid, first, nxt, slot, n_used = plan
    n_pad = xs.shape[0]
    n_exp, d, two_ff = w_gu.shape
    d_ff = two_ff // 2
    nblk = n_pad // EXPERT_ROWS
    rows = lambda i, be, nu, *_: (jnp.minimum(i, nu[0] - 1), 0)
    per_e = lambda i, be, *_: (be[i], 0, 0)
    grid_spec = pltpu.PrefetchScalarGridSpec(
        num_scalar_prefetch=6,
        grid=(nblk,),
        in_specs=[pl.BlockSpec((EXPERT_ROWS, d // 2), rows),
                  pl.BlockSpec(memory_space=pl.ANY), pl.BlockSpec((1, 1, two_ff), per_e),
                  pl.BlockSpec(memory_space=pl.ANY), pl.BlockSpec((1, 1, d), per_e)],
        out_specs=pl.BlockSpec((EXPERT_ROWS, d // 2), rows),
        scratch_shapes=[pltpu.VMEM((2, d, two_ff), F32), pltpu.VMEM((2, d_ff, d), F32),
                        pltpu.VMEM((d, two_ff), BF16), pltpu.VMEM((d_ff, d), BF16),
                        pltpu.SemaphoreType.DMA((2, 2))],
    )
    flat = lambda a: a.reshape(-1)[:nblk]
    return pl.pallas_call(
        functools.partial(_expert_kernel, d_ff=d_ff),
        grid_spec=grid_spec,
        out_shape=jax.ShapeDtypeStruct((n_pad, d // 2), U32),
        compiler_params=pltpu.CompilerParams(dimension_semantics=("arbitrary",),
                                             vmem_limit_bytes=EXPERT_VMEM_BYTES),
        name="experts",
    )(flat(block_e), n_used, flat(n_valid), flat(first), flat(nxt), flat(slot),
      xs, w_gu, b_gu.reshape(n_exp, 1, two_ff), w_down, b_down.reshape(n_exp, 1, d))


def _combine_kernel(y4_ref, wt_ref, h_ref, mod_ref, fw_ref, o_ref):
    wt = wt_ref[...]
    acc_lo, acc_hi = None, None
    for j in range(TOP_K):
        lo, hi = _unpack_bf16_pairs(y4_ref[j])
        w = wt[:, j:j + 1]
        acc_lo = w * lo if j == 0 else acc_lo + w * lo
        acc_hi = w * hi if j == 0 else acc_hi + w * hi
    acc = jnp.concatenate([acc_lo, acc_hi], axis=1)
    m = mod_ref[0]
    o_ref[...] = _rmsnorm(h_ref[...] + m[5:6] * acc, fw_ref[...])


def _combine_call(y4, wts, h, mod, fw, prev_out, *, n, tg, first_tile):
    d = h.shape[1]
    t_per_b = n // mod.shape[0]
    part = lambda width: pl.BlockSpec((tg, width), lambda i: (i, 0))
    row = lambda width: pl.BlockSpec((tg, width), lambda i: (i + first_tile, 0))
    in_specs = [pl.BlockSpec((TOP_K, tg, d // 2), lambda i: (0, i, 0)),
                part(LANES), part(d),
                pl.BlockSpec((1, N_MOD, d), lambda i: ((i + first_tile) * tg // t_per_b, 0, 0)),
                pl.BlockSpec((1, d), lambda i: (0, 0))]
    args = [y4, wts, h, mod, fw]
    kern = _combine_kernel
    aliases = {}
    if prev_out is not None:
        in_specs.append(pl.BlockSpec(memory_space=pl.ANY))
        args.append(prev_out)
        kern = lambda y4_ref, wt_ref, h_ref, mod_ref, fw_ref, prev_ref, o_ref: _combine_kernel(
            y4_ref, wt_ref, h_ref, mod_ref, fw_ref, o_ref)
        aliases = {len(args) - 1: 0}
    return pl.pallas_call(
        kern,
        grid=(y4.shape[1] // tg,),
        in_specs=in_specs,
        out_specs=row(d),
        out_shape=jax.ShapeDtypeStruct((n, d), F32),
        input_output_aliases=aliases,
        compiler_params=pltpu.CompilerParams(dimension_semantics=("arbitrary",)),
        name="combine",
    )(*args)


def kernel(x, c, ctx, c_ctx, w_ada, b_ada, norm_mix_w, norm_mlp_w, w_in, w_gk_f, b_gk_f, w_gk_b, b_gk_b,
           gla_norm_w, w_pool, pool_scale, w_out, w_router, b_router, w_gu, b_gu, w_down, b_down,
           final_norm_w):
    b, t, d = x.shape
    assert w_ada.shape[0] == 1, "single-layer trunk"
    n_exp = w_router.shape[2]
    rank = w_gk_f.shape[1]
    qk = w_gk_f.shape[2]
    dk = qk // GLA_HEADS
    gw = GLA_HEADS * gla_norm_w.shape[1]
    pw = w_pool.shape[1] * w_pool.shape[2]
    assert w_in.shape[2] == 2 * qk + 2 * gw + 2 * rank + pw and 2 * rank <= LANES
    assert t % SUPER == 0 and ctx.shape[1] % SUPER == 0 and n_exp <= LANES

    rows = -(-(b + 1) // 8) * 8
    cc = jnp.zeros((rows, d), F32).at[:b].set(c).at[b].set(c_ctx)
    mod = _mod_call(cc, w_ada[0], b_ada)
    mod_x = mod[:b].reshape(b, N_MOD, d)
    mod_c = mod[b:b + 1].reshape(1, N_MOD, d)

    wi = w_in[0]
    o_r = 2 * qk + 2 * gw
    w_cat = jnp.concatenate([wi[:, :o_r], wi[:, o_r + 2 * rank:], wi[:, o_r:o_r + 2 * rank],
                             jnp.zeros((d, LANES - 2 * rank), F32)], axis=1).astype(BF16)
    wgk = jnp.zeros((LANES, 2 * qk), F32).at[:rank, :qk].set(w_gk_f[0]).at[rank:2 * rank, qk:].set(w_gk_b[0])
    bgk = jnp.concatenate([b_gk_f[0], b_gk_b[0]])[None, :]
    proj = functools.partial(_inproj_call, nw=norm_mix_w, w=w_cat, wgk=wgk.astype(BF16), bgk=bgk,
                             qk=qk, gw=gw, pw=pw, dk=dk)
    q, k, v, vt, g, xp, gk = proj(x, mod_x, tm=512, state_only=False)
    kc, vtc, gkc = proj(ctx, mod_c, tm=SUPER, state_only=True)

    gla = _gla_call(q, k, v, vt, gk, g, kc, vtc, gkc, gla_norm_w)
    pool = _pool_call(xp, w_pool[0].astype(BF16), pool_scale)

    n = b * t
    wr = jnp.zeros((d, LANES), F32).at[:, :n_exp].set(w_router[0])
    br = jnp.zeros((1, LANES), F32).at[0, :n_exp].set(b_router[0])
    n_part = n // MOE_PARTS
    tm, tg = 512, 256
    assert n % MOE_PARTS == 0 and n_part % (SC_WORKERS * SC_WINDOW) == 0 and n_part % tm == 0 and n_part < RANK_LIMIT
    assert (n_part * TOP_K) % (SC_WORKERS * 2 * SC_GATHER_WINDOW) == 0
    n_pad = n_part * TOP_K + n_exp * EXPERT_ROWS
    nblk = n_pad // EXPERT_ROWS
    assert nblk <= 8 * LANES
    parts = range(MOE_PARTS)
    routed = [_route_call(gla.reshape(n, gw), pool.reshape(n, pw), x.reshape(n, d), mod_x, w_out[0].astype(BF16),
                          norm_mlp_w, wr, br, tm=tm, n_exp=n_exp,
                          first_tile=p * n_part // tm, n_tiles=n_part // tm) for p in parts]
    plans = [_plan_call(cnt[0, :n_exp], code_t, n_exp=n_exp, rows=EXPERT_ROWS, chunk=min(n_part, 4096))
             for (_, _, code_t, _, cnt) in routed]
    dests = [plan[0][:TOP_K] for plan in plans]
    xs = [_sc_scatter_call(routed[p][1],
                           dests[p].reshape(TOP_K, n_part // SC_WINDOW, SC_WINDOW).transpose(1, 0, 2), n_out=n_pad)
          for p in parts]
    ys = [_expert_call(plans[p][1:], xs[p], w_gu[0], b_gu[0], w_down[0], b_down[0]) for p in parts]
    y4 = [_sc_gather_call(ys[p], dests[p].reshape(SC_WORKERS, -1, SC_GATHER_WINDOW)).reshape(TOP_K, n_part, d // 2)
          for p in parts]
    out = None
    for p in parts:
        out = _combine_call(y4[p], routed[p][3], routed[p][0], mod_x, final_norm_w[None, :], out,
                            n=n, tg=tg, first_tile=p * n_part // tg)
    return out.reshape(b, t, d)
```

```python
import functools

import numpy as np
import jax
import jax.numpy as jnp
from jax import lax
from jax.experimental import pallas as pl
from jax.experimental.pallas import tpu as pltpu
from jax.experimental.pallas import tpu_sc as plsc

F32 = jnp.float32
BF16 = jnp.bfloat16
I32 = jnp.int32
U32 = jnp.uint32

GRID_W = 64
GLA_HEADS = 4
GLA_CHUNK = 64
GATE_NORMALIZER = 16.0
POOL_WINDOWS = (2, 4, 8, 16)
POOL_PAD_GRID_ROWS = 8
TOP_K = 4
RANK_LIMIT = 1 << 20
SWIGLU_LIMIT = 7.0
SWIGLU_ALPHA = 1.702
N_MOD = 6
EPS = 1e-6

LANES = 128
SUPER = 256
HEAD_PAIR_DK = 128
EXPERT_ROWS = 512
EXPERT_ROW_STEP = 128
MOE_PARTS = 2
EXPERT_VMEM_BYTES = 56 * 1024 * 1024
SC_CORES = 2
SC_SUBCORES = 16
SC_WORKERS = SC_CORES * SC_SUBCORES
SC_WINDOW = 32
SC_GATHER_WINDOW = 64


def _dot(a, b):
    return jnp.dot(a, b, preferred_element_type=F32)


def _dot_nt(a, b):
    return lax.dot_general(a, b, (((1,), (1,)), ((), ())), preferred_element_type=F32)


def _split_bf16(x):
    hi = x.astype(BF16)
    lo = (x - hi.astype(F32)).astype(BF16)
    return hi, lo


def _pack_bf16_pairs(x):
    c = x.shape[1] // 2
    lo = lax.bitcast_convert_type(x[:, :c].astype(BF16).astype(F32), U32)
    hi = lax.bitcast_convert_type(x[:, c:].astype(BF16).astype(F32), U32)
    return (lo >> 16) | hi


def _unpack_bf16_pairs(p):
    lo = lax.bitcast_convert_type(p << 16, F32)
    hi = lax.bitcast_convert_type(p & jnp.uint32(0xFFFF0000), F32)
    return lo, hi


def _rmsnorm(x, w):
    var = jnp.mean(x * x, axis=-1, keepdims=True)
    return x * lax.rsqrt(var + EPS) * w


def _mod_kernel(c_ref, w_ref, b_ref, o_ref):
    c = c_ref[...]
    s = c * jax.nn.sigmoid(c)
    o_ref[...] = jnp.dot(s, w_ref[...], precision=lax.Precision.HIGHEST,
                         preferred_element_type=F32) + b_ref[...]


def _mod_call(cc, w_ada, b_ada):
    rows, d = cc.shape
    n = w_ada.shape[1]
    tn = 1024
    return pl.pallas_call(
        _mod_kernel,
        grid=(n // tn,),
        in_specs=[pl.BlockSpec((rows, d), lambda j: (0, 0)),
                  pl.BlockSpec((d, tn), lambda j: (0, j)),
                  pl.BlockSpec((1, tn), lambda j: (0, j))],
        out_specs=pl.BlockSpec((rows, tn), lambda j: (0, j)),
        out_shape=jax.ShapeDtypeStruct((rows, n), F32),
        name="mod",
    )(cc, w_ada, b_ada)


def _inproj_kernel(x_ref, mod_ref, nw_ref, w_ref, wgk_ref, bgk_ref,
                   q_ref, k_ref, v_ref, vt_ref, g_ref, p_ref, gk_ref, *, qk, gw, pw, dk):
    x = x_ref[0]
    m = mod_ref[0]
    hm = (_rmsnorm(x, nw_ref[...]) * (1.0 + m[1:2]) + m[0:1]).astype(BF16)
    p = _dot(hm, w_ref[...])
    vt = p[:, 2 * qk:2 * qk + gw].T
    for s in range(vt_ref.shape[1]):
        vt_ref[0, s] = vt[:, s * SUPER:(s + 1) * SUPER].astype(BF16)
    o = 0
    q_ref[0] = (p[:, o:o + qk] * (dk ** -0.5)).astype(BF16); o += qk
    k_ref[0] = p[:, o:o + qk].astype(BF16); o += qk
    v_ref[0] = p[:, o:o + gw].astype(BF16); o += gw
    g_ref[0] = p[:, o:o + gw].astype(BF16); o += gw
    p_ref[0] = p[:, o:o + pw].astype(BF16); o += pw
    r = p[:, o:o + LANES]
    z = _dot(r.astype(BF16), wgk_ref[...]) + bgk_ref[...]
    gk_ref[0] = (jnp.minimum(z, 0.0) - jnp.log1p(jnp.exp(-jnp.abs(z)))) * (1.0 / GATE_NORMALIZER)


def _inproj_call(x, mod, nw, w, wgk, bgk, *, qk, gw, pw, dk, tm):
    b, t, d = x.shape
    n_in = w.shape[1]
    bs = lambda width: pl.BlockSpec((1, tm, width), lambda i, j: (i, j, 0))
    const = lambda shape: pl.BlockSpec(shape, lambda i, j: (0,) * len(shape))
    per_batch = mod.shape[0] > 1
    sds = jax.ShapeDtypeStruct
    return pl.pallas_call(
        functools.partial(_inproj_kernel, qk=qk, gw=gw, pw=pw, dk=dk),
        grid=(b, t // tm),
        in_specs=[bs(d),
                  pl.BlockSpec((1, N_MOD, d), (lambda i, j: (i, 0, 0)) if per_batch else (lambda i, j: (0, 0, 0))),
                  const((1, d)), const((d, n_in)), const((LANES, 2 * qk)), const((1, 2 * qk))],
        out_specs=[bs(qk), bs(qk), bs(gw),
                   pl.BlockSpec((1, tm // SUPER, gw, SUPER), lambda i, j: (i, j, 0, 0)),
                   bs(gw), bs(pw), bs(2 * qk)],
        out_shape=[sds((b, t, qk), BF16), sds((b, t, qk), BF16), sds((b, t, gw), BF16),
                   sds((b, t // SUPER, gw, SUPER), BF16),
                   sds((b, t, gw), BF16), sds((b, t, pw), BF16), sds((b, t, 2 * qk), F32)],
        compiler_params=pltpu.CompilerParams(dimension_semantics=("arbitrary", "arbitrary")),
        name="inproj",
    )(x, mod, nw, w, wgk, bgk)


def _gla_super(q, k, v, vt, gk, cm, amask, bd_mask, st, fwd, want_out):
    nch = SUPER // GLA_CHUNK
    order = tuple(range(nch)) if fwd else tuple(reversed(range(nch)))
    last_row = GLA_CHUNK - 1 if fwd else 0
    mid_row = GLA_CHUNK // 2 - 1 if fwd else GLA_CHUNK // 2
    hi, lo = _split_bf16(gk)
    bcum = _dot(cm, hi) + _dot(cm, lo)

    def chunk_row(r):
        return jnp.concatenate(
            [jnp.broadcast_to(bcum[c * GLA_CHUNK + r:c * GLA_CHUNK + r + 1, :], (GLA_CHUNK, bcum.shape[1]))
             for c in range(nch)], axis=0)

    chunk_of_row = lax.broadcasted_iota(I32, bcum.shape, 0) // GLA_CHUNK

    def by_chunk(x):
        return jnp.concatenate([jnp.where(chunk_of_row == c, x, 0.0).astype(BF16) for c in range(nch)], axis=1)

    blast = chunk_row(last_row)
    u_all = _dot(vt, by_chunk(k * jnp.exp(blast - bcum)))
    before = [None] * nch
    for c in order:
        before[c] = st
        decay = jnp.exp(bcum[c * GLA_CHUNK + last_row:c * GLA_CHUNK + last_row + 1, :])
        st = st * decay + jnp.where(bd_mask, u_all[:, c * HEAD_PAIR_DK:(c + 1) * HEAD_PAIR_DK], 0.0)
    if not want_out:
        return None, st
    bmid = chunk_row(mid_row)
    qt = q * jnp.exp(bcum - bmid)
    kt = (k * jnp.exp(bmid - bcum)).astype(BF16)
    lane = lax.broadcasted_iota(I32, qt.shape, 1)
    half = HEAD_PAIR_DK // 2
    o_heads = []
    for hh in range(2):
        sel = (lane < half) if hh == 0 else (lane >= half)
        a = _dot_nt(jnp.where(sel, qt, 0.0).astype(BF16), kt)
        a = jnp.where(amask, a, 0.0).astype(BF16)
        o_heads.append(_dot(a, v[:, hh * LANES:(hh + 1) * LANES]))
    qh = (q * jnp.exp(bcum)).astype(BF16)
    o_inter = jnp.concatenate(
        [_dot_nt(qh[c * GLA_CHUNK:(c + 1) * GLA_CHUNK], before[c].astype(BF16)) for c in range(nch)], axis=0)
    return jnp.concatenate(o_heads, axis=1) + o_inter, st


def _gla_kernel(q_ref, k_ref, v_ref, vt_ref, gkf_ref, gkb_ref, g_ref, kc_ref, vtc_ref, gkfc_ref, gkbc_ref,
                nw_ref, cmf_ref, cmb_ref, o_ref, stf_ref, stb_ref, of_ref, ob_ref):
    t = q_ref.shape[1]
    tc = kc_ref.shape[1]
    nsc, nscc = t // SUPER, tc // SUPER
    cmf = cmf_ref[...]
    cmb = cmb_ref[...]
    amask_f = cmf > 0
    amask_b = cmb > 0
    row = lax.broadcasted_iota(I32, (2 * LANES, HEAD_PAIR_DK), 0)
    lane = lax.broadcasted_iota(I32, (2 * LANES, HEAD_PAIR_DK), 1)
    bd_mask = (row < LANES) == (lane < HEAD_PAIR_DK // 2)

    def ctx_state(gk_ref, cm, fwd, j, st):
        rows = pl.ds(j * SUPER, SUPER)
        return _gla_super(None, kc_ref[0, rows, :].astype(F32), None, vtc_ref[0, j], gk_ref[0, rows, :],
                          cm, None, bd_mask, st, fwd, False)[1]

    def latent(gk_ref, cm, amask, fwd, j, st):
        rows = pl.ds(pl.multiple_of(j * SUPER, SUPER), SUPER)
        return _gla_super(q_ref[0, rows, :].astype(F32), k_ref[0, rows, :].astype(F32), v_ref[0, rows, :],
                          vt_ref[0, j], gk_ref[0, rows, :], cm, amask, bd_mask, st, fwd, True)

    st = jnp.zeros(stf_ref.shape, F32)
    for j in range(nscc):
        st = ctx_state(gkfc_ref, cmf, True, j, st)
    stf_ref[...] = st
    st = jnp.zeros(stb_ref.shape, F32)
    for j in reversed(range(nscc)):
        st = ctx_state(gkbc_ref, cmb, False, j, st)
    stb_ref[...] = st

    def scan_body(jj, carry):
        jb = nsc - 1 - jj
        of, stf = latent(gkf_ref, cmf, amask_f, True, jj, stf_ref[...])
        of_ref[pl.ds(pl.multiple_of(jj * SUPER, SUPER), SUPER), :] = of
        stf_ref[...] = stf
        ob, stb = latent(gkb_ref, cmb, amask_b, False, jb, stb_ref[...])
        ob_ref[pl.ds(pl.multiple_of(jb * SUPER, SUPER), SUPER), :] = ob
        stb_ref[...] = stb
        return carry

    lax.fori_loop(0, nsc, scan_body, 0, unroll=2)

    nw = nw_ref[...]

    def out_body(j, carry):
        rows = pl.ds(pl.multiple_of(j * SUPER, SUPER), SUPER)
        o = of_ref[rows, :] + ob_ref[rows, :]
        g = g_ref[0, rows, :].astype(F32)
        gate = g * jax.nn.sigmoid(g)
        for hh in range(2):
            oh = o[:, hh * LANES:(hh + 1) * LANES]
            on = oh * lax.rsqrt(jnp.mean(oh * oh, axis=-1, keepdims=True) + EPS) * nw
            o_ref[0, rows, hh * LANES:(hh + 1) * LANES] = (on * gate[:, hh * LANES:(hh + 1) * LANES]).astype(BF16)
        return carry

    lax.fori_loop(0, nsc, out_body, 0)


def _gla_masks():
    i = np.arange(SUPER)
    same = (i[:, None] // GLA_CHUNK) == (i[None, :] // GLA_CHUNK)
    fwd = same & (i[None, :] <= i[:, None])
    bwd = same & (i[None, :] >= i[:, None])
    return jnp.asarray(fwd, BF16), jnp.asarray(bwd, BF16)


def _gla_call(q, k, v, vt, gk, g, kc, vtc, gkc, nw):
    b, t, qk = q.shape
    tc = kc.shape[1]
    npair = qk // HEAD_PAIR_DK
    cmf, cmb = _gla_masks()
    lat = lambda width, off: pl.BlockSpec((1, t, width), lambda i, j: (i, 0, j + off))
    ctx = lambda width, off: pl.BlockSpec((1, tc, width), lambda i, j: (i, 0, j + off))
    tr = lambda n_groups: pl.BlockSpec((1, n_groups, 2 * LANES, SUPER), lambda i, j: (i, 0, j, 0))
    const = lambda shape: pl.BlockSpec(shape, lambda i, j: (0,) * len(shape))
    return pl.pallas_call(
        _gla_kernel,
        grid=(b, npair),
        in_specs=[lat(HEAD_PAIR_DK, 0), lat(HEAD_PAIR_DK, 0), lat(2 * LANES, 0), tr(t // SUPER),
                  lat(HEAD_PAIR_DK, 0), lat(HEAD_PAIR_DK, npair), lat(2 * LANES, 0),
                  ctx(HEAD_PAIR_DK, 0), tr(tc // SUPER), ctx(HEAD_PAIR_DK, 0), ctx(HEAD_PAIR_DK, npair),
                  const((1, LANES)), const(cmf.shape), const(cmb.shape)],
        out_specs=lat(2 * LANES, 0),
        out_shape=jax.ShapeDtypeStruct((b, t, v.shape[2]), BF16),
        scratch_shapes=[pltpu.VMEM((2 * LANES, HEAD_PAIR_DK), F32), pltpu.VMEM((2 * LANES, HEAD_PAIR_DK), F32),
                        pltpu.VMEM((t, 2 * LANES), F32), pltpu.VMEM((t, 2 * LANES), F32)],
        compiler_params=pltpu.CompilerParams(dimension_semantics=("arbitrary", "arbitrary")),
        name="gla",
    )(q, k, v, vt, gk, gk, g, kc, vtc, gkc, gkc, nw, cmf, cmb)


def _pool_kernel(x_ref, cm_ref, cnt_ref, wp_ref, ps_ref, o_ref, a_ref, b_ref):
    t = x_ref.shape[1]
    pad = POOL_PAD_GRID_ROWS * GRID_W
    total = t + 2 * pad
    for gi, w in enumerate(POOL_WINDOWS):
        lo = w // 2
        cols = slice(gi * LANES, (gi + 1) * LANES)
        cmat = cm_ref[gi]
        a_ref[0:pad, :] = jnp.zeros((pad, LANES), F32)
        a_ref[pad + t:total, :] = jnp.zeros((pad, LANES), F32)
        for blk in range(t // SUPER):
            rs = slice(blk * SUPER, (blk + 1) * SUPER)
            a_ref[pad + blk * SUPER:pad + (blk + 1) * SUPER, :] = _dot(cmat, x_ref[0, rs, cols])
        src, dst = a_ref, b_ref
        m = 1
        while m < w:
            sh = m * GRID_W
            dst[0:total - sh, :] = src[0:total - sh, :] + src[sh:total, :]
            src, dst = dst, src
            m *= 2
        first = pad - lo * GRID_W
        pooled = src[first:first + t, :] / cnt_ref[gi] - x_ref[0, :, cols].astype(F32)
        yp = _dot(pooled.astype(BF16), wp_ref[gi]) * ps_ref[:, cols]
        o_ref[0, :, cols] = yp.astype(BF16)


def _pool_col_mats():
    i = np.arange(SUPER)
    same_row = (i[:, None] // GRID_W) == (i[None, :] // GRID_W)
    d = i[None, :] - i[:, None]
    mats = []
    for w in POOL_WINDOWS:
        lo = w // 2
        hi = w - 1 - lo
        mats.append(same_row & (d >= -lo) & (d <= hi))
    return jnp.asarray(np.stack(mats), BF16)


def _pool_counts(t):
    rows = t // GRID_W
    r = np.arange(t) // GRID_W
    c = np.arange(t) % GRID_W
    out = []
    for w in POOL_WINDOWS:
        lo = w // 2
        hi = w - 1 - lo
        cnt_r = np.minimum(r + hi + 1, rows) - np.maximum(r - lo, 0)
        cnt_c = np.minimum(c + hi + 1, GRID_W) - np.maximum(c - lo, 0)
        out.append(np.broadcast_to((cnt_r * cnt_c).astype(np.float32)[:, None], (t, LANES)))
    return jnp.asarray(np.stack(out))


def _pool_call(xp, w_pool, pool_scale):
    b, t, pw = xp.shape
    ng = len(POOL_WINDOWS)
    assert max(POOL_WINDOWS) // 2 <= POOL_PAD_GRID_ROWS and t % GRID_W == 0
    cm = _pool_col_mats()
    cnt = _pool_counts(t)
    staged = t + 2 * POOL_PAD_GRID_ROWS * GRID_W
    const = lambda shape: pl.BlockSpec(shape, lambda i: (0,) * len(shape))
    return pl.pallas_call(
        _pool_kernel,
        grid=(b,),
        in_specs=[pl.BlockSpec((1, t, pw), lambda i: (i, 0, 0)),
                  const(cm.shape), const(cnt.shape), const((ng, LANES, LANES)), const((1, pw))],
        out_specs=pl.BlockSpec((1, t, pw), lambda i: (i, 0, 0)),
        out_shape=jax.ShapeDtypeStruct((b, t, pw), BF16),
        scratch_shapes=[pltpu.VMEM((staged, LANES), F32), pltpu.VMEM((staged, LANES), F32)],
        compiler_params=pltpu.CompilerParams(dimension_semantics=("arbitrary",)),
        name="pool",
    )(xp, cm, cnt, w_pool, pool_scale)


def _route_kernel(gla_ref, pool_ref, x_ref, mod_ref, wo_ref, nw_ref, wr_ref, br_ref, lt_ref,
                  h_ref, xt_ref, code_ref, wt_ref, cnt_ref, run_ref, wr2_ref, *, gw, n_exp):
    i = pl.program_id(0)

    @pl.when(i == 0)
    def _():
        run_ref[...] = jnp.zeros_like(run_ref)
        wh, wl = _split_bf16(wr_ref[...])
        wr2_ref[:, :LANES] = wh
        wr2_ref[:, LANES:] = wl

    m = mod_ref[0]
    acc = _dot(gla_ref[...], wo_ref[0:gw, :]) + _dot(pool_ref[...], wo_ref[gw:, :])
    h = x_ref[...] + m[2:3] * acc
    h_ref[...] = h
    xt = _rmsnorm(h, nw_ref[...]) * (1.0 + m[4:5]) + m[3:4]
    xt_ref[...] = _pack_bf16_pairs(xt)
    xh, xl = _split_bf16(xt)
    wr2 = wr2_ref[...]
    t1 = _dot(xh, wr2)
    logits = t1[:, :LANES] + t1[:, LANES:] + _dot(xl, wr2[:, :LANES]) + br_ref[...]
    lane = lax.broadcasted_iota(I32, logits.shape, 1)
    neg = jnp.float32(-jnp.inf)
    logits = jnp.where(lane < n_exp, logits, neg)
    vals, hots = [], []
    e_out = jnp.zeros(logits.shape, I32)
    for j in range(TOP_K):
        mx = jnp.max(logits, axis=-1, keepdims=True)
        idx = jnp.min(jnp.where(logits == mx, lane, LANES), axis=-1, keepdims=True)
        hot = lane == idx
        vals.append(mx)
        hots.append(hot)
        e_out = jnp.where(lane == j, idx, e_out)
        logits = jnp.where(hot, neg, logits)
    ex = [jnp.exp(v - vals[0]) for v in vals]
    den = ex[0] + ex[1] + ex[2] + ex[3]
    w_out = jnp.zeros(logits.shape, F32)
    for j in range(TOP_K):
        w_out = jnp.where(lane == j, ex[j] / den, w_out)
    osum = jnp.where(hots[0] | hots[1] | hots[2] | hots[3], 1.0, 0.0)
    before = _dot(lt_ref[...], osum.astype(BF16)) + run_ref[0:1, :]
    rk_out = jnp.zeros(logits.shape, I32)
    for j in range(TOP_K):
        rj = jnp.sum(jnp.where(hots[j], before, 0.0), axis=-1, keepdims=True)
        rk_out = jnp.where(lane == j, rj.astype(I32), rk_out)
    run = run_ref[0:1, :] + jnp.sum(osum, axis=0, keepdims=True)
    run_ref[...] = jnp.broadcast_to(run, run_ref.shape)
    code = e_out * RANK_LIMIT + rk_out
    code_ref[...] = code.T[:code_ref.shape[0], :]
    wt_ref[...] = w_out
    cnt_ref[...] = jnp.broadcast_to(run, cnt_ref.shape).astype(I32)


def _route_call(gla, pool, x, mod, w_out, nw, wr, br, *, tm, n_exp, first_tile, n_tiles):
    n_all, d = x.shape
    n = n_tiles * tm
    gw = gla.shape[1]
    t_per_b = n_all // mod.shape[0]
    lt = jnp.asarray(np.tril(np.ones((tm, tm), np.float32), -1), BF16)
    row_in = lambda width: pl.BlockSpec((tm, width), lambda i: (i + first_tile, 0))
    row = lambda width: pl.BlockSpec((tm, width), lambda i: (i, 0))
    const = lambda shape: pl.BlockSpec(shape, lambda i: (0,) * len(shape))
    sds = jax.ShapeDtypeStruct
    return pl.pallas_call(
        functools.partial(_route_kernel, gw=gw, n_exp=n_exp),
        grid=(n_tiles,),
        in_specs=[row_in(gw), row_in(pool.shape[1]), row_in(d),
                  pl.BlockSpec((1, N_MOD, d), lambda i: ((i + first_tile) * tm // t_per_b, 0, 0)),
                  const(w_out.shape), const((1, d)), const(wr.shape), const((1, LANES)), const((tm, tm))],
        out_specs=[row(d), row(d // 2), pl.BlockSpec((8, tm), lambda i: (0, i)), row(LANES), const((8, LANES))],
        out_shape=[sds((n, d), F32), sds((n, d // 2), U32), sds((8, n), I32),
                   sds((n, LANES), F32), sds((8, LANES), I32)],
        scratch_shapes=[pltpu.VMEM((8, LANES), F32), pltpu.VMEM((d, 2 * LANES), BF16)],
        compiler_params=pltpu.CompilerParams(dimension_semantics=("arbitrary",)),
        name="route",
    )(gla, pool, x, mod, w_out, nw, wr, br, lt)


def _plan_kernel(cnt_ref, code_ref, dest_ref, be_ref, nv_ref, first_ref, next_ref, slot_ref, nu_ref, start_ref,
                 *, n_exp, rows):
    @pl.when(pl.program_id(0) == 0)
    def _():
        blk = (lax.broadcasted_iota(I32, be_ref.shape, 0) * LANES + lax.broadcasted_iota(I32, be_ref.shape, 1))
        blk_row0 = blk * rows
        nxt_e = [None] * n_exp
        nxt = jnp.int32(-1)
        for e in reversed(range(n_exp)):
            nxt_e[e] = nxt
            nxt = jnp.where(cnt_ref[e] > 0, e, nxt)
        zeros = jnp.zeros(be_ref.shape, I32)
        be, end_valid, first, nxt_blk, slot = zeros, zeros, zeros, zeros - 1, zeros
        acc = jnp.int32(0)
        last_e = jnp.int32(0)
        ordinal = jnp.int32(0)
        for e in range(n_exp):
            c = cnt_ref[e]
            start_ref[e] = acc
            in_e = (blk_row0 >= acc) & (c > 0)
            end_valid = jnp.where(in_e, acc + c, end_valid)
            be = jnp.where(in_e, e, be)
            first = jnp.where(in_e, (blk_row0 == acc).astype(I32), first)
            nxt_blk = jnp.where(in_e, nxt_e[e], nxt_blk)
            slot = jnp.where(in_e, ordinal % 2, slot)
            acc = acc + (c + rows - 1) // rows * rows
            last_e = jnp.where(c > 0, e, last_e)
            ordinal = ordinal + (c > 0).astype(I32)
        n_used = acc // rows
        nu_ref[0] = n_used
        be_ref[...] = jnp.where(blk < n_used, be, last_e)
        nv_ref[...] = jnp.clip(end_valid - blk_row0, 0, rows)
        first_ref[...] = first
        next_ref[...] = nxt_blk
        slot_ref[...] = slot

    code = code_ref[...]
    e_vec = code // RANK_LIMIT
    dest = code % RANK_LIMIT
    for e in range(n_exp):
        dest = dest + jnp.where(e_vec == e, start_ref[e], 0)
    dest_ref[...] = dest


def _plan_call(counts, code_t, *, n_exp, rows, chunk):
    n = code_t.shape[1]
    sds = jax.ShapeDtypeStruct
    smem = pltpu.SMEM
    return pl.pallas_call(
        functools.partial(_plan_kernel, n_exp=n_exp, rows=rows),
        grid=(n // chunk,),
        in_specs=[pl.BlockSpec(memory_space=smem), pl.BlockSpec((8, chunk), lambda i: (0, i))],
        out_specs=[pl.BlockSpec((8, chunk), lambda i: (0, i))]
        + [pl.BlockSpec((8, LANES), lambda i: (0, 0))] * 5 + [pl.BlockSpec(memory_space=smem)],
        out_shape=[sds((8, n), I32)] + [sds((8, LANES), I32)] * 5 + [sds((1,), I32)],
        scratch_shapes=[pltpu.SMEM((n_exp,), I32)],
        compiler_params=pltpu.CompilerParams(dimension_semantics=("arbitrary",)),
        name="plan",
    )(counts, code_t)


def _sc_worker_id():
    return lax.axis_index("s") * SC_CORES + lax.axis_index("c")


def _sc_scatter_call(x, idx3, *, n_out):
    n, d = x.shape
    n_win_total, k, w = idx3.shape
    n_win = n_win_total // SC_WORKERS
    mesh = plsc.VectorSubcoreMesh(core_axis_name="c", subcore_axis_name="s")

    @functools.partial(
        pl.kernel, mesh=mesh,
        out_type=jax.ShapeDtypeStruct((n_out, d), x.dtype),
        scratch_types=[pltpu.VMEM((k, w), I32), pltpu.VMEM((w, d), x.dtype), pltpu.SemaphoreType.DMA],
        name="sc_dispatch",
    )
    def kern(x_hbm, idx_hbm, out_hbm, idx_v, rows_v, sem):
        wid = _sc_worker_id()

        @pl.loop(0, n_win)
        def _(i):
            win = wid * n_win + i
            pltpu.sync_copy(idx_hbm.at[win], idx_v)
            pltpu.sync_copy(x_hbm.at[pl.ds(win * w, w)], rows_v)
            for j in range(k):
                pltpu.async_copy(rows_v, out_hbm.at[idx_v.at[j]], sem).wait()

    return kern(x, idx3)


def _sc_gather_call(table, idx3):
    n_workers, n_win, w = idx3.shape
    d = table.shape[1]
    assert n_workers == SC_WORKERS and n_win % 2 == 0
    mesh = plsc.VectorSubcoreMesh(core_axis_name="c", subcore_axis_name="s")

    @functools.partial(
        pl.kernel, mesh=mesh,
        out_type=jax.ShapeDtypeStruct((n_workers * n_win * w, d), table.dtype),
        scratch_types=[pltpu.VMEM((n_win, w), I32), pltpu.VMEM((2, w, d), table.dtype),
                       pltpu.SemaphoreType.DMA((2,)), pltpu.SemaphoreType.DMA((2,))],
        name="sc_gather",
    )
    def kern(table_hbm, idx_hbm, out_hbm, idx_v, rows_v, gsem, osem):
        wid = _sc_worker_id()
        base = wid * n_win
        pltpu.sync_copy(idx_hbm.at[wid], idx_v)

        def gather(wi, b):
            return pltpu.make_async_copy(table_hbm.at[idx_v.at[wi]], rows_v.at[b], gsem.at[b])

        def put(wi, b):
            return pltpu.make_async_copy(rows_v.at[b], out_hbm.at[pl.ds((base + wi) * w, w)], osem.at[b])

        gather(0, 0).start()

        @pl.loop(0, n_win, step=2)
        def _(i):
            for b in range(2):
                wi = i + b

                @pl.when(wi + 1 < n_win)
                def _():
                    @pl.when(wi >= 1)
                    def _():
                        put(wi - 1, 1 - b).wait()
                    gather(wi + 1, 1 - b).start()

                gather(wi, b).wait()
                put(wi, b).start()

        put(n_win - 2, 0).wait()
        put(n_win - 1, 1).wait()

    return kern(table, idx3)


def _expert_kernel(be_ref, nu_ref, nv_ref, first_ref, next_ref, slot_ref,
                   x_ref, wgu_hbm, bgu_ref, wd_hbm, bd_ref, y_ref,
                   wgu_f32_ref, wd_f32_ref, wgu_bf_ref, wd_bf_ref, sem, *, d_ff):
    i = pl.program_id(0)
    slot = slot_ref[i]

    def fetch(e, s):
        return (pltpu.make_async_copy(wgu_hbm.at[e], wgu_f32_ref.at[s], sem.at[0, s]),
                pltpu.make_async_copy(wd_hbm.at[e], wd_f32_ref.at[s], sem.at[1, s]))

    @pl.when(i == 0)
    def _():
        for cp in fetch(be_ref[0], slot):
            cp.start()

    @pl.when(first_ref[i] == 1)
    def _():
        for cp in fetch(be_ref[i], slot):
            cp.wait()

        @pl.when(next_ref[i] >= 0)
        def _():
            for cp in fetch(next_ref[i], 1 - slot):
                cp.start()

        wgu_bf_ref[...] = wgu_f32_ref[slot].astype(BF16)
        wd_bf_ref[...] = wd_f32_ref[slot].astype(BF16)

    n_valid = nv_ref[i]
    n_rows = x_ref.shape[0]

    def expert_rows(rows):
        row = lax.broadcasted_iota(I32, (rows, x_ref.shape[1]), 0)
        lo, hi = _unpack_bf16_pairs(jnp.where(row < n_valid, x_ref[0:rows, :], jnp.uint32(0)))
        xb = jnp.concatenate([lo, hi], axis=1).astype(BF16)
        gu = _dot(xb, wgu_bf_ref[...]) + bgu_ref[0]
        gate = jnp.minimum(gu[:, :d_ff], SWIGLU_LIMIT)
        up = jnp.clip(gu[:, d_ff:], -SWIGLU_LIMIT, SWIGLU_LIMIT)
        act = (up + 1.0) * gate * jax.nn.sigmoid(SWIGLU_ALPHA * gate)
        y_ref[0:rows, :] = _pack_bf16_pairs(_dot(act.astype(BF16), wd_bf_ref[...]) + bd_ref[0])

    in_use = i < nu_ref[0]

    for rows in range(EXPERT_ROW_STEP, n_rows + 1, EXPERT_ROW_STEP):
        @pl.when(in_use & (n_valid > rows - EXPERT_ROW_STEP) & (n_valid <= rows))
        def _(rows=rows):
            expert_rows(rows)
            if rows < n_rows:
                y_ref[rows:, :] = jnp.zeros((n_rows - rows, y_ref.shape[1]), y_ref.dtype)


def _expert_call(plan, xs, w_gu, b_gu, w_down, b_down):
    block_e, n_valid, first, nxt, slot, n_used = plan
    n_pad = xs.shape[0]
    n_exp, d, two_ff = w_gu.shape
    d_ff = two_ff // 2
    nblk = n_pad // EXPERT_ROWS
    rows = lambda i, be, nu, *_: (jnp.minimum(i, nu[0] - 1), 0)
    per_e = lambda i, be, *_: (be[i], 0, 0)
    grid_spec = pltpu.PrefetchScalarGridSpec(
        num_scalar_prefetch=6,
        grid=(nblk,),
        in_specs=[pl.BlockSpec((EXPERT_ROWS, d // 2), rows),
                  pl.BlockSpec(memory_space=pl.ANY), pl.BlockSpec((1, 1, two_ff), per_e),
                  pl.BlockSpec(memory_space=pl.ANY), pl.BlockSpec((1, 1, d), per_e)],
        out_specs=pl.BlockSpec((EXPERT_ROWS, d // 2), rows),
        scratch_shapes=[pltpu.VMEM((2, d, two_ff), F32), pltpu.VMEM((2, d_ff, d), F32),
                        pltpu.VMEM((d, two_ff), BF16), pltpu.VMEM((d_ff, d), BF16),
                        pltpu.SemaphoreType.DMA((2, 2))],
    )
    flat = lambda a: a.reshape(-1)[:nblk]
    return pl.pallas_call(
        functools.partial(_expert_kernel, d_ff=d_ff),
        grid_spec=grid_spec,
        out_shape=jax.ShapeDtypeStruct((n_pad, d // 2), U32),
        compiler_params=pltpu.CompilerParams(dimension_semantics=("arbitrary",),
                                             vmem_limit_bytes=EXPERT_VMEM_BYTES),
        name="experts",
    )(flat(block_e), n_used, flat(n_valid), flat(first), flat(nxt), flat(slot),
      xs, w_gu, b_gu.reshape(n_exp, 1, two_ff), w_down, b_down.reshape(n_exp, 1, d))


def _combine_kernel(y4_ref, wt_ref, h_ref, mod_ref, fw_ref, o_ref):
    wt = wt_ref[...]
    acc_lo, acc_hi = None, None
    for j in range(TOP_K):
        lo, hi = _unpack_bf16_pairs(y4_ref[j])
        w = wt[:, j:j + 1]
        acc_lo = w * lo if j == 0 else acc_lo + w * lo
        acc_hi = w * hi if j == 0 else acc_hi + w * hi
    acc = jnp.concatenate([acc_lo, acc_hi], axis=1)
    m = mod_ref[0]
    o_ref[...] = _rmsnorm(h_ref[...] + m[5:6] * acc, fw_ref[...])


def _combine_call(y4, wts, h, mod, fw, prev_out, *, n, tg, first_tile):
    d = h.shape[1]
    t_per_b = n // mod.shape[0]
    part = lambda width: pl.BlockSpec((tg, width), lambda i: (i, 0))
    row = lambda width: pl.BlockSpec((tg, width), lambda i: (i + first_tile, 0))
    in_specs = [pl.BlockSpec((TOP_K, tg, d // 2), lambda i: (0, i, 0)),
                part(LANES), part(d),
                pl.BlockSpec((1, N_MOD, d), lambda i: ((i + first_tile) * tg // t_per_b, 0, 0)),
                pl.BlockSpec((1, d), lambda i: (0, 0))]
    args = [y4, wts, h, mod, fw]
    kern = _combine_kernel
    aliases = {}
    if prev_out is not None:
        in_specs.append(pl.BlockSpec(memory_space=pl.ANY))
        args.append(prev_out)
        kern = lambda y4_ref, wt_ref, h_ref, mod_ref, fw_ref, prev_ref, o_ref: _combine_kernel(
            y4_ref, wt_ref, h_ref, mod_ref, fw_ref, o_ref)
        aliases = {len(args) - 1: 0}
    return pl.pallas_call(
        kern,
        grid=(y4.shape[1] // tg,),
        in_specs=in_specs,
        out_specs=row(d),
        out_shape=jax.ShapeDtypeStruct((n, d), F32),
        input_output_aliases=aliases,
        compiler_params=pltpu.CompilerParams(dimension_semantics=("arbitrary",)),
        name="combine",
    )(*args)


def kernel(x, c, ctx, c_ctx, w_ada, b_ada, norm_mix_w, norm_mlp_w, w_in, w_gk_f, b_gk_f, w_gk_b, b_gk_b,
           gla_norm_w, w_pool, pool_scale, w_out, w_router, b_router, w_gu, b_gu, w_down, b_down,
           final_norm_w):
    b, t, d = x.shape
    assert w_ada.shape[0] == 1, "single-layer trunk"
    n_exp = w_router.shape[2]
    rank = w_gk_f.shape[1]
    qk = w_gk_f.shape[2]
    dk = qk // GLA_HEADS
    gw = GLA_HEADS * gla_norm_w.shape[1]
    pw = w_pool.shape[1] * w_pool.shape[2]
    assert w_in.shape[2] == 2 * qk + 2 * gw + 2 * rank + pw and 2 * rank <= LANES
    assert t % SUPER == 0 and ctx.shape[1] % SUPER == 0 and n_exp <= LANES

    rows = -(-(b + 1) // 8) * 8
    cc = jnp.zeros((rows, d), F32).at[:b].set(c).at[b].set(c_ctx)
    mod = _mod_call(cc, w_ada[0], b_ada)
    mod_x = mod[:b].reshape(b, N_MOD, d)
    mod_c = mod[b:b + 1].reshape(1, N_MOD, d)

    wi = w_in[0]
    o_r = 2 * qk + 2 * gw
    w_cat = jnp.concatenate([wi[:, :o_r], wi[:, o_r + 2 * rank:], wi[:, o_r:o_r + 2 * rank],
                             jnp.zeros((d, LANES - 2 * rank), F32)], axis=1).astype(BF16)
    wgk = jnp.zeros((LANES, 2 * qk), F32).at[:rank, :qk].set(w_gk_f[0]).at[rank:2 * rank, qk:].set(w_gk_b[0])
    bgk = jnp.concatenate([b_gk_f[0], b_gk_b[0]])[None, :]
    proj = functools.partial(_inproj_call, nw=norm_mix_w, w=w_cat, wgk=wgk.astype(BF16), bgk=bgk,
                             qk=qk, gw=gw, pw=pw, dk=dk)
    q, k, v, vt, g, xp, gk = proj(x, mod_x, tm=1024)
    _, kc, _, vtc, _, _, gkc = proj(ctx, mod_c, tm=SUPER)

    gla = _gla_call(q, k, v, vt, gk, g, kc, vtc, gkc, gla_norm_w)
    pool = _pool_call(xp, w_pool[0].astype(BF16), pool_scale)

    n = b * t
    wr = jnp.zeros((d, LANES), F32).at[:, :n_exp].set(w_router[0])
    br = jnp.zeros((1, LANES), F32).at[0, :n_exp].set(b_router[0])
    n_part = n // MOE_PARTS
    tm, tg = 512, 256
    assert n % MOE_PARTS == 0 and n_part % (SC_WORKERS * SC_WINDOW) == 0 and n_part % tm == 0 and n_part < RANK_LIMIT
    assert (n_part * TOP_K) % (SC_WORKERS * 2 * SC_GATHER_WINDOW) == 0
    n_pad = n_part * TOP_K + n_exp * EXPERT_ROWS
    nblk = n_pad // EXPERT_ROWS
    assert nblk <= 8 * LANES
    parts = range(MOE_PARTS)
    routed = [_route_call(gla.reshape(n, gw), pool.reshape(n, pw), x.reshape(n, d), mod_x, w_out[0].astype(BF16),
                          norm_mlp_w, wr, br, tm=tm, n_exp=n_exp,
                          first_tile=p * n_part // tm, n_tiles=n_part // tm) for p in parts]
    plans = [_plan_call(cnt[0, :n_exp], code_t, n_exp=n_exp, rows=EXPERT_ROWS, chunk=min(n_part, 4096))
             for (_, _, code_t, _, cnt) in routed]
    dests = [plan[0][:TOP_K] for plan in plans]
    xs = [_sc_scatter_call(routed[p][1],
                           dests[p].reshape(TOP_K, n_part // SC_WINDOW, SC_WINDOW).transpose(1, 0, 2), n_out=n_pad)
          for p in parts]
    ys = [_expert_call(plans[p][1:], xs[p], w_gu[0], b_gu[0], w_down[0], b_down[0]) for p in parts]
    y4 = [_sc_gather_call(ys[p], dests[p].reshape(SC_WORKERS, -1, SC_GATHER_WINDOW)).reshape(TOP_K, n_part, d // 2)
          for p in parts]
    out = None
    for p in parts:
        out = _combine_call(y4[p], routed[p][3], routed[p][0], mod_x, final_norm_w[None, :], out,
                            n=n, tg=tg, first_tile=p * n_part // tg)
    return out.reshape(b, t, d)
```

```python
import functools

import numpy as np
import jax
import jax.numpy as jnp
from jax import lax
from jax.experimental import pallas as pl
from jax.experimental.pallas import tpu as pltpu
from jax.experimental.pallas import tpu_sc as plsc

F32 = jnp.float32
BF16 = jnp.bfloat16
I32 = jnp.int32
U32 = jnp.uint32

GRID_W = 64
GLA_HEADS = 4
GLA_CHUNK = 64
GATE_NORMALIZER = 16.0
POOL_WINDOWS = (2, 4, 8, 16)
POOL_PAD_GRID_ROWS = 8
TOP_K = 4
RANK_LIMIT = 1 << 20
SWIGLU_LIMIT = 7.0
SWIGLU_ALPHA = 1.702
N_MOD = 6
EPS = 1e-6

LANES = 128
SUPER = 256
HEAD_PAIR_DK = 128
EXPERT_ROWS = 512
EXPERT_ROW_STEP = 128
MOE_PARTS = 2
EXPERT_VMEM_BYTES = 56 * 1024 * 1024
SC_CORES = 2
SC_SUBCORES = 16
SC_WORKERS = SC_CORES * SC_SUBCORES
SC_WINDOW = 32
SC_GATHER_WINDOW = 64


def _dot(a, b):
    return jnp.dot(a, b, preferred_element_type=F32)


def _dot_nt(a, b):
    return lax.dot_general(a, b, (((1,), (1,)), ((), ())), preferred_element_type=F32)


def _split_bf16(x):
    hi = x.astype(BF16)
    lo = (x - hi.astype(F32)).astype(BF16)
    return hi, lo


def _pack_bf16_pairs(x):
    c = x.shape[1] // 2
    lo = lax.bitcast_convert_type(x[:, :c].astype(BF16).astype(F32), U32)
    hi = lax.bitcast_convert_type(x[:, c:].astype(BF16).astype(F32), U32)
    return (lo >> 16) | hi


def _unpack_bf16_pairs(p):
    lo = lax.bitcast_convert_type(p << 16, F32)
    hi = lax.bitcast_convert_type(p & jnp.uint32(0xFFFF0000), F32)
    return lo, hi


def _rmsnorm(x, w):
    var = jnp.mean(x * x, axis=-1, keepdims=True)
    return x * lax.rsqrt(var + EPS) * w


def _mod_kernel(c_ref, w_ref, b_ref, o_ref):
    c = c_ref[...]
    s = c * jax.nn.sigmoid(c)
    o_ref[...] = jnp.dot(s, w_ref[...], precision=lax.Precision.HIGHEST,
                         preferred_element_type=F32) + b_ref[...]


def _mod_call(cc, w_ada, b_ada):
    rows, d = cc.shape
    n = w_ada.shape[1]
    tn = 1024
    return pl.pallas_call(
        _mod_kernel,
        grid=(n // tn,),
        in_specs=[pl.BlockSpec((rows, d), lambda j: (0, 0)),
                  pl.BlockSpec((d, tn), lambda j: (0, j)),
                  pl.BlockSpec((1, tn), lambda j: (0, j))],
        out_specs=pl.BlockSpec((rows, tn), lambda j: (0, j)),
        out_shape=jax.ShapeDtypeStruct((rows, n), F32),
        name="mod",
    )(cc, w_ada, b_ada)


def _inproj_kernel(x_ref, mod_ref, nw_ref, w_ref, wgk_ref, bgk_ref,
                   q_ref, k_ref, v_ref, vt_ref, g_ref, p_ref, gk_ref, *, qk, gw, pw, dk):
    x = x_ref[0]
    m = mod_ref[0]
    hm = (_rmsnorm(x, nw_ref[...]) * (1.0 + m[1:2]) + m[0:1]).astype(BF16)
    p = _dot(hm, w_ref[...])
    vt = p[:, 2 * qk:2 * qk + gw].T
    for s in range(vt_ref.shape[1]):
        vt_ref[0, s] = vt[:, s * SUPER:(s + 1) * SUPER].astype(BF16)
    o = 0
    q_ref[0] = (p[:, o:o + qk] * (dk ** -0.5)).astype(BF16); o += qk
    k_ref[0] = p[:, o:o + qk].astype(BF16); o += qk
    v_ref[0] = p[:, o:o + gw].astype(BF16); o += gw
    g_ref[0] = p[:, o:o + gw].astype(BF16); o += gw
    p_ref[0] = p[:, o:o + pw].astype(BF16); o += pw
    r = p[:, o:o + LANES]
    z = _dot(r.astype(BF16), wgk_ref[...]) + bgk_ref[...]
    gk_ref[0] = (jnp.minimum(z, 0.0) - jnp.log1p(jnp.exp(-jnp.abs(z)))) * (1.0 / GATE_NORMALIZER)


def _inproj_call(x, mod, nw, w, wgk, bgk, *, qk, gw, pw, dk, tm):
    b, t, d = x.shape
    n_in = w.shape[1]
    bs = lambda width: pl.BlockSpec((1, tm, width), lambda i, j: (i, j, 0))
    const = lambda shape: pl.BlockSpec(shape, lambda i, j: (0,) * len(shape))
    per_batch = mod.shape[0] > 1
    sds = jax.ShapeDtypeStruct
    return pl.pallas_call(
        functools.partial(_inproj_kernel, qk=qk, gw=gw, pw=pw, dk=dk),
        grid=(b, t // tm),
        in_specs=[bs(d),
                  pl.BlockSpec((1, N_MOD, d), (lambda i, j: (i, 0, 0)) if per_batch else (lambda i, j: (0, 0, 0))),
                  const((1, d)), const((d, n_in)), const((LANES, 2 * qk)), const((1, 2 * qk))],
        out_specs=[bs(qk), bs(qk), bs(gw),
                   pl.BlockSpec((1, tm // SUPER, gw, SUPER), lambda i, j: (i, j, 0, 0)),
                   bs(gw), bs(pw), bs(2 * qk)],
        out_shape=[sds((b, t, qk), BF16), sds((b, t, qk), BF16), sds((b, t, gw), BF16),
                   sds((b, t // SUPER, gw, SUPER), BF16),
                   sds((b, t, gw), BF16), sds((b, t, pw), BF16), sds((b, t, 2 * qk), F32)],
        compiler_params=pltpu.CompilerParams(dimension_semantics=("arbitrary", "arbitrary")),
        name="inproj",
    )(x, mod, nw, w, wgk, bgk)


def _gla_super(q, k, v, vt, gk, cm, amask, bd_mask, st, fwd, want_out):
    nch = SUPER // GLA_CHUNK
    order = tuple(range(nch)) if fwd else tuple(reversed(range(nch)))
    last_row = GLA_CHUNK - 1 if fwd else 0
    mid_row = GLA_CHUNK // 2 - 1 if fwd else GLA_CHUNK // 2
    hi, lo = _split_bf16(gk)
    bcum = _dot(cm, hi) + _dot(cm, lo)

    def chunk_row(r):
        return jnp.concatenate(
            [jnp.broadcast_to(bcum[c * GLA_CHUNK + r:c * GLA_CHUNK + r + 1, :], (GLA_CHUNK, bcum.shape[1]))
             for c in range(nch)], axis=0)

    chunk_of_row = lax.broadcasted_iota(I32, bcum.shape, 0) // GLA_CHUNK

    def by_chunk(x):
        return jnp.concatenate([jnp.where(chunk_of_row == c, x, 0.0).astype(BF16) for c in range(nch)], axis=1)

    blast = chunk_row(last_row)
    u_all = _dot(vt, by_chunk(k * jnp.exp(blast - bcum)))
    before = [None] * nch
    for c in order:
        before[c] = st
        decay = jnp.exp(bcum[c * GLA_CHUNK + last_row:c * GLA_CHUNK + last_row + 1, :])
        st = st * decay + jnp.where(bd_mask, u_all[:, c * HEAD_PAIR_DK:(c + 1) * HEAD_PAIR_DK], 0.0)
    if not want_out:
        return None, st
    bmid = chunk_row(mid_row)
    qt = q * jnp.exp(bcum - bmid)
    kt = (k * jnp.exp(bmid - bcum)).astype(BF16)
    lane = lax.broadcasted_iota(I32, qt.shape, 1)
    half = HEAD_PAIR_DK // 2
    o_heads = []
    for hh in range(2):
        sel = (lane < half) if hh == 0 else (lane >= half)
        a = _dot_nt(jnp.where(sel, qt, 0.0).astype(BF16), kt)
        a = jnp.where(amask, a, 0.0).astype(BF16)
        o_heads.append(_dot(a, v[:, hh * LANES:(hh + 1) * LANES]))
    qh = (q * jnp.exp(bcum)).astype(BF16)
    o_inter = jnp.concatenate(
        [_dot_nt(qh[c * GLA_CHUNK:(c + 1) * GLA_CHUNK], before[c].astype(BF16)) for c in range(nch)], axis=0)
    return jnp.concatenate(o_heads, axis=1) + o_inter, st


def _gla_kernel(q_ref, k_ref, v_ref, vt_ref, gkf_ref, gkb_ref, g_ref, kc_ref, vtc_ref, gkfc_ref, gkbc_ref,
                nw_ref, cmf_ref, cmb_ref, o_ref, stf_ref, stb_ref, of_ref, ob_ref):
    t = q_ref.shape[1]
    tc = kc_ref.shape[1]
    nsc, nscc = t // SUPER, tc // SUPER
    cmf = cmf_ref[...]
    cmb = cmb_ref[...]
    amask_f = cmf > 0
    amask_b = cmb > 0
    row = lax.broadcasted_iota(I32, (2 * LANES, HEAD_PAIR_DK), 0)
    lane = lax.broadcasted_iota(I32, (2 * LANES, HEAD_PAIR_DK), 1)
    bd_mask = (row < LANES) == (lane < HEAD_PAIR_DK // 2)

    def ctx_state(gk_ref, cm, fwd, j, st):
        rows = pl.ds(j * SUPER, SUPER)
        return _gla_super(None, kc_ref[0, rows, :].astype(F32), None, vtc_ref[0, j], gk_ref[0, rows, :],
                          cm, None, bd_mask, st, fwd, False)[1]

    def latent(gk_ref, cm, amask, fwd, j, st):
        rows = pl.ds(pl.multiple_of(j * SUPER, SUPER), SUPER)
        return _gla_super(q_ref[0, rows, :].astype(F32), k_ref[0, rows, :].astype(F32), v_ref[0, rows, :],
                          vt_ref[0, j], gk_ref[0, rows, :], cm, amask, bd_mask, st, fwd, True)

    st = jnp.zeros(stf_ref.shape, F32)
    for j in range(nscc):
        st = ctx_state(gkfc_ref, cmf, True, j, st)
    stf_ref[...] = st
    st = jnp.zeros(stb_ref.shape, F32)
    for j in reversed(range(nscc)):
        st = ctx_state(gkbc_ref, cmb, False, j, st)
    stb_ref[...] = st

    def scan_body(jj, carry):
        jb = nsc - 1 - jj
        of, stf = latent(gkf_ref, cmf, amask_f, True, jj, stf_ref[...])
        of_ref[pl.ds(pl.multiple_of(jj * SUPER, SUPER), SUPER), :] = of
        stf_ref[...] = stf
        ob, stb = latent(gkb_ref, cmb, amask_b, False, jb, stb_ref[...])
        ob_ref[pl.ds(pl.multiple_of(jb * SUPER, SUPER), SUPER), :] = ob
        stb_ref[...] = stb
        return carry

    lax.fori_loop(0, nsc, scan_body, 0, unroll=2)

    nw = nw_ref[...]

    def out_body(j, carry):
        rows = pl.ds(pl.multiple_of(j * SUPER, SUPER), SUPER)
        o = of_ref[rows, :] + ob_ref[rows, :]
        g = g_ref[0, rows, :].astype(F32)
        gate = g * jax.nn.sigmoid(g)
        for hh in range(2):
            oh = o[:, hh * LANES:(hh + 1) * LANES]
            on = oh * lax.rsqrt(jnp.mean(oh * oh, axis=-1, keepdims=True) + EPS) * nw
            o_ref[0, rows, hh * LANES:(hh + 1) * LANES] = (on * gate[:, hh * LANES:(hh + 1) * LANES]).astype(BF16)
        return carry

    lax.fori_loop(0, nsc, out_body, 0)


def _gla_masks():
    i = np.arange(SUPER)
    same = (i[:, None] // GLA_CHUNK) == (i[None, :] // GLA_CHUNK)
    fwd = same & (i[None, :] <= i[:, None])
    bwd = same & (i[None, :] >= i[:, None])
    return jnp.asarray(fwd, BF16), jnp.asarray(bwd, BF16)


def _gla_call(q, k, v, vt, gk, g, kc, vtc, gkc, nw):
    b, t, qk = q.shape
    tc = kc.shape[1]
    npair = qk // HEAD_PAIR_DK
    cmf, cmb = _gla_masks()
    lat = lambda width, off: pl.BlockSpec((1, t, width), lambda i, j: (i, 0, j + off))
    ctx = lambda width, off: pl.BlockSpec((1, tc, width), lambda i, j: (i, 0, j + off))
    tr = lambda n_groups: pl.BlockSpec((1, n_groups, 2 * LANES, SUPER), lambda i, j: (i, 0, j, 0))
    const = lambda shape: pl.BlockSpec(shape, lambda i, j: (0,) * len(shape))
    return pl.pallas_call(
        _gla_kernel,
        grid=(b, npair),
        in_specs=[lat(HEAD_PAIR_DK, 0), lat(HEAD_PAIR_DK, 0), lat(2 * LANES, 0), tr(t // SUPER),
                  lat(HEAD_PAIR_DK, 0), lat(HEAD_PAIR_DK, npair), lat(2 * LANES, 0),
                  ctx(HEAD_PAIR_DK, 0), tr(tc // SUPER), ctx(HEAD_PAIR_DK, 0), ctx(HEAD_PAIR_DK, npair),
                  const((1, LANES)), const(cmf.shape), const(cmb.shape)],
        out_specs=lat(2 * LANES, 0),
        out_shape=jax.ShapeDtypeStruct((b, t, v.shape[2]), BF16),
        scratch_shapes=[pltpu.VMEM((2 * LANES, HEAD_PAIR_DK), F32), pltpu.VMEM((2 * LANES, HEAD_PAIR_DK), F32),
                        pltpu.VMEM((t, 2 * LANES), F32), pltpu.VMEM((t, 2 * LANES), F32)],
        compiler_params=pltpu.CompilerParams(dimension_semantics=("arbitrary", "arbitrary")),
        name="gla",
    )(q, k, v, vt, gk, gk, g, kc, vtc, gkc, gkc, nw, cmf, cmb)


def _pool_kernel(x_ref, cm_ref, cnt_ref, wp_ref, ps_ref, o_ref, a_ref, b_ref):
    t = x_ref.shape[1]
    pad = POOL_PAD_GRID_ROWS * GRID_W
    total = t + 2 * pad
    for gi, w in enumerate(POOL_WINDOWS):
        lo = w // 2
        cols = slice(gi * LANES, (gi + 1) * LANES)
        cmat = cm_ref[gi]
        a_ref[0:pad, :] = jnp.zeros((pad, LANES), F32)
        a_ref[pad + t:total, :] = jnp.zeros((pad, LANES), F32)
        for blk in range(t // SUPER):
            rs = slice(blk * SUPER, (blk + 1) * SUPER)
            a_ref[pad + blk * SUPER:pad + (blk + 1) * SUPER, :] = _dot(cmat, x_ref[0, rs, cols])
        src, dst = a_ref, b_ref
        m = 1
        while m < w:
            sh = m * GRID_W
            dst[0:total - sh, :] = src[0:total - sh, :] + src[sh:total, :]
            src, dst = dst, src
            m *= 2
        first = pad - lo * GRID_W
        pooled = src[first:first + t, :] / cnt_ref[gi] - x_ref[0, :, cols].astype(F32)
        yp = _dot(pooled.astype(BF16), wp_ref[gi]) * ps_ref[:, cols]
        o_ref[0, :, cols] = yp.astype(BF16)


def _pool_col_mats():
    i = np.arange(SUPER)
    same_row = (i[:, None] // GRID_W) == (i[None, :] // GRID_W)
    d = i[None, :] - i[:, None]
    mats = []
    for w in POOL_WINDOWS:
        lo = w // 2
        hi = w - 1 - lo
        mats.append(same_row & (d >= -lo) & (d <= hi))
    return jnp.asarray(np.stack(mats), BF16)


def _pool_counts(t):
    rows = t // GRID_W
    r = np.arange(t) // GRID_W
    c = np.arange(t) % GRID_W
    out = []
    for w in POOL_WINDOWS:
        lo = w // 2
        hi = w - 1 - lo
        cnt_r = np.minimum(r + hi + 1, rows) - np.maximum(r - lo, 0)
        cnt_c = np.minimum(c + hi + 1, GRID_W) - np.maximum(c - lo, 0)
        out.append(np.broadcast_to((cnt_r * cnt_c).astype(np.float32)[:, None], (t, LANES)))
    return jnp.asarray(np.stack(out))


def _pool_call(xp, w_pool, pool_scale):
    b, t, pw = xp.shape
    ng = len(POOL_WINDOWS)
    assert max(POOL_WINDOWS) // 2 <= POOL_PAD_GRID_ROWS and t % GRID_W == 0
    cm = _pool_col_mats()
    cnt = _pool_counts(t)
    staged = t + 2 * POOL_PAD_GRID_ROWS * GRID_W
    const = lambda shape: pl.BlockSpec(shape, lambda i: (0,) * len(shape))
    return pl.pallas_call(
        _pool_kernel,
        grid=(b,),
        in_specs=[pl.BlockSpec((1, t, pw), lambda i: (i, 0, 0)),
                  const(cm.shape), const(cnt.shape), const((ng, LANES, LANES)), const((1, pw))],
        out_specs=pl.BlockSpec((1, t, pw), lambda i: (i, 0, 0)),
        out_shape=jax.ShapeDtypeStruct((b, t, pw), BF16),
        scratch_shapes=[pltpu.VMEM((staged, LANES), F32), pltpu.VMEM((staged, LANES), F32)],
        compiler_params=pltpu.CompilerParams(dimension_semantics=("arbitrary",)),
        name="pool",
    )(xp, cm, cnt, w_pool, pool_scale)


def _route_kernel(gla_ref, pool_ref, x_ref, mod_ref, wo_ref, nw_ref, wr_ref, br_ref, lt_ref,
                  h_ref, xt_ref, code_ref, wt_ref, cnt_ref, run_ref, wr2_ref, *, gw, n_exp):
    i = pl.program_id(0)

    @pl.when(i == 0)
    def _():
        run_ref[...] = jnp.zeros_like(run_ref)
        wh, wl = _split_bf16(wr_ref[...])
        wr2_ref[:, :LANES] = wh
        wr2_ref[:, LANES:] = wl

    m = mod_ref[0]
    acc = _dot(gla_ref[...], wo_ref[0:gw, :]) + _dot(pool_ref[...], wo_ref[gw:, :])
    h = x_ref[...] + m[2:3] * acc
    h_ref[...] = h
    xt = _rmsnorm(h, nw_ref[...]) * (1.0 + m[4:5]) + m[3:4]
    xt_ref[...] = _pack_bf16_pairs(xt)
    xh, xl = _split_bf16(xt)
    wr2 = wr2_ref[...]
    t1 = _dot(xh, wr2)
    logits = t1[:, :LANES] + t1[:, LANES:] + _dot(xl, wr2[:, :LANES]) + br_ref[...]
    lane = lax.broadcasted_iota(I32, logits.shape, 1)
    neg = jnp.float32(-jnp.inf)
    logits = jnp.where(lane < n_exp, logits, neg)
    vals, hots = [], []
    e_out = jnp.zeros(logits.shape, I32)
    lane_f = lane.astype(F32)
    for j in range(TOP_K):
        mx = jnp.max(logits, axis=-1, keepdims=True)
        idx = jnp.min(jnp.where(logits == mx, lane_f, float(LANES)), axis=-1, keepdims=True)
        hot = lane_f == idx
        vals.append(mx)
        hots.append(hot)
        e_out = jnp.where(lane == j, idx.astype(I32), e_out)
        logits = jnp.where(hot, neg, logits)
    ex = [jnp.exp(v - vals[0]) for v in vals]
    den = ex[0] + ex[1] + ex[2] + ex[3]
    w_out = jnp.zeros(logits.shape, F32)
    for j in range(TOP_K):
        w_out = jnp.where(lane == j, ex[j] / den, w_out)
    osum = jnp.where(hots[0] | hots[1] | hots[2] | hots[3], 1.0, 0.0)
    before = _dot(lt_ref[...], osum.astype(BF16)) + run_ref[0:1, :]
    rk_out = jnp.zeros(logits.shape, I32)
    for j in range(TOP_K):
        rj = jnp.sum(jnp.where(hots[j], before, 0.0), axis=-1, keepdims=True)
        rk_out = jnp.where(lane == j, rj.astype(I32), rk_out)
    run = run_ref[0:1, :] + jnp.sum(osum, axis=0, keepdims=True)
    run_ref[...] = jnp.broadcast_to(run, run_ref.shape)
    code = e_out * RANK_LIMIT + rk_out
    code_ref[...] = code.T[:code_ref.shape[0], :]
    wt_ref[...] = w_out
    cnt_ref[...] = jnp.broadcast_to(run, cnt_ref.shape).astype(I32)


def _route_call(gla, pool, x, mod, w_out, nw, wr, br, *, tm, n_exp, first_tile, n_tiles):
    n_all, d = x.shape
    n = n_tiles * tm
    gw = gla.shape[1]
    t_per_b = n_all // mod.shape[0]
    lt = jnp.asarray(np.tril(np.ones((tm, tm), np.float32), -1), BF16)
    row_in = lambda width: pl.BlockSpec((tm, width), lambda i: (i + first_tile, 0))
    row = lambda width: pl.BlockSpec((tm, width), lambda i: (i, 0))
    const = lambda shape: pl.BlockSpec(shape, lambda i: (0,) * len(shape))
    sds = jax.ShapeDtypeStruct
    return pl.pallas_call(
        functools.partial(_route_kernel, gw=gw, n_exp=n_exp),
        grid=(n_tiles,),
        in_specs=[row_in(gw), row_in(pool.shape[1]), row_in(d),
                  pl.BlockSpec((1, N_MOD, d), lambda i: ((i + first_tile) * tm // t_per_b, 0, 0)),
                  const(w_out.shape), const((1, d)), const(wr.shape), const((1, LANES)), const((tm, tm))],
        out_specs=[row(d), row(d // 2), pl.BlockSpec((8, tm), lambda i: (0, i)), row(LANES), const((8, LANES))],
        out_shape=[sds((n, d), F32), sds((n, d // 2), U32), sds((8, n), I32),
                   sds((n, LANES), F32), sds((8, LANES), I32)],
        scratch_shapes=[pltpu.VMEM((8, LANES), F32), pltpu.VMEM((d, 2 * LANES), BF16)],
        compiler_params=pltpu.CompilerParams(dimension_semantics=("arbitrary",)),
        name="route",
    )(gla, pool, x, mod, w_out, nw, wr, br, lt)


def _plan_kernel(cnt_ref, code_ref, dest_ref, be_ref, nv_ref, first_ref, next_ref, slot_ref, nu_ref, start_ref,
                 *, n_exp, rows):
    @pl.when(pl.program_id(0) == 0)
    def _():
        blk = (lax.broadcasted_iota(I32, be_ref.shape, 0) * LANES + lax.broadcasted_iota(I32, be_ref.shape, 1))
        blk_row0 = blk * rows
        nxt_e = [None] * n_exp
        nxt = jnp.int32(-1)
        for e in reversed(range(n_exp)):
            nxt_e[e] = nxt
            nxt = jnp.where(cnt_ref[e] > 0, e, nxt)
        zeros = jnp.zeros(be_ref.shape, I32)
        be, end_valid, first, nxt_blk, slot = zeros, zeros, zeros, zeros - 1, zeros
        acc = jnp.int32(0)
        last_e = jnp.int32(0)
        ordinal = jnp.int32(0)
        for e in range(n_exp):
            c = cnt_ref[e]
            start_ref[e] = acc
            in_e = (blk_row0 >= acc) & (c > 0)
            end_valid = jnp.where(in_e, acc + c, end_valid)
            be = jnp.where(in_e, e, be)
            first = jnp.where(in_e, (blk_row0 == acc).astype(I32), first)
            nxt_blk = jnp.where(in_e, nxt_e[e], nxt_blk)
            slot = jnp.where(in_e, ordinal % 2, slot)
            acc = acc + (c + rows - 1) // rows * rows
            last_e = jnp.where(c > 0, e, last_e)
            ordinal = ordinal + (c > 0).astype(I32)
        n_used = acc // rows
        nu_ref[0] = n_used
        be_ref[...] = jnp.where(blk < n_used, be, last_e)
        nv_ref[...] = jnp.clip(end_valid - blk_row0, 0, rows)
        first_ref[...] = first
        next_ref[...] = nxt_blk
        slot_ref[...] = slot

    code = code_ref[...]
    e_vec = code // RANK_LIMIT
    dest = code % RANK_LIMIT
    for e in range(n_exp):
        dest = dest + jnp.where(e_vec == e, start_ref[e], 0)
    dest_ref[...] = dest


def _plan_call(counts, code_t, *, n_exp, rows, chunk):
    n = code_t.shape[1]
    sds = jax.ShapeDtypeStruct
    smem = pltpu.SMEM
    return pl.pallas_call(
        functools.partial(_plan_kernel, n_exp=n_exp, rows=rows),
        grid=(n // chunk,),
        in_specs=[pl.BlockSpec(memory_space=smem), pl.BlockSpec((8, chunk), lambda i: (0, i))],
        out_specs=[pl.BlockSpec((8, chunk), lambda i: (0, i))]
        + [pl.BlockSpec((8, LANES), lambda i: (0, 0))] * 5 + [pl.BlockSpec(memory_space=smem)],
        out_shape=[sds((8, n), I32)] + [sds((8, LANES), I32)] * 5 + [sds((1,), I32)],
        scratch_shapes=[pltpu.SMEM((n_exp,), I32)],
        compiler_params=pltpu.CompilerParams(dimension_semantics=("arbitrary",)),
        name="plan",
    )(counts, code_t)


def _sc_worker_id():
    return lax.axis_index("s") * SC_CORES + lax.axis_index("c")


def _sc_scatter_call(x, idx3, *, n_out):
    n, d = x.shape
    n_win_total, k, w = idx3.shape
    n_win = n_win_total // SC_WORKERS
    mesh = plsc.VectorSubcoreMesh(core_axis_name="c", subcore_axis_name="s")

    @functools.partial(
        pl.kernel, mesh=mesh,
        out_type=jax.ShapeDtypeStruct((n_out, d), x.dtype),
        scratch_types=[pltpu.VMEM((k, w), I32), pltpu.VMEM((w, d), x.dtype), pltpu.SemaphoreType.DMA],
        name="sc_dispatch",
    )
    def kern(x_hbm, idx_hbm, out_hbm, idx_v, rows_v, sem):
        wid = _sc_worker_id()

        @pl.loop(0, n_win)
        def _(i):
            win = wid * n_win + i
            pltpu.sync_copy(idx_hbm.at[win], idx_v)
            pltpu.sync_copy(x_hbm.at[pl.ds(win * w, w)], rows_v)
            for j in range(k):
                pltpu.async_copy(rows_v, out_hbm.at[idx_v.at[j]], sem).wait()

    return kern(x, idx3)


def _sc_gather_call(table, idx3):
    n_workers, n_win, w = idx3.shape
    d = table.shape[1]
    assert n_workers == SC_WORKERS and n_win % 2 == 0
    mesh = plsc.VectorSubcoreMesh(core_axis_name="c", subcore_axis_name="s")

    @functools.partial(
        pl.kernel, mesh=mesh,
        out_type=jax.ShapeDtypeStruct((n_workers * n_win * w, d), table.dtype),
        scratch_types=[pltpu.VMEM((n_win, w), I32), pltpu.VMEM((2, w, d), table.dtype),
                       pltpu.SemaphoreType.DMA((2,)), pltpu.SemaphoreType.DMA((2,))],
        name="sc_gather",
    )
    def kern(table_hbm, idx_hbm, out_hbm, idx_v, rows_v, gsem, osem):
        wid = _sc_worker_id()
        base = wid * n_win
        pltpu.sync_copy(idx_hbm.at[wid], idx_v)

        def gather(wi, b):
            return pltpu.make_async_copy(table_hbm.at[idx_v.at[wi]], rows_v.at[b], gsem.at[b])

        def put(wi, b):
            return pltpu.make_async_copy(rows_v.at[b], out_hbm.at[pl.ds((base + wi) * w, w)], osem.at[b])

        gather(0, 0).start()

        @pl.loop(0, n_win, step=2)
        def _(i):
            for b in range(2):
                wi = i + b

                @pl.when(wi + 1 < n_win)
                def _():
                    @pl.when(wi >= 1)
                    def _():
                        put(wi - 1, 1 - b).wait()
                    gather(wi + 1, 1 - b).start()

                gather(wi, b).wait()
                put(wi, b).start()

        put(n_win - 2, 0).wait()
        put(n_win - 1, 1).wait()

    return kern(table, idx3)


def _expert_kernel(be_ref, nu_ref, nv_ref, first_ref, next_ref, slot_ref,
                   x_ref, wgu_hbm, bgu_ref, wd_hbm, bd_ref, y_ref,
                   wgu_f32_ref, wd_f32_ref, wgu_bf_ref, wd_bf_ref, sem, *, d_ff):
    i = pl.program_id(0)
    slot = slot_ref[i]

    def fetch(e, s):
        return (pltpu.make_async_copy(wgu_hbm.at[e], wgu_f32_ref.at[s], sem.at[0, s]),
                pltpu.make_async_copy(wd_hbm.at[e], wd_f32_ref.at[s], sem.at[1, s]))

    @pl.when(i == 0)
    def _():
        for cp in fetch(be_ref[0], slot):
            cp.start()

    @pl.when(first_ref[i] == 1)
    def _():
        for cp in fetch(be_ref[i], slot):
            cp.wait()

        @pl.when(next_ref[i] >= 0)
        def _():
            for cp in fetch(next_ref[i], 1 - slot):
                cp.start()

        wgu_bf_ref[...] = wgu_f32_ref[slot].astype(BF16)
        wd_bf_ref[...] = wd_f32_ref[slot].astype(BF16)

    n_valid = nv_ref[i]
    n_rows = x_ref.shape[0]

    def expert_rows(rows):
        row = lax.broadcasted_iota(I32, (rows, x_ref.shape[1]), 0)
        lo, hi = _unpack_bf16_pairs(jnp.where(row < n_valid, x_ref[0:rows, :], jnp.uint32(0)))
        xb = jnp.concatenate([lo, hi], axis=1).astype(BF16)
        gu = _dot(xb, wgu_bf_ref[...]) + bgu_ref[0]
        gate = jnp.minimum(gu[:, :d_ff], SWIGLU_LIMIT)
        up = jnp.clip(gu[:, d_ff:], -SWIGLU_LIMIT, SWIGLU_LIMIT)
        act = (up + 1.0) * gate * jax.nn.sigmoid(SWIGLU_ALPHA * gate)
        y_ref[0:rows, :] = _pack_bf16_pairs(_dot(act.astype(BF16), wd_bf_ref[...]) + bd_ref[0])

    in_use = i < nu_ref[0]

    for rows in range(EXPERT_ROW_STEP, n_rows + 1, EXPERT_ROW_STEP):
        @pl.when(in_use & (n_valid > rows - EXPERT_ROW_STEP) & (n_valid <= rows))
        def _(rows=rows):
            expert_rows(rows)
            if rows < n_rows:
                y_ref[rows:, :] = jnp.zeros((n_rows - rows, y_ref.shape[1]), y_ref.dtype)


def _expert_call(plan, xs, w_gu, b_gu, w_down, b_down):
    block_e, n_valid, first, nxt, slot, n_used = plan
    n_pad = xs.shape[0]
    n_exp, d, two_ff = w_gu.shape
    d_ff = two_ff // 2
    nblk = n_pad // EXPERT_ROWS
    rows = lambda i, be, nu, *_: (jnp.minimum(i, nu[0] - 1), 0)
    per_e = lambda i, be, *_: (be[i], 0, 0)
    grid_spec = pltpu.PrefetchScalarGridSpec(
        num_scalar_prefetch=6,
        grid=(nblk,),
        in_specs=[pl.BlockSpec((EXPERT_ROWS, d // 2), rows),
                  pl.BlockSpec(memory_space=pl.ANY), pl.BlockSpec((1, 1, two_ff), per_e),
                  pl.BlockSpec(memory_space=pl.ANY), pl.BlockSpec((1, 1, d), per_e)],
        out_specs=pl.BlockSpec((EXPERT_ROWS, d // 2), rows),
        scratch_shapes=[pltpu.VMEM((2, d, two_ff), F32), pltpu.VMEM((2, d_ff, d), F32),
                        pltpu.VMEM((d, two_ff), BF16), pltpu.VMEM((d_ff, d), BF16),
                        pltpu.SemaphoreType.DMA((2, 2))],
    )
    flat = lambda a: a.reshape(-1)[:nblk]
    return pl.pallas_call(
        functools.partial(_expert_kernel, d_ff=d_ff),
        grid_spec=grid_spec,
        out_shape=jax.ShapeDtypeStruct((n_pad, d // 2), U32),
        compiler_params=pltpu.CompilerParams(dimension_semantics=("arbitrary",),
                                             vmem_limit_bytes=EXPERT_VMEM_BYTES),
        name="experts",
    )(flat(block_e), n_used, flat(n_valid), flat(first), flat(nxt), flat(slot),
      xs, w_gu, b_gu.reshape(n_exp, 1, two_ff), w_down, b_down.reshape(n_exp, 1, d))


def _combine_kernel(y4_ref, wt_ref, h_ref, mod_ref, fw_ref, o_ref):
    wt = wt_ref[...]
    acc_lo, acc_hi = None, None
    for j in range(TOP_K):
        lo, hi = _unpack_bf16_pairs(y4_ref[j])
        w = wt[:, j:j + 1]
        acc_lo = w * lo if j == 0 else acc_lo + w * lo
        acc_hi = w * hi if j == 0 else acc_hi + w * hi
    acc = jnp.concatenate([acc_lo, acc_hi], axis=1)
    m = mod_ref[0]
    o_ref[...] = _rmsnorm(h_ref[...] + m[5:6] * acc, fw_ref[...])


def _combine_call(y4, wts, h, mod, fw, prev_out, *, n, tg, first_tile):
    d = h.shape[1]
    t_per_b = n // mod.shape[0]
    part = lambda width: pl.BlockSpec((tg, width), lambda i: (i, 0))
    row = lambda width: pl.BlockSpec((tg, width), lambda i: (i + first_tile, 0))
    in_specs = [pl.BlockSpec((TOP_K, tg, d // 2), lambda i: (0, i, 0)),
                part(LANES), part(d),
                pl.BlockSpec((1, N_MOD, d), lambda i: ((i + first_tile) * tg // t_per_b, 0, 0)),
                pl.BlockSpec((1, d), lambda i: (0, 0))]
    args = [y4, wts, h, mod, fw]
    kern = _combine_kernel
    aliases = {}
    if prev_out is not None:
        in_specs.append(pl.BlockSpec(memory_space=pl.ANY))
        args.append(prev_out)
        kern = lambda y4_ref, wt_ref, h_ref, mod_ref, fw_ref, prev_ref, o_ref: _combine_kernel(
            y4_ref, wt_ref, h_ref, mod_ref, fw_ref, o_ref)
        aliases = {len(args) - 1: 0}
    return pl.pallas_call(
        kern,
        grid=(y4.shape[1] // tg,),
        in_specs=in_specs,
        out_specs=row(d),
        out_shape=jax.ShapeDtypeStruct((n, d), F32),
        input_output_aliases=aliases,
        compiler_params=pltpu.CompilerParams(dimension_semantics=("arbitrary",)),
        name="combine",
    )(*args)


def kernel(x, c, ctx, c_ctx, w_ada, b_ada, norm_mix_w, norm_mlp_w, w_in, w_gk_f, b_gk_f, w_gk_b, b_gk_b,
           gla_norm_w, w_pool, pool_scale, w_out, w_router, b_router, w_gu, b_gu, w_down, b_down,
           final_norm_w):
    b, t, d = x.shape
    assert w_ada.shape[0] == 1, "single-layer trunk"
    n_exp = w_router.shape[2]
    rank = w_gk_f.shape[1]
    qk = w_gk_f.shape[2]
    dk = qk // GLA_HEADS
    gw = GLA_HEADS * gla_norm_w.shape[1]
    pw = w_pool.shape[1] * w_pool.shape[2]
    assert w_in.shape[2] == 2 * qk + 2 * gw + 2 * rank + pw and 2 * rank <= LANES
    assert t % SUPER == 0 and ctx.shape[1] % SUPER == 0 and n_exp <= LANES

    rows = -(-(b + 1) // 8) * 8
    cc = jnp.zeros((rows, d), F32).at[:b].set(c).at[b].set(c_ctx)
    mod = _mod_call(cc, w_ada[0], b_ada)
    mod_x = mod[:b].reshape(b, N_MOD, d)
    mod_c = mod[b:b + 1].reshape(1, N_MOD, d)

    wi = w_in[0]
    o_r = 2 * qk + 2 * gw
    w_cat = jnp.concatenate([wi[:, :o_r], wi[:, o_r + 2 * rank:], wi[:, o_r:o_r + 2 * rank],
                             jnp.zeros((d, LANES - 2 * rank), F32)], axis=1).astype(BF16)
    wgk = jnp.zeros((LANES, 2 * qk), F32).at[:rank, :qk].set(w_gk_f[0]).at[rank:2 * rank, qk:].set(w_gk_b[0])
    bgk = jnp.concatenate([b_gk_f[0], b_gk_b[0]])[None, :]
    proj = functools.partial(_inproj_call, nw=norm_mix_w, w=w_cat, wgk=wgk.astype(BF16), bgk=bgk,
                             qk=qk, gw=gw, pw=pw, dk=dk)
    q, k, v, vt, g, xp, gk = proj(x, mod_x, tm=1024)
    _, kc, _, vtc, _, _, gkc = proj(ctx, mod_c, tm=SUPER)

    gla = _gla_call(q, k, v, vt, gk, g, kc, vtc, gkc, gla_norm_w)
    pool = _pool_call(xp, w_pool[0].astype(BF16), pool_scale)

    n = b * t
    wr = jnp.zeros((d, LANES), F32).at[:, :n_exp].set(w_router[0])
    br = jnp.zeros((1, LANES), F32).at[0, :n_exp].set(b_router[0])
    n_part = n // MOE_PARTS
    tm, tg = 512, 256
    assert n % MOE_PARTS == 0 and n_part % (SC_WORKERS * SC_WINDOW) == 0 and n_part % tm == 0 and n_part < RANK_LIMIT
    assert (n_part * TOP_K) % (SC_WORKERS * 2 * SC_GATHER_WINDOW) == 0
    n_pad = n_part * TOP_K + n_exp * EXPERT_ROWS
    nblk = n_pad // EXPERT_ROWS
    assert nblk <= 8 * LANES
    parts = range(MOE_PARTS)
    routed = [_route_call(gla.reshape(n, gw), pool.reshape(n, pw), x.reshape(n, d), mod_x, w_out[0].astype(BF16),
                          norm_mlp_w, wr, br, tm=tm, n_exp=n_exp,
                          first_tile=p * n_part // tm, n_tiles=n_part // tm) for p in parts]
    plans = [_plan_call(cnt[0, :n_exp], code_t, n_exp=n_exp, rows=EXPERT_ROWS, chunk=min(n_part, 4096))
             for (_, _, code_t, _, cnt) in routed]
    dests = [plan[0][:TOP_K] for plan in plans]
    xs = [_sc_scatter_call(routed[p][1],
                           dests[p].reshape(TOP_K, n_part // SC_WINDOW, SC_WINDOW).transpose(1, 0, 2), n_out=n_pad)
          for p in parts]
    ys = [_expert_call(plans[p][1:], xs[p], w_gu[0], b_gu[0], w_down[0], b_down[0]) for p in parts]
    y4 = [_sc_gather_call(ys[p], dests[p].reshape(SC_WORKERS, -1, SC_GATHER_WINDOW)).reshape(TOP_K, n_part, d // 2)
          for p in parts]
    out = None
    for p in parts:
        out = _combine_call(y4[p], routed[p][3], routed[p][0], mod_x, final_norm_w[None, :], out,
                            n=n, tg=tg, first_tile=p * n_part // tg)
    return out.reshape(b, t, d)
```

```python
import functools

import numpy as np
import jax
import jax.numpy as jnp
from jax import lax
from jax.experimental import pallas as pl
from jax.experimental.pallas import tpu as pltpu
from jax.experimental.pallas import tpu_sc as plsc

F32 = jnp.float32
BF16 = jnp.bfloat16
I32 = jnp.int32
U32 = jnp.uint32

GRID_W = 64
GLA_HEADS = 4
GLA_CHUNK = 64
GATE_NORMALIZER = 16.0
POOL_WINDOWS = (2, 4, 8, 16)
POOL_PAD_GRID_ROWS = 8
TOP_K = 4
RANK_LIMIT = 1 << 20
SWIGLU_LIMIT = 7.0
SWIGLU_ALPHA = 1.702
N_MOD = 6
EPS = 1e-6

LANES = 128
SUPER = 256
HEAD_PAIR_DK = 128
EXPERT_ROWS = 512
EXPERT_ROW_STEP = 128
MOE_PARTS = 2
EXPERT_VMEM_BYTES = 56 * 1024 * 1024
SC_CORES = 2
SC_SUBCORES = 16
SC_WORKERS = SC_CORES * SC_SUBCORES
SC_WINDOW = 32
SC_GATHER_WINDOW = 64


def _dot(a, b):
    return jnp.dot(a, b, preferred_element_type=F32)


def _dot_nt(a, b):
    return lax.dot_general(a, b, (((1,), (1,)), ((), ())), preferred_element_type=F32)


def _split_bf16(x):
    hi = x.astype(BF16)
    lo = (x - hi.astype(F32)).astype(BF16)
    return hi, lo


def _pack_bf16_pairs(x):
    c = x.shape[1] // 2
    lo = lax.bitcast_convert_type(x[:, :c].astype(BF16).astype(F32), U32)
    hi = lax.bitcast_convert_type(x[:, c:].astype(BF16).astype(F32), U32)
    return (lo >> 16) | hi


def _unpack_bf16_pairs(p):
    lo = lax.bitcast_convert_type(p << 16, F32)
    hi = lax.bitcast_convert_type(p & jnp.uint32(0xFFFF0000), F32)
    return lo, hi


def _rmsnorm(x, w):
    var = jnp.mean(x * x, axis=-1, keepdims=True)
    return x * lax.rsqrt(var + EPS) * w


def _mod_kernel(c_ref, w_ref, b_ref, o_ref):
    c = c_ref[...]
    s = c * jax.nn.sigmoid(c)
    o_ref[...] = jnp.dot(s, w_ref[...], precision=lax.Precision.HIGHEST,
                         preferred_element_type=F32) + b_ref[...]


def _mod_call(cc, w_ada, b_ada):
    rows, d = cc.shape
    n = w_ada.shape[1]
    tn = 1024
    return pl.pallas_call(
        _mod_kernel,
        grid=(n // tn,),
        in_specs=[pl.BlockSpec((rows, d), lambda j: (0, 0)),
                  pl.BlockSpec((d, tn), lambda j: (0, j)),
                  pl.BlockSpec((1, tn), lambda j: (0, j))],
        out_specs=pl.BlockSpec((rows, tn), lambda j: (0, j)),
        out_shape=jax.ShapeDtypeStruct((rows, n), F32),
        name="mod",
    )(cc, w_ada, b_ada)


def _inproj_kernel(x_ref, mod_ref, nw_ref, w_ref, wgk_ref, bgk_ref,
                   q_ref, k_ref, v_ref, vt_ref, g_ref, p_ref, gk_ref, *, qk, gw, pw, dk):
    x = x_ref[0]
    m = mod_ref[0]
    hm = (_rmsnorm(x, nw_ref[...]) * (1.0 + m[1:2]) + m[0:1]).astype(BF16)
    p = _dot(hm, w_ref[...])
    vt = p[:, 2 * qk:2 * qk + gw].T
    for s in range(vt_ref.shape[1]):
        vt_ref[0, s] = vt[:, s * SUPER:(s + 1) * SUPER].astype(BF16)
    o = 0
    q_ref[0] = (p[:, o:o + qk] * (dk ** -0.5)).astype(BF16); o += qk
    k_ref[0] = p[:, o:o + qk].astype(BF16); o += qk
    v_ref[0] = p[:, o:o + gw].astype(BF16); o += gw
    g_ref[0] = p[:, o:o + gw].astype(BF16); o += gw
    p_ref[0] = p[:, o:o + pw].astype(BF16); o += pw
    r = p[:, o:o + LANES]
    z = _dot(r.astype(BF16), wgk_ref[...]) + bgk_ref[...]
    gk_ref[0] = (jnp.minimum(z, 0.0) - jnp.log1p(jnp.exp(-jnp.abs(z)))) * (1.0 / GATE_NORMALIZER)


def _inproj_call(x, mod, nw, w, wgk, bgk, *, qk, gw, pw, dk, tm):
    b, t, d = x.shape
    n_in = w.shape[1]
    bs = lambda width: pl.BlockSpec((1, tm, width), lambda i, j: (i, j, 0))
    const = lambda shape: pl.BlockSpec(shape, lambda i, j: (0,) * len(shape))
    per_batch = mod.shape[0] > 1
    sds = jax.ShapeDtypeStruct
    return pl.pallas_call(
        functools.partial(_inproj_kernel, qk=qk, gw=gw, pw=pw, dk=dk),
        grid=(b, t // tm),
        in_specs=[bs(d),
                  pl.BlockSpec((1, N_MOD, d), (lambda i, j: (i, 0, 0)) if per_batch else (lambda i, j: (0, 0, 0))),
                  const((1, d)), const((d, n_in)), const((LANES, 2 * qk)), const((1, 2 * qk))],
        out_specs=[bs(qk), bs(qk), bs(gw),
                   pl.BlockSpec((1, tm // SUPER, gw, SUPER), lambda i, j: (i, j, 0, 0)),
                   bs(gw), bs(pw), bs(2 * qk)],
        out_shape=[sds((b, t, qk), BF16), sds((b, t, qk), BF16), sds((b, t, gw), BF16),
                   sds((b, t // SUPER, gw, SUPER), BF16),
                   sds((b, t, gw), BF16), sds((b, t, pw), BF16), sds((b, t, 2 * qk), F32)],
        compiler_params=pltpu.CompilerParams(dimension_semantics=("arbitrary", "arbitrary")),
        name="inproj",
    )(x, mod, nw, w, wgk, bgk)


def _gla_super(q, k, v, vt, gk, cm, amask, bd_mask, st, fwd, want_out):
    nch = SUPER // GLA_CHUNK
    order = tuple(range(nch)) if fwd else tuple(reversed(range(nch)))
    last_row = GLA_CHUNK - 1 if fwd else 0
    mid_row = GLA_CHUNK // 2 - 1 if fwd else GLA_CHUNK // 2
    hi, lo = _split_bf16(gk)
    bcum = _dot(cm, hi) + _dot(cm, lo)

    def chunk_row(r):
        return jnp.concatenate(
            [jnp.broadcast_to(bcum[c * GLA_CHUNK + r:c * GLA_CHUNK + r + 1, :], (GLA_CHUNK, bcum.shape[1]))
             for c in range(nch)], axis=0)

    chunk_of_row = lax.broadcasted_iota(I32, bcum.shape, 0) // GLA_CHUNK

    def by_chunk(x):
        return jnp.concatenate([jnp.where(chunk_of_row == c, x, 0.0).astype(BF16) for c in range(nch)], axis=1)

    blast = chunk_row(last_row)
    u_all = _dot(vt, by_chunk(k * jnp.exp(blast - bcum)))
    before = [None] * nch
    for c in order:
        before[c] = st
        decay = jnp.exp(bcum[c * GLA_CHUNK + last_row:c * GLA_CHUNK + last_row + 1, :])
        st = st * decay + jnp.where(bd_mask, u_all[:, c * HEAD_PAIR_DK:(c + 1) * HEAD_PAIR_DK], 0.0)
    if not want_out:
        return None, st
    bmid = chunk_row(mid_row)
    qt = q * jnp.exp(bcum - bmid)
    kt = (k * jnp.exp(bmid - bcum)).astype(BF16)
    lane = lax.broadcasted_iota(I32, qt.shape, 1)
    half = HEAD_PAIR_DK // 2
    o_heads = []
    for hh in range(2):
        sel = (lane < half) if hh == 0 else (lane >= half)
        a = _dot_nt(jnp.where(sel, qt, 0.0).astype(BF16), kt)
        a = jnp.where(amask, a, 0.0).astype(BF16)
        o_heads.append(_dot(a, v[:, hh * LANES:(hh + 1) * LANES]))
    qh = (q * jnp.exp(bcum)).astype(BF16)
    o_inter = jnp.concatenate(
        [_dot_nt(qh[c * GLA_CHUNK:(c + 1) * GLA_CHUNK], before[c].astype(BF16)) for c in range(nch)], axis=0)
    return jnp.concatenate(o_heads, axis=1) + o_inter, st


def _gla_kernel(q_ref, k_ref, v_ref, vt_ref, gkf_ref, gkb_ref, g_ref, kc_ref, vtc_ref, gkfc_ref, gkbc_ref,
                nw_ref, cmf_ref, cmb_ref, o_ref, stf_ref, stb_ref, of_ref, ob_ref):
    t = q_ref.shape[1]
    tc = kc_ref.shape[1]
    nsc, nscc = t // SUPER, tc // SUPER
    cmf = cmf_ref[...]
    cmb = cmb_ref[...]
    amask_f = cmf > 0
    amask_b = cmb > 0
    row = lax.broadcasted_iota(I32, (2 * LANES, HEAD_PAIR_DK), 0)
    lane = lax.broadcasted_iota(I32, (2 * LANES, HEAD_PAIR_DK), 1)
    bd_mask = (row < LANES) == (lane < HEAD_PAIR_DK // 2)

    def ctx_state(gk_ref, cm, fwd, j, st):
        rows = pl.ds(j * SUPER, SUPER)
        return _gla_super(None, kc_ref[0, rows, :].astype(F32), None, vtc_ref[0, j], gk_ref[0, rows, :],
                          cm, None, bd_mask, st, fwd, False)[1]

    def latent(gk_ref, cm, amask, fwd, j, st):
        rows = pl.ds(pl.multiple_of(j * SUPER, SUPER), SUPER)
        return _gla_super(q_ref[0, rows, :].astype(F32), k_ref[0, rows, :].astype(F32), v_ref[0, rows, :],
                          vt_ref[0, j], gk_ref[0, rows, :], cm, amask, bd_mask, st, fwd, True)

    st = jnp.zeros(stf_ref.shape, F32)
    for j in range(nscc):
        st = ctx_state(gkfc_ref, cmf, True, j, st)
    stf_ref[...] = st
    st = jnp.zeros(stb_ref.shape, F32)
    for j in reversed(range(nscc)):
        st = ctx_state(gkbc_ref, cmb, False, j, st)
    stb_ref[...] = st

    def scan_body(jj, carry):
        jb = nsc - 1 - jj
        of, stf = latent(gkf_ref, cmf, amask_f, True, jj, stf_ref[...])
        of_ref[pl.ds(pl.multiple_of(jj * SUPER, SUPER), SUPER), :] = of
        stf_ref[...] = stf
        ob, stb = latent(gkb_ref, cmb, amask_b, False, jb, stb_ref[...])
        ob_ref[pl.ds(pl.multiple_of(jb * SUPER, SUPER), SUPER), :] = ob
        stb_ref[...] = stb
        return carry

    lax.fori_loop(0, nsc, scan_body, 0, unroll=2)

    nw = nw_ref[...]

    def out_body(j, carry):
        rows = pl.ds(pl.multiple_of(j * SUPER, SUPER), SUPER)
        o = of_ref[rows, :] + ob_ref[rows, :]
        g = g_ref[0, rows, :].astype(F32)
        gate = g * jax.nn.sigmoid(g)
        for hh in range(2):
            oh = o[:, hh * LANES:(hh + 1) * LANES]
            on = oh * lax.rsqrt(jnp.mean(oh * oh, axis=-1, keepdims=True) + EPS) * nw
            o_ref[0, rows, hh * LANES:(hh + 1) * LANES] = (on * gate[:, hh * LANES:(hh + 1) * LANES]).astype(BF16)
        return carry

    lax.fori_loop(0, nsc, out_body, 0)


def _gla_masks():
    i = np.arange(SUPER)
    same = (i[:, None] // GLA_CHUNK) == (i[None, :] // GLA_CHUNK)
    fwd = same & (i[None, :] <= i[:, None])
    bwd = same & (i[None, :] >= i[:, None])
    return jnp.asarray(fwd, BF16), jnp.asarray(bwd, BF16)


def _gla_call(q, k, v, vt, gk, g, kc, vtc, gkc, nw):
    b, t, qk = q.shape
    tc = kc.shape[1]
    npair = qk // HEAD_PAIR_DK
    cmf, cmb = _gla_masks()
    lat = lambda width, off: pl.BlockSpec((1, t, width), lambda i, j: (i, 0, j + off))
    ctx = lambda width, off: pl.BlockSpec((1, tc, width), lambda i, j: (i, 0, j + off))
    tr = lambda n_groups: pl.BlockSpec((1, n_groups, 2 * LANES, SUPER), lambda i, j: (i, 0, j, 0))
    const = lambda shape: pl.BlockSpec(shape, lambda i, j: (0,) * len(shape))
    return pl.pallas_call(
        _gla_kernel,
        grid=(b, npair),
        in_specs=[lat(HEAD_PAIR_DK, 0), lat(HEAD_PAIR_DK, 0), lat(2 * LANES, 0), tr(t // SUPER),
                  lat(HEAD_PAIR_DK, 0), lat(HEAD_PAIR_DK, npair), lat(2 * LANES, 0),
                  ctx(HEAD_PAIR_DK, 0), tr(tc // SUPER), ctx(HEAD_PAIR_DK, 0), ctx(HEAD_PAIR_DK, npair),
                  const((1, LANES)), const(cmf.shape), const(cmb.shape)],
        out_specs=lat(2 * LANES, 0),
        out_shape=jax.ShapeDtypeStruct((b, t, v.shape[2]), BF16),
        scratch_shapes=[pltpu.VMEM((2 * LANES, HEAD_PAIR_DK), F32), pltpu.VMEM((2 * LANES, HEAD_PAIR_DK), F32),
                        pltpu.VMEM((t, 2 * LANES), F32), pltpu.VMEM((t, 2 * LANES), F32)],
        compiler_params=pltpu.CompilerParams(dimension_semantics=("arbitrary", "arbitrary")),
        name="gla",
    )(q, k, v, vt, gk, gk, g, kc, vtc, gkc, gkc, nw, cmf, cmb)


def _pool_kernel(x_ref, cm_ref, cnt_ref, wp_ref, ps_ref, o_ref, a_ref, b_ref):
    t = x_ref.shape[1]
    pad = POOL_PAD_GRID_ROWS * GRID_W
    total = t + 2 * pad
    for gi, w in enumerate(POOL_WINDOWS):
        lo = w // 2
        cols = slice(gi * LANES, (gi + 1) * LANES)
        cmat = cm_ref[gi]
        a_ref[0:pad, :] = jnp.zeros((pad, LANES), F32)
        a_ref[pad + t:total, :] = jnp.zeros((pad, LANES), F32)
        for blk in range(t // SUPER):
            rs = slice(blk * SUPER, (blk + 1) * SUPER)
            a_ref[pad + blk * SUPER:pad + (blk + 1) * SUPER, :] = _dot(cmat, x_ref[0, rs, cols])
        src, dst = a_ref, b_ref
        m = 1
        while m < w:
            sh = m * GRID_W
            dst[0:total - sh, :] = src[0:total - sh, :] + src[sh:total, :]
            src, dst = dst, src
            m *= 2
        first = pad - lo * GRID_W
        pooled = src[first:first + t, :] / cnt_ref[gi] - x_ref[0, :, cols].astype(F32)
        yp = _dot(pooled.astype(BF16), wp_ref[gi]) * ps_ref[:, cols]
        o_ref[0, :, cols] = yp.astype(BF16)


def _pool_col_mats():
    i = np.arange(SUPER)
    same_row = (i[:, None] // GRID_W) == (i[None, :] // GRID_W)
    d = i[None, :] - i[:, None]
    mats = []
    for w in POOL_WINDOWS:
        lo = w // 2
        hi = w - 1 - lo
        mats.append(same_row & (d >= -lo) & (d <= hi))
    return jnp.asarray(np.stack(mats), BF16)


def _pool_counts(t):
    rows = t // GRID_W
    r = np.arange(t) // GRID_W
    c = np.arange(t) % GRID_W
    out = []
    for w in POOL_WINDOWS:
        lo = w // 2
        hi = w - 1 - lo
        cnt_r = np.minimum(r + hi + 1, rows) - np.maximum(r - lo, 0)
        cnt_c = np.minimum(c + hi + 1, GRID_W) - np.maximum(c - lo, 0)
        out.append(np.broadcast_to((cnt_r * cnt_c).astype(np.float32)[:, None], (t, LANES)))
    return jnp.asarray(np.stack(out))


def _pool_call(xp, w_pool, pool_scale):
    b, t, pw = xp.shape
    ng = len(POOL_WINDOWS)
    assert max(POOL_WINDOWS) // 2 <= POOL_PAD_GRID_ROWS and t % GRID_W == 0
    cm = _pool_col_mats()
    cnt = _pool_counts(t)
    staged = t + 2 * POOL_PAD_GRID_ROWS * GRID_W
    const = lambda shape: pl.BlockSpec(shape, lambda i: (0,) * len(shape))
    return pl.pallas_call(
        _pool_kernel,
        grid=(b,),
        in_specs=[pl.BlockSpec((1, t, pw), lambda i: (i, 0, 0)),
                  const(cm.shape), const(cnt.shape), const((ng, LANES, LANES)), const((1, pw))],
        out_specs=pl.BlockSpec((1, t, pw), lambda i: (i, 0, 0)),
        out_shape=jax.ShapeDtypeStruct((b, t, pw), BF16),
        scratch_shapes=[pltpu.VMEM((staged, LANES), F32), pltpu.VMEM((staged, LANES), F32)],
        compiler_params=pltpu.CompilerParams(dimension_semantics=("arbitrary",)),
        name="pool",
    )(xp, cm, cnt, w_pool, pool_scale)


def _route_kernel(gla_ref, pool_ref, x_ref, mod_ref, wo_ref, nw_ref, wr_ref, br_ref, lt_ref,
                  h_ref, xt_ref, code_ref, wt_ref, cnt_ref, run_ref, wr2_ref, *, gw, n_exp):
    i = pl.program_id(0)

    @pl.when(i == 0)
    def _():
        run_ref[...] = jnp.zeros_like(run_ref)
        wh, wl = _split_bf16(wr_ref[...])
        wr2_ref[:, :LANES] = wh
        wr2_ref[:, LANES:] = wl

    m = mod_ref[0]
    acc = _dot(gla_ref[...], wo_ref[0:gw, :]) + _dot(pool_ref[...], wo_ref[gw:, :])
    h = x_ref[...] + m[2:3] * acc
    h_ref[...] = h
    xt = _rmsnorm(h, nw_ref[...]) * (1.0 + m[4:5]) + m[3:4]
    xt_ref[...] = _pack_bf16_pairs(xt)
    xh, xl = _split_bf16(xt)
    wr2 = wr2_ref[...]
    t1 = _dot(xh, wr2)
    logits = t1[:, :LANES] + t1[:, LANES:] + _dot(xl, wr2[:, :LANES]) + br_ref[...]
    lane = lax.broadcasted_iota(I32, logits.shape, 1)
    neg = jnp.float32(-jnp.inf)
    logits = jnp.where(lane < n_exp, logits, neg)
    vals, hots = [], []
    e_out = jnp.zeros(logits.shape, I32)
    lane_f = lane.astype(F32)
    for j in range(TOP_K):
        mx = jnp.max(logits, axis=-1, keepdims=True)
        idx = jnp.min(jnp.where(logits == mx, lane_f, float(LANES)), axis=-1, keepdims=True)
        hot = lane_f == idx
        vals.append(mx)
        hots.append(hot)
        e_out = jnp.where(lane == j, idx.astype(I32), e_out)
        logits = jnp.where(hot, neg, logits)
    ex = [jnp.exp(v - vals[0]) for v in vals]
    den = ex[0] + ex[1] + ex[2] + ex[3]
    w_out = jnp.zeros(logits.shape, F32)
    for j in range(TOP_K):
        w_out = jnp.where(lane == j, ex[j] / den, w_out)
    osum = jnp.where(hots[0] | hots[1] | hots[2] | hots[3], 1.0, 0.0)
    before = _dot(lt_ref[...], osum.astype(BF16)) + run_ref[0:1, :]
    rk_out = jnp.zeros(logits.shape, I32)
    for j in range(TOP_K):
        rj = jnp.sum(jnp.where(hots[j], before, 0.0), axis=-1, keepdims=True)
        rk_out = jnp.where(lane == j, rj.astype(I32), rk_out)
    run = run_ref[0:1, :] + jnp.sum(osum, axis=0, keepdims=True)
    run_ref[...] = jnp.broadcast_to(run, run_ref.shape)
    code = e_out * RANK_LIMIT + rk_out
    code_ref[...] = code.T[:code_ref.shape[0], :]
    wt_ref[...] = w_out
    cnt_ref[...] = jnp.broadcast_to(run, cnt_ref.shape).astype(I32)


def _route_call(gla, pool, x, mod, w_out, nw, wr, br, *, tm, n_exp, first_tile, n_tiles):
    n_all, d = x.shape
    n = n_tiles * tm
    gw = gla.shape[1]
    t_per_b = n_all // mod.shape[0]
    lt = jnp.asarray(np.tril(np.ones((tm, tm), np.float32), -1), BF16)
    row_in = lambda width: pl.BlockSpec((tm, width), lambda i: (i + first_tile, 0))
    row = lambda width: pl.BlockSpec((tm, width), lambda i: (i, 0))
    const = lambda shape: pl.BlockSpec(shape, lambda i: (0,) * len(shape))
    sds = jax.ShapeDtypeStruct
    return pl.pallas_call(
        functools.partial(_route_kernel, gw=gw, n_exp=n_exp),
        grid=(n_tiles,),
        in_specs=[row_in(gw), row_in(pool.shape[1]), row_in(d),
                  pl.BlockSpec((1, N_MOD, d), lambda i: ((i + first_tile) * tm // t_per_b, 0, 0)),
                  const(w_out.shape), const((1, d)), const(wr.shape), const((1, LANES)), const((tm, tm))],
        out_specs=[row(d), row(d // 2), pl.BlockSpec((8, tm), lambda i: (0, i)), row(LANES), const((8, LANES))],
        out_shape=[sds((n, d), F32), sds((n, d // 2), U32), sds((8, n), I32),
                   sds((n, LANES), F32), sds((8, LANES), I32)],
        scratch_shapes=[pltpu.VMEM((8, LANES), F32), pltpu.VMEM((d, 2 * LANES), BF16)],
        compiler_params=pltpu.CompilerParams(dimension_semantics=("arbitrary",)),
        name="route",
    )(gla, pool, x, mod, w_out, nw, wr, br, lt)


def _plan_kernel(cnt_ref, code_ref, dest_ref, be_ref, nv_ref, first_ref, next_ref, slot_ref, nu_ref, start_ref,
                 *, n_exp, rows):
    @pl.when(pl.program_id(0) == 0)
    def _():
        blk = (lax.broadcasted_iota(I32, be_ref.shape, 0) * LANES + lax.broadcasted_iota(I32, be_ref.shape, 1))
        blk_row0 = blk * rows
        nxt_e = [None] * n_exp
        nxt = jnp.int32(-1)
        for e in reversed(range(n_exp)):
            nxt_e[e] = nxt
            nxt = jnp.where(cnt_ref[e] > 0, e, nxt)
        zeros = jnp.zeros(be_ref.shape, I32)
        be, end_valid, first, nxt_blk, slot = zeros, zeros, zeros, zeros - 1, zeros
        acc = jnp.int32(0)
        last_e = jnp.int32(0)
        ordinal = jnp.int32(0)
        for e in range(n_exp):
            c = cnt_ref[e]
            start_ref[e] = acc
            in_e = (blk_row0 >= acc) & (c > 0)
            end_valid = jnp.where(in_e, acc + c, end_valid)
            be = jnp.where(in_e, e, be)
            first = jnp.where(in_e, (blk_row0 == acc).astype(I32), first)
            nxt_blk = jnp.where(in_e, nxt_e[e], nxt_blk)
            slot = jnp.where(in_e, ordinal % 2, slot)
            acc = acc + (c + rows - 1) // rows * rows
            last_e = jnp.where(c > 0, e, last_e)
            ordinal = ordinal + (c > 0).astype(I32)
        n_used = acc // rows
        nu_ref[0] = n_used
        be_ref[...] = jnp.where(blk < n_used, be, last_e)
        nv_ref[...] = jnp.clip(end_valid - blk_row0, 0, rows)
        first_ref[...] = first
        next_ref[...] = nxt_blk
        slot_ref[...] = slot

    code = code_ref[...]
    e_vec = code // RANK_LIMIT
    dest = code % RANK_LIMIT
    for e in range(n_exp):
        dest = dest + jnp.where(e_vec == e, start_ref[e], 0)
    dest_ref[...] = dest


def _plan_call(counts, code_t, *, n_exp, rows, chunk):
    n = code_t.shape[1]
    sds = jax.ShapeDtypeStruct
    smem = pltpu.SMEM
    return pl.pallas_call(
        functools.partial(_plan_kernel, n_exp=n_exp, rows=rows),
        grid=(n // chunk,),
        in_specs=[pl.BlockSpec(memory_space=smem), pl.BlockSpec((8, chunk), lambda i: (0, i))],
        out_specs=[pl.BlockSpec((8, chunk), lambda i: (0, i))]
        + [pl.BlockSpec((8, LANES), lambda i: (0, 0))] * 5 + [pl.BlockSpec(memory_space=smem)],
        out_shape=[sds((8, n), I32)] + [sds((8, LANES), I32)] * 5 + [sds((1,), I32)],
        scratch_shapes=[pltpu.SMEM((n_exp,), I32)],
        compiler_params=pltpu.CompilerParams(dimension_semantics=("arbitrary",)),
        name="plan",
    )(counts, code_t)


def _sc_worker_id():
    return lax.axis_index("s") * SC_CORES + lax.axis_index("c")


def _sc_scatter_call(x, idx3, *, n_out):
    n, d = x.shape
    n_win_total, k, w = idx3.shape
    n_win = n_win_total // SC_WORKERS
    mesh = plsc.VectorSubcoreMesh(core_axis_name="c", subcore_axis_name="s")

    @functools.partial(
        pl.kernel, mesh=mesh,
        out_type=jax.ShapeDtypeStruct((n_out, d), x.dtype),
        scratch_types=[pltpu.VMEM((k, w), I32), pltpu.VMEM((w, d), x.dtype), pltpu.SemaphoreType.DMA],
        name="sc_dispatch",
    )
    def kern(x_hbm, idx_hbm, out_hbm, idx_v, rows_v, sem):
        wid = _sc_worker_id()

        @pl.loop(0, n_win)
        def _(i):
            win = wid * n_win + i
            pltpu.sync_copy(idx_hbm.at[win], idx_v)
            pltpu.sync_copy(x_hbm.at[pl.ds(win * w, w)], rows_v)
            for j in range(k):
                pltpu.async_copy(rows_v, out_hbm.at[idx_v.at[j]], sem).wait()

    return kern(x, idx3)


def _sc_gather_call(table, idx3):
    n_workers, n_win, w = idx3.shape
    d = table.shape[1]
    assert n_workers == SC_WORKERS and n_win % 2 == 0
    mesh = plsc.VectorSubcoreMesh(core_axis_name="c", subcore_axis_name="s")

    @functools.partial(
        pl.kernel, mesh=mesh,
        out_type=jax.ShapeDtypeStruct((n_workers * n_win * w, d), table.dtype),
        scratch_types=[pltpu.VMEM((n_win, w), I32), pltpu.VMEM((2, w, d), table.dtype),
                       pltpu.SemaphoreType.DMA((2,)), pltpu.SemaphoreType.DMA((2,))],
        name="sc_gather",
    )
    def kern(table_hbm, idx_hbm, out_hbm, idx_v, rows_v, gsem, osem):
        wid = _sc_worker_id()
        base = wid * n_win
        pltpu.sync_copy(idx_hbm.at[wid], idx_v)

        def gather(wi, b):
            return pltpu.make_async_copy(table_hbm.at[idx_v.at[wi]], rows_v.at[b], gsem.at[b])

        def put(wi, b):
            return pltpu.make_async_copy(rows_v.at[b], out_hbm.at[pl.ds((base + wi) * w, w)], osem.at[b])

        gather(0, 0).start()

        @pl.loop(0, n_win, step=2)
        def _(i):
            for b in range(2):
                wi = i + b

                @pl.when(wi + 1 < n_win)
                def _():
                    @pl.when(wi >= 1)
                    def _():
                        put(wi - 1, 1 - b).wait()
                    gather(wi + 1, 1 - b).start()

                gather(wi, b).wait()
                put(wi, b).start()

        put(n_win - 2, 0).wait()
        put(n_win - 1, 1).wait()

    return kern(table, idx3)


def _expert_kernel(be_ref, nu_ref, nv_ref, first_ref, next_ref, slot_ref,
                   x_ref, wgu_hbm, bgu_ref, wd_hbm, bd_ref, y_ref,
                   wgu_f32_ref, wd_f32_ref, wgu_bf_ref, wd_bf_ref, sem, *, d_ff):
    i = pl.program_id(0)
    slot = slot_ref[i]

    def fetch(e, s):
        return (pltpu.make_async_copy(wgu_hbm.at[e], wgu_f32_ref.at[s], sem.at[0, s]),
                pltpu.make_async_copy(wd_hbm.at[e], wd_f32_ref.at[s], sem.at[1, s]))

    @pl.when(i == 0)
    def _():
        for cp in fetch(be_ref[0], slot):
            cp.start()

    @pl.when(first_ref[i] == 1)
    def _():
        for cp in fetch(be_ref[i], slot):
            cp.wait()

        @pl.when(next_ref[i] >= 0)
        def _():
            for cp in fetch(next_ref[i], 1 - slot):
                cp.start()

        wgu_bf_ref[...] = wgu_f32_ref[slot].astype(BF16)
        wd_bf_ref[...] = wd_f32_ref[slot].astype(BF16)

    n_valid = nv_ref[i]
    n_rows = x_ref.shape[0]

    def expert_rows(rows):
        row = lax.broadcasted_iota(I32, (rows, x_ref.shape[1]), 0)
        lo, hi = _unpack_bf16_pairs(jnp.where(row < n_valid, x_ref[0:rows, :], jnp.uint32(0)))
        xb = jnp.concatenate([lo, hi], axis=1).astype(BF16)
        gu = _dot(xb, wgu_bf_ref[...]) + bgu_ref[0]
        gate = jnp.minimum(gu[:, :d_ff], SWIGLU_LIMIT)
        up = jnp.clip(gu[:, d_ff:], -SWIGLU_LIMIT, SWIGLU_LIMIT)
        act = (up + 1.0) * gate * jax.nn.sigmoid(SWIGLU_ALPHA * gate)
        y_ref[0:rows, :] = _pack_bf16_pairs(_dot(act.astype(BF16), wd_bf_ref[...]) + bd_ref[0])

    in_use = i < nu_ref[0]

    for rows in range(EXPERT_ROW_STEP, n_rows + 1, EXPERT_ROW_STEP):
        @pl.when(in_use & (n_valid > rows - EXPERT_ROW_STEP) & (n_valid <= rows))
        def _(rows=rows):
            expert_rows(rows)
            if rows < n_rows:
                y_ref[rows:, :] = jnp.zeros((n_rows - rows, y_ref.shape[1]), y_ref.dtype)


def _expert_call(plan, xs, w_gu, b_gu, w_down, b_down):
    block_e, n_valid, first, nxt, slot, n_used = plan
    n_pad = xs.shape[0]
    n_exp, d, two_ff = w_gu.shape
    d_ff = two_ff // 2
    nblk = n_pad // EXPERT_ROWS
    rows = lambda i, be, nu, *_: (jnp.minimum(i, nu[0] - 1), 0)
    per_e = lambda i, be, *_: (be[i], 0, 0)
    grid_spec = pltpu.PrefetchScalarGridSpec(
        num_scalar_prefetch=6,
        grid=(nblk,),
        in_specs=[pl.BlockSpec((EXPERT_ROWS, d // 2), rows),
                  pl.BlockSpec(memory_space=pl.ANY), pl.BlockSpec((1, 1, two_ff), per_e),
                  pl.BlockSpec(memory_space=pl.ANY), pl.BlockSpec((1, 1, d), per_e)],
        out_specs=pl.BlockSpec((EXPERT_ROWS, d // 2), rows),
        scratch_shapes=[pltpu.VMEM((2, d, two_ff), F32), pltpu.VMEM((2, d_ff, d), F32),
                        pltpu.VMEM((d, two_ff), BF16), pltpu.VMEM((d_ff, d), BF16),
                        pltpu.SemaphoreType.DMA((2, 2))],
    )
    flat = lambda a: a.reshape(-1)
    return pl.pallas_call(
        functools.partial(_expert_kernel, d_ff=d_ff),
        grid_spec=grid_spec,
        out_shape=jax.ShapeDtypeStruct((n_pad, d // 2), U32),
        compiler_params=pltpu.CompilerParams(dimension_semantics=("arbitrary",),
                                             vmem_limit_bytes=EXPERT_VMEM_BYTES),
        name="experts",
    )(flat(block_e), n_used, flat(n_valid), flat(first), flat(nxt), flat(slot),
      xs, w_gu, b_gu.reshape(n_exp, 1, two_ff), w_down, b_down.reshape(n_exp, 1, d))


def _combine_kernel(y4_ref, wt_ref, h_ref, mod_ref, fw_ref, o_ref):
    wt = wt_ref[...]
    acc_lo, acc_hi = None, None
    for j in range(TOP_K):
        lo, hi = _unpack_bf16_pairs(y4_ref[j])
        w = wt[:, j:j + 1]
        acc_lo = w * lo if j == 0 else acc_lo + w * lo
        acc_hi = w * hi if j == 0 else acc_hi + w * hi
    acc = jnp.concatenate([acc_lo, acc_hi], axis=1)
    m = mod_ref[0]
    o_ref[...] = _rmsnorm(h_ref[...] + m[5:6] * acc, fw_ref[...])


def _combine_call(y4, wts, h, mod, fw, prev_out, *, n, tg, first_tile):
    d = h.shape[1]
    t_per_b = n // mod.shape[0]
    part = lambda width: pl.BlockSpec((tg, width), lambda i: (i, 0))
    row = lambda width: pl.BlockSpec((tg, width), lambda i: (i + first_tile, 0))
    in_specs = [pl.BlockSpec((TOP_K, tg, d // 2), lambda i: (0, i, 0)),
                part(LANES), part(d),
                pl.BlockSpec((1, N_MOD, d), lambda i: ((i + first_tile) * tg // t_per_b, 0, 0)),
                pl.BlockSpec((1, d), lambda i: (0, 0))]
    args = [y4, wts, h, mod, fw]
    kern = _combine_kernel
    aliases = {}
    if prev_out is not None:
        in_specs.append(pl.BlockSpec(memory_space=pl.ANY))
        args.append(prev_out)
        kern = lambda y4_ref, wt_ref, h_ref, mod_ref, fw_ref, prev_ref, o_ref: _combine_kernel(
            y4_ref, wt_ref, h_ref, mod_ref, fw_ref, o_ref)
        aliases = {len(args) - 1: 0}
    return pl.pallas_call(
        kern,
        grid=(y4.shape[1] // tg,),
        in_specs=in_specs,
        out_specs=row(d),
        out_shape=jax.ShapeDtypeStruct((n, d), F32),
        input_output_aliases=aliases,
        compiler_params=pltpu.CompilerParams(dimension_semantics=("arbitrary",)),
        name="combine",
    )(*args)


def kernel(x, c, ctx, c_ctx, w_ada, b_ada, norm_mix_w, norm_mlp_w, w_in, w_gk_f, b_gk_f, w_gk_b, b_gk_b,
           gla_norm_w, w_pool, pool_scale, w_out, w_router, b_router, w_gu, b_gu, w_down, b_down,
           final_norm_w):
    b, t, d = x.shape
    assert w_ada.shape[0] == 1, "single-layer trunk"
    n_exp = w_router.shape[2]
    rank = w_gk_f.shape[1]
    qk = w_gk_f.shape[2]
    dk = qk // GLA_HEADS
    gw = GLA_HEADS * gla_norm_w.shape[1]
    pw = w_pool.shape[1] * w_pool.shape[2]
    assert w_in.shape[2] == 2 * qk + 2 * gw + 2 * rank + pw and 2 * rank <= LANES
    assert t % SUPER == 0 and ctx.shape[1] % SUPER == 0 and n_exp <= LANES

    rows = -(-(b + 1) // 8) * 8
    cc = jnp.concatenate([c, c_ctx[None, :], jnp.zeros((rows - b - 1, d), F32)], axis=0)
    mod = _mod_call(cc, w_ada[0], b_ada)
    mod_x = mod[:b].reshape(b, N_MOD, d)
    mod_c = mod[b:b + 1].reshape(1, N_MOD, d)

    wi = w_in[0]
    o_r = 2 * qk + 2 * gw
    w_cat = jnp.concatenate([wi[:, :o_r], wi[:, o_r + 2 * rank:], wi[:, o_r:o_r + 2 * rank],
                             jnp.zeros((d, LANES - 2 * rank), F32)], axis=1).astype(BF16)
    wgk = jnp.concatenate([jnp.pad(w_gk_f[0], ((0, 0), (0, qk))), jnp.pad(w_gk_b[0], ((0, 0), (qk, 0))),
                           jnp.zeros((LANES - 2 * rank, 2 * qk), F32)], axis=0)
    bgk = jnp.concatenate([b_gk_f[0], b_gk_b[0]])[None, :]
    proj = functools.partial(_inproj_call, nw=norm_mix_w, w=w_cat, wgk=wgk.astype(BF16), bgk=bgk,
                             qk=qk, gw=gw, pw=pw, dk=dk)
    q, k, v, vt, g, xp, gk = proj(x, mod_x, tm=1024)
    _, kc, _, vtc, _, _, gkc = proj(ctx, mod_c, tm=SUPER)

    gla = _gla_call(q, k, v, vt, gk, g, kc, vtc, gkc, gla_norm_w)
    pool = _pool_call(xp, w_pool[0].astype(BF16), pool_scale)

    n = b * t
    wr = jnp.pad(w_router[0], ((0, 0), (0, LANES - n_exp)))
    br = jnp.pad(b_router, ((0, 0), (0, LANES - n_exp)))
    n_part = n // MOE_PARTS
    tm, tg = 512, 256
    assert n % MOE_PARTS == 0 and n_part % (SC_WORKERS * SC_WINDOW) == 0 and n_part % tm == 0 and n_part < RANK_LIMIT
    assert (n_part * TOP_K) % (SC_WORKERS * 2 * SC_GATHER_WINDOW) == 0
    n_pad = n_part * TOP_K + n_exp * EXPERT_ROWS
    nblk = n_pad // EXPERT_ROWS
    assert nblk <= 8 * LANES
    parts = range(MOE_PARTS)
    routed = [_route_call(gla.reshape(n, gw), pool.reshape(n, pw), x.reshape(n, d), mod_x, w_out[0].astype(BF16),
                          norm_mlp_w, wr, br, tm=tm, n_exp=n_exp,
                          first_tile=p * n_part // tm, n_tiles=n_part // tm) for p in parts]
    plans = [_plan_call(cnt[0, :n_exp], code_t, n_exp=n_exp, rows=EXPERT_ROWS, chunk=min(n_part, 4096))
             for (_, _, code_t, _, cnt) in routed]
    dests = [plan[0][:TOP_K] for plan in plans]
    xs = [_sc_scatter_call(routed[p][1],
                           dests[p].reshape(TOP_K, n_part // SC_WINDOW, SC_WINDOW).transpose(1, 0, 2), n_out=n_pad)
          for p in parts]
    ys = [_expert_call(plans[p][1:], xs[p], w_gu[0], b_gu[0], w_down[0], b_down[0]) for p in parts]
    y4 = [_sc_gather_call(ys[p], dests[p].reshape(SC_WORKERS, -1, SC_GATHER_WINDOW)).reshape(TOP_K, n_part, d // 2)
          for p in parts]
    out = None
    for p in parts:
        out = _combine_call(y4[p], routed[p][3], routed[p][0], mod_x, final_norm_w[None, :], out,
                            n=n, tg=tg, first_tile=p * n_part // tg)
    return out.reshape(b, t, d)
```

```python
import functools

import numpy as np
import jax
import jax.numpy as jnp
from jax import lax
from jax.experimental import pallas as pl
from jax.experimental.pallas import tpu as pltpu
from jax.experimental.pallas import tpu_sc as plsc

F32 = jnp.float32
BF16 = jnp.bfloat16
I32 = jnp.int32
U32 = jnp.uint32

GRID_W = 64
GLA_HEADS = 4
GLA_CHUNK = 64
GATE_NORMALIZER = 16.0
POOL_WINDOWS = (2, 4, 8, 16)
POOL_PAD_GRID_ROWS = 8
TOP_K = 4
RANK_LIMIT = 1 << 20
SWIGLU_LIMIT = 7.0
SWIGLU_ALPHA = 1.702
N_MOD = 6
EPS = 1e-6

LANES = 128
SUPER = 256
HEAD_PAIR_DK = 128
EXPERT_ROWS = 512
EXPERT_ROW_STEP = 128
MOE_PARTS = 2
EXPERT_VMEM_BYTES = 56 * 1024 * 1024
SC_CORES = 2
SC_SUBCORES = 16
SC_WORKERS = SC_CORES * SC_SUBCORES
SC_WINDOW = 32
SC_GATHER_WINDOW = 64


def _dot(a, b):
    return jnp.dot(a, b, preferred_element_type=F32)


def _dot_nt(a, b):
    return lax.dot_general(a, b, (((1,), (1,)), ((), ())), preferred_element_type=F32)


def _split_bf16(x):
    hi = x.astype(BF16)
    lo = (x - hi.astype(F32)).astype(BF16)
    return hi, lo


def _pack_bf16_pairs(x):
    c = x.shape[1] // 2
    lo = lax.bitcast_convert_type(x[:, :c].astype(BF16).astype(F32), U32)
    hi = lax.bitcast_convert_type(x[:, c:].astype(BF16).astype(F32), U32)
    return (lo >> 16) | hi


def _unpack_bf16_pairs(p):
    lo = lax.bitcast_convert_type(p << 16, F32)
    hi = lax.bitcast_convert_type(p & jnp.uint32(0xFFFF0000), F32)
    return lo, hi


def _rmsnorm(x, w):
    var = jnp.mean(x * x, axis=-1, keepdims=True)
    return x * lax.rsqrt(var + EPS) * w


def _mod_kernel(c_ref, w_ref, b_ref, o_ref):
    c = c_ref[...]
    s = c * jax.nn.sigmoid(c)
    o_ref[...] = jnp.dot(s, w_ref[...], precision=lax.Precision.HIGHEST,
                         preferred_element_type=F32) + b_ref[...]


def _mod_call(cc, w_ada, b_ada):
    rows, d = cc.shape
    n = w_ada.shape[1]
    tn = 1024
    return pl.pallas_call(
        _mod_kernel,
        grid=(n // tn,),
        in_specs=[pl.BlockSpec((rows, d), lambda j: (0, 0)),
                  pl.BlockSpec((d, tn), lambda j: (0, j)),
                  pl.BlockSpec((1, tn), lambda j: (0, j))],
        out_specs=pl.BlockSpec((rows, tn), lambda j: (0, j)),
        out_shape=jax.ShapeDtypeStruct((rows, n), F32),
        name="mod",
    )(cc, w_ada, b_ada)


def _inproj_kernel(x_ref, mod_ref, nw_ref, w_ref, wgk_ref, bgk_ref,
                   q_ref, k_ref, v_ref, vt_ref, g_ref, p_ref, gk_ref, *, qk, gw, pw, dk):
    x = x_ref[0]
    m = mod_ref[0]
    hm = (_rmsnorm(x, nw_ref[...]) * (1.0 + m[1:2]) + m[0:1]).astype(BF16)
    p = _dot(hm, w_ref[...])
    vt = p[:, 2 * qk:2 * qk + gw].T
    for s in range(vt_ref.shape[1]):
        vt_ref[0, s] = vt[:, s * SUPER:(s + 1) * SUPER].astype(BF16)
    o = 0
    q_ref[0] = (p[:, o:o + qk] * (dk ** -0.5)).astype(BF16); o += qk
    k_ref[0] = p[:, o:o + qk].astype(BF16); o += qk
    v_ref[0] = p[:, o:o + gw].astype(BF16); o += gw
    g_ref[0] = p[:, o:o + gw].astype(BF16); o += gw
    p_ref[0] = p[:, o:o + pw].astype(BF16); o += pw
    r = p[:, o:o + LANES]
    z = _dot(r.astype(BF16), wgk_ref[...]) + bgk_ref[...]
    gk_ref[0] = (jnp.minimum(z, 0.0) - jnp.log1p(jnp.exp(-jnp.abs(z)))) * (1.0 / GATE_NORMALIZER)


def _inproj_call(x, mod, nw, w, wgk, bgk, *, qk, gw, pw, dk, tm):
    b, t, d = x.shape
    n_in = w.shape[1]
    bs = lambda width: pl.BlockSpec((1, tm, width), lambda i, j: (i, j, 0))
    const = lambda shape: pl.BlockSpec(shape, lambda i, j: (0,) * len(shape))
    per_batch = mod.shape[0] > 1
    sds = jax.ShapeDtypeStruct
    return pl.pallas_call(
        functools.partial(_inproj_kernel, qk=qk, gw=gw, pw=pw, dk=dk),
        grid=(b, t // tm),
        in_specs=[bs(d),
                  pl.BlockSpec((1, N_MOD, d), (lambda i, j: (i, 0, 0)) if per_batch else (lambda i, j: (0, 0, 0))),
                  const((1, d)), const((d, n_in)), const((LANES, 2 * qk)), const((1, 2 * qk))],
        out_specs=[bs(qk), bs(qk), bs(gw),
                   pl.BlockSpec((1, tm // SUPER, gw, SUPER), lambda i, j: (i, j, 0, 0)),
                   bs(gw), bs(pw), bs(2 * qk)],
        out_shape=[sds((b, t, qk), BF16), sds((b, t, qk), BF16), sds((b, t, gw), BF16),
                   sds((b, t // SUPER, gw, SUPER), BF16),
                   sds((b, t, gw), BF16), sds((b, t, pw), BF16), sds((b, t, 2 * qk), F32)],
        compiler_params=pltpu.CompilerParams(dimension_semantics=("arbitrary", "arbitrary")),
        name="inproj",
    )(x, mod, nw, w, wgk, bgk)


def _gla_super(q, k, v, vt, gk, cm, amask, bd_mask, st, fwd, want_out):
    nch = SUPER // GLA_CHUNK
    order = tuple(range(nch)) if fwd else tuple(reversed(range(nch)))
    last_row = GLA_CHUNK - 1 if fwd else 0
    mid_row = GLA_CHUNK // 2 - 1 if fwd else GLA_CHUNK // 2
    hi, lo = _split_bf16(gk)
    bcum = _dot(cm, hi) + _dot(cm, lo)

    def chunk_row(r):
        return jnp.concatenate(
            [jnp.broadcast_to(bcum[c * GLA_CHUNK + r:c * GLA_CHUNK + r + 1, :], (GLA_CHUNK, bcum.shape[1]))
             for c in range(nch)], axis=0)

    chunk_of_row = lax.broadcasted_iota(I32, bcum.shape, 0) // GLA_CHUNK

    def by_chunk(x):
        return jnp.concatenate([jnp.where(chunk_of_row == c, x, 0.0).astype(BF16) for c in range(nch)], axis=1)

    blast = chunk_row(last_row)
    u_all = _dot(vt, by_chunk(k * jnp.exp(blast - bcum)))
    before = [None] * nch
    for c in order:
        before[c] = st
        decay = jnp.exp(bcum[c * GLA_CHUNK + last_row:c * GLA_CHUNK + last_row + 1, :])
        st = st * decay + jnp.where(bd_mask, u_all[:, c * HEAD_PAIR_DK:(c + 1) * HEAD_PAIR_DK], 0.0)
    if not want_out:
        return None, st
    bmid = chunk_row(mid_row)
    qt = q * jnp.exp(bcum - bmid)
    kt = (k * jnp.exp(bmid - bcum)).astype(BF16)
    lane = lax.broadcasted_iota(I32, qt.shape, 1)
    half = HEAD_PAIR_DK // 2
    o_heads = []
    for hh in range(2):
        sel = (lane < half) if hh == 0 else (lane >= half)
        a = _dot_nt(jnp.where(sel, qt, 0.0).astype(BF16), kt)
        a = jnp.where(amask, a, 0.0).astype(BF16)
        o_heads.append(_dot(a, v[:, hh * LANES:(hh + 1) * LANES]))
    qh = (q * jnp.exp(bcum)).astype(BF16)
    o_inter = jnp.concatenate(
        [_dot_nt(qh[c * GLA_CHUNK:(c + 1) * GLA_CHUNK], before[c].astype(BF16)) for c in range(nch)], axis=0)
    return jnp.concatenate(o_heads, axis=1) + o_inter, st


def _gla_kernel(q_ref, k_ref, v_ref, vt_ref, gkf_ref, gkb_ref, g_ref, kc_ref, vtc_ref, gkfc_ref, gkbc_ref,
                nw_ref, cmf_ref, cmb_ref, o_ref, stf_ref, stb_ref, of_ref, ob_ref):
    t = q_ref.shape[1]
    tc = kc_ref.shape[1]
    nsc, nscc = t // SUPER, tc // SUPER
    cmf = cmf_ref[...]
    cmb = cmb_ref[...]
    amask_f = cmf > 0
    amask_b = cmb > 0
    row = lax.broadcasted_iota(I32, (2 * LANES, HEAD_PAIR_DK), 0)
    lane = lax.broadcasted_iota(I32, (2 * LANES, HEAD_PAIR_DK), 1)
    bd_mask = (row < LANES) == (lane < HEAD_PAIR_DK // 2)

    def ctx_state(gk_ref, cm, fwd, j, st):
        rows = pl.ds(j * SUPER, SUPER)
        return _gla_super(None, kc_ref[0, rows, :].astype(F32), None, vtc_ref[0, j], gk_ref[0, rows, :],
                          cm, None, bd_mask, st, fwd, False)[1]

    def latent(gk_ref, cm, amask, fwd, j, st):
        rows = pl.ds(pl.multiple_of(j * SUPER, SUPER), SUPER)
        return _gla_super(q_ref[0, rows, :].astype(F32), k_ref[0, rows, :].astype(F32), v_ref[0, rows, :],
                          vt_ref[0, j], gk_ref[0, rows, :], cm, amask, bd_mask, st, fwd, True)

    st = jnp.zeros(stf_ref.shape, F32)
    for j in range(nscc):
        st = ctx_state(gkfc_ref, cmf, True, j, st)
    stf_ref[...] = st
    st = jnp.zeros(stb_ref.shape, F32)
    for j in reversed(range(nscc)):
        st = ctx_state(gkbc_ref, cmb, False, j, st)
    stb_ref[...] = st

    def scan_body(jj, carry):
        jb = nsc - 1 - jj
        of, stf = latent(gkf_ref, cmf, amask_f, True, jj, stf_ref[...])
        of_ref[pl.ds(pl.multiple_of(jj * SUPER, SUPER), SUPER), :] = of
        stf_ref[...] = stf
        ob, stb = latent(gkb_ref, cmb, amask_b, False, jb, stb_ref[...])
        ob_ref[pl.ds(pl.multiple_of(jb * SUPER, SUPER), SUPER), :] = ob
        stb_ref[...] = stb
        return carry

    lax.fori_loop(0, nsc, scan_body, 0, unroll=2)

    nw = nw_ref[...]

    def out_body(j, carry):
        rows = pl.ds(pl.multiple_of(j * SUPER, SUPER), SUPER)
        o = of_ref[rows, :] + ob_ref[rows, :]
        g = g_ref[0, rows, :].astype(F32)
        gate = g * jax.nn.sigmoid(g)
        for hh in range(2):
            oh = o[:, hh * LANES:(hh + 1) * LANES]
            on = oh * lax.rsqrt(jnp.mean(oh * oh, axis=-1, keepdims=True) + EPS) * nw
            o_ref[0, rows, hh * LANES:(hh + 1) * LANES] = (on * gate[:, hh * LANES:(hh + 1) * LANES]).astype(BF16)
        return carry

    lax.fori_loop(0, nsc, out_body, 0)


def _gla_masks():
    i = np.arange(SUPER)
    same = (i[:, None] // GLA_CHUNK) == (i[None, :] // GLA_CHUNK)
    fwd = same & (i[None, :] <= i[:, None])
    bwd = same & (i[None, :] >= i[:, None])
    return jnp.asarray(fwd, BF16), jnp.asarray(bwd, BF16)


def _gla_call(q, k, v, vt, gk, g, kc, vtc, gkc, nw):
    b, t, qk = q.shape
    tc = kc.shape[1]
    npair = qk // HEAD_PAIR_DK
    cmf, cmb = _gla_masks()
    lat = lambda width, off: pl.BlockSpec((1, t, width), lambda i, j: (i, 0, j + off))
    ctx = lambda width, off: pl.BlockSpec((1, tc, width), lambda i, j: (i, 0, j + off))
    tr = lambda n_groups: pl.BlockSpec((1, n_groups, 2 * LANES, SUPER), lambda i, j: (i, 0, j, 0))
    const = lambda shape: pl.BlockSpec(shape, lambda i, j: (0,) * len(shape))
    return pl.pallas_call(
        _gla_kernel,
        grid=(b, npair),
        in_specs=[lat(HEAD_PAIR_DK, 0), lat(HEAD_PAIR_DK, 0), lat(2 * LANES, 0), tr(t // SUPER),
                  lat(HEAD_PAIR_DK, 0), lat(HEAD_PAIR_DK, npair), lat(2 * LANES, 0),
                  ctx(HEAD_PAIR_DK, 0), tr(tc // SUPER), ctx(HEAD_PAIR_DK, 0), ctx(HEAD_PAIR_DK, npair),
                  const((1, LANES)), const(cmf.shape), const(cmb.shape)],
        out_specs=lat(2 * LANES, 0),
        out_shape=jax.ShapeDtypeStruct((b, t, v.shape[2]), BF16),
        scratch_shapes=[pltpu.VMEM((2 * LANES, HEAD_PAIR_DK), F32), pltpu.VMEM((2 * LANES, HEAD_PAIR_DK), F32),
                        pltpu.VMEM((t, 2 * LANES), F32), pltpu.VMEM((t, 2 * LANES), F32)],
        compiler_params=pltpu.CompilerParams(dimension_semantics=("arbitrary", "arbitrary")),
        name="gla",
    )(q, k, v, vt, gk, gk, g, kc, vtc, gkc, gkc, nw, cmf, cmb)


def _pool_kernel(x_ref, cm_ref, cnt_ref, wp_ref, ps_ref, o_ref, a_ref, b_ref):
    t = x_ref.shape[1]
    pad = POOL_PAD_GRID_ROWS * GRID_W
    total = t + 2 * pad
    for gi, w in enumerate(POOL_WINDOWS):
        lo = w // 2
        cols = slice(gi * LANES, (gi + 1) * LANES)
        cmat = cm_ref[gi]
        a_ref[0:pad, :] = jnp.zeros((pad, LANES), F32)
        a_ref[pad + t:total, :] = jnp.zeros((pad, LANES), F32)
        for blk in range(t // SUPER):
            rs = slice(blk * SUPER, (blk + 1) * SUPER)
            a_ref[pad + blk * SUPER:pad + (blk + 1) * SUPER, :] = _dot(cmat, x_ref[0, rs, cols])
        src, dst = a_ref, b_ref
        m = 1
        while m < w:
            sh = m * GRID_W
            dst[0:total - sh, :] = src[0:total - sh, :] + src[sh:total, :]
            src, dst = dst, src
            m *= 2
        first = pad - lo * GRID_W
        pooled = src[first:first + t, :] / cnt_ref[gi] - x_ref[0, :, cols].astype(F32)
        yp = _dot(pooled.astype(BF16), wp_ref[gi]) * ps_ref[:, cols]
        o_ref[0, :, cols] = yp.astype(BF16)


def _pool_col_mats():
    i = np.arange(SUPER)
    same_row = (i[:, None] // GRID_W) == (i[None, :] // GRID_W)
    d = i[None, :] - i[:, None]
    mats = []
    for w in POOL_WINDOWS:
        lo = w // 2
        hi = w - 1 - lo
        mats.append(same_row & (d >= -lo) & (d <= hi))
    return jnp.asarray(np.stack(mats), BF16)


def _pool_counts(t):
    rows = t // GRID_W
    r = np.arange(t) // GRID_W
    c = np.arange(t) % GRID_W
    out = []
    for w in POOL_WINDOWS:
        lo = w // 2
        hi = w - 1 - lo
        cnt_r = np.minimum(r + hi + 1, rows) - np.maximum(r - lo, 0)
        cnt_c = np.minimum(c + hi + 1, GRID_W) - np.maximum(c - lo, 0)
        out.append(np.broadcast_to((cnt_r * cnt_c).astype(np.float32)[:, None], (t, LANES)))
    return jnp.asarray(np.stack(out))


def _pool_call(xp, w_pool, pool_scale):
    b, t, pw = xp.shape
    ng = len(POOL_WINDOWS)
    assert max(POOL_WINDOWS) // 2 <= POOL_PAD_GRID_ROWS and t % GRID_W == 0
    cm = _pool_col_mats()
    cnt = _pool_counts(t)
    staged = t + 2 * POOL_PAD_GRID_ROWS * GRID_W
    const = lambda shape: pl.BlockSpec(shape, lambda i: (0,) * len(shape))
    return pl.pallas_call(
        _pool_kernel,
        grid=(b,),
        in_specs=[pl.BlockSpec((1, t, pw), lambda i: (i, 0, 0)),
                  const(cm.shape), const(cnt.shape), const((ng, LANES, LANES)), const((1, pw))],
        out_specs=pl.BlockSpec((1, t, pw), lambda i: (i, 0, 0)),
        out_shape=jax.ShapeDtypeStruct((b, t, pw), BF16),
        scratch_shapes=[pltpu.VMEM((staged, LANES), F32), pltpu.VMEM((staged, LANES), F32)],
        compiler_params=pltpu.CompilerParams(dimension_semantics=("arbitrary",)),
        name="pool",
    )(xp, cm, cnt, w_pool, pool_scale)


def _route_kernel(gla_ref, pool_ref, x_ref, mod_ref, wo_ref, nw_ref, wr_ref, br_ref, lt_ref,
                  h_ref, xt_ref, code_ref, wt_ref, cnt_ref, run_ref, wr2_ref, *, gw, n_exp):
    i = pl.program_id(0)

    @pl.when(i == 0)
    def _():
        run_ref[...] = jnp.zeros_like(run_ref)
        wh, wl = _split_bf16(wr_ref[...])
        wr2_ref[:, :LANES] = wh
        wr2_ref[:, LANES:] = wl

    m = mod_ref[0]
    acc = _dot(gla_ref[...], wo_ref[0:gw, :]) + _dot(pool_ref[...], wo_ref[gw:, :])
    h = x_ref[...] + m[2:3] * acc
    h_ref[...] = h
    xt = _rmsnorm(h, nw_ref[...]) * (1.0 + m[4:5]) + m[3:4]
    xt_ref[...] = _pack_bf16_pairs(xt)
    xh, xl = _split_bf16(xt)
    wr2 = wr2_ref[...]
    t1 = _dot(xh, wr2)
    logits = t1[:, :LANES] + t1[:, LANES:] + _dot(xl, wr2[:, :LANES]) + br_ref[...]
    lane = lax.broadcasted_iota(I32, logits.shape, 1)
    neg = jnp.float32(-jnp.inf)
    logits = jnp.where(lane < n_exp, logits, neg)
    vals, hots = [], []
    e_out = jnp.zeros(logits.shape, I32)
    lane_f = lane.astype(F32)
    for j in range(TOP_K):
        mx = jnp.max(logits, axis=-1, keepdims=True)
        idx = jnp.min(jnp.where(logits == mx, lane_f, float(LANES)), axis=-1, keepdims=True)
        hot = lane_f == idx
        vals.append(mx)
        hots.append(hot)
        e_out = jnp.where(lane == j, idx.astype(I32), e_out)
        logits = jnp.where(hot, neg, logits)
    ex = [jnp.exp(v - vals[0]) for v in vals]
    den = ex[0] + ex[1] + ex[2] + ex[3]
    w_out = jnp.zeros(logits.shape, F32)
    for j in range(TOP_K):
        w_out = jnp.where(lane == j, ex[j] / den, w_out)
    osum = jnp.where(hots[0] | hots[1] | hots[2] | hots[3], 1.0, 0.0)
    before = _dot(lt_ref[...], osum.astype(BF16)) + run_ref[0:1, :]
    rk_out = jnp.zeros(logits.shape, I32)
    for j in range(TOP_K):
        rj = jnp.sum(jnp.where(hots[j], before, 0.0), axis=-1, keepdims=True)
        rk_out = jnp.where(lane == j, rj.astype(I32), rk_out)
    run = run_ref[0:1, :] + jnp.sum(osum, axis=0, keepdims=True)
    run_ref[...] = jnp.broadcast_to(run, run_ref.shape)
    code = e_out * RANK_LIMIT + rk_out
    code_ref[...] = code.T[:code_ref.shape[0], :]
    wt_ref[...] = w_out
    cnt_ref[...] = jnp.broadcast_to(run, cnt_ref.shape).astype(I32)


def _route_call(gla, pool, x, mod, w_out, nw, wr, br, *, tm, n_exp, first_tile, n_tiles):
    n_all, d = x.shape
    n = n_tiles * tm
    gw = gla.shape[1]
    t_per_b = n_all // mod.shape[0]
    lt = jnp.asarray(np.tril(np.ones((tm, tm), np.float32), -1), BF16)
    row_in = lambda width: pl.BlockSpec((tm, width), lambda i: (i + first_tile, 0))
    row = lambda width: pl.BlockSpec((tm, width), lambda i: (i, 0))
    const = lambda shape: pl.BlockSpec(shape, lambda i: (0,) * len(shape))
    sds = jax.ShapeDtypeStruct
    return pl.pallas_call(
        functools.partial(_route_kernel, gw=gw, n_exp=n_exp),
        grid=(n_tiles,),
        in_specs=[row_in(gw), row_in(pool.shape[1]), row_in(d),
                  pl.BlockSpec((1, N_MOD, d), lambda i: ((i + first_tile) * tm // t_per_b, 0, 0)),
                  const(w_out.shape), const((1, d)), const(wr.shape), const((1, LANES)), const((tm, tm))],
        out_specs=[row(d), row(d // 2), pl.BlockSpec((8, tm), lambda i: (0, i)), row(LANES), const((8, LANES))],
        out_shape=[sds((n, d), F32), sds((n, d // 2), U32), sds((8, n), I32),
                   sds((n, LANES), F32), sds((8, LANES), I32)],
        scratch_shapes=[pltpu.VMEM((8, LANES), F32), pltpu.VMEM((d, 2 * LANES), BF16)],
        compiler_params=pltpu.CompilerParams(dimension_semantics=("arbitrary",)),
        name="route",
    )(gla, pool, x, mod, w_out, nw, wr, br, lt)


def _plan_kernel(cnt_ref, code_ref, dest_ref, be_ref, nv_ref, first_ref, next_ref, slot_ref, nu_ref, start_ref,
                 *, n_exp, rows):
    @pl.when(pl.program_id(0) == 0)
    def _():
        blk = (lax.broadcasted_iota(I32, be_ref.shape, 0) * LANES + lax.broadcasted_iota(I32, be_ref.shape, 1))
        blk_row0 = blk * rows
        nxt_e = [None] * n_exp
        nxt = jnp.int32(-1)
        for e in reversed(range(n_exp)):
            nxt_e[e] = nxt
            nxt = jnp.where(cnt_ref[e] > 0, e, nxt)
        zeros = jnp.zeros(be_ref.shape, I32)
        be, end_valid, first, nxt_blk, slot = zeros, zeros, zeros, zeros - 1, zeros
        acc = jnp.int32(0)
        last_e = jnp.int32(0)
        ordinal = jnp.int32(0)
        for e in range(n_exp):
            c = cnt_ref[e]
            start_ref[e] = acc
            in_e = (blk_row0 >= acc) & (c > 0)
            end_valid = jnp.where(in_e, acc + c, end_valid)
            be = jnp.where(in_e, e, be)
            first = jnp.where(in_e, (blk_row0 == acc).astype(I32), first)
            nxt_blk = jnp.where(in_e, nxt_e[e], nxt_blk)
            slot = jnp.where(in_e, ordinal % 2, slot)
            acc = acc + (c + rows - 1) // rows * rows
            last_e = jnp.where(c > 0, e, last_e)
            ordinal = ordinal + (c > 0).astype(I32)
        n_used = acc // rows
        nu_ref[0] = n_used
        be_ref[...] = jnp.where(blk < n_used, be, last_e)
        nv_ref[...] = jnp.clip(end_valid - blk_row0, 0, rows)
        first_ref[...] = first
        next_ref[...] = nxt_blk
        slot_ref[...] = slot

    code = code_ref[...]
    e_vec = code // RANK_LIMIT
    dest = code % RANK_LIMIT
    for e in range(n_exp):
        dest = dest + jnp.where(e_vec == e, start_ref[e], 0)
    dest_ref[...] = dest


def _plan_call(counts, code_t, *, n_exp, rows, chunk):
    n = code_t.shape[1]
    sds = jax.ShapeDtypeStruct
    smem = pltpu.SMEM
    return pl.pallas_call(
        functools.partial(_plan_kernel, n_exp=n_exp, rows=rows),
        grid=(n // chunk,),
        in_specs=[pl.BlockSpec(memory_space=smem), pl.BlockSpec((8, chunk), lambda i: (0, i))],
        out_specs=[pl.BlockSpec((8, chunk), lambda i: (0, i))]
        + [pl.BlockSpec((8, LANES), lambda i: (0, 0))] * 5 + [pl.BlockSpec(memory_space=smem)],
        out_shape=[sds((8, n), I32)] + [sds((8, LANES), I32)] * 5 + [sds((1,), I32)],
        scratch_shapes=[pltpu.SMEM((n_exp,), I32)],
        compiler_params=pltpu.CompilerParams(dimension_semantics=("arbitrary",)),
        name="plan",
    )(counts, code_t)


def _sc_worker_id():
    return lax.axis_index("s") * SC_CORES + lax.axis_index("c")


def _sc_scatter_call(x, idx3, *, n_out):
    n, d = x.shape
    n_win_total, k, w = idx3.shape
    n_win = n_win_total // SC_WORKERS
    mesh = plsc.VectorSubcoreMesh(core_axis_name="c", subcore_axis_name="s")

    @functools.partial(
        pl.kernel, mesh=mesh,
        out_type=jax.ShapeDtypeStruct((n_out, d), x.dtype),
        scratch_types=[pltpu.VMEM((k, w), I32), pltpu.VMEM((w, d), x.dtype), pltpu.SemaphoreType.DMA],
        name="sc_dispatch",
    )
    def kern(x_hbm, idx_hbm, out_hbm, idx_v, rows_v, sem):
        wid = _sc_worker_id()

        @pl.loop(0, n_win)
        def _(i):
            win = wid * n_win + i
            pltpu.sync_copy(idx_hbm.at[win], idx_v)
            pltpu.sync_copy(x_hbm.at[pl.ds(win * w, w)], rows_v)
            for j in range(k):
                pltpu.async_copy(rows_v, out_hbm.at[idx_v.at[j]], sem).wait()

    return kern(x, idx3)


def _sc_gather_call(table, idx3):
    n_workers, n_win, w = idx3.shape
    d = table.shape[1]
    assert n_workers == SC_WORKERS and n_win % 2 == 0
    mesh = plsc.VectorSubcoreMesh(core_axis_name="c", subcore_axis_name="s")

    @functools.partial(
        pl.kernel, mesh=mesh,
        out_type=jax.ShapeDtypeStruct((n_workers * n_win * w, d), table.dtype),
        scratch_types=[pltpu.VMEM((n_win, w), I32), pltpu.VMEM((2, w, d), table.dtype),
                       pltpu.SemaphoreType.DMA((2,)), pltpu.SemaphoreType.DMA((2,))],
        name="sc_gather",
    )
    def kern(table_hbm, idx_hbm, out_hbm, idx_v, rows_v, gsem, osem):
        wid = _sc_worker_id()
        base = wid * n_win
        pltpu.sync_copy(idx_hbm.at[wid], idx_v)

        def gather(wi, b):
            return pltpu.make_async_copy(table_hbm.at[idx_v.at[wi]], rows_v.at[b], gsem.at[b])

        def put(wi, b):
            return pltpu.make_async_copy(rows_v.at[b], out_hbm.at[pl.ds((base + wi) * w, w)], osem.at[b])

        gather(0, 0).start()

        @pl.loop(0, n_win, step=2)
        def _(i):
            for b in range(2):
                wi = i + b

                @pl.when(wi + 1 < n_win)
                def _():
                    @pl.when(wi >= 1)
                    def _():
                        put(wi - 1, 1 - b).wait()
                    gather(wi + 1, 1 - b).start()

                gather(wi, b).wait()
                put(wi, b).start()

        put(n_win - 2, 0).wait()
        put(n_win - 1, 1).wait()

    return kern(table, idx3)


def _expert_kernel(be_ref, nu_ref, nv_ref, first_ref, next_ref, slot_ref,
                   x_ref, wgu_hbm, bgu_ref, wd_hbm, bd_ref, y_ref,
                   wgu_f32_ref, wd_f32_ref, wgu_bf_ref, wd_bf_ref, sem, *, d_ff):
    i = pl.program_id(0)
    slot = slot_ref[i]

    def fetch(e, s):
        return (pltpu.make_async_copy(wgu_hbm.at[e], wgu_f32_ref.at[s], sem.at[0, s]),
                pltpu.make_async_copy(wd_hbm.at[e], wd_f32_ref.at[s], sem.at[1, s]))

    @pl.when(i == 0)
    def _():
        for cp in fetch(be_ref[0], slot):
            cp.start()

    @pl.when(first_ref[i] == 1)
    def _():
        for cp in fetch(be_ref[i], slot):
            cp.wait()

        @pl.when(next_ref[i] >= 0)
        def _():
            for cp in fetch(next_ref[i], 1 - slot):
                cp.start()

        wgu_bf_ref[...] = wgu_f32_ref[slot].astype(BF16)
        wd_bf_ref[...] = wd_f32_ref[slot].astype(BF16)

    n_valid = nv_ref[i]
    n_rows = x_ref.shape[0]

    def expert_rows(rows):
        row = lax.broadcasted_iota(I32, (rows, x_ref.shape[1]), 0)
        lo, hi = _unpack_bf16_pairs(jnp.where(row < n_valid, x_ref[0:rows, :], jnp.uint32(0)))
        xb = jnp.concatenate([lo, hi], axis=1).astype(BF16)
        gu = _dot(xb, wgu_bf_ref[...]) + bgu_ref[0]
        gate = jnp.minimum(gu[:, :d_ff], SWIGLU_LIMIT)
        up = jnp.clip(gu[:, d_ff:], -SWIGLU_LIMIT, SWIGLU_LIMIT)
        act = (up + 1.0) * gate * jax.nn.sigmoid(SWIGLU_ALPHA * gate)
        y_ref[0:rows, :] = _pack_bf16_pairs(_dot(act.astype(BF16), wd_bf_ref[...]) + bd_ref[0])

    in_use = i < nu_ref[0]

    for rows in range(EXPERT_ROW_STEP, n_rows + 1, EXPERT_ROW_STEP):
        @pl.when(in_use & (n_valid > rows - EXPERT_ROW_STEP) & (n_valid <= rows))
        def _(rows=rows):
            expert_rows(rows)
            if rows < n_rows:
                y_ref[rows:, :] = jnp.zeros((n_rows - rows, y_ref.shape[1]), y_ref.dtype)


def _expert_call(plan, xs, w_gu, b_gu, w_down, b_down):
    block_e, n_valid, first, nxt, slot, n_used = plan
    n_pad = xs.shape[0]
    n_exp, d, two_ff = w_gu.shape
    d_ff = two_ff // 2
    nblk = n_pad // EXPERT_ROWS
    rows = lambda i, be, nu, *_: (jnp.minimum(i, nu[0] - 1), 0)
    per_e = lambda i, be, *_: (be[i], 0, 0)
    grid_spec = pltpu.PrefetchScalarGridSpec(
        num_scalar_prefetch=6,
        grid=(nblk,),
        in_specs=[pl.BlockSpec((EXPERT_ROWS, d // 2), rows),
                  pl.BlockSpec(memory_space=pl.ANY), pl.BlockSpec((1, 1, two_ff), per_e),
                  pl.BlockSpec(memory_space=pl.ANY), pl.BlockSpec((1, 1, d), per_e)],
        out_specs=pl.BlockSpec((EXPERT_ROWS, d // 2), rows),
        scratch_shapes=[pltpu.VMEM((2, d, two_ff), F32), pltpu.VMEM((2, d_ff, d), F32),
                        pltpu.VMEM((d, two_ff), BF16), pltpu.VMEM((d_ff, d), BF16),
                        pltpu.SemaphoreType.DMA((2, 2))],
    )
    flat = lambda a: a.reshape(-1)
    return pl.pallas_call(
        functools.partial(_expert_kernel, d_ff=d_ff),
        grid_spec=grid_spec,
        out_shape=jax.ShapeDtypeStruct((n_pad, d // 2), U32),
        compiler_params=pltpu.CompilerParams(dimension_semantics=("arbitrary",),
                                             vmem_limit_bytes=EXPERT_VMEM_BYTES),
        name="experts",
    )(flat(block_e), n_used, flat(n_valid), flat(first), flat(nxt), flat(slot),
      xs, w_gu, b_gu.reshape(n_exp, 1, two_ff), w_down, b_down.reshape(n_exp, 1, d))


def _combine_kernel(y4_ref, wt_ref, h_ref, mod_ref, fw_ref, o_ref):
    wt = wt_ref[...]
    acc_lo, acc_hi = None, None
    for j in range(TOP_K):
        lo, hi = _unpack_bf16_pairs(y4_ref[j])
        w = wt[:, j:j + 1]
        acc_lo = w * lo if j == 0 else acc_lo + w * lo
        acc_hi = w * hi if j == 0 else acc_hi + w * hi
    acc = jnp.concatenate([acc_lo, acc_hi], axis=1)
    m = mod_ref[0]
    o_ref[...] = _rmsnorm(h_ref[...] + m[5:6] * acc, fw_ref[...])


def _combine_call(y4, wts, h, mod, fw, prev_out, *, n, tg, first_tile):
    d = h.shape[1]
    t_per_b = n // mod.shape[0]
    part = lambda width: pl.BlockSpec((tg, width), lambda i: (i, 0))
    row = lambda width: pl.BlockSpec((tg, width), lambda i: (i + first_tile, 0))
    in_specs = [pl.BlockSpec((TOP_K, tg, d // 2), lambda i: (0, i, 0)),
                part(LANES), part(d),
                pl.BlockSpec((1, N_MOD, d), lambda i: ((i + first_tile) * tg // t_per_b, 0, 0)),
                pl.BlockSpec((1, d), lambda i: (0, 0))]
    args = [y4, wts, h, mod, fw]
    kern = _combine_kernel
    aliases = {}
    if prev_out is not None:
        in_specs.append(pl.BlockSpec(memory_space=pl.ANY))
        args.append(prev_out)
        kern = lambda y4_ref, wt_ref, h_ref, mod_ref, fw_ref, prev_ref, o_ref: _combine_kernel(
            y4_ref, wt_ref, h_ref, mod_ref, fw_ref, o_ref)
        aliases = {len(args) - 1: 0}
    return pl.pallas_call(
        kern,
        grid=(y4.shape[1] // tg,),
        in_specs=in_specs,
        out_specs=row(d),
        out_shape=jax.ShapeDtypeStruct((n, d), F32),
        input_output_aliases=aliases,
        compiler_params=pltpu.CompilerParams(dimension_semantics=("arbitrary",)),
        name="combine",
    )(*args)


def kernel(x, c, ctx, c_ctx, w_ada, b_ada, norm_mix_w, norm_mlp_w, w_in, w_gk_f, b_gk_f, w_gk_b, b_gk_b,
           gla_norm_w, w_pool, pool_scale, w_out, w_router, b_router, w_gu, b_gu, w_down, b_down,
           final_norm_w):
    b, t, d = x.shape
    assert w_ada.shape[0] == 1, "single-layer trunk"
    n_exp = w_router.shape[2]
    rank = w_gk_f.shape[1]
    qk = w_gk_f.shape[2]
    dk = qk // GLA_HEADS
    gw = GLA_HEADS * gla_norm_w.shape[1]
    pw = w_pool.shape[1] * w_pool.shape[2]
    assert w_in.shape[2] == 2 * qk + 2 * gw + 2 * rank + pw and 2 * rank <= LANES
    assert t % SUPER == 0 and ctx.shape[1] % SUPER == 0 and n_exp <= LANES

    rows = -(-(b + 1) // 8) * 8
    cc = jnp.concatenate([c, c_ctx[None, :], jnp.zeros((rows - b - 1, d), F32)], axis=0)
    mod = _mod_call(cc, w_ada[0], b_ada)
    mod_x = mod[:b].reshape(b, N_MOD, d)
    mod_c = mod[b:b + 1].reshape(1, N_MOD, d)

    wi = w_in[0]
    o_r = 2 * qk + 2 * gw
    w_cat = jnp.concatenate([wi[:, :o_r], wi[:, o_r + 2 * rank:], wi[:, o_r:o_r + 2 * rank],
                             jnp.zeros((d, LANES - 2 * rank), F32)], axis=1).astype(BF16)
    wgk = jnp.concatenate([jnp.pad(w_gk_f[0], ((0, 0), (0, qk))), jnp.pad(w_gk_b[0], ((0, 0), (qk, 0))),
                           jnp.zeros((LANES - 2 * rank, 2 * qk), F32)], axis=0)
    bgk = jnp.concatenate([b_gk_f[0], b_gk_b[0]])[None, :]
    proj = functools.partial(_inproj_call, nw=norm_mix_w, w=w_cat, wgk=wgk.astype(BF16), bgk=bgk,
                             qk=qk, gw=gw, pw=pw, dk=dk)
    q, k, v, vt, g, xp, gk = proj(x, mod_x, tm=1024)
    _, kc, _, vtc, _, _, gkc = proj(ctx, mod_c, tm=SUPER)

    gla = _gla_call(q, k, v, vt, gk, g, kc, vtc, gkc, gla_norm_w)
    pool = _pool_call(xp, w_pool[0].astype(BF16), pool_scale)

    n = b * t
    wr = jnp.pad(w_router[0], ((0, 0), (0, LANES - n_exp)))
    br = jnp.pad(b_router, ((0, 0), (0, LANES - n_exp)))
    n_part = n // MOE_PARTS
    tm, tg = 512, 1024
    assert n % MOE_PARTS == 0 and n_part % (SC_WORKERS * SC_WINDOW) == 0 and n_part % tm == 0 and n_part < RANK_LIMIT
    assert (n_part * TOP_K) % (SC_WORKERS * 2 * SC_GATHER_WINDOW) == 0
    n_pad = n_part * TOP_K + n_exp * EXPERT_ROWS
    nblk = n_pad // EXPERT_ROWS
    assert nblk <= 8 * LANES
    parts = range(MOE_PARTS)
    routed = [_route_call(gla.reshape(n, gw), pool.reshape(n, pw), x.reshape(n, d), mod_x, w_out[0].astype(BF16),
                          norm_mlp_w, wr, br, tm=tm, n_exp=n_exp,
                          first_tile=p * n_part // tm, n_tiles=n_part // tm) for p in parts]
    plans = [_plan_call(cnt[0, :n_exp], code_t, n_exp=n_exp, rows=EXPERT_ROWS, chunk=min(n_part, 4096))
             for (_, _, code_t, _, cnt) in routed]
    dests = [plan[0][:TOP_K] for plan in plans]
    xs = [_sc_scatter_call(routed[p][1],
                           dests[p].reshape(TOP_K, n_part // SC_WINDOW, SC_WINDOW).transpose(1, 0, 2), n_out=n_pad)
          for p in parts]
    ys = [_expert_call(plans[p][1:], xs[p], w_gu[0], b_gu[0], w_down[0], b_down[0]) for p in parts]
    y4 = [_sc_gather_call(ys[p], dests[p].reshape(SC_WORKERS, -1, SC_GATHER_WINDOW)).reshape(TOP_K, n_part, d // 2)
          for p in parts]
    out = None
    for p in parts:
        out = _combine_call(y4[p], routed[p][3], routed[p][0], mod_x, final_norm_w[None, :], out,
                            n=n, tg=tg, first_tile=p * n_part // tg)
    return out.reshape(b, t, d)
```

```python
import functools

import numpy as np
import jax
import jax.numpy as jnp
from jax import lax
from jax.experimental import pallas as pl
from jax.experimental.pallas import tpu as pltpu
from jax.experimental.pallas import tpu_sc as plsc

F32 = jnp.float32
BF16 = jnp.bfloat16
I32 = jnp.int32
U32 = jnp.uint32

GRID_W = 64
GLA_HEADS = 4
GLA_CHUNK = 64
GATE_NORMALIZER = 16.0
POOL_WINDOWS = (2, 4, 8, 16)
POOL_PAD_GRID_ROWS = 8
TOP_K = 4
RANK_LIMIT = 1 << 20
SWIGLU_LIMIT = 7.0
SWIGLU_ALPHA = 1.702
N_MOD = 6
EPS = 1e-6

LANES = 128
SUPER = 256
HEAD_PAIR_DK = 128
EXPERT_ROWS = 512
EXPERT_ROW_STEP = 128
MOE_PARTS = 2
EXPERT_VMEM_BYTES = 56 * 1024 * 1024
SC_CORES = 2
SC_SUBCORES = 16
SC_WORKERS = SC_CORES * SC_SUBCORES
SC_WINDOW = 32
SC_GATHER_WINDOW = 64


def _dot(a, b):
    return jnp.dot(a, b, preferred_element_type=F32)


def _dot_nt(a, b):
    return lax.dot_general(a, b, (((1,), (1,)), ((), ())), preferred_element_type=F32)


def _split_bf16(x):
    hi = x.astype(BF16)
    lo = (x - hi.astype(F32)).astype(BF16)
    return hi, lo


def _pack_bf16_pairs(x):
    c = x.shape[1] // 2
    lo = lax.bitcast_convert_type(x[:, :c].astype(BF16).astype(F32), U32)
    hi = lax.bitcast_convert_type(x[:, c:].astype(BF16).astype(F32), U32)
    return (lo >> 16) | hi


def _unpack_bf16_pairs(p):
    lo = lax.bitcast_convert_type(p << 16, F32)
    hi = lax.bitcast_convert_type(p & jnp.uint32(0xFFFF0000), F32)
    return lo, hi


def _rmsnorm(x, w):
    var = jnp.mean(x * x, axis=-1, keepdims=True)
    return x * lax.rsqrt(var + EPS) * w


def _mod_kernel(c_ref, w_ref, b_ref, o_ref):
    c = c_ref[...]
    s = c * jax.nn.sigmoid(c)
    o_ref[...] = jnp.dot(s, w_ref[...], precision=lax.Precision.HIGHEST,
                         preferred_element_type=F32) + b_ref[...]


def _mod_call(cc, w_ada, b_ada):
    rows, d = cc.shape
    n = w_ada.shape[1]
    tn = 2048
    return pl.pallas_call(
        _mod_kernel,
        grid=(n // tn,),
        in_specs=[pl.BlockSpec((rows, d), lambda j: (0, 0)),
                  pl.BlockSpec((d, tn), lambda j: (0, j)),
                  pl.BlockSpec((1, tn), lambda j: (0, j))],
        out_specs=pl.BlockSpec((rows, tn), lambda j: (0, j)),
        out_shape=jax.ShapeDtypeStruct((rows, n), F32),
        name="mod",
    )(cc, w_ada, b_ada)


def _inproj_kernel(x_ref, mod_ref, nw_ref, w_ref, wgk_ref, bgk_ref,
                   q_ref, k_ref, v_ref, vt_ref, g_ref, p_ref, gk_ref, *, qk, gw, pw, dk):
    x = x_ref[0]
    m = mod_ref[0]
    hm = (_rmsnorm(x, nw_ref[...]) * (1.0 + m[1:2]) + m[0:1]).astype(BF16)
    p = _dot(hm, w_ref[...])
    vt = p[:, 2 * qk:2 * qk + gw].T
    for s in range(vt_ref.shape[1]):
        vt_ref[0, s] = vt[:, s * SUPER:(s + 1) * SUPER].astype(BF16)
    o = 0
    q_ref[0] = (p[:, o:o + qk] * (dk ** -0.5)).astype(BF16); o += qk
    k_ref[0] = p[:, o:o + qk].astype(BF16); o += qk
    v_ref[0] = p[:, o:o + gw].astype(BF16); o += gw
    g_ref[0] = p[:, o:o + gw].astype(BF16); o += gw
    p_ref[0] = p[:, o:o + pw].astype(BF16); o += pw
    r = p[:, o:o + LANES]
    z = _dot(r.astype(BF16), wgk_ref[...]) + bgk_ref[...]
    gk_ref[0] = (jnp.minimum(z, 0.0) - jnp.log1p(jnp.exp(-jnp.abs(z)))) * (1.0 / GATE_NORMALIZER)


def _inproj_call(x, mod, nw, w, wgk, bgk, *, qk, gw, pw, dk, tm):
    b, t, d = x.shape
    n_in = w.shape[1]
    bs = lambda width: pl.BlockSpec((1, tm, width), lambda i, j: (i, j, 0))
    const = lambda shape: pl.BlockSpec(shape, lambda i, j: (0,) * len(shape))
    per_batch = mod.shape[0] > 1
    sds = jax.ShapeDtypeStruct
    return pl.pallas_call(
        functools.partial(_inproj_kernel, qk=qk, gw=gw, pw=pw, dk=dk),
        grid=(b, t // tm),
        in_specs=[bs(d),
                  pl.BlockSpec((1, N_MOD, d), (lambda i, j: (i, 0, 0)) if per_batch else (lambda i, j: (0, 0, 0))),
                  const((1, d)), const((d, n_in)), const((LANES, 2 * qk)), const((1, 2 * qk))],
        out_specs=[bs(qk), bs(qk), bs(gw),
                   pl.BlockSpec((1, tm // SUPER, gw, SUPER), lambda i, j: (i, j, 0, 0)),
                   bs(gw), bs(pw), bs(2 * qk)],
        out_shape=[sds((b, t, qk), BF16), sds((b, t, qk), BF16), sds((b, t, gw), BF16),
                   sds((b, t // SUPER, gw, SUPER), BF16),
                   sds((b, t, gw), BF16), sds((b, t, pw), BF16), sds((b, t, 2 * qk), F32)],
        compiler_params=pltpu.CompilerParams(dimension_semantics=("arbitrary", "arbitrary")),
        name="inproj",
    )(x, mod, nw, w, wgk, bgk)


def _gla_super(q, k, v, vt, gk, cm, amask, bd_mask, st, fwd, want_out):
    nch = SUPER // GLA_CHUNK
    order = tuple(range(nch)) if fwd else tuple(reversed(range(nch)))
    last_row = GLA_CHUNK - 1 if fwd else 0
    mid_row = GLA_CHUNK // 2 - 1 if fwd else GLA_CHUNK // 2
    hi, lo = _split_bf16(gk)
    bcum = _dot(cm, hi) + _dot(cm, lo)

    def chunk_row(r):
        return jnp.concatenate(
            [jnp.broadcast_to(bcum[c * GLA_CHUNK + r:c * GLA_CHUNK + r + 1, :], (GLA_CHUNK, bcum.shape[1]))
             for c in range(nch)], axis=0)

    chunk_of_row = lax.broadcasted_iota(I32, bcum.shape, 0) // GLA_CHUNK

    def by_chunk(x):
        return jnp.concatenate([jnp.where(chunk_of_row == c, x, 0.0).astype(BF16) for c in range(nch)], axis=1)

    blast = chunk_row(last_row)
    u_all = _dot(vt, by_chunk(k * jnp.exp(blast - bcum)))
    before = [None] * nch
    for c in order:
        before[c] = st
        decay = jnp.exp(bcum[c * GLA_CHUNK + last_row:c * GLA_CHUNK + last_row + 1, :])
        st = st * decay + jnp.where(bd_mask, u_all[:, c * HEAD_PAIR_DK:(c + 1) * HEAD_PAIR_DK], 0.0)
    if not want_out:
        return None, st
    bmid = chunk_row(mid_row)
    qt = q * jnp.exp(bcum - bmid)
    kt = (k * jnp.exp(bmid - bcum)).astype(BF16)
    lane = lax.broadcasted_iota(I32, qt.shape, 1)
    half = HEAD_PAIR_DK // 2
    o_heads = []
    for hh in range(2):
        sel = (lane < half) if hh == 0 else (lane >= half)
        a = _dot_nt(jnp.where(sel, qt, 0.0).astype(BF16), kt)
        a = jnp.where(amask, a, 0.0).astype(BF16)
        o_heads.append(_dot(a, v[:, hh * LANES:(hh + 1) * LANES]))
    qh = (q * jnp.exp(bcum)).astype(BF16)
    o_inter = jnp.concatenate(
        [_dot_nt(qh[c * GLA_CHUNK:(c + 1) * GLA_CHUNK], before[c].astype(BF16)) for c in range(nch)], axis=0)
    return jnp.concatenate(o_heads, axis=1) + o_inter, st


def _gla_kernel(q_ref, k_ref, v_ref, vt_ref, gkf_ref, gkb_ref, g_ref, kc_ref, vtc_ref, gkfc_ref, gkbc_ref,
                nw_ref, cmf_ref, cmb_ref, o_ref, stf_ref, stb_ref, of_ref, ob_ref):
    t = q_ref.shape[1]
    tc = kc_ref.shape[1]
    nsc, nscc = t // SUPER, tc // SUPER
    cmf = cmf_ref[...]
    cmb = cmb_ref[...]
    amask_f = cmf > 0
    amask_b = cmb > 0
    row = lax.broadcasted_iota(I32, (2 * LANES, HEAD_PAIR_DK), 0)
    lane = lax.broadcasted_iota(I32, (2 * LANES, HEAD_PAIR_DK), 1)
    bd_mask = (row < LANES) == (lane < HEAD_PAIR_DK // 2)

    def ctx_state(gk_ref, cm, fwd, j, st):
        rows = pl.ds(j * SUPER, SUPER)
        return _gla_super(None, kc_ref[0, rows, :].astype(F32), None, vtc_ref[0, j], gk_ref[0, rows, :],
                          cm, None, bd_mask, st, fwd, False)[1]

    def latent(gk_ref, cm, amask, fwd, j, st):
        rows = pl.ds(pl.multiple_of(j * SUPER, SUPER), SUPER)
        return _gla_super(q_ref[0, rows, :].astype(F32), k_ref[0, rows, :].astype(F32), v_ref[0, rows, :],
                          vt_ref[0, j], gk_ref[0, rows, :], cm, amask, bd_mask, st, fwd, True)

    st = jnp.zeros(stf_ref.shape, F32)
    for j in range(nscc):
        st = ctx_state(gkfc_ref, cmf, True, j, st)
    stf_ref[...] = st
    st = jnp.zeros(stb_ref.shape, F32)
    for j in reversed(range(nscc)):
        st = ctx_state(gkbc_ref, cmb, False, j, st)
    stb_ref[...] = st

    def scan_body(jj, carry):
        jb = nsc - 1 - jj
        of, stf = latent(gkf_ref, cmf, amask_f, True, jj, stf_ref[...])
        of_ref[pl.ds(pl.multiple_of(jj * SUPER, SUPER), SUPER), :] = of
        stf_ref[...] = stf
        ob, stb = latent(gkb_ref, cmb, amask_b, False, jb, stb_ref[...])
        ob_ref[pl.ds(pl.multiple_of(jb * SUPER, SUPER), SUPER), :] = ob
        stb_ref[...] = stb
        return carry

    lax.fori_loop(0, nsc, scan_body, 0, unroll=2)

    nw = nw_ref[...]

    def out_body(j, carry):
        rows = pl.ds(pl.multiple_of(j * SUPER, SUPER), SUPER)
        o = of_ref[rows, :] + ob_ref[rows, :]
        g = g_ref[0, rows, :].astype(F32)
        gate = g * jax.nn.sigmoid(g)
        for hh in range(2):
            oh = o[:, hh * LANES:(hh + 1) * LANES]
            on = oh * lax.rsqrt(jnp.mean(oh * oh, axis=-1, keepdims=True) + EPS) * nw
            o_ref[0, rows, hh * LANES:(hh + 1) * LANES] = (on * gate[:, hh * LANES:(hh + 1) * LANES]).astype(BF16)
        return carry

    lax.fori_loop(0, nsc, out_body, 0)


def _gla_masks():
    i = np.arange(SUPER)
    same = (i[:, None] // GLA_CHUNK) == (i[None, :] // GLA_CHUNK)
    fwd = same & (i[None, :] <= i[:, None])
    bwd = same & (i[None, :] >= i[:, None])
    return jnp.asarray(fwd, BF16), jnp.asarray(bwd, BF16)


def _gla_call(q, k, v, vt, gk, g, kc, vtc, gkc, nw):
    b, t, qk = q.shape
    tc = kc.shape[1]
    npair = qk // HEAD_PAIR_DK
    cmf, cmb = _gla_masks()
    lat = lambda width, off: pl.BlockSpec((1, t, width), lambda i, j: (i, 0, j + off))
    ctx = lambda width, off: pl.BlockSpec((1, tc, width), lambda i, j: (i, 0, j + off))
    tr = lambda n_groups: pl.BlockSpec((1, n_groups, 2 * LANES, SUPER), lambda i, j: (i, 0, j, 0))
    const = lambda shape: pl.BlockSpec(shape, lambda i, j: (0,) * len(shape))
    return pl.pallas_call(
        _gla_kernel,
        grid=(b, npair),
        in_specs=[lat(HEAD_PAIR_DK, 0), lat(HEAD_PAIR_DK, 0), lat(2 * LANES, 0), tr(t // SUPER),
                  lat(HEAD_PAIR_DK, 0), lat(HEAD_PAIR_DK, npair), lat(2 * LANES, 0),
                  ctx(HEAD_PAIR_DK, 0), tr(tc // SUPER), ctx(HEAD_PAIR_DK, 0), ctx(HEAD_PAIR_DK, npair),
                  const((1, LANES)), const(cmf.shape), const(cmb.shape)],
        out_specs=lat(2 * LANES, 0),
        out_shape=jax.ShapeDtypeStruct((b, t, v.shape[2]), BF16),
        scratch_shapes=[pltpu.VMEM((2 * LANES, HEAD_PAIR_DK), F32), pltpu.VMEM((2 * LANES, HEAD_PAIR_DK), F32),
                        pltpu.VMEM((t, 2 * LANES), F32), pltpu.VMEM((t, 2 * LANES), F32)],
        compiler_params=pltpu.CompilerParams(dimension_semantics=("arbitrary", "arbitrary")),
        name="gla",
    )(q, k, v, vt, gk, gk, g, kc, vtc, gkc, gkc, nw, cmf, cmb)


def _pool_kernel(x_ref, cm_ref, cnt_ref, wp_ref, ps_ref, o_ref, a_ref, b_ref):
    t = x_ref.shape[1]
    pad = POOL_PAD_GRID_ROWS * GRID_W
    total = t + 2 * pad
    for gi, w in enumerate(POOL_WINDOWS):
        lo = w // 2
        cols = slice(gi * LANES, (gi + 1) * LANES)
        cmat = cm_ref[gi]
        a_ref[0:pad, :] = jnp.zeros((pad, LANES), F32)
        a_ref[pad + t:total, :] = jnp.zeros((pad, LANES), F32)
        for blk in range(t // SUPER):
            rs = slice(blk * SUPER, (blk + 1) * SUPER)
            a_ref[pad + blk * SUPER:pad + (blk + 1) * SUPER, :] = _dot(cmat, x_ref[0, rs, cols])
        src, dst = a_ref, b_ref
        m = 1
        while m < w:
            sh = m * GRID_W
            dst[0:total - sh, :] = src[0:total - sh, :] + src[sh:total, :]
            src, dst = dst, src
            m *= 2
        first = pad - lo * GRID_W
        pooled = src[first:first + t, :] / cnt_ref[gi] - x_ref[0, :, cols].astype(F32)
        yp = _dot(pooled.astype(BF16), wp_ref[gi]) * ps_ref[:, cols]
        o_ref[0, :, cols] = yp.astype(BF16)


def _pool_col_mats():
    i = np.arange(SUPER)
    same_row = (i[:, None] // GRID_W) == (i[None, :] // GRID_W)
    d = i[None, :] - i[:, None]
    mats = []
    for w in POOL_WINDOWS:
        lo = w // 2
        hi = w - 1 - lo
        mats.append(same_row & (d >= -lo) & (d <= hi))
    return jnp.asarray(np.stack(mats), BF16)


def _pool_counts(t):
    rows = t // GRID_W
    r = np.arange(t) // GRID_W
    c = np.arange(t) % GRID_W
    out = []
    for w in POOL_WINDOWS:
        lo = w // 2
        hi = w - 1 - lo
        cnt_r = np.minimum(r + hi + 1, rows) - np.maximum(r - lo, 0)
        cnt_c = np.minimum(c + hi + 1, GRID_W) - np.maximum(c - lo, 0)
        out.append(np.broadcast_to((cnt_r * cnt_c).astype(np.float32)[:, None], (t, LANES)))
    return jnp.asarray(np.stack(out))


def _pool_call(xp, w_pool, pool_scale):
    b, t, pw = xp.shape
    ng = len(POOL_WINDOWS)
    assert max(POOL_WINDOWS) // 2 <= POOL_PAD_GRID_ROWS and t % GRID_W == 0
    cm = _pool_col_mats()
    cnt = _pool_counts(t)
    staged = t + 2 * POOL_PAD_GRID_ROWS * GRID_W
    const = lambda shape: pl.BlockSpec(shape, lambda i: (0,) * len(shape))
    return pl.pallas_call(
        _pool_kernel,
        grid=(b,),
        in_specs=[pl.BlockSpec((1, t, pw), lambda i: (i, 0, 0)),
                  const(cm.shape), const(cnt.shape), const((ng, LANES, LANES)), const((1, pw))],
        out_specs=pl.BlockSpec((1, t, pw), lambda i: (i, 0, 0)),
        out_shape=jax.ShapeDtypeStruct((b, t, pw), BF16),
        scratch_shapes=[pltpu.VMEM((staged, LANES), F32), pltpu.VMEM((staged, LANES), F32)],
        compiler_params=pltpu.CompilerParams(dimension_semantics=("arbitrary",)),
        name="pool",
    )(xp, cm, cnt, w_pool, pool_scale)


def _route_kernel(gla_ref, pool_ref, x_ref, mod_ref, wo_ref, nw_ref, wr_ref, br_ref, lt_ref,
                  h_ref, xt_ref, code_ref, wt_ref, cnt_ref, run_ref, wr2_ref, *, gw, n_exp):
    i = pl.program_id(0)

    @pl.when(i == 0)
    def _():
        run_ref[...] = jnp.zeros_like(run_ref)
        wh, wl = _split_bf16(wr_ref[...])
        wr2_ref[:, :LANES] = wh
        wr2_ref[:, LANES:] = wl

    m = mod_ref[0]
    acc = _dot(gla_ref[...], wo_ref[0:gw, :]) + _dot(pool_ref[...], wo_ref[gw:, :])
    h = x_ref[...] + m[2:3] * acc
    h_ref[...] = h
    xt = _rmsnorm(h, nw_ref[...]) * (1.0 + m[4:5]) + m[3:4]
    xt_ref[...] = _pack_bf16_pairs(xt)
    xh, xl = _split_bf16(xt)
    wr2 = wr2_ref[...]
    t1 = _dot(xh, wr2)
    logits = t1[:, :LANES] + t1[:, LANES:] + _dot(xl, wr2[:, :LANES]) + br_ref[...]
    lane = lax.broadcasted_iota(I32, logits.shape, 1)
    neg = jnp.float32(-jnp.inf)
    logits = jnp.where(lane < n_exp, logits, neg)
    vals, hots = [], []
    e_out = jnp.zeros(logits.shape, I32)
    lane_f = lane.astype(F32)
    for j in range(TOP_K):
        mx = jnp.max(logits, axis=-1, keepdims=True)
        idx = jnp.min(jnp.where(logits == mx, lane_f, float(LANES)), axis=-1, keepdims=True)
        hot = lane_f == idx
        vals.append(mx)
        hots.append(hot)
        e_out = jnp.where(lane == j, idx.astype(I32), e_out)
        logits = jnp.where(hot, neg, logits)
    ex = [jnp.exp(v - vals[0]) for v in vals]
    den = ex[0] + ex[1] + ex[2] + ex[3]
    w_out = jnp.zeros(logits.shape, F32)
    for j in range(TOP_K):
        w_out = jnp.where(lane == j, ex[j] / den, w_out)
    osum = jnp.where(hots[0] | hots[1] | hots[2] | hots[3], 1.0, 0.0)
    before = _dot(lt_ref[...], osum.astype(BF16)) + run_ref[0:1, :]
    rk_out = jnp.zeros(logits.shape, I32)
    for j in range(TOP_K):
        rj = jnp.sum(jnp.where(hots[j], before, 0.0), axis=-1, keepdims=True)
        rk_out = jnp.where(lane == j, rj.astype(I32), rk_out)
    run = run_ref[0:1, :] + jnp.sum(osum, axis=0, keepdims=True)
    run_ref[...] = jnp.broadcast_to(run, run_ref.shape)
    code = e_out * RANK_LIMIT + rk_out
    code_ref[...] = code.T[:code_ref.shape[0], :]
    wt_ref[...] = w_out
    cnt_ref[...] = jnp.broadcast_to(run, cnt_ref.shape).astype(I32)


def _route_call(gla, pool, x, mod, w_out, nw, wr, br, *, tm, n_exp, first_tile, n_tiles):
    n_all, d = x.shape
    n = n_tiles * tm
    gw = gla.shape[1]
    t_per_b = n_all // mod.shape[0]
    lt = jnp.asarray(np.tril(np.ones((tm, tm), np.float32), -1), BF16)
    row_in = lambda width: pl.BlockSpec((tm, width), lambda i: (i + first_tile, 0))
    row = lambda width: pl.BlockSpec((tm, width), lambda i: (i, 0))
    const = lambda shape: pl.BlockSpec(shape, lambda i: (0,) * len(shape))
    sds = jax.ShapeDtypeStruct
    return pl.pallas_call(
        functools.partial(_route_kernel, gw=gw, n_exp=n_exp),
        grid=(n_tiles,),
        in_specs=[row_in(gw), row_in(pool.shape[1]), row_in(d),
                  pl.BlockSpec((1, N_MOD, d), lambda i: ((i + first_tile) * tm // t_per_b, 0, 0)),
                  const(w_out.shape), const((1, d)), const(wr.shape), const((1, LANES)), const((tm, tm))],
        out_specs=[row(d), row(d // 2), pl.BlockSpec((8, tm), lambda i: (0, i)), row(LANES), const((8, LANES))],
        out_shape=[sds((n, d), F32), sds((n, d // 2), U32), sds((8, n), I32),
                   sds((n, LANES), F32), sds((8, LANES), I32)],
        scratch_shapes=[pltpu.VMEM((8, LANES), F32), pltpu.VMEM((d, 2 * LANES), BF16)],
        compiler_params=pltpu.CompilerParams(dimension_semantics=("arbitrary",)),
        name="route",
    )(gla, pool, x, mod, w_out, nw, wr, br, lt)


def _plan_kernel(cnt_ref, code_ref, dest_ref, be_ref, nv_ref, first_ref, next_ref, slot_ref, nu_ref, start_ref,
                 *, n_exp, rows):
    @pl.when(pl.program_id(0) == 0)
    def _():
        blk = (lax.broadcasted_iota(I32, be_ref.shape, 0) * LANES + lax.broadcasted_iota(I32, be_ref.shape, 1))
        blk_row0 = blk * rows
        nxt_e = [None] * n_exp
        nxt = jnp.int32(-1)
        for e in reversed(range(n_exp)):
            nxt_e[e] = nxt
            nxt = jnp.where(cnt_ref[e] > 0, e, nxt)
        zeros = jnp.zeros(be_ref.shape, I32)
        be, end_valid, first, nxt_blk, slot = zeros, zeros, zeros, zeros - 1, zeros
        acc = jnp.int32(0)
        last_e = jnp.int32(0)
        ordinal = jnp.int32(0)
        for e in range(n_exp):
            c = cnt_ref[e]
            start_ref[e] = acc
            in_e = (blk_row0 >= acc) & (c > 0)
            end_valid = jnp.where(in_e, acc + c, end_valid)
            be = jnp.where(in_e, e, be)
            first = jnp.where(in_e, (blk_row0 == acc).astype(I32), first)
            nxt_blk = jnp.where(in_e, nxt_e[e], nxt_blk)
            slot = jnp.where(in_e, ordinal % 2, slot)
            acc = acc + (c + rows - 1) // rows * rows
            last_e = jnp.where(c > 0, e, last_e)
            ordinal = ordinal + (c > 0).astype(I32)
        n_used = acc // rows
        nu_ref[0] = n_used
        be_ref[...] = jnp.where(blk < n_used, be, last_e)
        nv_ref[...] = jnp.clip(end_valid - blk_row0, 0, rows)
        first_ref[...] = first
        next_ref[...] = nxt_blk
        slot_ref[...] = slot

    code = code_ref[...]
    e_vec = code // RANK_LIMIT
    dest = code % RANK_LIMIT
    for e in range(n_exp):
        dest = dest + jnp.where(e_vec == e, start_ref[e], 0)
    dest_ref[...] = dest


def _plan_call(counts, code_t, *, n_exp, rows, chunk):
    n = code_t.shape[1]
    sds = jax.ShapeDtypeStruct
    smem = pltpu.SMEM
    return pl.pallas_call(
        functools.partial(_plan_kernel, n_exp=n_exp, rows=rows),
        grid=(n // chunk,),
        in_specs=[pl.BlockSpec(memory_space=smem), pl.BlockSpec((8, chunk), lambda i: (0, i))],
        out_specs=[pl.BlockSpec((8, chunk), lambda i: (0, i))]
        + [pl.BlockSpec((8, LANES), lambda i: (0, 0))] * 5 + [pl.BlockSpec(memory_space=smem)],
        out_shape=[sds((8, n), I32)] + [sds((8, LANES), I32)] * 5 + [sds((1,), I32)],
        scratch_shapes=[pltpu.SMEM((n_exp,), I32)],
        compiler_params=pltpu.CompilerParams(dimension_semantics=("arbitrary",)),
        name="plan",
    )(counts, code_t)


def _sc_worker_id():
    return lax.axis_index("s") * SC_CORES + lax.axis_index("c")


def _sc_scatter_call(x, idx3, *, n_out):
    n, d = x.shape
    n_win_total, k, w = idx3.shape
    n_win = n_win_total // SC_WORKERS
    mesh = plsc.VectorSubcoreMesh(core_axis_name="c", subcore_axis_name="s")

    @functools.partial(
        pl.kernel, mesh=mesh,
        out_type=jax.ShapeDtypeStruct((n_out, d), x.dtype),
        scratch_types=[pltpu.VMEM((k, w), I32), pltpu.VMEM((w, d), x.dtype), pltpu.SemaphoreType.DMA],
        name="sc_dispatch",
    )
    def kern(x_hbm, idx_hbm, out_hbm, idx_v, rows_v, sem):
        wid = _sc_worker_id()

        @pl.loop(0, n_win)
        def _(i):
            win = wid * n_win + i
            pltpu.sync_copy(idx_hbm.at[win], idx_v)
            pltpu.sync_copy(x_hbm.at[pl.ds(win * w, w)], rows_v)
            for j in range(k):
                pltpu.async_copy(rows_v, out_hbm.at[idx_v.at[j]], sem).wait()

    return kern(x, idx3)


def _sc_gather_call(table, idx3):
    n_workers, n_win, w = idx3.shape
    d = table.shape[1]
    assert n_workers == SC_WORKERS and n_win % 2 == 0
    mesh = plsc.VectorSubcoreMesh(core_axis_name="c", subcore_axis_name="s")

    @functools.partial(
        pl.kernel, mesh=mesh,
        out_type=jax.ShapeDtypeStruct((n_workers * n_win * w, d), table.dtype),
        scratch_types=[pltpu.VMEM((n_win, w), I32), pltpu.VMEM((2, w, d), table.dtype),
                       pltpu.SemaphoreType.DMA((2,)), pltpu.SemaphoreType.DMA((2,))],
        name="sc_gather",
    )
    def kern(table_hbm, idx_hbm, out_hbm, idx_v, rows_v, gsem, osem):
        wid = _sc_worker_id()
        base = wid * n_win
        pltpu.sync_copy(idx_hbm.at[wid], idx_v)

        def gather(wi, b):
            return pltpu.make_async_copy(table_hbm.at[idx_v.at[wi]], rows_v.at[b], gsem.at[b])

        def put(wi, b):
            return pltpu.make_async_copy(rows_v.at[b], out_hbm.at[pl.ds((base + wi) * w, w)], osem.at[b])

        gather(0, 0).start()

        @pl.loop(0, n_win, step=2)
        def _(i):
            for b in range(2):
                wi = i + b

                @pl.when(wi + 1 < n_win)
                def _():
                    @pl.when(wi >= 1)
                    def _():
                        put(wi - 1, 1 - b).wait()
                    gather(wi + 1, 1 - b).start()

                gather(wi, b).wait()
                put(wi, b).start()

        put(n_win - 2, 0).wait()
        put(n_win - 1, 1).wait()

    return kern(table, idx3)


def _expert_kernel(be_ref, nu_ref, nv_ref, first_ref, next_ref, slot_ref,
                   x_ref, wgu_hbm, bgu_ref, wd_hbm, bd_ref, y_ref,
                   wgu_f32_ref, wd_f32_ref, wgu_bf_ref, wd_bf_ref, sem, *, d_ff):
    i = pl.program_id(0)
    slot = slot_ref[i]

    def fetch(e, s):
        return (pltpu.make_async_copy(wgu_hbm.at[e], wgu_f32_ref.at[s], sem.at[0, s]),
                pltpu.make_async_copy(wd_hbm.at[e], wd_f32_ref.at[s], sem.at[1, s]))

    @pl.when(i == 0)
    def _():
        for cp in fetch(be_ref[0], slot):
            cp.start()

    @pl.when(first_ref[i] == 1)
    def _():
        for cp in fetch(be_ref[i], slot):
            cp.wait()

        @pl.when(next_ref[i] >= 0)
        def _():
            for cp in fetch(next_ref[i], 1 - slot):
                cp.start()

        wgu_bf_ref[...] = wgu_f32_ref[slot].astype(BF16)
        wd_bf_ref[...] = wd_f32_ref[slot].astype(BF16)

    n_valid = nv_ref[i]
    n_rows = x_ref.shape[0]

    def expert_rows(rows):
        row = lax.broadcasted_iota(I32, (rows, x_ref.shape[1]), 0)
        lo, hi = _unpack_bf16_pairs(jnp.where(row < n_valid, x_ref[0:rows, :], jnp.uint32(0)))
        xb = jnp.concatenate([lo, hi], axis=1).astype(BF16)
        gu = _dot(xb, wgu_bf_ref[...]) + bgu_ref[0]
        gate = jnp.minimum(gu[:, :d_ff], SWIGLU_LIMIT)
        up = jnp.clip(gu[:, d_ff:], -SWIGLU_LIMIT, SWIGLU_LIMIT)
        act = (up + 1.0) * gate * jax.nn.sigmoid(SWIGLU_ALPHA * gate)
        y_ref[0:rows, :] = _pack_bf16_pairs(_dot(act.astype(BF16), wd_bf_ref[...]) + bd_ref[0])

    in_use = i < nu_ref[0]

    for rows in range(EXPERT_ROW_STEP, n_rows + 1, EXPERT_ROW_STEP):
        @pl.when(in_use & (n_valid > rows - EXPERT_ROW_STEP) & (n_valid <= rows))
        def _(rows=rows):
            expert_rows(rows)
            if rows < n_rows:
                y_ref[rows:, :] = jnp.zeros((n_rows - rows, y_ref.shape[1]), y_ref.dtype)


def _expert_call(plan, xs, w_gu, b_gu, w_down, b_down):
    block_e, n_valid, first, nxt, slot, n_used = plan
    n_pad = xs.shape[0]
    n_exp, d, two_ff = w_gu.shape
    d_ff = two_ff // 2
    nblk = n_pad // EXPERT_ROWS
    rows = lambda i, be, nu, *_: (jnp.minimum(i, nu[0] - 1), 0)
    per_e = lambda i, be, *_: (be[i], 0, 0)
    grid_spec = pltpu.PrefetchScalarGridSpec(
        num_scalar_prefetch=6,
        grid=(nblk,),
        in_specs=[pl.BlockSpec((EXPERT_ROWS, d // 2), rows),
                  pl.BlockSpec(memory_space=pl.ANY), pl.BlockSpec((1, 1, two_ff), per_e),
                  pl.BlockSpec(memory_space=pl.ANY), pl.BlockSpec((1, 1, d), per_e)],
        out_specs=pl.BlockSpec((EXPERT_ROWS, d // 2), rows),
        scratch_shapes=[pltpu.VMEM((2, d, two_ff), F32), pltpu.VMEM((2, d_ff, d), F32),
                        pltpu.VMEM((d, two_ff), BF16), pltpu.VMEM((d_ff, d), BF16),
                        pltpu.SemaphoreType.DMA((2, 2))],
    )
    flat = lambda a: a.reshape(-1)
    return pl.pallas_call(
        functools.partial(_expert_kernel, d_ff=d_ff),
        grid_spec=grid_spec,
        out_shape=jax.ShapeDtypeStruct((n_pad, d // 2), U32),
        compiler_params=pltpu.CompilerParams(dimension_semantics=("arbitrary",),
                                             vmem_limit_bytes=EXPERT_VMEM_BYTES),
        name="experts",
    )(flat(block_e), n_used, flat(n_valid), flat(first), flat(nxt), flat(slot),
      xs, w_gu, b_gu.reshape(n_exp, 1, two_ff), w_down, b_down.reshape(n_exp, 1, d))


def _combine_kernel(y4_ref, wt_ref, h_ref, mod_ref, fw_ref, o_ref):
    wt = wt_ref[...]
    acc_lo, acc_hi = None, None
    for j in range(TOP_K):
        lo, hi = _unpack_bf16_pairs(y4_ref[j])
        w = wt[:, j:j + 1]
        acc_lo = w * lo if j == 0 else acc_lo + w * lo
        acc_hi = w * hi if j == 0 else acc_hi + w * hi
    acc = jnp.concatenate([acc_lo, acc_hi], axis=1)
    m = mod_ref[0]
    o_ref[...] = _rmsnorm(h_ref[...] + m[5:6] * acc, fw_ref[...])


def _combine_call(y4, wts, h, mod, fw, prev_out, *, n, tg, first_tile):
    d = h.shape[1]
    t_per_b = n // mod.shape[0]
    part = lambda width: pl.BlockSpec((tg, width), lambda i: (i, 0))
    row = lambda width: pl.BlockSpec((tg, width), lambda i: (i + first_tile, 0))
    in_specs = [pl.BlockSpec((TOP_K, tg, d // 2), lambda i: (0, i, 0)),
                part(LANES), part(d),
                pl.BlockSpec((1, N_MOD, d), lambda i: ((i + first_tile) * tg // t_per_b, 0, 0)),
                pl.BlockSpec((1, d), lambda i: (0, 0))]
    args = [y4, wts, h, mod, fw]
    kern = _combine_kernel
    aliases = {}
    if prev_out is not None:
        in_specs.append(pl.BlockSpec(memory_space=pl.ANY))
        args.append(prev_out)
        kern = lambda y4_ref, wt_ref, h_ref, mod_ref, fw_ref, prev_ref, o_ref: _combine_kernel(
            y4_ref, wt_ref, h_ref, mod_ref, fw_ref, o_ref)
        aliases = {len(args) - 1: 0}
    return pl.pallas_call(
        kern,
        grid=(y4.shape[1] // tg,),
        in_specs=in_specs,
        out_specs=row(d),
        out_shape=jax.ShapeDtypeStruct((n, d), F32),
        input_output_aliases=aliases,
        compiler_params=pltpu.CompilerParams(dimension_semantics=("arbitrary",)),
        name="combine",
    )(*args)


def kernel(x, c, ctx, c_ctx, w_ada, b_ada, norm_mix_w, norm_mlp_w, w_in, w_gk_f, b_gk_f, w_gk_b, b_gk_b,
           gla_norm_w, w_pool, pool_scale, w_out, w_router, b_router, w_gu, b_gu, w_down, b_down,
           final_norm_w):
    b, t, d = x.shape
    assert w_ada.shape[0] == 1, "single-layer trunk"
    n_exp = w_router.shape[2]
    rank = w_gk_f.shape[1]
    qk = w_gk_f.shape[2]
    dk = qk // GLA_HEADS
    gw = GLA_HEADS * gla_norm_w.shape[1]
    pw = w_pool.shape[1] * w_pool.shape[2]
    assert w_in.shape[2] == 2 * qk + 2 * gw + 2 * rank + pw and 2 * rank <= LANES
    assert t % SUPER == 0 and ctx.shape[1] % SUPER == 0 and n_exp <= LANES

    rows = -(-(b + 1) // 8) * 8
    cc = jnp.concatenate([c, c_ctx[None, :], jnp.zeros((rows - b - 1, d), F32)], axis=0)
    mod = _mod_call(cc, w_ada[0], b_ada)
    mod_x = mod[:b].reshape(b, N_MOD, d)
    mod_c = mod[b:b + 1].reshape(1, N_MOD, d)

    wi = w_in[0]
    o_r = 2 * qk + 2 * gw
    w_cat = jnp.concatenate([wi[:, :o_r], wi[:, o_r + 2 * rank:], wi[:, o_r:o_r + 2 * rank],
                             jnp.zeros((d, LANES - 2 * rank), F32)], axis=1).astype(BF16)
    wgk = jnp.concatenate([jnp.pad(w_gk_f[0], ((0, 0), (0, qk))), jnp.pad(w_gk_b[0], ((0, 0), (qk, 0))),
                           jnp.zeros((LANES - 2 * rank, 2 * qk), F32)], axis=0)
    bgk = jnp.concatenate([b_gk_f[0], b_gk_b[0]])[None, :]
    proj = functools.partial(_inproj_call, nw=norm_mix_w, w=w_cat, wgk=wgk.astype(BF16), bgk=bgk,
                             qk=qk, gw=gw, pw=pw, dk=dk)
    q, k, v, vt, g, xp, gk = proj(x, mod_x, tm=1024)
    _, kc, _, vtc, _, _, gkc = proj(ctx, mod_c, tm=SUPER)

    gla = _gla_call(q, k, v, vt, gk, g, kc, vtc, gkc, gla_norm_w)
    pool = _pool_call(xp, w_pool[0].astype(BF16), pool_scale)

    n = b * t
    wr = jnp.pad(w_router[0], ((0, 0), (0, LANES - n_exp)))
    br = jnp.pad(b_router, ((0, 0), (0, LANES - n_exp)))
    n_part = n // MOE_PARTS
    tm, tg = 1024, 1024
    assert n % MOE_PARTS == 0 and n_part % (SC_WORKERS * SC_WINDOW) == 0 and n_part % tm == 0 and n_part < RANK_LIMIT
    assert (n_part * TOP_K) % (SC_WORKERS * 2 * SC_GATHER_WINDOW) == 0
    n_pad = n_part * TOP_K + n_exp * EXPERT_ROWS
    nblk = n_pad // EXPERT_ROWS
    assert nblk <= 8 * LANES
    parts = range(MOE_PARTS)
    routed = [_route_call(gla.reshape(n, gw), pool.reshape(n, pw), x.reshape(n, d), mod_x, w_out[0].astype(BF16),
                          norm_mlp_w, wr, br, tm=tm, n_exp=n_exp,
                          first_tile=p * n_part // tm, n_tiles=n_part // tm) for p in parts]
    plans = [_plan_call(cnt[0, :n_exp], code_t, n_exp=n_exp, rows=EXPERT_ROWS, chunk=n_part)
             for (_, _, code_t, _, cnt) in routed]
    dests = [plan[0][:TOP_K] for plan in plans]
    xs = [_sc_scatter_call(routed[p][1],
                           dests[p].reshape(TOP_K, n_part // SC_WINDOW, SC_WINDOW).transpose(1, 0, 2), n_out=n_pad)
          for p in parts]
    ys = [_expert_call(plans[p][1:], xs[p], w_gu[0], b_gu[0], w_down[0], b_down[0]) for p in parts]
    y4 = [_sc_gather_call(ys[p], dests[p].reshape(SC_WORKERS, -1, SC_GATHER_WINDOW)).reshape(TOP_K, n_part, d // 2)
          for p in parts]
    out = None
    for p in parts:
        out = _combine_call(y4[p], routed[p][3], routed[p][0], mod_x, final_norm_w[None, :], out,
                            n=n, tg=tg, first_tile=p * n_part // tg)
    return out.reshape(b, t, d)
```

```python
import functools

import numpy as np
import jax
import jax.numpy as jnp
from jax import lax
from jax.experimental import pallas as pl
from jax.experimental.pallas import tpu as pltpu
from jax.experimental.pallas import tpu_sc as plsc

F32 = jnp.float32
BF16 = jnp.bfloat16
I32 = jnp.int32
U32 = jnp.uint32

GRID_W = 64
GLA_HEADS = 4
GLA_CHUNK = 64
GATE_NORMALIZER = 16.0
POOL_WINDOWS = (2, 4, 8, 16)
POOL_PAD_GRID_ROWS = 8
TOP_K = 4
RANK_LIMIT = 1 << 20
SWIGLU_LIMIT = 7.0
SWIGLU_ALPHA = 1.702
N_MOD = 6
EPS = 1e-6

LANES = 128
SUPER = 256
HEAD_PAIR_DK = 128
EXPERT_ROWS = 512
EXPERT_ROW_STEP = 128
MOE_PARTS = 2
EXPERT_VMEM_BYTES = 56 * 1024 * 1024
SC_CORES = 2
SC_SUBCORES = 16
SC_WORKERS = SC_CORES * SC_SUBCORES
SC_WINDOW = 32
SC_GATHER_WINDOW = 64


def _dot(a, b):
    return jnp.dot(a, b, preferred_element_type=F32)


def _dot_nt(a, b):
    return lax.dot_general(a, b, (((1,), (1,)), ((), ())), preferred_element_type=F32)


def _split_bf16(x):
    hi = x.astype(BF16)
    lo = (x - hi.astype(F32)).astype(BF16)
    return hi, lo


def _pack_bf16_pairs(x):
    c = x.shape[1] // 2
    lo = lax.bitcast_convert_type(x[:, :c].astype(BF16).astype(F32), U32)
    hi = lax.bitcast_convert_type(x[:, c:].astype(BF16).astype(F32), U32)
    return (lo >> 16) | hi


def _unpack_bf16_pairs(p):
    lo = lax.bitcast_convert_type(p << 16, F32)
    hi = lax.bitcast_convert_type(p & jnp.uint32(0xFFFF0000), F32)
    return lo, hi


def _rmsnorm(x, w):
    var = jnp.mean(x * x, axis=-1, keepdims=True)
    return x * lax.rsqrt(var + EPS) * w


def _mod_kernel(c_ref, w_ref, b_ref, o_ref):
    c = c_ref[...]
    s = c * jax.nn.sigmoid(c)
    o_ref[...] = jnp.dot(s, w_ref[...], precision=lax.Precision.HIGHEST,
                         preferred_element_type=F32) + b_ref[...]


def _mod_call(cc, w_ada, b_ada):
    rows, d = cc.shape
    n = w_ada.shape[1]
    tn = 1024
    return pl.pallas_call(
        _mod_kernel,
        grid=(n // tn,),
        in_specs=[pl.BlockSpec((rows, d), lambda j: (0, 0)),
                  pl.BlockSpec((d, tn), lambda j: (0, j)),
                  pl.BlockSpec((1, tn), lambda j: (0, j))],
        out_specs=pl.BlockSpec((rows, tn), lambda j: (0, j)),
        out_shape=jax.ShapeDtypeStruct((rows, n), F32),
        name="mod",
    )(cc, w_ada, b_ada)


def _inproj_kernel(x_ref, mod_ref, nw_ref, w_ref, wgk_ref, bgk_ref,
                   q_ref, k_ref, v_ref, vt_ref, g_ref, p_ref, gk_ref, *, qk, gw, pw, dk):
    x = x_ref[0]
    m = mod_ref[0]
    hm = (_rmsnorm(x, nw_ref[...]) * (1.0 + m[1:2]) + m[0:1]).astype(BF16)
    p = _dot(hm, w_ref[...])
    vt = p[:, 2 * qk:2 * qk + gw].T
    for s in range(vt_ref.shape[1]):
        vt_ref[0, s] = vt[:, s * SUPER:(s + 1) * SUPER].astype(BF16)
    o = 0
    q_ref[0] = (p[:, o:o + qk] * (dk ** -0.5)).astype(BF16); o += qk
    k_ref[0] = p[:, o:o + qk].astype(BF16); o += qk
    v_ref[0] = p[:, o:o + gw].astype(BF16); o += gw
    g_ref[0] = p[:, o:o + gw].astype(BF16); o += gw
    p_ref[0] = p[:, o:o + pw].astype(BF16); o += pw
    r = p[:, o:o + LANES]
    z = _dot(r.astype(BF16), wgk_ref[...]) + bgk_ref[...]
    gk_ref[0] = (jnp.minimum(z, 0.0) - jnp.log1p(jnp.exp(-jnp.abs(z)))) * (1.0 / GATE_NORMALIZER)


def _inproj_call(x, mod, nw, w, wgk, bgk, *, qk, gw, pw, dk, tm):
    b, t, d = x.shape
    n_in = w.shape[1]
    bs = lambda width: pl.BlockSpec((1, tm, width), lambda i, j: (i, j, 0))
    const = lambda shape: pl.BlockSpec(shape, lambda i, j: (0,) * len(shape))
    per_batch = mod.shape[0] > 1
    sds = jax.ShapeDtypeStruct
    return pl.pallas_call(
        functools.partial(_inproj_kernel, qk=qk, gw=gw, pw=pw, dk=dk),
        grid=(b, t // tm),
        in_specs=[bs(d),
                  pl.BlockSpec((1, N_MOD, d), (lambda i, j: (i, 0, 0)) if per_batch else (lambda i, j: (0, 0, 0))),
                  const((1, d)), const((d, n_in)), const((LANES, 2 * qk)), const((1, 2 * qk))],
        out_specs=[bs(qk), bs(qk), bs(gw),
                   pl.BlockSpec((1, tm // SUPER, gw, SUPER), lambda i, j: (i, j, 0, 0)),
                   bs(gw), bs(pw), bs(2 * qk)],
        out_shape=[sds((b, t, qk), BF16), sds((b, t, qk), BF16), sds((b, t, gw), BF16),
                   sds((b, t // SUPER, gw, SUPER), BF16),
                   sds((b, t, gw), BF16), sds((b, t, pw), BF16), sds((b, t, 2 * qk), F32)],
        compiler_params=pltpu.CompilerParams(dimension_semantics=("arbitrary", "arbitrary")),
        name="inproj",
    )(x, mod, nw, w, wgk, bgk)


def _gla_super(q, k, v, vt, gk, cm, amask, bd_mask, st, fwd, want_out):
    nch = SUPER // GLA_CHUNK
    order = tuple(range(nch)) if fwd else tuple(reversed(range(nch)))
    last_row = GLA_CHUNK - 1 if fwd else 0
    mid_row = GLA_CHUNK // 2 - 1 if fwd else GLA_CHUNK // 2
    hi, lo = _split_bf16(gk)
    bcum = _dot(cm, hi) + _dot(cm, lo)

    def chunk_row(r):
        return jnp.concatenate(
            [jnp.broadcast_to(bcum[c * GLA_CHUNK + r:c * GLA_CHUNK + r + 1, :], (GLA_CHUNK, bcum.shape[1]))
             for c in range(nch)], axis=0)

    chunk_of_row = lax.broadcasted_iota(I32, bcum.shape, 0) // GLA_CHUNK

    def by_chunk(x):
        return jnp.concatenate([jnp.where(chunk_of_row == c, x, 0.0).astype(BF16) for c in range(nch)], axis=1)

    blast = chunk_row(last_row)
    u_all = _dot(vt, by_chunk(k * jnp.exp(blast - bcum)))
    before = [None] * nch
    for c in order:
        before[c] = st
        decay = jnp.exp(bcum[c * GLA_CHUNK + last_row:c * GLA_CHUNK + last_row + 1, :])
        st = st * decay + jnp.where(bd_mask, u_all[:, c * HEAD_PAIR_DK:(c + 1) * HEAD_PAIR_DK], 0.0)
    if not want_out:
        return None, st
    bmid = chunk_row(mid_row)
    qt = q * jnp.exp(bcum - bmid)
    kt = (k * jnp.exp(bmid - bcum)).astype(BF16)
    lane = lax.broadcasted_iota(I32, qt.shape, 1)
    half = HEAD_PAIR_DK // 2
    o_heads = []
    for hh in range(2):
        sel = (lane < half) if hh == 0 else (lane >= half)
        a = _dot_nt(jnp.where(sel, qt, 0.0).astype(BF16), kt)
        a = jnp.where(amask, a, 0.0).astype(BF16)
        o_heads.append(_dot(a, v[:, hh * LANES:(hh + 1) * LANES]))
    qh = (q * jnp.exp(bcum)).astype(BF16)
    o_inter = jnp.concatenate(
        [_dot_nt(qh[c * GLA_CHUNK:(c + 1) * GLA_CHUNK], before[c].astype(BF16)) for c in range(nch)], axis=0)
    return jnp.concatenate(o_heads, axis=1) + o_inter, st


def _gla_kernel(q_ref, k_ref, v_ref, vt_ref, gkf_ref, gkb_ref, g_ref, kc_ref, vtc_ref, gkfc_ref, gkbc_ref,
                nw_ref, cmf_ref, cmb_ref, o_ref, stf_ref, stb_ref, of_ref, ob_ref):
    t = q_ref.shape[1]
    tc = kc_ref.shape[1]
    nsc, nscc = t // SUPER, tc // SUPER
    cmf = cmf_ref[...]
    cmb = cmb_ref[...]
    amask_f = cmf > 0
    amask_b = cmb > 0
    row = lax.broadcasted_iota(I32, (2 * LANES, HEAD_PAIR_DK), 0)
    lane = lax.broadcasted_iota(I32, (2 * LANES, HEAD_PAIR_DK), 1)
    bd_mask = (row < LANES) == (lane < HEAD_PAIR_DK // 2)

    def ctx_state(gk_ref, cm, fwd, j, st):
        rows = pl.ds(j * SUPER, SUPER)
        return _gla_super(None, kc_ref[0, rows, :].astype(F32), None, vtc_ref[0, j], gk_ref[0, rows, :],
                          cm, None, bd_mask, st, fwd, False)[1]

    def latent(gk_ref, cm, amask, fwd, j, st):
        rows = pl.ds(pl.multiple_of(j * SUPER, SUPER), SUPER)
        return _gla_super(q_ref[0, rows, :].astype(F32), k_ref[0, rows, :].astype(F32), v_ref[0, rows, :],
                          vt_ref[0, j], gk_ref[0, rows, :], cm, amask, bd_mask, st, fwd, True)

    st = jnp.zeros(stf_ref.shape, F32)
    for j in range(nscc):
        st = ctx_state(gkfc_ref, cmf, True, j, st)
    stf_ref[...] = st
    st = jnp.zeros(stb_ref.shape, F32)
    for j in reversed(range(nscc)):
        st = ctx_state(gkbc_ref, cmb, False, j, st)
    stb_ref[...] = st

    def scan_body(jj, carry):
        jb = nsc - 1 - jj
        of, stf = latent(gkf_ref, cmf, amask_f, True, jj, stf_ref[...])
        of_ref[pl.ds(pl.multiple_of(jj * SUPER, SUPER), SUPER), :] = of
        stf_ref[...] = stf
        ob, stb = latent(gkb_ref, cmb, amask_b, False, jb, stb_ref[...])
        ob_ref[pl.ds(pl.multiple_of(jb * SUPER, SUPER), SUPER), :] = ob
        stb_ref[...] = stb
        return carry

    lax.fori_loop(0, nsc, scan_body, 0, unroll=2)

    nw = nw_ref[...]

    def out_body(j, carry):
        rows = pl.ds(pl.multiple_of(j * SUPER, SUPER), SUPER)
        o = of_ref[rows, :] + ob_ref[rows, :]
        g = g_ref[0, rows, :].astype(F32)
        gate = g * jax.nn.sigmoid(g)
        for hh in range(2):
            oh = o[:, hh * LANES:(hh + 1) * LANES]
            on = oh * lax.rsqrt(jnp.mean(oh * oh, axis=-1, keepdims=True) + EPS) * nw
            o_ref[0, rows, hh * LANES:(hh + 1) * LANES] = (on * gate[:, hh * LANES:(hh + 1) * LANES]).astype(BF16)
        return carry

    lax.fori_loop(0, nsc, out_body, 0)


def _gla_masks():
    i = np.arange(SUPER)
    same = (i[:, None] // GLA_CHUNK) == (i[None, :] // GLA_CHUNK)
    fwd = same & (i[None, :] <= i[:, None])
    bwd = same & (i[None, :] >= i[:, None])
    return jnp.asarray(fwd, BF16), jnp.asarray(bwd, BF16)


def _gla_call(q, k, v, vt, gk, g, kc, vtc, gkc, nw):
    b, t, qk = q.shape
    tc = kc.shape[1]
    npair = qk // HEAD_PAIR_DK
    cmf, cmb = _gla_masks()
    lat = lambda width, off: pl.BlockSpec((1, t, width), lambda i, j: (i, 0, j + off))
    ctx = lambda width, off: pl.BlockSpec((1, tc, width), lambda i, j: (i, 0, j + off))
    tr = lambda n_groups: pl.BlockSpec((1, n_groups, 2 * LANES, SUPER), lambda i, j: (i, 0, j, 0))
    const = lambda shape: pl.BlockSpec(shape, lambda i, j: (0,) * len(shape))
    return pl.pallas_call(
        _gla_kernel,
        grid=(b, npair),
        in_specs=[lat(HEAD_PAIR_DK, 0), lat(HEAD_PAIR_DK, 0), lat(2 * LANES, 0), tr(t // SUPER),
                  lat(HEAD_PAIR_DK, 0), lat(HEAD_PAIR_DK, npair), lat(2 * LANES, 0),
                  ctx(HEAD_PAIR_DK, 0), tr(tc // SUPER), ctx(HEAD_PAIR_DK, 0), ctx(HEAD_PAIR_DK, npair),
                  const((1, LANES)), const(cmf.shape), const(cmb.shape)],
        out_specs=lat(2 * LANES, 0),
        out_shape=jax.ShapeDtypeStruct((b, t, v.shape[2]), BF16),
        scratch_shapes=[pltpu.VMEM((2 * LANES, HEAD_PAIR_DK), F32), pltpu.VMEM((2 * LANES, HEAD_PAIR_DK), F32),
                        pltpu.VMEM((t, 2 * LANES), F32), pltpu.VMEM((t, 2 * LANES), F32)],
        compiler_params=pltpu.CompilerParams(dimension_semantics=("arbitrary", "arbitrary")),
        name="gla",
    )(q, k, v, vt, gk, gk, g, kc, vtc, gkc, gkc, nw, cmf, cmb)


def _pool_kernel(x_ref, cm_ref, cnt_ref, wp_ref, ps_ref, o_ref, a_ref, b_ref):
    t = x_ref.shape[1]
    pad = POOL_PAD_GRID_ROWS * GRID_W
    total = t + 2 * pad
    for gi, w in enumerate(POOL_WINDOWS):
        lo = w // 2
        cols = slice(gi * LANES, (gi + 1) * LANES)
        cmat = cm_ref[gi]
        a_ref[0:pad, :] = jnp.zeros((pad, LANES), F32)
        a_ref[pad + t:total, :] = jnp.zeros((pad, LANES), F32)
        for blk in range(t // SUPER):
            rs = slice(blk * SUPER, (blk + 1) * SUPER)
            a_ref[pad + blk * SUPER:pad + (blk + 1) * SUPER, :] = _dot(cmat, x_ref[0, rs, cols])
        src, dst = a_ref, b_ref
        m = 1
        while m < w:
            sh = m * GRID_W
            dst[0:total - sh, :] = src[0:total - sh, :] + src[sh:total, :]
            src, dst = dst, src
            m *= 2
        first = pad - lo * GRID_W
        pooled = src[first:first + t, :] / cnt_ref[gi] - x_ref[0, :, cols].astype(F32)
        yp = _dot(pooled.astype(BF16), wp_ref[gi]) * ps_ref[:, cols]
        o_ref[0, :, cols] = yp.astype(BF16)


def _pool_col_mats():
    i = np.arange(SUPER)
    same_row = (i[:, None] // GRID_W) == (i[None, :] // GRID_W)
    d = i[None, :] - i[:, None]
    mats = []
    for w in POOL_WINDOWS:
        lo = w // 2
        hi = w - 1 - lo
        mats.append(same_row & (d >= -lo) & (d <= hi))
    return jnp.asarray(np.stack(mats), BF16)


def _pool_counts(t):
    rows = t // GRID_W
    r = np.arange(t) // GRID_W
    c = np.arange(t) % GRID_W
    out = []
    for w in POOL_WINDOWS:
        lo = w // 2
        hi = w - 1 - lo
        cnt_r = np.minimum(r + hi + 1, rows) - np.maximum(r - lo, 0)
        cnt_c = np.minimum(c + hi + 1, GRID_W) - np.maximum(c - lo, 0)
        out.append(np.broadcast_to((cnt_r * cnt_c).astype(np.float32)[:, None], (t, LANES)))
    return jnp.asarray(np.stack(out))


def _pool_call(xp, w_pool, pool_scale):
    b, t, pw = xp.shape
    ng = len(POOL_WINDOWS)
    assert max(POOL_WINDOWS) // 2 <= POOL_PAD_GRID_ROWS and t % GRID_W == 0
    cm = _pool_col_mats()
    cnt = _pool_counts(t)
    staged = t + 2 * POOL_PAD_GRID_ROWS * GRID_W
    const = lambda shape: pl.BlockSpec(shape, lambda i: (0,) * len(shape))
    return pl.pallas_call(
        _pool_kernel,
        grid=(b,),
        in_specs=[pl.BlockSpec((1, t, pw), lambda i: (i, 0, 0)),
                  const(cm.shape), const(cnt.shape), const((ng, LANES, LANES)), const((1, pw))],
        out_specs=pl.BlockSpec((1, t, pw), lambda i: (i, 0, 0)),
        out_shape=jax.ShapeDtypeStruct((b, t, pw), BF16),
        scratch_shapes=[pltpu.VMEM((staged, LANES), F32), pltpu.VMEM((staged, LANES), F32)],
        compiler_params=pltpu.CompilerParams(dimension_semantics=("arbitrary",)),
        name="pool",
    )(xp, cm, cnt, w_pool, pool_scale)


def _route_kernel(gla_ref, pool_ref, x_ref, mod_ref, wo_ref, nw_ref, wr_ref, br_ref, lt_ref,
                  h_ref, xt_ref, code_ref, wt_ref, cnt_ref, run_ref, wr2_ref, *, gw, n_exp):
    i = pl.program_id(0)

    @pl.when(i == 0)
    def _():
        run_ref[...] = jnp.zeros_like(run_ref)
        wh, wl = _split_bf16(wr_ref[...])
        wr2_ref[:, :LANES] = wh
        wr2_ref[:, LANES:] = wl

    m = mod_ref[0]
    acc = _dot(gla_ref[...], wo_ref[0:gw, :]) + _dot(pool_ref[...], wo_ref[gw:, :])
    h = x_ref[...] + m[2:3] * acc
    h_ref[...] = h
    xt = _rmsnorm(h, nw_ref[...]) * (1.0 + m[4:5]) + m[3:4]
    xt_ref[...] = _pack_bf16_pairs(xt)
    xh, xl = _split_bf16(xt)
    wr2 = wr2_ref[...]
    t1 = _dot(xh, wr2)
    logits = t1[:, :LANES] + t1[:, LANES:] + _dot(xl, wr2[:, :LANES]) + br_ref[...]
    lane = lax.broadcasted_iota(I32, logits.shape, 1)
    neg = jnp.float32(-jnp.inf)
    logits = jnp.where(lane < n_exp, logits, neg)
    vals, hots = [], []
    e_out = jnp.zeros(logits.shape, I32)
    lane_f = lane.astype(F32)
    for j in range(TOP_K):
        mx = jnp.max(logits, axis=-1, keepdims=True)
        idx = jnp.min(jnp.where(logits == mx, lane_f, float(LANES)), axis=-1, keepdims=True)
        hot = lane_f == idx
        vals.append(mx)
        hots.append(hot)
        e_out = jnp.where(lane == j, idx.astype(I32), e_out)
        logits = jnp.where(hot, neg, logits)
    ex = [jnp.exp(v - vals[0]) for v in vals]
    den = ex[0] + ex[1] + ex[2] + ex[3]
    w_out = jnp.zeros(logits.shape, F32)
    for j in range(TOP_K):
        w_out = jnp.where(lane == j, ex[j] / den, w_out)
    osum = jnp.where(hots[0] | hots[1] | hots[2] | hots[3], 1.0, 0.0)
    before = _dot(lt_ref[...], osum.astype(BF16)) + run_ref[0:1, :]
    rk_out = jnp.zeros(logits.shape, I32)
    for j in range(TOP_K):
        rj = jnp.sum(jnp.where(hots[j], before, 0.0), axis=-1, keepdims=True)
        rk_out = jnp.where(lane == j, rj.astype(I32), rk_out)
    run = run_ref[0:1, :] + jnp.sum(osum, axis=0, keepdims=True)
    run_ref[...] = jnp.broadcast_to(run, run_ref.shape)
    code = e_out * RANK_LIMIT + rk_out
    code_ref[...] = code.T[:code_ref.shape[0], :]
    wt_ref[...] = w_out
    cnt_ref[...] = jnp.broadcast_to(run, cnt_ref.shape).astype(I32)


def _route_call(gla, pool, x, mod, w_out, nw, wr, br, *, tm, n_exp, first_tile, n_tiles):
    n_all, d = x.shape
    n = n_tiles * tm
    gw = gla.shape[1]
    t_per_b = n_all // mod.shape[0]
    lt = jnp.asarray(np.tril(np.ones((tm, tm), np.float32), -1), BF16)
    row_in = lambda width: pl.BlockSpec((tm, width), lambda i: (i + first_tile, 0))
    row = lambda width: pl.BlockSpec((tm, width), lambda i: (i, 0))
    const = lambda shape: pl.BlockSpec(shape, lambda i: (0,) * len(shape))
    sds = jax.ShapeDtypeStruct
    return pl.pallas_call(
        functools.partial(_route_kernel, gw=gw, n_exp=n_exp),
        grid=(n_tiles,),
        in_specs=[row_in(gw), row_in(pool.shape[1]), row_in(d),
                  pl.BlockSpec((1, N_MOD, d), lambda i: ((i + first_tile) * tm // t_per_b, 0, 0)),
                  const(w_out.shape), const((1, d)), const(wr.shape), const((1, LANES)), const((tm, tm))],
        out_specs=[row(d), row(d // 2), pl.BlockSpec((8, tm), lambda i: (0, i)), row(LANES), const((8, LANES))],
        out_shape=[sds((n, d), F32), sds((n, d // 2), U32), sds((8, n), I32),
                   sds((n, LANES), F32), sds((8, LANES), I32)],
        scratch_shapes=[pltpu.VMEM((8, LANES), F32), pltpu.VMEM((d, 2 * LANES), BF16)],
        compiler_params=pltpu.CompilerParams(dimension_semantics=("arbitrary",)),
        name="route",
    )(gla, pool, x, mod, w_out, nw, wr, br, lt)


def _plan_kernel(cnt_ref, code_ref, dest_ref, be_ref, nv_ref, first_ref, next_ref, slot_ref, nu_ref, start_ref,
                 *, n_exp, rows):
    @pl.when(pl.program_id(0) == 0)
    def _():
        blk = (lax.broadcasted_iota(I32, be_ref.shape, 0) * LANES + lax.broadcasted_iota(I32, be_ref.shape, 1))
        blk_row0 = blk * rows
        nxt_e = [None] * n_exp
        nxt = jnp.int32(-1)
        for e in reversed(range(n_exp)):
            nxt_e[e] = nxt
            nxt = jnp.where(cnt_ref[e] > 0, e, nxt)
        zeros = jnp.zeros(be_ref.shape, I32)
        be, end_valid, first, nxt_blk, slot = zeros, zeros, zeros, zeros - 1, zeros
        acc = jnp.int32(0)
        last_e = jnp.int32(0)
        ordinal = jnp.int32(0)
        for e in range(n_exp):
            c = cnt_ref[e]
            start_ref[e] = acc
            in_e = (blk_row0 >= acc) & (c > 0)
            end_valid = jnp.where(in_e, acc + c, end_valid)
            be = jnp.where(in_e, e, be)
            first = jnp.where(in_e, (blk_row0 == acc).astype(I32), first)
            nxt_blk = jnp.where(in_e, nxt_e[e], nxt_blk)
            slot = jnp.where(in_e, ordinal % 2, slot)
            acc = acc + (c + rows - 1) // rows * rows
            last_e = jnp.where(c > 0, e, last_e)
            ordinal = ordinal + (c > 0).astype(I32)
        n_used = acc // rows
        nu_ref[0] = n_used
        be_ref[...] = jnp.where(blk < n_used, be, last_e)
        nv_ref[...] = jnp.clip(end_valid - blk_row0, 0, rows)
        first_ref[...] = first
        next_ref[...] = nxt_blk
        slot_ref[...] = slot

    code = code_ref[...]
    e_vec = code // RANK_LIMIT
    dest = code % RANK_LIMIT
    for e in range(n_exp):
        dest = dest + jnp.where(e_vec == e, start_ref[e], 0)
    dest_ref[...] = dest


def _plan_call(counts, code_t, *, n_exp, rows, chunk):
    n = code_t.shape[1]
    sds = jax.ShapeDtypeStruct
    smem = pltpu.SMEM
    return pl.pallas_call(
        functools.partial(_plan_kernel, n_exp=n_exp, rows=rows),
        grid=(n // chunk,),
        in_specs=[pl.BlockSpec(memory_space=smem), pl.BlockSpec((8, chunk), lambda i: (0, i))],
        out_specs=[pl.BlockSpec((8, chunk), lambda i: (0, i))]
        + [pl.BlockSpec((8, LANES), lambda i: (0, 0))] * 5 + [pl.BlockSpec(memory_space=smem)],
        out_shape=[sds((8, n), I32)] + [sds((8, LANES), I32)] * 5 + [sds((1,), I32)],
        scratch_shapes=[pltpu.SMEM((n_exp,), I32)],
        compiler_params=pltpu.CompilerParams(dimension_semantics=("arbitrary",)),
        name="plan",
    )(counts, code_t)


def _sc_worker_id():
    return lax.axis_index("s") * SC_CORES + lax.axis_index("c")


def _sc_scatter_call(x, idx3, *, n_out):
    n, d = x.shape
    n_win_total, k, w = idx3.shape
    n_win = n_win_total // SC_WORKERS
    mesh = plsc.VectorSubcoreMesh(core_axis_name="c", subcore_axis_name="s")

    @functools.partial(
        pl.kernel, mesh=mesh,
        out_type=jax.ShapeDtypeStruct((n_out, d), x.dtype),
        scratch_types=[pltpu.VMEM((k, w), I32), pltpu.VMEM((w, d), x.dtype), pltpu.SemaphoreType.DMA],
        name="sc_dispatch",
    )
    def kern(x_hbm, idx_hbm, out_hbm, idx_v, rows_v, sem):
        wid = _sc_worker_id()

        @pl.loop(0, n_win)
        def _(i):
            win = wid * n_win + i
            pltpu.sync_copy(idx_hbm.at[win], idx_v)
            pltpu.sync_copy(x_hbm.at[pl.ds(win * w, w)], rows_v)
            for j in range(k):
                pltpu.async_copy(rows_v, out_hbm.at[idx_v.at[j]], sem).wait()

    return kern(x, idx3)


def _sc_gather_call(table, idx3):
    n_workers, n_win, w = idx3.shape
    d = table.shape[1]
    assert n_workers == SC_WORKERS and n_win % 2 == 0
    mesh = plsc.VectorSubcoreMesh(core_axis_name="c", subcore_axis_name="s")

    @functools.partial(
        pl.kernel, mesh=mesh,
        out_type=jax.ShapeDtypeStruct((n_workers * n_win * w, d), table.dtype),
        scratch_types=[pltpu.VMEM((n_win, w), I32), pltpu.VMEM((2, w, d), table.dtype),
                       pltpu.SemaphoreType.DMA((2,)), pltpu.SemaphoreType.DMA((2,))],
        name="sc_gather",
    )
    def kern(table_hbm, idx_hbm, out_hbm, idx_v, rows_v, gsem, osem):
        wid = _sc_worker_id()
        base = wid * n_win
        pltpu.sync_copy(idx_hbm.at[wid], idx_v)

        def gather(wi, b):
            return pltpu.make_async_copy(table_hbm.at[idx_v.at[wi]], rows_v.at[b], gsem.at[b])

        def put(wi, b):
            return pltpu.make_async_copy(rows_v.at[b], out_hbm.at[pl.ds((base + wi) * w, w)], osem.at[b])

        gather(0, 0).start()

        @pl.loop(0, n_win, step=2)
        def _(i):
            for b in range(2):
                wi = i + b

                @pl.when(wi + 1 < n_win)
                def _():
                    @pl.when(wi >= 1)
                    def _():
                        put(wi - 1, 1 - b).wait()
                    gather(wi + 1, 1 - b).start()

                gather(wi, b).wait()
                put(wi, b).start()

        put(n_win - 2, 0).wait()
        put(n_win - 1, 1).wait()

    return kern(table, idx3)


def _expert_kernel(be_ref, nu_ref, nv_ref, first_ref, next_ref, slot_ref,
                   x_ref, wgu_hbm, bgu_ref, wd_hbm, bd_ref, y_ref,
                   wgu_f32_ref, wd_f32_ref, wgu_bf_ref, wd_bf_ref, sem, *, d_ff):
    i = pl.program_id(0)
    slot = slot_ref[i]

    def fetch(e, s):
        return (pltpu.make_async_copy(wgu_hbm.at[e], wgu_f32_ref.at[s], sem.at[0, s]),
                pltpu.make_async_copy(wd_hbm.at[e], wd_f32_ref.at[s], sem.at[1, s]))

    @pl.when(i == 0)
    def _():
        for cp in fetch(be_ref[0], slot):
            cp.start()

    @pl.when(first_ref[i] == 1)
    def _():
        for cp in fetch(be_ref[i], slot):
            cp.wait()

        @pl.when(next_ref[i] >= 0)
        def _():
            for cp in fetch(next_ref[i], 1 - slot):
                cp.start()

        wgu_bf_ref[...] = wgu_f32_ref[slot].astype(BF16)
        wd_bf_ref[...] = wd_f32_ref[slot].astype(BF16)

    n_valid = nv_ref[i]
    n_rows = x_ref.shape[0]

    def expert_rows(rows):
        row = lax.broadcasted_iota(I32, (rows, x_ref.shape[1]), 0)
        lo, hi = _unpack_bf16_pairs(jnp.where(row < n_valid, x_ref[0:rows, :], jnp.uint32(0)))
        xb = jnp.concatenate([lo, hi], axis=1).astype(BF16)
        gu = _dot(xb, wgu_bf_ref[...]) + bgu_ref[0]
        gate = jnp.minimum(gu[:, :d_ff], SWIGLU_LIMIT)
        up = jnp.clip(gu[:, d_ff:], -SWIGLU_LIMIT, SWIGLU_LIMIT)
        act = (up + 1.0) * (0.5 * gate) * (1.0 + jnp.tanh((0.5 * SWIGLU_ALPHA) * gate))
        y_ref[0:rows, :] = _pack_bf16_pairs(_dot(act.astype(BF16), wd_bf_ref[...]) + bd_ref[0])

    in_use = i < nu_ref[0]

    for rows in range(EXPERT_ROW_STEP, n_rows + 1, EXPERT_ROW_STEP):
        @pl.when(in_use & (n_valid > rows - EXPERT_ROW_STEP) & (n_valid <= rows))
        def _(rows=rows):
            expert_rows(rows)
            if rows < n_rows:
                y_ref[rows:, :] = jnp.zeros((n_rows - rows, y_ref.shape[1]), y_ref.dtype)


def _expert_call(plan, xs, w_gu, b_gu, w_down, b_down):
    block_e, n_valid, first, nxt, slot, n_used = plan
    n_pad = xs.shape[0]
    n_exp, d, two_ff = w_gu.shape
    d_ff = two_ff // 2
    nblk = n_pad // EXPERT_ROWS
    rows = lambda i, be, nu, *_: (jnp.minimum(i, nu[0] - 1), 0)
    per_e = lambda i, be, *_: (be[i], 0, 0)
    grid_spec = pltpu.PrefetchScalarGridSpec(
        num_scalar_prefetch=6,
        grid=(nblk,),
        in_specs=[pl.BlockSpec((EXPERT_ROWS, d // 2), rows),
                  pl.BlockSpec(memory_space=pl.ANY), pl.BlockSpec((1, 1, two_ff), per_e),
                  pl.BlockSpec(memory_space=pl.ANY), pl.BlockSpec((1, 1, d), per_e)],
        out_specs=pl.BlockSpec((EXPERT_ROWS, d // 2), rows),
        scratch_shapes=[pltpu.VMEM((2, d, two_ff), F32), pltpu.VMEM((2, d_ff, d), F32),
                        pltpu.VMEM((d, two_ff), BF16), pltpu.VMEM((d_ff, d), BF16),
                        pltpu.SemaphoreType.DMA((2, 2))],
    )
    flat = lambda a: a.reshape(-1)
    return pl.pallas_call(
        functools.partial(_expert_kernel, d_ff=d_ff),
        grid_spec=grid_spec,
        out_shape=jax.ShapeDtypeStruct((n_pad, d // 2), U32),
        compiler_params=pltpu.CompilerParams(dimension_semantics=("arbitrary",),
                                             vmem_limit_bytes=EXPERT_VMEM_BYTES),
        name="experts",
    )(flat(block_e), n_used, flat(n_valid), flat(first), flat(nxt), flat(slot),
      xs, w_gu, b_gu.reshape(n_exp, 1, two_ff), w_down, b_down.reshape(n_exp, 1, d))


def _combine_kernel(y4_ref, wt_ref, h_ref, mod_ref, fw_ref, o_ref):
    wt = wt_ref[...]
    acc_lo, acc_hi = None, None
    for j in range(TOP_K):
        lo, hi = _unpack_bf16_pairs(y4_ref[j])
        w = wt[:, j:j + 1]
        acc_lo = w * lo if j == 0 else acc_lo + w * lo
        acc_hi = w * hi if j == 0 else acc_hi + w * hi
    acc = jnp.concatenate([acc_lo, acc_hi], axis=1)
    m = mod_ref[0]
    o_ref[...] = _rmsnorm(h_ref[...] + m[5:6] * acc, fw_ref[...])


def _combine_call(y4, wts, h, mod, fw, prev_out, *, n, tg, first_tile):
    d = h.shape[1]
    t_per_b = n // mod.shape[0]
    part = lambda width: pl.BlockSpec((tg, width), lambda i: (i, 0))
    row = lambda width: pl.BlockSpec((tg, width), lambda i: (i + first_tile, 0))
    in_specs = [pl.BlockSpec((TOP_K, tg, d // 2), lambda i: (0, i, 0)),
                part(LANES), part(d),
                pl.BlockSpec((1, N_MOD, d), lambda i: ((i + first_tile) * tg // t_per_b, 0, 0)),
                pl.BlockSpec((1, d), lambda i: (0, 0))]
    args = [y4, wts, h, mod, fw]
    kern = _combine_kernel
    aliases = {}
    if prev_out is not None:
        in_specs.append(pl.BlockSpec(memory_space=pl.ANY))
        args.append(prev_out)
        kern = lambda y4_ref, wt_ref, h_ref, mod_ref, fw_ref, prev_ref, o_ref: _combine_kernel(
            y4_ref, wt_ref, h_ref, mod_ref, fw_ref, o_ref)
        aliases = {len(args) - 1: 0}
    return pl.pallas_call(
        kern,
        grid=(y4.shape[1] // tg,),
        in_specs=in_specs,
        out_specs=row(d),
        out_shape=jax.ShapeDtypeStruct((n, d), F32),
        input_output_aliases=aliases,
        compiler_params=pltpu.CompilerParams(dimension_semantics=("arbitrary",)),
        name="combine",
    )(*args)


def kernel(x, c, ctx, c_ctx, w_ada, b_ada, norm_mix_w, norm_mlp_w, w_in, w_gk_f, b_gk_f, w_gk_b, b_gk_b,
           gla_norm_w, w_pool, pool_scale, w_out, w_router, b_router, w_gu, b_gu, w_down, b_down,
           final_norm_w):
    b, t, d = x.shape
    assert w_ada.shape[0] == 1, "single-layer trunk"
    n_exp = w_router.shape[2]
    rank = w_gk_f.shape[1]
    qk = w_gk_f.shape[2]
    dk = qk // GLA_HEADS
    gw = GLA_HEADS * gla_norm_w.shape[1]
    pw = w_pool.shape[1] * w_pool.shape[2]
    assert w_in.shape[2] == 2 * qk + 2 * gw + 2 * rank + pw and 2 * rank <= LANES
    assert t % SUPER == 0 and ctx.shape[1] % SUPER == 0 and n_exp <= LANES

    rows = -(-(b + 1) // 8) * 8
    cc = jnp.concatenate([c, c_ctx[None, :], jnp.zeros((rows - b - 1, d), F32)], axis=0)
    mod = _mod_call(cc, w_ada[0], b_ada)
    mod_x = mod[:b].reshape(b, N_MOD, d)
    mod_c = mod[b:b + 1].reshape(1, N_MOD, d)

    wi = w_in[0]
    o_r = 2 * qk + 2 * gw
    w_cat = jnp.concatenate([wi[:, :o_r], wi[:, o_r + 2 * rank:], wi[:, o_r:o_r + 2 * rank],
                             jnp.zeros((d, LANES - 2 * rank), F32)], axis=1).astype(BF16)
    wgk = jnp.concatenate([jnp.pad(w_gk_f[0], ((0, 0), (0, qk))), jnp.pad(w_gk_b[0], ((0, 0), (qk, 0))),
                           jnp.zeros((LANES - 2 * rank, 2 * qk), F32)], axis=0)
    bgk = jnp.concatenate([b_gk_f[0], b_gk_b[0]])[None, :]
    proj = functools.partial(_inproj_call, nw=norm_mix_w, w=w_cat, wgk=wgk.astype(BF16), bgk=bgk,
                             qk=qk, gw=gw, pw=pw, dk=dk)
    q, k, v, vt, g, xp, gk = proj(x, mod_x, tm=1024)
    _, kc, _, vtc, _, _, gkc = proj(ctx, mod_c, tm=SUPER)

    gla = _gla_call(q, k, v, vt, gk, g, kc, vtc, gkc, gla_norm_w)
    pool = _pool_call(xp, w_pool[0].astype(BF16), pool_scale)

    n = b * t
    wr = jnp.pad(w_router[0], ((0, 0), (0, LANES - n_exp)))
    br = jnp.pad(b_router, ((0, 0), (0, LANES - n_exp)))
    n_part = n // MOE_PARTS
    tm, tg = 512, 1024
    assert n % MOE_PARTS == 0 and n_part % (SC_WORKERS * SC_WINDOW) == 0 and n_part % tm == 0 and n_part < RANK_LIMIT
    assert (n_part * TOP_K) % (SC_WORKERS * 2 * SC_GATHER_WINDOW) == 0
    n_pad = n_part * TOP_K + n_exp * EXPERT_ROWS
    nblk = n_pad // EXPERT_ROWS
    assert nblk <= 8 * LANES
    parts = range(MOE_PARTS)
    routed = [_route_call(gla.reshape(n, gw), pool.reshape(n, pw), x.reshape(n, d), mod_x, w_out[0].astype(BF16),
                          norm_mlp_w, wr, br, tm=tm, n_exp=n_exp,
                          first_tile=p * n_part // tm, n_tiles=n_part // tm) for p in parts]
    plans = [_plan_call(cnt[0, :n_exp], code_t, n_exp=n_exp, rows=EXPERT_ROWS, chunk=min(n_part, 4096))
             for (_, _, code_t, _, cnt) in routed]
    dests = [plan[0][:TOP_K] for plan in plans]
    xs = [_sc_scatter_call(routed[p][1],
                           dests[p].reshape(TOP_K, n_part // SC_WINDOW, SC_WINDOW).transpose(1, 0, 2), n_out=n_pad)
          for p in parts]
    ys = [_expert_call(plans[p][1:], xs[p], w_gu[0], b_gu[0], w_down[0], b_down[0]) for p in parts]
    y4 = [_sc_gather_call(ys[p], dests[p].reshape(SC_WORKERS, -1, SC_GATHER_WINDOW)).reshape(TOP_K, n_part, d // 2)
          for p in parts]
    out = None
    for p in parts:
        out = _combine_call(y4[p], routed[p][3], routed[p][0], mod_x, final_norm_w[None, :], out,
                            n=n, tg=tg, first_tile=p * n_part // tg)
    return out.reshape(b, t, d)
```

```python
import functools

import numpy as np
import jax
import jax.numpy as jnp
from jax import lax
from jax.experimental import pallas as pl
from jax.experimental.pallas import tpu as pltpu
from jax.experimental.pallas import tpu_sc as plsc

F32 = jnp.float32
BF16 = jnp.bfloat16
I32 = jnp.int32
U32 = jnp.uint32

GRID_W = 64
GLA_HEADS = 4
GLA_CHUNK = 64
GATE_NORMALIZER = 16.0
POOL_WINDOWS = (2, 4, 8, 16)
POOL_PAD_GRID_ROWS = 8
TOP_K = 4
RANK_LIMIT = 1 << 20
SWIGLU_LIMIT = 7.0
SWIGLU_ALPHA = 1.702
N_MOD = 6
EPS = 1e-6

LANES = 128
SUPER = 256
HEAD_PAIR_DK = 128
EXPERT_ROWS = 512
EXPERT_ROW_STEP = 128
MOE_PARTS = 2
EXPERT_VMEM_BYTES = 56 * 1024 * 1024
SC_CORES = 2
SC_SUBCORES = 16
SC_WORKERS = SC_CORES * SC_SUBCORES
SC_WINDOW = 32
SC_GATHER_WINDOW = 64


def _dot(a, b):
    return jnp.dot(a, b, preferred_element_type=F32)


def _dot_nt(a, b):
    return lax.dot_general(a, b, (((1,), (1,)), ((), ())), preferred_element_type=F32)


def _split_bf16(x):
    hi = x.astype(BF16)
    lo = (x - hi.astype(F32)).astype(BF16)
    return hi, lo


def _pack_bf16_pairs(x):
    c = x.shape[1] // 2
    lo = lax.bitcast_convert_type(x[:, :c].astype(BF16).astype(F32), U32)
    hi = lax.bitcast_convert_type(x[:, c:].astype(BF16).astype(F32), U32)
    return (lo >> 16) | hi


def _unpack_bf16_pairs(p):
    lo = lax.bitcast_convert_type(p << 16, F32)
    hi = lax.bitcast_convert_type(p & jnp.uint32(0xFFFF0000), F32)
    return lo, hi


def _rmsnorm(x, w):
    var = jnp.mean(x * x, axis=-1, keepdims=True)
    return x * lax.rsqrt(var + EPS) * w


def _mod_kernel(c_ref, w_ref, b_ref, o_ref):
    c = c_ref[...]
    s = c * jax.nn.sigmoid(c)
    o_ref[...] = jnp.dot(s, w_ref[...], precision=lax.Precision.HIGHEST,
                         preferred_element_type=F32) + b_ref[...]


def _mod_call(cc, w_ada, b_ada):
    rows, d = cc.shape
    n = w_ada.shape[1]
    tn = 1024
    return pl.pallas_call(
        _mod_kernel,
        grid=(n // tn,),
        in_specs=[pl.BlockSpec((rows, d), lambda j: (0, 0)),
                  pl.BlockSpec((d, tn), lambda j: (0, j)),
                  pl.BlockSpec((1, tn), lambda j: (0, j))],
        out_specs=pl.BlockSpec((rows, tn), lambda j: (0, j)),
        out_shape=jax.ShapeDtypeStruct((rows, n), F32),
        name="mod",
    )(cc, w_ada, b_ada)


def _inproj_kernel(x_ref, mod_ref, nw_ref, w_ref, wgk_ref, bgk_ref,
                   q_ref, k_ref, v_ref, vt_ref, g_ref, p_ref, gk_ref, *, qk, gw, pw, dk):
    x = x_ref[0]
    m = mod_ref[0]
    hm = (_rmsnorm(x, nw_ref[...]) * (1.0 + m[1:2]) + m[0:1]).astype(BF16)
    p = _dot(hm, w_ref[...])
    vt = p[:, 2 * qk:2 * qk + gw].T
    for s in range(vt_ref.shape[1]):
        vt_ref[0, s] = vt[:, s * SUPER:(s + 1) * SUPER].astype(BF16)
    o = 0
    q_ref[0] = (p[:, o:o + qk] * (dk ** -0.5)).astype(BF16); o += qk
    k_ref[0] = p[:, o:o + qk].astype(BF16); o += qk
    v_ref[0] = p[:, o:o + gw].astype(BF16); o += gw
    g_ref[0] = p[:, o:o + gw].astype(BF16); o += gw
    p_ref[0] = p[:, o:o + pw].astype(BF16); o += pw
    r = p[:, o:o + LANES]
    z = _dot(r.astype(BF16), wgk_ref[...]) + bgk_ref[...]
    gk_ref[0] = (jnp.minimum(z, 0.0) - jnp.log1p(jnp.exp(-jnp.abs(z)))) * (1.0 / GATE_NORMALIZER)


def _inproj_call(x, mod, nw, w, wgk, bgk, *, qk, gw, pw, dk, tm):
    b, t, d = x.shape
    n_in = w.shape[1]
    bs = lambda width: pl.BlockSpec((1, tm, width), lambda i, j: (i, j, 0))
    const = lambda shape: pl.BlockSpec(shape, lambda i, j: (0,) * len(shape))
    per_batch = mod.shape[0] > 1
    sds = jax.ShapeDtypeStruct
    return pl.pallas_call(
        functools.partial(_inproj_kernel, qk=qk, gw=gw, pw=pw, dk=dk),
        grid=(b, t // tm),
        in_specs=[bs(d),
                  pl.BlockSpec((1, N_MOD, d), (lambda i, j: (i, 0, 0)) if per_batch else (lambda i, j: (0, 0, 0))),
                  const((1, d)), const((d, n_in)), const((LANES, 2 * qk)), const((1, 2 * qk))],
        out_specs=[bs(qk), bs(qk), bs(gw),
                   pl.BlockSpec((1, tm // SUPER, gw, SUPER), lambda i, j: (i, j, 0, 0)),
                   bs(gw), bs(pw), bs(2 * qk)],
        out_shape=[sds((b, t, qk), BF16), sds((b, t, qk), BF16), sds((b, t, gw), BF16),
                   sds((b, t // SUPER, gw, SUPER), BF16),
                   sds((b, t, gw), BF16), sds((b, t, pw), BF16), sds((b, t, 2 * qk), F32)],
        compiler_params=pltpu.CompilerParams(dimension_semantics=("arbitrary", "arbitrary")),
        name="inproj",
    )(x, mod, nw, w, wgk, bgk)


def _gla_super(q, k, v, vt, gk, cm, amask, bd_mask, st, fwd, want_out):
    nch = SUPER // GLA_CHUNK
    order = tuple(range(nch)) if fwd else tuple(reversed(range(nch)))
    last_row = GLA_CHUNK - 1 if fwd else 0
    mid_row = GLA_CHUNK // 2 - 1 if fwd else GLA_CHUNK // 2
    hi, lo = _split_bf16(gk)
    bcum = _dot(cm, hi) + _dot(cm, lo)

    def chunk_row(r):
        return jnp.concatenate(
            [jnp.broadcast_to(bcum[c * GLA_CHUNK + r:c * GLA_CHUNK + r + 1, :], (GLA_CHUNK, bcum.shape[1]))
             for c in range(nch)], axis=0)

    chunk_of_row = lax.broadcasted_iota(I32, bcum.shape, 0) // GLA_CHUNK

    def by_chunk(x):
        return jnp.concatenate([jnp.where(chunk_of_row == c, x, 0.0).astype(BF16) for c in range(nch)], axis=1)

    blast = chunk_row(last_row)
    u_all = _dot(vt, by_chunk(k * jnp.exp(blast - bcum)))
    before = [None] * nch
    for c in order:
        before[c] = st
        decay = jnp.exp(bcum[c * GLA_CHUNK + last_row:c * GLA_CHUNK + last_row + 1, :])
        st = st * decay + jnp.where(bd_mask, u_all[:, c * HEAD_PAIR_DK:(c + 1) * HEAD_PAIR_DK], 0.0)
    if not want_out:
        return None, st
    bmid = chunk_row(mid_row)
    qt = q * jnp.exp(bcum - bmid)
    kt = (k * jnp.exp(bmid - bcum)).astype(BF16)
    lane = lax.broadcasted_iota(I32, qt.shape, 1)
    half = HEAD_PAIR_DK // 2
    o_heads = []
    for hh in range(2):
        sel = (lane < half) if hh == 0 else (lane >= half)
        a = _dot_nt(jnp.where(sel, qt, 0.0).astype(BF16), kt)
        a = jnp.where(amask, a, 0.0).astype(BF16)
        o_heads.append(_dot(a, v[:, hh * LANES:(hh + 1) * LANES]))
    qh = (q * jnp.exp(bcum)).astype(BF16)
    o_inter = jnp.concatenate(
        [_dot_nt(qh[c * GLA_CHUNK:(c + 1) * GLA_CHUNK], before[c].astype(BF16)) for c in range(nch)], axis=0)
    return jnp.concatenate(o_heads, axis=1) + o_inter, st


def _gla_kernel(q_ref, k_ref, v_ref, vt_ref, gkf_ref, gkb_ref, g_ref, kc_ref, vtc_ref, gkfc_ref, gkbc_ref,
                nw_ref, cmf_ref, cmb_ref, o_ref, stf_ref, stb_ref, of_ref, ob_ref):
    t = q_ref.shape[1]
    tc = kc_ref.shape[1]
    nsc, nscc = t // SUPER, tc // SUPER
    cmf = cmf_ref[...]
    cmb = cmb_ref[...]
    amask_f = cmf > 0
    amask_b = cmb > 0
    row = lax.broadcasted_iota(I32, (2 * LANES, HEAD_PAIR_DK), 0)
    lane = lax.broadcasted_iota(I32, (2 * LANES, HEAD_PAIR_DK), 1)
    bd_mask = (row < LANES) == (lane < HEAD_PAIR_DK // 2)

    def ctx_state(gk_ref, cm, fwd, j, st):
        rows = pl.ds(j * SUPER, SUPER)
        return _gla_super(None, kc_ref[0, rows, :].astype(F32), None, vtc_ref[0, j], gk_ref[0, rows, :],
                          cm, None, bd_mask, st, fwd, False)[1]

    def latent(gk_ref, cm, amask, fwd, j, st):
        rows = pl.ds(pl.multiple_of(j * SUPER, SUPER), SUPER)
        return _gla_super(q_ref[0, rows, :].astype(F32), k_ref[0, rows, :].astype(F32), v_ref[0, rows, :],
                          vt_ref[0, j], gk_ref[0, rows, :], cm, amask, bd_mask, st, fwd, True)

    st = jnp.zeros(stf_ref.shape, F32)
    for j in range(nscc):
        st = ctx_state(gkfc_ref, cmf, True, j, st)
    stf_ref[...] = st
    st = jnp.zeros(stb_ref.shape, F32)
    for j in reversed(range(nscc)):
        st = ctx_state(gkbc_ref, cmb, False, j, st)
    stb_ref[...] = st

    def scan_body(jj, carry):
        jb = nsc - 1 - jj
        of, stf = latent(gkf_ref, cmf, amask_f, True, jj, stf_ref[...])
        of_ref[pl.ds(pl.multiple_of(jj * SUPER, SUPER), SUPER), :] = of
        stf_ref[...] = stf
        ob, stb = latent(gkb_ref, cmb, amask_b, False, jb, stb_ref[...])
        ob_ref[pl.ds(pl.multiple_of(jb * SUPER, SUPER), SUPER), :] = ob
        stb_ref[...] = stb
        return carry

    lax.fori_loop(0, nsc, scan_body, 0, unroll=4)

    nw = nw_ref[...]

    def out_body(j, carry):
        rows = pl.ds(pl.multiple_of(j * SUPER, SUPER), SUPER)
        o = of_ref[rows, :] + ob_ref[rows, :]
        g = g_ref[0, rows, :].astype(F32)
        gate = g * jax.nn.sigmoid(g)
        for hh in range(2):
            oh = o[:, hh * LANES:(hh + 1) * LANES]
            on = oh * lax.rsqrt(jnp.mean(oh * oh, axis=-1, keepdims=True) + EPS) * nw
            o_ref[0, rows, hh * LANES:(hh + 1) * LANES] = (on * gate[:, hh * LANES:(hh + 1) * LANES]).astype(BF16)
        return carry

    lax.fori_loop(0, nsc, out_body, 0)


def _gla_masks():
    i = np.arange(SUPER)
    same = (i[:, None] // GLA_CHUNK) == (i[None, :] // GLA_CHUNK)
    fwd = same & (i[None, :] <= i[:, None])
    bwd = same & (i[None, :] >= i[:, None])
    return jnp.asarray(fwd, BF16), jnp.asarray(bwd, BF16)


def _gla_call(q, k, v, vt, gk, g, kc, vtc, gkc, nw):
    b, t, qk = q.shape
    tc = kc.shape[1]
    npair = qk // HEAD_PAIR_DK
    cmf, cmb = _gla_masks()
    lat = lambda width, off: pl.BlockSpec((1, t, width), lambda i, j: (i, 0, j + off))
    ctx = lambda width, off: pl.BlockSpec((1, tc, width), lambda i, j: (i, 0, j + off))
    tr = lambda n_groups: pl.BlockSpec((1, n_groups, 2 * LANES, SUPER), lambda i, j: (i, 0, j, 0))
    const = lambda shape: pl.BlockSpec(shape, lambda i, j: (0,) * len(shape))
    return pl.pallas_call(
        _gla_kernel,
        grid=(b, npair),
        in_specs=[lat(HEAD_PAIR_DK, 0), lat(HEAD_PAIR_DK, 0), lat(2 * LANES, 0), tr(t // SUPER),
                  lat(HEAD_PAIR_DK, 0), lat(HEAD_PAIR_DK, npair), lat(2 * LANES, 0),
                  ctx(HEAD_PAIR_DK, 0), tr(tc // SUPER), ctx(HEAD_PAIR_DK, 0), ctx(HEAD_PAIR_DK, npair),
                  const((1, LANES)), const(cmf.shape), const(cmb.shape)],
        out_specs=lat(2 * LANES, 0),
        out_shape=jax.ShapeDtypeStruct((b, t, v.shape[2]), BF16),
        scratch_shapes=[pltpu.VMEM((2 * LANES, HEAD_PAIR_DK), F32), pltpu.VMEM((2 * LANES, HEAD_PAIR_DK), F32),
                        pltpu.VMEM((t, 2 * LANES), F32), pltpu.VMEM((t, 2 * LANES), F32)],
        compiler_params=pltpu.CompilerParams(dimension_semantics=("arbitrary", "arbitrary")),
        name="gla",
    )(q, k, v, vt, gk, gk, g, kc, vtc, gkc, gkc, nw, cmf, cmb)


def _pool_kernel(x_ref, cm_ref, cnt_ref, wp_ref, ps_ref, o_ref, a_ref, b_ref):
    t = x_ref.shape[1]
    pad = POOL_PAD_GRID_ROWS * GRID_W
    total = t + 2 * pad
    for gi, w in enumerate(POOL_WINDOWS):
        lo = w // 2
        cols = slice(gi * LANES, (gi + 1) * LANES)
        cmat = cm_ref[gi]
        a_ref[0:pad, :] = jnp.zeros((pad, LANES), F32)
        a_ref[pad + t:total, :] = jnp.zeros((pad, LANES), F32)
        for blk in range(t // SUPER):
            rs = slice(blk * SUPER, (blk + 1) * SUPER)
            a_ref[pad + blk * SUPER:pad + (blk + 1) * SUPER, :] = _dot(cmat, x_ref[0, rs, cols])
        src, dst = a_ref, b_ref
        m = 1
        while m < w:
            sh = m * GRID_W
            dst[0:total - sh, :] = src[0:total - sh, :] + src[sh:total, :]
            src, dst = dst, src
            m *= 2
        first = pad - lo * GRID_W
        pooled = src[first:first + t, :] / cnt_ref[gi] - x_ref[0, :, cols].astype(F32)
        yp = _dot(pooled.astype(BF16), wp_ref[gi]) * ps_ref[:, cols]
        o_ref[0, :, cols] = yp.astype(BF16)


def _pool_col_mats():
    i = np.arange(SUPER)
    same_row = (i[:, None] // GRID_W) == (i[None, :] // GRID_W)
    d = i[None, :] - i[:, None]
    mats = []
    for w in POOL_WINDOWS:
        lo = w // 2
        hi = w - 1 - lo
        mats.append(same_row & (d >= -lo) & (d <= hi))
    return jnp.asarray(np.stack(mats), BF16)


def _pool_counts(t):
    rows = t // GRID_W
    r = np.arange(t) // GRID_W
    c = np.arange(t) % GRID_W
    out = []
    for w in POOL_WINDOWS:
        lo = w // 2
        hi = w - 1 - lo
        cnt_r = np.minimum(r + hi + 1, rows) - np.maximum(r - lo, 0)
        cnt_c = np.minimum(c + hi + 1, GRID_W) - np.maximum(c - lo, 0)
        out.append(np.broadcast_to((cnt_r * cnt_c).astype(np.float32)[:, None], (t, LANES)))
    return jnp.asarray(np.stack(out))


def _pool_call(xp, w_pool, pool_scale):
    b, t, pw = xp.shape
    ng = len(POOL_WINDOWS)
    assert max(POOL_WINDOWS) // 2 <= POOL_PAD_GRID_ROWS and t % GRID_W == 0
    cm = _pool_col_mats()
    cnt = _pool_counts(t)
    staged = t + 2 * POOL_PAD_GRID_ROWS * GRID_W
    const = lambda shape: pl.BlockSpec(shape, lambda i: (0,) * len(shape))
    return pl.pallas_call(
        _pool_kernel,
        grid=(b,),
        in_specs=[pl.BlockSpec((1, t, pw), lambda i: (i, 0, 0)),
                  const(cm.shape), const(cnt.shape), const((ng, LANES, LANES)), const((1, pw))],
        out_specs=pl.BlockSpec((1, t, pw), lambda i: (i, 0, 0)),
        out_shape=jax.ShapeDtypeStruct((b, t, pw), BF16),
        scratch_shapes=[pltpu.VMEM((staged, LANES), F32), pltpu.VMEM((staged, LANES), F32)],
        compiler_params=pltpu.CompilerParams(dimension_semantics=("arbitrary",)),
        name="pool",
    )(xp, cm, cnt, w_pool, pool_scale)


def _route_kernel(gla_ref, pool_ref, x_ref, mod_ref, wo_ref, nw_ref, wr_ref, br_ref, lt_ref,
                  h_ref, xt_ref, code_ref, wt_ref, cnt_ref, run_ref, wr2_ref, *, gw, n_exp):
    i = pl.program_id(0)

    @pl.when(i == 0)
    def _():
        run_ref[...] = jnp.zeros_like(run_ref)
        wh, wl = _split_bf16(wr_ref[...])
        wr2_ref[:, :LANES] = wh
        wr2_ref[:, LANES:] = wl

    m = mod_ref[0]
    acc = _dot(gla_ref[...], wo_ref[0:gw, :]) + _dot(pool_ref[...], wo_ref[gw:, :])
    h = x_ref[...] + m[2:3] * acc
    h_ref[...] = h
    xt = _rmsnorm(h, nw_ref[...]) * (1.0 + m[4:5]) + m[3:4]
    xt_ref[...] = _pack_bf16_pairs(xt)
    xh, xl = _split_bf16(xt)
    wr2 = wr2_ref[...]
    t1 = _dot(xh, wr2)
    logits = t1[:, :LANES] + t1[:, LANES:] + _dot(xl, wr2[:, :LANES]) + br_ref[...]
    lane = lax.broadcasted_iota(I32, logits.shape, 1)
    neg = jnp.float32(-jnp.inf)
    logits = jnp.where(lane < n_exp, logits, neg)
    vals, hots = [], []
    e_out = jnp.zeros(logits.shape, I32)
    lane_f = lane.astype(F32)
    for j in range(TOP_K):
        mx = jnp.max(logits, axis=-1, keepdims=True)
        idx = jnp.min(jnp.where(logits == mx, lane_f, float(LANES)), axis=-1, keepdims=True)
        hot = lane_f == idx
        vals.append(mx)
        hots.append(hot)
        e_out = jnp.where(lane == j, idx.astype(I32), e_out)
        logits = jnp.where(hot, neg, logits)
    ex = [jnp.exp(v - vals[0]) for v in vals]
    den = ex[0] + ex[1] + ex[2] + ex[3]
    w_out = jnp.zeros(logits.shape, F32)
    for j in range(TOP_K):
        w_out = jnp.where(lane == j, ex[j] / den, w_out)
    osum = jnp.where(hots[0] | hots[1] | hots[2] | hots[3], 1.0, 0.0)
    before = _dot(lt_ref[...], osum.astype(BF16)) + run_ref[0:1, :]
    rk_out = jnp.zeros(logits.shape, I32)
    for j in range(TOP_K):
        rj = jnp.sum(jnp.where(hots[j], before, 0.0), axis=-1, keepdims=True)
        rk_out = jnp.where(lane == j, rj.astype(I32), rk_out)
    run = run_ref[0:1, :] + jnp.sum(osum, axis=0, keepdims=True)
    run_ref[...] = jnp.broadcast_to(run, run_ref.shape)
    code = e_out * RANK_LIMIT + rk_out
    code_ref[...] = code.T[:code_ref.shape[0], :]
    wt_ref[...] = w_out
    cnt_ref[...] = jnp.broadcast_to(run, cnt_ref.shape).astype(I32)


def _route_call(gla, pool, x, mod, w_out, nw, wr, br, *, tm, n_exp, first_tile, n_tiles):
    n_all, d = x.shape
    n = n_tiles * tm
    gw = gla.shape[1]
    t_per_b = n_all // mod.shape[0]
    lt = jnp.asarray(np.tril(np.ones((tm, tm), np.float32), -1), BF16)
    row_in = lambda width: pl.BlockSpec((tm, width), lambda i: (i + first_tile, 0))
    row = lambda width: pl.BlockSpec((tm, width), lambda i: (i, 0))
    const = lambda shape: pl.BlockSpec(shape, lambda i: (0,) * len(shape))
    sds = jax.ShapeDtypeStruct
    return pl.pallas_call(
        functools.partial(_route_kernel, gw=gw, n_exp=n_exp),
        grid=(n_tiles,),
        in_specs=[row_in(gw), row_in(pool.shape[1]), row_in(d),
                  pl.BlockSpec((1, N_MOD, d), lambda i: ((i + first_tile) * tm // t_per_b, 0, 0)),
                  const(w_out.shape), const((1, d)), const(wr.shape), const((1, LANES)), const((tm, tm))],
        out_specs=[row(d), row(d // 2), pl.BlockSpec((8, tm), lambda i: (0, i)), row(LANES), const((8, LANES))],
        out_shape=[sds((n, d), F32), sds((n, d // 2), U32), sds((8, n), I32),
                   sds((n, LANES), F32), sds((8, LANES), I32)],
        scratch_shapes=[pltpu.VMEM((8, LANES), F32), pltpu.VMEM((d, 2 * LANES), BF16)],
        compiler_params=pltpu.CompilerParams(dimension_semantics=("arbitrary",)),
        name="route",
    )(gla, pool, x, mod, w_out, nw, wr, br, lt)


def _plan_kernel(cnt_ref, code_ref, dest_ref, be_ref, nv_ref, first_ref, next_ref, slot_ref, nu_ref, start_ref,
                 *, n_exp, rows):
    @pl.when(pl.program_id(0) == 0)
    def _():
        blk = (lax.broadcasted_iota(I32, be_ref.shape, 0) * LANES + lax.broadcasted_iota(I32, be_ref.shape, 1))
        blk_row0 = blk * rows
        nxt_e = [None] * n_exp
        nxt = jnp.int32(-1)
        for e in reversed(range(n_exp)):
            nxt_e[e] = nxt
            nxt = jnp.where(cnt_ref[e] > 0, e, nxt)
        zeros = jnp.zeros(be_ref.shape, I32)
        be, end_valid, first, nxt_blk, slot = zeros, zeros, zeros, zeros - 1, zeros
        acc = jnp.int32(0)
        last_e = jnp.int32(0)
        ordinal = jnp.int32(0)
        for e in range(n_exp):
            c = cnt_ref[e]
            start_ref[e] = acc
            in_e = (blk_row0 >= acc) & (c > 0)
            end_valid = jnp.where(in_e, acc + c, end_valid)
            be = jnp.where(in_e, e, be)
            first = jnp.where(in_e, (blk_row0 == acc).astype(I32), first)
            nxt_blk = jnp.where(in_e, nxt_e[e], nxt_blk)
            slot = jnp.where(in_e, ordinal % 2, slot)
            acc = acc + (c + rows - 1) // rows * rows
            last_e = jnp.where(c > 0, e, last_e)
            ordinal = ordinal + (c > 0).astype(I32)
        n_used = acc // rows
        nu_ref[0] = n_used
        be_ref[...] = jnp.where(blk < n_used, be, last_e)
        nv_ref[...] = jnp.clip(end_valid - blk_row0, 0, rows)
        first_ref[...] = first
        next_ref[...] = nxt_blk
        slot_ref[...] = slot

    code = code_ref[...]
    e_vec = code // RANK_LIMIT
    dest = code % RANK_LIMIT
    for e in range(n_exp):
        dest = dest + jnp.where(e_vec == e, start_ref[e], 0)
    dest_ref[...] = dest


def _plan_call(counts, code_t, *, n_exp, rows, chunk):
    n = code_t.shape[1]
    sds = jax.ShapeDtypeStruct
    smem = pltpu.SMEM
    return pl.pallas_call(
        functools.partial(_plan_kernel, n_exp=n_exp, rows=rows),
        grid=(n // chunk,),
        in_specs=[pl.BlockSpec(memory_space=smem), pl.BlockSpec((8, chunk), lambda i: (0, i))],
        out_specs=[pl.BlockSpec((8, chunk), lambda i: (0, i))]
        + [pl.BlockSpec((8, LANES), lambda i: (0, 0))] * 5 + [pl.BlockSpec(memory_space=smem)],
        out_shape=[sds((8, n), I32)] + [sds((8, LANES), I32)] * 5 + [sds((1,), I32)],
        scratch_shapes=[pltpu.SMEM((n_exp,), I32)],
        compiler_params=pltpu.CompilerParams(dimension_semantics=("arbitrary",)),
        name="plan",
    )(counts, code_t)


def _sc_worker_id():
    return lax.axis_index("s") * SC_CORES + lax.axis_index("c")


def _sc_scatter_call(x, idx3, *, n_out):
    n, d = x.shape
    n_win_total, k, w = idx3.shape
    n_win = n_win_total // SC_WORKERS
    mesh = plsc.VectorSubcoreMesh(core_axis_name="c", subcore_axis_name="s")

    @functools.partial(
        pl.kernel, mesh=mesh,
        out_type=jax.ShapeDtypeStruct((n_out, d), x.dtype),
        scratch_types=[pltpu.VMEM((k, w), I32), pltpu.VMEM((w, d), x.dtype), pltpu.SemaphoreType.DMA],
        name="sc_dispatch",
    )
    def kern(x_hbm, idx_hbm, out_hbm, idx_v, rows_v, sem):
        wid = _sc_worker_id()

        @pl.loop(0, n_win)
        def _(i):
            win = wid * n_win + i
            pltpu.sync_copy(idx_hbm.at[win], idx_v)
            pltpu.sync_copy(x_hbm.at[pl.ds(win * w, w)], rows_v)
            for j in range(k):
                pltpu.async_copy(rows_v, out_hbm.at[idx_v.at[j]], sem).wait()

    return kern(x, idx3)


def _sc_gather_call(table, idx3):
    n_workers, n_win, w = idx3.shape
    d = table.shape[1]
    assert n_workers == SC_WORKERS and n_win % 2 == 0
    mesh = plsc.VectorSubcoreMesh(core_axis_name="c", subcore_axis_name="s")

    @functools.partial(
        pl.kernel, mesh=mesh,
        out_type=jax.ShapeDtypeStruct((n_workers * n_win * w, d), table.dtype),
        scratch_types=[pltpu.VMEM((n_win, w), I32), pltpu.VMEM((2, w, d), table.dtype),
                       pltpu.SemaphoreType.DMA((2,)), pltpu.SemaphoreType.DMA((2,))],
        name="sc_gather",
    )
    def kern(table_hbm, idx_hbm, out_hbm, idx_v, rows_v, gsem, osem):
        wid = _sc_worker_id()
        base = wid * n_win
        pltpu.sync_copy(idx_hbm.at[wid], idx_v)

        def gather(wi, b):
            return pltpu.make_async_copy(table_hbm.at[idx_v.at[wi]], rows_v.at[b], gsem.at[b])

        def put(wi, b):
            return pltpu.make_async_copy(rows_v.at[b], out_hbm.at[pl.ds((base + wi) * w, w)], osem.at[b])

        gather(0, 0).start()

        @pl.loop(0, n_win, step=2)
        def _(i):
            for b in range(2):
                wi = i + b

                @pl.when(wi + 1 < n_win)
                def _():
                    @pl.when(wi >= 1)
                    def _():
                        put(wi - 1, 1 - b).wait()
                    gather(wi + 1, 1 - b).start()

                gather(wi, b).wait()
                put(wi, b).start()

        put(n_win - 2, 0).wait()
        put(n_win - 1, 1).wait()

    return kern(table, idx3)


def _expert_kernel(be_ref, nu_ref, nv_ref, first_ref, next_ref, slot_ref,
                   x_ref, wgu_hbm, bgu_ref, wd_hbm, bd_ref, y_ref,
                   wgu_f32_ref, wd_f32_ref, wgu_bf_ref, wd_bf_ref, sem, *, d_ff):
    i = pl.program_id(0)
    slot = slot_ref[i]

    def fetch(e, s):
        return (pltpu.make_async_copy(wgu_hbm.at[e], wgu_f32_ref.at[s], sem.at[0, s]),
                pltpu.make_async_copy(wd_hbm.at[e], wd_f32_ref.at[s], sem.at[1, s]))

    @pl.when(i == 0)
    def _():
        for cp in fetch(be_ref[0], slot):
            cp.start()

    @pl.when(first_ref[i] == 1)
    def _():
        for cp in fetch(be_ref[i], slot):
            cp.wait()

        @pl.when(next_ref[i] >= 0)
        def _():
            for cp in fetch(next_ref[i], 1 - slot):
                cp.start()

        wgu_bf_ref[...] = wgu_f32_ref[slot].astype(BF16)
        wd_bf_ref[...] = wd_f32_ref[slot].astype(BF16)

    n_valid = nv_ref[i]
    n_rows = x_ref.shape[0]

    def expert_rows(rows):
        row = lax.broadcasted_iota(I32, (rows, x_ref.shape[1]), 0)
        lo, hi = _unpack_bf16_pairs(jnp.where(row < n_valid, x_ref[0:rows, :], jnp.uint32(0)))
        xb = jnp.concatenate([lo, hi], axis=1).astype(BF16)
        col = lax.broadcasted_iota(I32, bgu_ref.shape[1:], 1)
        gu = _dot(xb, wgu_bf_ref[...]) + (bgu_ref[0] + jnp.where(col >= d_ff, 1.0, 0.0))
        gate = jnp.minimum(gu[:, :d_ff], SWIGLU_LIMIT)
        up1 = jnp.clip(gu[:, d_ff:], 1.0 - SWIGLU_LIMIT, 1.0 + SWIGLU_LIMIT)
        act = up1 * (0.5 * gate) * (1.0 + jnp.tanh((0.5 * SWIGLU_ALPHA) * gate))
        y_ref[0:rows, :] = _pack_bf16_pairs(_dot(act.astype(BF16), wd_bf_ref[...]) + bd_ref[0])

    in_use = i < nu_ref[0]

    for rows in range(EXPERT_ROW_STEP, n_rows + 1, EXPERT_ROW_STEP):
        @pl.when(in_use & (n_valid > rows - EXPERT_ROW_STEP) & (n_valid <= rows))
        def _(rows=rows):
            expert_rows(rows)
            if rows < n_rows:
                y_ref[rows:, :] = jnp.zeros((n_rows - rows, y_ref.shape[1]), y_ref.dtype)


def _expert_call(plan, xs, w_gu, b_gu, w_down, b_down):
    block_e, n_valid, first, nxt, slot, n_used = plan
    n_pad = xs.shape[0]
    n_exp, d, two_ff = w_gu.shape
    d_ff = two_ff // 2
    nblk = n_pad // EXPERT_ROWS
    rows = lambda i, be, nu, *_: (jnp.minimum(i, nu[0] - 1), 0)
    per_e = lambda i, be, *_: (be[i], 0, 0)
    grid_spec = pltpu.PrefetchScalarGridSpec(
        num_scalar_prefetch=6,
        grid=(nblk,),
        in_specs=[pl.BlockSpec((EXPERT_ROWS, d // 2), rows),
                  pl.BlockSpec(memory_space=pl.ANY), pl.BlockSpec((1, 1, two_ff), per_e),
                  pl.BlockSpec(memory_space=pl.ANY), pl.BlockSpec((1, 1, d), per_e)],
        out_specs=pl.BlockSpec((EXPERT_ROWS, d // 2), rows),
        scratch_shapes=[pltpu.VMEM((2, d, two_ff), F32), pltpu.VMEM((2, d_ff, d), F32),
                        pltpu.VMEM((d, two_ff), BF16), pltpu.VMEM((d_ff, d), BF16),
                        pltpu.SemaphoreType.DMA((2, 2))],
    )
    flat = lambda a: a.reshape(-1)
    return pl.pallas_call(
        functools.partial(_expert_kernel, d_ff=d_ff),
        grid_spec=grid_spec,
        out_shape=jax.ShapeDtypeStruct((n_pad, d // 2), U32),
        compiler_params=pltpu.CompilerParams(dimension_semantics=("arbitrary",),
                                             vmem_limit_bytes=EXPERT_VMEM_BYTES),
        name="experts",
    )(flat(block_e), n_used, flat(n_valid), flat(first), flat(nxt), flat(slot),
      xs, w_gu, b_gu.reshape(n_exp, 1, two_ff), w_down, b_down.reshape(n_exp, 1, d))


def _combine_kernel(y4_ref, wt_ref, h_ref, mod_ref, fw_ref, o_ref):
    wt = wt_ref[...]
    acc_lo, acc_hi = None, None
    for j in range(TOP_K):
        lo, hi = _unpack_bf16_pairs(y4_ref[j])
        w = wt[:, j:j + 1]
        acc_lo = w * lo if j == 0 else acc_lo + w * lo
        acc_hi = w * hi if j == 0 else acc_hi + w * hi
    acc = jnp.concatenate([acc_lo, acc_hi], axis=1)
    m = mod_ref[0]
    o_ref[...] = _rmsnorm(h_ref[...] + m[5:6] * acc, fw_ref[...])


def _combine_call(y4, wts, h, mod, fw, prev_out, *, n, tg, first_tile):
    d = h.shape[1]
    t_per_b = n // mod.shape[0]
    part = lambda width: pl.BlockSpec((tg, width), lambda i: (i, 0))
    row = lambda width: pl.BlockSpec((tg, width), lambda i: (i + first_tile, 0))
    in_specs = [pl.BlockSpec((TOP_K, tg, d // 2), lambda i: (0, i, 0)),
                part(LANES), part(d),
                pl.BlockSpec((1, N_MOD, d), lambda i: ((i + first_tile) * tg // t_per_b, 0, 0)),
                pl.BlockSpec((1, d), lambda i: (0, 0))]
    args = [y4, wts, h, mod, fw]
    kern = _combine_kernel
    aliases = {}
    if prev_out is not None:
        in_specs.append(pl.BlockSpec(memory_space=pl.ANY))
        args.append(prev_out)
        kern = lambda y4_ref, wt_ref, h_ref, mod_ref, fw_ref, prev_ref, o_ref: _combine_kernel(
            y4_ref, wt_ref, h_ref, mod_ref, fw_ref, o_ref)
        aliases = {len(args) - 1: 0}
    return pl.pallas_call(
        kern,
        grid=(y4.shape[1] // tg,),
        in_specs=in_specs,
        out_specs=row(d),
        out_shape=jax.ShapeDtypeStruct((n, d), F32),
        input_output_aliases=aliases,
        compiler_params=pltpu.CompilerParams(dimension_semantics=("arbitrary",)),
        name="combine",
    )(*args)


def kernel(x, c, ctx, c_ctx, w_ada, b_ada, norm_mix_w, norm_mlp_w, w_in, w_gk_f, b_gk_f, w_gk_b, b_gk_b,
           gla_norm_w, w_pool, pool_scale, w_out, w_router, b_router, w_gu, b_gu, w_down, b_down,
           final_norm_w):
    b, t, d = x.shape
    assert w_ada.shape[0] == 1, "single-layer trunk"
    n_exp = w_router.shape[2]
    rank = w_gk_f.shape[1]
    qk = w_gk_f.shape[2]
    dk = qk // GLA_HEADS
    gw = GLA_HEADS * gla_norm_w.shape[1]
    pw = w_pool.shape[1] * w_pool.shape[2]
    assert w_in.shape[2] == 2 * qk + 2 * gw + 2 * rank + pw and 2 * rank <= LANES
    assert t % SUPER == 0 and ctx.shape[1] % SUPER == 0 and n_exp <= LANES

    rows = -(-(b + 1) // 8) * 8
    cc = jnp.concatenate([c, c_ctx[None, :], jnp.zeros((rows - b - 1, d), F32)], axis=0)
    mod = _mod_call(cc, w_ada[0], b_ada)
    mod_x = mod[:b].reshape(b, N_MOD, d)
    mod_c = mod[b:b + 1].reshape(1, N_MOD, d)

    wi = w_in[0]
    o_r = 2 * qk + 2 * gw
    w_cat = jnp.concatenate([wi[:, :o_r], wi[:, o_r + 2 * rank:], wi[:, o_r:o_r + 2 * rank],
                             jnp.zeros((d, LANES - 2 * rank), F32)], axis=1).astype(BF16)
    wgk = jnp.concatenate([jnp.pad(w_gk_f[0], ((0, 0), (0, qk))), jnp.pad(w_gk_b[0], ((0, 0), (qk, 0))),
                           jnp.zeros((LANES - 2 * rank, 2 * qk), F32)], axis=0)
    bgk = jnp.concatenate([b_gk_f[0], b_gk_b[0]])[None, :]
    proj = functools.partial(_inproj_call, nw=norm_mix_w, w=w_cat, wgk=wgk.astype(BF16), bgk=bgk,
                             qk=qk, gw=gw, pw=pw, dk=dk)
    q, k, v, vt, g, xp, gk = proj(x, mod_x, tm=1024)
    _, kc, _, vtc, _, _, gkc = proj(ctx, mod_c, tm=SUPER)

    gla = _gla_call(q, k, v, vt, gk, g, kc, vtc, gkc, gla_norm_w)
    pool = _pool_call(xp, w_pool[0].astype(BF16), pool_scale)

    n = b * t
    wr = jnp.pad(w_router[0], ((0, 0), (0, LANES - n_exp)))
    br = jnp.pad(b_router, ((0, 0), (0, LANES - n_exp)))
    n_part = n // MOE_PARTS
    tm, tg = 512, 1024
    assert n % MOE_PARTS == 0 and n_part % (SC_WORKERS * SC_WINDOW) == 0 and n_part % tm == 0 and n_part < RANK_LIMIT
    assert (n_part * TOP_K) % (SC_WORKERS * 2 * SC_GATHER_WINDOW) == 0
    n_pad = n_part * TOP_K + n_exp * EXPERT_ROWS
    nblk = n_pad // EXPERT_ROWS
    assert nblk <= 8 * LANES
    parts = range(MOE_PARTS)
    routed = [_route_call(gla.reshape(n, gw), pool.reshape(n, pw), x.reshape(n, d), mod_x, w_out[0].astype(BF16),
                          norm_mlp_w, wr, br, tm=tm, n_exp=n_exp,
                          first_tile=p * n_part // tm, n_tiles=n_part // tm) for p in parts]
    plans = [_plan_call(cnt[0, :n_exp], code_t, n_exp=n_exp, rows=EXPERT_ROWS, chunk=min(n_part, 4096))
             for (_, _, code_t, _, cnt) in routed]
    dests = [plan[0][:TOP_K] for plan in plans]
    xs = [_sc_scatter_call(routed[p][1],
                           dests[p].reshape(TOP_K, n_part // SC_WINDOW, SC_WINDOW).transpose(1, 0, 2), n_out=n_pad)
          for p in parts]
    ys = [_expert_call(plans[p][1:], xs[p], w_gu[0], b_gu[0], w_down[0], b_down[0]) for p in parts]
    y4 = [_sc_gather_call(ys[p], dests[p].reshape(SC_WORKERS, -1, SC_GATHER_WINDOW)).reshape(TOP_K, n_part, d // 2)
          for p in parts]
    out = None
    for p in parts:
        out = _combine_call(y4[p], routed[p][3], routed[p][0], mod_x, final_norm_w[None, :], out,
                            n=n, tg=tg, first_tile=p * n_part // tg)
    return out.reshape(b, t, d)
```

```python
import functools

import numpy as np
import jax
import jax.numpy as jnp
from jax import lax
from jax.experimental import pallas as pl
from jax.experimental.pallas import tpu as pltpu
from jax.experimental.pallas import tpu_sc as plsc

F32 = jnp.float32
BF16 = jnp.bfloat16
I32 = jnp.int32
U32 = jnp.uint32

GRID_W = 64
GLA_HEADS = 4
GLA_CHUNK = 64
GATE_NORMALIZER = 16.0
POOL_WINDOWS = (2, 4, 8, 16)
POOL_PAD_GRID_ROWS = 8
TOP_K = 4
RANK_LIMIT = 1 << 20
SWIGLU_LIMIT = 7.0
SWIGLU_ALPHA = 1.702
N_MOD = 6
EPS = 1e-6

LANES = 128
SUPER = 256
HEAD_PAIR_DK = 128
EXPERT_ROWS = 512
EXPERT_ROW_STEP = 128
MOE_PARTS = 2
EXPERT_VMEM_BYTES = 56 * 1024 * 1024
SC_CORES = 2
SC_SUBCORES = 16
SC_WORKERS = SC_CORES * SC_SUBCORES
SC_WINDOW = 32
SC_GATHER_WINDOW = 64


def _dot(a, b):
    return jnp.dot(a, b, preferred_element_type=F32)


def _dot_nt(a, b):
    return lax.dot_general(a, b, (((1,), (1,)), ((), ())), preferred_element_type=F32)


def _split_bf16(x):
    hi = x.astype(BF16)
    lo = (x - hi.astype(F32)).astype(BF16)
    return hi, lo


def _pack_bf16_pairs(x):
    c = x.shape[1] // 2
    lo = lax.bitcast_convert_type(x[:, :c].astype(BF16).astype(F32), U32)
    hi = lax.bitcast_convert_type(x[:, c:].astype(BF16).astype(F32), U32)
    return (lo >> 16) | hi


def _unpack_bf16_pairs(p):
    lo = lax.bitcast_convert_type(p << 16, F32)
    hi = lax.bitcast_convert_type(p & jnp.uint32(0xFFFF0000), F32)
    return lo, hi


def _rmsnorm(x, w):
    var = jnp.mean(x * x, axis=-1, keepdims=True)
    return x * lax.rsqrt(var + EPS) * w


def _mod_kernel(c_ref, w_ref, b_ref, o_ref):
    c = c_ref[...]
    s = c * jax.nn.sigmoid(c)
    o_ref[...] = jnp.dot(s, w_ref[...], precision=lax.Precision.HIGHEST,
                         preferred_element_type=F32) + b_ref[...]


def _mod_call(cc, w_ada, b_ada):
    rows, d = cc.shape
    n = w_ada.shape[1]
    tn = 1024
    return pl.pallas_call(
        _mod_kernel,
        grid=(n // tn,),
        in_specs=[pl.BlockSpec((rows, d), lambda j: (0, 0)),
                  pl.BlockSpec((d, tn), lambda j: (0, j)),
                  pl.BlockSpec((1, tn), lambda j: (0, j))],
        out_specs=pl.BlockSpec((rows, tn), lambda j: (0, j)),
        out_shape=jax.ShapeDtypeStruct((rows, n), F32),
        name="mod",
    )(cc, w_ada, b_ada)


def _inproj_kernel(x_ref, mod_ref, nw_ref, w_ref, wgk_ref, bgk_ref,
                   q_ref, k_ref, v_ref, vt_ref, g_ref, p_ref, gk_ref, *, qk, gw, pw, dk):
    x = x_ref[0]
    m = mod_ref[0]
    hm = (_rmsnorm(x, nw_ref[...]) * (1.0 + m[1:2]) + m[0:1]).astype(BF16)
    p = _dot(hm, w_ref[...])
    vt = p[:, 2 * qk:2 * qk + gw].T
    for s in range(vt_ref.shape[1]):
        vt_ref[0, s] = vt[:, s * SUPER:(s + 1) * SUPER].astype(BF16)
    o = 0
    q_ref[0] = (p[:, o:o + qk] * (dk ** -0.5)).astype(BF16); o += qk
    k_ref[0] = p[:, o:o + qk].astype(BF16); o += qk
    v_ref[0] = p[:, o:o + gw].astype(BF16); o += gw
    g_ref[0] = p[:, o:o + gw].astype(BF16); o += gw
    p_ref[0] = p[:, o:o + pw].astype(BF16); o += pw
    r = p[:, o:o + LANES]
    z = _dot(r.astype(BF16), wgk_ref[...]) + bgk_ref[...]
    gk_ref[0] = (jnp.minimum(z, 0.0) - jnp.log1p(jnp.exp(-jnp.abs(z)))) * (1.0 / GATE_NORMALIZER)


def _inproj_call(x, mod, nw, w, wgk, bgk, *, qk, gw, pw, dk, tm):
    b, t, d = x.shape
    n_in = w.shape[1]
    bs = lambda width: pl.BlockSpec((1, tm, width), lambda i, j: (i, j, 0))
    const = lambda shape: pl.BlockSpec(shape, lambda i, j: (0,) * len(shape))
    per_batch = mod.shape[0] > 1
    sds = jax.ShapeDtypeStruct
    return pl.pallas_call(
        functools.partial(_inproj_kernel, qk=qk, gw=gw, pw=pw, dk=dk),
        grid=(b, t // tm),
        in_specs=[bs(d),
                  pl.BlockSpec((1, N_MOD, d), (lambda i, j: (i, 0, 0)) if per_batch else (lambda i, j: (0, 0, 0))),
                  const((1, d)), const((d, n_in)), const((LANES, 2 * qk)), const((1, 2 * qk))],
        out_specs=[bs(qk), bs(qk), bs(gw),
                   pl.BlockSpec((1, tm // SUPER, gw, SUPER), lambda i, j: (i, j, 0, 0)),
                   bs(gw), bs(pw), bs(2 * qk)],
        out_shape=[sds((b, t, qk), BF16), sds((b, t, qk), BF16), sds((b, t, gw), BF16),
                   sds((b, t // SUPER, gw, SUPER), BF16),
                   sds((b, t, gw), BF16), sds((b, t, pw), BF16), sds((b, t, 2 * qk), F32)],
        compiler_params=pltpu.CompilerParams(dimension_semantics=("arbitrary", "arbitrary")),
        name="inproj",
    )(x, mod, nw, w, wgk, bgk)


def _gla_super(q, k, v, vt, gk, cm, amask, bd_mask, st, fwd, want_out):
    nch = SUPER // GLA_CHUNK
    order = tuple(range(nch)) if fwd else tuple(reversed(range(nch)))
    last_row = GLA_CHUNK - 1 if fwd else 0
    mid_row = GLA_CHUNK // 2 - 1 if fwd else GLA_CHUNK // 2
    hi, lo = _split_bf16(gk)
    bcum = _dot(cm, hi) + _dot(cm, lo)

    def chunk_row(r):
        return jnp.concatenate(
            [jnp.broadcast_to(bcum[c * GLA_CHUNK + r:c * GLA_CHUNK + r + 1, :], (GLA_CHUNK, bcum.shape[1]))
             for c in range(nch)], axis=0)

    chunk_of_row = lax.broadcasted_iota(I32, bcum.shape, 0) // GLA_CHUNK

    def by_chunk(x):
        return jnp.concatenate([jnp.where(chunk_of_row == c, x, 0.0).astype(BF16) for c in range(nch)], axis=1)

    blast = chunk_row(last_row)
    u_all = _dot(vt, by_chunk(k * jnp.exp(blast - bcum)))
    before = [None] * nch
    for c in order:
        before[c] = st
        decay = jnp.exp(bcum[c * GLA_CHUNK + last_row:c * GLA_CHUNK + last_row + 1, :])
        st = st * decay + jnp.where(bd_mask, u_all[:, c * HEAD_PAIR_DK:(c + 1) * HEAD_PAIR_DK], 0.0)
    if not want_out:
        return None, st
    bmid = chunk_row(mid_row)
    qt = q * jnp.exp(bcum - bmid)
    kt = (k * jnp.exp(bmid - bcum)).astype(BF16)
    lane = lax.broadcasted_iota(I32, qt.shape, 1)
    half = HEAD_PAIR_DK // 2
    o_heads = []
    for hh in range(2):
        sel = (lane < half) if hh == 0 else (lane >= half)
        a = _dot_nt(jnp.where(sel, qt, 0.0).astype(BF16), kt)
        a = jnp.where(amask, a, 0.0).astype(BF16)
        o_heads.append(_dot(a, v[:, hh * LANES:(hh + 1) * LANES]))
    qh = (q * jnp.exp(bcum)).astype(BF16)
    o_inter = jnp.concatenate(
        [_dot_nt(qh[c * GLA_CHUNK:(c + 1) * GLA_CHUNK], before[c].astype(BF16)) for c in range(nch)], axis=0)
    return jnp.concatenate(o_heads, axis=1) + o_inter, st


def _gla_kernel(q_ref, k_ref, v_ref, vt_ref, gkf_ref, gkb_ref, g_ref, kc_ref, vtc_ref, gkfc_ref, gkbc_ref,
                nw_ref, cmf_ref, cmb_ref, o_ref, stf_ref, stb_ref, of_ref, ob_ref):
    t = q_ref.shape[1]
    tc = kc_ref.shape[1]
    nsc, nscc = t // SUPER, tc // SUPER
    cmf = cmf_ref[...]
    cmb = cmb_ref[...]
    amask_f = cmf > 0
    amask_b = cmb > 0
    row = lax.broadcasted_iota(I32, (2 * LANES, HEAD_PAIR_DK), 0)
    lane = lax.broadcasted_iota(I32, (2 * LANES, HEAD_PAIR_DK), 1)
    bd_mask = (row < LANES) == (lane < HEAD_PAIR_DK // 2)

    def ctx_state(gk_ref, cm, fwd, j, st):
        rows = pl.ds(j * SUPER, SUPER)
        return _gla_super(None, kc_ref[0, rows, :].astype(F32), None, vtc_ref[0, j], gk_ref[0, rows, :],
                          cm, None, bd_mask, st, fwd, False)[1]

    def latent(gk_ref, cm, amask, fwd, j, st):
        rows = pl.ds(pl.multiple_of(j * SUPER, SUPER), SUPER)
        return _gla_super(q_ref[0, rows, :].astype(F32), k_ref[0, rows, :].astype(F32), v_ref[0, rows, :],
                          vt_ref[0, j], gk_ref[0, rows, :], cm, amask, bd_mask, st, fwd, True)

    st = jnp.zeros(stf_ref.shape, F32)
    for j in range(nscc):
        st = ctx_state(gkfc_ref, cmf, True, j, st)
    stf_ref[...] = st
    st = jnp.zeros(stb_ref.shape, F32)
    for j in reversed(range(nscc)):
        st = ctx_state(gkbc_ref, cmb, False, j, st)
    stb_ref[...] = st

    def scan_body(jj, carry):
        jb = nsc - 1 - jj
        of, stf = latent(gkf_ref, cmf, amask_f, True, jj, stf_ref[...])
        of_ref[pl.ds(pl.multiple_of(jj * SUPER, SUPER), SUPER), :] = of
        stf_ref[...] = stf
        ob, stb = latent(gkb_ref, cmb, amask_b, False, jb, stb_ref[...])
        ob_ref[pl.ds(pl.multiple_of(jb * SUPER, SUPER), SUPER), :] = ob
        stb_ref[...] = stb
        return carry

    lax.fori_loop(0, nsc, scan_body, 0, unroll=4)

    nw = nw_ref[...]

    def out_body(j, carry):
        rows = pl.ds(pl.multiple_of(j * SUPER, SUPER), SUPER)
        o = of_ref[rows, :] + ob_ref[rows, :]
        g = g_ref[0, rows, :].astype(F32)
        gate = g * jax.nn.sigmoid(g)
        for hh in range(2):
            oh = o[:, hh * LANES:(hh + 1) * LANES]
            on = oh * lax.rsqrt(jnp.mean(oh * oh, axis=-1, keepdims=True) + EPS) * nw
            o_ref[0, rows, hh * LANES:(hh + 1) * LANES] = (on * gate[:, hh * LANES:(hh + 1) * LANES]).astype(BF16)
        return carry

    lax.fori_loop(0, nsc, out_body, 0)


def _gla_masks():
    i = np.arange(SUPER)
    same = (i[:, None] // GLA_CHUNK) == (i[None, :] // GLA_CHUNK)
    fwd = same & (i[None, :] <= i[:, None])
    bwd = same & (i[None, :] >= i[:, None])
    return jnp.asarray(fwd, BF16), jnp.asarray(bwd, BF16)


def _gla_call(q, k, v, vt, gk, g, kc, vtc, gkc, nw):
    b, t, qk = q.shape
    tc = kc.shape[1]
    npair = qk // HEAD_PAIR_DK
    cmf, cmb = _gla_masks()
    lat = lambda width, off: pl.BlockSpec((1, t, width), lambda i, j: (i, 0, j + off))
    ctx = lambda width, off: pl.BlockSpec((1, tc, width), lambda i, j: (i, 0, j + off))
    tr = lambda n_groups: pl.BlockSpec((1, n_groups, 2 * LANES, SUPER), lambda i, j: (i, 0, j, 0))
    const = lambda shape: pl.BlockSpec(shape, lambda i, j: (0,) * len(shape))
    return pl.pallas_call(
        _gla_kernel,
        grid=(b, npair),
        in_specs=[lat(HEAD_PAIR_DK, 0), lat(HEAD_PAIR_DK, 0), lat(2 * LANES, 0), tr(t // SUPER),
                  lat(HEAD_PAIR_DK, 0), lat(HEAD_PAIR_DK, npair), lat(2 * LANES, 0),
                  ctx(HEAD_PAIR_DK, 0), tr(tc // SUPER), ctx(HEAD_PAIR_DK, 0), ctx(HEAD_PAIR_DK, npair),
                  const((1, LANES)), const(cmf.shape), const(cmb.shape)],
        out_specs=lat(2 * LANES, 0),
        out_shape=jax.ShapeDtypeStruct((b, t, v.shape[2]), BF16),
        scratch_shapes=[pltpu.VMEM((2 * LANES, HEAD_PAIR_DK), F32), pltpu.VMEM((2 * LANES, HEAD_PAIR_DK), F32),
                        pltpu.VMEM((t, 2 * LANES), F32), pltpu.VMEM((t, 2 * LANES), F32)],
        compiler_params=pltpu.CompilerParams(dimension_semantics=("arbitrary", "arbitrary")),
        name="gla",
    )(q, k, v, vt, gk, gk, g, kc, vtc, gkc, gkc, nw, cmf, cmb)


def _pool_kernel(x_ref, cm_ref, cnt_ref, wp_ref, ps_ref, o_ref, a_ref, b_ref):
    t = x_ref.shape[1]
    pad = POOL_PAD_GRID_ROWS * GRID_W
    total = t + 2 * pad
    for gi, w in enumerate(POOL_WINDOWS):
        lo = w // 2
        cols = slice(gi * LANES, (gi + 1) * LANES)
        cmat = cm_ref[gi]
        a_ref[0:pad, :] = jnp.zeros((pad, LANES), F32)
        a_ref[pad + t:total, :] = jnp.zeros((pad, LANES), F32)
        for blk in range(t // SUPER):
            rs = slice(blk * SUPER, (blk + 1) * SUPER)
            a_ref[pad + blk * SUPER:pad + (blk + 1) * SUPER, :] = _dot(cmat, x_ref[0, rs, cols])
        src, dst = a_ref, b_ref
        m = 1
        while m < w:
            sh = m * GRID_W
            dst[0:total - sh, :] = src[0:total - sh, :] + src[sh:total, :]
            src, dst = dst, src
            m *= 2
        first = pad - lo * GRID_W
        pooled = src[first:first + t, :] / cnt_ref[gi] - x_ref[0, :, cols].astype(F32)
        yp = _dot(pooled.astype(BF16), wp_ref[gi]) * ps_ref[:, cols]
        o_ref[0, :, cols] = yp.astype(BF16)


def _pool_col_mats():
    i = np.arange(SUPER)
    same_row = (i[:, None] // GRID_W) == (i[None, :] // GRID_W)
    d = i[None, :] - i[:, None]
    mats = []
    for w in POOL_WINDOWS:
        lo = w // 2
        hi = w - 1 - lo
        mats.append(same_row & (d >= -lo) & (d <= hi))
    return jnp.asarray(np.stack(mats), BF16)


def _pool_counts(t):
    rows = t // GRID_W
    r = np.arange(t) // GRID_W
    c = np.arange(t) % GRID_W
    out = []
    for w in POOL_WINDOWS:
        lo = w // 2
        hi = w - 1 - lo
        cnt_r = np.minimum(r + hi + 1, rows) - np.maximum(r - lo, 0)
        cnt_c = np.minimum(c + hi + 1, GRID_W) - np.maximum(c - lo, 0)
        out.append(np.broadcast_to((cnt_r * cnt_c).astype(np.float32)[:, None], (t, LANES)))
    return jnp.asarray(np.stack(out))


def _pool_call(xp, w_pool, pool_scale):
    b, t, pw = xp.shape
    ng = len(POOL_WINDOWS)
    assert max(POOL_WINDOWS) // 2 <= POOL_PAD_GRID_ROWS and t % GRID_W == 0
    cm = _pool_col_mats()
    cnt = _pool_counts(t)
    staged = t + 2 * POOL_PAD_GRID_ROWS * GRID_W
    const = lambda shape: pl.BlockSpec(shape, lambda i: (0,) * len(shape))
    return pl.pallas_call(
        _pool_kernel,
        grid=(b,),
        in_specs=[pl.BlockSpec((1, t, pw), lambda i: (i, 0, 0)),
                  const(cm.shape), const(cnt.shape), const((ng, LANES, LANES)), const((1, pw))],
        out_specs=pl.BlockSpec((1, t, pw), lambda i: (i, 0, 0)),
        out_shape=jax.ShapeDtypeStruct((b, t, pw), BF16),
        scratch_shapes=[pltpu.VMEM((staged, LANES), F32), pltpu.VMEM((staged, LANES), F32)],
        compiler_params=pltpu.CompilerParams(dimension_semantics=("arbitrary",)),
        name="pool",
    )(xp, cm, cnt, w_pool, pool_scale)


def _route_kernel(gla_ref, pool_ref, x_ref, mod_ref, wo_ref, nw_ref, wr_ref, br_ref, lt_ref,
                  h_ref, xt_ref, code_ref, wt_ref, cnt_ref, run_ref, wr2_ref, *, gw, n_exp):
    i = pl.program_id(0)

    @pl.when(i == 0)
    def _():
        run_ref[...] = jnp.zeros_like(run_ref)
        wh, wl = _split_bf16(wr_ref[...])
        wr2_ref[:, :LANES] = wh
        wr2_ref[:, LANES:] = wl

    m = mod_ref[0]
    acc = _dot(gla_ref[...], wo_ref[0:gw, :]) + _dot(pool_ref[...], wo_ref[gw:, :])
    h = x_ref[...] + m[2:3] * acc
    h_ref[...] = h
    xt = _rmsnorm(h, nw_ref[...]) * (1.0 + m[4:5]) + m[3:4]
    xt_ref[...] = _pack_bf16_pairs(xt)
    xh, xl = _split_bf16(xt)
    wr2 = wr2_ref[...]
    t1 = _dot(xh, wr2)
    logits = t1[:, :LANES] + t1[:, LANES:] + _dot(xl, wr2[:, :LANES]) + br_ref[...]
    lane = lax.broadcasted_iota(I32, logits.shape, 1)
    neg = jnp.float32(-jnp.inf)
    logits = jnp.where(lane < n_exp, logits, neg)
    vals, hots = [], []
    e_out = jnp.zeros(logits.shape, I32)
    lane_f = lane.astype(F32)
    for j in range(TOP_K):
        mx = jnp.max(logits, axis=-1, keepdims=True)
        idx = jnp.min(jnp.where(logits == mx, lane_f, float(LANES)), axis=-1, keepdims=True)
        hot = lane_f == idx
        vals.append(mx)
        hots.append(hot)
        e_out = jnp.where(lane == j, idx.astype(I32), e_out)
        logits = jnp.where(hot, neg, logits)
    ex = [jnp.exp(v - vals[0]) for v in vals]
    den = ex[0] + ex[1] + ex[2] + ex[3]
    w_out = jnp.zeros(logits.shape, F32)
    for j in range(TOP_K):
        w_out = jnp.where(lane == j, ex[j] / den, w_out)
    osum = jnp.where(hots[0] | hots[1] | hots[2] | hots[3], 1.0, 0.0)
    before = _dot(lt_ref[...], osum.astype(BF16)) + run_ref[0:1, :]
    rk_out = jnp.zeros(logits.shape, I32)
    for j in range(TOP_K):
        rj = jnp.sum(jnp.where(hots[j], before, 0.0), axis=-1, keepdims=True)
        rk_out = jnp.where(lane == j, rj.astype(I32), rk_out)
    run = run_ref[0:1, :] + jnp.sum(osum, axis=0, keepdims=True)
    run_ref[...] = jnp.broadcast_to(run, run_ref.shape)
    code = e_out * RANK_LIMIT + rk_out
    code_ref[...] = code.T[:code_ref.shape[0], :]
    wt_ref[...] = w_out
    cnt_ref[...] = jnp.broadcast_to(run, cnt_ref.shape).astype(I32)


def _route_call(gla, pool, x, mod, w_out, nw, wr, br, *, tm, n_exp, first_tile, n_tiles):
    n_all, d = x.shape
    n = n_tiles * tm
    gw = gla.shape[1]
    t_per_b = n_all // mod.shape[0]
    lt = jnp.asarray(np.tril(np.ones((tm, tm), np.float32), -1), BF16)
    row_in = lambda width: pl.BlockSpec((tm, width), lambda i: (i + first_tile, 0))
    row = lambda width: pl.BlockSpec((tm, width), lambda i: (i, 0))
    const = lambda shape: pl.BlockSpec(shape, lambda i: (0,) * len(shape))
    sds = jax.ShapeDtypeStruct
    return pl.pallas_call(
        functools.partial(_route_kernel, gw=gw, n_exp=n_exp),
        grid=(n_tiles,),
        in_specs=[row_in(gw), row_in(pool.shape[1]), row_in(d),
                  pl.BlockSpec((1, N_MOD, d), lambda i: ((i + first_tile) * tm // t_per_b, 0, 0)),
                  const(w_out.shape), const((1, d)), const(wr.shape), const((1, LANES)), const((tm, tm))],
        out_specs=[row(d), row(d // 2), pl.BlockSpec((8, tm), lambda i: (0, i)), row(LANES), const((8, LANES))],
        out_shape=[sds((n, d), F32), sds((n, d // 2), U32), sds((8, n), I32),
                   sds((n, LANES), F32), sds((8, LANES), I32)],
        scratch_shapes=[pltpu.VMEM((8, LANES), F32), pltpu.VMEM((d, 2 * LANES), BF16)],
        compiler_params=pltpu.CompilerParams(dimension_semantics=("arbitrary",)),
        name="route",
    )(gla, pool, x, mod, w_out, nw, wr, br, lt)


def _plan_kernel(cnt_ref, code_ref, dest_ref, be_ref, nv_ref, first_ref, next_ref, slot_ref, nu_ref, start_ref,
                 *, n_exp, rows):
    @pl.when(pl.program_id(0) == 0)
    def _():
        blk = (lax.broadcasted_iota(I32, be_ref.shape, 0) * LANES + lax.broadcasted_iota(I32, be_ref.shape, 1))
        blk_row0 = blk * rows
        nxt_e = [None] * n_exp
        nxt = jnp.int32(-1)
        for e in reversed(range(n_exp)):
            nxt_e[e] = nxt
            nxt = jnp.where(cnt_ref[e] > 0, e, nxt)
        zeros = jnp.zeros(be_ref.shape, I32)
        be, end_valid, first, nxt_blk, slot = zeros, zeros, zeros, zeros - 1, zeros
        acc = jnp.int32(0)
        last_e = jnp.int32(0)
        ordinal = jnp.int32(0)
        for e in range(n_exp):
            c = cnt_ref[e]
            start_ref[e] = acc
            in_e = (blk_row0 >= acc) & (c > 0)
            end_valid = jnp.where(in_e, acc + c, end_valid)
            be = jnp.where(in_e, e, be)
            first = jnp.where(in_e, (blk_row0 == acc).astype(I32), first)
            nxt_blk = jnp.where(in_e, nxt_e[e], nxt_blk)
            slot = jnp.where(in_e, ordinal % 2, slot)
            acc = acc + (c + rows - 1) // rows * rows
            last_e = jnp.where(c > 0, e, last_e)
            ordinal = ordinal + (c > 0).astype(I32)
        n_used = acc // rows
        nu_ref[0] = n_used
        be_ref[...] = jnp.where(blk < n_used, be, last_e)
        nv_ref[...] = jnp.clip(end_valid - blk_row0, 0, rows)
        first_ref[...] = first
        next_ref[...] = nxt_blk
        slot_ref[...] = slot

    code = code_ref[...]
    e_vec = code // RANK_LIMIT
    dest = code % RANK_LIMIT
    for e in range(n_exp):
        dest = dest + jnp.where(e_vec == e, start_ref[e], 0)
    dest_ref[...] = dest


def _plan_call(counts, code_t, *, n_exp, rows, chunk):
    n = code_t.shape[1]
    sds = jax.ShapeDtypeStruct
    smem = pltpu.SMEM
    return pl.pallas_call(
        functools.partial(_plan_kernel, n_exp=n_exp, rows=rows),
        grid=(n // chunk,),
        in_specs=[pl.BlockSpec(memory_space=smem), pl.BlockSpec((8, chunk), lambda i: (0, i))],
        out_specs=[pl.BlockSpec((8, chunk), lambda i: (0, i))]
        + [pl.BlockSpec((8, LANES), lambda i: (0, 0))] * 5 + [pl.BlockSpec(memory_space=smem)],
        out_shape=[sds((8, n), I32)] + [sds((8, LANES), I32)] * 5 + [sds((1,), I32)],
        scratch_shapes=[pltpu.SMEM((n_exp,), I32)],
        compiler_params=pltpu.CompilerParams(dimension_semantics=("arbitrary",)),
        name="plan",
    )(counts, code_t)


def _sc_worker_id():
    return lax.axis_index("s") * SC_CORES + lax.axis_index("c")


def _sc_scatter_call(x, idx3, *, n_out):
    n, d = x.shape
    n_win_total, k, w = idx3.shape
    n_win = n_win_total // SC_WORKERS
    mesh = plsc.VectorSubcoreMesh(core_axis_name="c", subcore_axis_name="s")

    @functools.partial(
        pl.kernel, mesh=mesh,
        out_type=jax.ShapeDtypeStruct((n_out, d), x.dtype),
        scratch_types=[pltpu.VMEM((k, w), I32), pltpu.VMEM((w, d), x.dtype), pltpu.SemaphoreType.DMA],
        name="sc_dispatch",
    )
    def kern(x_hbm, idx_hbm, out_hbm, idx_v, rows_v, sem):
        wid = _sc_worker_id()

        @pl.loop(0, n_win)
        def _(i):
            win = wid * n_win + i
            pltpu.sync_copy(idx_hbm.at[win], idx_v)
            pltpu.sync_copy(x_hbm.at[pl.ds(win * w, w)], rows_v)
            for j in range(k):
                pltpu.async_copy(rows_v, out_hbm.at[idx_v.at[j]], sem).wait()

    return kern(x, idx3)


def _sc_gather_call(table, idx3):
    n_workers, n_win, w = idx3.shape
    d = table.shape[1]
    assert n_workers == SC_WORKERS and n_win % 2 == 0
    mesh = plsc.VectorSubcoreMesh(core_axis_name="c", subcore_axis_name="s")

    @functools.partial(
        pl.kernel, mesh=mesh,
        out_type=jax.ShapeDtypeStruct((n_workers * n_win * w, d), table.dtype),
        scratch_types=[pltpu.VMEM((n_win, w), I32), pltpu.VMEM((2, w, d), table.dtype),
                       pltpu.SemaphoreType.DMA((2,)), pltpu.SemaphoreType.DMA((2,))],
        name="sc_gather",
    )
    def kern(table_hbm, idx_hbm, out_hbm, idx_v, rows_v, gsem, osem):
        wid = _sc_worker_id()
        base = wid * n_win
        pltpu.sync_copy(idx_hbm.at[wid], idx_v)

        def gather(wi, b):
            return pltpu.make_async_copy(table_hbm.at[idx_v.at[wi]], rows_v.at[b], gsem.at[b])

        def put(wi, b):
            return pltpu.make_async_copy(rows_v.at[b], out_hbm.at[pl.ds((base + wi) * w, w)], osem.at[b])

        gather(0, 0).start()

        @pl.loop(0, n_win, step=2)
        def _(i):
            for b in range(2):
                wi = i + b

                @pl.when(wi + 1 < n_win)
                def _():
                    @pl.when(wi >= 1)
                    def _():
                        put(wi - 1, 1 - b).wait()
                    gather(wi + 1, 1 - b).start()

                gather(wi, b).wait()
                put(wi, b).start()

        put(n_win - 2, 0).wait()
        put(n_win - 1, 1).wait()

    return kern(table, idx3)


def _expert_kernel(be_ref, nu_ref, nv_ref, first_ref, next_ref, slot_ref,
                   x_ref, wgu_hbm, bgu_ref, wd_hbm, bd_ref, y_ref,
                   wgu_f32_ref, wd_f32_ref, wgu_bf_ref, wd_bf_ref, sem, *, d_ff):
    i = pl.program_id(0)
    slot = slot_ref[i]

    def fetch(e, s):
        return (pltpu.make_async_copy(wgu_hbm.at[e], wgu_f32_ref.at[s], sem.at[0, s]),
                pltpu.make_async_copy(wd_hbm.at[e], wd_f32_ref.at[s], sem.at[1, s]))

    @pl.when(i == 0)
    def _():
        for cp in fetch(be_ref[0], slot):
            cp.start()

    @pl.when(first_ref[i] == 1)
    def _():
        for cp in fetch(be_ref[i], slot):
            cp.wait()

        @pl.when(next_ref[i] >= 0)
        def _():
            for cp in fetch(next_ref[i], 1 - slot):
                cp.start()

        wgu_bf_ref[...] = wgu_f32_ref[slot].astype(BF16)
        wd_bf_ref[...] = wd_f32_ref[slot].astype(BF16)

    n_valid = nv_ref[i]
    n_rows = x_ref.shape[0]

    def expert_rows(rows):
        row = lax.broadcasted_iota(I32, (rows, x_ref.shape[1]), 0)
        lo, hi = _unpack_bf16_pairs(jnp.where(row < n_valid, x_ref[0:rows, :], jnp.uint32(0)))
        xb = jnp.concatenate([lo, hi], axis=1).astype(BF16)
        gu = _dot(xb, wgu_bf_ref[...]) + bgu_ref[0]
        gate = jnp.minimum(gu[:, :d_ff], SWIGLU_LIMIT)
        up = jnp.clip(gu[:, d_ff:], -SWIGLU_LIMIT, SWIGLU_LIMIT)
        act = (up + 1.0) * (0.5 * gate) * (1.0 + jnp.tanh((0.5 * SWIGLU_ALPHA) * gate))
        y_ref[0:rows, :] = _pack_bf16_pairs(_dot(act.astype(BF16), wd_bf_ref[...]) + bd_ref[0])

    in_use = i < nu_ref[0]

    for rows in range(EXPERT_ROW_STEP, n_rows + 1, EXPERT_ROW_STEP):
        @pl.when(in_use & (n_valid > rows - EXPERT_ROW_STEP) & (n_valid <= rows))
        def _(rows=rows):
            expert_rows(rows)
            if rows < n_rows:
                y_ref[rows:, :] = jnp.zeros((n_rows - rows, y_ref.shape[1]), y_ref.dtype)


def _expert_call(plan, xs, w_gu, b_gu, w_down, b_down):
    block_e, n_valid, first, nxt, slot, n_used = plan
    n_pad = xs.shape[0]
    n_exp, d, two_ff = w_gu.shape
    d_ff = two_ff // 2
    nblk = n_pad // EXPERT_ROWS
    rows = lambda i, be, nu, *_: (jnp.minimum(i, nu[0] - 1), 0)
    per_e = lambda i, be, *_: (be[i], 0, 0)
    grid_spec = pltpu.PrefetchScalarGridSpec(
        num_scalar_prefetch=6,
        grid=(nblk,),
        in_specs=[pl.BlockSpec((EXPERT_ROWS, d // 2), rows),
                  pl.BlockSpec(memory_space=pl.ANY), pl.BlockSpec((1, 1, two_ff), per_e),
                  pl.BlockSpec(memory_space=pl.ANY), pl.BlockSpec((1, 1, d), per_e)],
        out_specs=pl.BlockSpec((EXPERT_ROWS, d // 2), rows),
        scratch_shapes=[pltpu.VMEM((2, d, two_ff), F32), pltpu.VMEM((2, d_ff, d), F32),
                        pltpu.VMEM((d, two_ff), BF16), pltpu.VMEM((d_ff, d), BF16),
                        pltpu.SemaphoreType.DMA((2, 2))],
    )
    flat = lambda a: a.reshape(-1)
    return pl.pallas_call(
        functools.partial(_expert_kernel, d_ff=d_ff),
        grid_spec=grid_spec,
        out_shape=jax.ShapeDtypeStruct((n_pad, d // 2), U32),
        compiler_params=pltpu.CompilerParams(dimension_semantics=("arbitrary",),
                                             vmem_limit_bytes=EXPERT_VMEM_BYTES),
        name="experts",
    )(flat(block_e), n_used, flat(n_valid), flat(first), flat(nxt), flat(slot),
      xs, w_gu, b_gu.reshape(n_exp, 1, two_ff), w_down, b_down.reshape(n_exp, 1, d))


def _combine_kernel(y4_ref, wt_ref, h_ref, mod_ref, fw_ref, o_ref):
    wt = wt_ref[...]
    acc_lo, acc_hi = None, None
    for j in range(TOP_K):
        lo, hi = _unpack_bf16_pairs(y4_ref[j])
        w = wt[:, j:j + 1]
        acc_lo = w * lo if j == 0 else acc_lo + w * lo
        acc_hi = w * hi if j == 0 else acc_hi + w * hi
    acc = jnp.concatenate([acc_lo, acc_hi], axis=1)
    m = mod_ref[0]
    o_ref[...] = _rmsnorm(h_ref[...] + m[5:6] * acc, fw_ref[...])


def _combine_call(y4, wts, h, mod, fw, prev_out, *, n, tg, first_tile):
    d = h.shape[1]
    t_per_b = n // mod.shape[0]
    part = lambda width: pl.BlockSpec((tg, width), lambda i: (i, 0))
    row = lambda width: pl.BlockSpec((tg, width), lambda i: (i + first_tile, 0))
    in_specs = [pl.BlockSpec((TOP_K, tg, d // 2), lambda i: (0, i, 0)),
                part(LANES), part(d),
                pl.BlockSpec((1, N_MOD, d), lambda i: ((i + first_tile) * tg // t_per_b, 0, 0)),
                pl.BlockSpec((1, d), lambda i: (0, 0))]
    args = [y4, wts, h, mod, fw]
    kern = _combine_kernel
    aliases = {}
    if prev_out is not None:
        in_specs.append(pl.BlockSpec(memory_space=pl.ANY))
        args.append(prev_out)
        kern = lambda y4_ref, wt_ref, h_ref, mod_ref, fw_ref, prev_ref, o_ref: _combine_kernel(
            y4_ref, wt_ref, h_ref, mod_ref, fw_ref, o_ref)
        aliases = {len(args) - 1: 0}
    return pl.pallas_call(
        kern,
        grid=(y4.shape[1] // tg,),
        in_specs=in_specs,
        out_specs=row(d),
        out_shape=jax.ShapeDtypeStruct((n, d), F32),
        input_output_aliases=aliases,
        compiler_params=pltpu.CompilerParams(dimension_semantics=("arbitrary",)),
        name="combine",
    )(*args)


def kernel(x, c, ctx, c_ctx, w_ada, b_ada, norm_mix_w, norm_mlp_w, w_in, w_gk_f, b_gk_f, w_gk_b, b_gk_b,
           gla_norm_w, w_pool, pool_scale, w_out, w_router, b_router, w_gu, b_gu, w_down, b_down,
           final_norm_w):
    b, t, d = x.shape
    assert w_ada.shape[0] == 1, "single-layer trunk"
    n_exp = w_router.shape[2]
    rank = w_gk_f.shape[1]
    qk = w_gk_f.shape[2]
    dk = qk // GLA_HEADS
    gw = GLA_HEADS * gla_norm_w.shape[1]
    pw = w_pool.shape[1] * w_pool.shape[2]
    assert w_in.shape[2] == 2 * qk + 2 * gw + 2 * rank + pw and 2 * rank <= LANES
    assert t % SUPER == 0 and ctx.shape[1] % SUPER == 0 and n_exp <= LANES

    rows = -(-(b + 1) // 8) * 8
    cc = jnp.concatenate([c, c_ctx[None, :], jnp.zeros((rows - b - 1, d), F32)], axis=0)
    mod = _mod_call(cc, w_ada[0], b_ada)
    mod_x = mod[:b].reshape(b, N_MOD, d)
    mod_c = mod[b:b + 1].reshape(1, N_MOD, d)

    wi = w_in[0]
    o_r = 2 * qk + 2 * gw
    w_cat = jnp.concatenate([wi[:, :o_r], wi[:, o_r + 2 * rank:], wi[:, o_r:o_r + 2 * rank],
                             jnp.zeros((d, LANES - 2 * rank), F32)], axis=1).astype(BF16)
    wgk = jnp.concatenate([jnp.pad(w_gk_f[0], ((0, 0), (0, qk))), jnp.pad(w_gk_b[0], ((0, 0), (qk, 0))),
                           jnp.zeros((LANES - 2 * rank, 2 * qk), F32)], axis=0)
    bgk = jnp.concatenate([b_gk_f[0], b_gk_b[0]])[None, :]
    proj = functools.partial(_inproj_call, nw=norm_mix_w, w=w_cat, wgk=wgk.astype(BF16), bgk=bgk,
                             qk=qk, gw=gw, pw=pw, dk=dk)
    q, k, v, vt, g, xp, gk = proj(x, mod_x, tm=1024)
    _, kc, _, vtc, _, _, gkc = proj(ctx, mod_c, tm=SUPER)

    gla = _gla_call(q, k, v, vt, gk, g, kc, vtc, gkc, gla_norm_w)
    pool = _pool_call(xp, w_pool[0].astype(BF16), pool_scale)

    n = b * t
    wr = jnp.pad(w_router[0], ((0, 0), (0, LANES - n_exp)))
    br = jnp.pad(b_router, ((0, 0), (0, LANES - n_exp)))
    n_part = n // MOE_PARTS
    tm, tg = 512, 1024
    assert n % MOE_PARTS == 0 and n_part % (SC_WORKERS * SC_WINDOW) == 0 and n_part % tm == 0 and n_part < RANK_LIMIT
    assert (n_part * TOP_K) % (SC_WORKERS * 2 * SC_GATHER_WINDOW) == 0
    n_pad = n_part * TOP_K + n_exp * EXPERT_ROWS
    nblk = n_pad // EXPERT_ROWS
    assert nblk <= 8 * LANES
    parts = range(MOE_PARTS)
    routed = [_route_call(gla.reshape(n, gw), pool.reshape(n, pw), x.reshape(n, d), mod_x, w_out[0].astype(BF16),
                          norm_mlp_w, wr, br, tm=tm, n_exp=n_exp,
                          first_tile=p * n_part // tm, n_tiles=n_part // tm) for p in parts]
    plans = [_plan_call(cnt[0, :n_exp], code_t, n_exp=n_exp, rows=EXPERT_ROWS, chunk=min(n_part, 4096))
             for (_, _, code_t, _, cnt) in routed]
    dests = [plan[0][:TOP_K] for plan in plans]
    xs = [_sc_scatter_call(routed[p][1],
                           dests[p].reshape(TOP_K, n_part // SC_WINDOW, SC_WINDOW).transpose(1, 0, 2), n_out=n_pad)
          for p in parts]
    ys = [_expert_call(plans[p][1:], xs[p], w_gu[0], b_gu[0], w_down[0], b_down[0]) for p in parts]
    y4 = [_sc_gather_call(ys[p], dests[p].reshape(SC_WORKERS, -1, SC_GATHER_WINDOW)).reshape(TOP_K, n_part, d // 2)
          for p in parts]
    out = None
    for p in parts:
        out = _combine_call(y4[p], routed[p][3], routed[p][0], mod_x, final_norm_w[None, :], out,
                            n=n, tg=tg, first_tile=p * n_part // tg)
    return out.reshape(b, t, d)
```

```python
import functools

import numpy as np
import jax
import jax.numpy as jnp
from jax import lax
from jax.experimental import pallas as pl
from jax.experimental.pallas import tpu as pltpu
from jax.experimental.pallas import tpu_sc as plsc

F32 = jnp.float32
BF16 = jnp.bfloat16
I32 = jnp.int32
U32 = jnp.uint32

GRID_W = 64
GLA_HEADS = 4
GLA_CHUNK = 64
GATE_NORMALIZER = 16.0
POOL_WINDOWS = (2, 4, 8, 16)
POOL_PAD_GRID_ROWS = 8
TOP_K = 4
RANK_LIMIT = 1 << 20
SWIGLU_LIMIT = 7.0
SWIGLU_ALPHA = 1.702
N_MOD = 6
EPS = 1e-6

LANES = 128
SUPER = 256
HEAD_PAIR_DK = 128
EXPERT_ROWS = 512
EXPERT_ROW_STEP = 128
MOE_PARTS = 2
EXPERT_VMEM_BYTES = 56 * 1024 * 1024
SC_CORES = 2
SC_SUBCORES = 16
SC_WORKERS = SC_CORES * SC_SUBCORES
SC_WINDOW = 32
SC_GATHER_WINDOW = 64


def _dot(a, b):
    return jnp.dot(a, b, preferred_element_type=F32)


def _dot_nt(a, b):
    return lax.dot_general(a, b, (((1,), (1,)), ((), ())), preferred_element_type=F32)


def _split_bf16(x):
    hi = x.astype(BF16)
    lo = (x - hi.astype(F32)).astype(BF16)
    return hi, lo


def _pack_bf16_pairs(x):
    c = x.shape[1] // 2
    lo = lax.bitcast_convert_type(x[:, :c].astype(BF16).astype(F32), U32)
    hi = lax.bitcast_convert_type(x[:, c:].astype(BF16).astype(F32), U32)
    return (lo >> 16) | hi


def _unpack_bf16_pairs(p):
    lo = lax.bitcast_convert_type(p << 16, F32)
    hi = lax.bitcast_convert_type(p & jnp.uint32(0xFFFF0000), F32)
    return lo, hi


def _rmsnorm(x, w):
    var = jnp.mean(x * x, axis=-1, keepdims=True)
    return x * lax.rsqrt(var + EPS) * w


def _mod_kernel(c_ref, w_ref, b_ref, o_ref):
    c = c_ref[...]
    s = c * jax.nn.sigmoid(c)
    o_ref[...] = jnp.dot(s, w_ref[...], precision=lax.Precision.HIGHEST,
                         preferred_element_type=F32) + b_ref[...]


def _mod_call(cc, w_ada, b_ada):
    rows, d = cc.shape
    n = w_ada.shape[1]
    tn = 1024
    return pl.pallas_call(
        _mod_kernel,
        grid=(n // tn,),
        in_specs=[pl.BlockSpec((rows, d), lambda j: (0, 0)),
                  pl.BlockSpec((d, tn), lambda j: (0, j)),
                  pl.BlockSpec((1, tn), lambda j: (0, j))],
        out_specs=pl.BlockSpec((rows, tn), lambda j: (0, j)),
        out_shape=jax.ShapeDtypeStruct((rows, n), F32),
        name="mod",
    )(cc, w_ada, b_ada)


def _inproj_kernel(x_ref, mod_ref, nw_ref, w_ref, wgk_ref, bgk_ref,
                   q_ref, k_ref, v_ref, vt_ref, g_ref, p_ref, gk_ref, *, qk, gw, pw, dk):
    x = x_ref[0]
    m = mod_ref[0]
    hm = (_rmsnorm(x, nw_ref[...]) * (1.0 + m[1:2]) + m[0:1]).astype(BF16)
    p = _dot(hm, w_ref[...])
    vt = p[:, 2 * qk:2 * qk + gw].T
    for s in range(vt_ref.shape[1]):
        vt_ref[0, s] = vt[:, s * SUPER:(s + 1) * SUPER].astype(BF16)
    o = 0
    q_ref[0] = (p[:, o:o + qk] * (dk ** -0.5)).astype(BF16); o += qk
    k_ref[0] = p[:, o:o + qk].astype(BF16); o += qk
    v_ref[0] = p[:, o:o + gw].astype(BF16); o += gw
    g_ref[0] = p[:, o:o + gw].astype(BF16); o += gw
    p_ref[0] = p[:, o:o + pw].astype(BF16); o += pw
    r = p[:, o:o + LANES]
    z = _dot(r.astype(BF16), wgk_ref[...]) + bgk_ref[...]
    gk_ref[0] = (jnp.minimum(z, 0.0) - jnp.log1p(jnp.exp(-jnp.abs(z)))) * (1.0 / GATE_NORMALIZER)


def _inproj_call(x, mod, nw, w, wgk, bgk, *, qk, gw, pw, dk, tm):
    b, t, d = x.shape
    n_in = w.shape[1]
    bs = lambda width: pl.BlockSpec((1, tm, width), lambda i, j: (i, j, 0))
    const = lambda shape: pl.BlockSpec(shape, lambda i, j: (0,) * len(shape))
    per_batch = mod.shape[0] > 1
    sds = jax.ShapeDtypeStruct
    return pl.pallas_call(
        functools.partial(_inproj_kernel, qk=qk, gw=gw, pw=pw, dk=dk),
        grid=(b, t // tm),
        in_specs=[bs(d),
                  pl.BlockSpec((1, N_MOD, d), (lambda i, j: (i, 0, 0)) if per_batch else (lambda i, j: (0, 0, 0))),
                  const((1, d)), const((d, n_in)), const((LANES, 2 * qk)), const((1, 2 * qk))],
        out_specs=[bs(qk), bs(qk), bs(gw),
                   pl.BlockSpec((1, tm // SUPER, gw, SUPER), lambda i, j: (i, j, 0, 0)),
                   bs(gw), bs(pw), bs(2 * qk)],
        out_shape=[sds((b, t, qk), BF16), sds((b, t, qk), BF16), sds((b, t, gw), BF16),
                   sds((b, t // SUPER, gw, SUPER), BF16),
                   sds((b, t, gw), BF16), sds((b, t, pw), BF16), sds((b, t, 2 * qk), F32)],
        compiler_params=pltpu.CompilerParams(dimension_semantics=("arbitrary", "arbitrary")),
        name="inproj",
    )(x, mod, nw, w, wgk, bgk)


def _gla_super(q, k, v, vt, gk, cm, amask, bd_mask, st, fwd, want_out):
    nch = SUPER // GLA_CHUNK
    order = tuple(range(nch)) if fwd else tuple(reversed(range(nch)))
    last_row = GLA_CHUNK - 1 if fwd else 0
    mid_row = GLA_CHUNK // 2 - 1 if fwd else GLA_CHUNK // 2
    hi, lo = _split_bf16(gk)
    bcum = _dot(cm, hi) + _dot(cm, lo)

    def chunk_row(r):
        return jnp.concatenate(
            [jnp.broadcast_to(bcum[c * GLA_CHUNK + r:c * GLA_CHUNK + r + 1, :], (GLA_CHUNK, bcum.shape[1]))
             for c in range(nch)], axis=0)

    chunk_of_row = lax.broadcasted_iota(I32, bcum.shape, 0) // GLA_CHUNK

    def by_chunk(x):
        return jnp.concatenate([jnp.where(chunk_of_row == c, x, 0.0).astype(BF16) for c in range(nch)], axis=1)

    blast = chunk_row(last_row)
    u_all = _dot(vt, by_chunk(k * jnp.exp(blast - bcum)))
    before = [None] * nch
    for c in order:
        before[c] = st
        decay = jnp.exp(bcum[c * GLA_CHUNK + last_row:c * GLA_CHUNK + last_row + 1, :])
        st = st * decay + jnp.where(bd_mask, u_all[:, c * HEAD_PAIR_DK:(c + 1) * HEAD_PAIR_DK], 0.0)
    if not want_out:
        return None, st
    bmid = chunk_row(mid_row)
    qt = q * jnp.exp(bcum - bmid)
    kt = (k * jnp.exp(bmid - bcum)).astype(BF16)
    lane = lax.broadcasted_iota(I32, qt.shape, 1)
    half = HEAD_PAIR_DK // 2
    o_heads = []
    for hh in range(2):
        sel = (lane < half) if hh == 0 else (lane >= half)
        a = _dot_nt(jnp.where(sel, qt, 0.0).astype(BF16), kt)
        a = jnp.where(amask, a, 0.0).astype(BF16)
        o_heads.append(_dot(a, v[:, hh * LANES:(hh + 1) * LANES]))
    qh = (q * jnp.exp(bcum)).astype(BF16)
    o_inter = jnp.concatenate(
        [_dot_nt(qh[c * GLA_CHUNK:(c + 1) * GLA_CHUNK], before[c].astype(BF16)) for c in range(nch)], axis=0)
    return jnp.concatenate(o_heads, axis=1) + o_inter, st


def _gla_kernel(q_ref, k_ref, v_ref, vt_ref, gkf_ref, gkb_ref, g_ref, kc_ref, vtc_ref, gkfc_ref, gkbc_ref,
                nw_ref, cmf_ref, cmb_ref, o_ref, stf_ref, stb_ref, of_ref, ob_ref):
    t = q_ref.shape[1]
    tc = kc_ref.shape[1]
    nsc, nscc = t // SUPER, tc // SUPER
    cmf = cmf_ref[...]
    cmb = cmb_ref[...]
    amask_f = cmf > 0
    amask_b = cmb > 0
    row = lax.broadcasted_iota(I32, (2 * LANES, HEAD_PAIR_DK), 0)
    lane = lax.broadcasted_iota(I32, (2 * LANES, HEAD_PAIR_DK), 1)
    bd_mask = (row < LANES) == (lane < HEAD_PAIR_DK // 2)

    def ctx_state(gk_ref, cm, fwd, j, st):
        rows = pl.ds(j * SUPER, SUPER)
        return _gla_super(None, kc_ref[0, rows, :].astype(F32), None, vtc_ref[0, j], gk_ref[0, rows, :],
                          cm, None, bd_mask, st, fwd, False)[1]

    def latent(gk_ref, cm, amask, fwd, j, st):
        rows = pl.ds(pl.multiple_of(j * SUPER, SUPER), SUPER)
        return _gla_super(q_ref[0, rows, :].astype(F32), k_ref[0, rows, :].astype(F32), v_ref[0, rows, :],
                          vt_ref[0, j], gk_ref[0, rows, :], cm, amask, bd_mask, st, fwd, True)

    st = jnp.zeros(stf_ref.shape, F32)
    for j in range(nscc):
        st = ctx_state(gkfc_ref, cmf, True, j, st)
    stf_ref[...] = st
    st = jnp.zeros(stb_ref.shape, F32)
    for j in reversed(range(nscc)):
        st = ctx_state(gkbc_ref, cmb, False, j, st)
    stb_ref[...] = st

    def scan_body(jj, carry):
        jb = nsc - 1 - jj
        of, stf = latent(gkf_ref, cmf, amask_f, True, jj, stf_ref[...])
        of_ref[pl.ds(pl.multiple_of(jj * SUPER, SUPER), SUPER), :] = of
        stf_ref[...] = stf
        ob, stb = latent(gkb_ref, cmb, amask_b, False, jb, stb_ref[...])
        ob_ref[pl.ds(pl.multiple_of(jb * SUPER, SUPER), SUPER), :] = ob
        stb_ref[...] = stb
        return carry

    lax.fori_loop(0, nsc, scan_body, 0, unroll=8)

    nw = nw_ref[...]

    def out_body(j, carry):
        rows = pl.ds(pl.multiple_of(j * SUPER, SUPER), SUPER)
        o = of_ref[rows, :] + ob_ref[rows, :]
        g = g_ref[0, rows, :].astype(F32)
        half_g = 0.5 * g
        gate = half_g * (1.0 + jnp.tanh(half_g))
        for hh in range(2):
            oh = o[:, hh * LANES:(hh + 1) * LANES]
            on = oh * lax.rsqrt(jnp.mean(oh * oh, axis=-1, keepdims=True) + EPS) * nw
            o_ref[0, rows, hh * LANES:(hh + 1) * LANES] = (on * gate[:, hh * LANES:(hh + 1) * LANES]).astype(BF16)
        return carry

    lax.fori_loop(0, nsc, out_body, 0)


def _gla_masks():
    i = np.arange(SUPER)
    same = (i[:, None] // GLA_CHUNK) == (i[None, :] // GLA_CHUNK)
    fwd = same & (i[None, :] <= i[:, None])
    bwd = same & (i[None, :] >= i[:, None])
    return jnp.asarray(fwd, BF16), jnp.asarray(bwd, BF16)


def _gla_call(q, k, v, vt, gk, g, kc, vtc, gkc, nw):
    b, t, qk = q.shape
    tc = kc.shape[1]
    npair = qk // HEAD_PAIR_DK
    cmf, cmb = _gla_masks()
    lat = lambda width, off: pl.BlockSpec((1, t, width), lambda i, j: (i, 0, j + off))
    ctx = lambda width, off: pl.BlockSpec((1, tc, width), lambda i, j: (i, 0, j + off))
    tr = lambda n_groups: pl.BlockSpec((1, n_groups, 2 * LANES, SUPER), lambda i, j: (i, 0, j, 0))
    const = lambda shape: pl.BlockSpec(shape, lambda i, j: (0,) * len(shape))
    return pl.pallas_call(
        _gla_kernel,
        grid=(b, npair),
        in_specs=[lat(HEAD_PAIR_DK, 0), lat(HEAD_PAIR_DK, 0), lat(2 * LANES, 0), tr(t // SUPER),
                  lat(HEAD_PAIR_DK, 0), lat(HEAD_PAIR_DK, npair), lat(2 * LANES, 0),
                  ctx(HEAD_PAIR_DK, 0), tr(tc // SUPER), ctx(HEAD_PAIR_DK, 0), ctx(HEAD_PAIR_DK, npair),
                  const((1, LANES)), const(cmf.shape), const(cmb.shape)],
        out_specs=lat(2 * LANES, 0),
        out_shape=jax.ShapeDtypeStruct((b, t, v.shape[2]), BF16),
        scratch_shapes=[pltpu.VMEM((2 * LANES, HEAD_PAIR_DK), F32), pltpu.VMEM((2 * LANES, HEAD_PAIR_DK), F32),
                        pltpu.VMEM((t, 2 * LANES), F32), pltpu.VMEM((t, 2 * LANES), F32)],
        compiler_params=pltpu.CompilerParams(dimension_semantics=("arbitrary", "arbitrary")),
        name="gla",
    )(q, k, v, vt, gk, gk, g, kc, vtc, gkc, gkc, nw, cmf, cmb)


def _pool_kernel(x_ref, cm_ref, cnt_ref, wp_ref, ps_ref, o_ref, a_ref, b_ref):
    t = x_ref.shape[1]
    pad = POOL_PAD_GRID_ROWS * GRID_W
    total = t + 2 * pad
    for gi, w in enumerate(POOL_WINDOWS):
        lo = w // 2
        cols = slice(gi * LANES, (gi + 1) * LANES)
        cmat = cm_ref[gi]
        a_ref[0:pad, :] = jnp.zeros((pad, LANES), F32)
        a_ref[pad + t:total, :] = jnp.zeros((pad, LANES), F32)
        for blk in range(t // SUPER):
            rs = slice(blk * SUPER, (blk + 1) * SUPER)
            a_ref[pad + blk * SUPER:pad + (blk + 1) * SUPER, :] = _dot(cmat, x_ref[0, rs, cols])
        src, dst = a_ref, b_ref
        m = 1
        while m < w:
            sh = m * GRID_W
            dst[0:total - sh, :] = src[0:total - sh, :] + src[sh:total, :]
            src, dst = dst, src
            m *= 2
        first = pad - lo * GRID_W
        pooled = src[first:first + t, :] / cnt_ref[gi] - x_ref[0, :, cols].astype(F32)
        yp = _dot(pooled.astype(BF16), wp_ref[gi]) * ps_ref[:, cols]
        o_ref[0, :, cols] = yp.astype(BF16)


def _pool_col_mats():
    i = np.arange(SUPER)
    same_row = (i[:, None] // GRID_W) == (i[None, :] // GRID_W)
    d = i[None, :] - i[:, None]
    mats = []
    for w in POOL_WINDOWS:
        lo = w // 2
        hi = w - 1 - lo
        mats.append(same_row & (d >= -lo) & (d <= hi))
    return jnp.asarray(np.stack(mats), BF16)


def _pool_counts(t):
    rows = t // GRID_W
    r = np.arange(t) // GRID_W
    c = np.arange(t) % GRID_W
    out = []
    for w in POOL_WINDOWS:
        lo = w // 2
        hi = w - 1 - lo
        cnt_r = np.minimum(r + hi + 1, rows) - np.maximum(r - lo, 0)
        cnt_c = np.minimum(c + hi + 1, GRID_W) - np.maximum(c - lo, 0)
        out.append(np.broadcast_to((cnt_r * cnt_c).astype(np.float32)[:, None], (t, LANES)))
    return jnp.asarray(np.stack(out))


def _pool_call(xp, w_pool, pool_scale):
    b, t, pw = xp.shape
    ng = len(POOL_WINDOWS)
    assert max(POOL_WINDOWS) // 2 <= POOL_PAD_GRID_ROWS and t % GRID_W == 0
    cm = _pool_col_mats()
    cnt = _pool_counts(t)
    staged = t + 2 * POOL_PAD_GRID_ROWS * GRID_W
    const = lambda shape: pl.BlockSpec(shape, lambda i: (0,) * len(shape))
    return pl.pallas_call(
        _pool_kernel,
        grid=(b,),
        in_specs=[pl.BlockSpec((1, t, pw), lambda i: (i, 0, 0)),
                  const(cm.shape), const(cnt.shape), const((ng, LANES, LANES)), const((1, pw))],
        out_specs=pl.BlockSpec((1, t, pw), lambda i: (i, 0, 0)),
        out_shape=jax.ShapeDtypeStruct((b, t, pw), BF16),
        scratch_shapes=[pltpu.VMEM((staged, LANES), F32), pltpu.VMEM((staged, LANES), F32)],
        compiler_params=pltpu.CompilerParams(dimension_semantics=("arbitrary",)),
        name="pool",
    )(xp, cm, cnt, w_pool, pool_scale)


def _route_kernel(gla_ref, pool_ref, x_ref, mod_ref, wo_ref, nw_ref, wr_ref, br_ref, lt_ref,
                  h_ref, xt_ref, code_ref, wt_ref, cnt_ref, run_ref, wr2_ref, *, gw, n_exp):
    i = pl.program_id(0)

    @pl.when(i == 0)
    def _():
        run_ref[...] = jnp.zeros_like(run_ref)
        wh, wl = _split_bf16(wr_ref[...])
        wr2_ref[:, :LANES] = wh
        wr2_ref[:, LANES:] = wl

    m = mod_ref[0]
    acc = _dot(gla_ref[...], wo_ref[0:gw, :]) + _dot(pool_ref[...], wo_ref[gw:, :])
    h = x_ref[...] + m[2:3] * acc
    h_ref[...] = h
    xt = _rmsnorm(h, nw_ref[...]) * (1.0 + m[4:5]) + m[3:4]
    xt_ref[...] = _pack_bf16_pairs(xt)
    xh, xl = _split_bf16(xt)
    wr2 = wr2_ref[...]
    t1 = _dot(xh, wr2)
    logits = t1[:, :LANES] + t1[:, LANES:] + _dot(xl, wr2[:, :LANES]) + br_ref[...]
    lane = lax.broadcasted_iota(I32, logits.shape, 1)
    neg = jnp.float32(-jnp.inf)
    logits = jnp.where(lane < n_exp, logits, neg)
    vals, hots = [], []
    e_out = jnp.zeros(logits.shape, I32)
    lane_f = lane.astype(F32)
    for j in range(TOP_K):
        mx = jnp.max(logits, axis=-1, keepdims=True)
        idx = jnp.min(jnp.where(logits == mx, lane_f, float(LANES)), axis=-1, keepdims=True)
        hot = lane_f == idx
        vals.append(mx)
        hots.append(hot)
        e_out = jnp.where(lane == j, idx.astype(I32), e_out)
        logits = jnp.where(hot, neg, logits)
    ex = [jnp.exp(v - vals[0]) for v in vals]
    den = ex[0] + ex[1] + ex[2] + ex[3]
    w_out = jnp.zeros(logits.shape, F32)
    for j in range(TOP_K):
        w_out = jnp.where(lane == j, ex[j] / den, w_out)
    osum = jnp.where(hots[0] | hots[1] | hots[2] | hots[3], 1.0, 0.0)
    before = _dot(lt_ref[...], osum.astype(BF16)) + run_ref[0:1, :]
    rk_out = jnp.zeros(logits.shape, I32)
    for j in range(TOP_K):
        rj = jnp.sum(jnp.where(hots[j], before, 0.0), axis=-1, keepdims=True)
        rk_out = jnp.where(lane == j, rj.astype(I32), rk_out)
    run = run_ref[0:1, :] + jnp.sum(osum, axis=0, keepdims=True)
    run_ref[...] = jnp.broadcast_to(run, run_ref.shape)
    code = e_out * RANK_LIMIT + rk_out
    code_ref[...] = code.T[:code_ref.shape[0], :]
    wt_ref[...] = w_out
    cnt_ref[...] = jnp.broadcast_to(run, cnt_ref.shape).astype(I32)


def _route_call(gla, pool, x, mod, w_out, nw, wr, br, *, tm, n_exp, first_tile, n_tiles):
    n_all, d = x.shape
    n = n_tiles * tm
    gw = gla.shape[1]
    t_per_b = n_all // mod.shape[0]
    lt = jnp.asarray(np.tril(np.ones((tm, tm), np.float32), -1), BF16)
    row_in = lambda width: pl.BlockSpec((tm, width), lambda i: (i + first_tile, 0))
    row = lambda width: pl.BlockSpec((tm, width), lambda i: (i, 0))
    const = lambda shape: pl.BlockSpec(shape, lambda i: (0,) * len(shape))
    sds = jax.ShapeDtypeStruct
    return pl.pallas_call(
        functools.partial(_route_kernel, gw=gw, n_exp=n_exp),
        grid=(n_tiles,),
        in_specs=[row_in(gw), row_in(pool.shape[1]), row_in(d),
                  pl.BlockSpec((1, N_MOD, d), lambda i: ((i + first_tile) * tm // t_per_b, 0, 0)),
                  const(w_out.shape), const((1, d)), const(wr.shape), const((1, LANES)), const((tm, tm))],
        out_specs=[row(d), row(d // 2), pl.BlockSpec((8, tm), lambda i: (0, i)), row(LANES), const((8, LANES))],
        out_shape=[sds((n, d), F32), sds((n, d // 2), U32), sds((8, n), I32),
                   sds((n, LANES), F32), sds((8, LANES), I32)],
        scratch_shapes=[pltpu.VMEM((8, LANES), F32), pltpu.VMEM((d, 2 * LANES), BF16)],
        compiler_params=pltpu.CompilerParams(dimension_semantics=("arbitrary",)),
        name="route",
    )(gla, pool, x, mod, w_out, nw, wr, br, lt)


def _plan_kernel(cnt_ref, code_ref, dest_ref, be_ref, nv_ref, first_ref, next_ref, slot_ref, nu_ref, start_ref,
                 *, n_exp, rows):
    @pl.when(pl.program_id(0) == 0)
    def _():
        blk = (lax.broadcasted_iota(I32, be_ref.shape, 0) * LANES + lax.broadcasted_iota(I32, be_ref.shape, 1))
        blk_row0 = blk * rows
        nxt_e = [None] * n_exp
        nxt = jnp.int32(-1)
        for e in reversed(range(n_exp)):
            nxt_e[e] = nxt
            nxt = jnp.where(cnt_ref[e] > 0, e, nxt)
        zeros = jnp.zeros(be_ref.shape, I32)
        be, end_valid, first, nxt_blk, slot = zeros, zeros, zeros, zeros - 1, zeros
        acc = jnp.int32(0)
        last_e = jnp.int32(0)
        ordinal = jnp.int32(0)
        for e in range(n_exp):
            c = cnt_ref[e]
            start_ref[e] = acc
            in_e = (blk_row0 >= acc) & (c > 0)
            end_valid = jnp.where(in_e, acc + c, end_valid)
            be = jnp.where(in_e, e, be)
            first = jnp.where(in_e, (blk_row0 == acc).astype(I32), first)
            nxt_blk = jnp.where(in_e, nxt_e[e], nxt_blk)
            slot = jnp.where(in_e, ordinal % 2, slot)
            acc = acc + (c + rows - 1) // rows * rows
            last_e = jnp.where(c > 0, e, last_e)
            ordinal = ordinal + (c > 0).astype(I32)
        n_used = acc // rows
        nu_ref[0] = n_used
        be_ref[...] = jnp.where(blk < n_used, be, last_e)
        nv_ref[...] = jnp.clip(end_valid - blk_row0, 0, rows)
        first_ref[...] = first
        next_ref[...] = nxt_blk
        slot_ref[...] = slot

    code = code_ref[...]
    e_vec = code // RANK_LIMIT
    dest = code % RANK_LIMIT
    for e in range(n_exp):
        dest = dest + jnp.where(e_vec == e, start_ref[e], 0)
    dest_ref[...] = dest


def _plan_call(counts, code_t, *, n_exp, rows, chunk):
    n = code_t.shape[1]
    sds = jax.ShapeDtypeStruct
    smem = pltpu.SMEM
    return pl.pallas_call(
        functools.partial(_plan_kernel, n_exp=n_exp, rows=rows),
        grid=(n // chunk,),
        in_specs=[pl.BlockSpec(memory_space=smem), pl.BlockSpec((8, chunk), lambda i: (0, i))],
        out_specs=[pl.BlockSpec((8, chunk), lambda i: (0, i))]
        + [pl.BlockSpec((8, LANES), lambda i: (0, 0))] * 5 + [pl.BlockSpec(memory_space=smem)],
        out_shape=[sds((8, n), I32)] + [sds((8, LANES), I32)] * 5 + [sds((1,), I32)],
        scratch_shapes=[pltpu.SMEM((n_exp,), I32)],
        compiler_params=pltpu.CompilerParams(dimension_semantics=("arbitrary",)),
        name="plan",
    )(counts, code_t)


def _sc_worker_id():
    return lax.axis_index("s") * SC_CORES + lax.axis_index("c")


def _sc_scatter_call(x, idx3, *, n_out):
    n, d = x.shape
    n_win_total, k, w = idx3.shape
    n_win = n_win_total // SC_WORKERS
    mesh = plsc.VectorSubcoreMesh(core_axis_name="c", subcore_axis_name="s")

    @functools.partial(
        pl.kernel, mesh=mesh,
        out_type=jax.ShapeDtypeStruct((n_out, d), x.dtype),
        scratch_types=[pltpu.VMEM((k, w), I32), pltpu.VMEM((w, d), x.dtype), pltpu.SemaphoreType.DMA],
        name="sc_dispatch",
    )
    def kern(x_hbm, idx_hbm, out_hbm, idx_v, rows_v, sem):
        wid = _sc_worker_id()

        @pl.loop(0, n_win)
        def _(i):
            win = wid * n_win + i
            pltpu.sync_copy(idx_hbm.at[win], idx_v)
            pltpu.sync_copy(x_hbm.at[pl.ds(win * w, w)], rows_v)
            for j in range(k):
                pltpu.async_copy(rows_v, out_hbm.at[idx_v.at[j]], sem).wait()

    return kern(x, idx3)


def _sc_gather_call(table, idx3):
    n_workers, n_win, w = idx3.shape
    d = table.shape[1]
    assert n_workers == SC_WORKERS and n_win % 2 == 0
    mesh = plsc.VectorSubcoreMesh(core_axis_name="c", subcore_axis_name="s")

    @functools.partial(
        pl.kernel, mesh=mesh,
        out_type=jax.ShapeDtypeStruct((n_workers * n_win * w, d), table.dtype),
        scratch_types=[pltpu.VMEM((n_win, w), I32), pltpu.VMEM((2, w, d), table.dtype),
                       pltpu.SemaphoreType.DMA((2,)), pltpu.SemaphoreType.DMA((2,))],
        name="sc_gather",
    )
    def kern(table_hbm, idx_hbm, out_hbm, idx_v, rows_v, gsem, osem):
        wid = _sc_worker_id()
        base = wid * n_win
        pltpu.sync_copy(idx_hbm.at[wid], idx_v)

        def gather(wi, b):
            return pltpu.make_async_copy(table_hbm.at[idx_v.at[wi]], rows_v.at[b], gsem.at[b])

        def put(wi, b):
            return pltpu.make_async_copy(rows_v.at[b], out_hbm.at[pl.ds((base + wi) * w, w)], osem.at[b])

        gather(0, 0).start()

        @pl.loop(0, n_win, step=2)
        def _(i):
            for b in range(2):
                wi = i + b

                @pl.when(wi + 1 < n_win)
                def _():
                    @pl.when(wi >= 1)
                    def _():
                        put(wi - 1, 1 - b).wait()
                    gather(wi + 1, 1 - b).start()

                gather(wi, b).wait()
                put(wi, b).start()

        put(n_win - 2, 0).wait()
        put(n_win - 1, 1).wait()

    return kern(table, idx3)


def _expert_kernel(be_ref, nu_ref, nv_ref, first_ref, next_ref, slot_ref,
                   x_ref, wgu_hbm, bgu_ref, wd_hbm, bd_ref, y_ref,
                   wgu_f32_ref, wd_f32_ref, wgu_bf_ref, wd_bf_ref, sem, *, d_ff):
    i = pl.program_id(0)
    slot = slot_ref[i]

    def fetch(e, s):
        return (pltpu.make_async_copy(wgu_hbm.at[e], wgu_f32_ref.at[s], sem.at[0, s]),
                pltpu.make_async_copy(wd_hbm.at[e], wd_f32_ref.at[s], sem.at[1, s]))

    @pl.when(i == 0)
    def _():
        for cp in fetch(be_ref[0], slot):
            cp.start()

    @pl.when(first_ref[i] == 1)
    def _():
        for cp in fetch(be_ref[i], slot):
            cp.wait()

        @pl.when(next_ref[i] >= 0)
        def _():
            for cp in fetch(next_ref[i], 1 - slot):
                cp.start()

        wgu_bf_ref[...] = wgu_f32_ref[slot].astype(BF16)
        wd_bf_ref[...] = wd_f32_ref[slot].astype(BF16)

    n_valid = nv_ref[i]
    n_rows = x_ref.shape[0]

    def expert_rows(rows):
        row = lax.broadcasted_iota(I32, (rows, x_ref.shape[1]), 0)
        lo, hi = _unpack_bf16_pairs(jnp.where(row < n_valid, x_ref[0:rows, :], jnp.uint32(0)))
        xb = jnp.concatenate([lo, hi], axis=1).astype(BF16)
        col = lax.broadcasted_iota(I32, bgu_ref.shape[1:], 1)
        gu = _dot(xb, wgu_bf_ref[...]) + (bgu_ref[0] + jnp.where(col >= d_ff, 1.0, 0.0))
        gate = jnp.minimum(gu[:, :d_ff], SWIGLU_LIMIT)
        up1 = jnp.clip(gu[:, d_ff:], 1.0 - SWIGLU_LIMIT, 1.0 + SWIGLU_LIMIT)
        act = up1 * (0.5 * gate) * (1.0 + jnp.tanh((0.5 * SWIGLU_ALPHA) * gate))
        y_ref[0:rows, :] = _pack_bf16_pairs(_dot(act.astype(BF16), wd_bf_ref[...]) + bd_ref[0])

    in_use = i < nu_ref[0]

    for rows in range(EXPERT_ROW_STEP, n_rows + 1, EXPERT_ROW_STEP):
        @pl.when(in_use & (n_valid > rows - EXPERT_ROW_STEP) & (n_valid <= rows))
        def _(rows=rows):
            expert_rows(rows)
            if rows < n_rows:
                y_ref[rows:, :] = jnp.zeros((n_rows - rows, y_ref.shape[1]), y_ref.dtype)


def _expert_call(plan, xs, w_gu, b_gu, w_down, b_down):
    block_e, n_valid, first, nxt, slot, n_used = plan
    n_pad = xs.shape[0]
    n_exp, d, two_ff = w_gu.shape
    d_ff = two_ff // 2
    nblk = n_pad // EXPERT_ROWS
    rows = lambda i, be, nu, *_: (jnp.minimum(i, nu[0] - 1), 0)
    per_e = lambda i, be, *_: (be[i], 0, 0)
    grid_spec = pltpu.PrefetchScalarGridSpec(
        num_scalar_prefetch=6,
        grid=(nblk,),
        in_specs=[pl.BlockSpec((EXPERT_ROWS, d // 2), rows),
                  pl.BlockSpec(memory_space=pl.ANY), pl.BlockSpec((1, 1, two_ff), per_e),
                  pl.BlockSpec(memory_space=pl.ANY), pl.BlockSpec((1, 1, d), per_e)],
        out_specs=pl.BlockSpec((EXPERT_ROWS, d // 2), rows),
        scratch_shapes=[pltpu.VMEM((2, d, two_ff), F32), pltpu.VMEM((2, d_ff, d), F32),
                        pltpu.VMEM((d, two_ff), BF16), pltpu.VMEM((d_ff, d), BF16),
                        pltpu.SemaphoreType.DMA((2, 2))],
    )
    flat = lambda a: a.reshape(-1)
    return pl.pallas_call(
        functools.partial(_expert_kernel, d_ff=d_ff),
        grid_spec=grid_spec,
        out_shape=jax.ShapeDtypeStruct((n_pad, d // 2), U32),
        compiler_params=pltpu.CompilerParams(dimension_semantics=("arbitrary",),
                                             vmem_limit_bytes=EXPERT_VMEM_BYTES),
        name="experts",
    )(flat(block_e), n_used, flat(n_valid), flat(first), flat(nxt), flat(slot),
      xs, w_gu, b_gu.reshape(n_exp, 1, two_ff), w_down, b_down.reshape(n_exp, 1, d))


def _combine_kernel(y4_ref, wt_ref, h_ref, mod_ref, fw_ref, o_ref):
    wt = wt_ref[...]
    acc_lo, acc_hi = None, None
    for j in range(TOP_K):
        lo, hi = _unpack_bf16_pairs(y4_ref[j])
        w = wt[:, j:j + 1]
        acc_lo = w * lo if j == 0 else acc_lo + w * lo
        acc_hi = w * hi if j == 0 else acc_hi + w * hi
    acc = jnp.concatenate([acc_lo, acc_hi], axis=1)
    m = mod_ref[0]
    o_ref[...] = _rmsnorm(h_ref[...] + m[5:6] * acc, fw_ref[...])


def _combine_call(y4, wts, h, mod, fw, prev_out, *, n, tg, first_tile):
    d = h.shape[1]
    t_per_b = n // mod.shape[0]
    part = lambda width: pl.BlockSpec((tg, width), lambda i: (i, 0))
    row = lambda width: pl.BlockSpec((tg, width), lambda i: (i + first_tile, 0))
    in_specs = [pl.BlockSpec((TOP_K, tg, d // 2), lambda i: (0, i, 0)),
                part(LANES), part(d),
                pl.BlockSpec((1, N_MOD, d), lambda i: ((i + first_tile) * tg // t_per_b, 0, 0)),
                pl.BlockSpec((1, d), lambda i: (0, 0))]
    args = [y4, wts, h, mod, fw]
    kern = _combine_kernel
    aliases = {}
    if prev_out is not None:
        in_specs.append(pl.BlockSpec(memory_space=pl.ANY))
        args.append(prev_out)
        kern = lambda y4_ref, wt_ref, h_ref, mod_ref, fw_ref, prev_ref, o_ref: _combine_kernel(
            y4_ref, wt_ref, h_ref, mod_ref, fw_ref, o_ref)
        aliases = {len(args) - 1: 0}
    return pl.pallas_call(
        kern,
        grid=(y4.shape[1] // tg,),
        in_specs=in_specs,
        out_specs=row(d),
        out_shape=jax.ShapeDtypeStruct((n, d), F32),
        input_output_aliases=aliases,
        compiler_params=pltpu.CompilerParams(dimension_semantics=("arbitrary",)),
        name="combine",
    )(*args)


def kernel(x, c, ctx, c_ctx, w_ada, b_ada, norm_mix_w, norm_mlp_w, w_in, w_gk_f, b_gk_f, w_gk_b, b_gk_b,
           gla_norm_w, w_pool, pool_scale, w_out, w_router, b_router, w_gu, b_gu, w_down, b_down,
           final_norm_w):
    b, t, d = x.shape
    assert w_ada.shape[0] == 1, "single-layer trunk"
    n_exp = w_router.shape[2]
    rank = w_gk_f.shape[1]
    qk = w_gk_f.shape[2]
    dk = qk // GLA_HEADS
    gw = GLA_HEADS * gla_norm_w.shape[1]
    pw = w_pool.shape[1] * w_pool.shape[2]
    assert w_in.shape[2] == 2 * qk + 2 * gw + 2 * rank + pw and 2 * rank <= LANES
    assert t % SUPER == 0 and ctx.shape[1] % SUPER == 0 and n_exp <= LANES

    rows = -(-(b + 1) // 8) * 8
    cc = jnp.concatenate([c, c_ctx[None, :], jnp.zeros((rows - b - 1, d), F32)], axis=0)
    mod = _mod_call(cc, w_ada[0], b_ada)
    mod_x = mod[:b].reshape(b, N_MOD, d)
    mod_c = mod[b:b + 1].reshape(1, N_MOD, d)

    wi = w_in[0]
    o_r = 2 * qk + 2 * gw
    w_cat = jnp.concatenate([wi[:, :o_r], wi[:, o_r + 2 * rank:], wi[:, o_r:o_r + 2 * rank],
                             jnp.zeros((d, LANES - 2 * rank), F32)], axis=1).astype(BF16)
    wgk = jnp.concatenate([jnp.pad(w_gk_f[0], ((0, 0), (0, qk))), jnp.pad(w_gk_b[0], ((0, 0), (qk, 0))),
                           jnp.zeros((LANES - 2 * rank, 2 * qk), F32)], axis=0)
    bgk = jnp.concatenate([b_gk_f[0], b_gk_b[0]])[None, :]
    proj = functools.partial(_inproj_call, nw=norm_mix_w, w=w_cat, wgk=wgk.astype(BF16), bgk=bgk,
                             qk=qk, gw=gw, pw=pw, dk=dk)
    q, k, v, vt, g, xp, gk = proj(x, mod_x, tm=1024)
    _, kc, _, vtc, _, _, gkc = proj(ctx, mod_c, tm=SUPER)

    gla = _gla_call(q, k, v, vt, gk, g, kc, vtc, gkc, gla_norm_w)
    pool = _pool_call(xp, w_pool[0].astype(BF16), pool_scale)

    n = b * t
    wr = jnp.pad(w_router[0], ((0, 0), (0, LANES - n_exp)))
    br = jnp.pad(b_router, ((0, 0), (0, LANES - n_exp)))
    n_part = n // MOE_PARTS
    tm, tg = 512, 1024
    assert n % MOE_PARTS == 0 and n_part % (SC_WORKERS * SC_WINDOW) == 0 and n_part % tm == 0 and n_part < RANK_LIMIT
    assert (n_part * TOP_K) % (SC_WORKERS * 2 * SC_GATHER_WINDOW) == 0
    n_pad = n_part * TOP_K + n_exp * EXPERT_ROWS
    nblk = n_pad // EXPERT_ROWS
    assert nblk <= 8 * LANES
    parts = range(MOE_PARTS)
    routed = [_route_call(gla.reshape(n, gw), pool.reshape(n, pw), x.reshape(n, d), mod_x, w_out[0].astype(BF16),
                          norm_mlp_w, wr, br, tm=tm, n_exp=n_exp,
                          first_tile=p * n_part // tm, n_tiles=n_part // tm) for p in parts]
    plans = [_plan_call(cnt[0, :n_exp], code_t, n_exp=n_exp, rows=EXPERT_ROWS, chunk=min(n_part, 4096))
             for (_, _, code_t, _, cnt) in routed]
    dests = [plan[0][:TOP_K] for plan in plans]
    xs = [_sc_scatter_call(routed[p][1],
                           dests[p].reshape(TOP_K, n_part // SC_WINDOW, SC_WINDOW).transpose(1, 0, 2), n_out=n_pad)
          for p in parts]
    ys = [_expert_call(plans[p][1:], xs[p], w_gu[0], b_gu[0], w_down[0], b_down[0]) for p in parts]
    y4 = [_sc_gather_call(ys[p], dests[p].reshape(SC_WORKERS, -1, SC_GATHER_WINDOW)).reshape(TOP_K, n_part, d // 2)
          for p in parts]
    out = None
    for p in parts:
        out = _combine_call(y4[p], routed[p][3], routed[p][0], mod_x, final_norm_w[None, :], out,
                            n=n, tg=tg, first_tile=p * n_part // tg)
    return out.reshape(b, t, d)
```

```python
import functools

import numpy as np
import jax
import jax.numpy as jnp
from jax import lax
from jax.experimental import pallas as pl
from jax.experimental.pallas import tpu as pltpu
from jax.experimental.pallas import tpu_sc as plsc

F32 = jnp.float32
BF16 = jnp.bfloat16
I32 = jnp.int32
U32 = jnp.uint32

GRID_W = 64
GLA_HEADS = 4
GLA_CHUNK = 64
GATE_NORMALIZER = 16.0
POOL_WINDOWS = (2, 4, 8, 16)
POOL_PAD_GRID_ROWS = 8
TOP_K = 4
RANK_LIMIT = 1 << 20
SWIGLU_LIMIT = 7.0
SWIGLU_ALPHA = 1.702
N_MOD = 6
EPS = 1e-6

LANES = 128
SUPER = 256
HEAD_PAIR_DK = 128
EXPERT_ROWS = 512
EXPERT_ROW_STEP = 128
MOE_PARTS = 2
EXPERT_VMEM_BYTES = 56 * 1024 * 1024
SC_CORES = 2
SC_SUBCORES = 16
SC_WORKERS = SC_CORES * SC_SUBCORES
SC_WINDOW = 32
SC_GATHER_WINDOW = 64


def _dot(a, b):
    return jnp.dot(a, b, preferred_element_type=F32)


def _dot_nt(a, b):
    return lax.dot_general(a, b, (((1,), (1,)), ((), ())), preferred_element_type=F32)


def _split_bf16(x):
    hi = x.astype(BF16)
    lo = (x - hi.astype(F32)).astype(BF16)
    return hi, lo


def _pack_bf16_pairs(x):
    c = x.shape[1] // 2
    lo = lax.bitcast_convert_type(x[:, :c].astype(BF16).astype(F32), U32)
    hi = lax.bitcast_convert_type(x[:, c:].astype(BF16).astype(F32), U32)
    return (lo >> 16) | hi


def _unpack_bf16_pairs(p):
    lo = lax.bitcast_convert_type(p << 16, F32)
    hi = lax.bitcast_convert_type(p & jnp.uint32(0xFFFF0000), F32)
    return lo, hi


def _rmsnorm(x, w):
    var = jnp.mean(x * x, axis=-1, keepdims=True)
    return x * lax.rsqrt(var + EPS) * w


def _mod_kernel(c_ref, w_ref, b_ref, o_ref):
    c = c_ref[...]
    s = c * jax.nn.sigmoid(c)
    o_ref[...] = jnp.dot(s, w_ref[...], precision=lax.Precision.HIGHEST,
                         preferred_element_type=F32) + b_ref[...]


def _mod_call(cc, w_ada, b_ada):
    rows, d = cc.shape
    n = w_ada.shape[1]
    tn = 1024
    return pl.pallas_call(
        _mod_kernel,
        grid=(n // tn,),
        in_specs=[pl.BlockSpec((rows, d), lambda j: (0, 0)),
                  pl.BlockSpec((d, tn), lambda j: (0, j)),
                  pl.BlockSpec((1, tn), lambda j: (0, j))],
        out_specs=pl.BlockSpec((rows, tn), lambda j: (0, j)),
        out_shape=jax.ShapeDtypeStruct((rows, n), F32),
        name="mod",
    )(cc, w_ada, b_ada)


def _inproj_kernel(x_ref, mod_ref, nw_ref, w_ref, wgk_ref, bgk_ref,
                   q_ref, k_ref, v_ref, vt_ref, g_ref, p_ref, gk_ref, *, qk, gw, pw, dk):
    x = x_ref[0]
    m = mod_ref[0]
    hm = (_rmsnorm(x, nw_ref[...]) * (1.0 + m[1:2]) + m[0:1]).astype(BF16)
    p = _dot(hm, w_ref[...])
    vt = p[:, 2 * qk:2 * qk + gw].T
    for s in range(vt_ref.shape[1]):
        vt_ref[0, s] = vt[:, s * SUPER:(s + 1) * SUPER].astype(BF16)
    o = 0
    q_ref[0] = (p[:, o:o + qk] * (dk ** -0.5)).astype(BF16); o += qk
    k_ref[0] = p[:, o:o + qk].astype(BF16); o += qk
    v_ref[0] = p[:, o:o + gw].astype(BF16); o += gw
    g_ref[0] = p[:, o:o + gw].astype(BF16); o += gw
    p_ref[0] = p[:, o:o + pw].astype(BF16); o += pw
    r = p[:, o:o + LANES]
    z = _dot(r.astype(BF16), wgk_ref[...]) + bgk_ref[...]
    gk_ref[0] = (jnp.minimum(z, 0.0) - jnp.log1p(jnp.exp(-jnp.abs(z)))) * (1.0 / GATE_NORMALIZER)


def _inproj_call(x, mod, nw, w, wgk, bgk, *, qk, gw, pw, dk, tm):
    b, t, d = x.shape
    n_in = w.shape[1]
    bs = lambda width: pl.BlockSpec((1, tm, width), lambda i, j: (i, j, 0))
    const = lambda shape: pl.BlockSpec(shape, lambda i, j: (0,) * len(shape))
    per_batch = mod.shape[0] > 1
    sds = jax.ShapeDtypeStruct
    return pl.pallas_call(
        functools.partial(_inproj_kernel, qk=qk, gw=gw, pw=pw, dk=dk),
        grid=(b, t // tm),
        in_specs=[bs(d),
                  pl.BlockSpec((1, N_MOD, d), (lambda i, j: (i, 0, 0)) if per_batch else (lambda i, j: (0, 0, 0))),
                  const((1, d)), const((d, n_in)), const((LANES, 2 * qk)), const((1, 2 * qk))],
        out_specs=[bs(qk), bs(qk), bs(gw),
                   pl.BlockSpec((1, tm // SUPER, gw, SUPER), lambda i, j: (i, j, 0, 0)),
                   bs(gw), bs(pw), bs(2 * qk)],
        out_shape=[sds((b, t, qk), BF16), sds((b, t, qk), BF16), sds((b, t, gw), BF16),
                   sds((b, t // SUPER, gw, SUPER), BF16),
                   sds((b, t, gw), BF16), sds((b, t, pw), BF16), sds((b, t, 2 * qk), F32)],
        compiler_params=pltpu.CompilerParams(dimension_semantics=("arbitrary", "arbitrary")),
        name="inproj",
    )(x, mod, nw, w, wgk, bgk)


def _chunk_row(x, r):
    return jnp.concatenate(
        [jnp.broadcast_to(x[c * GLA_CHUNK + r:c * GLA_CHUNK + r + 1, :], (GLA_CHUNK, x.shape[1]))
         for c in range(x.shape[0] // GLA_CHUNK)], axis=0)


def _decay_sums(gk_f, gk_b, cm):
    n = gk_f.shape[1]
    hi, lo = _split_bf16(jnp.concatenate([gk_f, gk_b], axis=1))
    pre = _dot(cm, hi) + _dot(cm, lo)
    pre_b = pre[:, n:]
    return pre[:, :n], _chunk_row(pre_b, GLA_CHUNK - 1) - pre_b + gk_b


def _gla_super(q, k, v, vt, bcum, amask, bd_mask, st, fwd, want_out):
    nch = SUPER // GLA_CHUNK
    order = tuple(range(nch)) if fwd else tuple(reversed(range(nch)))
    last_row = GLA_CHUNK - 1 if fwd else 0
    mid_row = GLA_CHUNK // 2 - 1 if fwd else GLA_CHUNK // 2
    chunk_row = functools.partial(_chunk_row, bcum)
    chunk_of_row = lax.broadcasted_iota(I32, bcum.shape, 0) // GLA_CHUNK

    def by_chunk(x):
        return jnp.concatenate([jnp.where(chunk_of_row == c, x, 0.0).astype(BF16) for c in range(nch)], axis=1)

    blast = chunk_row(last_row)
    u_all = _dot(vt, by_chunk(k * jnp.exp(blast - bcum)))
    before = [None] * nch
    for c in order:
        before[c] = st
        decay = jnp.exp(bcum[c * GLA_CHUNK + last_row:c * GLA_CHUNK + last_row + 1, :])
        st = st * decay + jnp.where(bd_mask, u_all[:, c * HEAD_PAIR_DK:(c + 1) * HEAD_PAIR_DK], 0.0)
    if not want_out:
        return None, st
    bmid = chunk_row(mid_row)
    qt = q * jnp.exp(bcum - bmid)
    kt = (k * jnp.exp(bmid - bcum)).astype(BF16)
    lane = lax.broadcasted_iota(I32, qt.shape, 1)
    half = HEAD_PAIR_DK // 2
    o_heads = []
    for hh in range(2):
        sel = (lane < half) if hh == 0 else (lane >= half)
        a = _dot_nt(jnp.where(sel, qt, 0.0).astype(BF16), kt)
        a = jnp.where(amask, a, 0.0).astype(BF16)
        o_heads.append(_dot(a, v[:, hh * LANES:(hh + 1) * LANES]))
    qh = (q * jnp.exp(bcum)).astype(BF16)
    o_inter = jnp.concatenate(
        [_dot_nt(qh[c * GLA_CHUNK:(c + 1) * GLA_CHUNK], before[c].astype(BF16)) for c in range(nch)], axis=0)
    return jnp.concatenate(o_heads, axis=1) + o_inter, st


def _gla_kernel(q_ref, k_ref, v_ref, vt_ref, gkf_ref, gkb_ref, g_ref, kc_ref, vtc_ref, gkfc_ref, gkbc_ref,
                nw_ref, cmf_ref, cmb_ref, o_ref, stf_ref, stb_ref, of_ref, ob_ref):
    t = q_ref.shape[1]
    tc = kc_ref.shape[1]
    nsc, nscc = t // SUPER, tc // SUPER
    cmf = cmf_ref[...]
    cmb = cmb_ref[...]
    amask_f = cmf > 0
    amask_b = cmb > 0
    row = lax.broadcasted_iota(I32, (2 * LANES, HEAD_PAIR_DK), 0)
    lane = lax.broadcasted_iota(I32, (2 * LANES, HEAD_PAIR_DK), 1)
    bd_mask = (row < LANES) == (lane < HEAD_PAIR_DK // 2)

    def group_rows(j):
        start = j * SUPER
        return pl.ds(start if isinstance(start, int) else pl.multiple_of(start, SUPER), SUPER)

    def ctx_state(bcum, fwd, j, st):
        return _gla_super(None, kc_ref[0, group_rows(j), :].astype(F32), None, vtc_ref[0, j], bcum,
                          None, bd_mask, st, fwd, False)[1]

    def latent(bcum, amask, fwd, j, st):
        rows = group_rows(j)
        return _gla_super(q_ref[0, rows, :].astype(F32), k_ref[0, rows, :].astype(F32), v_ref[0, rows, :],
                          vt_ref[0, j], bcum, amask, bd_mask, st, fwd, True)

    stf = jnp.zeros(stf_ref.shape, F32)
    stb = jnp.zeros(stb_ref.shape, F32)
    for j in range(nscc):
        jb = nscc - 1 - j
        bcum_f, bcum_b = _decay_sums(gkfc_ref[0, group_rows(j), :], gkbc_ref[0, group_rows(jb), :], cmf)
        stf = ctx_state(bcum_f, True, j, stf)
        stb = ctx_state(bcum_b, False, jb, stb)
    stf_ref[...] = stf
    stb_ref[...] = stb

    def scan_body(jj, carry):
        jb = nsc - 1 - jj
        bcum_f, bcum_b = _decay_sums(gkf_ref[0, group_rows(jj), :], gkb_ref[0, group_rows(jb), :], cmf)
        of, stf = latent(bcum_f, amask_f, True, jj, stf_ref[...])
        of_ref[pl.ds(pl.multiple_of(jj * SUPER, SUPER), SUPER), :] = of
        stf_ref[...] = stf
        ob, stb = latent(bcum_b, amask_b, False, jb, stb_ref[...])
        ob_ref[pl.ds(pl.multiple_of(jb * SUPER, SUPER), SUPER), :] = ob
        stb_ref[...] = stb
        return carry

    lax.fori_loop(0, nsc, scan_body, 0, unroll=8)

    nw = nw_ref[...]

    def out_body(j, carry):
        rows = pl.ds(pl.multiple_of(j * SUPER, SUPER), SUPER)
        o = of_ref[rows, :] + ob_ref[rows, :]
        g = g_ref[0, rows, :].astype(F32)
        half_g = 0.5 * g
        gate = half_g * (1.0 + jnp.tanh(half_g))
        for hh in range(2):
            oh = o[:, hh * LANES:(hh + 1) * LANES]
            on = oh * lax.rsqrt(jnp.mean(oh * oh, axis=-1, keepdims=True) + EPS) * nw
            o_ref[0, rows, hh * LANES:(hh + 1) * LANES] = (on * gate[:, hh * LANES:(hh + 1) * LANES]).astype(BF16)
        return carry

    lax.fori_loop(0, nsc, out_body, 0)


def _gla_masks():
    i = np.arange(SUPER)
    same = (i[:, None] // GLA_CHUNK) == (i[None, :] // GLA_CHUNK)
    fwd = same & (i[None, :] <= i[:, None])
    bwd = same & (i[None, :] >= i[:, None])
    return jnp.asarray(fwd, BF16), jnp.asarray(bwd, BF16)


def _gla_call(q, k, v, vt, gk, g, kc, vtc, gkc, nw):
    b, t, qk = q.shape
    tc = kc.shape[1]
    npair = qk // HEAD_PAIR_DK
    cmf, cmb = _gla_masks()
    lat = lambda width, off: pl.BlockSpec((1, t, width), lambda i, j: (i, 0, j + off))
    ctx = lambda width, off: pl.BlockSpec((1, tc, width), lambda i, j: (i, 0, j + off))
    tr = lambda n_groups: pl.BlockSpec((1, n_groups, 2 * LANES, SUPER), lambda i, j: (i, 0, j, 0))
    const = lambda shape: pl.BlockSpec(shape, lambda i, j: (0,) * len(shape))
    return pl.pallas_call(
        _gla_kernel,
        grid=(b, npair),
        in_specs=[lat(HEAD_PAIR_DK, 0), lat(HEAD_PAIR_DK, 0), lat(2 * LANES, 0), tr(t // SUPER),
                  lat(HEAD_PAIR_DK, 0), lat(HEAD_PAIR_DK, npair), lat(2 * LANES, 0),
                  ctx(HEAD_PAIR_DK, 0), tr(tc // SUPER), ctx(HEAD_PAIR_DK, 0), ctx(HEAD_PAIR_DK, npair),
                  const((1, LANES)), const(cmf.shape), const(cmb.shape)],
        out_specs=lat(2 * LANES, 0),
        out_shape=jax.ShapeDtypeStruct((b, t, v.shape[2]), BF16),
        scratch_shapes=[pltpu.VMEM((2 * LANES, HEAD_PAIR_DK), F32), pltpu.VMEM((2 * LANES, HEAD_PAIR_DK), F32),
                        pltpu.VMEM((t, 2 * LANES), F32), pltpu.VMEM((t, 2 * LANES), F32)],
        compiler_params=pltpu.CompilerParams(dimension_semantics=("arbitrary", "arbitrary")),
        name="gla",
    )(q, k, v, vt, gk, gk, g, kc, vtc, gkc, gkc, nw, cmf, cmb)


def _pool_kernel(x_ref, cm_ref, cnt_ref, wp_ref, ps_ref, o_ref, a_ref, b_ref):
    t = x_ref.shape[1]
    pad = POOL_PAD_GRID_ROWS * GRID_W
    total = t + 2 * pad
    for gi, w in enumerate(POOL_WINDOWS):
        lo = w // 2
        cols = slice(gi * LANES, (gi + 1) * LANES)
        cmat = cm_ref[gi]
        a_ref[0:pad, :] = jnp.zeros((pad, LANES), F32)
        a_ref[pad + t:total, :] = jnp.zeros((pad, LANES), F32)
        for blk in range(t // SUPER):
            rs = slice(blk * SUPER, (blk + 1) * SUPER)
            a_ref[pad + blk * SUPER:pad + (blk + 1) * SUPER, :] = _dot(cmat, x_ref[0, rs, cols])
        src, dst = a_ref, b_ref
        m = 1
        while m < w:
            sh = m * GRID_W
            dst[0:total - sh, :] = src[0:total - sh, :] + src[sh:total, :]
            src, dst = dst, src
            m *= 2
        first = pad - lo * GRID_W
        pooled = src[first:first + t, :] / cnt_ref[gi] - x_ref[0, :, cols].astype(F32)
        yp = _dot(pooled.astype(BF16), wp_ref[gi]) * ps_ref[:, cols]
        o_ref[0, :, cols] = yp.astype(BF16)


def _pool_col_mats():
    i = np.arange(SUPER)
    same_row = (i[:, None] // GRID_W) == (i[None, :] // GRID_W)
    d = i[None, :] - i[:, None]
    mats = []
    for w in POOL_WINDOWS:
        lo = w // 2
        hi = w - 1 - lo
        mats.append(same_row & (d >= -lo) & (d <= hi))
    return jnp.asarray(np.stack(mats), BF16)


def _pool_counts(t):
    rows = t // GRID_W
    r = np.arange(t) // GRID_W
    c = np.arange(t) % GRID_W
    out = []
    for w in POOL_WINDOWS:
        lo = w // 2
        hi = w - 1 - lo
        cnt_r = np.minimum(r + hi + 1, rows) - np.maximum(r - lo, 0)
        cnt_c = np.minimum(c + hi + 1, GRID_W) - np.maximum(c - lo, 0)
        out.append(np.broadcast_to((cnt_r * cnt_c).astype(np.float32)[:, None], (t, LANES)))
    return jnp.asarray(np.stack(out))


def _pool_call(xp, w_pool, pool_scale):
    b, t, pw = xp.shape
    ng = len(POOL_WINDOWS)
    assert max(POOL_WINDOWS) // 2 <= POOL_PAD_GRID_ROWS and t % GRID_W == 0
    cm = _pool_col_mats()
    cnt = _pool_counts(t)
    staged = t + 2 * POOL_PAD_GRID_ROWS * GRID_W
    const = lambda shape: pl.BlockSpec(shape, lambda i: (0,) * len(shape))
    return pl.pallas_call(
        _pool_kernel,
        grid=(b,),
        in_specs=[pl.BlockSpec((1, t, pw), lambda i: (i, 0, 0)),
                  const(cm.shape), const(cnt.shape), const((ng, LANES, LANES)), const((1, pw))],
        out_specs=pl.BlockSpec((1, t, pw), lambda i: (i, 0, 0)),
        out_shape=jax.ShapeDtypeStruct((b, t, pw), BF16),
        scratch_shapes=[pltpu.VMEM((staged, LANES), F32), pltpu.VMEM((staged, LANES), F32)],
        compiler_params=pltpu.CompilerParams(dimension_semantics=("arbitrary",)),
        name="pool",
    )(xp, cm, cnt, w_pool, pool_scale)


def _route_kernel(gla_ref, pool_ref, x_ref, mod_ref, wo_ref, nw_ref, wr_ref, br_ref, lt_ref,
                  h_ref, xt_ref, code_ref, wt_ref, cnt_ref, run_ref, wr2_ref, *, gw, n_exp):
    i = pl.program_id(0)

    @pl.when(i == 0)
    def _():
        run_ref[...] = jnp.zeros_like(run_ref)
        wh, wl = _split_bf16(wr_ref[...])
        wr2_ref[:, :LANES] = wh
        wr2_ref[:, LANES:] = wl

    m = mod_ref[0]
    acc = _dot(gla_ref[...], wo_ref[0:gw, :]) + _dot(pool_ref[...], wo_ref[gw:, :])
    h = x_ref[...] + m[2:3] * acc
    h_ref[...] = h
    xt = _rmsnorm(h, nw_ref[...]) * (1.0 + m[4:5]) + m[3:4]
    xt_ref[...] = _pack_bf16_pairs(xt)
    xh, xl = _split_bf16(xt)
    wr2 = wr2_ref[...]
    t1 = _dot(xh, wr2)
    logits = t1[:, :LANES] + t1[:, LANES:] + _dot(xl, wr2[:, :LANES]) + br_ref[...]
    lane = lax.broadcasted_iota(I32, logits.shape, 1)
    neg = jnp.float32(-jnp.inf)
    logits = jnp.where(lane < n_exp, logits, neg)
    vals, hots = [], []
    e_out = jnp.zeros(logits.shape, I32)
    lane_f = lane.astype(F32)
    for j in range(TOP_K):
        mx = jnp.max(logits, axis=-1, keepdims=True)
        idx = jnp.min(jnp.where(logits == mx, lane_f, float(LANES)), axis=-1, keepdims=True)
        hot = lane_f == idx
        vals.append(mx)
        hots.append(hot)
        e_out = jnp.where(lane == j, idx.astype(I32), e_out)
        logits = jnp.where(hot, neg, logits)
    ex = [jnp.exp(v - vals[0]) for v in vals]
    den = ex[0] + ex[1] + ex[2] + ex[3]
    w_out = jnp.zeros(logits.shape, F32)
    for j in range(TOP_K):
        w_out = jnp.where(lane == j, ex[j] / den, w_out)
    osum = jnp.where(hots[0] | hots[1] | hots[2] | hots[3], 1.0, 0.0)
    before = _dot(lt_ref[...], osum.astype(BF16)) + run_ref[0:1, :]
    rk_out = jnp.zeros(logits.shape, I32)
    for j in range(TOP_K):
        rj = jnp.sum(jnp.where(hots[j], before, 0.0), axis=-1, keepdims=True)
        rk_out = jnp.where(lane == j, rj.astype(I32), rk_out)
    run = run_ref[0:1, :] + jnp.sum(osum, axis=0, keepdims=True)
    run_ref[...] = jnp.broadcast_to(run, run_ref.shape)
    code = e_out * RANK_LIMIT + rk_out
    code_ref[...] = code.T[:code_ref.shape[0], :]
    wt_ref[...] = w_out
    cnt_ref[...] = jnp.broadcast_to(run, cnt_ref.shape).astype(I32)


def _route_call(gla, pool, x, mod, w_out, nw, wr, br, *, tm, n_exp, first_tile, n_tiles):
    n_all, d = x.shape
    n = n_tiles * tm
    gw = gla.shape[1]
    t_per_b = n_all // mod.shape[0]
    lt = jnp.asarray(np.tril(np.ones((tm, tm), np.float32), -1), BF16)
    row_in = lambda width: pl.BlockSpec((tm, width), lambda i: (i + first_tile, 0))
    row = lambda width: pl.BlockSpec((tm, width), lambda i: (i, 0))
    const = lambda shape: pl.BlockSpec(shape, lambda i: (0,) * len(shape))
    sds = jax.ShapeDtypeStruct
    return pl.pallas_call(
        functools.partial(_route_kernel, gw=gw, n_exp=n_exp),
        grid=(n_tiles,),
        in_specs=[row_in(gw), row_in(pool.shape[1]), row_in(d),
                  pl.BlockSpec((1, N_MOD, d), lambda i: ((i + first_tile) * tm // t_per_b, 0, 0)),
                  const(w_out.shape), const((1, d)), const(wr.shape), const((1, LANES)), const((tm, tm))],
        out_specs=[row(d), row(d // 2), pl.BlockSpec((8, tm), lambda i: (0, i)), row(LANES), const((8, LANES))],
        out_shape=[sds((n, d), F32), sds((n, d // 2), U32), sds((8, n), I32),
                   sds((n, LANES), F32), sds((8, LANES), I32)],
        scratch_shapes=[pltpu.VMEM((8, LANES), F32), pltpu.VMEM((d, 2 * LANES), BF16)],
        compiler_params=pltpu.CompilerParams(dimension_semantics=("arbitrary",)),
        name="route",
    )(gla, pool, x, mod, w_out, nw, wr, br, lt)


def _plan_kernel(cnt_ref, code_ref, dest_ref, be_ref, nv_ref, first_ref, next_ref, slot_ref, nu_ref, start_ref,
                 *, n_exp, rows):
    @pl.when(pl.program_id(0) == 0)
    def _():
        blk = (lax.broadcasted_iota(I32, be_ref.shape, 0) * LANES + lax.broadcasted_iota(I32, be_ref.shape, 1))
        blk_row0 = blk * rows
        nxt_e = [None] * n_exp
        nxt = jnp.int32(-1)
        for e in reversed(range(n_exp)):
            nxt_e[e] = nxt
            nxt = jnp.where(cnt_ref[e] > 0, e, nxt)
        zeros = jnp.zeros(be_ref.shape, I32)
        be, end_valid, first, nxt_blk, slot = zeros, zeros, zeros, zeros - 1, zeros
        acc = jnp.int32(0)
        last_e = jnp.int32(0)
        ordinal = jnp.int32(0)
        for e in range(n_exp):
            c = cnt_ref[e]
            start_ref[e] = acc
            in_e = (blk_row0 >= acc) & (c > 0)
            end_valid = jnp.where(in_e, acc + c, end_valid)
            be = jnp.where(in_e, e, be)
            first = jnp.where(in_e, (blk_row0 == acc).astype(I32), first)
            nxt_blk = jnp.where(in_e, nxt_e[e], nxt_blk)
            slot = jnp.where(in_e, ordinal % 2, slot)
            acc = acc + (c + rows - 1) // rows * rows
            last_e = jnp.where(c > 0, e, last_e)
            ordinal = ordinal + (c > 0).astype(I32)
        n_used = acc // rows
        nu_ref[0] = n_used
        be_ref[...] = jnp.where(blk < n_used, be, last_e)
        nv_ref[...] = jnp.clip(end_valid - blk_row0, 0, rows)
        first_ref[...] = first
        next_ref[...] = nxt_blk
        slot_ref[...] = slot

    code = code_ref[...]
    e_vec = code // RANK_LIMIT
    dest = code % RANK_LIMIT
    for e in range(n_exp):
        dest = dest + jnp.where(e_vec == e, start_ref[e], 0)
    dest_ref[...] = dest


def _plan_call(counts, code_t, *, n_exp, rows, chunk):
    n = code_t.shape[1]
    sds = jax.ShapeDtypeStruct
    smem = pltpu.SMEM
    return pl.pallas_call(
        functools.partial(_plan_kernel, n_exp=n_exp, rows=rows),
        grid=(n // chunk,),
        in_specs=[pl.BlockSpec(memory_space=smem), pl.BlockSpec((8, chunk), lambda i: (0, i))],
        out_specs=[pl.BlockSpec((8, chunk), lambda i: (0, i))]
        + [pl.BlockSpec((8, LANES), lambda i: (0, 0))] * 5 + [pl.BlockSpec(memory_space=smem)],
        out_shape=[sds((8, n), I32)] + [sds((8, LANES), I32)] * 5 + [sds((1,), I32)],
        scratch_shapes=[pltpu.SMEM((n_exp,), I32)],
        compiler_params=pltpu.CompilerParams(dimension_semantics=("arbitrary",)),
        name="plan",
    )(counts, code_t)


def _sc_worker_id():
    return lax.axis_index("s") * SC_CORES + lax.axis_index("c")


def _sc_scatter_call(x, idx3, *, n_out):
    n, d = x.shape
    n_win_total, k, w = idx3.shape
    n_win = n_win_total // SC_WORKERS
    mesh = plsc.VectorSubcoreMesh(core_axis_name="c", subcore_axis_name="s")

    @functools.partial(
        pl.kernel, mesh=mesh,
        out_type=jax.ShapeDtypeStruct((n_out, d), x.dtype),
        scratch_types=[pltpu.VMEM((k, w), I32), pltpu.VMEM((w, d), x.dtype), pltpu.SemaphoreType.DMA],
        name="sc_dispatch",
    )
    def kern(x_hbm, idx_hbm, out_hbm, idx_v, rows_v, sem):
        wid = _sc_worker_id()

        @pl.loop(0, n_win)
        def _(i):
            win = wid * n_win + i
            pltpu.sync_copy(idx_hbm.at[win], idx_v)
            pltpu.sync_copy(x_hbm.at[pl.ds(win * w, w)], rows_v)
            for j in range(k):
                pltpu.async_copy(rows_v, out_hbm.at[idx_v.at[j]], sem).wait()

    return kern(x, idx3)


def _sc_gather_call(table, idx3):
    n_workers, n_win, w = idx3.shape
    d = table.shape[1]
    assert n_workers == SC_WORKERS and n_win % 2 == 0
    mesh = plsc.VectorSubcoreMesh(core_axis_name="c", subcore_axis_name="s")

    @functools.partial(
        pl.kernel, mesh=mesh,
        out_type=jax.ShapeDtypeStruct((n_workers * n_win * w, d), table.dtype),
        scratch_types=[pltpu.VMEM((n_win, w), I32), pltpu.VMEM((2, w, d), table.dtype),
                       pltpu.SemaphoreType.DMA((2,)), pltpu.SemaphoreType.DMA((2,))],
        name="sc_gather",
    )
    def kern(table_hbm, idx_hbm, out_hbm, idx_v, rows_v, gsem, osem):
        wid = _sc_worker_id()
        base = wid * n_win
        pltpu.sync_copy(idx_hbm.at[wid], idx_v)

        def gather(wi, b):
            return pltpu.make_async_copy(table_hbm.at[idx_v.at[wi]], rows_v.at[b], gsem.at[b])

        def put(wi, b):
            return pltpu.make_async_copy(rows_v.at[b], out_hbm.at[pl.ds((base + wi) * w, w)], osem.at[b])

        gather(0, 0).start()

        @pl.loop(0, n_win, step=2)
        def _(i):
            for b in range(2):
                wi = i + b

                @pl.when(wi + 1 < n_win)
                def _():
                    @pl.when(wi >= 1)
                    def _():
                        put(wi - 1, 1 - b).wait()
                    gather(wi + 1, 1 - b).start()

                gather(wi, b).wait()
                put(wi, b).start()

        put(n_win - 2, 0).wait()
        put(n_win - 1, 1).wait()

    return kern(table, idx3)


def _expert_kernel(be_ref, nu_ref, nv_ref, first_ref, next_ref, slot_ref,
                   x_ref, wgu_hbm, bgu_ref, wd_hbm, bd_ref, y_ref,
                   wgu_f32_ref, wd_f32_ref, wgu_bf_ref, wd_bf_ref, sem, *, d_ff):
    i = pl.program_id(0)
    slot = slot_ref[i]

    def fetch(e, s):
        return (pltpu.make_async_copy(wgu_hbm.at[e], wgu_f32_ref.at[s], sem.at[0, s]),
                pltpu.make_async_copy(wd_hbm.at[e], wd_f32_ref.at[s], sem.at[1, s]))

    @pl.when(i == 0)
    def _():
        for cp in fetch(be_ref[0], slot):
            cp.start()

    @pl.when(first_ref[i] == 1)
    def _():
        for cp in fetch(be_ref[i], slot):
            cp.wait()

        @pl.when(next_ref[i] >= 0)
        def _():
            for cp in fetch(next_ref[i], 1 - slot):
                cp.start()

        wgu_bf_ref[...] = wgu_f32_ref[slot].astype(BF16)
        wd_bf_ref[...] = wd_f32_ref[slot].astype(BF16)

    n_valid = nv_ref[i]
    n_rows = x_ref.shape[0]

    def expert_rows(rows):
        row = lax.broadcasted_iota(I32, (rows, x_ref.shape[1]), 0)
        lo, hi = _unpack_bf16_pairs(jnp.where(row < n_valid, x_ref[0:rows, :], jnp.uint32(0)))
        xb = jnp.concatenate([lo, hi], axis=1).astype(BF16)
        col = lax.broadcasted_iota(I32, bgu_ref.shape[1:], 1)
        gu = _dot(xb, wgu_bf_ref[...]) + (bgu_ref[0] + jnp.where(col >= d_ff, 1.0, 0.0))
        gate = jnp.minimum(gu[:, :d_ff], SWIGLU_LIMIT)
        up1 = jnp.clip(gu[:, d_ff:], 1.0 - SWIGLU_LIMIT, 1.0 + SWIGLU_LIMIT)
        act = up1 * (0.5 * gate) * (1.0 + jnp.tanh((0.5 * SWIGLU_ALPHA) * gate))
        y_ref[0:rows, :] = _pack_bf16_pairs(_dot(act.astype(BF16), wd_bf_ref[...]) + bd_ref[0])

    in_use = i < nu_ref[0]

    for rows in range(EXPERT_ROW_STEP, n_rows + 1, EXPERT_ROW_STEP):
        @pl.when(in_use & (n_valid > rows - EXPERT_ROW_STEP) & (n_valid <= rows))
        def _(rows=rows):
            expert_rows(rows)
            if rows < n_rows:
                y_ref[rows:, :] = jnp.zeros((n_rows - rows, y_ref.shape[1]), y_ref.dtype)


def _expert_call(plan, xs, w_gu, b_gu, w_down, b_down):
    block_e, n_valid, first, nxt, slot, n_used = plan
    n_pad = xs.shape[0]
    n_exp, d, two_ff = w_gu.shape
    d_ff = two_ff // 2
    nblk = n_pad // EXPERT_ROWS
    rows = lambda i, be, nu, *_: (jnp.minimum(i, nu[0] - 1), 0)
    per_e = lambda i, be, *_: (be[i], 0, 0)
    grid_spec = pltpu.PrefetchScalarGridSpec(
        num_scalar_prefetch=6,
        grid=(nblk,),
        in_specs=[pl.BlockSpec((EXPERT_ROWS, d // 2), rows),
                  pl.BlockSpec(memory_space=pl.ANY), pl.BlockSpec((1, 1, two_ff), per_e),
                  pl.BlockSpec(memory_space=pl.ANY), pl.BlockSpec((1, 1, d), per_e)],
        out_specs=pl.BlockSpec((EXPERT_ROWS, d // 2), rows),
        scratch_shapes=[pltpu.VMEM((2, d, two_ff), F32), pltpu.VMEM((2, d_ff, d), F32),
                        pltpu.VMEM((d, two_ff), BF16), pltpu.VMEM((d_ff, d), BF16),
                        pltpu.SemaphoreType.DMA((2, 2))],
    )
    flat = lambda a: a.reshape(-1)
    return pl.pallas_call(
        functools.partial(_expert_kernel, d_ff=d_ff),
        grid_spec=grid_spec,
        out_shape=jax.ShapeDtypeStruct((n_pad, d // 2), U32),
        compiler_params=pltpu.CompilerParams(dimension_semantics=("arbitrary",),
                                             vmem_limit_bytes=EXPERT_VMEM_BYTES),
        name="experts",
    )(flat(block_e), n_used, flat(n_valid), flat(first), flat(nxt), flat(slot),
      xs, w_gu, b_gu.reshape(n_exp, 1, two_ff), w_down, b_down.reshape(n_exp, 1, d))


def _combine_kernel(y4_ref, wt_ref, h_ref, mod_ref, fw_ref, o_ref):
    wt = wt_ref[...]
    acc_lo, acc_hi = None, None
    for j in range(TOP_K):
        lo, hi = _unpack_bf16_pairs(y4_ref[j])
        w = wt[:, j:j + 1]
        acc_lo = w * lo if j == 0 else acc_lo + w * lo
        acc_hi = w * hi if j == 0 else acc_hi + w * hi
    acc = jnp.concatenate([acc_lo, acc_hi], axis=1)
    m = mod_ref[0]
    o_ref[...] = _rmsnorm(h_ref[...] + m[5:6] * acc, fw_ref[...])


def _combine_call(y4, wts, h, mod, fw, prev_out, *, n, tg, first_tile):
    d = h.shape[1]
    t_per_b = n // mod.shape[0]
    part = lambda width: pl.BlockSpec((tg, width), lambda i: (i, 0))
    row = lambda width: pl.BlockSpec((tg, width), lambda i: (i + first_tile, 0))
    in_specs = [pl.BlockSpec((TOP_K, tg, d // 2), lambda i: (0, i, 0)),
                part(LANES), part(d),
                pl.BlockSpec((1, N_MOD, d), lambda i: ((i + first_tile) * tg // t_per_b, 0, 0)),
                pl.BlockSpec((1, d), lambda i: (0, 0))]
    args = [y4, wts, h, mod, fw]
    kern = _combine_kernel
    aliases = {}
    if prev_out is not None:
        in_specs.append(pl.BlockSpec(memory_space=pl.ANY))
        args.append(prev_out)
        kern = lambda y4_ref, wt_ref, h_ref, mod_ref, fw_ref, prev_ref, o_ref: _combine_kernel(
            y4_ref, wt_ref, h_ref, mod_ref, fw_ref, o_ref)
        aliases = {len(args) - 1: 0}
    return pl.pallas_call(
        kern,
        grid=(y4.shape[1] // tg,),
        in_specs=in_specs,
        out_specs=row(d),
        out_shape=jax.ShapeDtypeStruct((n, d), F32),
        input_output_aliases=aliases,
        compiler_params=pltpu.CompilerParams(dimension_semantics=("arbitrary",)),
        name="combine",
    )(*args)


def kernel(x, c, ctx, c_ctx, w_ada, b_ada, norm_mix_w, norm_mlp_w, w_in, w_gk_f, b_gk_f, w_gk_b, b_gk_b,
           gla_norm_w, w_pool, pool_scale, w_out, w_router, b_router, w_gu, b_gu, w_down, b_down,
           final_norm_w):
    b, t, d = x.shape
    assert w_ada.shape[0] == 1, "single-layer trunk"
    n_exp = w_router.shape[2]
    rank = w_gk_f.shape[1]
    qk = w_gk_f.shape[2]
    dk = qk // GLA_HEADS
    gw = GLA_HEADS * gla_norm_w.shape[1]
    pw = w_pool.shape[1] * w_pool.shape[2]
    assert w_in.shape[2] == 2 * qk + 2 * gw + 2 * rank + pw and 2 * rank <= LANES
    assert t % SUPER == 0 and ctx.shape[1] % SUPER == 0 and n_exp <= LANES

    rows = -(-(b + 1) // 8) * 8
    cc = jnp.concatenate([c, c_ctx[None, :], jnp.zeros((rows - b - 1, d), F32)], axis=0)
    mod = _mod_call(cc, w_ada[0], b_ada)
    mod_x = mod[:b].reshape(b, N_MOD, d)
    mod_c = mod[b:b + 1].reshape(1, N_MOD, d)

    wi = w_in[0]
    o_r = 2 * qk + 2 * gw
    w_cat = jnp.concatenate([wi[:, :o_r], wi[:, o_r + 2 * rank:], wi[:, o_r:o_r + 2 * rank],
                             jnp.zeros((d, LANES - 2 * rank), F32)], axis=1).astype(BF16)
    wgk = jnp.concatenate([jnp.pad(w_gk_f[0], ((0, 0), (0, qk))), jnp.pad(w_gk_b[0], ((0, 0), (qk, 0))),
                           jnp.zeros((LANES - 2 * rank, 2 * qk), F32)], axis=0)
    bgk = jnp.concatenate([b_gk_f[0], b_gk_b[0]])[None, :]
    proj = functools.partial(_inproj_call, nw=norm_mix_w, w=w_cat, wgk=wgk.astype(BF16), bgk=bgk,
                             qk=qk, gw=gw, pw=pw, dk=dk)
    q, k, v, vt, g, xp, gk = proj(x, mod_x, tm=1024)
    _, kc, _, vtc, _, _, gkc = proj(ctx, mod_c, tm=SUPER)

    gla = _gla_call(q, k, v, vt, gk, g, kc, vtc, gkc, gla_norm_w)
    pool = _pool_call(xp, w_pool[0].astype(BF16), pool_scale)

    n = b * t
    wr = jnp.pad(w_router[0], ((0, 0), (0, LANES - n_exp)))
    br = jnp.pad(b_router, ((0, 0), (0, LANES - n_exp)))
    n_part = n // MOE_PARTS
    tm, tg = 512, 1024
    assert n % MOE_PARTS == 0 and n_part % (SC_WORKERS * SC_WINDOW) == 0 and n_part % tm == 0 and n_part < RANK_LIMIT
    assert (n_part * TOP_K) % (SC_WORKERS * 2 * SC_GATHER_WINDOW) == 0
    n_pad = n_part * TOP_K + n_exp * EXPERT_ROWS
    nblk = n_pad // EXPERT_ROWS
    assert nblk <= 8 * LANES
    parts = range(MOE_PARTS)
    routed = [_route_call(gla.reshape(n, gw), pool.reshape(n, pw), x.reshape(n, d), mod_x, w_out[0].astype(BF16),
                          norm_mlp_w, wr, br, tm=tm, n_exp=n_exp,
                          first_tile=p * n_part // tm, n_tiles=n_part // tm) for p in parts]
    plans = [_plan_call(cnt[0, :n_exp], code_t, n_exp=n_exp, rows=EXPERT_ROWS, chunk=min(n_part, 4096))
             for (_, _, code_t, _, cnt) in routed]
    dests = [plan[0][:TOP_K] for plan in plans]
    xs = [_sc_scatter_call(routed[p][1],
                           dests[p].reshape(TOP_K, n_part // SC_WINDOW, SC_WINDOW).transpose(1, 0, 2), n_out=n_pad)
          for p in parts]
    ys = [_expert_call(plans[p][1:], xs[p], w_gu[0], b_gu[0], w_down[0], b_down[0]) for p in parts]
    y4 = [_sc_gather_call(ys[p], dests[p].reshape(SC_WORKERS, -1, SC_GATHER_WINDOW)).reshape(TOP_K, n_part, d // 2)
          for p in parts]
    out = None
    for p in parts:
        out = _combine_call(y4[p], routed[p][3], routed[p][0], mod_x, final_norm_w[None, :], out,
                            n=n, tg=tg, first_tile=p * n_part // tg)
    return out.reshape(b, t, d)
```

```python
import functools

import numpy as np
import jax
import jax.numpy as jnp
from jax import lax
from jax.experimental import pallas as pl
from jax.experimental.pallas import tpu as pltpu
from jax.experimental.pallas import tpu_sc as plsc

F32 = jnp.float32
BF16 = jnp.bfloat16
I32 = jnp.int32
U32 = jnp.uint32

GRID_W = 64
GLA_HEADS = 4
GLA_CHUNK = 64
GATE_NORMALIZER = 16.0
POOL_WINDOWS = (2, 4, 8, 16)
POOL_PAD_GRID_ROWS = 8
TOP_K = 4
RANK_LIMIT = 1 << 20
SWIGLU_LIMIT = 7.0
SWIGLU_ALPHA = 1.702
N_MOD = 6
EPS = 1e-6

LANES = 128
SUPER = 256
HEAD_PAIR_DK = 128
EXPERT_ROWS = 512
EXPERT_ROW_STEP = 128
MOE_PARTS = 2
EXPERT_VMEM_BYTES = 56 * 1024 * 1024
SC_CORES = 2
SC_SUBCORES = 16
SC_WORKERS = SC_CORES * SC_SUBCORES
SC_WINDOW = 32
SC_GATHER_WINDOW = 64


def _dot(a, b):
    return jnp.dot(a, b, preferred_element_type=F32)


def _dot_nt(a, b):
    return lax.dot_general(a, b, (((1,), (1,)), ((), ())), preferred_element_type=F32)


def _split_bf16(x):
    hi = x.astype(BF16)
    lo = (x - hi.astype(F32)).astype(BF16)
    return hi, lo


def _pack_bf16_pairs(x):
    c = x.shape[1] // 2
    lo = lax.bitcast_convert_type(x[:, :c].astype(BF16).astype(F32), U32)
    hi = lax.bitcast_convert_type(x[:, c:].astype(BF16).astype(F32), U32)
    return (lo >> 16) | hi


def _unpack_bf16_pairs(p):
    lo = lax.bitcast_convert_type(p << 16, F32)
    hi = lax.bitcast_convert_type(p & jnp.uint32(0xFFFF0000), F32)
    return lo, hi


def _rmsnorm(x, w):
    var = jnp.mean(x * x, axis=-1, keepdims=True)
    return x * lax.rsqrt(var + EPS) * w


def _mod_kernel(c_ref, w_ref, b_ref, o_ref):
    c = c_ref[...]
    s = c * jax.nn.sigmoid(c)
    o_ref[...] = jnp.dot(s, w_ref[...], precision=lax.Precision.HIGHEST,
                         preferred_element_type=F32) + b_ref[...]


def _mod_call(cc, w_ada, b_ada):
    rows, d = cc.shape
    n = w_ada.shape[1]
    tn = 1024
    return pl.pallas_call(
        _mod_kernel,
        grid=(n // tn,),
        in_specs=[pl.BlockSpec((rows, d), lambda j: (0, 0)),
                  pl.BlockSpec((d, tn), lambda j: (0, j)),
                  pl.BlockSpec((1, tn), lambda j: (0, j))],
        out_specs=pl.BlockSpec((rows, tn), lambda j: (0, j)),
        out_shape=jax.ShapeDtypeStruct((rows, n), F32),
        name="mod",
    )(cc, w_ada, b_ada)


def _inproj_kernel(x_ref, mod_ref, nw_ref, w_ref, wgk_ref, bgk_ref,
                   q_ref, k_ref, v_ref, vt_ref, g_ref, p_ref, gk_ref, *, qk, gw, pw, dk):
    x = x_ref[0]
    m = mod_ref[0]
    hm = (_rmsnorm(x, nw_ref[...]) * (1.0 + m[1:2]) + m[0:1]).astype(BF16)
    p = _dot(hm, w_ref[...])
    vt = p[:, 2 * qk:2 * qk + gw].T
    for s in range(vt_ref.shape[1]):
        vt_ref[0, s] = vt[:, s * SUPER:(s + 1) * SUPER].astype(BF16)
    o = 0
    q_ref[0] = (p[:, o:o + qk] * (dk ** -0.5)).astype(BF16); o += qk
    k_ref[0] = p[:, o:o + qk].astype(BF16); o += qk
    v_ref[0] = p[:, o:o + gw].astype(BF16); o += gw
    g_ref[0] = p[:, o:o + gw].astype(BF16); o += gw
    p_ref[0] = p[:, o:o + pw].astype(BF16); o += pw
    r = p[:, o:o + LANES]
    z = _dot(r.astype(BF16), wgk_ref[...]) + bgk_ref[...]
    gk_ref[0] = (jnp.minimum(z, 0.0) - jnp.log1p(jnp.exp(-jnp.abs(z)))) * (1.0 / GATE_NORMALIZER)


def _inproj_call(x, mod, nw, w, wgk, bgk, *, qk, gw, pw, dk, tm):
    b, t, d = x.shape
    n_in = w.shape[1]
    bs = lambda width: pl.BlockSpec((1, tm, width), lambda i, j: (i, j, 0))
    const = lambda shape: pl.BlockSpec(shape, lambda i, j: (0,) * len(shape))
    per_batch = mod.shape[0] > 1
    sds = jax.ShapeDtypeStruct
    return pl.pallas_call(
        functools.partial(_inproj_kernel, qk=qk, gw=gw, pw=pw, dk=dk),
        grid=(b, t // tm),
        in_specs=[bs(d),
                  pl.BlockSpec((1, N_MOD, d), (lambda i, j: (i, 0, 0)) if per_batch else (lambda i, j: (0, 0, 0))),
                  const((1, d)), const((d, n_in)), const((LANES, 2 * qk)), const((1, 2 * qk))],
        out_specs=[bs(qk), bs(qk), bs(gw),
                   pl.BlockSpec((1, tm // SUPER, gw, SUPER), lambda i, j: (i, j, 0, 0)),
                   bs(gw), bs(pw), bs(2 * qk)],
        out_shape=[sds((b, t, qk), BF16), sds((b, t, qk), BF16), sds((b, t, gw), BF16),
                   sds((b, t // SUPER, gw, SUPER), BF16),
                   sds((b, t, gw), BF16), sds((b, t, pw), BF16), sds((b, t, 2 * qk), F32)],
        compiler_params=pltpu.CompilerParams(dimension_semantics=("arbitrary", "arbitrary")),
        name="inproj",
    )(x, mod, nw, w, wgk, bgk)


def _chunk_row(x, r):
    return jnp.concatenate(
        [jnp.broadcast_to(x[c * GLA_CHUNK + r:c * GLA_CHUNK + r + 1, :], (GLA_CHUNK, x.shape[1]))
         for c in range(x.shape[0] // GLA_CHUNK)], axis=0)


def _decay_sums(gk_f, gk_b, cm):
    n = gk_f.shape[1]
    hi, lo = _split_bf16(jnp.concatenate([gk_f, gk_b], axis=1))
    pre = _dot(cm, hi) + _dot(cm, lo)
    pre_b = pre[:, n:]
    return pre[:, :n], _chunk_row(pre_b, GLA_CHUNK - 1) - pre_b + gk_b


def _gla_super(q, k, v, vt, bcum, amask, bd_mask, st, fwd, want_out):
    nch = SUPER // GLA_CHUNK
    order = tuple(range(nch)) if fwd else tuple(reversed(range(nch)))
    last_row = GLA_CHUNK - 1 if fwd else 0
    mid_row = GLA_CHUNK // 2 - 1 if fwd else GLA_CHUNK // 2
    chunk_row = functools.partial(_chunk_row, bcum)
    chunk_of_row = lax.broadcasted_iota(I32, bcum.shape, 0) // GLA_CHUNK

    def by_chunk(x):
        return jnp.concatenate([jnp.where(chunk_of_row == c, x, 0.0).astype(BF16) for c in range(nch)], axis=1)

    blast = chunk_row(last_row)
    u_all = _dot(vt, by_chunk(k * jnp.exp(blast - bcum)))
    before = [None] * nch
    for c in order:
        before[c] = st
        decay = jnp.exp(bcum[c * GLA_CHUNK + last_row:c * GLA_CHUNK + last_row + 1, :])
        st = st * decay + jnp.where(bd_mask, u_all[:, c * HEAD_PAIR_DK:(c + 1) * HEAD_PAIR_DK], 0.0)
    if not want_out:
        return None, st
    bmid = chunk_row(mid_row)
    qt = q * jnp.exp(bcum - bmid)
    kt = (k * jnp.exp(bmid - bcum)).astype(BF16)
    lane = lax.broadcasted_iota(I32, qt.shape, 1)
    half = HEAD_PAIR_DK // 2
    o_heads = []
    for hh in range(2):
        sel = (lane < half) if hh == 0 else (lane >= half)
        a = _dot_nt(jnp.where(sel, qt, 0.0).astype(BF16), kt)
        a = jnp.where(amask, a, 0.0).astype(BF16)
        o_heads.append(_dot(a, v[:, hh * LANES:(hh + 1) * LANES]))
    qh = (q * jnp.exp(bcum)).astype(BF16)
    o_inter = jnp.concatenate(
        [_dot_nt(qh[c * GLA_CHUNK:(c + 1) * GLA_CHUNK], before[c].astype(BF16)) for c in range(nch)], axis=0)
    return jnp.concatenate(o_heads, axis=1) + o_inter, st


def _gla_kernel(q_ref, k_ref, v_ref, vt_ref, gkf_ref, gkb_ref, g_ref, kc_ref, vtc_ref, gkfc_ref, gkbc_ref,
                nw_ref, cmf_ref, cmb_ref, o_ref, stf_ref, stb_ref, of_ref, ob_ref):
    t = q_ref.shape[1]
    tc = kc_ref.shape[1]
    nsc, nscc = t // SUPER, tc // SUPER
    cmf = cmf_ref[...]
    cmb = cmb_ref[...]
    amask_f = cmf > 0
    amask_b = cmb > 0
    row = lax.broadcasted_iota(I32, (2 * LANES, HEAD_PAIR_DK), 0)
    lane = lax.broadcasted_iota(I32, (2 * LANES, HEAD_PAIR_DK), 1)
    bd_mask = (row < LANES) == (lane < HEAD_PAIR_DK // 2)

    def group_rows(j):
        start = j * SUPER
        return pl.ds(start if isinstance(start, int) else pl.multiple_of(start, SUPER), SUPER)

    def ctx_state(bcum, fwd, j, st):
        return _gla_super(None, kc_ref[0, group_rows(j), :].astype(F32), None, vtc_ref[0, j], bcum,
                          None, bd_mask, st, fwd, False)[1]

    def latent(bcum, amask, fwd, j, st):
        rows = group_rows(j)
        return _gla_super(q_ref[0, rows, :].astype(F32), k_ref[0, rows, :].astype(F32), v_ref[0, rows, :],
                          vt_ref[0, j], bcum, amask, bd_mask, st, fwd, True)

    stf = jnp.zeros(stf_ref.shape, F32)
    stb = jnp.zeros(stb_ref.shape, F32)
    for j in range(nscc):
        jb = nscc - 1 - j
        bcum_f, bcum_b = _decay_sums(gkfc_ref[0, group_rows(j), :], gkbc_ref[0, group_rows(jb), :], cmf)
        stf = ctx_state(bcum_f, True, j, stf)
        stb = ctx_state(bcum_b, False, jb, stb)
    stf_ref[...] = stf
    stb_ref[...] = stb

    def scan_body(jj, carry):
        jb = nsc - 1 - jj
        bcum_f, bcum_b = _decay_sums(gkf_ref[0, group_rows(jj), :], gkb_ref[0, group_rows(jb), :], cmf)
        of, stf = latent(bcum_f, amask_f, True, jj, stf_ref[...])
        of_ref[pl.ds(pl.multiple_of(jj * SUPER, SUPER), SUPER), :] = of
        stf_ref[...] = stf
        ob, stb = latent(bcum_b, amask_b, False, jb, stb_ref[...])
        ob_ref[pl.ds(pl.multiple_of(jb * SUPER, SUPER), SUPER), :] = ob
        stb_ref[...] = stb
        return carry

    lax.fori_loop(0, nsc, scan_body, 0, unroll=8)

    nw = nw_ref[...]

    def out_body(j, carry):
        rows = pl.ds(pl.multiple_of(j * SUPER, SUPER), SUPER)
        o = of_ref[rows, :] + ob_ref[rows, :]
        g = g_ref[0, rows, :].astype(F32)
        half_g = 0.5 * g
        gate = half_g * (1.0 + jnp.tanh(half_g))
        for hh in range(2):
            oh = o[:, hh * LANES:(hh + 1) * LANES]
            on = oh * lax.rsqrt(jnp.mean(oh * oh, axis=-1, keepdims=True) + EPS) * nw
            o_ref[0, rows, hh * LANES:(hh + 1) * LANES] = (on * gate[:, hh * LANES:(hh + 1) * LANES]).astype(BF16)
        return carry

    lax.fori_loop(0, nsc, out_body, 0)


def _gla_masks():
    i = np.arange(SUPER)
    same = (i[:, None] // GLA_CHUNK) == (i[None, :] // GLA_CHUNK)
    fwd = same & (i[None, :] <= i[:, None])
    bwd = same & (i[None, :] >= i[:, None])
    return jnp.asarray(fwd, BF16), jnp.asarray(bwd, BF16)


def _gla_call(q, k, v, vt, gk, g, kc, vtc, gkc, nw):
    b, t, qk = q.shape
    tc = kc.shape[1]
    npair = qk // HEAD_PAIR_DK
    cmf, cmb = _gla_masks()
    lat = lambda width, off: pl.BlockSpec((1, t, width), lambda i, j: (i, 0, j + off))
    ctx = lambda width, off: pl.BlockSpec((1, tc, width), lambda i, j: (i, 0, j + off))
    tr = lambda n_groups: pl.BlockSpec((1, n_groups, 2 * LANES, SUPER), lambda i, j: (i, 0, j, 0))
    const = lambda shape: pl.BlockSpec(shape, lambda i, j: (0,) * len(shape))
    return pl.pallas_call(
        _gla_kernel,
        grid=(b, npair),
        in_specs=[lat(HEAD_PAIR_DK, 0), lat(HEAD_PAIR_DK, 0), lat(2 * LANES, 0), tr(t // SUPER),
                  lat(HEAD_PAIR_DK, 0), lat(HEAD_PAIR_DK, npair), lat(2 * LANES, 0),
                  ctx(HEAD_PAIR_DK, 0), tr(tc // SUPER), ctx(HEAD_PAIR_DK, 0), ctx(HEAD_PAIR_DK, npair),
                  const((1, LANES)), const(cmf.shape), const(cmb.shape)],
        out_specs=lat(2 * LANES, 0),
        out_shape=jax.ShapeDtypeStruct((b, t, v.shape[2]), BF16),
        scratch_shapes=[pltpu.VMEM((2 * LANES, HEAD_PAIR_DK), F32), pltpu.VMEM((2 * LANES, HEAD_PAIR_DK), F32),
                        pltpu.VMEM((t, 2 * LANES), F32), pltpu.VMEM((t, 2 * LANES), F32)],
        compiler_params=pltpu.CompilerParams(dimension_semantics=("arbitrary", "arbitrary")),
        name="gla",
    )(q, k, v, vt, gk, gk, g, kc, vtc, gkc, gkc, nw, cmf, cmb)


def _pool_kernel(x_ref, cm_ref, cnt_ref, wp_ref, ps_ref, o_ref, a_ref, b_ref):
    t = x_ref.shape[1]
    pad = POOL_PAD_GRID_ROWS * GRID_W
    total = t + 2 * pad
    for gi, w in enumerate(POOL_WINDOWS):
        lo = w // 2
        cols = slice(gi * LANES, (gi + 1) * LANES)
        cmat = cm_ref[gi]
        a_ref[0:pad, :] = jnp.zeros((pad, LANES), F32)
        a_ref[pad + t:total, :] = jnp.zeros((pad, LANES), F32)
        for blk in range(t // SUPER):
            rs = slice(blk * SUPER, (blk + 1) * SUPER)
            a_ref[pad + blk * SUPER:pad + (blk + 1) * SUPER, :] = _dot(cmat, x_ref[0, rs, cols])
        src, dst = a_ref, b_ref
        m = 1
        while m < w:
            sh = m * GRID_W
            dst[0:total - sh, :] = src[0:total - sh, :] + src[sh:total, :]
            src, dst = dst, src
            m *= 2
        first = pad - lo * GRID_W
        pooled = src[first:first + t, :] / cnt_ref[gi] - x_ref[0, :, cols].astype(F32)
        yp = _dot(pooled.astype(BF16), wp_ref[gi]) * ps_ref[:, cols]
        o_ref[0, :, cols] = yp.astype(BF16)


def _pool_col_mats():
    i = np.arange(SUPER)
    same_row = (i[:, None] // GRID_W) == (i[None, :] // GRID_W)
    d = i[None, :] - i[:, None]
    mats = []
    for w in POOL_WINDOWS:
        lo = w // 2
        hi = w - 1 - lo
        mats.append(same_row & (d >= -lo) & (d <= hi))
    return jnp.asarray(np.stack(mats), BF16)


def _pool_counts(t):
    rows = t // GRID_W
    r = np.arange(t) // GRID_W
    c = np.arange(t) % GRID_W
    out = []
    for w in POOL_WINDOWS:
        lo = w // 2
        hi = w - 1 - lo
        cnt_r = np.minimum(r + hi + 1, rows) - np.maximum(r - lo, 0)
        cnt_c = np.minimum(c + hi + 1, GRID_W) - np.maximum(c - lo, 0)
        out.append(np.broadcast_to((cnt_r * cnt_c).astype(np.float32)[:, None], (t, LANES)))
    return jnp.asarray(np.stack(out))


def _pool_call(xp, w_pool, pool_scale):
    b, t, pw = xp.shape
    ng = len(POOL_WINDOWS)
    assert max(POOL_WINDOWS) // 2 <= POOL_PAD_GRID_ROWS and t % GRID_W == 0
    cm = _pool_col_mats()
    cnt = _pool_counts(t)
    staged = t + 2 * POOL_PAD_GRID_ROWS * GRID_W
    const = lambda shape: pl.BlockSpec(shape, lambda i: (0,) * len(shape))
    return pl.pallas_call(
        _pool_kernel,
        grid=(b,),
        in_specs=[pl.BlockSpec((1, t, pw), lambda i: (i, 0, 0)),
                  const(cm.shape), const(cnt.shape), const((ng, LANES, LANES)), const((1, pw))],
        out_specs=pl.BlockSpec((1, t, pw), lambda i: (i, 0, 0)),
        out_shape=jax.ShapeDtypeStruct((b, t, pw), BF16),
        scratch_shapes=[pltpu.VMEM((staged, LANES), F32), pltpu.VMEM((staged, LANES), F32)],
        compiler_params=pltpu.CompilerParams(dimension_semantics=("arbitrary",)),
        name="pool",
    )(xp, cm, cnt, w_pool, pool_scale)


def _route_kernel(gla_ref, pool_ref, x_ref, mod_ref, wo_ref, nw_ref, wr_ref, br_ref, ut_ref,
                  h_ref, xt_ref, code_ref, wt_ref, cnt_ref, run_ref, wr2_ref, *, gw, n_exp):
    i = pl.program_id(0)

    @pl.when(i == 0)
    def _():
        run_ref[...] = jnp.zeros_like(run_ref)
        wh, wl = _split_bf16(wr_ref[...])
        wr2_ref[:LANES, :] = wh
        wr2_ref[LANES:, :] = wl

    m = mod_ref[0]
    acc = _dot(gla_ref[...], wo_ref[0:gw, :]) + _dot(pool_ref[...], wo_ref[gw:, :])
    h = x_ref[...] + m[2:3] * acc
    h_ref[...] = h
    xt = _rmsnorm(h, nw_ref[...]) * (1.0 + m[4:5]) + m[3:4]
    xt_ref[...] = _pack_bf16_pairs(xt)
    xh, xl = _split_bf16(xt)
    wr2 = wr2_ref[...]
    t1 = _dot_nt(wr2, xh)
    logits = t1[:LANES] + t1[LANES:] + _dot_nt(wr2[:LANES], xl) + br_ref[...]
    row = lax.broadcasted_iota(I32, logits.shape, 0)
    row_f = row.astype(F32)
    neg = jnp.float32(-jnp.inf)
    logits = jnp.where(row < n_exp, logits, neg)
    vals, hots, idxs = [], [], []
    for j in range(TOP_K):
        mx = jnp.max(logits, axis=0, keepdims=True)
        idx = jnp.min(jnp.where(logits == mx, row_f, float(LANES)), axis=0, keepdims=True)
        hot = row_f == idx
        vals.append(mx)
        hots.append(hot)
        idxs.append(idx)
        logits = jnp.where(hot, neg, logits)
    ex = [jnp.exp(v - vals[0]) for v in vals]
    den = ex[0] + ex[1] + ex[2] + ex[3]
    osum = jnp.where(hots[0] | hots[1] | hots[2] | hots[3], 1.0, 0.0)
    before = _dot(osum.astype(BF16), ut_ref[...]) + run_ref[:, 0:1]
    run = run_ref[:, 0:1] + jnp.sum(osum, axis=1, keepdims=True)
    run_ref[...] = jnp.broadcast_to(run, run_ref.shape)
    code_row = lax.broadcasted_iota(I32, code_ref.shape, 0)
    code = jnp.zeros(code_ref.shape, I32)
    w_t = jnp.zeros(logits.shape, F32)
    for j in range(TOP_K):
        rank = jnp.sum(jnp.where(hots[j], before, 0.0), axis=0, keepdims=True)
        code = jnp.where(code_row == j, idxs[j].astype(I32) * RANK_LIMIT + rank.astype(I32), code)
        w_t = jnp.where(row == j, ex[j] / den, w_t)
    code_ref[...] = code
    wt_ref[...] = w_t.T
    cnt_ref[...] = jnp.broadcast_to(run, cnt_ref.shape).astype(I32)


def _route_call(gla, pool, x, mod, w_out, nw, wr, br, *, tm, n_exp, first_tile, n_tiles):
    n_all, d = x.shape
    n = n_tiles * tm
    gw = gla.shape[1]
    t_per_b = n_all // mod.shape[0]
    ut = jnp.asarray(np.triu(np.ones((tm, tm), np.float32), 1), BF16)
    row_in = lambda width: pl.BlockSpec((tm, width), lambda i: (i + first_tile, 0))
    row = lambda width: pl.BlockSpec((tm, width), lambda i: (i, 0))
    const = lambda shape: pl.BlockSpec(shape, lambda i: (0,) * len(shape))
    sds = jax.ShapeDtypeStruct
    return pl.pallas_call(
        functools.partial(_route_kernel, gw=gw, n_exp=n_exp),
        grid=(n_tiles,),
        in_specs=[row_in(gw), row_in(pool.shape[1]), row_in(d),
                  pl.BlockSpec((1, N_MOD, d), lambda i: ((i + first_tile) * tm // t_per_b, 0, 0)),
                  const(w_out.shape), const((1, d)), const(wr.shape), const((LANES, 1)), const((tm, tm))],
        out_specs=[row(d), row(d // 2), pl.BlockSpec((8, tm), lambda i: (0, i)), row(LANES), const((LANES, LANES))],
        out_shape=[sds((n, d), F32), sds((n, d // 2), U32), sds((8, n), I32),
                   sds((n, LANES), F32), sds((LANES, LANES), I32)],
        scratch_shapes=[pltpu.VMEM((LANES, LANES), F32), pltpu.VMEM((2 * LANES, d), BF16)],
        compiler_params=pltpu.CompilerParams(dimension_semantics=("arbitrary",)),
        name="route",
    )(gla, pool, x, mod, w_out, nw, wr, br, ut)


def _plan_kernel(cnt_ref, code_ref, dest_ref, be_ref, nv_ref, first_ref, next_ref, slot_ref, nu_ref, start_ref,
                 *, n_exp, rows):
    @pl.when(pl.program_id(0) == 0)
    def _():
        blk = (lax.broadcasted_iota(I32, be_ref.shape, 0) * LANES + lax.broadcasted_iota(I32, be_ref.shape, 1))
        blk_row0 = blk * rows
        nxt_e = [None] * n_exp
        nxt = jnp.int32(-1)
        for e in reversed(range(n_exp)):
            nxt_e[e] = nxt
            nxt = jnp.where(cnt_ref[e] > 0, e, nxt)
        zeros = jnp.zeros(be_ref.shape, I32)
        be, end_valid, first, nxt_blk, slot = zeros, zeros, zeros, zeros - 1, zeros
        acc = jnp.int32(0)
        last_e = jnp.int32(0)
        ordinal = jnp.int32(0)
        for e in range(n_exp):
            c = cnt_ref[e]
            start_ref[e] = acc
            in_e = (blk_row0 >= acc) & (c > 0)
            end_valid = jnp.where(in_e, acc + c, end_valid)
            be = jnp.where(in_e, e, be)
            first = jnp.where(in_e, (blk_row0 == acc).astype(I32), first)
            nxt_blk = jnp.where(in_e, nxt_e[e], nxt_blk)
            slot = jnp.where(in_e, ordinal % 2, slot)
            acc = acc + (c + rows - 1) // rows * rows
            last_e = jnp.where(c > 0, e, last_e)
            ordinal = ordinal + (c > 0).astype(I32)
        n_used = acc // rows
        nu_ref[0] = n_used
        be_ref[...] = jnp.where(blk < n_used, be, last_e)
        nv_ref[...] = jnp.clip(end_valid - blk_row0, 0, rows)
        first_ref[...] = first
        next_ref[...] = nxt_blk
        slot_ref[...] = slot

    code = code_ref[...]
    e_vec = code // RANK_LIMIT
    dest = code % RANK_LIMIT
    for e in range(n_exp):
        dest = dest + jnp.where(e_vec == e, start_ref[e], 0)
    dest_ref[...] = dest


def _plan_call(counts, code_t, *, n_exp, rows, chunk):
    n = code_t.shape[1]
    sds = jax.ShapeDtypeStruct
    smem = pltpu.SMEM
    return pl.pallas_call(
        functools.partial(_plan_kernel, n_exp=n_exp, rows=rows),
        grid=(n // chunk,),
        in_specs=[pl.BlockSpec(memory_space=smem), pl.BlockSpec((8, chunk), lambda i: (0, i))],
        out_specs=[pl.BlockSpec((8, chunk), lambda i: (0, i))]
        + [pl.BlockSpec((8, LANES), lambda i: (0, 0))] * 5 + [pl.BlockSpec(memory_space=smem)],
        out_shape=[sds((8, n), I32)] + [sds((8, LANES), I32)] * 5 + [sds((1,), I32)],
        scratch_shapes=[pltpu.SMEM((n_exp,), I32)],
        compiler_params=pltpu.CompilerParams(dimension_semantics=("arbitrary",)),
        name="plan",
    )(counts, code_t)


def _sc_worker_id():
    return lax.axis_index("s") * SC_CORES + lax.axis_index("c")


def _sc_scatter_call(x, idx3, *, n_out):
    n, d = x.shape
    n_win_total, k, w = idx3.shape
    n_win = n_win_total // SC_WORKERS
    mesh = plsc.VectorSubcoreMesh(core_axis_name="c", subcore_axis_name="s")

    @functools.partial(
        pl.kernel, mesh=mesh,
        out_type=jax.ShapeDtypeStruct((n_out, d), x.dtype),
        scratch_types=[pltpu.VMEM((k, w), I32), pltpu.VMEM((w, d), x.dtype), pltpu.SemaphoreType.DMA],
        name="sc_dispatch",
    )
    def kern(x_hbm, idx_hbm, out_hbm, idx_v, rows_v, sem):
        wid = _sc_worker_id()

        @pl.loop(0, n_win)
        def _(i):
            win = wid * n_win + i
            pltpu.sync_copy(idx_hbm.at[win], idx_v)
            pltpu.sync_copy(x_hbm.at[pl.ds(win * w, w)], rows_v)
            for j in range(k):
                pltpu.async_copy(rows_v, out_hbm.at[idx_v.at[j]], sem).wait()

    return kern(x, idx3)


def _sc_gather_call(table, idx3):
    n_workers, n_win, w = idx3.shape
    d = table.shape[1]
    assert n_workers == SC_WORKERS and n_win % 2 == 0
    mesh = plsc.VectorSubcoreMesh(core_axis_name="c", subcore_axis_name="s")

    @functools.partial(
        pl.kernel, mesh=mesh,
        out_type=jax.ShapeDtypeStruct((n_workers * n_win * w, d), table.dtype),
        scratch_types=[pltpu.VMEM((n_win, w), I32), pltpu.VMEM((2, w, d), table.dtype),
                       pltpu.SemaphoreType.DMA((2,)), pltpu.SemaphoreType.DMA((2,))],
        name="sc_gather",
    )
    def kern(table_hbm, idx_hbm, out_hbm, idx_v, rows_v, gsem, osem):
        wid = _sc_worker_id()
        base = wid * n_win
        pltpu.sync_copy(idx_hbm.at[wid], idx_v)

        def gather(wi, b):
            return pltpu.make_async_copy(table_hbm.at[idx_v.at[wi]], rows_v.at[b], gsem.at[b])

        def put(wi, b):
            return pltpu.make_async_copy(rows_v.at[b], out_hbm.at[pl.ds((base + wi) * w, w)], osem.at[b])

        gather(0, 0).start()

        @pl.loop(0, n_win, step=2)
        def _(i):
            for b in range(2):
                wi = i + b

                @pl.when(wi + 1 < n_win)
                def _():
                    @pl.when(wi >= 1)
                    def _():
                        put(wi - 1, 1 - b).wait()
                    gather(wi + 1, 1 - b).start()

                gather(wi, b).wait()
                put(wi, b).start()

        put(n_win - 2, 0).wait()
        put(n_win - 1, 1).wait()

    return kern(table, idx3)


def _expert_kernel(be_ref, nu_ref, nv_ref, first_ref, next_ref, slot_ref,
                   x_ref, wgu_hbm, bgu_ref, wd_hbm, bd_ref, y_ref,
                   wgu_f32_ref, wd_f32_ref, wgu_bf_ref, wd_bf_ref, sem, *, d_ff):
    i = pl.program_id(0)
    slot = slot_ref[i]

    def fetch(e, s):
        return (pltpu.make_async_copy(wgu_hbm.at[e], wgu_f32_ref.at[s], sem.at[0, s]),
                pltpu.make_async_copy(wd_hbm.at[e], wd_f32_ref.at[s], sem.at[1, s]))

    @pl.when(i == 0)
    def _():
        for cp in fetch(be_ref[0], slot):
            cp.start()

    @pl.when(first_ref[i] == 1)
    def _():
        for cp in fetch(be_ref[i], slot):
            cp.wait()

        @pl.when(next_ref[i] >= 0)
        def _():
            for cp in fetch(next_ref[i], 1 - slot):
                cp.start()

        wgu_bf_ref[...] = wgu_f32_ref[slot].astype(BF16)
        wd_bf_ref[...] = wd_f32_ref[slot].astype(BF16)

    n_valid = nv_ref[i]
    n_rows = x_ref.shape[0]

    def expert_rows(rows):
        row = lax.broadcasted_iota(I32, (rows, x_ref.shape[1]), 0)
        lo, hi = _unpack_bf16_pairs(jnp.where(row < n_valid, x_ref[0:rows, :], jnp.uint32(0)))
        xb = jnp.concatenate([lo, hi], axis=1).astype(BF16)
        col = lax.broadcasted_iota(I32, bgu_ref.shape[1:], 1)
        gu = _dot(xb, wgu_bf_ref[...]) + (bgu_ref[0] + jnp.where(col >= d_ff, 1.0, 0.0))
        gate = jnp.minimum(gu[:, :d_ff], SWIGLU_LIMIT)
        up1 = jnp.clip(gu[:, d_ff:], 1.0 - SWIGLU_LIMIT, 1.0 + SWIGLU_LIMIT)
        act = up1 * (0.5 * gate) * (1.0 + jnp.tanh((0.5 * SWIGLU_ALPHA) * gate))
        y_ref[0:rows, :] = _pack_bf16_pairs(_dot(act.astype(BF16), wd_bf_ref[...]) + bd_ref[0])

    in_use = i < nu_ref[0]

    for rows in range(EXPERT_ROW_STEP, n_rows + 1, EXPERT_ROW_STEP):
        @pl.when(in_use & (n_valid > rows - EXPERT_ROW_STEP) & (n_valid <= rows))
        def _(rows=rows):
            expert_rows(rows)
            if rows < n_rows:
                y_ref[rows:, :] = jnp.zeros((n_rows - rows, y_ref.shape[1]), y_ref.dtype)


def _expert_call(plan, xs, w_gu, b_gu, w_down, b_down):
    block_e, n_valid, first, nxt, slot, n_used = plan
    n_pad = xs.shape[0]
    n_exp, d, two_ff = w_gu.shape
    d_ff = two_ff // 2
    nblk = n_pad // EXPERT_ROWS
    rows = lambda i, be, nu, *_: (jnp.minimum(i, nu[0] - 1), 0)
    per_e = lambda i, be, *_: (be[i], 0, 0)
    grid_spec = pltpu.PrefetchScalarGridSpec(
        num_scalar_prefetch=6,
        grid=(nblk,),
        in_specs=[pl.BlockSpec((EXPERT_ROWS, d // 2), rows),
                  pl.BlockSpec(memory_space=pl.ANY), pl.BlockSpec((1, 1, two_ff), per_e),
                  pl.BlockSpec(memory_space=pl.ANY), pl.BlockSpec((1, 1, d), per_e)],
        out_specs=pl.BlockSpec((EXPERT_ROWS, d // 2), rows),
        scratch_shapes=[pltpu.VMEM((2, d, two_ff), F32), pltpu.VMEM((2, d_ff, d), F32),
                        pltpu.VMEM((d, two_ff), BF16), pltpu.VMEM((d_ff, d), BF16),
                        pltpu.SemaphoreType.DMA((2, 2))],
    )
    flat = lambda a: a.reshape(-1)
    return pl.pallas_call(
        functools.partial(_expert_kernel, d_ff=d_ff),
        grid_spec=grid_spec,
        out_shape=jax.ShapeDtypeStruct((n_pad, d // 2), U32),
        compiler_params=pltpu.CompilerParams(dimension_semantics=("arbitrary",),
                                             vmem_limit_bytes=EXPERT_VMEM_BYTES),
        name="experts",
    )(flat(block_e), n_used, flat(n_valid), flat(first), flat(nxt), flat(slot),
      xs, w_gu, b_gu.reshape(n_exp, 1, two_ff), w_down, b_down.reshape(n_exp, 1, d))


def _combine_kernel(y4_ref, wt_ref, h_ref, mod_ref, fw_ref, o_ref):
    wt = wt_ref[...]
    acc_lo, acc_hi = None, None
    for j in range(TOP_K):
        lo, hi = _unpack_bf16_pairs(y4_ref[j])
        w = wt[:, j:j + 1]
        acc_lo = w * lo if j == 0 else acc_lo + w * lo
        acc_hi = w * hi if j == 0 else acc_hi + w * hi
    acc = jnp.concatenate([acc_lo, acc_hi], axis=1)
    m = mod_ref[0]
    o_ref[...] = _rmsnorm(h_ref[...] + m[5:6] * acc, fw_ref[...])


def _combine_call(y4, wts, h, mod, fw, prev_out, *, n, tg, first_tile):
    d = h.shape[1]
    t_per_b = n // mod.shape[0]
    part = lambda width: pl.BlockSpec((tg, width), lambda i: (i, 0))
    row = lambda width: pl.BlockSpec((tg, width), lambda i: (i + first_tile, 0))
    in_specs = [pl.BlockSpec((TOP_K, tg, d // 2), lambda i: (0, i, 0)),
                part(LANES), part(d),
                pl.BlockSpec((1, N_MOD, d), lambda i: ((i + first_tile) * tg // t_per_b, 0, 0)),
                pl.BlockSpec((1, d), lambda i: (0, 0))]
    args = [y4, wts, h, mod, fw]
    kern = _combine_kernel
    aliases = {}
    if prev_out is not None:
        in_specs.append(pl.BlockSpec(memory_space=pl.ANY))
        args.append(prev_out)
        kern = lambda y4_ref, wt_ref, h_ref, mod_ref, fw_ref, prev_ref, o_ref: _combine_kernel(
            y4_ref, wt_ref, h_ref, mod_ref, fw_ref, o_ref)
        aliases = {len(args) - 1: 0}
    return pl.pallas_call(
        kern,
        grid=(y4.shape[1] // tg,),
        in_specs=in_specs,
        out_specs=row(d),
        out_shape=jax.ShapeDtypeStruct((n, d), F32),
        input_output_aliases=aliases,
        compiler_params=pltpu.CompilerParams(dimension_semantics=("arbitrary",)),
        name="combine",
    )(*args)


def kernel(x, c, ctx, c_ctx, w_ada, b_ada, norm_mix_w, norm_mlp_w, w_in, w_gk_f, b_gk_f, w_gk_b, b_gk_b,
           gla_norm_w, w_pool, pool_scale, w_out, w_router, b_router, w_gu, b_gu, w_down, b_down,
           final_norm_w):
    b, t, d = x.shape
    assert w_ada.shape[0] == 1, "single-layer trunk"
    n_exp = w_router.shape[2]
    rank = w_gk_f.shape[1]
    qk = w_gk_f.shape[2]
    dk = qk // GLA_HEADS
    gw = GLA_HEADS * gla_norm_w.shape[1]
    pw = w_pool.shape[1] * w_pool.shape[2]
    assert w_in.shape[2] == 2 * qk + 2 * gw + 2 * rank + pw and 2 * rank <= LANES
    assert t % SUPER == 0 and ctx.shape[1] % SUPER == 0 and n_exp <= LANES

    rows = -(-(b + 1) // 8) * 8
    cc = jnp.concatenate([c, c_ctx[None, :], jnp.zeros((rows - b - 1, d), F32)], axis=0)
    mod = _mod_call(cc, w_ada[0], b_ada)
    mod_x = mod[:b].reshape(b, N_MOD, d)
    mod_c = mod[b:b + 1].reshape(1, N_MOD, d)

    wi = w_in[0]
    o_r = 2 * qk + 2 * gw
    w_cat = jnp.concatenate([wi[:, :o_r], wi[:, o_r + 2 * rank:], wi[:, o_r:o_r + 2 * rank],
                             jnp.zeros((d, LANES - 2 * rank), F32)], axis=1).astype(BF16)
    wgk = jnp.concatenate([jnp.pad(w_gk_f[0], ((0, 0), (0, qk))), jnp.pad(w_gk_b[0], ((0, 0), (qk, 0))),
                           jnp.zeros((LANES - 2 * rank, 2 * qk), F32)], axis=0)
    bgk = jnp.concatenate([b_gk_f[0], b_gk_b[0]])[None, :]
    proj = functools.partial(_inproj_call, nw=norm_mix_w, w=w_cat, wgk=wgk.astype(BF16), bgk=bgk,
                             qk=qk, gw=gw, pw=pw, dk=dk)
    q, k, v, vt, g, xp, gk = proj(x, mod_x, tm=1024)
    _, kc, _, vtc, _, _, gkc = proj(ctx, mod_c, tm=SUPER)

    gla = _gla_call(q, k, v, vt, gk, g, kc, vtc, gkc, gla_norm_w)
    pool = _pool_call(xp, w_pool[0].astype(BF16), pool_scale)

    n = b * t
    wr = jnp.pad(w_router[0].T, ((0, LANES - n_exp), (0, 0)))
    br = jnp.pad(b_router.T, ((0, LANES - n_exp), (0, 0)))
    n_part = n // MOE_PARTS
    tm, tg = 512, 1024
    assert n % MOE_PARTS == 0 and n_part % (SC_WORKERS * SC_WINDOW) == 0 and n_part % tm == 0 and n_part < RANK_LIMIT
    assert (n_part * TOP_K) % (SC_WORKERS * 2 * SC_GATHER_WINDOW) == 0
    n_pad = n_part * TOP_K + n_exp * EXPERT_ROWS
    nblk = n_pad // EXPERT_ROWS
    assert nblk <= 8 * LANES
    parts = range(MOE_PARTS)
    routed = [_route_call(gla.reshape(n, gw), pool.reshape(n, pw), x.reshape(n, d), mod_x, w_out[0].astype(BF16),
                          norm_mlp_w, wr, br, tm=tm, n_exp=n_exp,
                          first_tile=p * n_part // tm, n_tiles=n_part // tm) for p in parts]
    plans = [_plan_call(cnt[:n_exp, 0], code_t, n_exp=n_exp, rows=EXPERT_ROWS, chunk=min(n_part, 4096))
             for (_, _, code_t, _, cnt) in routed]
    dests = [plan[0][:TOP_K] for plan in plans]
    xs = [_sc_scatter_call(routed[p][1],
                           dests[p].reshape(TOP_K, n_part // SC_WINDOW, SC_WINDOW).transpose(1, 0, 2), n_out=n_pad)
          for p in parts]
    ys = [_expert_call(plans[p][1:], xs[p], w_gu[0], b_gu[0], w_down[0], b_down[0]) for p in parts]
    y4 = [_sc_gather_call(ys[p], dests[p].reshape(SC_WORKERS, -1, SC_GATHER_WINDOW)).reshape(TOP_K, n_part, d // 2)
          for p in parts]
    out = None
    for p in parts:
        out = _combine_call(y4[p], routed[p][3], routed[p][0], mod_x, final_norm_w[None, :], out,
                            n=n, tg=tg, first_tile=p * n_part // tg)
    return out.reshape(b, t, d)
```

```python
import functools

import numpy as np
import jax
import jax.numpy as jnp
from jax import lax
from jax.experimental import pallas as pl
from jax.experimental.pallas import tpu as pltpu
from jax.experimental.pallas import tpu_sc as plsc

F32 = jnp.float32
BF16 = jnp.bfloat16
I32 = jnp.int32
U32 = jnp.uint32

GRID_W = 64
GLA_HEADS = 4
GLA_CHUNK = 64
GATE_NORMALIZER = 16.0
POOL_WINDOWS = (2, 4, 8, 16)
POOL_PAD_GRID_ROWS = 8
TOP_K = 4
RANK_LIMIT = 1 << 20
SWIGLU_LIMIT = 7.0
SWIGLU_ALPHA = 1.702
N_MOD = 6
EPS = 1e-6

LANES = 128
SUPER = 256
HEAD_PAIR_DK = 128
EXPERT_ROWS = 512
EXPERT_ROW_STEP = 128
MOE_PARTS = 2
EXPERT_VMEM_BYTES = 56 * 1024 * 1024
SC_CORES = 2
SC_SUBCORES = 16
SC_WORKERS = SC_CORES * SC_SUBCORES
SC_WINDOW = 32
SC_GATHER_WINDOW = 64


def _dot(a, b):
    return jnp.dot(a, b, preferred_element_type=F32)


def _dot_nt(a, b):
    return lax.dot_general(a, b, (((1,), (1,)), ((), ())), preferred_element_type=F32)


def _split_bf16(x):
    hi = x.astype(BF16)
    lo = (x - hi.astype(F32)).astype(BF16)
    return hi, lo


def _pack_bf16_pairs(x):
    c = x.shape[1] // 2
    lo = lax.bitcast_convert_type(x[:, :c].astype(BF16).astype(F32), U32)
    hi = lax.bitcast_convert_type(x[:, c:].astype(BF16).astype(F32), U32)
    return (lo >> 16) | hi


def _unpack_bf16_pairs(p):
    lo = lax.bitcast_convert_type(p << 16, F32)
    hi = lax.bitcast_convert_type(p & jnp.uint32(0xFFFF0000), F32)
    return lo, hi


def _rmsnorm(x, w):
    var = jnp.mean(x * x, axis=-1, keepdims=True)
    return x * lax.rsqrt(var + EPS) * w


def _mod_kernel(c_ref, w_ref, b_ref, o_ref):
    c = c_ref[...]
    s = c * jax.nn.sigmoid(c)
    o_ref[...] = jnp.dot(s, w_ref[...], precision=lax.Precision.HIGHEST,
                         preferred_element_type=F32) + b_ref[...]


def _mod_call(cc, w_ada, b_ada):
    rows, d = cc.shape
    n = w_ada.shape[1]
    tn = 1024
    return pl.pallas_call(
        _mod_kernel,
        grid=(n // tn,),
        in_specs=[pl.BlockSpec((rows, d), lambda j: (0, 0)),
                  pl.BlockSpec((d, tn), lambda j: (0, j)),
                  pl.BlockSpec((1, tn), lambda j: (0, j))],
        out_specs=pl.BlockSpec((rows, tn), lambda j: (0, j)),
        out_shape=jax.ShapeDtypeStruct((rows, n), F32),
        name="mod",
    )(cc, w_ada, b_ada)


def _inproj_kernel(x_ref, mod_ref, nw_ref, w_ref, wgk_ref, bgk_ref,
                   q_ref, k_ref, v_ref, vt_ref, g_ref, p_ref, gk_ref, *, qk, gw, pw, dk):
    x = x_ref[0]
    m = mod_ref[0]
    hm = (_rmsnorm(x, nw_ref[...]) * (1.0 + m[1:2]) + m[0:1]).astype(BF16)
    p = _dot(hm, w_ref[...])
    vt = p[:, 2 * qk:2 * qk + gw].T
    for s in range(vt_ref.shape[1]):
        vt_ref[0, s] = vt[:, s * SUPER:(s + 1) * SUPER].astype(BF16)
    o = 0
    q_ref[0] = (p[:, o:o + qk] * (dk ** -0.5)).astype(BF16); o += qk
    k_ref[0] = p[:, o:o + qk].astype(BF16); o += qk
    v_ref[0] = p[:, o:o + gw].astype(BF16); o += gw
    g_ref[0] = p[:, o:o + gw].astype(BF16); o += gw
    p_ref[0] = p[:, o:o + pw].astype(BF16); o += pw
    r = p[:, o:o + LANES]
    z = _dot(r.astype(BF16), wgk_ref[...]) + bgk_ref[...]
    gk_ref[0] = (jnp.minimum(z, 0.0) - jnp.log1p(jnp.exp(-jnp.abs(z)))) * (1.0 / GATE_NORMALIZER)


def _inproj_call(x, mod, nw, w, wgk, bgk, *, qk, gw, pw, dk, tm):
    b, t, d = x.shape
    n_in = w.shape[1]
    bs = lambda width: pl.BlockSpec((1, tm, width), lambda i, j: (i, j, 0))
    const = lambda shape: pl.BlockSpec(shape, lambda i, j: (0,) * len(shape))
    per_batch = mod.shape[0] > 1
    sds = jax.ShapeDtypeStruct
    return pl.pallas_call(
        functools.partial(_inproj_kernel, qk=qk, gw=gw, pw=pw, dk=dk),
        grid=(b, t // tm),
        in_specs=[bs(d),
                  pl.BlockSpec((1, N_MOD, d), (lambda i, j: (i, 0, 0)) if per_batch else (lambda i, j: (0, 0, 0))),
                  const((1, d)), const((d, n_in)), const((LANES, 2 * qk)), const((1, 2 * qk))],
        out_specs=[bs(qk), bs(qk), bs(gw),
                   pl.BlockSpec((1, tm // SUPER, gw, SUPER), lambda i, j: (i, j, 0, 0)),
                   bs(gw), bs(pw), bs(2 * qk)],
        out_shape=[sds((b, t, qk), BF16), sds((b, t, qk), BF16), sds((b, t, gw), BF16),
                   sds((b, t // SUPER, gw, SUPER), BF16),
                   sds((b, t, gw), BF16), sds((b, t, pw), BF16), sds((b, t, 2 * qk), F32)],
        compiler_params=pltpu.CompilerParams(dimension_semantics=("arbitrary", "arbitrary")),
        name="inproj",
    )(x, mod, nw, w, wgk, bgk)


def _chunk_row(x, r):
    return jnp.concatenate(
        [jnp.broadcast_to(x[c * GLA_CHUNK + r:c * GLA_CHUNK + r + 1, :], (GLA_CHUNK, x.shape[1]))
         for c in range(x.shape[0] // GLA_CHUNK)], axis=0)


def _decay_sums(gk_f, gk_b, cm):
    n = gk_f.shape[1]
    hi, lo = _split_bf16(jnp.concatenate([gk_f, gk_b], axis=1))
    pre = _dot(cm, hi) + _dot(cm, lo)
    pre_b = pre[:, n:]
    return pre[:, :n], _chunk_row(pre_b, GLA_CHUNK - 1) - pre_b + gk_b


def _gla_super(q, k, v, vt, bcum, amask, bd_mask, st, fwd, want_out):
    nch = SUPER // GLA_CHUNK
    order = tuple(range(nch)) if fwd else tuple(reversed(range(nch)))
    last_row = GLA_CHUNK - 1 if fwd else 0
    mid_row = GLA_CHUNK // 2 - 1 if fwd else GLA_CHUNK // 2
    chunk_row = functools.partial(_chunk_row, bcum)
    chunk_of_row = lax.broadcasted_iota(I32, bcum.shape, 0) // GLA_CHUNK

    def by_chunk(x):
        return jnp.concatenate([jnp.where(chunk_of_row == c, x, 0.0).astype(BF16) for c in range(nch)], axis=1)

    blast = chunk_row(last_row)
    u_all = _dot(vt, by_chunk(k * jnp.exp(blast - bcum)))
    before = [None] * nch
    for c in order:
        before[c] = st
        decay = jnp.exp(bcum[c * GLA_CHUNK + last_row:c * GLA_CHUNK + last_row + 1, :])
        st = st * decay + jnp.where(bd_mask, u_all[:, c * HEAD_PAIR_DK:(c + 1) * HEAD_PAIR_DK], 0.0)
    if not want_out:
        return None, st
    bmid = chunk_row(mid_row)
    qt = q * jnp.exp(bcum - bmid)
    kt = (k * jnp.exp(bmid - bcum)).astype(BF16)
    lane = lax.broadcasted_iota(I32, qt.shape, 1)
    half = HEAD_PAIR_DK // 2
    o_heads = []
    for hh in range(2):
        sel = (lane < half) if hh == 0 else (lane >= half)
        a = _dot_nt(jnp.where(sel, qt, 0.0).astype(BF16), kt)
        a = jnp.where(amask, a, 0.0).astype(BF16)
        o_heads.append(_dot(a, v[:, hh * LANES:(hh + 1) * LANES]))
    qh = (q * jnp.exp(bcum)).astype(BF16)
    o_inter = jnp.concatenate(
        [_dot_nt(qh[c * GLA_CHUNK:(c + 1) * GLA_CHUNK], before[c].astype(BF16)) for c in range(nch)], axis=0)
    return jnp.concatenate(o_heads, axis=1) + o_inter, st


def _gla_kernel(q_ref, k_ref, v_ref, vt_ref, gkf_ref, gkb_ref, g_ref, kc_ref, vtc_ref, gkfc_ref, gkbc_ref,
                nw_ref, cmf_ref, cmb_ref, o_ref, stf_ref, stb_ref, of_ref, ob_ref):
    t = q_ref.shape[1]
    tc = kc_ref.shape[1]
    nsc, nscc = t // SUPER, tc // SUPER
    cmf = cmf_ref[...]
    cmb = cmb_ref[...]
    amask_f = cmf > 0
    amask_b = cmb > 0
    row = lax.broadcasted_iota(I32, (2 * LANES, HEAD_PAIR_DK), 0)
    lane = lax.broadcasted_iota(I32, (2 * LANES, HEAD_PAIR_DK), 1)
    bd_mask = (row < LANES) == (lane < HEAD_PAIR_DK // 2)

    def group_rows(j):
        start = j * SUPER
        return pl.ds(start if isinstance(start, int) else pl.multiple_of(start, SUPER), SUPER)

    def ctx_state(bcum, fwd, j, st):
        return _gla_super(None, kc_ref[0, group_rows(j), :].astype(F32), None, vtc_ref[0, j], bcum,
                          None, bd_mask, st, fwd, False)[1]

    def latent(bcum, amask, fwd, j, st):
        rows = group_rows(j)
        return _gla_super(q_ref[0, rows, :].astype(F32), k_ref[0, rows, :].astype(F32), v_ref[0, rows, :],
                          vt_ref[0, j], bcum, amask, bd_mask, st, fwd, True)

    stf = jnp.zeros(stf_ref.shape, F32)
    stb = jnp.zeros(stb_ref.shape, F32)
    for j in range(nscc):
        jb = nscc - 1 - j
        bcum_f, bcum_b = _decay_sums(gkfc_ref[0, group_rows(j), :], gkbc_ref[0, group_rows(jb), :], cmf)
        stf = ctx_state(bcum_f, True, j, stf)
        stb = ctx_state(bcum_b, False, jb, stb)
    stf_ref[...] = stf
    stb_ref[...] = stb

    def scan_body(jj, carry):
        jb = nsc - 1 - jj
        bcum_f, bcum_b = _decay_sums(gkf_ref[0, group_rows(jj), :], gkb_ref[0, group_rows(jb), :], cmf)
        of, stf = latent(bcum_f, amask_f, True, jj, stf_ref[...])
        of_ref[pl.ds(pl.multiple_of(jj * SUPER, SUPER), SUPER), :] = of
        stf_ref[...] = stf
        ob, stb = latent(bcum_b, amask_b, False, jb, stb_ref[...])
        ob_ref[pl.ds(pl.multiple_of(jb * SUPER, SUPER), SUPER), :] = ob
        stb_ref[...] = stb
        return carry

    lax.fori_loop(0, nsc, scan_body, 0, unroll=8)

    nw = nw_ref[...]

    def out_body(j, carry):
        rows = pl.ds(pl.multiple_of(j * SUPER, SUPER), SUPER)
        o = of_ref[rows, :] + ob_ref[rows, :]
        g = g_ref[0, rows, :].astype(F32)
        half_g = 0.5 * g
        gate = half_g * (1.0 + jnp.tanh(half_g))
        for hh in range(2):
            oh = o[:, hh * LANES:(hh + 1) * LANES]
            on = oh * lax.rsqrt(jnp.mean(oh * oh, axis=-1, keepdims=True) + EPS) * nw
            o_ref[0, rows, hh * LANES:(hh + 1) * LANES] = (on * gate[:, hh * LANES:(hh + 1) * LANES]).astype(BF16)
        return carry

    lax.fori_loop(0, nsc, out_body, 0)


def _gla_masks():
    i = np.arange(SUPER)
    same = (i[:, None] // GLA_CHUNK) == (i[None, :] // GLA_CHUNK)
    fwd = same & (i[None, :] <= i[:, None])
    bwd = same & (i[None, :] >= i[:, None])
    return jnp.asarray(fwd, BF16), jnp.asarray(bwd, BF16)


def _gla_call(q, k, v, vt, gk, g, kc, vtc, gkc, nw):
    b, t, qk = q.shape
    tc = kc.shape[1]
    npair = qk // HEAD_PAIR_DK
    cmf, cmb = _gla_masks()
    lat = lambda width, off: pl.BlockSpec((1, t, width), lambda i, j: (i, 0, j + off))
    ctx = lambda width, off: pl.BlockSpec((1, tc, width), lambda i, j: (i, 0, j + off))
    tr = lambda n_groups: pl.BlockSpec((1, n_groups, 2 * LANES, SUPER), lambda i, j: (i, 0, j, 0))
    const = lambda shape: pl.BlockSpec(shape, lambda i, j: (0,) * len(shape))
    return pl.pallas_call(
        _gla_kernel,
        grid=(b, npair),
        in_specs=[lat(HEAD_PAIR_DK, 0), lat(HEAD_PAIR_DK, 0), lat(2 * LANES, 0), tr(t // SUPER),
                  lat(HEAD_PAIR_DK, 0), lat(HEAD_PAIR_DK, npair), lat(2 * LANES, 0),
                  ctx(HEAD_PAIR_DK, 0), tr(tc // SUPER), ctx(HEAD_PAIR_DK, 0), ctx(HEAD_PAIR_DK, npair),
                  const((1, LANES)), const(cmf.shape), const(cmb.shape)],
        out_specs=lat(2 * LANES, 0),
        out_shape=jax.ShapeDtypeStruct((b, t, v.shape[2]), BF16),
        scratch_shapes=[pltpu.VMEM((2 * LANES, HEAD_PAIR_DK), F32), pltpu.VMEM((2 * LANES, HEAD_PAIR_DK), F32),
                        pltpu.VMEM((t, 2 * LANES), F32), pltpu.VMEM((t, 2 * LANES), F32)],
        compiler_params=pltpu.CompilerParams(dimension_semantics=("arbitrary", "arbitrary")),
        name="gla",
    )(q, k, v, vt, gk, gk, g, kc, vtc, gkc, gkc, nw, cmf, cmb)


def _pool_kernel(x_ref, cm_ref, cnt_ref, wp_ref, ps_ref, o_ref, a_ref, b_ref, pooled_ref):
    t = x_ref.shape[1]
    pad = POOL_PAD_GRID_ROWS * GRID_W
    total = t + 2 * pad
    for gi, w in enumerate(POOL_WINDOWS):
        lo = w // 2
        cols = slice(gi * LANES, (gi + 1) * LANES)
        cmat = cm_ref[gi]
        a_ref[0:pad, :] = jnp.zeros((pad, LANES), F32)
        a_ref[pad + t:total, :] = jnp.zeros((pad, LANES), F32)
        for blk in range(0, t // SUPER, 2):
            r0, r1, r2 = blk * SUPER, (blk + 1) * SUPER, (blk + 2) * SUPER
            both = _dot(cmat, jnp.concatenate([x_ref[0, r0:r1, cols], x_ref[0, r1:r2, cols]], axis=1))
            a_ref[pad + r0:pad + r1, :] = both[:, :LANES]
            a_ref[pad + r1:pad + r2, :] = both[:, LANES:]
        src, dst = a_ref, b_ref
        m = 1
        while m < w:
            sh = m * GRID_W
            dst[0:total - sh, :] = src[0:total - sh, :] + src[sh:total, :]
            src, dst = dst, src
            m *= 2
        first = pad - lo * GRID_W
        pooled = src[first:first + t, :] / cnt_ref[gi] - x_ref[0, :, cols].astype(F32)
        pooled_ref[:, cols] = pooled.astype(BF16)
    for pair in range(wp_ref.shape[0]):
        cols = slice(pair * 2 * LANES, (pair + 1) * 2 * LANES)
        o_ref[0, :, cols] = (_dot(pooled_ref[:, cols], wp_ref[pair]) * ps_ref[:, cols]).astype(BF16)


def _pool_col_mats():
    i = np.arange(SUPER)
    same_row = (i[:, None] // GRID_W) == (i[None, :] // GRID_W)
    d = i[None, :] - i[:, None]
    mats = []
    for w in POOL_WINDOWS:
        lo = w // 2
        hi = w - 1 - lo
        mats.append(same_row & (d >= -lo) & (d <= hi))
    return jnp.asarray(np.stack(mats), BF16)


def _pool_counts(t):
    rows = t // GRID_W
    r = np.arange(t) // GRID_W
    c = np.arange(t) % GRID_W
    out = []
    for w in POOL_WINDOWS:
        lo = w // 2
        hi = w - 1 - lo
        cnt_r = np.minimum(r + hi + 1, rows) - np.maximum(r - lo, 0)
        cnt_c = np.minimum(c + hi + 1, GRID_W) - np.maximum(c - lo, 0)
        out.append(np.broadcast_to((cnt_r * cnt_c).astype(np.float32)[:, None], (t, LANES)))
    return jnp.asarray(np.stack(out))


def _pool_call(xp, w_pool, pool_scale):
    b, t, pw = xp.shape
    ng = len(POOL_WINDOWS)
    assert max(POOL_WINDOWS) // 2 <= POOL_PAD_GRID_ROWS and t % (2 * SUPER) == 0 and ng % 2 == 0
    zero = jnp.zeros((LANES, LANES), w_pool.dtype)
    w_pairs = jnp.stack([jnp.block([[w_pool[2 * p], zero], [zero, w_pool[2 * p + 1]]]) for p in range(ng // 2)])
    cm = _pool_col_mats()
    cnt = _pool_counts(t)
    staged = t + 2 * POOL_PAD_GRID_ROWS * GRID_W
    const = lambda shape: pl.BlockSpec(shape, lambda i: (0,) * len(shape))
    return pl.pallas_call(
        _pool_kernel,
        grid=(b,),
        in_specs=[pl.BlockSpec((1, t, pw), lambda i: (i, 0, 0)),
                  const(cm.shape), const(cnt.shape), const(w_pairs.shape), const((1, pw))],
        out_specs=pl.BlockSpec((1, t, pw), lambda i: (i, 0, 0)),
        out_shape=jax.ShapeDtypeStruct((b, t, pw), BF16),
        scratch_shapes=[pltpu.VMEM((staged, LANES), F32), pltpu.VMEM((staged, LANES), F32),
                        pltpu.VMEM((t, pw), BF16)],
        compiler_params=pltpu.CompilerParams(dimension_semantics=("arbitrary",)),
        name="pool",
    )(xp, cm, cnt, w_pairs, pool_scale)


def _route_kernel(gla_ref, pool_ref, x_ref, mod_ref, wo_ref, nw_ref, wr_ref, br_ref, ut_ref,
                  h_ref, xt_ref, code_ref, wt_ref, cnt_ref, run_ref, wr2_ref, *, gw, n_exp):
    i = pl.program_id(0)

    @pl.when(i == 0)
    def _():
        run_ref[...] = jnp.zeros_like(run_ref)
        wh, wl = _split_bf16(wr_ref[...])
        wr2_ref[:LANES, :] = wh
        wr2_ref[LANES:, :] = wl

    m = mod_ref[0]
    acc = _dot(gla_ref[...], wo_ref[0:gw, :]) + _dot(pool_ref[...], wo_ref[gw:, :])
    h = x_ref[...] + m[2:3] * acc
    h_ref[...] = h
    xt = _rmsnorm(h, nw_ref[...]) * (1.0 + m[4:5]) + m[3:4]
    xt_ref[...] = _pack_bf16_pairs(xt)
    xh, xl = _split_bf16(xt)
    wr2 = wr2_ref[...]
    t1 = _dot_nt(wr2, xh)
    logits = t1[:LANES] + t1[LANES:] + _dot_nt(wr2[:LANES], xl) + br_ref[...]
    row = lax.broadcasted_iota(I32, logits.shape, 0)
    row_f = row.astype(F32)
    neg = jnp.float32(-jnp.inf)
    logits = jnp.where(row < n_exp, logits, neg)
    vals, hots, idxs = [], [], []
    for j in range(TOP_K):
        mx = jnp.max(logits, axis=0, keepdims=True)
        idx = jnp.min(jnp.where(logits == mx, row_f, float(LANES)), axis=0, keepdims=True)
        hot = row_f == idx
        vals.append(mx)
        hots.append(hot)
        idxs.append(idx)
        logits = jnp.where(hot, neg, logits)
    ex = [jnp.exp(v - vals[0]) for v in vals]
    den = ex[0] + ex[1] + ex[2] + ex[3]
    osum = jnp.where(hots[0] | hots[1] | hots[2] | hots[3], 1.0, 0.0)
    before = _dot(osum.astype(BF16), ut_ref[...]) + run_ref[:, 0:1]
    run = run_ref[:, 0:1] + jnp.sum(osum, axis=1, keepdims=True)
    run_ref[...] = jnp.broadcast_to(run, run_ref.shape)
    code_row = lax.broadcasted_iota(I32, code_ref.shape, 0)
    code = jnp.zeros(code_ref.shape, I32)
    w_t = jnp.zeros(logits.shape, F32)
    for j in range(TOP_K):
        rank = jnp.sum(jnp.where(hots[j], before, 0.0), axis=0, keepdims=True)
        code = jnp.where(code_row == j, idxs[j].astype(I32) * RANK_LIMIT + rank.astype(I32), code)
        w_t = jnp.where(row == j, ex[j] / den, w_t)
    code_ref[...] = code
    wt_ref[...] = w_t.T
    cnt_ref[...] = jnp.broadcast_to(run, cnt_ref.shape).astype(I32)


def _route_call(gla, pool, x, mod, w_out, nw, wr, br, *, tm, n_exp, first_tile, n_tiles):
    n_all, d = x.shape
    n = n_tiles * tm
    gw = gla.shape[1]
    t_per_b = n_all // mod.shape[0]
    ut = jnp.asarray(np.triu(np.ones((tm, tm), np.float32), 1), BF16)
    row_in = lambda width: pl.BlockSpec((tm, width), lambda i: (i + first_tile, 0))
    row = lambda width: pl.BlockSpec((tm, width), lambda i: (i, 0))
    const = lambda shape: pl.BlockSpec(shape, lambda i: (0,) * len(shape))
    sds = jax.ShapeDtypeStruct
    return pl.pallas_call(
        functools.partial(_route_kernel, gw=gw, n_exp=n_exp),
        grid=(n_tiles,),
        in_specs=[row_in(gw), row_in(pool.shape[1]), row_in(d),
                  pl.BlockSpec((1, N_MOD, d), lambda i: ((i + first_tile) * tm // t_per_b, 0, 0)),
                  const(w_out.shape), const((1, d)), const(wr.shape), const((LANES, 1)), const((tm, tm))],
        out_specs=[row(d), row(d // 2), pl.BlockSpec((8, tm), lambda i: (0, i)), row(LANES), const((LANES, LANES))],
        out_shape=[sds((n, d), F32), sds((n, d // 2), U32), sds((8, n), I32),
                   sds((n, LANES), F32), sds((LANES, LANES), I32)],
        scratch_shapes=[pltpu.VMEM((LANES, LANES), F32), pltpu.VMEM((2 * LANES, d), BF16)],
        compiler_params=pltpu.CompilerParams(dimension_semantics=("arbitrary",)),
        name="route",
    )(gla, pool, x, mod, w_out, nw, wr, br, ut)


def _plan_kernel(cnt_ref, code_ref, dest_ref, be_ref, nv_ref, first_ref, next_ref, slot_ref, nu_ref, start_ref,
                 *, n_exp, rows):
    @pl.when(pl.program_id(0) == 0)
    def _():
        blk = (lax.broadcasted_iota(I32, be_ref.shape, 0) * LANES + lax.broadcasted_iota(I32, be_ref.shape, 1))
        blk_row0 = blk * rows
        nxt_e = [None] * n_exp
        nxt = jnp.int32(-1)
        for e in reversed(range(n_exp)):
            nxt_e[e] = nxt
            nxt = jnp.where(cnt_ref[e] > 0, e, nxt)
        zeros = jnp.zeros(be_ref.shape, I32)
        be, end_valid, first, nxt_blk, slot = zeros, zeros, zeros, zeros - 1, zeros
        acc = jnp.int32(0)
        last_e = jnp.int32(0)
        ordinal = jnp.int32(0)
        for e in range(n_exp):
            c = cnt_ref[e]
            start_ref[e] = acc
            in_e = (blk_row0 >= acc) & (c > 0)
            end_valid = jnp.where(in_e, acc + c, end_valid)
            be = jnp.where(in_e, e, be)
            first = jnp.where(in_e, (blk_row0 == acc).astype(I32), first)
            nxt_blk = jnp.where(in_e, nxt_e[e], nxt_blk)
            slot = jnp.where(in_e, ordinal % 2, slot)
            acc = acc + (c + rows - 1) // rows * rows
            last_e = jnp.where(c > 0, e, last_e)
            ordinal = ordinal + (c > 0).astype(I32)
        n_used = acc // rows
        nu_ref[0] = n_used
        be_ref[...] = jnp.where(blk < n_used, be, last_e)
        nv_ref[...] = jnp.clip(end_valid - blk_row0, 0, rows)
        first_ref[...] = first
        next_ref[...] = nxt_blk
        slot_ref[...] = slot

    code = code_ref[...]
    e_vec = code // RANK_LIMIT
    dest = code % RANK_LIMIT
    for e in range(n_exp):
        dest = dest + jnp.where(e_vec == e, start_ref[e], 0)
    dest_ref[...] = dest


def _plan_call(counts, code_t, *, n_exp, rows, chunk):
    n = code_t.shape[1]
    sds = jax.ShapeDtypeStruct
    smem = pltpu.SMEM
    return pl.pallas_call(
        functools.partial(_plan_kernel, n_exp=n_exp, rows=rows),
        grid=(n // chunk,),
        in_specs=[pl.BlockSpec(memory_space=smem), pl.BlockSpec((8, chunk), lambda i: (0, i))],
        out_specs=[pl.BlockSpec((8, chunk), lambda i: (0, i))]
        + [pl.BlockSpec((8, LANES), lambda i: (0, 0))] * 5 + [pl.BlockSpec(memory_space=smem)],
        out_shape=[sds((8, n), I32)] + [sds((8, LANES), I32)] * 5 + [sds((1,), I32)],
        scratch_shapes=[pltpu.SMEM((n_exp,), I32)],
        compiler_params=pltpu.CompilerParams(dimension_semantics=("arbitrary",)),
        name="plan",
    )(counts, code_t)


def _sc_worker_id():
    return lax.axis_index("s") * SC_CORES + lax.axis_index("c")


def _sc_scatter_call(x, idx3, *, n_out):
    n, d = x.shape
    n_win_total, k, w = idx3.shape
    n_win = n_win_total // SC_WORKERS
    mesh = plsc.VectorSubcoreMesh(core_axis_name="c", subcore_axis_name="s")

    @functools.partial(
        pl.kernel, mesh=mesh,
        out_type=jax.ShapeDtypeStruct((n_out, d), x.dtype),
        scratch_types=[pltpu.VMEM((k, w), I32), pltpu.VMEM((w, d), x.dtype), pltpu.SemaphoreType.DMA],
        name="sc_dispatch",
    )
    def kern(x_hbm, idx_hbm, out_hbm, idx_v, rows_v, sem):
        wid = _sc_worker_id()

        @pl.loop(0, n_win)
        def _(i):
            win = wid * n_win + i
            pltpu.sync_copy(idx_hbm.at[win], idx_v)
            pltpu.sync_copy(x_hbm.at[pl.ds(win * w, w)], rows_v)
            for j in range(k):
                pltpu.async_copy(rows_v, out_hbm.at[idx_v.at[j]], sem).wait()

    return kern(x, idx3)


def _sc_gather_call(table, idx3):
    n_workers, n_win, w = idx3.shape
    d = table.shape[1]
    assert n_workers == SC_WORKERS and n_win % 2 == 0
    mesh = plsc.VectorSubcoreMesh(core_axis_name="c", subcore_axis_name="s")

    @functools.partial(
        pl.kernel, mesh=mesh,
        out_type=jax.ShapeDtypeStruct((n_workers * n_win * w, d), table.dtype),
        scratch_types=[pltpu.VMEM((n_win, w), I32), pltpu.VMEM((2, w, d), table.dtype),
                       pltpu.SemaphoreType.DMA((2,)), pltpu.SemaphoreType.DMA((2,))],
        name="sc_gather",
    )
    def kern(table_hbm, idx_hbm, out_hbm, idx_v, rows_v, gsem, osem):
        wid = _sc_worker_id()
        base = wid * n_win
        pltpu.sync_copy(idx_hbm.at[wid], idx_v)

        def gather(wi, b):
            return pltpu.make_async_copy(table_hbm.at[idx_v.at[wi]], rows_v.at[b], gsem.at[b])

        def put(wi, b):
            return pltpu.make_async_copy(rows_v.at[b], out_hbm.at[pl.ds((base + wi) * w, w)], osem.at[b])

        gather(0, 0).start()

        @pl.loop(0, n_win, step=2)
        def _(i):
            for b in range(2):
                wi = i + b

                @pl.when(wi + 1 < n_win)
                def _():
                    @pl.when(wi >= 1)
                    def _():
                        put(wi - 1, 1 - b).wait()
                    gather(wi + 1, 1 - b).start()

                gather(wi, b).wait()
                put(wi, b).start()

        put(n_win - 2, 0).wait()
        put(n_win - 1, 1).wait()

    return kern(table, idx3)


def _expert_kernel(be_ref, nu_ref, nv_ref, first_ref, next_ref, slot_ref,
                   x_ref, wgu_hbm, bgu_ref, wd_hbm, bd_ref, y_ref,
                   wgu_f32_ref, wd_f32_ref, wgu_bf_ref, wd_bf_ref, sem, *, d_ff):
    i = pl.program_id(0)
    slot = slot_ref[i]

    def fetch(e, s):
        return (pltpu.make_async_copy(wgu_hbm.at[e], wgu_f32_ref.at[s], sem.at[0, s]),
                pltpu.make_async_copy(wd_hbm.at[e], wd_f32_ref.at[s], sem.at[1, s]))

    @pl.when(i == 0)
    def _():
        for cp in fetch(be_ref[0], slot):
            cp.start()

    @pl.when(first_ref[i] == 1)
    def _():
        for cp in fetch(be_ref[i], slot):
            cp.wait()

        @pl.when(next_ref[i] >= 0)
        def _():
            for cp in fetch(next_ref[i], 1 - slot):
                cp.start()

        wgu_bf_ref[...] = wgu_f32_ref[slot].astype(BF16)
        wd_bf_ref[...] = wd_f32_ref[slot].astype(BF16)

    n_valid = nv_ref[i]
    n_rows = x_ref.shape[0]

    def expert_rows(rows):
        row = lax.broadcasted_iota(I32, (rows, x_ref.shape[1]), 0)
        lo, hi = _unpack_bf16_pairs(jnp.where(row < n_valid, x_ref[0:rows, :], jnp.uint32(0)))
        xb = jnp.concatenate([lo, hi], axis=1).astype(BF16)
        col = lax.broadcasted_iota(I32, bgu_ref.shape[1:], 1)
        gu = _dot(xb, wgu_bf_ref[...]) + (bgu_ref[0] + jnp.where(col >= d_ff, 1.0, 0.0))
        gate = jnp.minimum(gu[:, :d_ff], SWIGLU_LIMIT)
        up1 = jnp.clip(gu[:, d_ff:], 1.0 - SWIGLU_LIMIT, 1.0 + SWIGLU_LIMIT)
        act = up1 * (0.5 * gate) * (1.0 + jnp.tanh((0.5 * SWIGLU_ALPHA) * gate))
        y_ref[0:rows, :] = _pack_bf16_pairs(_dot(act.astype(BF16), wd_bf_ref[...]) + bd_ref[0])

    in_use = i < nu_ref[0]

    for rows in range(EXPERT_ROW_STEP, n_rows + 1, EXPERT_ROW_STEP):
        @pl.when(in_use & (n_valid > rows - EXPERT_ROW_STEP) & (n_valid <= rows))
        def _(rows=rows):
            expert_rows(rows)
            if rows < n_rows:
                y_ref[rows:, :] = jnp.zeros((n_rows - rows, y_ref.shape[1]), y_ref.dtype)


def _expert_call(plan, xs, w_gu, b_gu, w_down, b_down):
    block_e, n_valid, first, nxt, slot, n_used = plan
    n_pad = xs.shape[0]
    n_exp, d, two_ff = w_gu.shape
    d_ff = two_ff // 2
    nblk = n_pad // EXPERT_ROWS
    rows = lambda i, be, nu, *_: (jnp.minimum(i, nu[0] - 1), 0)
    per_e = lambda i, be, *_: (be[i], 0, 0)
    grid_spec = pltpu.PrefetchScalarGridSpec(
        num_scalar_prefetch=6,
        grid=(nblk,),
        in_specs=[pl.BlockSpec((EXPERT_ROWS, d // 2), rows),
                  pl.BlockSpec(memory_space=pl.ANY), pl.BlockSpec((1, 1, two_ff), per_e),
                  pl.BlockSpec(memory_space=pl.ANY), pl.BlockSpec((1, 1, d), per_e)],
        out_specs=pl.BlockSpec((EXPERT_ROWS, d // 2), rows),
        scratch_shapes=[pltpu.VMEM((2, d, two_ff), F32), pltpu.VMEM((2, d_ff, d), F32),
                        pltpu.VMEM((d, two_ff), BF16), pltpu.VMEM((d_ff, d), BF16),
                        pltpu.SemaphoreType.DMA((2, 2))],
    )
    flat = lambda a: a.reshape(-1)
    return pl.pallas_call(
        functools.partial(_expert_kernel, d_ff=d_ff),
        grid_spec=grid_spec,
        out_shape=jax.ShapeDtypeStruct((n_pad, d // 2), U32),
        compiler_params=pltpu.CompilerParams(dimension_semantics=("arbitrary",),
                                             vmem_limit_bytes=EXPERT_VMEM_BYTES),
        name="experts",
    )(flat(block_e), n_used, flat(n_valid), flat(first), flat(nxt), flat(slot),
      xs, w_gu, b_gu.reshape(n_exp, 1, two_ff), w_down, b_down.reshape(n_exp, 1, d))


def _combine_kernel(y4_ref, wt_ref, h_ref, mod_ref, fw_ref, o_ref):
    wt = wt_ref[...]
    acc_lo, acc_hi = None, None
    for j in range(TOP_K):
        lo, hi = _unpack_bf16_pairs(y4_ref[j])
        w = wt[:, j:j + 1]
        acc_lo = w * lo if j == 0 else acc_lo + w * lo
        acc_hi = w * hi if j == 0 else acc_hi + w * hi
    acc = jnp.concatenate([acc_lo, acc_hi], axis=1)
    m = mod_ref[0]
    o_ref[...] = _rmsnorm(h_ref[...] + m[5:6] * acc, fw_ref[...])


def _combine_call(y4, wts, h, mod, fw, prev_out, *, n, tg, first_tile):
    d = h.shape[1]
    t_per_b = n // mod.shape[0]
    part = lambda width: pl.BlockSpec((tg, width), lambda i: (i, 0))
    row = lambda width: pl.BlockSpec((tg, width), lambda i: (i + first_tile, 0))
    in_specs = [pl.BlockSpec((TOP_K, tg, d // 2), lambda i: (0, i, 0)),
                part(LANES), part(d),
                pl.BlockSpec((1, N_MOD, d), lambda i: ((i + first_tile) * tg // t_per_b, 0, 0)),
                pl.BlockSpec((1, d), lambda i: (0, 0))]
    args = [y4, wts, h, mod, fw]
    kern = _combine_kernel
    aliases = {}
    if prev_out is not None:
        in_specs.append(pl.BlockSpec(memory_space=pl.ANY))
        args.append(prev_out)
        kern = lambda y4_ref, wt_ref, h_ref, mod_ref, fw_ref, prev_ref, o_ref: _combine_kernel(
            y4_ref, wt_ref, h_ref, mod_ref, fw_ref, o_ref)
        aliases = {len(args) - 1: 0}
    return pl.pallas_call(
        kern,
        grid=(y4.shape[1] // tg,),
        in_specs=in_specs,
        out_specs=row(d),
        out_shape=jax.ShapeDtypeStruct((n, d), F32),
        input_output_aliases=aliases,
        compiler_params=pltpu.CompilerParams(dimension_semantics=("arbitrary",)),
        name="combine",
    )(*args)


def kernel(x, c, ctx, c_ctx, w_ada, b_ada, norm_mix_w, norm_mlp_w, w_in, w_gk_f, b_gk_f, w_gk_b, b_gk_b,
           gla_norm_w, w_pool, pool_scale, w_out, w_router, b_router, w_gu, b_gu, w_down, b_down,
           final_norm_w):
    b, t, d = x.shape
    assert w_ada.shape[0] == 1, "single-layer trunk"
    n_exp = w_router.shape[2]
    rank = w_gk_f.shape[1]
    qk = w_gk_f.shape[2]
    dk = qk // GLA_HEADS
    gw = GLA_HEADS * gla_norm_w.shape[1]
    pw = w_pool.shape[1] * w_pool.shape[2]
    assert w_in.shape[2] == 2 * qk + 2 * gw + 2 * rank + pw and 2 * rank <= LANES
    assert t % SUPER == 0 and ctx.shape[1] % SUPER == 0 and n_exp <= LANES

    rows = -(-(b + 1) // 8) * 8
    cc = jnp.concatenate([c, c_ctx[None, :], jnp.zeros((rows - b - 1, d), F32)], axis=0)
    mod = _mod_call(cc, w_ada[0], b_ada)
    mod_x = mod[:b].reshape(b, N_MOD, d)
    mod_c = mod[b:b + 1].reshape(1, N_MOD, d)

    wi = w_in[0]
    o_r = 2 * qk + 2 * gw
    w_cat = jnp.concatenate([wi[:, :o_r], wi[:, o_r + 2 * rank:], wi[:, o_r:o_r + 2 * rank],
                             jnp.zeros((d, LANES - 2 * rank), F32)], axis=1).astype(BF16)
    wgk = jnp.concatenate([jnp.pad(w_gk_f[0], ((0, 0), (0, qk))), jnp.pad(w_gk_b[0], ((0, 0), (qk, 0))),
                           jnp.zeros((LANES - 2 * rank, 2 * qk), F32)], axis=0)
    bgk = jnp.concatenate([b_gk_f[0], b_gk_b[0]])[None, :]
    proj = functools.partial(_inproj_call, nw=norm_mix_w, w=w_cat, wgk=wgk.astype(BF16), bgk=bgk,
                             qk=qk, gw=gw, pw=pw, dk=dk)
    q, k, v, vt, g, xp, gk = proj(x, mod_x, tm=1024)
    _, kc, _, vtc, _, _, gkc = proj(ctx, mod_c, tm=SUPER)

    gla = _gla_call(q, k, v, vt, gk, g, kc, vtc, gkc, gla_norm_w)
    pool = _pool_call(xp, w_pool[0].astype(BF16), pool_scale)

    n = b * t
    wr = jnp.pad(w_router[0].T, ((0, LANES - n_exp), (0, 0)))
    br = jnp.pad(b_router.T, ((0, LANES - n_exp), (0, 0)))
    n_part = n // MOE_PARTS
    tm, tg = 512, 1024
    assert n % MOE_PARTS == 0 and n_part % (SC_WORKERS * SC_WINDOW) == 0 and n_part % tm == 0 and n_part < RANK_LIMIT
    assert (n_part * TOP_K) % (SC_WORKERS * 2 * SC_GATHER_WINDOW) == 0
    n_pad = n_part * TOP_K + n_exp * EXPERT_ROWS
    nblk = n_pad // EXPERT_ROWS
    assert nblk <= 8 * LANES
    parts = range(MOE_PARTS)
    routed = [_route_call(gla.reshape(n, gw), pool.reshape(n, pw), x.reshape(n, d), mod_x, w_out[0].astype(BF16),
                          norm_mlp_w, wr, br, tm=tm, n_exp=n_exp,
                          first_tile=p * n_part // tm, n_tiles=n_part // tm) for p in parts]
    plans = [_plan_call(cnt[:n_exp, 0], code_t, n_exp=n_exp, rows=EXPERT_ROWS, chunk=min(n_part, 4096))
             for (_, _, code_t, _, cnt) in routed]
    dests = [plan[0][:TOP_K] for plan in plans]
    xs = [_sc_scatter_call(routed[p][1],
                           dests[p].reshape(TOP_K, n_part // SC_WINDOW, SC_WINDOW).transpose(1, 0, 2), n_out=n_pad)
          for p in parts]
    ys = [_expert_call(plans[p][1:], xs[p], w_gu[0], b_gu[0], w_down[0], b_down[0]) for p in parts]
    y4 = [_sc_gather_call(ys[p], dests[p].reshape(SC_WORKERS, -1, SC_GATHER_WINDOW)).reshape(TOP_K, n_part, d // 2)
          for p in parts]
    out = None
    for p in parts:
        out = _combine_call(y4[p], routed[p][3], routed[p][0], mod_x, final_norm_w[None, :], out,
                            n=n, tg=tg, first_tile=p * n_part // tg)
    return out.reshape(b, t, d)
```

```python
import functools

import numpy as np
import jax
import jax.numpy as jnp
from jax import lax
from jax.experimental import pallas as pl
from jax.experimental.pallas import tpu as pltpu
from jax.experimental.pallas import tpu_sc as plsc

F32 = jnp.float32
BF16 = jnp.bfloat16
I32 = jnp.int32
U32 = jnp.uint32

GRID_W = 64
GLA_HEADS = 4
GLA_CHUNK = 64
GATE_NORMALIZER = 16.0
POOL_WINDOWS = (2, 4, 8, 16)
POOL_PAD_GRID_ROWS = 8
TOP_K = 4
RANK_LIMIT = 1 << 20
SWIGLU_LIMIT = 7.0
SWIGLU_ALPHA = 1.702
N_MOD = 6
EPS = 1e-6

LANES = 128
SUPER = 256
HEAD_PAIR_DK = 128
EXPERT_ROWS = 512
EXPERT_ROW_STEP = 128
MOE_PARTS = 2
EXPERT_VMEM_BYTES = 56 * 1024 * 1024
SC_CORES = 2
SC_SUBCORES = 16
SC_WORKERS = SC_CORES * SC_SUBCORES
SC_WINDOW = 32
SC_GATHER_WINDOW = 64


def _dot(a, b):
    return jnp.dot(a, b, preferred_element_type=F32)


def _dot_nt(a, b):
    return lax.dot_general(a, b, (((1,), (1,)), ((), ())), preferred_element_type=F32)


def _split_bf16(x):
    hi = x.astype(BF16)
    lo = (x - hi.astype(F32)).astype(BF16)
    return hi, lo


def _pack_bf16_pairs(x):
    c = x.shape[1] // 2
    lo = lax.bitcast_convert_type(x[:, :c].astype(BF16).astype(F32), U32)
    hi = lax.bitcast_convert_type(x[:, c:].astype(BF16).astype(F32), U32)
    return (lo >> 16) | hi


def _unpack_bf16_pairs(p):
    lo = lax.bitcast_convert_type(p << 16, F32)
    hi = lax.bitcast_convert_type(p & jnp.uint32(0xFFFF0000), F32)
    return lo, hi


def _rmsnorm(x, w):
    var = jnp.mean(x * x, axis=-1, keepdims=True)
    return x * lax.rsqrt(var + EPS) * w


def _mod_kernel(c_ref, w_ref, b_ref, o_ref):
    c = c_ref[...]
    s = c * jax.nn.sigmoid(c)
    o_ref[...] = jnp.dot(s, w_ref[...], precision=lax.Precision.HIGHEST,
                         preferred_element_type=F32) + b_ref[...]


def _mod_call(cc, w_ada, b_ada):
    rows, d = cc.shape
    n = w_ada.shape[1]
    tn = 1024
    return pl.pallas_call(
        _mod_kernel,
        grid=(n // tn,),
        in_specs=[pl.BlockSpec((rows, d), lambda j: (0, 0)),
                  pl.BlockSpec((d, tn), lambda j: (0, j)),
                  pl.BlockSpec((1, tn), lambda j: (0, j))],
        out_specs=pl.BlockSpec((rows, tn), lambda j: (0, j)),
        out_shape=jax.ShapeDtypeStruct((rows, n), F32),
        name="mod",
    )(cc, w_ada, b_ada)


def _inproj_kernel(x_ref, mod_ref, nw_ref, w_ref, wgk_ref, bgk_ref,
                   q_ref, k_ref, v_ref, vt_ref, g_ref, p_ref, gk_ref, *, qk, gw, pw, dk):
    x = x_ref[0]
    m = mod_ref[0]
    hm = (_rmsnorm(x, nw_ref[...]) * (1.0 + m[1:2]) + m[0:1]).astype(BF16)
    p = _dot(hm, w_ref[...])
    vt = p[:, 2 * qk:2 * qk + gw].T
    for s in range(vt_ref.shape[1]):
        vt_ref[0, s] = vt[:, s * SUPER:(s + 1) * SUPER].astype(BF16)
    o = 0
    q_ref[0] = (p[:, o:o + qk] * (dk ** -0.5)).astype(BF16); o += qk
    k_ref[0] = p[:, o:o + qk].astype(BF16); o += qk
    v_ref[0] = p[:, o:o + gw].astype(BF16); o += gw
    g_ref[0] = p[:, o:o + gw].astype(BF16); o += gw
    p_ref[0] = p[:, o:o + pw].astype(BF16); o += pw
    r = p[:, o:o + LANES]
    z = _dot(r.astype(BF16), wgk_ref[...]) + bgk_ref[...]
    gk_ref[0] = (jnp.minimum(z, 0.0) - jnp.log1p(jnp.exp(-jnp.abs(z)))) * (1.0 / GATE_NORMALIZER)


def _inproj_call(x, mod, nw, w, wgk, bgk, *, qk, gw, pw, dk, tm):
    b, t, d = x.shape
    n_in = w.shape[1]
    bs = lambda width: pl.BlockSpec((1, tm, width), lambda i, j: (i, j, 0))
    const = lambda shape: pl.BlockSpec(shape, lambda i, j: (0,) * len(shape))
    per_batch = mod.shape[0] > 1
    sds = jax.ShapeDtypeStruct
    return pl.pallas_call(
        functools.partial(_inproj_kernel, qk=qk, gw=gw, pw=pw, dk=dk),
        grid=(b, t // tm),
        in_specs=[bs(d),
                  pl.BlockSpec((1, N_MOD, d), (lambda i, j: (i, 0, 0)) if per_batch else (lambda i, j: (0, 0, 0))),
                  const((1, d)), const((d, n_in)), const((LANES, 2 * qk)), const((1, 2 * qk))],
        out_specs=[bs(qk), bs(qk), bs(gw),
                   pl.BlockSpec((1, tm // SUPER, gw, SUPER), lambda i, j: (i, j, 0, 0)),
                   bs(gw), bs(pw), bs(2 * qk)],
        out_shape=[sds((b, t, qk), BF16), sds((b, t, qk), BF16), sds((b, t, gw), BF16),
                   sds((b, t // SUPER, gw, SUPER), BF16),
                   sds((b, t, gw), BF16), sds((b, t, pw), BF16), sds((b, t, 2 * qk), F32)],
        compiler_params=pltpu.CompilerParams(dimension_semantics=("arbitrary", "arbitrary")),
        name="inproj",
    )(x, mod, nw, w, wgk, bgk)


def _chunk_row(x, r):
    return jnp.concatenate(
        [jnp.broadcast_to(x[c * GLA_CHUNK + r:c * GLA_CHUNK + r + 1, :], (GLA_CHUNK, x.shape[1]))
         for c in range(x.shape[0] // GLA_CHUNK)], axis=0)


def _decay_sums(gk_f, gk_b, cm):
    n = gk_f.shape[1]
    hi, lo = _split_bf16(jnp.concatenate([gk_f, gk_b], axis=1))
    pre = _dot(cm, hi) + _dot(cm, lo)
    pre_b = pre[:, n:]
    return pre[:, :n], _chunk_row(pre_b, GLA_CHUNK - 1) - pre_b + gk_b


def _gla_super(q, k, v, vt, bcum, amask, bd_mask, st, fwd, want_out):
    nch = SUPER // GLA_CHUNK
    order = tuple(range(nch)) if fwd else tuple(reversed(range(nch)))
    last_row = GLA_CHUNK - 1 if fwd else 0
    mid_row = GLA_CHUNK // 2 - 1 if fwd else GLA_CHUNK // 2
    chunk_row = functools.partial(_chunk_row, bcum)
    chunk_of_row = lax.broadcasted_iota(I32, bcum.shape, 0) // GLA_CHUNK

    def by_chunk(x):
        return jnp.concatenate([jnp.where(chunk_of_row == c, x, 0.0).astype(BF16) for c in range(nch)], axis=1)

    blast = chunk_row(last_row)
    u_all = _dot(vt, by_chunk(k * jnp.exp(blast - bcum)))
    before = [None] * nch
    for c in order:
        before[c] = st
        decay = jnp.exp(bcum[c * GLA_CHUNK + last_row:c * GLA_CHUNK + last_row + 1, :])
        st = st * decay + jnp.where(bd_mask, u_all[:, c * HEAD_PAIR_DK:(c + 1) * HEAD_PAIR_DK], 0.0)
    if not want_out:
        return None, st
    bmid = chunk_row(mid_row)
    qt = q * jnp.exp(bcum - bmid)
    kt = (k * jnp.exp(bmid - bcum)).astype(BF16)
    lane = lax.broadcasted_iota(I32, qt.shape, 1)
    half = HEAD_PAIR_DK // 2
    o_heads = []
    for hh in range(2):
        sel = (lane < half) if hh == 0 else (lane >= half)
        a = _dot_nt(jnp.where(sel, qt, 0.0).astype(BF16), kt)
        a = jnp.where(amask, a, 0.0).astype(BF16)
        o_heads.append(_dot(a, v[:, hh * LANES:(hh + 1) * LANES]))
    qh = (q * jnp.exp(bcum)).astype(BF16)
    o_inter = jnp.concatenate(
        [_dot_nt(qh[c * GLA_CHUNK:(c + 1) * GLA_CHUNK], before[c].astype(BF16)) for c in range(nch)], axis=0)
    return jnp.concatenate(o_heads, axis=1) + o_inter, st


def _gla_kernel(q_ref, k_ref, v_ref, vt_ref, gkf_ref, gkb_ref, g_ref, kc_ref, vtc_ref, gkfc_ref, gkbc_ref,
                nw_ref, cmf_ref, cmb_ref, o_ref, stf_ref, stb_ref, of_ref, ob_ref):
    t = q_ref.shape[1]
    tc = kc_ref.shape[1]
    nsc, nscc = t // SUPER, tc // SUPER
    cmf = cmf_ref[...]
    cmb = cmb_ref[...]
    amask_f = cmf > 0
    amask_b = cmb > 0
    row = lax.broadcasted_iota(I32, (2 * LANES, HEAD_PAIR_DK), 0)
    lane = lax.broadcasted_iota(I32, (2 * LANES, HEAD_PAIR_DK), 1)
    bd_mask = (row < LANES) == (lane < HEAD_PAIR_DK // 2)

    def group_rows(j):
        start = j * SUPER
        return pl.ds(start if isinstance(start, int) else pl.multiple_of(start, SUPER), SUPER)

    def ctx_state(bcum, fwd, j, st):
        return _gla_super(None, kc_ref[0, group_rows(j), :].astype(F32), None, vtc_ref[0, j], bcum,
                          None, bd_mask, st, fwd, False)[1]

    def latent(bcum, amask, fwd, j, st):
        rows = group_rows(j)
        return _gla_super(q_ref[0, rows, :].astype(F32), k_ref[0, rows, :].astype(F32), v_ref[0, rows, :],
                          vt_ref[0, j], bcum, amask, bd_mask, st, fwd, True)

    stf = jnp.zeros(stf_ref.shape, F32)
    stb = jnp.zeros(stb_ref.shape, F32)
    for j in range(nscc):
        jb = nscc - 1 - j
        bcum_f, bcum_b = _decay_sums(gkfc_ref[0, group_rows(j), :], gkbc_ref[0, group_rows(jb), :], cmf)
        stf = ctx_state(bcum_f, True, j, stf)
        stb = ctx_state(bcum_b, False, jb, stb)
    stf_ref[...] = stf
    stb_ref[...] = stb

    nw = nw_ref[...]

    def finish(j, o):
        rows = group_rows(j)
        g = g_ref[0, rows, :].astype(F32)
        half_g = 0.5 * g
        gate = half_g * (1.0 + jnp.tanh(half_g))
        for hh in range(2):
            oh = o[:, hh * LANES:(hh + 1) * LANES]
            on = oh * lax.rsqrt(jnp.mean(oh * oh, axis=-1, keepdims=True) + EPS) * nw
            o_ref[0, rows, hh * LANES:(hh + 1) * LANES] = (on * gate[:, hh * LANES:(hh + 1) * LANES]).astype(BF16)

    for jj in range(nsc):
        jb = nsc - 1 - jj
        bcum_f, bcum_b = _decay_sums(gkf_ref[0, group_rows(jj), :], gkb_ref[0, group_rows(jb), :], cmf)
        of, stf = latent(bcum_f, amask_f, True, jj, stf_ref[...])
        stf_ref[...] = stf
        ob, stb = latent(bcum_b, amask_b, False, jb, stb_ref[...])
        stb_ref[...] = stb
        if jj == jb:
            finish(jj, of + ob)
        elif jj < jb:
            of_ref[group_rows(jj), :] = of
            ob_ref[group_rows(jb), :] = ob
        else:
            finish(jj, of + ob_ref[group_rows(jj), :])
            finish(jb, of_ref[group_rows(jb), :] + ob)


def _gla_masks():
    i = np.arange(SUPER)
    same = (i[:, None] // GLA_CHUNK) == (i[None, :] // GLA_CHUNK)
    fwd = same & (i[None, :] <= i[:, None])
    bwd = same & (i[None, :] >= i[:, None])
    return jnp.asarray(fwd, BF16), jnp.asarray(bwd, BF16)


def _gla_call(q, k, v, vt, gk, g, kc, vtc, gkc, nw):
    b, t, qk = q.shape
    tc = kc.shape[1]
    npair = qk // HEAD_PAIR_DK
    cmf, cmb = _gla_masks()
    lat = lambda width, off: pl.BlockSpec((1, t, width), lambda i, j: (i, 0, j + off))
    ctx = lambda width, off: pl.BlockSpec((1, tc, width), lambda i, j: (i, 0, j + off))
    tr = lambda n_groups: pl.BlockSpec((1, n_groups, 2 * LANES, SUPER), lambda i, j: (i, 0, j, 0))
    const = lambda shape: pl.BlockSpec(shape, lambda i, j: (0,) * len(shape))
    return pl.pallas_call(
        _gla_kernel,
        grid=(b, npair),
        in_specs=[lat(HEAD_PAIR_DK, 0), lat(HEAD_PAIR_DK, 0), lat(2 * LANES, 0), tr(t // SUPER),
                  lat(HEAD_PAIR_DK, 0), lat(HEAD_PAIR_DK, npair), lat(2 * LANES, 0),
                  ctx(HEAD_PAIR_DK, 0), tr(tc // SUPER), ctx(HEAD_PAIR_DK, 0), ctx(HEAD_PAIR_DK, npair),
                  const((1, LANES)), const(cmf.shape), const(cmb.shape)],
        out_specs=lat(2 * LANES, 0),
        out_shape=jax.ShapeDtypeStruct((b, t, v.shape[2]), BF16),
        scratch_shapes=[pltpu.VMEM((2 * LANES, HEAD_PAIR_DK), F32), pltpu.VMEM((2 * LANES, HEAD_PAIR_DK), F32),
                        pltpu.VMEM((t, 2 * LANES), F32), pltpu.VMEM((t, 2 * LANES), F32)],
        compiler_params=pltpu.CompilerParams(dimension_semantics=("arbitrary", "arbitrary")),
        name="gla",
    )(q, k, v, vt, gk, gk, g, kc, vtc, gkc, gkc, nw, cmf, cmb)


def _pool_kernel(x_ref, cm_ref, cnt_ref, wp_ref, ps_ref, o_ref, a_ref, b_ref):
    t = x_ref.shape[1]
    pad = POOL_PAD_GRID_ROWS * GRID_W
    total = t + 2 * pad
    for gi, w in enumerate(POOL_WINDOWS):
        lo = w // 2
        cols = slice(gi * LANES, (gi + 1) * LANES)
        cmat = cm_ref[gi]
        a_ref[0:pad, :] = jnp.zeros((pad, LANES), F32)
        a_ref[pad + t:total, :] = jnp.zeros((pad, LANES), F32)
        for blk in range(t // SUPER):
            rs = slice(blk * SUPER, (blk + 1) * SUPER)
            a_ref[pad + blk * SUPER:pad + (blk + 1) * SUPER, :] = _dot(cmat, x_ref[0, rs, cols])
        src, dst = a_ref, b_ref
        m = 1
        while m < w:
            sh = m * GRID_W
            dst[0:total - sh, :] = src[0:total - sh, :] + src[sh:total, :]
            src, dst = dst, src
            m *= 2
        first = pad - lo * GRID_W
        pooled = src[first:first + t, :] / cnt_ref[gi] - x_ref[0, :, cols].astype(F32)
        yp = _dot(pooled.astype(BF16), wp_ref[gi]) * ps_ref[:, cols]
        o_ref[0, :, cols] = yp.astype(BF16)


def _pool_col_mats():
    i = np.arange(SUPER)
    same_row = (i[:, None] // GRID_W) == (i[None, :] // GRID_W)
    d = i[None, :] - i[:, None]
    mats = []
    for w in POOL_WINDOWS:
        lo = w // 2
        hi = w - 1 - lo
        mats.append(same_row & (d >= -lo) & (d <= hi))
    return jnp.asarray(np.stack(mats), BF16)


def _pool_counts(t):
    rows = t // GRID_W
    r = np.arange(t) // GRID_W
    c = np.arange(t) % GRID_W
    out = []
    for w in POOL_WINDOWS:
        lo = w // 2
        hi = w - 1 - lo
        cnt_r = np.minimum(r + hi + 1, rows) - np.maximum(r - lo, 0)
        cnt_c = np.minimum(c + hi + 1, GRID_W) - np.maximum(c - lo, 0)
        out.append(np.broadcast_to((cnt_r * cnt_c).astype(np.float32)[:, None], (t, LANES)))
    return jnp.asarray(np.stack(out))


def _pool_call(xp, w_pool, pool_scale):
    b, t, pw = xp.shape
    ng = len(POOL_WINDOWS)
    assert max(POOL_WINDOWS) // 2 <= POOL_PAD_GRID_ROWS and t % GRID_W == 0
    cm = _pool_col_mats()
    cnt = _pool_counts(t)
    staged = t + 2 * POOL_PAD_GRID_ROWS * GRID_W
    const = lambda shape: pl.BlockSpec(shape, lambda i: (0,) * len(shape))
    return pl.pallas_call(
        _pool_kernel,
        grid=(b,),
        in_specs=[pl.BlockSpec((1, t, pw), lambda i: (i, 0, 0)),
                  const(cm.shape), const(cnt.shape), const((ng, LANES, LANES)), const((1, pw))],
        out_specs=pl.BlockSpec((1, t, pw), lambda i: (i, 0, 0)),
        out_shape=jax.ShapeDtypeStruct((b, t, pw), BF16),
        scratch_shapes=[pltpu.VMEM((staged, LANES), F32), pltpu.VMEM((staged, LANES), F32)],
        compiler_params=pltpu.CompilerParams(dimension_semantics=("arbitrary",)),
        name="pool",
    )(xp, cm, cnt, w_pool, pool_scale)


def _route_kernel(gla_ref, pool_ref, x_ref, mod_ref, wo_ref, nw_ref, wr_ref, br_ref, ut_ref,
                  h_ref, xt_ref, code_ref, wt_ref, cnt_ref, run_ref, wr2_ref, *, gw, n_exp):
    i = pl.program_id(0)

    @pl.when(i == 0)
    def _():
        run_ref[...] = jnp.zeros_like(run_ref)
        wh, wl = _split_bf16(wr_ref[...])
        wr2_ref[:LANES, :] = wh
        wr2_ref[LANES:, :] = wl

    m = mod_ref[0]
    acc = _dot(gla_ref[...], wo_ref[0:gw, :]) + _dot(pool_ref[...], wo_ref[gw:, :])
    h = x_ref[...] + m[2:3] * acc
    h_ref[...] = h
    xt = _rmsnorm(h, nw_ref[...]) * (1.0 + m[4:5]) + m[3:4]
    xt_ref[...] = _pack_bf16_pairs(xt)
    xh, xl = _split_bf16(xt)
    wr2 = wr2_ref[...]
    t1 = _dot_nt(wr2, xh)
    logits = t1[:LANES] + t1[LANES:] + _dot_nt(wr2[:LANES], xl) + br_ref[...]
    row = lax.broadcasted_iota(I32, logits.shape, 0)
    row_f = row.astype(F32)
    neg = jnp.float32(-jnp.inf)
    logits = jnp.where(row < n_exp, logits, neg)
    vals, hots, idxs = [], [], []
    for j in range(TOP_K):
        mx = jnp.max(logits, axis=0, keepdims=True)
        idx = jnp.min(jnp.where(logits == mx, row_f, float(LANES)), axis=0, keepdims=True)
        hot = row_f == idx
        vals.append(mx)
        hots.append(hot)
        idxs.append(idx)
        logits = jnp.where(hot, neg, logits)
    ex = [jnp.exp(v - vals[0]) for v in vals]
    den = ex[0] + ex[1] + ex[2] + ex[3]
    osum = jnp.where(hots[0] | hots[1] | hots[2] | hots[3], 1.0, 0.0)
    before = _dot(osum.astype(BF16), ut_ref[...]) + run_ref[:, 0:1]
    run = run_ref[:, 0:1] + jnp.sum(osum, axis=1, keepdims=True)
    run_ref[...] = jnp.broadcast_to(run, run_ref.shape)
    code_row = lax.broadcasted_iota(I32, code_ref.shape, 0)
    code = jnp.zeros(code_ref.shape, I32)
    w_t = jnp.zeros(logits.shape, F32)
    for j in range(TOP_K):
        rank = jnp.sum(jnp.where(hots[j], before, 0.0), axis=0, keepdims=True)
        code = jnp.where(code_row == j, idxs[j].astype(I32) * RANK_LIMIT + rank.astype(I32), code)
        w_t = jnp.where(row == j, ex[j] / den, w_t)
    code_ref[...] = code
    wt_ref[...] = w_t.T
    cnt_ref[...] = jnp.broadcast_to(run, cnt_ref.shape).astype(I32)


def _route_call(gla, pool, x, mod, w_out, nw, wr, br, *, tm, n_exp, first_tile, n_tiles):
    n_all, d = x.shape
    n = n_tiles * tm
    gw = gla.shape[1]
    t_per_b = n_all // mod.shape[0]
    ut = jnp.asarray(np.triu(np.ones((tm, tm), np.float32), 1), BF16)
    row_in = lambda width: pl.BlockSpec((tm, width), lambda i: (i + first_tile, 0))
    row = lambda width: pl.BlockSpec((tm, width), lambda i: (i, 0))
    const = lambda shape: pl.BlockSpec(shape, lambda i: (0,) * len(shape))
    sds = jax.ShapeDtypeStruct
    return pl.pallas_call(
        functools.partial(_route_kernel, gw=gw, n_exp=n_exp),
        grid=(n_tiles,),
        in_specs=[row_in(gw), row_in(pool.shape[1]), row_in(d),
                  pl.BlockSpec((1, N_MOD, d), lambda i: ((i + first_tile) * tm // t_per_b, 0, 0)),
                  const(w_out.shape), const((1, d)), const(wr.shape), const((LANES, 1)), const((tm, tm))],
        out_specs=[row(d), row(d // 2), pl.BlockSpec((8, tm), lambda i: (0, i)), row(LANES), const((LANES, LANES))],
        out_shape=[sds((n, d), F32), sds((n, d // 2), U32), sds((8, n), I32),
                   sds((n, LANES), F32), sds((LANES, LANES), I32)],
        scratch_shapes=[pltpu.VMEM((LANES, LANES), F32), pltpu.VMEM((2 * LANES, d), BF16)],
        compiler_params=pltpu.CompilerParams(dimension_semantics=("arbitrary",)),
        name="route",
    )(gla, pool, x, mod, w_out, nw, wr, br, ut)


def _plan_kernel(cnt_ref, code_ref, dest_ref, be_ref, nv_ref, first_ref, next_ref, slot_ref, nu_ref, start_ref,
                 *, n_exp, rows):
    @pl.when(pl.program_id(0) == 0)
    def _():
        blk = (lax.broadcasted_iota(I32, be_ref.shape, 0) * LANES + lax.broadcasted_iota(I32, be_ref.shape, 1))
        blk_row0 = blk * rows
        nxt_e = [None] * n_exp
        nxt = jnp.int32(-1)
        for e in reversed(range(n_exp)):
            nxt_e[e] = nxt
            nxt = jnp.where(cnt_ref[e] > 0, e, nxt)
        zeros = jnp.zeros(be_ref.shape, I32)
        be, end_valid, first, nxt_blk, slot = zeros, zeros, zeros, zeros - 1, zeros
        acc = jnp.int32(0)
        last_e = jnp.int32(0)
        ordinal = jnp.int32(0)
        for e in range(n_exp):
            c = cnt_ref[e]
            start_ref[e] = acc
            in_e = (blk_row0 >= acc) & (c > 0)
            end_valid = jnp.where(in_e, acc + c, end_valid)
            be = jnp.where(in_e, e, be)
            first = jnp.where(in_e, (blk_row0 == acc).astype(I32), first)
            nxt_blk = jnp.where(in_e, nxt_e[e], nxt_blk)
            slot = jnp.where(in_e, ordinal % 2, slot)
            acc = acc + (c + rows - 1) // rows * rows
            last_e = jnp.where(c > 0, e, last_e)
            ordinal = ordinal + (c > 0).astype(I32)
        n_used = acc // rows
        nu_ref[0] = n_used
        be_ref[...] = jnp.where(blk < n_used, be, last_e)
        nv_ref[...] = jnp.clip(end_valid - blk_row0, 0, rows)
        first_ref[...] = first
        next_ref[...] = nxt_blk
        slot_ref[...] = slot

    code = code_ref[...]
    e_vec = code // RANK_LIMIT
    dest = code % RANK_LIMIT
    for e in range(n_exp):
        dest = dest + jnp.where(e_vec == e, start_ref[e], 0)
    dest_ref[...] = dest


def _plan_call(counts, code_t, *, n_exp, rows, chunk):
    n = code_t.shape[1]
    sds = jax.ShapeDtypeStruct
    smem = pltpu.SMEM
    return pl.pallas_call(
        functools.partial(_plan_kernel, n_exp=n_exp, rows=rows),
        grid=(n // chunk,),
        in_specs=[pl.BlockSpec(memory_space=smem), pl.BlockSpec((8, chunk), lambda i: (0, i))],
        out_specs=[pl.BlockSpec((8, chunk), lambda i: (0, i))]
        + [pl.BlockSpec((8, LANES), lambda i: (0, 0))] * 5 + [pl.BlockSpec(memory_space=smem)],
        out_shape=[sds((8, n), I32)] + [sds((8, LANES), I32)] * 5 + [sds((1,), I32)],
        scratch_shapes=[pltpu.SMEM((n_exp,), I32)],
        compiler_params=pltpu.CompilerParams(dimension_semantics=("arbitrary",)),
        name="plan",
    )(counts, code_t)


def _sc_worker_id():
    return lax.axis_index("s") * SC_CORES + lax.axis_index("c")


def _sc_scatter_call(x, idx3, *, n_out):
    n, d = x.shape
    n_win_total, k, w = idx3.shape
    n_win = n_win_total // SC_WORKERS
    mesh = plsc.VectorSubcoreMesh(core_axis_name="c", subcore_axis_name="s")

    @functools.partial(
        pl.kernel, mesh=mesh,
        out_type=jax.ShapeDtypeStruct((n_out, d), x.dtype),
        scratch_types=[pltpu.VMEM((k, w), I32), pltpu.VMEM((w, d), x.dtype), pltpu.SemaphoreType.DMA],
        name="sc_dispatch",
    )
    def kern(x_hbm, idx_hbm, out_hbm, idx_v, rows_v, sem):
        wid = _sc_worker_id()

        @pl.loop(0, n_win)
        def _(i):
            win = wid * n_win + i
            pltpu.sync_copy(idx_hbm.at[win], idx_v)
            pltpu.sync_copy(x_hbm.at[pl.ds(win * w, w)], rows_v)
            for j in range(k):
                pltpu.async_copy(rows_v, out_hbm.at[idx_v.at[j]], sem).wait()

    return kern(x, idx3)


def _sc_gather_call(table, idx3):
    n_workers, n_win, w = idx3.shape
    d = table.shape[1]
    assert n_workers == SC_WORKERS and n_win % 2 == 0
    mesh = plsc.VectorSubcoreMesh(core_axis_name="c", subcore_axis_name="s")

    @functools.partial(
        pl.kernel, mesh=mesh,
        out_type=jax.ShapeDtypeStruct((n_workers * n_win * w, d), table.dtype),
        scratch_types=[pltpu.VMEM((n_win, w), I32), pltpu.VMEM((2, w, d), table.dtype),
                       pltpu.SemaphoreType.DMA((2,)), pltpu.SemaphoreType.DMA((2,))],
        name="sc_gather",
    )
    def kern(table_hbm, idx_hbm, out_hbm, idx_v, rows_v, gsem, osem):
        wid = _sc_worker_id()
        base = wid * n_win
        pltpu.sync_copy(idx_hbm.at[wid], idx_v)

        def gather(wi, b):
            return pltpu.make_async_copy(table_hbm.at[idx_v.at[wi]], rows_v.at[b], gsem.at[b])

        def put(wi, b):
            return pltpu.make_async_copy(rows_v.at[b], out_hbm.at[pl.ds((base + wi) * w, w)], osem.at[b])

        gather(0, 0).start()

        @pl.loop(0, n_win, step=2)
        def _(i):
            for b in range(2):
                wi = i + b

                @pl.when(wi + 1 < n_win)
                def _():
                    @pl.when(wi >= 1)
                    def _():
                        put(wi - 1, 1 - b).wait()
                    gather(wi + 1, 1 - b).start()

                gather(wi, b).wait()
                put(wi, b).start()

        put(n_win - 2, 0).wait()
        put(n_win - 1, 1).wait()

    return kern(table, idx3)


def _expert_kernel(be_ref, nu_ref, nv_ref, first_ref, next_ref, slot_ref,
                   x_ref, wgu_hbm, bgu_ref, wd_hbm, bd_ref, y_ref,
                   wgu_f32_ref, wd_f32_ref, wgu_bf_ref, wd_bf_ref, sem, *, d_ff):
    i = pl.program_id(0)
    slot = slot_ref[i]

    def fetch(e, s):
        return (pltpu.make_async_copy(wgu_hbm.at[e], wgu_f32_ref.at[s], sem.at[0, s]),
                pltpu.make_async_copy(wd_hbm.at[e], wd_f32_ref.at[s], sem.at[1, s]))

    @pl.when(i == 0)
    def _():
        for cp in fetch(be_ref[0], slot):
            cp.start()

    @pl.when(first_ref[i] == 1)
    def _():
        for cp in fetch(be_ref[i], slot):
            cp.wait()

        @pl.when(next_ref[i] >= 0)
        def _():
            for cp in fetch(next_ref[i], 1 - slot):
                cp.start()

        wgu_bf_ref[...] = wgu_f32_ref[slot].astype(BF16)
        wd_bf_ref[...] = wd_f32_ref[slot].astype(BF16)

    n_valid = nv_ref[i]
    n_rows = x_ref.shape[0]

    def expert_rows(rows):
        row = lax.broadcasted_iota(I32, (rows, x_ref.shape[1]), 0)
        lo, hi = _unpack_bf16_pairs(jnp.where(row < n_valid, x_ref[0:rows, :], jnp.uint32(0)))
        xb = jnp.concatenate([lo, hi], axis=1).astype(BF16)
        col = lax.broadcasted_iota(I32, bgu_ref.shape[1:], 1)
        gu = _dot(xb, wgu_bf_ref[...]) + (bgu_ref[0] + jnp.where(col >= d_ff, 1.0, 0.0))
        gate = jnp.minimum(gu[:, :d_ff], SWIGLU_LIMIT)
        up1 = jnp.clip(gu[:, d_ff:], 1.0 - SWIGLU_LIMIT, 1.0 + SWIGLU_LIMIT)
        act = up1 * (0.5 * gate) * (1.0 + jnp.tanh((0.5 * SWIGLU_ALPHA) * gate))
        y_ref[0:rows, :] = _pack_bf16_pairs(_dot(act.astype(BF16), wd_bf_ref[...]) + bd_ref[0])

    in_use = i < nu_ref[0]

    for rows in range(EXPERT_ROW_STEP, n_rows + 1, EXPERT_ROW_STEP):
        @pl.when(in_use & (n_valid > rows - EXPERT_ROW_STEP) & (n_valid <= rows))
        def _(rows=rows):
            expert_rows(rows)
            if rows < n_rows:
                y_ref[rows:, :] = jnp.zeros((n_rows - rows, y_ref.shape[1]), y_ref.dtype)


def _expert_call(plan, xs, w_gu, b_gu, w_down, b_down):
    block_e, n_valid, first, nxt, slot, n_used = plan
    n_pad = xs.shape[0]
    n_exp, d, two_ff = w_gu.shape
    d_ff = two_ff // 2
    nblk = n_pad // EXPERT_ROWS
    rows = lambda i, be, nu, *_: (jnp.minimum(i, nu[0] - 1), 0)
    per_e = lambda i, be, *_: (be[i], 0, 0)
    grid_spec = pltpu.PrefetchScalarGridSpec(
        num_scalar_prefetch=6,
        grid=(nblk,),
        in_specs=[pl.BlockSpec((EXPERT_ROWS, d // 2), rows),
                  pl.BlockSpec(memory_space=pl.ANY), pl.BlockSpec((1, 1, two_ff), per_e),
                  pl.BlockSpec(memory_space=pl.ANY), pl.BlockSpec((1, 1, d), per_e)],
        out_specs=pl.BlockSpec((EXPERT_ROWS, d // 2), rows),
        scratch_shapes=[pltpu.VMEM((2, d, two_ff), F32), pltpu.VMEM((2, d_ff, d), F32),
                        pltpu.VMEM((d, two_ff), BF16), pltpu.VMEM((d_ff, d), BF16),
                        pltpu.SemaphoreType.DMA((2, 2))],
    )
    flat = lambda a: a.reshape(-1)
    return pl.pallas_call(
        functools.partial(_expert_kernel, d_ff=d_ff),
        grid_spec=grid_spec,
        out_shape=jax.ShapeDtypeStruct((n_pad, d // 2), U32),
        compiler_params=pltpu.CompilerParams(dimension_semantics=("arbitrary",),
                                             vmem_limit_bytes=EXPERT_VMEM_BYTES),
        name="experts",
    )(flat(block_e), n_used, flat(n_valid), flat(first), flat(nxt), flat(slot),
      xs, w_gu, b_gu.reshape(n_exp, 1, two_ff), w_down, b_down.reshape(n_exp, 1, d))


def _combine_kernel(y4_ref, wt_ref, h_ref, mod_ref, fw_ref, o_ref):
    wt = wt_ref[...]
    acc_lo, acc_hi = None, None
    for j in range(TOP_K):
        lo, hi = _unpack_bf16_pairs(y4_ref[j])
        w = wt[:, j:j + 1]
        acc_lo = w * lo if j == 0 else acc_lo + w * lo
        acc_hi = w * hi if j == 0 else acc_hi + w * hi
    acc = jnp.concatenate([acc_lo, acc_hi], axis=1)
    m = mod_ref[0]
    o_ref[...] = _rmsnorm(h_ref[...] + m[5:6] * acc, fw_ref[...])


def _combine_call(y4, wts, h, mod, fw, prev_out, *, n, tg, first_tile):
    d = h.shape[1]
    t_per_b = n // mod.shape[0]
    part = lambda width: pl.BlockSpec((tg, width), lambda i: (i, 0))
    row = lambda width: pl.BlockSpec((tg, width), lambda i: (i + first_tile, 0))
    in_specs = [pl.BlockSpec((TOP_K, tg, d // 2), lambda i: (0, i, 0)),
                part(LANES), part(d),
                pl.BlockSpec((1, N_MOD, d), lambda i: ((i + first_tile) * tg // t_per_b, 0, 0)),
                pl.BlockSpec((1, d), lambda i: (0, 0))]
    args = [y4, wts, h, mod, fw]
    kern = _combine_kernel
    aliases = {}
    if prev_out is not None:
        in_specs.append(pl.BlockSpec(memory_space=pl.ANY))
        args.append(prev_out)
        kern = lambda y4_ref, wt_ref, h_ref, mod_ref, fw_ref, prev_ref, o_ref: _combine_kernel(
            y4_ref, wt_ref, h_ref, mod_ref, fw_ref, o_ref)
        aliases = {len(args) - 1: 0}
    return pl.pallas_call(
        kern,
        grid=(y4.shape[1] // tg,),
        in_specs=in_specs,
        out_specs=row(d),
        out_shape=jax.ShapeDtypeStruct((n, d), F32),
        input_output_aliases=aliases,
        compiler_params=pltpu.CompilerParams(dimension_semantics=("arbitrary",)),
        name="combine",
    )(*args)


def kernel(x, c, ctx, c_ctx, w_ada, b_ada, norm_mix_w, norm_mlp_w, w_in, w_gk_f, b_gk_f, w_gk_b, b_gk_b,
           gla_norm_w, w_pool, pool_scale, w_out, w_router, b_router, w_gu, b_gu, w_down, b_down,
           final_norm_w):
    b, t, d = x.shape
    assert w_ada.shape[0] == 1, "single-layer trunk"
    n_exp = w_router.shape[2]
    rank = w_gk_f.shape[1]
    qk = w_gk_f.shape[2]
    dk = qk // GLA_HEADS
    gw = GLA_HEADS * gla_norm_w.shape[1]
    pw = w_pool.shape[1] * w_pool.shape[2]
    assert w_in.shape[2] == 2 * qk + 2 * gw + 2 * rank + pw and 2 * rank <= LANES
    assert t % SUPER == 0 and ctx.shape[1] % SUPER == 0 and n_exp <= LANES

    rows = -(-(b + 1) // 8) * 8
    cc = jnp.concatenate([c, c_ctx[None, :], jnp.zeros((rows - b - 1, d), F32)], axis=0)
    mod = _mod_call(cc, w_ada[0], b_ada)
    mod_x = mod[:b].reshape(b, N_MOD, d)
    mod_c = mod[b:b + 1].reshape(1, N_MOD, d)

    wi = w_in[0]
    o_r = 2 * qk + 2 * gw
    w_cat = jnp.concatenate([wi[:, :o_r], wi[:, o_r + 2 * rank:], wi[:, o_r:o_r + 2 * rank],
                             jnp.zeros((d, LANES - 2 * rank), F32)], axis=1).astype(BF16)
    wgk = jnp.concatenate([jnp.pad(w_gk_f[0], ((0, 0), (0, qk))), jnp.pad(w_gk_b[0], ((0, 0), (qk, 0))),
                           jnp.zeros((LANES - 2 * rank, 2 * qk), F32)], axis=0)
    bgk = jnp.concatenate([b_gk_f[0], b_gk_b[0]])[None, :]
    proj = functools.partial(_inproj_call, nw=norm_mix_w, w=w_cat, wgk=wgk.astype(BF16), bgk=bgk,
                             qk=qk, gw=gw, pw=pw, dk=dk)
    q, k, v, vt, g, xp, gk = proj(x, mod_x, tm=1024)
    _, kc, _, vtc, _, _, gkc = proj(ctx, mod_c, tm=SUPER)

    gla = _gla_call(q, k, v, vt, gk, g, kc, vtc, gkc, gla_norm_w)
    pool = _pool_call(xp, w_pool[0].astype(BF16), pool_scale)

    n = b * t
    wr = jnp.pad(w_router[0].T, ((0, LANES - n_exp), (0, 0)))
    br = jnp.pad(b_router.T, ((0, LANES - n_exp), (0, 0)))
    n_part = n // MOE_PARTS
    tm, tg = 512, 1024
    assert n % MOE_PARTS == 0 and n_part % (SC_WORKERS * SC_WINDOW) == 0 and n_part % tm == 0 and n_part < RANK_LIMIT
    assert (n_part * TOP_K) % (SC_WORKERS * 2 * SC_GATHER_WINDOW) == 0
    n_pad = n_part * TOP_K + n_exp * EXPERT_ROWS
    nblk = n_pad // EXPERT_ROWS
    assert nblk <= 8 * LANES
    parts = range(MOE_PARTS)
    routed = [_route_call(gla.reshape(n, gw), pool.reshape(n, pw), x.reshape(n, d), mod_x, w_out[0].astype(BF16),
                          norm_mlp_w, wr, br, tm=tm, n_exp=n_exp,
                          first_tile=p * n_part // tm, n_tiles=n_part // tm) for p in parts]
    plans = [_plan_call(cnt[:n_exp, 0], code_t, n_exp=n_exp, rows=EXPERT_ROWS, chunk=min(n_part, 4096))
             for (_, _, code_t, _, cnt) in routed]
    dests = [plan[0][:TOP_K] for plan in plans]
    xs = [_sc_scatter_call(routed[p][1],
                           dests[p].reshape(TOP_K, n_part // SC_WINDOW, SC_WINDOW).transpose(1, 0, 2), n_out=n_pad)
          for p in parts]
    ys = [_expert_call(plans[p][1:], xs[p], w_gu[0], b_gu[0], w_down[0], b_down[0]) for p in parts]
    y4 = [_sc_gather_call(ys[p], dests[p].reshape(SC_WORKERS, -1, SC_GATHER_WINDOW)).reshape(TOP_K, n_part, d // 2)
          for p in parts]
    out = None
    for p in parts:
        out = _combine_call(y4[p], routed[p][3], routed[p][0], mod_x, final_norm_w[None, :], out,
                            n=n, tg=tg, first_tile=p * n_part // tg)
    return out.reshape(b, t, d)
```

```python
import functools

import numpy as np
import jax
import jax.numpy as jnp
from jax import lax
from jax.experimental import pallas as pl
from jax.experimental.pallas import tpu as pltpu
from jax.experimental.pallas import tpu_sc as plsc

F32 = jnp.float32
BF16 = jnp.bfloat16
I32 = jnp.int32
U32 = jnp.uint32

GRID_W = 64
GLA_HEADS = 4
GLA_CHUNK = 64
GATE_NORMALIZER = 16.0
POOL_WINDOWS = (2, 4, 8, 16)
POOL_PAD_GRID_ROWS = 8
TOP_K = 4
RANK_LIMIT = 1 << 20
SWIGLU_LIMIT = 7.0
SWIGLU_ALPHA = 1.702
N_MOD = 6
EPS = 1e-6

LANES = 128
SUPER = 256
HEAD_PAIR_DK = 128
EXPERT_ROWS = 512
EXPERT_ROW_STEP = 128
MOE_PARTS = 2
EXPERT_VMEM_BYTES = 56 * 1024 * 1024
SC_CORES = 2
SC_SUBCORES = 16
SC_WORKERS = SC_CORES * SC_SUBCORES
SC_WINDOW = 32
SC_GATHER_WINDOW = 64


def _dot(a, b):
    return jnp.dot(a, b, preferred_element_type=F32)


def _dot_nt(a, b):
    return lax.dot_general(a, b, (((1,), (1,)), ((), ())), preferred_element_type=F32)


def _split_bf16(x):
    hi = x.astype(BF16)
    lo = (x - hi.astype(F32)).astype(BF16)
    return hi, lo


def _pack_bf16_pairs(x):
    c = x.shape[1] // 2
    lo = lax.bitcast_convert_type(x[:, :c].astype(BF16).astype(F32), U32)
    hi = lax.bitcast_convert_type(x[:, c:].astype(BF16).astype(F32), U32)
    return (lo >> 16) | hi


def _unpack_bf16_pairs(p):
    lo = lax.bitcast_convert_type(p << 16, F32)
    hi = lax.bitcast_convert_type(p & jnp.uint32(0xFFFF0000), F32)
    return lo, hi


def _rmsnorm(x, w):
    var = jnp.mean(x * x, axis=-1, keepdims=True)
    return x * lax.rsqrt(var + EPS) * w


def _mod_kernel(c_ref, w_ref, b_ref, o_ref):
    c = c_ref[...]
    s = c * jax.nn.sigmoid(c)
    o_ref[...] = jnp.dot(s, w_ref[...], precision=lax.Precision.HIGHEST,
                         preferred_element_type=F32) + b_ref[...]


def _mod_call(cc, w_ada, b_ada):
    rows, d = cc.shape
    n = w_ada.shape[1]
    tn = 1024
    return pl.pallas_call(
        _mod_kernel,
        grid=(n // tn,),
        in_specs=[pl.BlockSpec((rows, d), lambda j: (0, 0)),
                  pl.BlockSpec((d, tn), lambda j: (0, j)),
                  pl.BlockSpec((1, tn), lambda j: (0, j))],
        out_specs=pl.BlockSpec((rows, tn), lambda j: (0, j)),
        out_shape=jax.ShapeDtypeStruct((rows, n), F32),
        name="mod",
    )(cc, w_ada, b_ada)


def _inproj_kernel(x_ref, mod_ref, nw_ref, w_ref, wgk_ref, bgk_ref,
                   q_ref, k_ref, v_ref, vt_ref, g_ref, p_ref, gk_ref, *, qk, gw, pw, dk):
    x = x_ref[0]
    m = mod_ref[0]
    hm = (_rmsnorm(x, nw_ref[...]) * (1.0 + m[1:2]) + m[0:1]).astype(BF16)
    p = _dot(hm, w_ref[...])
    vt = p[:, 2 * qk:2 * qk + gw].T
    for s in range(vt_ref.shape[1]):
        vt_ref[0, s] = vt[:, s * SUPER:(s + 1) * SUPER].astype(BF16)
    o = 0
    q_ref[0] = (p[:, o:o + qk] * (dk ** -0.5)).astype(BF16); o += qk
    k_ref[0] = p[:, o:o + qk].astype(BF16); o += qk
    v_ref[0] = p[:, o:o + gw].astype(BF16); o += gw
    g_ref[0] = p[:, o:o + gw].astype(BF16); o += gw
    p_ref[0] = p[:, o:o + pw].astype(BF16); o += pw
    r = p[:, o:o + LANES]
    z = _dot(r.astype(BF16), wgk_ref[...]) + bgk_ref[...]
    gk_ref[0] = (jnp.minimum(z, 0.0) - jnp.log1p(jnp.exp(-jnp.abs(z)))) * (1.0 / GATE_NORMALIZER)


def _inproj_call(x, mod, nw, w, wgk, bgk, *, qk, gw, pw, dk, tm):
    b, t, d = x.shape
    n_in = w.shape[1]
    bs = lambda width: pl.BlockSpec((1, tm, width), lambda i, j: (i, j, 0))
    const = lambda shape: pl.BlockSpec(shape, lambda i, j: (0,) * len(shape))
    per_batch = mod.shape[0] > 1
    sds = jax.ShapeDtypeStruct
    return pl.pallas_call(
        functools.partial(_inproj_kernel, qk=qk, gw=gw, pw=pw, dk=dk),
        grid=(b, t // tm),
        in_specs=[bs(d),
                  pl.BlockSpec((1, N_MOD, d), (lambda i, j: (i, 0, 0)) if per_batch else (lambda i, j: (0, 0, 0))),
                  const((1, d)), const((d, n_in)), const((LANES, 2 * qk)), const((1, 2 * qk))],
        out_specs=[bs(qk), bs(qk), bs(gw),
                   pl.BlockSpec((1, tm // SUPER, gw, SUPER), lambda i, j: (i, j, 0, 0)),
                   bs(gw), bs(pw), bs(2 * qk)],
        out_shape=[sds((b, t, qk), BF16), sds((b, t, qk), BF16), sds((b, t, gw), BF16),
                   sds((b, t // SUPER, gw, SUPER), BF16),
                   sds((b, t, gw), BF16), sds((b, t, pw), BF16), sds((b, t, 2 * qk), F32)],
        compiler_params=pltpu.CompilerParams(dimension_semantics=("arbitrary", "arbitrary")),
        name="inproj",
    )(x, mod, nw, w, wgk, bgk)


def _chunk_row(x, r):
    return jnp.concatenate(
        [jnp.broadcast_to(x[c * GLA_CHUNK + r:c * GLA_CHUNK + r + 1, :], (GLA_CHUNK, x.shape[1]))
         for c in range(x.shape[0] // GLA_CHUNK)], axis=0)


def _decay_sums(gk_f, gk_b, cm):
    n = gk_f.shape[1]
    hi, lo = _split_bf16(jnp.concatenate([gk_f, gk_b], axis=1))
    pre = _dot(cm, hi) + _dot(cm, lo)
    pre_b = pre[:, n:]
    return pre[:, :n], _chunk_row(pre_b, GLA_CHUNK - 1) - pre_b + gk_b


def _gla_super(q, k, v, vt, bcum, amask, bd_mask, st, fwd, want_out):
    nch = SUPER // GLA_CHUNK
    order = tuple(range(nch)) if fwd else tuple(reversed(range(nch)))
    last_row = GLA_CHUNK - 1 if fwd else 0
    mid_row = GLA_CHUNK // 2 - 1 if fwd else GLA_CHUNK // 2
    chunk_row = functools.partial(_chunk_row, bcum)
    chunk_of_row = lax.broadcasted_iota(I32, bcum.shape, 0) // GLA_CHUNK

    def by_chunk(x):
        return jnp.concatenate([jnp.where(chunk_of_row == c, x, 0.0).astype(BF16) for c in range(nch)], axis=1)

    blast = chunk_row(last_row)
    u_all = _dot(vt, by_chunk(k * jnp.exp(blast - bcum)))
    before = [None] * nch
    for c in order:
        before[c] = st
        decay = jnp.exp(bcum[c * GLA_CHUNK + last_row:c * GLA_CHUNK + last_row + 1, :])
        st = st * decay + jnp.where(bd_mask, u_all[:, c * HEAD_PAIR_DK:(c + 1) * HEAD_PAIR_DK], 0.0)
    if not want_out:
        return None, st
    bmid = chunk_row(mid_row)
    qt = q * jnp.exp(bcum - bmid)
    kt = (k * jnp.exp(bmid - bcum)).astype(BF16)
    lane = lax.broadcasted_iota(I32, qt.shape, 1)
    half = HEAD_PAIR_DK // 2
    o_heads = []
    for hh in range(2):
        sel = (lane < half) if hh == 0 else (lane >= half)
        a = _dot_nt(jnp.where(sel, qt, 0.0).astype(BF16), kt)
        a = jnp.where(amask, a, 0.0).astype(BF16)
        o_heads.append(_dot(a, v[:, hh * LANES:(hh + 1) * LANES]))
    qh = (q * jnp.exp(bcum)).astype(BF16)
    o_inter = jnp.concatenate(
        [_dot_nt(qh[c * GLA_CHUNK:(c + 1) * GLA_CHUNK], before[c].astype(BF16)) for c in range(nch)], axis=0)
    return jnp.concatenate(o_heads, axis=1) + o_inter, st


def _gla_kernel(q_ref, k_ref, v_ref, vt_ref, gkf_ref, gkb_ref, g_ref, kc_ref, vtc_ref, gkfc_ref, gkbc_ref,
                nw_ref, cmf_ref, cmb_ref, o_ref, stf_ref, stb_ref, of_ref, ob_ref):
    t = q_ref.shape[1]
    tc = kc_ref.shape[1]
    nsc, nscc = t // SUPER, tc // SUPER
    cmf = cmf_ref[...]
    cmb = cmb_ref[...]
    amask_f = cmf > 0
    amask_b = cmb > 0
    row = lax.broadcasted_iota(I32, (2 * LANES, HEAD_PAIR_DK), 0)
    lane = lax.broadcasted_iota(I32, (2 * LANES, HEAD_PAIR_DK), 1)
    bd_mask = (row < LANES) == (lane < HEAD_PAIR_DK // 2)

    def group_rows(j):
        start = j * SUPER
        return pl.ds(start if isinstance(start, int) else pl.multiple_of(start, SUPER), SUPER)

    def ctx_state(bcum, fwd, j, st):
        return _gla_super(None, kc_ref[0, group_rows(j), :].astype(F32), None, vtc_ref[0, j], bcum,
                          None, bd_mask, st, fwd, False)[1]

    def latent(bcum, amask, fwd, j, st):
        rows = group_rows(j)
        return _gla_super(q_ref[0, rows, :].astype(F32), k_ref[0, rows, :].astype(F32), v_ref[0, rows, :],
                          vt_ref[0, j], bcum, amask, bd_mask, st, fwd, True)

    stf = jnp.zeros(stf_ref.shape, F32)
    stb = jnp.zeros(stb_ref.shape, F32)
    for j in range(nscc):
        jb = nscc - 1 - j
        bcum_f, bcum_b = _decay_sums(gkfc_ref[0, group_rows(j), :], gkbc_ref[0, group_rows(jb), :], cmf)
        stf = ctx_state(bcum_f, True, j, stf)
        stb = ctx_state(bcum_b, False, jb, stb)
    stf_ref[...] = stf
    stb_ref[...] = stb

    nw = nw_ref[...]

    def finish(j, o):
        rows = group_rows(j)
        g = g_ref[0, rows, :].astype(F32)
        half_g = 0.5 * g
        gate = half_g * (1.0 + jnp.tanh(half_g))
        for hh in range(2):
            oh = o[:, hh * LANES:(hh + 1) * LANES]
            on = oh * lax.rsqrt(jnp.mean(oh * oh, axis=-1, keepdims=True) + EPS) * nw
            o_ref[0, rows, hh * LANES:(hh + 1) * LANES] = (on * gate[:, hh * LANES:(hh + 1) * LANES]).astype(BF16)

    for jj in range(nsc):
        jb = nsc - 1 - jj
        bcum_f, bcum_b = _decay_sums(gkf_ref[0, group_rows(jj), :], gkb_ref[0, group_rows(jb), :], cmf)
        of, stf = latent(bcum_f, amask_f, True, jj, stf_ref[...])
        stf_ref[...] = stf
        ob, stb = latent(bcum_b, amask_b, False, jb, stb_ref[...])
        stb_ref[...] = stb
        if jj == jb:
            finish(jj, of + ob)
        elif jj < jb:
            of_ref[group_rows(jj), :] = of
            ob_ref[group_rows(jb), :] = ob
        else:
            finish(jj, of + ob_ref[group_rows(jj), :])
            finish(jb, of_ref[group_rows(jb), :] + ob)


def _gla_masks():
    i = np.arange(SUPER)
    same = (i[:, None] // GLA_CHUNK) == (i[None, :] // GLA_CHUNK)
    fwd = same & (i[None, :] <= i[:, None])
    bwd = same & (i[None, :] >= i[:, None])
    return jnp.asarray(fwd, BF16), jnp.asarray(bwd, BF16)


def _gla_call(q, k, v, vt, gk, g, kc, vtc, gkc, nw):
    b, t, qk = q.shape
    tc = kc.shape[1]
    npair = qk // HEAD_PAIR_DK
    cmf, cmb = _gla_masks()
    lat = lambda width, off: pl.BlockSpec((1, t, width), lambda i, j: (i, 0, j + off))
    ctx = lambda width, off: pl.BlockSpec((1, tc, width), lambda i, j: (i, 0, j + off))
    tr = lambda n_groups: pl.BlockSpec((1, n_groups, 2 * LANES, SUPER), lambda i, j: (i, 0, j, 0))
    const = lambda shape: pl.BlockSpec(shape, lambda i, j: (0,) * len(shape))
    return pl.pallas_call(
        _gla_kernel,
        grid=(b, npair),
        in_specs=[lat(HEAD_PAIR_DK, 0), lat(HEAD_PAIR_DK, 0), lat(2 * LANES, 0), tr(t // SUPER),
                  lat(HEAD_PAIR_DK, 0), lat(HEAD_PAIR_DK, npair), lat(2 * LANES, 0),
                  ctx(HEAD_PAIR_DK, 0), tr(tc // SUPER), ctx(HEAD_PAIR_DK, 0), ctx(HEAD_PAIR_DK, npair),
                  const((1, LANES)), const(cmf.shape), const(cmb.shape)],
        out_specs=lat(2 * LANES, 0),
        out_shape=jax.ShapeDtypeStruct((b, t, v.shape[2]), BF16),
        scratch_shapes=[pltpu.VMEM((2 * LANES, HEAD_PAIR_DK), F32), pltpu.VMEM((2 * LANES, HEAD_PAIR_DK), F32),
                        pltpu.VMEM((t, 2 * LANES), F32), pltpu.VMEM((t, 2 * LANES), F32)],
        compiler_params=pltpu.CompilerParams(dimension_semantics=("arbitrary", "arbitrary")),
        name="gla",
    )(q, k, v, vt, gk, gk, g, kc, vtc, gkc, gkc, nw, cmf, cmb)


def _pool_kernel(x_ref, cm_ref, cnt_ref, wp_ref, ps_ref, o_ref, a_ref, b_ref):
    t = x_ref.shape[1]
    pad = POOL_PAD_GRID_ROWS * GRID_W
    total = t + 2 * pad
    for gi, w in enumerate(POOL_WINDOWS):
        lo = w // 2
        cols = slice(gi * LANES, (gi + 1) * LANES)
        cmat = cm_ref[gi]
        a_ref[0:pad, :] = jnp.zeros((pad, LANES), F32)
        a_ref[pad + t:total, :] = jnp.zeros((pad, LANES), F32)
        for blk in range(t // SUPER):
            rs = slice(blk * SUPER, (blk + 1) * SUPER)
            a_ref[pad + blk * SUPER:pad + (blk + 1) * SUPER, :] = _dot(cmat, x_ref[0, rs, cols])
        src, dst = a_ref, b_ref
        m = 1
        while m < w:
            sh = m * GRID_W
            dst[0:total - sh, :] = src[0:total - sh, :] + src[sh:total, :]
            src, dst = dst, src
            m *= 2
        first = pad - lo * GRID_W
        pooled = src[first:first + t, :] / cnt_ref[gi] - x_ref[0, :, cols].astype(F32)
        yp = _dot(pooled.astype(BF16), wp_ref[gi]) * ps_ref[:, cols]
        o_ref[0, :, cols] = yp.astype(BF16)


def _pool_col_mats():
    i = np.arange(SUPER)
    same_row = (i[:, None] // GRID_W) == (i[None, :] // GRID_W)
    d = i[None, :] - i[:, None]
    mats = []
    for w in POOL_WINDOWS:
        lo = w // 2
        hi = w - 1 - lo
        mats.append(same_row & (d >= -lo) & (d <= hi))
    return jnp.asarray(np.stack(mats), BF16)


def _pool_counts(t):
    rows = t // GRID_W
    r = np.arange(t) // GRID_W
    c = np.arange(t) % GRID_W
    out = []
    for w in POOL_WINDOWS:
        lo = w // 2
        hi = w - 1 - lo
        cnt_r = np.minimum(r + hi + 1, rows) - np.maximum(r - lo, 0)
        cnt_c = np.minimum(c + hi + 1, GRID_W) - np.maximum(c - lo, 0)
        out.append(np.broadcast_to((cnt_r * cnt_c).astype(np.float32)[:, None], (t, LANES)))
    return jnp.asarray(np.stack(out))


def _pool_call(xp, w_pool, pool_scale):
    b, t, pw = xp.shape
    ng = len(POOL_WINDOWS)
    assert max(POOL_WINDOWS) // 2 <= POOL_PAD_GRID_ROWS and t % GRID_W == 0
    cm = _pool_col_mats()
    cnt = _pool_counts(t)
    staged = t + 2 * POOL_PAD_GRID_ROWS * GRID_W
    const = lambda shape: pl.BlockSpec(shape, lambda i: (0,) * len(shape))
    return pl.pallas_call(
        _pool_kernel,
        grid=(b,),
        in_specs=[pl.BlockSpec((1, t, pw), lambda i: (i, 0, 0)),
                  const(cm.shape), const(cnt.shape), const((ng, LANES, LANES)), const((1, pw))],
        out_specs=pl.BlockSpec((1, t, pw), lambda i: (i, 0, 0)),
        out_shape=jax.ShapeDtypeStruct((b, t, pw), BF16),
        scratch_shapes=[pltpu.VMEM((staged, LANES), F32), pltpu.VMEM((staged, LANES), F32)],
        compiler_params=pltpu.CompilerParams(dimension_semantics=("arbitrary",)),
        name="pool",
    )(xp, cm, cnt, w_pool, pool_scale)


def _route_kernel(gla_ref, pool_ref, x_ref, mod_ref, wo_ref, nw_ref, wr_ref, br_ref, ut_ref,
                  h_ref, xt_ref, code_ref, wt_ref, cnt_ref, run_ref, wr2_ref, *, gw, n_exp):
    i = pl.program_id(0)

    @pl.when(i == 0)
    def _():
        run_ref[...] = jnp.zeros_like(run_ref)
        wh, wl = _split_bf16(wr_ref[...])
        wr2_ref[:LANES, :] = wh
        wr2_ref[LANES:, :] = wl

    m = mod_ref[0]
    acc = _dot(gla_ref[...], wo_ref[0:gw, :]) + _dot(pool_ref[...], wo_ref[gw:, :])
    h = x_ref[...] + m[2:3] * acc
    h_ref[...] = h
    xt = _rmsnorm(h, nw_ref[...]) * (1.0 + m[4:5]) + m[3:4]
    xt_ref[...] = _pack_bf16_pairs(xt)
    xh, xl = _split_bf16(xt)
    wr2 = wr2_ref[...]
    t1 = _dot_nt(wr2, xh)
    logits = t1[:LANES] + t1[LANES:] + _dot_nt(wr2[:LANES], xl) + br_ref[...]
    row = lax.broadcasted_iota(I32, logits.shape, 0)
    row_f = row.astype(F32)
    neg = jnp.float32(-jnp.inf)
    logits = jnp.where(row < n_exp, logits, neg)
    vals, hots, idxs = [], [], []
    for j in range(TOP_K):
        mx = jnp.max(logits, axis=0, keepdims=True)
        idx = jnp.min(jnp.where(logits == mx, row_f, float(LANES)), axis=0, keepdims=True)
        hot = row_f == idx
        vals.append(mx)
        hots.append(hot)
        idxs.append(idx)
        logits = jnp.where(hot, neg, logits)
    ex = [jnp.exp(v - vals[0]) for v in vals]
    den = ex[0] + ex[1] + ex[2] + ex[3]
    osum = jnp.where(hots[0] | hots[1] | hots[2] | hots[3], 1.0, 0.0)
    before = _dot(osum.astype(BF16), ut_ref[...]) + run_ref[:, 0:1]
    run = run_ref[:, 0:1] + jnp.sum(osum, axis=1, keepdims=True)
    run_ref[...] = jnp.broadcast_to(run, run_ref.shape)
    code_row = lax.broadcasted_iota(I32, code_ref.shape, 0)
    code = jnp.zeros(code_ref.shape, I32)
    w_t = jnp.zeros(logits.shape, F32)
    for j in range(TOP_K):
        rank = jnp.sum(jnp.where(hots[j], before, 0.0), axis=0, keepdims=True)
        code = jnp.where(code_row == j, idxs[j].astype(I32) * RANK_LIMIT + rank.astype(I32), code)
        w_t = jnp.where(row == j, ex[j] / den, w_t)
    code_ref[...] = code
    wt_ref[...] = w_t.T
    cnt_ref[...] = jnp.broadcast_to(run, cnt_ref.shape).astype(I32)


def _route_call(gla, pool, x, mod, w_out, nw, wr, br, *, tm, n_exp, first_tile, n_tiles):
    n_all, d = x.shape
    n = n_tiles * tm
    gw = gla.shape[1]
    t_per_b = n_all // mod.shape[0]
    ut = jnp.asarray(np.triu(np.ones((tm, tm), np.float32), 1), BF16)
    row_in = lambda width: pl.BlockSpec((tm, width), lambda i: (i + first_tile, 0))
    row = lambda width: pl.BlockSpec((tm, width), lambda i: (i, 0))
    const = lambda shape: pl.BlockSpec(shape, lambda i: (0,) * len(shape))
    sds = jax.ShapeDtypeStruct
    return pl.pallas_call(
        functools.partial(_route_kernel, gw=gw, n_exp=n_exp),
        grid=(n_tiles,),
        in_specs=[row_in(gw), row_in(pool.shape[1]), row_in(d),
                  pl.BlockSpec((1, N_MOD, d), lambda i: ((i + first_tile) * tm // t_per_b, 0, 0)),
                  const(w_out.shape), const((1, d)), const(wr.shape), const((LANES, 1)), const((tm, tm))],
        out_specs=[row(d), row(d // 2), pl.BlockSpec((8, tm), lambda i: (0, i)), row(LANES), const((LANES, LANES))],
        out_shape=[sds((n, d), F32), sds((n, d // 2), U32), sds((8, n), I32),
                   sds((n, LANES), F32), sds((LANES, LANES), I32)],
        scratch_shapes=[pltpu.VMEM((LANES, LANES), F32), pltpu.VMEM((2 * LANES, d), BF16)],
        compiler_params=pltpu.CompilerParams(dimension_semantics=("arbitrary",)),
        name="route",
    )(gla, pool, x, mod, w_out, nw, wr, br, ut)


def _plan_kernel(cnt_ref, code_ref, dest_ref, be_ref, nv_ref, first_ref, next_ref, slot_ref, nu_ref, start_ref,
                 *, n_exp, rows):
    @pl.when(pl.program_id(0) == 0)
    def _():
        blk = (lax.broadcasted_iota(I32, be_ref.shape, 0) * LANES + lax.broadcasted_iota(I32, be_ref.shape, 1))
        blk_row0 = blk * rows
        nxt_e = [None] * n_exp
        nxt = jnp.int32(-1)
        for e in reversed(range(n_exp)):
            nxt_e[e] = nxt
            nxt = jnp.where(cnt_ref[e] > 0, e, nxt)
        zeros = jnp.zeros(be_ref.shape, I32)
        be, end_valid, first, nxt_blk, slot = zeros, zeros, zeros, zeros - 1, zeros
        acc = jnp.int32(0)
        last_e = jnp.int32(0)
        ordinal = jnp.int32(0)
        for e in range(n_exp):
            c = cnt_ref[e]
            start_ref[e] = acc
            in_e = (blk_row0 >= acc) & (c > 0)
            end_valid = jnp.where(in_e, acc + c, end_valid)
            be = jnp.where(in_e, e, be)
            first = jnp.where(in_e, (blk_row0 == acc).astype(I32), first)
            nxt_blk = jnp.where(in_e, nxt_e[e], nxt_blk)
            slot = jnp.where(in_e, ordinal % 2, slot)
            acc = acc + (c + rows - 1) // rows * rows
            last_e = jnp.where(c > 0, e, last_e)
            ordinal = ordinal + (c > 0).astype(I32)
        n_used = acc // rows
        nu_ref[0] = n_used
        be_ref[...] = jnp.where(blk < n_used, be, last_e)
        nv_ref[...] = jnp.clip(end_valid - blk_row0, 0, rows)
        first_ref[...] = first
        next_ref[...] = nxt_blk
        slot_ref[...] = slot

    code = code_ref[...]
    e_vec = code // RANK_LIMIT
    dest = code % RANK_LIMIT
    for e in range(n_exp):
        dest = dest + jnp.where(e_vec == e, start_ref[e], 0)
    dest_ref[...] = dest


def _plan_call(counts, code_t, *, n_exp, rows, chunk):
    n = code_t.shape[1]
    sds = jax.ShapeDtypeStruct
    smem = pltpu.SMEM
    return pl.pallas_call(
        functools.partial(_plan_kernel, n_exp=n_exp, rows=rows),
        grid=(n // chunk,),
        in_specs=[pl.BlockSpec(memory_space=smem), pl.BlockSpec((8, chunk), lambda i: (0, i))],
        out_specs=[pl.BlockSpec((8, chunk), lambda i: (0, i))]
        + [pl.BlockSpec((8, LANES), lambda i: (0, 0))] * 5 + [pl.BlockSpec(memory_space=smem)],
        out_shape=[sds((8, n), I32)] + [sds((8, LANES), I32)] * 5 + [sds((1,), I32)],
        scratch_shapes=[pltpu.SMEM((n_exp,), I32)],
        compiler_params=pltpu.CompilerParams(dimension_semantics=("arbitrary",)),
        name="plan",
    )(counts, code_t)


def _sc_worker_id():
    return lax.axis_index("s") * SC_CORES + lax.axis_index("c")


def _sc_scatter_call(x, idx3, *, n_out):
    n, d = x.shape
    n_win_total, k, w = idx3.shape
    n_win = n_win_total // SC_WORKERS
    mesh = plsc.VectorSubcoreMesh(core_axis_name="c", subcore_axis_name="s")

    @functools.partial(
        pl.kernel, mesh=mesh,
        out_type=jax.ShapeDtypeStruct((n_out, d), x.dtype),
        scratch_types=[pltpu.VMEM((k, w), I32), pltpu.VMEM((w, d), x.dtype), pltpu.SemaphoreType.DMA],
        name="sc_dispatch",
    )
    def kern(x_hbm, idx_hbm, out_hbm, idx_v, rows_v, sem):
        wid = _sc_worker_id()

        @pl.loop(0, n_win)
        def _(i):
            win = wid * n_win + i
            pltpu.sync_copy(idx_hbm.at[win], idx_v)
            pltpu.sync_copy(x_hbm.at[pl.ds(win * w, w)], rows_v)
            for j in range(k):
                pltpu.async_copy(rows_v, out_hbm.at[idx_v.at[j]], sem).wait()

    return kern(x, idx3)


def _sc_gather_call(table, idx3):
    n_workers, n_win, w = idx3.shape
    d = table.shape[1]
    assert n_workers == SC_WORKERS and n_win % 2 == 0
    mesh = plsc.VectorSubcoreMesh(core_axis_name="c", subcore_axis_name="s")

    @functools.partial(
        pl.kernel, mesh=mesh,
        out_type=jax.ShapeDtypeStruct((n_workers * n_win * w, d), table.dtype),
        scratch_types=[pltpu.VMEM((n_win, w), I32), pltpu.VMEM((2, w, d), table.dtype),
                       pltpu.SemaphoreType.DMA((2,)), pltpu.SemaphoreType.DMA((2,))],
        name="sc_gather",
    )
    def kern(table_hbm, idx_hbm, out_hbm, idx_v, rows_v, gsem, osem):
        wid = _sc_worker_id()
        base = wid * n_win
        pltpu.sync_copy(idx_hbm.at[wid], idx_v)

        def gather(wi, b):
            return pltpu.make_async_copy(table_hbm.at[idx_v.at[wi]], rows_v.at[b], gsem.at[b])

        def put(wi, b):
            return pltpu.make_async_copy(rows_v.at[b], out_hbm.at[pl.ds((base + wi) * w, w)], osem.at[b])

        gather(0, 0).start()

        @pl.loop(0, n_win, step=2)
        def _(i):
            for b in range(2):
                wi = i + b

                @pl.when(wi + 1 < n_win)
                def _():
                    @pl.when(wi >= 1)
                    def _():
                        put(wi - 1, 1 - b).wait()
                    gather(wi + 1, 1 - b).start()

                gather(wi, b).wait()
                put(wi, b).start()

        put(n_win - 2, 0).wait()
        put(n_win - 1, 1).wait()

    return kern(table, idx3)


def _expert_kernel(be_ref, nu_ref, nv_ref, first_ref, next_ref, slot_ref,
                   x_ref, wgu_hbm, bgu_ref, wd_hbm, bd_ref, y_ref,
                   wgu_f32_ref, wd_f32_ref, wgu_bf_ref, wd_bf_ref, sem, *, d_ff):
    i = pl.program_id(0)
    slot = slot_ref[i]

    def fetch(e, s):
        return (pltpu.make_async_copy(wgu_hbm.at[e], wgu_f32_ref.at[s], sem.at[0, s]),
                pltpu.make_async_copy(wd_hbm.at[e], wd_f32_ref.at[s], sem.at[1, s]))

    @pl.when(i == 0)
    def _():
        for cp in fetch(be_ref[0], slot):
            cp.start()

    @pl.when(first_ref[i] == 1)
    def _():
        for cp in fetch(be_ref[i], slot):
            cp.wait()

        @pl.when(next_ref[i] >= 0)
        def _():
            for cp in fetch(next_ref[i], 1 - slot):
                cp.start()


    n_valid = nv_ref[i]
    n_rows = x_ref.shape[0]

    def expert_rows(rows):
        row = lax.broadcasted_iota(I32, (rows, x_ref.shape[1]), 0)
        lo, hi = _unpack_bf16_pairs(jnp.where(row < n_valid, x_ref[0:rows, :], jnp.uint32(0)))
        xb = jnp.concatenate([lo, hi], axis=1).astype(BF16)
        col = lax.broadcasted_iota(I32, bgu_ref.shape[1:], 1)
        gu = _dot(xb, wgu_f32_ref[slot].astype(BF16)) + (bgu_ref[0] + jnp.where(col >= d_ff, 1.0, 0.0))
        gate = jnp.minimum(gu[:, :d_ff], SWIGLU_LIMIT)
        up1 = jnp.clip(gu[:, d_ff:], 1.0 - SWIGLU_LIMIT, 1.0 + SWIGLU_LIMIT)
        act = up1 * (0.5 * gate) * (1.0 + jnp.tanh((0.5 * SWIGLU_ALPHA) * gate))
        y_ref[0:rows, :] = _pack_bf16_pairs(_dot(act.astype(BF16), wd_f32_ref[slot].astype(BF16)) + bd_ref[0])

    in_use = i < nu_ref[0]

    for rows in range(EXPERT_ROW_STEP, n_rows + 1, EXPERT_ROW_STEP):
        @pl.when(in_use & (n_valid > rows - EXPERT_ROW_STEP) & (n_valid <= rows))
        def _(rows=rows):
            expert_rows(rows)
            if rows < n_rows:
                y_ref[rows:, :] = jnp.zeros((n_rows - rows, y_ref.shape[1]), y_ref.dtype)


def _expert_call(plan, xs, w_gu, b_gu, w_down, b_down):
    block_e, n_valid, first, nxt, slot, n_used = plan
    n_pad = xs.shape[0]
    n_exp, d, two_ff = w_gu.shape
    d_ff = two_ff // 2
    nblk = n_pad // EXPERT_ROWS
    rows = lambda i, be, nu, *_: (jnp.minimum(i, nu[0] - 1), 0)
    per_e = lambda i, be, *_: (be[i], 0, 0)
    grid_spec = pltpu.PrefetchScalarGridSpec(
        num_scalar_prefetch=6,
        grid=(nblk,),
        in_specs=[pl.BlockSpec((EXPERT_ROWS, d // 2), rows),
                  pl.BlockSpec(memory_space=pl.ANY), pl.BlockSpec((1, 1, two_ff), per_e),
                  pl.BlockSpec(memory_space=pl.ANY), pl.BlockSpec((1, 1, d), per_e)],
        out_specs=pl.BlockSpec((EXPERT_ROWS, d // 2), rows),
        scratch_shapes=[pltpu.VMEM((2, d, two_ff), F32), pltpu.VMEM((2, d_ff, d), F32),
                        pltpu.VMEM((d, two_ff), BF16), pltpu.VMEM((d_ff, d), BF16),
                        pltpu.SemaphoreType.DMA((2, 2))],
    )
    flat = lambda a: a.reshape(-1)
    return pl.pallas_call(
        functools.partial(_expert_kernel, d_ff=d_ff),
        grid_spec=grid_spec,
        out_shape=jax.ShapeDtypeStruct((n_pad, d // 2), U32),
        compiler_params=pltpu.CompilerParams(dimension_semantics=("arbitrary",),
                                             vmem_limit_bytes=EXPERT_VMEM_BYTES),
        name="experts",
    )(flat(block_e), n_used, flat(n_valid), flat(first), flat(nxt), flat(slot),
      xs, w_gu, b_gu.reshape(n_exp, 1, two_ff), w_down, b_down.reshape(n_exp, 1, d))


def _combine_kernel(y4_ref, wt_ref, h_ref, mod_ref, fw_ref, o_ref):
    wt = wt_ref[...]
    acc_lo, acc_hi = None, None
    for j in range(TOP_K):
        lo, hi = _unpack_bf16_pairs(y4_ref[j])
        w = wt[:, j:j + 1]
        acc_lo = w * lo if j == 0 else acc_lo + w * lo
        acc_hi = w * hi if j == 0 else acc_hi + w * hi
    acc = jnp.concatenate([acc_lo, acc_hi], axis=1)
    m = mod_ref[0]
    o_ref[...] = _rmsnorm(h_ref[...] + m[5:6] * acc, fw_ref[...])


def _combine_call(y4, wts, h, mod, fw, prev_out, *, n, tg, first_tile):
    d = h.shape[1]
    t_per_b = n // mod.shape[0]
    part = lambda width: pl.BlockSpec((tg, width), lambda i: (i, 0))
    row = lambda width: pl.BlockSpec((tg, width), lambda i: (i + first_tile, 0))
    in_specs = [pl.BlockSpec((TOP_K, tg, d // 2), lambda i: (0, i, 0)),
                part(LANES), part(d),
                pl.BlockSpec((1, N_MOD, d), lambda i: ((i + first_tile) * tg // t_per_b, 0, 0)),
                pl.BlockSpec((1, d), lambda i: (0, 0))]
    args = [y4, wts, h, mod, fw]
    kern = _combine_kernel
    aliases = {}
    if prev_out is not None:
        in_specs.append(pl.BlockSpec(memory_space=pl.ANY))
        args.append(prev_out)
        kern = lambda y4_ref, wt_ref, h_ref, mod_ref, fw_ref, prev_ref, o_ref: _combine_kernel(
            y4_ref, wt_ref, h_ref, mod_ref, fw_ref, o_ref)
        aliases = {len(args) - 1: 0}
    return pl.pallas_call(
        kern,
        grid=(y4.shape[1] // tg,),
        in_specs=in_specs,
        out_specs=row(d),
        out_shape=jax.ShapeDtypeStruct((n, d), F32),
        input_output_aliases=aliases,
        compiler_params=pltpu.CompilerParams(dimension_semantics=("arbitrary",)),
        name="combine",
    )(*args)


def kernel(x, c, ctx, c_ctx, w_ada, b_ada, norm_mix_w, norm_mlp_w, w_in, w_gk_f, b_gk_f, w_gk_b, b_gk_b,
           gla_norm_w, w_pool, pool_scale, w_out, w_router, b_router, w_gu, b_gu, w_down, b_down,
           final_norm_w):
    b, t, d = x.shape
    assert w_ada.shape[0] == 1, "single-layer trunk"
    n_exp = w_router.shape[2]
    rank = w_gk_f.shape[1]
    qk = w_gk_f.shape[2]
    dk = qk // GLA_HEADS
    gw = GLA_HEADS * gla_norm_w.shape[1]
    pw = w_pool.shape[1] * w_pool.shape[2]
    assert w_in.shape[2] == 2 * qk + 2 * gw + 2 * rank + pw and 2 * rank <= LANES
    assert t % SUPER == 0 and ctx.shape[1] % SUPER == 0 and n_exp <= LANES

    rows = -(-(b + 1) // 8) * 8
    cc = jnp.concatenate([c, c_ctx[None, :], jnp.zeros((rows - b - 1, d), F32)], axis=0)
    mod = _mod_call(cc, w_ada[0], b_ada)
    mod_x = mod[:b].reshape(b, N_MOD, d)
    mod_c = mod[b:b + 1].reshape(1, N_MOD, d)

    wi = w_in[0]
    o_r = 2 * qk + 2 * gw
    w_cat = jnp.concatenate([wi[:, :o_r], wi[:, o_r + 2 * rank:], wi[:, o_r:o_r + 2 * rank],
                             jnp.zeros((d, LANES - 2 * rank), F32)], axis=1).astype(BF16)
    wgk = jnp.concatenate([jnp.pad(w_gk_f[0], ((0, 0), (0, qk))), jnp.pad(w_gk_b[0], ((0, 0), (qk, 0))),
                           jnp.zeros((LANES - 2 * rank, 2 * qk), F32)], axis=0)
    bgk = jnp.concatenate([b_gk_f[0], b_gk_b[0]])[None, :]
    proj = functools.partial(_inproj_call, nw=norm_mix_w, w=w_cat, wgk=wgk.astype(BF16), bgk=bgk,
                             qk=qk, gw=gw, pw=pw, dk=dk)
    q, k, v, vt, g, xp, gk = proj(x, mod_x, tm=1024)
    _, kc, _, vtc, _, _, gkc = proj(ctx, mod_c, tm=SUPER)

    gla = _gla_call(q, k, v, vt, gk, g, kc, vtc, gkc, gla_norm_w)
    pool = _pool_call(xp, w_pool[0].astype(BF16), pool_scale)

    n = b * t
    wr = jnp.pad(w_router[0].T, ((0, LANES - n_exp), (0, 0)))
    br = jnp.pad(b_router.T, ((0, LANES - n_exp), (0, 0)))
    n_part = n // MOE_PARTS
    tm, tg = 512, 1024
    assert n % MOE_PARTS == 0 and n_part % (SC_WORKERS * SC_WINDOW) == 0 and n_part % tm == 0 and n_part < RANK_LIMIT
    assert (n_part * TOP_K) % (SC_WORKERS * 2 * SC_GATHER_WINDOW) == 0
    n_pad = n_part * TOP_K + n_exp * EXPERT_ROWS
    nblk = n_pad // EXPERT_ROWS
    assert nblk <= 8 * LANES
    parts = range(MOE_PARTS)
    routed = [_route_call(gla.reshape(n, gw), pool.reshape(n, pw), x.reshape(n, d), mod_x, w_out[0].astype(BF16),
                          norm_mlp_w, wr, br, tm=tm, n_exp=n_exp,
                          first_tile=p * n_part // tm, n_tiles=n_part // tm) for p in parts]
    plans = [_plan_call(cnt[:n_exp, 0], code_t, n_exp=n_exp, rows=EXPERT_ROWS, chunk=min(n_part, 4096))
             for (_, _, code_t, _, cnt) in routed]
    dests = [plan[0][:TOP_K] for plan in plans]
    xs = [_sc_scatter_call(routed[p][1],
                           dests[p].reshape(TOP_K, n_part // SC_WINDOW, SC_WINDOW).transpose(1, 0, 2), n_out=n_pad)
          for p in parts]
    ys = [_expert_call(plans[p][1:], xs[p], w_gu[0], b_gu[0], w_down[0], b_down[0]) for p in parts]
    y4 = [_sc_gather_call(ys[p], dests[p].reshape(SC_WORKERS, -1, SC_GATHER_WINDOW)).reshape(TOP_K, n_part, d // 2)
          for p in parts]
    out = None
    for p in parts:
        out = _combine_call(y4[p], routed[p][3], routed[p][0], mod_x, final_norm_w[None, :], out,
                            n=n, tg=tg, first_tile=p * n_part // tg)
    return out.reshape(b, t, d)
```

```python
import functools

import numpy as np
import jax
import jax.numpy as jnp
from jax import lax
from jax.experimental import pallas as pl
from jax.experimental.pallas import tpu as pltpu
from jax.experimental.pallas import tpu_sc as plsc

F32 = jnp.float32
BF16 = jnp.bfloat16
I32 = jnp.int32
U32 = jnp.uint32

GRID_W = 64
GLA_HEADS = 4
GLA_CHUNK = 64
GATE_NORMALIZER = 16.0
POOL_WINDOWS = (2, 4, 8, 16)
POOL_PAD_GRID_ROWS = 8
TOP_K = 4
RANK_LIMIT = 1 << 20
SWIGLU_LIMIT = 7.0
SWIGLU_ALPHA = 1.702
N_MOD = 6
EPS = 1e-6

LANES = 128
SUPER = 256
HEAD_PAIR_DK = 128
EXPERT_ROWS = 512
EXPERT_ROW_STEP = 128
MOE_PARTS = 2
EXPERT_VMEM_BYTES = 56 * 1024 * 1024
SC_CORES = 2
SC_SUBCORES = 16
SC_WORKERS = SC_CORES * SC_SUBCORES
SC_WINDOW = 32
SC_GATHER_WINDOW = 64


def _dot(a, b):
    return jnp.dot(a, b, preferred_element_type=F32)


def _dot_nt(a, b):
    return lax.dot_general(a, b, (((1,), (1,)), ((), ())), preferred_element_type=F32)


def _split_bf16(x):
    hi = x.astype(BF16)
    lo = (x - hi.astype(F32)).astype(BF16)
    return hi, lo


def _pack_bf16_pairs(x):
    c = x.shape[1] // 2
    lo = lax.bitcast_convert_type(x[:, :c].astype(BF16).astype(F32), U32)
    hi = lax.bitcast_convert_type(x[:, c:].astype(BF16).astype(F32), U32)
    return (lo >> 16) | hi


def _unpack_bf16_pairs(p):
    lo = lax.bitcast_convert_type(p << 16, F32)
    hi = lax.bitcast_convert_type(p & jnp.uint32(0xFFFF0000), F32)
    return lo, hi


def _rmsnorm(x, w):
    var = jnp.mean(x * x, axis=-1, keepdims=True)
    return x * lax.rsqrt(var + EPS) * w


def _mod_kernel(c_ref, w_ref, b_ref, o_ref):
    c = c_ref[...]
    s = c * jax.nn.sigmoid(c)
    o_ref[...] = jnp.dot(s, w_ref[...], precision=lax.Precision.HIGHEST,
                         preferred_element_type=F32) + b_ref[...]


def _mod_call(cc, w_ada, b_ada):
    rows, d = cc.shape
    n = w_ada.shape[1]
    tn = 1024
    return pl.pallas_call(
        _mod_kernel,
        grid=(n // tn,),
        in_specs=[pl.BlockSpec((rows, d), lambda j: (0, 0)),
                  pl.BlockSpec((d, tn), lambda j: (0, j)),
                  pl.BlockSpec((1, tn), lambda j: (0, j))],
        out_specs=pl.BlockSpec((rows, tn), lambda j: (0, j)),
        out_shape=jax.ShapeDtypeStruct((rows, n), F32),
        name="mod",
    )(cc, w_ada, b_ada)


def _inproj_kernel(x_ref, mod_ref, nw_ref, w_ref, wgk_ref, bgk_ref,
                   q_ref, k_ref, v_ref, vt_ref, g_ref, p_ref, gk_ref, *, qk, gw, pw, dk):
    x = x_ref[0]
    m = mod_ref[0]
    hm = (_rmsnorm(x, nw_ref[...]) * (1.0 + m[1:2]) + m[0:1]).astype(BF16)
    p = _dot(hm, w_ref[...])
    vt = p[:, 2 * qk:2 * qk + gw].T
    for s in range(vt_ref.shape[1]):
        vt_ref[0, s] = vt[:, s * SUPER:(s + 1) * SUPER].astype(BF16)
    o = 0
    q_ref[0] = (p[:, o:o + qk] * (dk ** -0.5)).astype(BF16); o += qk
    k_ref[0] = p[:, o:o + qk].astype(BF16); o += qk
    v_ref[0] = p[:, o:o + gw].astype(BF16); o += gw
    g_ref[0] = p[:, o:o + gw].astype(BF16); o += gw
    p_ref[0] = p[:, o:o + pw].astype(BF16); o += pw
    r = p[:, o:o + LANES]
    z = _dot(r.astype(BF16), wgk_ref[...]) + bgk_ref[...]
    gk_ref[0] = (jnp.minimum(z, 0.0) - jnp.log1p(jnp.exp(-jnp.abs(z)))) * (1.0 / GATE_NORMALIZER)


def _inproj_call(x, mod, nw, w, wgk, bgk, *, qk, gw, pw, dk, tm):
    b, t, d = x.shape
    n_in = w.shape[1]
    bs = lambda width: pl.BlockSpec((1, tm, width), lambda i, j: (i, j, 0))
    const = lambda shape: pl.BlockSpec(shape, lambda i, j: (0,) * len(shape))
    per_batch = mod.shape[0] > 1
    sds = jax.ShapeDtypeStruct
    return pl.pallas_call(
        functools.partial(_inproj_kernel, qk=qk, gw=gw, pw=pw, dk=dk),
        grid=(b, t // tm),
        in_specs=[bs(d),
                  pl.BlockSpec((1, N_MOD, d), (lambda i, j: (i, 0, 0)) if per_batch else (lambda i, j: (0, 0, 0))),
                  const((1, d)), const((d, n_in)), const((LANES, 2 * qk)), const((1, 2 * qk))],
        out_specs=[bs(qk), bs(qk), bs(gw),
                   pl.BlockSpec((1, tm // SUPER, gw, SUPER), lambda i, j: (i, j, 0, 0)),
                   bs(gw), bs(pw), bs(2 * qk)],
        out_shape=[sds((b, t, qk), BF16), sds((b, t, qk), BF16), sds((b, t, gw), BF16),
                   sds((b, t // SUPER, gw, SUPER), BF16),
                   sds((b, t, gw), BF16), sds((b, t, pw), BF16), sds((b, t, 2 * qk), F32)],
        compiler_params=pltpu.CompilerParams(dimension_semantics=("arbitrary", "arbitrary")),
        name="inproj",
    )(x, mod, nw, w, wgk, bgk)


def _chunk_row(x, r):
    return jnp.concatenate(
        [jnp.broadcast_to(x[c * GLA_CHUNK + r:c * GLA_CHUNK + r + 1, :], (GLA_CHUNK, x.shape[1]))
         for c in range(x.shape[0] // GLA_CHUNK)], axis=0)


def _decay_sums(gk_f, gk_b, cm):
    n = gk_f.shape[1]
    hi, lo = _split_bf16(jnp.concatenate([gk_f, gk_b], axis=1))
    pre = _dot(cm, hi) + _dot(cm, lo)
    pre_b = pre[:, n:]
    return pre[:, :n], _chunk_row(pre_b, GLA_CHUNK - 1) - pre_b + gk_b


def _gla_super(q, k, v, vt, bcum, amask, bd_mask, st, fwd, want_out):
    nch = SUPER // GLA_CHUNK
    order = tuple(range(nch)) if fwd else tuple(reversed(range(nch)))
    last_row = GLA_CHUNK - 1 if fwd else 0
    mid_row = GLA_CHUNK // 2 - 1 if fwd else GLA_CHUNK // 2
    chunk_row = functools.partial(_chunk_row, bcum)
    chunk_of_row = lax.broadcasted_iota(I32, bcum.shape, 0) // GLA_CHUNK

    def by_chunk(x):
        return jnp.concatenate([jnp.where(chunk_of_row == c, x, 0.0).astype(BF16) for c in range(nch)], axis=1)

    blast = chunk_row(last_row)
    u_all = _dot(vt, by_chunk(k * jnp.exp(blast - bcum)))
    before = [None] * nch
    for c in order:
        before[c] = st
        decay = jnp.exp(bcum[c * GLA_CHUNK + last_row:c * GLA_CHUNK + last_row + 1, :])
        st = st * decay + jnp.where(bd_mask, u_all[:, c * HEAD_PAIR_DK:(c + 1) * HEAD_PAIR_DK], 0.0)
    if not want_out:
        return None, st
    bmid = chunk_row(mid_row)
    qt = q * jnp.exp(bcum - bmid)
    kt = (k * jnp.exp(bmid - bcum)).astype(BF16)
    lane = lax.broadcasted_iota(I32, qt.shape, 1)
    half = HEAD_PAIR_DK // 2
    o_heads = []
    for hh in range(2):
        sel = (lane < half) if hh == 0 else (lane >= half)
        a = _dot_nt(jnp.where(sel, qt, 0.0).astype(BF16), kt)
        a = jnp.where(amask, a, 0.0).astype(BF16)
        o_heads.append(_dot(a, v[:, hh * LANES:(hh + 1) * LANES]))
    qh = (q * jnp.exp(bcum)).astype(BF16)
    o_inter = jnp.concatenate(
        [_dot_nt(qh[c * GLA_CHUNK:(c + 1) * GLA_CHUNK], before[c].astype(BF16)) for c in range(nch)], axis=0)
    return jnp.concatenate(o_heads, axis=1) + o_inter, st


def _gla_kernel(q_ref, k_ref, v_ref, vt_ref, gkf_ref, gkb_ref, g_ref, kc_ref, vtc_ref, gkfc_ref, gkbc_ref,
                nw_ref, cmf_ref, cmb_ref, o_ref, stf_ref, stb_ref, of_ref, ob_ref):
    t = q_ref.shape[1]
    tc = kc_ref.shape[1]
    nsc, nscc = t // SUPER, tc // SUPER
    cmf = cmf_ref[...]
    cmb = cmb_ref[...]
    amask_f = cmf > 0
    amask_b = cmb > 0
    row = lax.broadcasted_iota(I32, (2 * LANES, HEAD_PAIR_DK), 0)
    lane = lax.broadcasted_iota(I32, (2 * LANES, HEAD_PAIR_DK), 1)
    bd_mask = (row < LANES) == (lane < HEAD_PAIR_DK // 2)

    def group_rows(j):
        start = j * SUPER
        return pl.ds(start if isinstance(start, int) else pl.multiple_of(start, SUPER), SUPER)

    def ctx_state(bcum, fwd, j, st):
        return _gla_super(None, kc_ref[0, group_rows(j), :].astype(F32), None, vtc_ref[0, j], bcum,
                          None, bd_mask, st, fwd, False)[1]

    def latent(bcum, amask, fwd, j, st):
        rows = group_rows(j)
        return _gla_super(q_ref[0, rows, :].astype(F32), k_ref[0, rows, :].astype(F32), v_ref[0, rows, :],
                          vt_ref[0, j], bcum, amask, bd_mask, st, fwd, True)

    stf = jnp.zeros(stf_ref.shape, F32)
    stb = jnp.zeros(stb_ref.shape, F32)
    for j in range(nscc):
        jb = nscc - 1 - j
        bcum_f, bcum_b = _decay_sums(gkfc_ref[0, group_rows(j), :], gkbc_ref[0, group_rows(jb), :], cmf)
        stf = ctx_state(bcum_f, True, j, stf)
        stb = ctx_state(bcum_b, False, jb, stb)
    stf_ref[...] = stf
    stb_ref[...] = stb

    nw = nw_ref[...]

    def finish(j, o):
        rows = group_rows(j)
        g = g_ref[0, rows, :].astype(F32)
        half_g = 0.5 * g
        gate = half_g * (1.0 + jnp.tanh(half_g))
        for hh in range(2):
            oh = o[:, hh * LANES:(hh + 1) * LANES]
            on = oh * lax.rsqrt(jnp.mean(oh * oh, axis=-1, keepdims=True) + EPS) * nw
            o_ref[0, rows, hh * LANES:(hh + 1) * LANES] = (on * gate[:, hh * LANES:(hh + 1) * LANES]).astype(BF16)

    for jj in range(nsc):
        jb = nsc - 1 - jj
        bcum_f, bcum_b = _decay_sums(gkf_ref[0, group_rows(jj), :], gkb_ref[0, group_rows(jb), :], cmf)
        of, stf = latent(bcum_f, amask_f, True, jj, stf_ref[...])
        stf_ref[...] = stf
        ob, stb = latent(bcum_b, amask_b, False, jb, stb_ref[...])
        stb_ref[...] = stb
        if jj == jb:
            finish(jj, of + ob)
        elif jj < jb:
            of_ref[group_rows(jj), :] = of
            ob_ref[group_rows(jb), :] = ob
        else:
            finish(jj, of + ob_ref[group_rows(jj), :])
            finish(jb, of_ref[group_rows(jb), :] + ob)


def _gla_masks():
    i = np.arange(SUPER)
    same = (i[:, None] // GLA_CHUNK) == (i[None, :] // GLA_CHUNK)
    fwd = same & (i[None, :] <= i[:, None])
    bwd = same & (i[None, :] >= i[:, None])
    return jnp.asarray(fwd, BF16), jnp.asarray(bwd, BF16)


def _gla_call(q, k, v, vt, gk, g, kc, vtc, gkc, nw):
    b, t, qk = q.shape
    tc = kc.shape[1]
    npair = qk // HEAD_PAIR_DK
    cmf, cmb = _gla_masks()
    lat = lambda width, off: pl.BlockSpec((1, t, width), lambda i, j: (i, 0, j + off))
    ctx = lambda width, off: pl.BlockSpec((1, tc, width), lambda i, j: (i, 0, j + off))
    tr = lambda n_groups: pl.BlockSpec((1, n_groups, 2 * LANES, SUPER), lambda i, j: (i, 0, j, 0))
    const = lambda shape: pl.BlockSpec(shape, lambda i, j: (0,) * len(shape))
    return pl.pallas_call(
        _gla_kernel,
        grid=(b, npair),
        in_specs=[lat(HEAD_PAIR_DK, 0), lat(HEAD_PAIR_DK, 0), lat(2 * LANES, 0), tr(t // SUPER),
                  lat(HEAD_PAIR_DK, 0), lat(HEAD_PAIR_DK, npair), lat(2 * LANES, 0),
                  ctx(HEAD_PAIR_DK, 0), tr(tc // SUPER), ctx(HEAD_PAIR_DK, 0), ctx(HEAD_PAIR_DK, npair),
                  const((1, LANES)), const(cmf.shape), const(cmb.shape)],
        out_specs=lat(2 * LANES, 0),
        out_shape=jax.ShapeDtypeStruct((b, t, v.shape[2]), BF16),
        scratch_shapes=[pltpu.VMEM((2 * LANES, HEAD_PAIR_DK), F32), pltpu.VMEM((2 * LANES, HEAD_PAIR_DK), F32),
                        pltpu.VMEM((t, 2 * LANES), F32), pltpu.VMEM((t, 2 * LANES), F32)],
        compiler_params=pltpu.CompilerParams(dimension_semantics=("arbitrary", "arbitrary")),
        name="gla",
    )(q, k, v, vt, gk, gk, g, kc, vtc, gkc, gkc, nw, cmf, cmb)


def _pool_kernel(x_ref, cm_ref, cnt_ref, wp_ref, ps_ref, o_ref, a_ref, b_ref):
    t = x_ref.shape[1]
    pad = POOL_PAD_GRID_ROWS * GRID_W
    total = t + 2 * pad
    for gi, w in enumerate(POOL_WINDOWS):
        lo = w // 2
        cols = slice(gi * LANES, (gi + 1) * LANES)
        cmat = cm_ref[gi]
        a_ref[0:pad, :] = jnp.zeros((pad, LANES), F32)
        a_ref[pad + t:total, :] = jnp.zeros((pad, LANES), F32)
        for blk in range(t // SUPER):
            rs = slice(blk * SUPER, (blk + 1) * SUPER)
            a_ref[pad + blk * SUPER:pad + (blk + 1) * SUPER, :] = _dot(cmat, x_ref[0, rs, cols])
        src, dst = a_ref, b_ref
        m = 1
        while m < w:
            sh = m * GRID_W
            dst[0:total - sh, :] = src[0:total - sh, :] + src[sh:total, :]
            src, dst = dst, src
            m *= 2
        first = pad - lo * GRID_W
        pooled = src[first:first + t, :] / cnt_ref[gi] - x_ref[0, :, cols].astype(F32)
        yp = _dot(pooled.astype(BF16), wp_ref[gi]) * ps_ref[:, cols]
        o_ref[0, :, cols] = yp.astype(BF16)


def _pool_col_mats():
    i = np.arange(SUPER)
    same_row = (i[:, None] // GRID_W) == (i[None, :] // GRID_W)
    d = i[None, :] - i[:, None]
    mats = []
    for w in POOL_WINDOWS:
        lo = w // 2
        hi = w - 1 - lo
        mats.append(same_row & (d >= -lo) & (d <= hi))
    return jnp.asarray(np.stack(mats), BF16)


def _pool_counts(t):
    rows = t // GRID_W
    r = np.arange(t) // GRID_W
    c = np.arange(t) % GRID_W
    out = []
    for w in POOL_WINDOWS:
        lo = w // 2
        hi = w - 1 - lo
        cnt_r = np.minimum(r + hi + 1, rows) - np.maximum(r - lo, 0)
        cnt_c = np.minimum(c + hi + 1, GRID_W) - np.maximum(c - lo, 0)
        out.append(np.broadcast_to((cnt_r * cnt_c).astype(np.float32)[:, None], (t, LANES)))
    return jnp.asarray(np.stack(out))


def _pool_call(xp, w_pool, pool_scale):
    b, t, pw = xp.shape
    ng = len(POOL_WINDOWS)
    assert max(POOL_WINDOWS) // 2 <= POOL_PAD_GRID_ROWS and t % GRID_W == 0
    cm = _pool_col_mats()
    cnt = _pool_counts(t)
    staged = t + 2 * POOL_PAD_GRID_ROWS * GRID_W
    const = lambda shape: pl.BlockSpec(shape, lambda i: (0,) * len(shape))
    return pl.pallas_call(
        _pool_kernel,
        grid=(b,),
        in_specs=[pl.BlockSpec((1, t, pw), lambda i: (i, 0, 0)),
                  const(cm.shape), const(cnt.shape), const((ng, LANES, LANES)), const((1, pw))],
        out_specs=pl.BlockSpec((1, t, pw), lambda i: (i, 0, 0)),
        out_shape=jax.ShapeDtypeStruct((b, t, pw), BF16),
        scratch_shapes=[pltpu.VMEM((staged, LANES), F32), pltpu.VMEM((staged, LANES), F32)],
        compiler_params=pltpu.CompilerParams(dimension_semantics=("arbitrary",)),
        name="pool",
    )(xp, cm, cnt, w_pool, pool_scale)


def _route_kernel(gla_ref, pool_ref, x_ref, mod_ref, wo_ref, nw_ref, wr_ref, br_ref, ut_ref,
                  h_ref, xt_ref, code_ref, wt_ref, cnt_ref, run_ref, wr2_ref, *, gw, n_exp):
    i = pl.program_id(0)

    @pl.when(i == 0)
    def _():
        run_ref[...] = jnp.zeros_like(run_ref)
        wh, wl = _split_bf16(wr_ref[...])
        wr2_ref[:LANES, :] = wh
        wr2_ref[LANES:, :] = wl

    m = mod_ref[0]
    acc = _dot(gla_ref[...], wo_ref[0:gw, :]) + _dot(pool_ref[...], wo_ref[gw:, :])
    h = x_ref[...] + m[2:3] * acc
    h_ref[...] = h
    xt = _rmsnorm(h, nw_ref[...]) * (1.0 + m[4:5]) + m[3:4]
    xt_ref[...] = _pack_bf16_pairs(xt)
    xh, xl = _split_bf16(xt)
    wr2 = wr2_ref[...]
    t1 = _dot_nt(wr2, xh)
    logits = t1[:LANES] + t1[LANES:] + _dot_nt(wr2[:LANES], xl) + br_ref[...]
    row = lax.broadcasted_iota(I32, logits.shape, 0)
    row_f = row.astype(F32)
    neg = jnp.float32(-jnp.inf)
    logits = jnp.where(row < n_exp, logits, neg)
    vals, hots, idxs = [], [], []
    for j in range(TOP_K):
        mx = jnp.max(logits, axis=0, keepdims=True)
        idx = jnp.min(jnp.where(logits == mx, row_f, float(LANES)), axis=0, keepdims=True)
        hot = row_f == idx
        vals.append(mx)
        hots.append(hot)
        idxs.append(idx)
        logits = jnp.where(hot, neg, logits)
    ex = [jnp.exp(v - vals[0]) for v in vals]
    den = ex[0] + ex[1] + ex[2] + ex[3]
    osum = jnp.where(hots[0] | hots[1] | hots[2] | hots[3], 1.0, 0.0)
    before = _dot(osum.astype(BF16), ut_ref[...]) + run_ref[:, 0:1]
    run = run_ref[:, 0:1] + jnp.sum(osum, axis=1, keepdims=True)
    run_ref[...] = jnp.broadcast_to(run, run_ref.shape)
    code_row = lax.broadcasted_iota(I32, code_ref.shape, 0)
    code = jnp.zeros(code_ref.shape, I32)
    w_t = jnp.zeros(logits.shape, F32)
    for j in range(TOP_K):
        rank = jnp.sum(jnp.where(hots[j], before, 0.0), axis=0, keepdims=True)
        code = jnp.where(code_row == j, idxs[j].astype(I32) * RANK_LIMIT + rank.astype(I32), code)
        w_t = jnp.where(row == j, ex[j] / den, w_t)
    code_ref[...] = code
    wt_ref[...] = w_t.T
    cnt_ref[...] = jnp.broadcast_to(run, cnt_ref.shape).astype(I32)


def _route_call(gla, pool, x, mod, w_out, nw, wr, br, *, tm, n_exp, first_tile, n_tiles):
    n_all, d = x.shape
    n = n_tiles * tm
    gw = gla.shape[1]
    t_per_b = n_all // mod.shape[0]
    ut = jnp.asarray(np.triu(np.ones((tm, tm), np.float32), 1), BF16)
    row_in = lambda width: pl.BlockSpec((tm, width), lambda i: (i + first_tile, 0))
    row = lambda width: pl.BlockSpec((tm, width), lambda i: (i, 0))
    const = lambda shape: pl.BlockSpec(shape, lambda i: (0,) * len(shape))
    sds = jax.ShapeDtypeStruct
    return pl.pallas_call(
        functools.partial(_route_kernel, gw=gw, n_exp=n_exp),
        grid=(n_tiles,),
        in_specs=[row_in(gw), row_in(pool.shape[1]), row_in(d),
                  pl.BlockSpec((1, N_MOD, d), lambda i: ((i + first_tile) * tm // t_per_b, 0, 0)),
                  const(w_out.shape), const((1, d)), const(wr.shape), const((LANES, 1)), const((tm, tm))],
        out_specs=[row(d), row(d // 2), pl.BlockSpec((8, tm), lambda i: (0, i)), row(LANES), const((LANES, LANES))],
        out_shape=[sds((n, d), F32), sds((n, d // 2), U32), sds((8, n), I32),
                   sds((n, LANES), F32), sds((LANES, LANES), I32)],
        scratch_shapes=[pltpu.VMEM((LANES, LANES), F32), pltpu.VMEM((2 * LANES, d), BF16)],
        compiler_params=pltpu.CompilerParams(dimension_semantics=("arbitrary",)),
        name="route",
    )(gla, pool, x, mod, w_out, nw, wr, br, ut)


def _plan_kernel(cnt_ref, code_ref, dest_ref, be_ref, nv_ref, first_ref, next_ref, slot_ref, nu_ref, start_ref,
                 *, n_exp, rows):
    @pl.when(pl.program_id(0) == 0)
    def _():
        blk = (lax.broadcasted_iota(I32, be_ref.shape, 0) * LANES + lax.broadcasted_iota(I32, be_ref.shape, 1))
        blk_row0 = blk * rows
        nxt_e = [None] * n_exp
        nxt = jnp.int32(-1)
        for e in reversed(range(n_exp)):
            nxt_e[e] = nxt
            nxt = jnp.where(cnt_ref[e] > 0, e, nxt)
        zeros = jnp.zeros(be_ref.shape, I32)
        be, end_valid, first, nxt_blk, slot = zeros, zeros, zeros, zeros - 1, zeros
        acc = jnp.int32(0)
        last_e = jnp.int32(0)
        ordinal = jnp.int32(0)
        for e in range(n_exp):
            c = cnt_ref[e]
            start_ref[e] = acc
            in_e = (blk_row0 >= acc) & (c > 0)
            end_valid = jnp.where(in_e, acc + c, end_valid)
            be = jnp.where(in_e, e, be)
            first = jnp.where(in_e, (blk_row0 == acc).astype(I32), first)
            nxt_blk = jnp.where(in_e, nxt_e[e], nxt_blk)
            slot = jnp.where(in_e, ordinal % 2, slot)
            acc = acc + (c + rows - 1) // rows * rows
            last_e = jnp.where(c > 0, e, last_e)
            ordinal = ordinal + (c > 0).astype(I32)
        n_used = acc // rows
        nu_ref[0] = n_used
        be_ref[...] = jnp.where(blk < n_used, be, last_e)
        nv_ref[...] = jnp.clip(end_valid - blk_row0, 0, rows)
        first_ref[...] = first
        next_ref[...] = nxt_blk
        slot_ref[...] = slot

    code = code_ref[...]
    e_vec = code // RANK_LIMIT
    dest = code % RANK_LIMIT
    for e in range(n_exp):
        dest = dest + jnp.where(e_vec == e, start_ref[e], 0)
    dest_ref[...] = dest


def _plan_call(counts, code_t, *, n_exp, rows, chunk):
    n = code_t.shape[1]
    sds = jax.ShapeDtypeStruct
    smem = pltpu.SMEM
    return pl.pallas_call(
        functools.partial(_plan_kernel, n_exp=n_exp, rows=rows),
        grid=(n // chunk,),
        in_specs=[pl.BlockSpec(memory_space=smem), pl.BlockSpec((8, chunk), lambda i: (0, i))],
        out_specs=[pl.BlockSpec((8, chunk), lambda i: (0, i))]
        + [pl.BlockSpec((8, LANES), lambda i: (0, 0))] * 5 + [pl.BlockSpec(memory_space=smem)],
        out_shape=[sds((8, n), I32)] + [sds((8, LANES), I32)] * 5 + [sds((1,), I32)],
        scratch_shapes=[pltpu.SMEM((n_exp,), I32)],
        compiler_params=pltpu.CompilerParams(dimension_semantics=("arbitrary",)),
        name="plan",
    )(counts, code_t)


def _sc_worker_id():
    return lax.axis_index("s") * SC_CORES + lax.axis_index("c")


def _sc_scatter_call(x, idx3, *, n_out):
    n, d = x.shape
    n_win_total, k, w = idx3.shape
    n_win = n_win_total // SC_WORKERS
    mesh = plsc.VectorSubcoreMesh(core_axis_name="c", subcore_axis_name="s")

    @functools.partial(
        pl.kernel, mesh=mesh,
        out_type=jax.ShapeDtypeStruct((n_out, d), x.dtype),
        scratch_types=[pltpu.VMEM((k, w), I32), pltpu.VMEM((w, d), x.dtype), pltpu.SemaphoreType.DMA],
        name="sc_dispatch",
    )
    def kern(x_hbm, idx_hbm, out_hbm, idx_v, rows_v, sem):
        wid = _sc_worker_id()

        @pl.loop(0, n_win)
        def _(i):
            win = wid * n_win + i
            pltpu.sync_copy(idx_hbm.at[win], idx_v)
            pltpu.sync_copy(x_hbm.at[pl.ds(win * w, w)], rows_v)
            for j in range(k):
                pltpu.async_copy(rows_v, out_hbm.at[idx_v.at[j]], sem).wait()

    return kern(x, idx3)


def _sc_gather_call(table, idx3):
    n_workers, n_win, w = idx3.shape
    d = table.shape[1]
    assert n_workers == SC_WORKERS and n_win % 2 == 0
    mesh = plsc.VectorSubcoreMesh(core_axis_name="c", subcore_axis_name="s")

    @functools.partial(
        pl.kernel, mesh=mesh,
        out_type=jax.ShapeDtypeStruct((n_workers * n_win * w, d), table.dtype),
        scratch_types=[pltpu.VMEM((n_win, w), I32), pltpu.VMEM((2, w, d), table.dtype),
                       pltpu.SemaphoreType.DMA((2,)), pltpu.SemaphoreType.DMA((2,))],
        name="sc_gather",
    )
    def kern(table_hbm, idx_hbm, out_hbm, idx_v, rows_v, gsem, osem):
        wid = _sc_worker_id()
        base = wid * n_win
        pltpu.sync_copy(idx_hbm.at[wid], idx_v)

        def gather(wi, b):
            return pltpu.make_async_copy(table_hbm.at[idx_v.at[wi]], rows_v.at[b], gsem.at[b])

        def put(wi, b):
            return pltpu.make_async_copy(rows_v.at[b], out_hbm.at[pl.ds((base + wi) * w, w)], osem.at[b])

        gather(0, 0).start()

        @pl.loop(0, n_win, step=2)
        def _(i):
            for b in range(2):
                wi = i + b

                @pl.when(wi + 1 < n_win)
                def _():
                    @pl.when(wi >= 1)
                    def _():
                        put(wi - 1, 1 - b).wait()
                    gather(wi + 1, 1 - b).start()

                gather(wi, b).wait()
                put(wi, b).start()

        put(n_win - 2, 0).wait()
        put(n_win - 1, 1).wait()

    return kern(table, idx3)


def _expert_kernel(be_ref, nu_ref, nv_ref, first_ref, next_ref, slot_ref,
                   x_ref, wgu_hbm, bgu_ref, wd_hbm, bd_ref, y_ref,
                   wgu_f32_ref, wd_f32_ref, sem, *, d_ff):
    i = pl.program_id(0)
    slot = slot_ref[i]

    def fetch(e, s):
        return (pltpu.make_async_copy(wgu_hbm.at[e], wgu_f32_ref.at[s], sem.at[0, s]),
                pltpu.make_async_copy(wd_hbm.at[e], wd_f32_ref.at[s], sem.at[1, s]))

    @pl.when(i == 0)
    def _():
        for cp in fetch(be_ref[0], slot):
            cp.start()

    @pl.when(first_ref[i] == 1)
    def _():
        for cp in fetch(be_ref[i], slot):
            cp.wait()

        @pl.when(next_ref[i] >= 0)
        def _():
            for cp in fetch(next_ref[i], 1 - slot):
                cp.start()

    n_valid = nv_ref[i]
    n_rows = x_ref.shape[0]

    def expert_rows(rows):
        row = lax.broadcasted_iota(I32, (rows, x_ref.shape[1]), 0)
        lo, hi = _unpack_bf16_pairs(jnp.where(row < n_valid, x_ref[0:rows, :], jnp.uint32(0)))
        xb = jnp.concatenate([lo, hi], axis=1).astype(BF16)
        col = lax.broadcasted_iota(I32, bgu_ref.shape[1:], 1)
        gu = _dot(xb, wgu_f32_ref[slot].astype(BF16)) + (bgu_ref[0] + jnp.where(col >= d_ff, 1.0, 0.0))
        gate = jnp.minimum(gu[:, :d_ff], SWIGLU_LIMIT)
        up1 = jnp.clip(gu[:, d_ff:], 1.0 - SWIGLU_LIMIT, 1.0 + SWIGLU_LIMIT)
        act = up1 * (0.5 * gate) * (1.0 + jnp.tanh((0.5 * SWIGLU_ALPHA) * gate))
        y_ref[0:rows, :] = _pack_bf16_pairs(_dot(act.astype(BF16), wd_f32_ref[slot].astype(BF16)) + bd_ref[0])

    in_use = i < nu_ref[0]

    for rows in range(EXPERT_ROW_STEP, n_rows + 1, EXPERT_ROW_STEP):
        @pl.when(in_use & (n_valid > rows - EXPERT_ROW_STEP) & (n_valid <= rows))
        def _(rows=rows):
            expert_rows(rows)
            if rows < n_rows:
                y_ref[rows:, :] = jnp.zeros((n_rows - rows, y_ref.shape[1]), y_ref.dtype)


def _expert_call(plan, xs, w_gu, b_gu, w_down, b_down):
    block_e, n_valid, first, nxt, slot, n_used = plan
    n_pad = xs.shape[0]
    n_exp, d, two_ff = w_gu.shape
    d_ff = two_ff // 2
    nblk = n_pad // EXPERT_ROWS
    rows = lambda i, be, nu, *_: (jnp.minimum(i, nu[0] - 1), 0)
    per_e = lambda i, be, *_: (be[i], 0, 0)
    grid_spec = pltpu.PrefetchScalarGridSpec(
        num_scalar_prefetch=6,
        grid=(nblk,),
        in_specs=[pl.BlockSpec((EXPERT_ROWS, d // 2), rows),
                  pl.BlockSpec(memory_space=pl.ANY), pl.BlockSpec((1, 1, two_ff), per_e),
                  pl.BlockSpec(memory_space=pl.ANY), pl.BlockSpec((1, 1, d), per_e)],
        out_specs=pl.BlockSpec((EXPERT_ROWS, d // 2), rows),
        scratch_shapes=[pltpu.VMEM((2, d, two_ff), F32), pltpu.VMEM((2, d_ff, d), F32),
                        pltpu.SemaphoreType.DMA((2, 2))],
    )
    flat = lambda a: a.reshape(-1)
    return pl.pallas_call(
        functools.partial(_expert_kernel, d_ff=d_ff),
        grid_spec=grid_spec,
        out_shape=jax.ShapeDtypeStruct((n_pad, d // 2), U32),
        compiler_params=pltpu.CompilerParams(dimension_semantics=("arbitrary",),
                                             vmem_limit_bytes=EXPERT_VMEM_BYTES),
        name="experts",
    )(flat(block_e), n_used, flat(n_valid), flat(first), flat(nxt), flat(slot),
      xs, w_gu, b_gu.reshape(n_exp, 1, two_ff), w_down, b_down.reshape(n_exp, 1, d))


def _combine_kernel(y4_ref, wt_ref, h_ref, mod_ref, fw_ref, o_ref):
    wt = wt_ref[...]
    acc_lo, acc_hi = None, None
    for j in range(TOP_K):
        lo, hi = _unpack_bf16_pairs(y4_ref[j])
        w = wt[:, j:j + 1]
        acc_lo = w * lo if j == 0 else acc_lo + w * lo
        acc_hi = w * hi if j == 0 else acc_hi + w * hi
    acc = jnp.concatenate([acc_lo, acc_hi], axis=1)
    m = mod_ref[0]
    o_ref[...] = _rmsnorm(h_ref[...] + m[5:6] * acc, fw_ref[...])


def _combine_call(y4, wts, h, mod, fw, prev_out, *, n, tg, first_tile):
    d = h.shape[1]
    t_per_b = n // mod.shape[0]
    part = lambda width: pl.BlockSpec((tg, width), lambda i: (i, 0))
    row = lambda width: pl.BlockSpec((tg, width), lambda i: (i + first_tile, 0))
    in_specs = [pl.BlockSpec((TOP_K, tg, d // 2), lambda i: (0, i, 0)),
                part(LANES), part(d),
                pl.BlockSpec((1, N_MOD, d), lambda i: ((i + first_tile) * tg // t_per_b, 0, 0)),
                pl.BlockSpec((1, d), lambda i: (0, 0))]
    args = [y4, wts, h, mod, fw]
    kern = _combine_kernel
    aliases = {}
    if prev_out is not None:
        in_specs.append(pl.BlockSpec(memory_space=pl.ANY))
        args.append(prev_out)
        kern = lambda y4_ref, wt_ref, h_ref, mod_ref, fw_ref, prev_ref, o_ref: _combine_kernel(
            y4_ref, wt_ref, h_ref, mod_ref, fw_ref, o_ref)
        aliases = {len(args) - 1: 0}
    return pl.pallas_call(
        kern,
        grid=(y4.shape[1] // tg,),
        in_specs=in_specs,
        out_specs=row(d),
        out_shape=jax.ShapeDtypeStruct((n, d), F32),
        input_output_aliases=aliases,
        compiler_params=pltpu.CompilerParams(dimension_semantics=("arbitrary",)),
        name="combine",
    )(*args)


def kernel(x, c, ctx, c_ctx, w_ada, b_ada, norm_mix_w, norm_mlp_w, w_in, w_gk_f, b_gk_f, w_gk_b, b_gk_b,
           gla_norm_w, w_pool, pool_scale, w_out, w_router, b_router, w_gu, b_gu, w_down, b_down,
           final_norm_w):
    b, t, d = x.shape
    assert w_ada.shape[0] == 1, "single-layer trunk"
    n_exp = w_router.shape[2]
    rank = w_gk_f.shape[1]
    qk = w_gk_f.shape[2]
    dk = qk // GLA_HEADS
    gw = GLA_HEADS * gla_norm_w.shape[1]
    pw = w_pool.shape[1] * w_pool.shape[2]
    assert w_in.shape[2] == 2 * qk + 2 * gw + 2 * rank + pw and 2 * rank <= LANES
    assert t % SUPER == 0 and ctx.shape[1] % SUPER == 0 and n_exp <= LANES

    rows = -(-(b + 1) // 8) * 8
    cc = jnp.concatenate([c, c_ctx[None, :], jnp.zeros((rows - b - 1, d), F32)], axis=0)
    mod = _mod_call(cc, w_ada[0], b_ada)
    mod_x = mod[:b].reshape(b, N_MOD, d)
    mod_c = mod[b:b + 1].reshape(1, N_MOD, d)

    wi = w_in[0]
    o_r = 2 * qk + 2 * gw
    w_cat = jnp.concatenate([wi[:, :o_r], wi[:, o_r + 2 * rank:], wi[:, o_r:o_r + 2 * rank],
                             jnp.zeros((d, LANES - 2 * rank), F32)], axis=1).astype(BF16)
    wgk = jnp.concatenate([jnp.pad(w_gk_f[0], ((0, 0), (0, qk))), jnp.pad(w_gk_b[0], ((0, 0), (qk, 0))),
                           jnp.zeros((LANES - 2 * rank, 2 * qk), F32)], axis=0)
    bgk = jnp.concatenate([b_gk_f[0], b_gk_b[0]])[None, :]
    proj = functools.partial(_inproj_call, nw=norm_mix_w, w=w_cat, wgk=wgk.astype(BF16), bgk=bgk,
                             qk=qk, gw=gw, pw=pw, dk=dk)
    q, k, v, vt, g, xp, gk = proj(x, mod_x, tm=1024)
    _, kc, _, vtc, _, _, gkc = proj(ctx, mod_c, tm=SUPER)

    gla = _gla_call(q, k, v, vt, gk, g, kc, vtc, gkc, gla_norm_w)
    pool = _pool_call(xp, w_pool[0].astype(BF16), pool_scale)

    n = b * t
    wr = jnp.pad(w_router[0].T, ((0, LANES - n_exp), (0, 0)))
    br = jnp.pad(b_router.T, ((0, LANES - n_exp), (0, 0)))
    n_part = n // MOE_PARTS
    tm, tg = 512, 1024
    assert n % MOE_PARTS == 0 and n_part % (SC_WORKERS * SC_WINDOW) == 0 and n_part % tm == 0 and n_part < RANK_LIMIT
    assert (n_part * TOP_K) % (SC_WORKERS * 2 * SC_GATHER_WINDOW) == 0
    n_pad = n_part * TOP_K + n_exp * EXPERT_ROWS
    nblk = n_pad // EXPERT_ROWS
    assert nblk <= 8 * LANES
    parts = range(MOE_PARTS)
    routed = [_route_call(gla.reshape(n, gw), pool.reshape(n, pw), x.reshape(n, d), mod_x, w_out[0].astype(BF16),
                          norm_mlp_w, wr, br, tm=tm, n_exp=n_exp,
                          first_tile=p * n_part // tm, n_tiles=n_part // tm) for p in parts]
    plans = [_plan_call(cnt[:n_exp, 0], code_t, n_exp=n_exp, rows=EXPERT_ROWS, chunk=min(n_part, 4096))
             for (_, _, code_t, _, cnt) in routed]
    dests = [plan[0][:TOP_K] for plan in plans]
    xs = [_sc_scatter_call(routed[p][1],
                           dests[p].reshape(TOP_K, n_part // SC_WINDOW, SC_WINDOW).transpose(1, 0, 2), n_out=n_pad)
          for p in parts]
    ys = [_expert_call(plans[p][1:], xs[p], w_gu[0], b_gu[0], w_down[0], b_down[0]) for p in parts]
    y4 = [_sc_gather_call(ys[p], dests[p].reshape(SC_WORKERS, -1, SC_GATHER_WINDOW)).reshape(TOP_K, n_part, d // 2)
          for p in parts]
    out = None
    for p in parts:
        out = _combine_call(y4[p], routed[p][3], routed[p][0], mod_x, final_norm_w[None, :], out,
                            n=n, tg=tg, first_tile=p * n_part // tg)
    return out.reshape(b, t, d)
```
